```python
import math, functools
import jax, jax.numpy as jnp
from jax import lax
import numpy as np

D_MODEL = 1024
BATCH = 8
SEQ = 8192
DEPTH = 4

N_MIXERS = 3
MEM_LEN = 256
BLOCK = 128
ROPE_THETA = 10000.0
NEG = -1e30
LN_EPS = 1e-5
RMS_EPS = 1e-6

A_HEADS = 16
A_KV_HEADS = 4
A_HEAD_DIM = 64
A_WINDOW = 128

LRU_WIDTH = D_MODEL
LRU_BLOCKS = 4
LRU_BLOCK_W = LRU_WIDTH // LRU_BLOCKS
LRU_CONV = 4
LRU_C = 8.0

C_HEADS = 8
C_NOPE = 128
C_ROPE = 64
C_V = 128
C_Q_RANK = 384
C_KV_RANK = 256

X_HEADS = 4
X_HEAD_DIM = D_MODEL // X_HEADS

D_FF = 2816
FFN_CONV = 3

ALPHA = (2.0 * DEPTH) ** 0.25
BETA = (8.0 * DEPTH) ** -0.25

N_A = (DEPTH + 2) // 3
N_B = (DEPTH + 1) // 3
N_C = DEPTH // 3

kernel_name = "interleaved_swa_rglru_mla_deepnorm_trunk"


def layer_norm(x, g, b):
    xf = x.astype(jnp.float32)
    mu = jnp.mean(xf, axis=-1, keepdims=True)
    var = jnp.mean(jnp.square(xf - mu), axis=-1, keepdims=True)
    y = (xf - mu) * lax.rsqrt(var + LN_EPS) * g.astype(jnp.float32) + b.astype(jnp.float32)
    return y.astype(x.dtype)


def rms_norm(x, g):
    xf = x.astype(jnp.float32)
    y = xf * lax.rsqrt(jnp.mean(jnp.square(xf), axis=-1, keepdims=True) + RMS_EPS)
    return (y * g.astype(jnp.float32)).astype(x.dtype)


def rope_tables(seq, dim):
    inv = 1.0 / (ROPE_THETA ** (jnp.arange(0, dim, 2, dtype=jnp.float32) / dim))
    ang = jnp.arange(seq, dtype=jnp.float32)[:, None] * inv[None, :]
    return jnp.cos(ang), jnp.sin(ang)


def apply_rope(x, cos, sin):
    c = cos[None, :, None, :].astype(x.dtype)
    s = sin[None, :, None, :].astype(x.dtype)
    x1, x2 = jnp.split(x, 2, axis=-1)
    return jnp.concatenate([x1 * c - x2 * s, x2 * c + x1 * s], axis=-1)


def causal_depthwise_conv(x, w, b):
    k_width = w.shape[0]
    s = x.shape[1]
    xp = jnp.pad(x, ((0, 0), (k_width - 1, 0), (0, 0)))
    y = xp[:, 0:s] * w[0]
    for k in range(1, k_width):
        y = y + xp[:, k:k + s] * w[k]
    return y + b


def swa_sink_attention(x, w_qkv, sinks, w_o, cos, sin):
    bsz, s, _ = x.shape
    grp = A_HEADS // A_KV_HEADS
    nb = s // BLOCK
    qkv = x @ w_qkv
    q, k, v = jnp.split(qkv, [A_HEADS * A_HEAD_DIM, (A_HEADS + A_KV_HEADS) * A_HEAD_DIM], axis=-1)
    q = apply_rope(q.reshape(bsz, s, A_HEADS, A_HEAD_DIM), cos, sin)
    k = apply_rope(k.reshape(bsz, s, A_KV_HEADS, A_HEAD_DIM), cos, sin)
    v = v.reshape(bsz, s, A_KV_HEADS, A_HEAD_DIM)
    qb = q.reshape(bsz, nb, BLOCK, A_KV_HEADS, grp, A_HEAD_DIM)

    def with_prev(t):
        tb = t.reshape(bsz, nb, BLOCK, A_KV_HEADS, A_HEAD_DIM)
        prev = jnp.pad(tb[:, :-1], ((0, 0), (1, 0), (0, 0), (0, 0), (0, 0)))
        return jnp.concatenate([prev, tb], axis=2)

    kb, vb = with_prev(k), with_prev(v)
    scores = jnp.einsum('bnqhgd,bnkhd->bnhgqk', qb, kb).astype(jnp.float32) * (A_HEAD_DIM ** -0.5)
    qi = jnp.arange(BLOCK)[:, None]
    kj = jnp.arange(2 * BLOCK)[None, :]
    dist = qi + BLOCK - kj
    band = (dist >= 0) & (dist < A_WINDOW)
    real_key = (jnp.arange(nb)[:, None, None] > 0) | (kj >= BLOCK)[None]
    valid = band[None] & real_key
    scores = jnp.where(valid[None, :, None, None], scores, NEG)
    sink = sinks.astype(jnp.float32).reshape(A_KV_HEADS, grp)[None, None, :, :, None, None]
    sink = jnp.broadcast_to(sink, scores.shape[:-1] + (1,))
    probs = jax.nn.softmax(jnp.concatenate([scores, sink], axis=-1), axis=-1)[..., :-1]
    out = jnp.einsum('bnhgqk,bnkhd->bnqhgd', probs.astype(v.dtype), vb)
    return out.reshape(bsz, s, A_HEADS * A_HEAD_DIM) @ w_o


def rglru_block(x, w_in, conv_w, conv_b, w_rgate, b_rgate, w_igate, b_igate, lam, w_o):
    bsz, s, _ = x.shape
    gate, u = jnp.split(x @ w_in, 2, axis=-1)
    u = causal_depthwise_conv(u, conv_w, conv_b)
    ub = u.reshape(bsz, s, LRU_BLOCKS, LRU_BLOCK_W)
    r = jax.nn.sigmoid(jnp.einsum('bshi,hij->bshj', ub, w_rgate).reshape(bsz, s, LRU_WIDTH) + b_rgate)
    i = jax.nn.sigmoid(jnp.einsum('bshi,hij->bshj', ub, w_igate).reshape(bsz, s, LRU_WIDTH) + b_igate)
    log_a = -LRU_C * r.astype(jnp.float32) * jax.nn.softplus(-lam.astype(jnp.float32))
    a = jnp.exp(log_a)
    b_in = jnp.sqrt(-jnp.expm1(2.0 * log_a)) * (i * u).astype(jnp.float32)

    def combine(c1, c2):
        a1, b1 = c1
        a2, b2 = c2
        return a1 * a2, a2 * b1 + b2

    _, h = lax.associative_scan(combine, (a, b_in), axis=1)
    y = h.astype(x.dtype) * jax.nn.gelu(gate)
    return y @ w_o


def mla_attention(x, w_down, q_norm, kv_norm, w_uq, w_ukv, w_o, cos_r, sin_r):
    bsz, s, _ = x.shape
    nb = s // BLOCK
    c = x @ w_down
    cq, ckv, k_rope = jnp.split(c, [C_Q_RANK, C_Q_RANK + C_KV_RANK], axis=-1)
    cq = rms_norm(cq, q_norm)
    ckv = rms_norm(ckv, kv_norm)
    q = (cq @ w_uq).reshape(bsz, s, C_HEADS, C_NOPE + C_ROPE)
    q_nope, q_rope = jnp.split(q, [C_NOPE], axis=-1)
    q_rope = apply_rope(q_rope, cos_r, sin_r)
    k_rope = apply_rope(k_rope[:, :, None, :], cos_r, sin_r)[:, :, 0]
    kv = (ckv @ w_ukv).reshape(bsz, s, C_HEADS, C_NOPE + C_V)
    k_nope, v = jnp.split(kv, [C_NOPE], axis=-1)
    scale = (C_NOPE + C_ROPE) ** -0.5
    qn = q_nope.reshape(bsz, nb, BLOCK, C_HEADS, C_NOPE).transpose(1, 0, 2, 3, 4)
    qr = q_rope.reshape(bsz, nb, BLOCK, C_HEADS, C_ROPE).transpose(1, 0, 2, 3, 4)
    key_pos = jnp.arange(s)

    def attend(args):
        n, qn_b, qr_b = args
        sc = (jnp.einsum('bqhd,bkhd->bhqk', qn_b, k_nope)
              + jnp.einsum('bqhd,bkd->bhqk', qr_b, k_rope)).astype(jnp.float32) * scale
        q_pos = n * BLOCK + jnp.arange(BLOCK)
        sc = jnp.where(key_pos[None, :] <= q_pos[:, None], sc, NEG)
        p = jax.nn.softmax(sc, axis=-1)
        return jnp.einsum('bhqk,bkhd->bqhd', p.astype(v.dtype), v)

    out = lax.map(attend, (jnp.arange(nb), qn, qr))
    out = out.transpose(1, 0, 2, 3, 4).reshape(bsz, s, C_HEADS * C_V)
    return out @ w_o


def memory_cross_attention(x, mem_k, mem_v, w_q, w_o):
    bsz, s, _ = x.shape
    q = (x @ w_q).reshape(bsz, s, X_HEADS, X_HEAD_DIM)
    sc = jnp.einsum('bshd,bmhd->bhsm', q, mem_k).astype(jnp.float32) * (X_HEAD_DIM ** -0.5)
    p = jax.nn.softmax(sc, axis=-1)
    o = jnp.einsum('bhsm,bmhd->bshd', p.astype(mem_v.dtype), mem_v).reshape(bsz, s, D_MODEL)
    return o @ w_o


def conv_glu_ffn(x, w_up, conv_w, conv_b, w_down):
    h = causal_depthwise_conv(x @ w_up, conv_w, conv_b)
    g, u = jnp.split(h, 2, axis=-1)
    return (jax.nn.silu(g) * u) @ w_down


def _fwd_setup_inputs(seed: int = 0) -> dict:
    key = jax.random.key(seed)
    ks = iter(jax.random.split(key, 40))

    def dense(shape, fan_in, scale=1.0):
        return jax.random.normal(next(ks), shape, jnp.float32) * (scale * fan_in ** -0.5)

    def small(shape, scale=0.01):
        return jax.random.normal(next(ks), shape, jnp.float32) * scale

    def gain(shape):
        return 1.0 + small(shape)

    qkv_w = (A_HEADS + 2 * A_KV_HEADS) * A_HEAD_DIM
    a0 = jax.random.uniform(next(ks), (N_B, LRU_WIDTH), jnp.float32, 0.9, 0.999) ** (1.0 / LRU_C)
    return {
        "x": jax.random.normal(next(ks), (BATCH, SEQ, D_MODEL), jnp.float32),
        "mem": jax.random.normal(next(ks), (BATCH, MEM_LEN, D_MODEL), jnp.float32),
        "a_w_qkv": dense((N_A, D_MODEL, qkv_w), D_MODEL),
        "a_sinks": small((N_A, A_HEADS), 1.0),
        "a_w_o": dense((N_A, A_HEADS * A_HEAD_DIM, D_MODEL), A_HEADS * A_HEAD_DIM, BETA),
        "b_w_in": dense((N_B, D_MODEL, 2 * LRU_WIDTH), D_MODEL),
        "b_conv_w": dense((N_B, LRU_CONV, LRU_WIDTH), LRU_CONV),
        "b_conv_b": small((N_B, LRU_WIDTH)),
        "b_w_rgate": dense((N_B, LRU_BLOCKS, LRU_BLOCK_W, LRU_BLOCK_W), LRU_BLOCK_W),
        "b_b_rgate": small((N_B, LRU_WIDTH)),
        "b_w_igate": dense((N_B, LRU_BLOCKS, LRU_BLOCK_W, LRU_BLOCK_W), LRU_BLOCK_W),
        "b_b_igate": small((N_B, LRU_WIDTH)),
        "b_lambda": jnp.log(a0) - jnp.log1p(-a0),
        "b_w_o": dense((N_B, LRU_WIDTH, D_MODEL), LRU_WIDTH, BETA),
        "c_w_down": dense((N_C, D_MODEL, C_Q_RANK + C_KV_RANK + C_ROPE), D_MODEL),
        "c_q_norm": gain((N_C, C_Q_RANK)),
        "c_kv_norm": gain((N_C, C_KV_RANK)),
        "c_w_uq": dense((N_C, C_Q_RANK, C_HEADS * (C_NOPE + C_ROPE)), C_Q_RANK),
        "c_w_ukv": dense((N_C, C_KV_RANK, C_HEADS * (C_NOPE + C_V)), C_KV_RANK),
        "c_w_o": dense((N_C, C_HEADS * C_V, D_MODEL), C_HEADS * C_V, BETA),
        "mem_w_kv": dense((D_MODEL, 2 * D_MODEL), D_MODEL),
        "x_w_q": dense((DEPTH, D_MODEL, D_MODEL), D_MODEL),
        "x_w_o": dense((DEPTH, D_MODEL, D_MODEL), D_MODEL, BETA),
        "f_w_up": dense((DEPTH, D_MODEL, 2 * D_FF), D_MODEL),
        "f_conv_w": dense((DEPTH, FFN_CONV, 2 * D_FF), FFN_CONV),
        "f_conv_b": small((DEPTH, 2 * D_FF)),
        "f_w_down": dense((DEPTH, D_FF, D_MODEL), D_FF, BETA),
        "ln_g": gain((DEPTH, 3, D_MODEL)),
        "ln_b": small((DEPTH, 3, D_MODEL)),
    }


def _fwd_reference(x, mem, a_w_qkv, a_sinks, a_w_o, b_w_in, b_conv_w, b_conv_b, b_w_rgate, b_b_rgate,
              b_w_igate, b_b_igate, b_lambda, b_w_o, c_w_down, c_q_norm, c_kv_norm, c_w_uq, c_w_ukv,
              c_w_o, mem_w_kv, x_w_q, x_w_o, f_w_up, f_conv_w, f_conv_b, f_w_down, ln_g, ln_b):
    bsz, s, _ = x.shape
    cos_a, sin_a = rope_tables(s, A_HEAD_DIM)
    cos_c, sin_c = rope_tables(s, C_ROPE)
    mem_k, mem_v = jnp.split(mem @ mem_w_kv, 2, axis=-1)
    mem_k = mem_k.reshape(bsz, MEM_LEN, X_HEADS, X_HEAD_DIM)
    mem_v = mem_v.reshape(bsz, MEM_LEN, X_HEADS, X_HEAD_DIM)
    for i in range(DEPTH):
        kind, j = i % N_MIXERS, i // N_MIXERS
        if kind == 0:
            y = swa_sink_attention(x, a_w_qkv[j], a_sinks[j], a_w_o[j], cos_a, sin_a)
        elif kind == 1:
            y = rglru_block(x, b_w_in[j], b_conv_w[j], b_conv_b[j], b_w_rgate[j], b_b_rgate[j],
                            b_w_igate[j], b_b_igate[j], b_lambda[j], b_w_o[j])
        else:
            y = mla_attention(x, c_w_down[j], c_q_norm[j], c_kv_norm[j], c_w_uq[j], c_w_ukv[j],
                              c_w_o[j], cos_c, sin_c)
        x = layer_norm(ALPHA * x + y, ln_g[i, 0], ln_b[i, 0])
        x = layer_norm(ALPHA * x + memory_cross_attention(x, mem_k, mem_v, x_w_q[i], x_w_o[i]),
                       ln_g[i, 1], ln_b[i, 1])
        x = layer_norm(ALPHA * x + conv_glu_ffn(x, f_w_up[i], f_conv_w[i], f_conv_b[i], f_w_down[i]),
                       ln_g[i, 2], ln_b[i, 2])
    return x


import jax as _jax
import jax.numpy as _jnp

TWIN_FORMAT = 'train_step'
FWD_PARAMS = ['x', 'mem', 'a_w_qkv', 'a_sinks', 'a_w_o', 'b_w_in', 'b_conv_w', 'b_conv_b', 'b_w_rgate', 'b_b_rgate', 'b_w_igate', 'b_b_igate', 'b_lambda', 'b_w_o', 'c_w_down', 'c_q_norm', 'c_kv_norm', 'c_w_uq', 'c_w_ukv', 'c_w_o', 'mem_w_kv', 'x_w_q', 'x_w_o', 'f_w_up', 'f_conv_w', 'f_conv_b', 'f_w_down', 'ln_g', 'ln_b']
TWIN_WEIGHTS = ['a_w_qkv', 'a_sinks', 'a_w_o', 'b_w_in', 'b_conv_w', 'b_conv_b', 'b_w_rgate', 'b_b_rgate', 'b_w_igate', 'b_b_igate', 'b_lambda', 'b_w_o', 'c_w_down', 'c_q_norm', 'c_kv_norm', 'c_w_uq', 'c_w_ukv', 'c_w_o', 'mem_w_kv', 'x_w_q', 'x_w_o', 'f_w_up', 'f_conv_w', 'f_conv_b', 'f_w_down', 'ln_g', 'ln_b']
TWIN_DIFF_INPUT = 'x'
TWIN_INPUTS = ['x', 'mem', 'a_w_qkv', 'a_sinks', 'a_w_o', 'b_w_in', 'b_conv_w', 'b_conv_b', 'b_w_rgate', 'b_b_rgate', 'b_w_igate', 'b_b_igate', 'b_lambda', 'b_w_o', 'c_w_down', 'c_q_norm', 'c_kv_norm', 'c_w_uq', 'c_w_ukv', 'c_w_o', 'mem_w_kv', 'x_w_q', 'x_w_o', 'f_w_up', 'f_conv_w', 'f_conv_b', 'f_w_down', 'ln_g', 'ln_b', 'loss_target', 'm_a_w_qkv', 'm_a_sinks', 'm_a_w_o', 'm_b_w_in', 'm_b_conv_w', 'm_b_conv_b', 'm_b_w_rgate', 'm_b_b_rgate', 'm_b_w_igate', 'm_b_b_igate', 'm_b_lambda', 'm_b_w_o', 'm_c_w_down', 'm_c_q_norm', 'm_c_kv_norm', 'm_c_w_uq', 'm_c_w_ukv', 'm_c_w_o', 'm_mem_w_kv', 'm_x_w_q', 'm_x_w_o', 'm_f_w_up', 'm_f_conv_w', 'm_f_conv_b', 'm_f_w_down', 'm_ln_g', 'm_ln_b', 'v_a_w_qkv', 'v_a_sinks', 'v_a_w_o', 'v_b_w_in', 'v_b_conv_w', 'v_b_conv_b', 'v_b_w_rgate', 'v_b_b_rgate', 'v_b_w_igate', 'v_b_b_igate', 'v_b_lambda', 'v_b_w_o', 'v_c_w_down', 'v_c_q_norm', 'v_c_kv_norm', 'v_c_w_uq', 'v_c_w_ukv', 'v_c_w_o', 'v_mem_w_kv', 'v_x_w_q', 'v_x_w_o', 'v_f_w_up', 'v_f_conv_w', 'v_f_conv_b', 'v_f_w_down', 'v_ln_g', 'v_ln_b']
TWIN_OUTPUTS = ['loss', 'grad_x', 'grad_a_w_qkv', 'grad_a_sinks', 'grad_a_w_o', 'grad_b_w_in', 'grad_b_conv_w', 'grad_b_conv_b', 'grad_b_w_rgate', 'grad_b_b_rgate', 'grad_b_w_igate', 'grad_b_b_igate', 'grad_b_lambda', 'grad_b_w_o', 'grad_c_w_down', 'grad_c_q_norm', 'grad_c_kv_norm', 'grad_c_w_uq', 'grad_c_w_ukv', 'grad_c_w_o', 'grad_mem_w_kv', 'grad_x_w_q', 'grad_x_w_o', 'grad_f_w_up', 'grad_f_conv_w', 'grad_f_conv_b', 'grad_f_w_down', 'grad_ln_g', 'grad_ln_b', 'delta_a_w_qkv', 'delta_a_sinks', 'delta_a_w_o', 'delta_b_w_in', 'delta_b_conv_w', 'delta_b_conv_b', 'delta_b_w_rgate', 'delta_b_b_rgate', 'delta_b_w_igate', 'delta_b_b_igate', 'delta_b_lambda', 'delta_b_w_o', 'delta_c_w_down', 'delta_c_q_norm', 'delta_c_kv_norm', 'delta_c_w_uq', 'delta_c_w_ukv', 'delta_c_w_o', 'delta_mem_w_kv', 'delta_x_w_q', 'delta_x_w_o', 'delta_f_w_up', 'delta_f_conv_w', 'delta_f_conv_b', 'delta_f_w_down', 'delta_ln_g', 'delta_ln_b', 'new_m_a_w_qkv', 'new_m_a_sinks', 'new_m_a_w_o', 'new_m_b_w_in', 'new_m_b_conv_w', 'new_m_b_conv_b', 'new_m_b_w_rgate', 'new_m_b_b_rgate', 'new_m_b_w_igate', 'new_m_b_b_igate', 'new_m_b_lambda', 'new_m_b_w_o', 'new_m_c_w_down', 'new_m_c_q_norm', 'new_m_c_kv_norm', 'new_m_c_w_uq', 'new_m_c_w_ukv', 'new_m_c_w_o', 'new_m_mem_w_kv', 'new_m_x_w_q', 'new_m_x_w_o', 'new_m_f_w_up', 'new_m_f_conv_w', 'new_m_f_conv_b', 'new_m_f_w_down', 'new_m_ln_g', 'new_m_ln_b', 'new_v_a_w_qkv', 'new_v_a_sinks', 'new_v_a_w_o', 'new_v_b_w_in', 'new_v_b_conv_w', 'new_v_b_conv_b', 'new_v_b_w_rgate', 'new_v_b_b_rgate', 'new_v_b_w_igate', 'new_v_b_b_igate', 'new_v_b_lambda', 'new_v_b_w_o', 'new_v_c_w_down', 'new_v_c_q_norm', 'new_v_c_kv_norm', 'new_v_c_w_uq', 'new_v_c_w_ukv', 'new_v_c_w_o', 'new_v_mem_w_kv', 'new_v_x_w_q', 'new_v_x_w_o', 'new_v_f_w_up', 'new_v_f_conv_w', 'new_v_f_conv_b', 'new_v_f_w_down', 'new_v_ln_g', 'new_v_ln_b']
TWIN_LEAF_KINDS = {'loss': 'loss', 'grad_x': 'grad_x', 'grad_a_w_qkv': 'grad_w', 'grad_a_sinks': 'grad_w', 'grad_a_w_o': 'grad_w', 'grad_b_w_in': 'grad_w', 'grad_b_conv_w': 'grad_w', 'grad_b_conv_b': 'grad_w', 'grad_b_w_rgate': 'grad_w', 'grad_b_b_rgate': 'grad_w', 'grad_b_w_igate': 'grad_w', 'grad_b_b_igate': 'grad_w', 'grad_b_lambda': 'grad_w', 'grad_b_w_o': 'grad_w', 'grad_c_w_down': 'grad_w', 'grad_c_q_norm': 'grad_w', 'grad_c_kv_norm': 'grad_w', 'grad_c_w_uq': 'grad_w', 'grad_c_w_ukv': 'grad_w', 'grad_c_w_o': 'grad_w', 'grad_mem_w_kv': 'grad_w', 'grad_x_w_q': 'grad_w', 'grad_x_w_o': 'grad_w', 'grad_f_w_up': 'grad_w', 'grad_f_conv_w': 'grad_w', 'grad_f_conv_b': 'grad_w', 'grad_f_w_down': 'grad_w', 'grad_ln_g': 'grad_w', 'grad_ln_b': 'grad_w', 'delta_a_w_qkv': 'delta_w', 'delta_a_sinks': 'delta_w', 'delta_a_w_o': 'delta_w', 'delta_b_w_in': 'delta_w', 'delta_b_conv_w': 'delta_w', 'delta_b_conv_b': 'delta_w', 'delta_b_w_rgate': 'delta_w', 'delta_b_b_rgate': 'delta_w', 'delta_b_w_igate': 'delta_w', 'delta_b_b_igate': 'delta_w', 'delta_b_lambda': 'delta_w', 'delta_b_w_o': 'delta_w', 'delta_c_w_down': 'delta_w', 'delta_c_q_norm': 'delta_w', 'delta_c_kv_norm': 'delta_w', 'delta_c_w_uq': 'delta_w', 'delta_c_w_ukv': 'delta_w', 'delta_c_w_o': 'delta_w', 'delta_mem_w_kv': 'delta_w', 'delta_x_w_q': 'delta_w', 'delta_x_w_o': 'delta_w', 'delta_f_w_up': 'delta_w', 'delta_f_conv_w': 'delta_w', 'delta_f_conv_b': 'delta_w', 'delta_f_w_down': 'delta_w', 'delta_ln_g': 'delta_w', 'delta_ln_b': 'delta_w', 'new_m_a_w_qkv': 'new_m', 'new_m_a_sinks': 'new_m', 'new_m_a_w_o': 'new_m', 'new_m_b_w_in': 'new_m', 'new_m_b_conv_w': 'new_m', 'new_m_b_conv_b': 'new_m', 'new_m_b_w_rgate': 'new_m', 'new_m_b_b_rgate': 'new_m', 'new_m_b_w_igate': 'new_m', 'new_m_b_b_igate': 'new_m', 'new_m_b_lambda': 'new_m', 'new_m_b_w_o': 'new_m', 'new_m_c_w_down': 'new_m', 'new_m_c_q_norm': 'new_m', 'new_m_c_kv_norm': 'new_m', 'new_m_c_w_uq': 'new_m', 'new_m_c_w_ukv': 'new_m', 'new_m_c_w_o': 'new_m', 'new_m_mem_w_kv': 'new_m', 'new_m_x_w_q': 'new_m', 'new_m_x_w_o': 'new_m', 'new_m_f_w_up': 'new_m', 'new_m_f_conv_w': 'new_m', 'new_m_f_conv_b': 'new_m', 'new_m_f_w_down': 'new_m', 'new_m_ln_g': 'new_m', 'new_m_ln_b': 'new_m', 'new_v_a_w_qkv': 'new_v', 'new_v_a_sinks': 'new_v', 'new_v_a_w_o': 'new_v', 'new_v_b_w_in': 'new_v', 'new_v_b_conv_w': 'new_v', 'new_v_b_conv_b': 'new_v', 'new_v_b_w_rgate': 'new_v', 'new_v_b_b_rgate': 'new_v', 'new_v_b_w_igate': 'new_v', 'new_v_b_b_igate': 'new_v', 'new_v_b_lambda': 'new_v', 'new_v_b_w_o': 'new_v', 'new_v_c_w_down': 'new_v', 'new_v_c_q_norm': 'new_v', 'new_v_c_kv_norm': 'new_v', 'new_v_c_w_uq': 'new_v', 'new_v_c_w_ukv': 'new_v', 'new_v_c_w_o': 'new_v', 'new_v_mem_w_kv': 'new_v', 'new_v_x_w_q': 'new_v', 'new_v_x_w_o': 'new_v', 'new_v_f_w_up': 'new_v', 'new_v_f_conv_w': 'new_v', 'new_v_f_conv_b': 'new_v', 'new_v_f_w_down': 'new_v', 'new_v_ln_g': 'new_v', 'new_v_ln_b': 'new_v'}


def _forward(args):
    return _fwd_reference(*[args[k] for k in FWD_PARAMS])


def _output_shape():
    def fwd():
        inp = _fwd_setup_inputs(0)
        return _fwd_reference(*[inp[k] for k in FWD_PARAMS])
    out = _jax.eval_shape(fwd)
    return out.shape, out.dtype

N_MICROBATCH = 1
ADAM_LR = 0.001
ADAM_B1 = 0.9
ADAM_B2 = 0.999
ADAM_EPS = 1e-08
ADAM_WD = 0.01
ADAM_STEP = 10
PER_EXAMPLE_BATCH_AXIS = {'x': 0, 'mem': 0, 'loss_target': 0}
SHARED_INPUTS = []
_WEIGHT_DTYPES = {'a_w_qkv': _jnp.float32, 'a_sinks': _jnp.float32, 'a_w_o': _jnp.float32, 'b_w_in': _jnp.float32, 'b_conv_w': _jnp.float32, 'b_conv_b': _jnp.float32, 'b_w_rgate': _jnp.float32, 'b_b_rgate': _jnp.float32, 'b_w_igate': _jnp.float32, 'b_b_igate': _jnp.float32, 'b_lambda': _jnp.float32, 'b_w_o': _jnp.float32, 'c_w_down': _jnp.float32, 'c_q_norm': _jnp.float32, 'c_kv_norm': _jnp.float32, 'c_w_uq': _jnp.float32, 'c_w_ukv': _jnp.float32, 'c_w_o': _jnp.float32, 'mem_w_kv': _jnp.float32, 'x_w_q': _jnp.float32, 'x_w_o': _jnp.float32, 'f_w_up': _jnp.float32, 'f_conv_w': _jnp.float32, 'f_conv_b': _jnp.float32, 'f_w_down': _jnp.float32, 'ln_g': _jnp.float32, 'ln_b': _jnp.float32}
MOMENT_SCALE = {'a_w_qkv': 1.757303e-02, 'a_sinks': 1.102626e-02, 'a_w_o': 3.349592e-02, 'b_w_in': 2.769867e-02, 'b_conv_w': 2.824338e-02, 'b_conv_b': 3.806057e-01, 'b_w_rgate': 7.983124e-03, 'b_b_rgate': 7.287088e-03, 'b_w_igate': 1.423139e-02, 'b_b_igate': 9.623440e-03, 'b_lambda': 1.437028e-02, 'b_w_o': 6.666580e-02, 'c_w_down': 2.591053e-02, 'c_q_norm': 1.849998e-02, 'c_kv_norm': 3.556988e-02, 'c_w_uq': 9.514463e-03, 'c_w_ukv': 1.233269e-02, 'c_w_o': 3.458752e-02, 'mem_w_kv': 1.411476e-02, 'x_w_q': 6.734034e-03, 'x_w_o': 1.745110e-02, 'f_w_up': 2.369425e-02, 'f_conv_w': 2.362115e-02, 'f_conv_b': 2.494054e-02, 'f_w_down': 9.189319e-02, 'ln_g': 1.850123e+01, 'ln_b': 7.574982e-01}


def _to_microbatches(a, axis):
    t = _jnp.moveaxis(a, axis, 0)
    t = t.reshape((N_MICROBATCH, t.shape[0] // N_MICROBATCH) + t.shape[1:])
    return _jnp.moveaxis(t, 1, axis + 1)


def setup_inputs(seed: int = 0) -> dict:
    inp = _fwd_setup_inputs(seed)
    key = _jax.random.fold_in(_jax.random.key(seed), 7919)
    shape, _ = _output_shape()
    out = dict(inp)
    out["loss_target"] = _jax.random.normal(_jax.random.fold_in(key, 0), shape, _jnp.float32)
    for i, name in enumerate(TWIN_WEIGHTS):
        w = inp[name].astype(_jnp.float32)
        if MOMENT_SCALE is None:
            s = _jnp.sqrt(_jnp.mean(_jnp.square(w)) + 1e-30)
        else:
            s = MOMENT_SCALE[name]
        km, kv = _jax.random.split(_jax.random.fold_in(key, i + 1))
        out[name] = w
        out["m_" + name] = s * _jax.random.normal(km, w.shape, _jnp.float32)
        out["v_" + name] = (s * s) * _jax.random.uniform(kv, w.shape, _jnp.float32, 0.5, 1.5)
    if N_MICROBATCH > 1:
        for name, axis in PER_EXAMPLE_BATCH_AXIS.items():
            out[name] = _to_microbatches(out[name], axis)
    return {'x': out['x'], 'mem': out['mem'], 'a_w_qkv': out['a_w_qkv'], 'a_sinks': out['a_sinks'], 'a_w_o': out['a_w_o'], 'b_w_in': out['b_w_in'], 'b_conv_w': out['b_conv_w'], 'b_conv_b': out['b_conv_b'], 'b_w_rgate': out['b_w_rgate'], 'b_b_rgate': out['b_b_rgate'], 'b_w_igate': out['b_w_igate'], 'b_b_igate': out['b_b_igate'], 'b_lambda': out['b_lambda'], 'b_w_o': out['b_w_o'], 'c_w_down': out['c_w_down'], 'c_q_norm': out['c_q_norm'], 'c_kv_norm': out['c_kv_norm'], 'c_w_uq': out['c_w_uq'], 'c_w_ukv': out['c_w_ukv'], 'c_w_o': out['c_w_o'], 'mem_w_kv': out['mem_w_kv'], 'x_w_q': out['x_w_q'], 'x_w_o': out['x_w_o'], 'f_w_up': out['f_w_up'], 'f_conv_w': out['f_conv_w'], 'f_conv_b': out['f_conv_b'], 'f_w_down': out['f_w_down'], 'ln_g': out['ln_g'], 'ln_b': out['ln_b'], 'loss_target': out['loss_target'], 'm_a_w_qkv': out['m_a_w_qkv'], 'm_a_sinks': out['m_a_sinks'], 'm_a_w_o': out['m_a_w_o'], 'm_b_w_in': out['m_b_w_in'], 'm_b_conv_w': out['m_b_conv_w'], 'm_b_conv_b': out['m_b_conv_b'], 'm_b_w_rgate': out['m_b_w_rgate'], 'm_b_b_rgate': out['m_b_b_rgate'], 'm_b_w_igate': out['m_b_w_igate'], 'm_b_b_igate': out['m_b_b_igate'], 'm_b_lambda': out['m_b_lambda'], 'm_b_w_o': out['m_b_w_o'], 'm_c_w_down': out['m_c_w_down'], 'm_c_q_norm': out['m_c_q_norm'], 'm_c_kv_norm': out['m_c_kv_norm'], 'm_c_w_uq': out['m_c_w_uq'], 'm_c_w_ukv': out['m_c_w_ukv'], 'm_c_w_o': out['m_c_w_o'], 'm_mem_w_kv': out['m_mem_w_kv'], 'm_x_w_q': out['m_x_w_q'], 'm_x_w_o': out['m_x_w_o'], 'm_f_w_up': out['m_f_w_up'], 'm_f_conv_w': out['m_f_conv_w'], 'm_f_conv_b': out['m_f_conv_b'], 'm_f_w_down': out['m_f_w_down'], 'm_ln_g': out['m_ln_g'], 'm_ln_b': out['m_ln_b'], 'v_a_w_qkv': out['v_a_w_qkv'], 'v_a_sinks': out['v_a_sinks'], 'v_a_w_o': out['v_a_w_o'], 'v_b_w_in': out['v_b_w_in'], 'v_b_conv_w': out['v_b_conv_w'], 'v_b_conv_b': out['v_b_conv_b'], 'v_b_w_rgate': out['v_b_w_rgate'], 'v_b_b_rgate': out['v_b_b_rgate'], 'v_b_w_igate': out['v_b_w_igate'], 'v_b_b_igate': out['v_b_b_igate'], 'v_b_lambda': out['v_b_lambda'], 'v_b_w_o': out['v_b_w_o'], 'v_c_w_down': out['v_c_w_down'], 'v_c_q_norm': out['v_c_q_norm'], 'v_c_kv_norm': out['v_c_kv_norm'], 'v_c_w_uq': out['v_c_w_uq'], 'v_c_w_ukv': out['v_c_w_ukv'], 'v_c_w_o': out['v_c_w_o'], 'v_mem_w_kv': out['v_mem_w_kv'], 'v_x_w_q': out['v_x_w_q'], 'v_x_w_o': out['v_x_w_o'], 'v_f_w_up': out['v_f_w_up'], 'v_f_conv_w': out['v_f_conv_w'], 'v_f_conv_b': out['v_f_conv_b'], 'v_f_w_down': out['v_f_w_down'], 'v_ln_g': out['v_ln_g'], 'v_ln_b': out['v_ln_b']}


def _loss(weights, diff, rest, loss_target):
    with _jax.named_scope("forward"):
        args = {**rest, TWIN_DIFF_INPUT: diff, **{k: w.astype(_WEIGHT_DTYPES[k]) for k, w in weights.items()}}
        y = _forward(args)
    with _jax.named_scope("loss_head"):
        err = _jnp.square(y.astype(_jnp.float32) - loss_target)
        return 0.5 * _jnp.sum(_jnp.mean(err, axis=-1)) if err.ndim else 0.5 * err


def _adamw(w, g, m, v):
    m = ADAM_B1 * m + (1.0 - ADAM_B1) * g
    v = ADAM_B2 * v + (1.0 - ADAM_B2) * _jnp.square(g)
    m_hat = m / (1.0 - ADAM_B1 ** ADAM_STEP)
    v_hat = v / (1.0 - ADAM_B2 ** ADAM_STEP)
    delta = -ADAM_LR * (m_hat / (_jnp.sqrt(v_hat) + ADAM_EPS) + ADAM_WD * w)
    return delta, m, v


def reference(x, mem, a_w_qkv, a_sinks, a_w_o, b_w_in, b_conv_w, b_conv_b, b_w_rgate, b_b_rgate, b_w_igate, b_b_igate, b_lambda, b_w_o, c_w_down, c_q_norm, c_kv_norm, c_w_uq, c_w_ukv, c_w_o, mem_w_kv, x_w_q, x_w_o, f_w_up, f_conv_w, f_conv_b, f_w_down, ln_g, ln_b, loss_target, m_a_w_qkv, m_a_sinks, m_a_w_o, m_b_w_in, m_b_conv_w, m_b_conv_b, m_b_w_rgate, m_b_b_rgate, m_b_w_igate, m_b_b_igate, m_b_lambda, m_b_w_o, m_c_w_down, m_c_q_norm, m_c_kv_norm, m_c_w_uq, m_c_w_ukv, m_c_w_o, m_mem_w_kv, m_x_w_q, m_x_w_o, m_f_w_up, m_f_conv_w, m_f_conv_b, m_f_w_down, m_ln_g, m_ln_b, v_a_w_qkv, v_a_sinks, v_a_w_o, v_b_w_in, v_b_conv_w, v_b_conv_b, v_b_w_rgate, v_b_b_rgate, v_b_w_igate, v_b_b_igate, v_b_lambda, v_b_w_o, v_c_w_down, v_c_q_norm, v_c_kv_norm, v_c_w_uq, v_c_w_ukv, v_c_w_o, v_mem_w_kv, v_x_w_q, v_x_w_o, v_f_w_up, v_f_conv_w, v_f_conv_b, v_f_w_down, v_ln_g, v_ln_b):
    given = dict(x=x, mem=mem, a_w_qkv=a_w_qkv, a_sinks=a_sinks, a_w_o=a_w_o, b_w_in=b_w_in, b_conv_w=b_conv_w, b_conv_b=b_conv_b, b_w_rgate=b_w_rgate, b_b_rgate=b_b_rgate, b_w_igate=b_w_igate, b_b_igate=b_b_igate, b_lambda=b_lambda, b_w_o=b_w_o, c_w_down=c_w_down, c_q_norm=c_q_norm, c_kv_norm=c_kv_norm, c_w_uq=c_w_uq, c_w_ukv=c_w_ukv, c_w_o=c_w_o, mem_w_kv=mem_w_kv, x_w_q=x_w_q, x_w_o=x_w_o, f_w_up=f_w_up, f_conv_w=f_conv_w, f_conv_b=f_conv_b, f_w_down=f_w_down, ln_g=ln_g, ln_b=ln_b, loss_target=loss_target, m_a_w_qkv=m_a_w_qkv, m_a_sinks=m_a_sinks, m_a_w_o=m_a_w_o, m_b_w_in=m_b_w_in, m_b_conv_w=m_b_conv_w, m_b_conv_b=m_b_conv_b, m_b_w_rgate=m_b_w_rgate, m_b_b_rgate=m_b_b_rgate, m_b_w_igate=m_b_w_igate, m_b_b_igate=m_b_b_igate, m_b_lambda=m_b_lambda, m_b_w_o=m_b_w_o, m_c_w_down=m_c_w_down, m_c_q_norm=m_c_q_norm, m_c_kv_norm=m_c_kv_norm, m_c_w_uq=m_c_w_uq, m_c_w_ukv=m_c_w_ukv, m_c_w_o=m_c_w_o, m_mem_w_kv=m_mem_w_kv, m_x_w_q=m_x_w_q, m_x_w_o=m_x_w_o, m_f_w_up=m_f_w_up, m_f_conv_w=m_f_conv_w, m_f_conv_b=m_f_conv_b, m_f_w_down=m_f_w_down, m_ln_g=m_ln_g, m_ln_b=m_ln_b, v_a_w_qkv=v_a_w_qkv, v_a_sinks=v_a_sinks, v_a_w_o=v_a_w_o, v_b_w_in=v_b_w_in, v_b_conv_w=v_b_conv_w, v_b_conv_b=v_b_conv_b, v_b_w_rgate=v_b_w_rgate, v_b_b_rgate=v_b_b_rgate, v_b_w_igate=v_b_w_igate, v_b_b_igate=v_b_b_igate, v_b_lambda=v_b_lambda, v_b_w_o=v_b_w_o, v_c_w_down=v_c_w_down, v_c_q_norm=v_c_q_norm, v_c_kv_norm=v_c_kv_norm, v_c_w_uq=v_c_w_uq, v_c_w_ukv=v_c_w_ukv, v_c_w_o=v_c_w_o, v_mem_w_kv=v_mem_w_kv, v_x_w_q=v_x_w_q, v_x_w_o=v_x_w_o, v_f_w_up=v_f_w_up, v_f_conv_w=v_f_conv_w, v_f_conv_b=v_f_conv_b, v_f_w_down=v_f_w_down, v_ln_g=v_ln_g, v_ln_b=v_ln_b)
    weights = {n: given[n] for n in TWIN_WEIGHTS}
    shared = {n: given[n] for n in SHARED_INPUTS}
    per_example = {n: given[n] for n in ['x', 'mem']}
    grad_fn = _jax.value_and_grad(_loss, argnums=(0, 1))

    def one_microbatch(ex, loss_target):
        ex = dict(ex)
        diff = ex.pop(TWIN_DIFF_INPUT)
        return grad_fn(weights, diff, {**shared, **ex}, loss_target)

    if N_MICROBATCH == 1:
        loss, (grad_w, grad_x) = one_microbatch(per_example, given["loss_target"])
    else:
        def body(carry, xs):
            loss_sum, grad_sum = carry
            l_k, (gw_k, gx_k) = one_microbatch(xs[0], xs[1])
            with _jax.named_scope("update"):
                return (loss_sum + l_k, _jax.tree.map(_jnp.add, grad_sum, gw_k)), gx_k

        init = (_jnp.zeros((), _jnp.float32), _jax.tree.map(_jnp.zeros_like, weights))
        (loss, grad_w), grad_x = _jax.lax.scan(body, init, (per_example, given["loss_target"]))
    with _jax.named_scope("update"):
        delta_w, new_m, new_v = {}, {}, {}
        for n in TWIN_WEIGHTS:
            delta_w[n], new_m[n], new_v[n] = _adamw(weights[n], grad_w[n], given["m_" + n], given["v_" + n])
    return (loss, grad_x, *[grad_w[n] for n in TWIN_WEIGHTS], *[delta_w[n] for n in TWIN_WEIGHTS],
            *[new_m[n] for n in TWIN_WEIGHTS], *[new_v[n] for n in TWIN_WEIGHTS])
```

```python
import functools
import math

import numpy as np
import jax
import jax.numpy as jnp
from jax import lax
from jax.experimental import pallas as pl
from jax.experimental.pallas import tpu as pltpu

F32 = jnp.float32
BF16 = jnp.bfloat16
MESH = pl.DeviceIdType.MESH

D_MODEL = 1024
DEPTH = 4
N_MIXERS = 3
MEM_LEN = 256
BLOCK = 128
ROPE_THETA = 10000.0
NEG = -1e30
LN_EPS = 1e-5
RMS_EPS = 1e-6
A_HEADS, A_KV_HEADS, A_HEAD_DIM = 16, 4, 64
LRU_BLOCKS, LRU_BLOCK_W, LRU_CONV, LRU_C = 4, 256, 4, 8.0
C_HEADS, C_NOPE, C_ROPE, C_V, C_Q_RANK, C_KV_RANK = 8, 128, 64, 128, 384, 256
X_HEADS, X_HEAD_DIM = 4, 256
D_FF, FFN_CONV = 2816, 3
ALPHA = (2.0 * DEPTH) ** 0.25
ADAM_LR, ADAM_B1, ADAM_B2, ADAM_EPS, ADAM_WD, ADAM_STEP = 0.001, 0.9, 0.999, 1e-08, 0.01, 10

VMEM_LIMIT = 56 * 2 ** 20
LANES = 128
PACK_COLS = 1024
ROW_T = 512
ACT_T = 256
LRU_T = 256
FLASH_T = 512
FFN_TC = 1408
MM_T = 1024
GRAD_TK = 1024

C11 = (((1,), (1,)), ((), ()))
C00 = (((0,), (0,)), ((), ()))

SHARDED = [
    ("a_w_qkv", 2), ("a_w_o", 1), ("b_w_in", 2), ("b_w_rgate", 2), ("b_w_igate", 2), ("b_w_o", 1), ("c_w_down", 1),
    ("c_w_uq", 2), ("c_w_ukv", 2), ("c_w_o", 1), ("mem_w_kv", 1), ("x_w_q", 1), ("x_w_o", 1), ("f_w_up", 2),
    ("f_w_down", 1),
    ("b_conv_w", 2), ("c_q_norm", 1), ("c_kv_norm", 1), ("f_conv_w", 2), ("ln_g", 2), ("ln_b", 2),
]
N_BIG = 15
REPLICATED = ["a_sinks", "b_conv_b", "b_b_rgate", "b_b_igate", "b_lambda", "f_conv_b"]
WEIGHTS = ["a_w_qkv", "a_sinks", "a_w_o", "b_w_in", "b_conv_w", "b_conv_b", "b_w_rgate", "b_b_rgate", "b_w_igate",
           "b_b_igate", "b_lambda", "b_w_o", "c_w_down", "c_q_norm", "c_kv_norm", "c_w_uq", "c_w_ukv", "c_w_o",
           "mem_w_kv", "x_w_q", "x_w_o", "f_w_up", "f_conv_w", "f_conv_b", "f_w_down", "ln_g", "ln_b"]


def _params(*sem):
    return pltpu.CompilerParams(dimension_semantics=sem, vmem_limit_bytes=VMEM_LIMIT)


def _sds(shape, dtype):
    return jax.ShapeDtypeStruct(tuple(shape), dtype)


def _mm(a, b, *, name, ta=False, tb=False, out_dtype=F32, tm=None, tn=None, tk=None):
    (K, M) = a.shape if ta else a.shape[::-1]
    (N, K2) = b.shape if tb else b.shape[::-1]
    assert K == K2, (a.shape, b.shape, ta, tb)
    tm = min(tm or MM_T, M)
    tn = min(tn or N, N)
    tk = min(tk or K, K)
    assert M % tm == 0 and N % tn == 0 and K % tk == 0, (M, N, K, tm, tn, tk)
    nk = K // tk
    assert nk == 1 or out_dtype == F32
    dims = (((0 if ta else 1,), (1 if tb else 0,)), ((), ()))

    def body(a_ref, b_ref, o_ref):
        p = lax.dot_general(a_ref[...].astype(BF16), b_ref[...].astype(BF16), dims, preferred_element_type=F32)
        if nk == 1:
            o_ref[...] = p.astype(out_dtype)
        else:
            k = pl.program_id(2)

            @pl.when(k == 0)
            def _():
                o_ref[...] = p

            @pl.when(k > 0)
            def _():
                o_ref[...] += p

    a_spec = pl.BlockSpec((tk, tm), lambda i, j, k: (k, i)) if ta else pl.BlockSpec((tm, tk), lambda i, j, k: (i, k))
    b_spec = pl.BlockSpec((tn, tk), lambda i, j, k: (j, k)) if tb else pl.BlockSpec((tk, tn), lambda i, j, k: (k, j))
    return pl.pallas_call(
        body, name=name, grid=(M // tm, N // tn, nk), in_specs=[a_spec, b_spec],
        out_specs=pl.BlockSpec((tm, tn), lambda i, j, k: (i, j)), out_shape=_sds((M, N), out_dtype),
        compiler_params=_params("parallel", "parallel", "arbitrary"),
    )(a, b)


def _shift_down(cur, prev8, d):
    rolled = pltpu.roll(cur, d, 0)
    rid = lax.broadcasted_iota(jnp.int32, prev8.shape, 0)
    head = jnp.where(rid < d, pltpu.roll(prev8, d, 0), rolled[0:8])
    return jnp.concatenate([head, rolled[8:]], axis=0)


def _shift_up(cur, next8, d):
    n = cur.shape[0]
    rolled = pltpu.roll(cur, n - d, 0)
    rid = lax.broadcasted_iota(jnp.int32, next8.shape, 0)
    tail = jnp.where(rid >= 8 - d, pltpu.roll(next8, 8 - d, 0), rolled[n - 8:n])
    return jnp.concatenate([rolled[0:n - 8], tail], axis=0)


def _swap_halves(x):
    w = x.shape[-1]
    if w == 64:
        return jnp.concatenate([x[:, 32:64], x[:, 0:32]], axis=1)
    lane = lax.broadcasted_iota(jnp.int32, x.shape, 1)
    return jnp.where((lane % 64) < 32, pltpu.roll(x, w - 32, 1), pltpu.roll(x, 32, 1))


def _tile_lanes(t, w):
    return t if w == t.shape[-1] else jnp.concatenate([t] * (w // t.shape[-1]), axis=1)


def _rope(x, cos, sin):
    w = x.shape[-1]
    if w == 64:
        cos, sin = cos[:, :64], sin[:, :64]
    else:
        cos, sin = _tile_lanes(cos, w), _tile_lanes(sin, w)
    return x * cos + _swap_halves(x) * sin


def _rope_t(x, cos, sin):
    w = x.shape[-1]
    if w == 64:
        cos, sin = cos[:, :64], sin[:, :64]
    else:
        cos, sin = _tile_lanes(cos, w), _tile_lanes(sin, w)
    return x * cos - _swap_halves(x) * sin


def _sigmoid(x):
    return 1.0 / (1.0 + jnp.exp(-x))


def _gelu_and_grad(x):
    c0, c1 = math.sqrt(2.0 / math.pi), 0.044715
    t = jnp.tanh(c0 * (x + c1 * x * x * x))
    g = 0.5 * x * (1.0 + t)
    dg = 0.5 * (1.0 + t) + 0.5 * x * (1.0 - t * t) * c0 * (1.0 + 3.0 * c1 * x * x)
    return g, dg


def _neg_expm1(x):
    series = -x * (1.0 + x * (0.5 + x * (1.0 / 6.0 + x * (1.0 / 24.0 + x * (1.0 / 120.0)))))
    return jnp.where(x > -0.1, series, 1.0 - jnp.exp(x))


def _softplus_neg(lam):
    z = -lam
    e = jnp.exp(-jnp.abs(z))
    log1p = jnp.where(e < 0.01, e * (1.0 - e * (0.5 - e * (1.0 / 3.0))), jnp.log(1.0 + e))
    sp = jnp.maximum(z, 0.0) + log1p
    dsp = -_sigmoid(z)
    return sp, dsp


def _ln_fwd(x, y, g, b, *, name):
    S, D = x.shape
    tm = min(ROW_T, S)

    def body(x_ref, y_ref, g_ref, b_ref, o_ref, ob_ref, xh_ref, rs_ref):
        z = ALPHA * x_ref[...] + y_ref[...]
        mu = jnp.mean(z, axis=-1, keepdims=True)
        zc = z - mu
        var = jnp.mean(zc * zc, axis=-1, keepdims=True)
        r = lax.rsqrt(var + LN_EPS)
        xh = zc * r
        o = xh * g_ref[...] + b_ref[...]
        o_ref[...] = o
        ob_ref[...] = o.astype(BF16)
        xh_ref[...] = xh
        rs_ref[...] = r

    row = pl.BlockSpec((tm, D), lambda i: (i, 0))
    vec = pl.BlockSpec((1, D), lambda i: (0, 0))
    return pl.pallas_call(
        body, name=name, grid=(S // tm,), in_specs=[row, row, vec, vec],
        out_specs=[row, row, row, pl.BlockSpec((tm, 1), lambda i: (i, 0))],
        out_shape=[_sds((S, D), F32), _sds((S, D), BF16), _sds((S, D), F32), _sds((S, 1), F32)],
        compiler_params=_params("parallel"),
    )(x, y, g, b)


def _ln_bwd(d1, d2, xh, rs, g, *, name):
    S, D = xh.shape
    tm = min(ROW_T, S)
    has_d1 = d1 is not None

    def body(*refs):
        if has_d1:
            d1_ref, d2_ref, xh_ref, rs_ref, g_ref, dz_ref, dzb_ref, dg_ref, db_ref = refs
            dout = ALPHA * d1_ref[...] + d2_ref[...]
        else:
            d2_ref, xh_ref, rs_ref, g_ref, dz_ref, dzb_ref, dg_ref, db_ref = refs
            dout = d2_ref[...]
        xh_v = xh_ref[...]
        dxh = dout * g_ref[...]
        m1 = jnp.mean(dxh, axis=-1, keepdims=True)
        m2 = jnp.mean(dxh * xh_v, axis=-1, keepdims=True)
        dz = rs_ref[...] * (dxh - m1 - xh_v * m2)
        dz_ref[...] = dz
        dzb_ref[...] = dz.astype(BF16)

        @pl.when(pl.program_id(0) == 0)
        def _():
            dg_ref[...] = jnp.zeros_like(dg_ref)
            db_ref[...] = jnp.zeros_like(db_ref)

        dg_ref[...] += jnp.sum(dout * xh_v, axis=0, keepdims=True)
        db_ref[...] += jnp.sum(dout, axis=0, keepdims=True)

    row = pl.BlockSpec((tm, D), lambda i: (i, 0))
    vec = pl.BlockSpec((1, D), lambda i: (0, 0))
    ins = ([row] if has_d1 else []) + [row, row, pl.BlockSpec((tm, 1), lambda i: (i, 0)), vec]
    args = ([d1] if has_d1 else []) + [d2, xh, rs, g]
    return pl.pallas_call(
        body, name=name, grid=(S // tm,), in_specs=ins, out_specs=[row, row, vec, vec],
        out_shape=[_sds((S, D), F32), _sds((S, D), BF16), _sds((1, D), F32), _sds((1, D), F32)],
        compiler_params=_params("arbitrary"),
    )(*args)


def _loss_fwd(y, target, *, name):
    S, D = y.shape
    tm = min(ROW_T, S)

    def body(y_ref, t_ref, d_ref, l_ref):
        e = y_ref[...] - t_ref[...]
        d_ref[...] = e * (1.0 / D)

        @pl.when(pl.program_id(0) == 0)
        def _():
            l_ref[...] = jnp.zeros_like(l_ref)

        part = jnp.sum(e * e, axis=0, keepdims=True)
        l_ref[...] += (0.5 / D) * jnp.sum(part, axis=1, keepdims=True)

    row = pl.BlockSpec((tm, D), lambda i: (i, 0))
    return pl.pallas_call(
        body, name=name, grid=(S // tm,), in_specs=[row, row],
        out_specs=[row, pl.BlockSpec((1, 1), lambda i: (0, 0))], out_shape=[_sds((S, D), F32), _sds((1, 1), F32)],
        compiler_params=_params("arbitrary"),
    )(y, target)


def _axpy(d1, d2, *, name):
    S, D = d1.shape
    tm = min(ROW_T, S)

    def body(a_ref, b_ref, o_ref):
        o_ref[...] = ALPHA * a_ref[...] + b_ref[...]

    row = pl.BlockSpec((tm, D), lambda i: (i, 0))
    return pl.pallas_call(body, name=name, grid=(S // tm,), in_specs=[row, row], out_specs=row,
                          out_shape=_sds((S, D), F32), compiler_params=_params("parallel"))(d1, d2)


def _ffn_act_fwd(h, cw, cb, *, name):
    S, W = h.shape
    tc = FFN_TC
    nj = W // (2 * tc)
    tm = min(ACT_T, S)

    def body(h_ref, w_ref, b_ref, a_ref, carry):
        @pl.when(pl.program_id(1) == 0)
        def _():
            carry[...] = jnp.zeros_like(carry)

        cur = h_ref[...].astype(F32)
        prev8 = carry[...]
        hc = cur * w_ref[2:3, :] + _shift_down(cur, prev8, 1) * w_ref[1:2, :] + _shift_down(cur, prev8, 2) * w_ref[0:1, :]
        hc = hc + b_ref[...]
        carry[...] = cur[tm - 8:tm]
        hg, hu = hc[:, :tc], hc[:, tc:]
        a_ref[...] = (hg * _sigmoid(hg) * hu).astype(BF16)

    return pl.pallas_call(
        body, name=name, grid=(nj, S // tm),
        in_specs=[pl.BlockSpec((tm, 2 * tc), lambda j, i: (i, j)), pl.BlockSpec((3, 2 * tc), lambda j, i: (0, j)),
                  pl.BlockSpec((1, 2 * tc), lambda j, i: (0, j))],
        out_specs=pl.BlockSpec((tm, tc), lambda j, i: (i, j)), out_shape=_sds((S, W // 2), BF16),
        scratch_shapes=[pltpu.VMEM((8, 2 * tc), F32)],
        compiler_params=_params("parallel", "arbitrary"),
    )(h, cw, cb)


def _ffn_act_bwd(h, da, cw, cb, *, name):
    S, W = h.shape
    tc = FFN_TC
    nj = W // (2 * tc)
    tm = min(ACT_T, S)
    ni = S // tm

    def body(h_ref, hp_ref, da_ref, w_ref, b_ref, dh_ref, dw_ref, db_ref, carry):
        i = pl.program_id(1)
        r = ni - 1 - i

        @pl.when(i == 0)
        def _():
            carry[...] = jnp.zeros_like(carry)
            dw_ref[...] = jnp.zeros_like(dw_ref)
            db_ref[...] = jnp.zeros_like(db_ref)

        cur = h_ref[...].astype(F32)
        prev8 = jnp.where(r > 0, hp_ref[8:16, :].astype(F32), 0.0)
        sh = [cur, _shift_down(cur, prev8, 1), _shift_down(cur, prev8, 2)]
        hc = sh[0] * w_ref[2:3, :] + sh[1] * w_ref[1:2, :] + sh[2] * w_ref[0:1, :] + b_ref[...]
        hg, hu = hc[:, :tc], hc[:, tc:]
        d = da_ref[...].astype(F32)
        sg = _sigmoid(hg)
        dg = d * hu * (sg * (1.0 + hg * (1.0 - sg)))
        du = d * (hg * sg)
        dhc = jnp.concatenate([dg, du], axis=1)
        db_ref[...] += jnp.sum(dhc, axis=0, keepdims=True)
        for k in range(3):
            dw_ref[k:k + 1, :] += jnp.sum(dhc * sh[2 - k], axis=0, keepdims=True)
        next8 = carry[...]
        dh = dhc * w_ref[2:3, :] + _shift_up(dhc, next8, 1) * w_ref[1:2, :] + _shift_up(dhc, next8, 2) * w_ref[0:1, :]
        carry[...] = dhc[0:8]
        dh_ref[...] = dh.astype(BF16)

    rev = lambda j, i: (ni - 1 - i, j)
    return pl.pallas_call(
        body, name=name, grid=(nj, ni),
        in_specs=[pl.BlockSpec((tm, 2 * tc), rev),
                  pl.BlockSpec((16, 2 * tc), lambda j, i: (jnp.maximum((ni - 1 - i) * (tm // 16) - 1, 0), j)),
                  pl.BlockSpec((tm, tc), rev), pl.BlockSpec((3, 2 * tc), lambda j, i: (0, j)),
                  pl.BlockSpec((1, 2 * tc), lambda j, i: (0, j))],
        out_specs=[pl.BlockSpec((tm, 2 * tc), rev), pl.BlockSpec((3, 2 * tc), lambda j, i: (0, j)),
                   pl.BlockSpec((1, 2 * tc), lambda j, i: (0, j))],
        out_shape=[_sds((S, W), BF16), _sds((3, W), F32), _sds((1, W), F32)],
        scratch_shapes=[pltpu.VMEM((8, 2 * tc), F32)],
        compiler_params=_params("parallel", "arbitrary"),
    )(h, h, da, cw, cb)


def _xattn_probs(q, k):
    s = lax.dot_general(q, k, C11, preferred_element_type=F32) * (X_HEAD_DIM ** -0.5)
    p = jnp.exp(s - jnp.max(s, axis=-1, keepdims=True))
    return p / jnp.sum(p, axis=-1, keepdims=True)


def _xattn_fwd(q, mkv, *, name):
    S, D = q.shape
    tm = min(ROW_T, S)

    def body(q_ref, k_ref, v_ref, o_ref):
        for h in range(X_HEADS):
            sl = slice(h * X_HEAD_DIM, (h + 1) * X_HEAD_DIM)
            p = _xattn_probs(q_ref[:, sl], k_ref[:, sl])
            o_ref[:, sl] = jnp.dot(p.astype(BF16), v_ref[:, sl], preferred_element_type=F32).astype(BF16)

    return pl.pallas_call(
        body, name=name, grid=(S // tm,),
        in_specs=[pl.BlockSpec((tm, D), lambda i: (i, 0)), pl.BlockSpec((MEM_LEN, D), lambda i: (0, 0)),
                  pl.BlockSpec((MEM_LEN, D), lambda i: (0, 1))],
        out_specs=pl.BlockSpec((tm, D), lambda i: (i, 0)), out_shape=_sds((S, D), BF16),
        compiler_params=_params("parallel"),
    )(q, mkv, mkv)


def _xattn_bwd(q, mkv, do, *, name):
    S, D = q.shape
    tm = min(ROW_T, S)
    scale = X_HEAD_DIM ** -0.5

    def body(q_ref, k_ref, v_ref, do_ref, dq_ref, dkv_ref):
        @pl.when(pl.program_id(0) == 0)
        def _():
            dkv_ref[...] = jnp.zeros_like(dkv_ref)

        for h in range(X_HEADS):
            sl = slice(h * X_HEAD_DIM, (h + 1) * X_HEAD_DIM)
            sv = slice(D + h * X_HEAD_DIM, D + (h + 1) * X_HEAD_DIM)
            qh, kh, vh, doh = q_ref[:, sl], k_ref[:, sl], v_ref[:, sl], do_ref[:, sl]
            p = _xattn_probs(qh, kh)
            dp = lax.dot_general(doh, vh, C11, preferred_element_type=F32)
            ds = (p * (dp - jnp.sum(p * dp, axis=-1, keepdims=True)) * scale).astype(BF16)
            dq_ref[:, sl] = jnp.dot(ds, kh, preferred_element_type=F32).astype(BF16)
            dkv_ref[:, sl] += lax.dot_general(ds, qh, C00, preferred_element_type=F32)
            dkv_ref[:, sv] += lax.dot_general(p.astype(BF16), doh, C00, preferred_element_type=F32)

    row = pl.BlockSpec((tm, D), lambda i: (i, 0))
    return pl.pallas_call(
        body, name=name, grid=(S // tm,),
        in_specs=[row, pl.BlockSpec((MEM_LEN, D), lambda i: (0, 0)), pl.BlockSpec((MEM_LEN, D), lambda i: (0, 1)), row],
        out_specs=[row, pl.BlockSpec((MEM_LEN, 2 * D), lambda i: (0, 0))],
        out_shape=[_sds((S, D), BF16), _sds((MEM_LEN, 2 * D), F32)],
        compiler_params=_params("arbitrary"),
    )(q, mkv, mkv, do)


def _rope_cols(x, cos, sin, n_rope, *, name):
    S, W = x.shape
    tm = min(ROW_T, S)

    def body(x_ref, c_ref, s_ref, o_ref):
        o_ref[:, :n_rope] = _rope(x_ref[:, :n_rope], c_ref[...], s_ref[...]).astype(BF16)
        if n_rope < W:
            o_ref[:, n_rope:] = x_ref[:, n_rope:].astype(BF16)

    row = pl.BlockSpec((tm, W), lambda i: (i, 0))
    tab = pl.BlockSpec((tm, LANES), lambda i: (i, 0))
    return pl.pallas_call(body, name=name, grid=(S // tm,), in_specs=[row, tab, tab], out_specs=row,
                          out_shape=_sds((S, W), BF16), compiler_params=_params("parallel"))(x, cos, sin)


def _swa_band(n):
    qi = lax.broadcasted_iota(jnp.int32, (BLOCK, 2 * BLOCK), 0)
    kj = lax.broadcasted_iota(jnp.int32, (BLOCK, 2 * BLOCK), 1)
    first = jnp.where(n > 0, 0, BLOCK)
    return ((kj < BLOCK) & (kj > qi + first)) | ((kj >= BLOCK) & (kj - BLOCK <= qi))


def _swa_probs(q, k, band, sink):
    s = lax.dot_general(q, k, C11, preferred_element_type=F32) * (A_HEAD_DIM ** -0.5)
    s = jnp.where(band, s, NEG)
    m = jnp.maximum(jnp.max(s, axis=-1, keepdims=True), sink)
    p = jnp.exp(s - m)
    e_sink = jnp.exp(sink - m)
    den = jnp.sum(p, axis=-1, keepdims=True) + e_sink
    return p / den, e_sink / den


def _swa_specs():
    nq, nkv = A_HEADS * A_HEAD_DIM, A_KV_HEADS * A_HEAD_DIM
    kb, vb = nq // nkv, nq // nkv + 1
    prev = lambda n: jnp.maximum(n - 1, 0)
    return [pl.BlockSpec((BLOCK, nq), lambda n: (n, 0)),
            pl.BlockSpec((BLOCK, nkv), lambda n: (n, kb)), pl.BlockSpec((BLOCK, nkv), lambda n: (prev(n), kb)),
            pl.BlockSpec((BLOCK, nkv), lambda n: (n, vb)), pl.BlockSpec((BLOCK, nkv), lambda n: (prev(n), vb)),
            pl.BlockSpec(memory_space=pltpu.SMEM)]


def _swa_fwd(qkv, sinks, *, name):
    S = qkv.shape[0]
    hd, grp = A_HEAD_DIM, A_HEADS // A_KV_HEADS

    def body(q_ref, kc_ref, kp_ref, vc_ref, vp_ref, sink_ref, o_ref):
        band = _swa_band(pl.program_id(0))
        qa, kc, kp, vc, vp = q_ref[...], kc_ref[...], kp_ref[...], vc_ref[...], vp_ref[...]
        for hk in range(A_KV_HEADS):
            ks = slice(hk * hd, (hk + 1) * hd)
            k = jnp.concatenate([kp[:, ks], kc[:, ks]], axis=0)
            v = jnp.concatenate([vp[:, ks], vc[:, ks]], axis=0)
            for gi in range(grp):
                h = hk * grp + gi
                p, _ = _swa_probs(qa[:, h * hd:(h + 1) * hd], k, band, sink_ref[h])
                o_ref[:, h * hd:(h + 1) * hd] = jnp.dot(p.astype(BF16), v, preferred_element_type=F32).astype(BF16)

    return pl.pallas_call(
        body, name=name, grid=(S // BLOCK,), in_specs=_swa_specs(),
        out_specs=pl.BlockSpec((BLOCK, A_HEADS * hd), lambda n: (n, 0)), out_shape=_sds((S, A_HEADS * hd), BF16),
        compiler_params=_params("parallel"),
    )(qkv, qkv, qkv, qkv, qkv, sinks)


def _swa_bwd(qkv, sinks, do, cos, sin, *, name):
    S = qkv.shape[0]
    hd, grp = A_HEAD_DIM, A_HEADS // A_KV_HEADS
    nq, nkv = A_HEADS * hd, A_KV_HEADS * hd
    scale = hd ** -0.5

    def body(q_ref, kc_ref, kp_ref, vc_ref, vp_ref, sink_ref, do_ref, c_ref, s_ref, dq_ref, dc_ref, dp_ref, ds_ref, dq_s):
        @pl.when(pl.program_id(0) == 0)
        def _():
            ds_ref[...] = jnp.zeros_like(ds_ref)

        band = _swa_band(pl.program_id(0))
        lane = lax.broadcasted_iota(jnp.int32, (1, LANES), 1)
        qa, kc, kp, vc, vp, doa = q_ref[...], kc_ref[...], kp_ref[...], vc_ref[...], vp_ref[...], do_ref[...]
        dsink = jnp.zeros((1, LANES), F32)
        for hk in range(A_KV_HEADS):
            ks = slice(hk * hd, (hk + 1) * hd)
            k = jnp.concatenate([kp[:, ks], kc[:, ks]], axis=0)
            v = jnp.concatenate([vp[:, ks], vc[:, ks]], axis=0)
            dk = jnp.zeros((2 * BLOCK, hd), F32)
            dv = jnp.zeros((2 * BLOCK, hd), F32)
            for gi in range(grp):
                h = hk * grp + gi
                hs = slice(h * hd, (h + 1) * hd)
                qh, doh = qa[:, hs], doa[:, hs]
                p, p_sink = _swa_probs(qh, k, band, sink_ref[h])
                dpr = lax.dot_general(doh, v, C11, preferred_element_type=F32)
                delta = jnp.sum(p * dpr, axis=-1, keepdims=True)
                dsc = (p * (dpr - delta) * scale).astype(BF16)
                dq_s[:, hs] = jnp.dot(dsc, k, preferred_element_type=F32)
                dk = dk + lax.dot_general(dsc, qh, C00, preferred_element_type=F32)
                dv = dv + lax.dot_general(p.astype(BF16), doh, C00, preferred_element_type=F32)
                dsink = dsink + jnp.where(lane == h, -jnp.sum(p_sink * delta, axis=0, keepdims=True), 0.0)
            dp_ref[:, ks] = dk[:BLOCK]
            dc_ref[:, ks] = dk[BLOCK:]
            dp_ref[:, nkv + hk * hd:nkv + (hk + 1) * hd] = dv[:BLOCK]
            dc_ref[:, nkv + hk * hd:nkv + (hk + 1) * hd] = dv[BLOCK:]
        ds_ref[...] += dsink
        dq_ref[...] = _rope_t(dq_s[...], c_ref[...], s_ref[...]).astype(BF16)

    tab = pl.BlockSpec((BLOCK, LANES), lambda n: (n, 0))
    blk = lambda w: pl.BlockSpec((BLOCK, w), lambda n: (n, 0))
    return pl.pallas_call(
        body, name=name, grid=(S // BLOCK,), in_specs=_swa_specs() + [blk(nq), tab, tab],
        out_specs=[blk(nq), blk(2 * nkv), blk(2 * nkv), pl.BlockSpec((1, LANES), lambda n: (0, 0))],
        out_shape=[_sds((S, nq), BF16), _sds((S, 2 * nkv), F32), _sds((S, 2 * nkv), F32), _sds((1, LANES), F32)],
        scratch_shapes=[pltpu.VMEM((BLOCK, nq), F32)],
        compiler_params=_params("arbitrary"),
    )(qkv, qkv, qkv, qkv, qkv, sinks, do, cos, sin)


def _swa_dqkv(dq, dcur, dprev, cos, sin, *, name):
    S, nq = dq.shape
    nkv = dcur.shape[1] // 2
    nb = S // BLOCK

    def body(dq_ref, dc_ref, dp_ref, c_ref, s_ref, o_ref):
        o_ref[:, :nq] = dq_ref[...]
        d = dc_ref[...] + jnp.where(pl.program_id(0) < nb - 1, dp_ref[...], 0.0)
        o_ref[:, nq:nq + nkv] = _rope_t(d[:, :nkv], c_ref[...], s_ref[...]).astype(BF16)
        o_ref[:, nq + nkv:] = d[:, nkv:].astype(BF16)

    tab = pl.BlockSpec((BLOCK, LANES), lambda m: (m, 0))
    blk = lambda w: pl.BlockSpec((BLOCK, w), lambda m: (m, 0))
    return pl.pallas_call(
        body, name=name, grid=(nb,),
        in_specs=[blk(nq), blk(2 * nkv), pl.BlockSpec((BLOCK, 2 * nkv), lambda m: (jnp.minimum(m + 1, nb - 1), 0)), tab, tab],
        out_specs=blk(nq + 2 * nkv), out_shape=_sds((S, nq + 2 * nkv), BF16), compiler_params=_params("parallel"),
    )(dq, dcur, dprev, cos, sin)


def _lru_gates(u, wri_ref, br, bi, sp):
    ub = u.astype(BF16)
    rs, igs = [], []
    for hb in range(LRU_BLOCKS):
        sl = slice(hb * LRU_BLOCK_W, (hb + 1) * LRU_BLOCK_W)
        ri = jnp.dot(ub[:, sl], wri_ref[hb], preferred_element_type=F32)
        rs.append(ri[:, :LRU_BLOCK_W])
        igs.append(ri[:, LRU_BLOCK_W:])
    r = _sigmoid(jnp.concatenate(rs, axis=1) + br)
    ig = _sigmoid(jnp.concatenate(igs, axis=1) + bi)
    la = -LRU_C * r * sp
    a = jnp.exp(la)
    sq = jnp.sqrt(_neg_expm1(2.0 * la))
    return r, ig, a, sq


def _lru_fwd(xw, cw, cb, wri, br, bi, lam, *, name):
    S = xw.shape[0]
    W = D_MODEL
    tm = min(LRU_T, S)

    def body(gate_ref, up_ref, cw_ref, cb_ref, wri_ref, br_ref, bi_ref, lam_ref, y_ref, u_ref, h_ref, cu, ch, a_s, b_s):
        @pl.when(pl.program_id(0) == 0)
        def _():
            cu[...] = jnp.zeros_like(cu)
            ch[...] = jnp.zeros_like(ch)

        up = up_ref[...]
        prev8 = cu[...]
        u = up * cw_ref[3:4, :] + cb_ref[...]
        for d in range(1, LRU_CONV):
            u = u + _shift_down(up, prev8, d) * cw_ref[3 - d:4 - d, :]
        cu[...] = up[tm - 8:tm]
        u_ref[...] = u
        sp, _ = _softplus_neg(lam_ref[...])
        _, ig, a, sq = _lru_gates(u, wri_ref, br_ref[...], bi_ref[...], sp)
        a_s[...] = a
        b_s[...] = sq * (ig * u)
        rid = lax.broadcasted_iota(jnp.int32, (8, W), 0)

        def tile(t, h):
            r0 = pl.multiple_of(t * 8, 8)
            at, bt = a_s[pl.ds(r0, 8), :], b_s[pl.ds(r0, 8), :]
            out = jnp.zeros((8, W), F32)
            for j in range(8):
                h = at[j:j + 1, :] * h + bt[j:j + 1, :]
                out = jnp.where(rid == j, h, out)
            h_ref[pl.ds(r0, 8), :] = out
            return h

        ch[0:1, :] = lax.fori_loop(0, tm // 8, tile, ch[0:1, :])
        g, _ = _gelu_and_grad(gate_ref[...])
        y_ref[...] = (h_ref[...] * g).astype(BF16)

    row = pl.BlockSpec((tm, W), lambda i: (i, 0))
    full = lambda shape: pl.BlockSpec(shape, lambda i: (0,) * len(shape))
    return pl.pallas_call(
        body, name=name, grid=(S // tm,),
        in_specs=[row, pl.BlockSpec((tm, W), lambda i: (i, 1)), full((LRU_CONV, W)), full((1, W)),
                  full((LRU_BLOCKS, LRU_BLOCK_W, 2 * LRU_BLOCK_W)), full((1, W)), full((1, W)), full((1, W))],
        out_specs=[row, row, row], out_shape=[_sds((S, W), BF16), _sds((S, W), F32), _sds((S, W), F32)],
        scratch_shapes=[pltpu.VMEM((8, W), F32), pltpu.VMEM((8, W), F32), pltpu.VMEM((tm, W), F32), pltpu.VMEM((tm, W), F32)],
        compiler_params=_params("arbitrary"),
    )(xw, xw, cw, cb, wri, br, bi, lam)


def _lru_bwd(xw, u, h, dy, cw, wri, br, bi, lam, *, name):
    S = xw.shape[0]
    W = D_MODEL
    tm = min(LRU_T, S)
    nb = S // tm

    def body(gate_ref, up_ref, upp_ref, u_ref, h_ref, hp_ref, dy_ref, cw_ref, wri_ref, br_ref, bi_ref, lam_ref,
             dxw_ref, dcw_ref, dcb_ref, dwri_ref, dbr_ref, dbi_ref, dlam_ref, cg, cdu, a_s, d_s, g_s):
        i = pl.program_id(0)
        r_blk = nb - 1 - i

        @pl.when(i == 0)
        def _():
            cg[...] = jnp.zeros_like(cg)
            cdu[...] = jnp.zeros_like(cdu)
            for ref in (dcw_ref, dcb_ref, dwri_ref, dbr_ref, dbi_ref, dlam_ref):
                ref[...] = jnp.zeros_like(ref)

        u = u_ref[...]
        hv = h_ref[...]
        sp, dsp = _softplus_neg(lam_ref[...])
        r, ig, a, sq = _lru_gates(u, wri_ref, br_ref[...], bi_ref[...], sp)
        dy = dy_ref[...].astype(F32)
        g, dgelu = _gelu_and_grad(gate_ref[...])
        dxw_ref[:, :W] = (dy * hv * dgelu).astype(BF16)
        a_s[...] = a
        d_s[...] = dy * g
        rid = lax.broadcasted_iota(jnp.int32, (8, W), 0)

        def tile(t, c):
            r0 = pl.multiple_of((tm // 8 - 1 - t) * 8, 8)
            at, dt = a_s[pl.ds(r0, 8), :], d_s[pl.ds(r0, 8), :]
            out = jnp.zeros((8, W), F32)
            for j in range(7, -1, -1):
                gt = dt[j:j + 1, :] + c
                c = at[j:j + 1, :] * gt
                out = jnp.where(rid == j, gt, out)
            g_s[pl.ds(r0, 8), :] = out
            return c

        cg[0:1, :] = lax.fori_loop(0, tm // 8, tile, cg[0:1, :])
        gt = g_s[...]
        hprev8 = jnp.where(r_blk > 0, hp_ref[...], 0.0)
        da = gt * _shift_down(hv, hprev8, 1)
        iu = ig * u
        d_iu = gt * sq
        dla = da * a - (gt * iu) * (a * a) / sq
        dlam_ref[...] += jnp.sum(dla * r, axis=0, keepdims=True) * (-LRU_C) * dsp
        dr_pre = dla * (-LRU_C) * sp * r * (1.0 - r)
        di_pre = d_iu * u * ig * (1.0 - ig)
        dbr_ref[...] += jnp.sum(dr_pre, axis=0, keepdims=True)
        dbi_ref[...] += jnp.sum(di_pre, axis=0, keepdims=True)
        ub = u.astype(BF16)
        dus = []
        for hb in range(LRU_BLOCKS):
            sl = slice(hb * LRU_BLOCK_W, (hb + 1) * LRU_BLOCK_W)
            dri = jnp.concatenate([dr_pre[:, sl], di_pre[:, sl]], axis=1).astype(BF16)
            dus.append(lax.dot_general(dri, wri_ref[hb], C11, preferred_element_type=F32))
            dwri_ref[hb] += lax.dot_general(ub[:, sl], dri, C00, preferred_element_type=F32)
        du = d_iu * ig + jnp.concatenate(dus, axis=1)
        dcb_ref[...] += jnp.sum(du, axis=0, keepdims=True)
        up = up_ref[...]
        upprev8 = jnp.where(r_blk > 0, upp_ref[...], 0.0)
        dcw_ref[3:4, :] += jnp.sum(du * up, axis=0, keepdims=True)
        for d in range(1, LRU_CONV):
            dcw_ref[3 - d:4 - d, :] += jnp.sum(du * _shift_down(up, upprev8, d), axis=0, keepdims=True)
        next8 = cdu[...]
        dup = du * cw_ref[3:4, :]
        for d in range(1, LRU_CONV):
            dup = dup + _shift_up(du, next8, d) * cw_ref[3 - d:4 - d, :]
        cdu[...] = du[0:8]
        dxw_ref[:, W:] = dup.astype(BF16)

    rev = lambda c: (lambda i: (nb - 1 - i, c))
    halo = lambda c: (lambda i: (jnp.maximum((nb - 1 - i) * (tm // 8) - 1, 0), c))
    full = lambda shape: pl.BlockSpec(shape, lambda i: (0,) * len(shape))
    vec = full((1, W))
    return pl.pallas_call(
        body, name=name, grid=(nb,),
        in_specs=[pl.BlockSpec((tm, W), rev(0)), pl.BlockSpec((tm, W), rev(1)), pl.BlockSpec((8, W), halo(1)),
                  pl.BlockSpec((tm, W), rev(0)), pl.BlockSpec((tm, W), rev(0)), pl.BlockSpec((8, W), halo(0)),
                  pl.BlockSpec((tm, W), rev(0)), full((LRU_CONV, W)), full((LRU_BLOCKS, LRU_BLOCK_W, 2 * LRU_BLOCK_W)),
                  vec, vec, vec],
        out_specs=[pl.BlockSpec((tm, 2 * W), rev(0)), full((LRU_CONV, W)), vec,
                   full((LRU_BLOCKS, LRU_BLOCK_W, 2 * LRU_BLOCK_W)), vec, vec, vec],
        out_shape=[_sds((S, 2 * W), BF16), _sds((LRU_CONV, W), F32), _sds((1, W), F32),
                   _sds((LRU_BLOCKS, LRU_BLOCK_W, 2 * LRU_BLOCK_W), F32), _sds((1, W), F32), _sds((1, W), F32),
                   _sds((1, W), F32)],
        scratch_shapes=[pltpu.VMEM((8, W), F32), pltpu.VMEM((8, W), F32), pltpu.VMEM((tm, W), F32),
                        pltpu.VMEM((tm, W), F32), pltpu.VMEM((tm, W), F32)],
        compiler_params=_params("arbitrary"),
    )(xw, xw, xw, u, h, h, dy, cw, wri, br, bi, lam)


def _rms(x, g):
    r = lax.rsqrt(jnp.mean(x * x, axis=-1, keepdims=True) + RMS_EPS)
    return x * r * g, r


def _mla_pre(c, qg, kvg, cos, sin, *, name):
    S = c.shape[0]
    tm = min(ROW_T, S)
    q0, k0 = C_Q_RANK, C_Q_RANK + C_KV_RANK

    def body(c_ref, qg_ref, kvg_ref, cs_ref, sn_ref, cq_ref, ckv_ref, kr_ref):
        cq_ref[...] = _rms(c_ref[:, :q0], qg_ref[...])[0].astype(BF16)
        ckv_ref[...] = _rms(c_ref[:, q0:k0], kvg_ref[...])[0].astype(BF16)
        kr_ref[...] = _rope(c_ref[:, k0:], cs_ref[...], sn_ref[...]).astype(BF16)

    blk = lambda w: pl.BlockSpec((tm, w), lambda i: (i, 0))
    vec = lambda w: pl.BlockSpec((1, w), lambda i: (0, 0))
    return pl.pallas_call(
        body, name=name, grid=(S // tm,),
        in_specs=[blk(c.shape[1]), vec(C_Q_RANK), vec(C_KV_RANK), blk(LANES), blk(LANES)],
        out_specs=[blk(C_Q_RANK), blk(C_KV_RANK), blk(C_ROPE)],
        out_shape=[_sds((S, C_Q_RANK), BF16), _sds((S, C_KV_RANK), BF16), _sds((S, C_ROPE), BF16)],
        compiler_params=_params("parallel"),
    )(c, qg, kvg, cos, sin)


def _mla_post_bwd(c, dcq_a, dcq_b, dckv, dkr_h, qg, kvg, cos, sin, *, name):
    S = c.shape[0]
    tm = min(ROW_T, S)
    q0, k0 = C_Q_RANK, C_Q_RANK + C_KV_RANK

    def rms_bwd(x, g, dy):
        r = lax.rsqrt(jnp.mean(x * x, axis=-1, keepdims=True) + RMS_EPS)
        uu = dy * g
        dx = r * uu - x * (r * r * r) * jnp.mean(uu * x, axis=-1, keepdims=True)
        return dx, jnp.sum(dy * x * r, axis=0, keepdims=True)

    def body(c_ref, da_ref, db_ref, dkv_ref, dkr_ref, qg_ref, kvg_ref, cs_ref, sn_ref, dc_ref, dqg_ref, dkvg_ref):
        @pl.when(pl.program_id(0) == 0)
        def _():
            dqg_ref[...] = jnp.zeros_like(dqg_ref)
            dkvg_ref[...] = jnp.zeros_like(dkvg_ref)

        dx, dg = rms_bwd(c_ref[:, :q0], qg_ref[...], da_ref[...] + db_ref[...])
        dc_ref[:, :q0] = dx.astype(BF16)
        dqg_ref[...] += dg
        dx, dg = rms_bwd(c_ref[:, q0:k0], kvg_ref[...], dkv_ref[...])
        dc_ref[:, q0:k0] = dx.astype(BF16)
        dkvg_ref[...] += dg
        dkr = dkr_ref[0]
        for hh in range(1, C_HEADS):
            dkr = dkr + dkr_ref[hh]
        dc_ref[:, k0:] = _rope_t(dkr, cs_ref[...], sn_ref[...]).astype(BF16)

    blk = lambda w: pl.BlockSpec((tm, w), lambda i: (i, 0))
    vec = lambda w: pl.BlockSpec((1, w), lambda i: (0, 0))
    return pl.pallas_call(
        body, name=name, grid=(S // tm,),
        in_specs=[blk(c.shape[1]), blk(C_Q_RANK), blk(C_Q_RANK), blk(C_KV_RANK),
                  pl.BlockSpec((C_HEADS, tm, C_ROPE), lambda i: (0, i, 0)), vec(C_Q_RANK), vec(C_KV_RANK), blk(LANES), blk(LANES)],
        out_specs=[blk(c.shape[1]), vec(C_Q_RANK), vec(C_KV_RANK)],
        out_shape=[_sds(c.shape, BF16), _sds((1, C_Q_RANK), F32), _sds((1, C_KV_RANK), F32)],
        compiler_params=_params("arbitrary"),
    )(c, dcq_a, dcq_b, dckv, dkr_h, qg, kvg, cos, sin)


def _rope_heads(x, cos, sin, *, transpose, name):
    S, W = x.shape
    tm = min(ROW_T, S)
    fn = _rope_t if transpose else _rope

    def body(x_ref, c_ref, s_ref, o_ref):
        o_ref[...] = fn(x_ref[...].astype(F32), c_ref[...], s_ref[...]).astype(BF16)

    row = pl.BlockSpec((tm, W), lambda i: (i, 0))
    tab = pl.BlockSpec((tm, LANES), lambda i: (i, 0))
    return pl.pallas_call(body, name=name, grid=(S // tm,), in_specs=[row, tab, tab], out_specs=row,
                          out_shape=_sds((S, W), BF16), compiler_params=_params("parallel"))(x, cos, sin)


def _mla_scores(qn, qr, kn, kr, qi, kj, t):
    s = lax.dot_general(qn, kn, C11, preferred_element_type=F32) + lax.dot_general(qr, kr, C11, preferred_element_type=F32)
    s = s * ((C_NOPE + C_ROPE) ** -0.5)
    row = qi * t + lax.broadcasted_iota(jnp.int32, (t, t), 0)
    col = kj * t + lax.broadcasted_iota(jnp.int32, (t, t), 1)
    return jnp.where(col <= row, s, NEG)


def _mla_flash_fwd(qn, qr, kv, kr, *, name):
    S = qn.shape[0]
    H, t = C_HEADS, min(FLASH_T, S)
    n = S // t

    def body(qn_ref, qr_ref, kn_ref, v_ref, kr_ref, o_ref, lse_ref, m_s, l_s, acc):
        i, j = pl.program_id(1), pl.program_id(2)

        @pl.when(j == 0)
        def _():
            m_s[...] = jnp.full_like(m_s, NEG)
            l_s[...] = jnp.zeros_like(l_s)
            acc[...] = jnp.zeros_like(acc)

        @pl.when(j <= i)
        def _():
            s = _mla_scores(qn_ref[...], qr_ref[...], kn_ref[...], kr_ref[...], i, j, t)
            m_prev = m_s[...]
            m_new = jnp.maximum(m_prev, jnp.max(s, axis=-1, keepdims=True))
            corr = jnp.exp(m_prev - m_new)
            p = jnp.exp(s - m_new[:, 0:1])
            l_s[...] = corr * l_s[...] + jnp.sum(p, axis=-1, keepdims=True)
            acc[...] = corr * acc[...] + jnp.dot(p.astype(BF16), v_ref[...], preferred_element_type=F32)
            m_s[...] = m_new

        @pl.when(j == i)
        def _():
            o_ref[...] = (acc[...] / l_s[...]).astype(BF16)
            lse_ref[...] = m_s[...] + jnp.log(l_s[...])

    kvi = lambda h, i, j: jnp.minimum(j, i)
    return pl.pallas_call(
        body, name=name, grid=(H, n, n),
        in_specs=[pl.BlockSpec((t, C_NOPE), lambda h, i, j: (i, h)), pl.BlockSpec((None, t, C_ROPE), lambda h, i, j: (h, i, 0)),
                  pl.BlockSpec((t, C_NOPE), lambda h, i, j: (kvi(h, i, j), h)),
                  pl.BlockSpec((t, C_V), lambda h, i, j: (kvi(h, i, j), H + h)),
                  pl.BlockSpec((t, C_ROPE), lambda h, i, j: (kvi(h, i, j), 0))],
        out_specs=[pl.BlockSpec((t, C_V), lambda h, i, j: (i, h)), pl.BlockSpec((t, LANES), lambda h, i, j: (i, h))],
        out_shape=[_sds((S, H * C_V), BF16), _sds((S, H * LANES), F32)],
        scratch_shapes=[pltpu.VMEM((t, LANES), F32), pltpu.VMEM((t, LANES), F32), pltpu.VMEM((t, C_V), F32)],
        compiler_params=_params("parallel", "parallel", "arbitrary"),
    )(qn, qr, kv, kv, kr)


def _mla_delta(do, o, *, name):
    S, W = do.shape
    tm = min(ROW_T, S)

    def body(do_ref, o_ref, d_ref):
        for h in range(C_HEADS):
            sl = slice(h * C_V, (h + 1) * C_V)
            d = jnp.sum(do_ref[:, sl].astype(F32) * o_ref[:, sl].astype(F32), axis=-1, keepdims=True)
            d_ref[:, sl] = jnp.broadcast_to(d, (tm, C_V))

    row = pl.BlockSpec((tm, W), lambda i: (i, 0))
    return pl.pallas_call(body, name=name, grid=(S // tm,), in_specs=[row, row], out_specs=row,
                          out_shape=_sds((S, W), F32), compiler_params=_params("parallel"))(do, o)


def _mla_flash_dq(qn, qr, kv, kr, do, lse, delta, *, name):
    S = qn.shape[0]
    H, t = C_HEADS, min(FLASH_T, S)
    n = S // t
    scale = (C_NOPE + C_ROPE) ** -0.5

    def body(qn_ref, qr_ref, kn_ref, v_ref, kr_ref, do_ref, lse_ref, dl_ref, dqn_ref, dqr_ref, an, ar):
        i, j = pl.program_id(1), pl.program_id(2)

        @pl.when(j == 0)
        def _():
            an[...] = jnp.zeros_like(an)
            ar[...] = jnp.zeros_like(ar)

        @pl.when(j <= i)
        def _():
            s = _mla_scores(qn_ref[...], qr_ref[...], kn_ref[...], kr_ref[...], i, j, t)
            p = jnp.exp(s - lse_ref[:, 0:1])
            dp = lax.dot_general(do_ref[...], v_ref[...], C11, preferred_element_type=F32)
            ds = (p * (dp - dl_ref[:, 0:1]) * scale).astype(BF16)
            an[...] += jnp.dot(ds, kn_ref[...], preferred_element_type=F32)
            ar[...] += jnp.dot(ds, kr_ref[...], preferred_element_type=F32)

        @pl.when(j == i)
        def _():
            dqn_ref[...] = an[...].astype(BF16)
            dqr_ref[...] = ar[...]

    kvi = lambda h, i, j: jnp.minimum(j, i)
    qb = pl.BlockSpec((t, C_NOPE), lambda h, i, j: (i, h))
    return pl.pallas_call(
        body, name=name, grid=(H, n, n),
        in_specs=[qb, pl.BlockSpec((None, t, C_ROPE), lambda h, i, j: (h, i, 0)),
                  pl.BlockSpec((t, C_NOPE), lambda h, i, j: (kvi(h, i, j), h)),
                  pl.BlockSpec((t, C_V), lambda h, i, j: (kvi(h, i, j), H + h)),
                  pl.BlockSpec((t, C_ROPE), lambda h, i, j: (kvi(h, i, j), 0)), qb, qb, qb],
        out_specs=[qb, pl.BlockSpec((None, t, C_ROPE), lambda h, i, j: (h, i, 0))],
        out_shape=[_sds((S, H * C_NOPE), BF16), _sds((H, S, C_ROPE), F32)],
        scratch_shapes=[pltpu.VMEM((t, C_NOPE), F32), pltpu.VMEM((t, C_ROPE), F32)],
        compiler_params=_params("parallel", "parallel", "arbitrary"),
    )(qn, qr, kv, kv, kr, do, lse, delta)


def _mla_flash_dkv(qn, qr, kv, kr, do, lse, delta, *, name):
    S = qn.shape[0]
    H, t = C_HEADS, min(FLASH_T, S)
    n = S // t
    scale = (C_NOPE + C_ROPE) ** -0.5

    def body(qn_ref, qr_ref, kn_ref, v_ref, kr_ref, do_ref, lse_ref, dl_ref, dkn_ref, dv_ref, dkr_ref, akn, av, akr):
        j, i = pl.program_id(1), pl.program_id(2)

        @pl.when(i == 0)
        def _():
            akn[...] = jnp.zeros_like(akn)
            av[...] = jnp.zeros_like(av)
            akr[...] = jnp.zeros_like(akr)

        @pl.when(i >= j)
        def _():
            s = _mla_scores(qn_ref[...], qr_ref[...], kn_ref[...], kr_ref[...], i, j, t)
            p = jnp.exp(s - lse_ref[:, 0:1])
            dp = lax.dot_general(do_ref[...], v_ref[...], C11, preferred_element_type=F32)
            ds = (p * (dp - dl_ref[:, 0:1]) * scale).astype(BF16)
            av[...] += lax.dot_general(p.astype(BF16), do_ref[...], C00, preferred_element_type=F32)
            akn[...] += lax.dot_general(ds, qn_ref[...], C00, preferred_element_type=F32)
            akr[...] += lax.dot_general(ds, qr_ref[...], C00, preferred_element_type=F32)

        @pl.when(i == n - 1)
        def _():
            dkn_ref[...] = akn[...].astype(BF16)
            dv_ref[...] = av[...].astype(BF16)
            dkr_ref[...] = akr[...]

    qi = lambda h, j, i: jnp.maximum(i, j)
    qb = pl.BlockSpec((t, C_NOPE), lambda h, j, i: (qi(h, j, i), h))
    return pl.pallas_call(
        body, name=name, grid=(H, n, n),
        in_specs=[qb, pl.BlockSpec((None, t, C_ROPE), lambda h, j, i: (h, qi(h, j, i), 0)),
                  pl.BlockSpec((t, C_NOPE), lambda h, j, i: (j, h)), pl.BlockSpec((t, C_V), lambda h, j, i: (j, H + h)),
                  pl.BlockSpec((t, C_ROPE), lambda h, j, i: (j, 0)), qb, qb, qb],
        out_specs=[pl.BlockSpec((t, C_NOPE), lambda h, j, i: (j, h)), pl.BlockSpec((t, C_V), lambda h, j, i: (j, h)),
                   pl.BlockSpec((None, t, C_ROPE), lambda h, j, i: (h, j, 0))],
        out_shape=[_sds((S, H * C_NOPE), BF16), _sds((S, H * C_V), BF16), _sds((H, S, C_ROPE), F32)],
        scratch_shapes=[pltpu.VMEM((t, C_NOPE), F32), pltpu.VMEM((t, C_V), F32), pltpu.VMEM((t, C_ROPE), F32)],
        compiler_params=_params("parallel", "parallel", "arbitrary"),
    )(qn, qr, kv, kv, kr, do, lse, delta)


def _place():
    return lax.axis_index("x"), lax.axis_index("y"), lax.axis_index("c")


def _other_chips(x, y):
    return [(1 - x, y), (x, 1 - y), (1 - x, 1 - y)]


def _all_gather_chips(p, *, name):
    R, C = p.shape

    def body(p_ref, o_ref, send_sems, recv_sems, local_sem):
        x, y, c = _place()
        me = 2 * x + y
        local = pltpu.make_async_copy(p_ref, o_ref.at[me], local_sem)
        local.start()
        copies = [pltpu.make_async_remote_copy(src_ref=p_ref, dst_ref=o_ref.at[me], send_sem=send_sems.at[k],
                                               recv_sem=recv_sems.at[k], device_id=(px, py, c), device_id_type=MESH)
                  for k, (px, py) in enumerate(_other_chips(x, y))]
        for cp in copies:
            cp.start()
        for cp in copies:
            cp.wait()
        local.wait()

    any_spec = pl.BlockSpec(memory_space=pl.ANY)
    return pl.pallas_call(
        body, name=name, in_specs=[any_spec], out_specs=any_spec, out_shape=_sds((4, R, C), p.dtype),
        scratch_shapes=[pltpu.SemaphoreType.DMA((3,)), pltpu.SemaphoreType.DMA((3,)), pltpu.SemaphoreType.DMA(())],
    )(p)


def _scatter_chips(g4, *, name):
    _, R, C = g4.shape

    def body(g_ref, r_ref, send_sems, recv_sems):
        x, y, c = _place()
        copies = [pltpu.make_async_remote_copy(src_ref=g_ref.at[2 * px + py], dst_ref=r_ref.at[k], send_sem=send_sems.at[k],
                                               recv_sem=recv_sems.at[k], device_id=(px, py, c), device_id_type=MESH)
                  for k, (px, py) in enumerate(_other_chips(x, y))]
        for cp in copies:
            cp.start()
        for cp in copies:
            cp.wait()

    any_spec = pl.BlockSpec(memory_space=pl.ANY)
    return pl.pallas_call(
        body, name=name, in_specs=[any_spec], out_specs=any_spec, out_shape=_sds((3, R, C), g4.dtype),
        scratch_shapes=[pltpu.SemaphoreType.DMA((3,)), pltpu.SemaphoreType.DMA((3,))],
    )(g4)


def _sum_partials(g4, recv, chip, *, name):
    _, R, C = g4.shape
    tr = math.gcd(R, 512)

    def body(chip_ref, g_ref, r0_ref, r1_ref, r2_ref, o_ref):
        o_ref[...] = ((g_ref[...] + r0_ref[...]) + r1_ref[...]) + r2_ref[...]

    rspec = lambda k: pl.BlockSpec((None, tr, C), lambda i, s: (k, i, 0))
    return pl.pallas_call(
        body, name=name,
        grid_spec=pltpu.PrefetchScalarGridSpec(
            num_scalar_prefetch=1, grid=(R // tr,),
            in_specs=[pl.BlockSpec((None, tr, C), lambda i, s: (s[0], i, 0)), rspec(0), rspec(1), rspec(2)],
            out_specs=pl.BlockSpec((tr, C), lambda i, s: (i, 0))),
        out_shape=_sds((R, C), F32), compiler_params=_params("parallel"),
    )(chip, g4, recv, recv, recv)


def _swap_cores(p, *, name):
    def body(p_ref, o_ref, send_sem, recv_sem):
        x, y, c = _place()
        cp = pltpu.make_async_remote_copy(src_ref=p_ref, dst_ref=o_ref, send_sem=send_sem, recv_sem=recv_sem,
                                          device_id=(x, y, 1 - c), device_id_type=MESH)
        cp.start()
        cp.wait()

    any_spec = pl.BlockSpec(memory_space=pl.ANY)
    return pl.pallas_call(
        body, name=name, in_specs=[any_spec], out_specs=any_spec, out_shape=_sds(p.shape, p.dtype),
        scratch_shapes=[pltpu.SemaphoreType.DMA(()), pltpu.SemaphoreType.DMA(())],
    )(p)


def _all_reduce_small(v, *, name):
    r, C = v.shape

    def body(v_ref, o_ref, buf, send_sems, recv_sems):
        x, y, c = _place()
        me = 4 * x + 2 * y + c
        buf[me] = v_ref[...]
        peers = []
        for k in range(1, 8):
            kx, ky, kc = (k >> 2) & 1, (k >> 1) & 1, k & 1
            px = 1 - x if kx else x
            py = 1 - y if ky else y
            pc = 1 - c if kc else c
            peers.append((px, py, pc))
        copies = []
        for k, peer in enumerate(peers):
            cp = pltpu.make_async_remote_copy(src_ref=v_ref, dst_ref=buf.at[me], send_sem=send_sems.at[k],
                                              recv_sem=recv_sems.at[me], device_id=peer, device_id_type=MESH)
            cp.start()
            copies.append(cp)
        for k, (px, py, pc) in enumerate(peers):
            src = 4 * px + 2 * py + pc
            pltpu.make_async_remote_copy(src_ref=v_ref, dst_ref=buf.at[src], send_sem=send_sems.at[k],
                                         recv_sem=recv_sems.at[src], device_id=peers[k], device_id_type=MESH).wait_recv()
        for cp in copies:
            cp.wait_send()
        acc = buf[0]
        for d in range(1, 8):
            acc = acc + buf[d]
        o_ref[...] = acc

    vm = pl.BlockSpec(memory_space=pltpu.VMEM)
    return pl.pallas_call(
        body, name=name, in_specs=[vm], out_specs=vm, out_shape=_sds((r, C), F32),
        scratch_shapes=[pltpu.VMEM((8, r, C), F32), pltpu.SemaphoreType.DMA((7,)), pltpu.SemaphoreType.DMA((8,))],
    )(v)


def _adamw(w, m, v, ga, gb, *, name):
    R, C = w.shape
    tr = math.gcd(R, 512)
    has_b = gb is not None
    c1 = 1.0 / (1.0 - ADAM_B1 ** ADAM_STEP)
    c2 = 1.0 / (1.0 - ADAM_B2 ** ADAM_STEP)

    def body(*refs):
        if has_b:
            w_ref, m_ref, v_ref, ga_ref, gb_ref, g_ref, d_ref, nm_ref, nv_ref = refs
            g = ga_ref[...] + gb_ref[...]
        else:
            w_ref, m_ref, v_ref, ga_ref, g_ref, d_ref, nm_ref, nv_ref = refs
            g = ga_ref[...]
        nm = ADAM_B1 * m_ref[...] + (1.0 - ADAM_B1) * g
        nv = ADAM_B2 * v_ref[...] + (1.0 - ADAM_B2) * (g * g)
        g_ref[...] = g
        nm_ref[...] = nm
        nv_ref[...] = nv
        d_ref[...] = -ADAM_LR * ((nm * c1) / (jnp.sqrt(nv * c2) + ADAM_EPS) + ADAM_WD * w_ref[...])

    blk = pl.BlockSpec((tr, C), lambda i: (i, 0))
    n_in = 5 if has_b else 4
    args = (w, m, v, ga) + ((gb,) if has_b else ())
    return pl.pallas_call(body, name=name, grid=(R // tr,), in_specs=[blk] * n_in, out_specs=[blk] * 4,
                          out_shape=[_sds((R, C), F32)] * 4, compiler_params=_params("parallel"))(*args)


def _seg_rows(n, cols):
    return -(-n // (16 * cols)) * 16


def _pack(arrays, dtype, cols=PACK_COLS, row_mult=512):
    parts, rows = [], 0
    for a in arrays:
        n = int(np.prod(a.shape))
        r = _seg_rows(n, cols)
        flat = a.reshape(-1).astype(dtype)
        if r * cols != n:
            flat = jnp.pad(flat, (0, r * cols - n))
        parts.append(flat.reshape(r, cols))
        rows += r
    pad = -rows % row_mult
    if pad:
        parts.append(jnp.zeros((pad, cols), dtype))
    return jnp.concatenate(parts, axis=0)


def _unpack(packed, shapes, cols=PACK_COLS):
    out, r0 = [], 0
    for shp in shapes:
        n = int(np.prod(shp))
        used = -(-n // cols)
        out.append(packed[r0:r0 + used].reshape(-1)[:n].reshape(shp))
        r0 += _seg_rows(n, cols)
    return out


def _rope_tables(seq):
    inv = 1.0 / (ROPE_THETA ** (jnp.arange(0, 64, 2, dtype=F32) / 64))
    ang = jnp.arange(seq, dtype=F32)[:, None] * inv[None, :]
    cos, sin = jnp.cos(ang), jnp.sin(ang)
    cos128 = jnp.concatenate([cos, cos, cos, cos], axis=1)
    sin128 = jnp.concatenate([-sin, sin, -sin, sin], axis=1)
    return cos128, sin128


def _ffn_perm(a):
    lead = a.shape[:-1]
    nj = D_FF // FFN_TC
    return jnp.swapaxes(a.reshape(lead + (2, nj, FFN_TC)), -3, -2).reshape(lead + (2 * D_FF,))


def _ffn_unperm(a):
    lead = a.shape[:-1]
    nj = D_FF // FFN_TC
    return jnp.swapaxes(a.reshape(lead + (nj, 2, FFN_TC)), -3, -2).reshape(lead + (2 * D_FF,))


def _mixer_a_fwd(xb, w, j, cos, sin, tag):
    qkv = _mm(xb, w["a_w_qkv"][j], name=f"mm_qkv_{tag}")
    qkv_r = _rope_cols(qkv, cos, sin, (A_HEADS + A_KV_HEADS) * A_HEAD_DIM, name=f"rope_qkv_{tag}")
    o = _swa_fwd(qkv_r, w["a_sinks"][j], name=f"swa_fwd_{tag}")
    y = _mm(o, w["a_w_o"][j], name=f"mm_ao_{tag}")
    return y, (xb, qkv_r, o)


def _mixer_a_bwd(dzb, res, w, j, cos, sin, tag, grads):
    xb, qkv_r, o = res
    do = _mm(dzb, w["a_w_o"][j], tb=True, out_dtype=BF16, name=f"mm_dao_{tag}")
    grads["a_w_o"][j] = _mm(o, dzb, ta=True, tk=GRAD_TK, name=f"mm_gao_{tag}")
    dq, dcur, dprev, dsink = _swa_bwd(qkv_r, w["a_sinks"][j], do, cos, sin, name=f"swa_bwd_{tag}")
    grads["a_sinks"][j] = dsink[0, :A_HEADS]
    dqkv = _swa_dqkv(dq, dcur, dprev, cos, sin, name=f"swa_dqkv_{tag}")
    grads["a_w_qkv"][j] = _mm(xb, dqkv, ta=True, tk=GRAD_TK, name=f"mm_gqkv_{tag}")
    return _mm(dqkv, w["a_w_qkv"][j], tb=True, name=f"mm_dxa_{tag}")


def _mixer_b_fwd(xb, w, j, tag):
    xw = _mm(xb, w["b_w_in"][j], name=f"mm_bin_{tag}")
    wri = jnp.concatenate([w["b_w_rgate"][j], w["b_w_igate"][j]], axis=-1)
    y, u, h = _lru_fwd(xw, w["b_conv_w"][j], w["b_conv_b"][j][None], wri, w["b_b_rgate"][j][None],
                       w["b_b_igate"][j][None], w["b_lambda"][j][None], name=f"lru_fwd_{tag}")
    out = _mm(y, w["b_w_o"][j], name=f"mm_bo_{tag}")
    return out, (xb, xw, wri, u, h, y)


def _mixer_b_bwd(dzb, res, w, j, tag, grads):
    xb, xw, wri, u, h, y = res
    dy = _mm(dzb, w["b_w_o"][j], tb=True, out_dtype=BF16, name=f"mm_dbo_{tag}")
    grads["b_w_o"][j] = _mm(y, dzb, ta=True, tk=GRAD_TK, name=f"mm_gbo_{tag}")
    dxw, dcw, dcb, dwri, dbr, dbi, dlam = _lru_bwd(
        xw, u, h, dy, w["b_conv_w"][j], wri, w["b_b_rgate"][j][None], w["b_b_igate"][j][None], w["b_lambda"][j][None],
        name=f"lru_bwd_{tag}")
    grads["b_conv_w"][j], grads["b_conv_b"][j] = dcw, dcb[0]
    grads["b_w_rgate"][j], grads["b_w_igate"][j] = dwri[..., :LRU_BLOCK_W], dwri[..., LRU_BLOCK_W:]
    grads["b_b_rgate"][j], grads["b_b_igate"][j], grads["b_lambda"][j] = dbr[0], dbi[0], dlam[0]
    grads["b_w_in"][j] = _mm(xb, dxw, ta=True, tk=GRAD_TK, name=f"mm_gbin_{tag}")
    return _mm(dxw, w["b_w_in"][j], tb=True, name=f"mm_dxb_{tag}")


def _mla_weights(w, j):
    H = C_HEADS
    uq = w["c_w_uq"][j].reshape(C_Q_RANK, H, C_NOPE + C_ROPE)
    ukv = w["c_w_ukv"][j].reshape(C_KV_RANK, H, C_NOPE + C_V)
    uq_n = uq[:, :, :C_NOPE].reshape(C_Q_RANK, H * C_NOPE)
    uq_r = uq[:, :, C_NOPE:].reshape(C_Q_RANK, H * C_ROPE)
    ukv_p = jnp.concatenate([ukv[:, :, :C_NOPE].reshape(C_KV_RANK, H * C_NOPE),
                             ukv[:, :, C_NOPE:].reshape(C_KV_RANK, H * C_V)], axis=1)
    return uq_n, uq_r, ukv_p


def _mixer_c_fwd(xb, w, j, cos, sin, tag):
    S = xb.shape[0]
    H = C_HEADS
    uq_n, uq_r, ukv_p = _mla_weights(w, j)
    c = _mm(xb, w["c_w_down"][j], name=f"mm_cdown_{tag}")
    cq, ckv, kr = _mla_pre(c, w["c_q_norm"][j][None], w["c_kv_norm"][j][None], cos, sin, name=f"mla_pre_{tag}")
    qn = _mm(cq, uq_n, out_dtype=BF16, name=f"mm_uqn_{tag}")
    qr_flat = _rope_heads(_mm(cq, uq_r, name=f"mm_uqr_{tag}"), cos, sin, transpose=False, name=f"rope_qr_{tag}")
    qr = jnp.transpose(qr_flat.reshape(S, H, C_ROPE), (1, 0, 2))
    kv = _mm(ckv, ukv_p, out_dtype=BF16, name=f"mm_ukv_{tag}")
    o, lse = _mla_flash_fwd(qn, qr, kv, kr, name=f"mla_fwd_{tag}")
    y = _mm(o, w["c_w_o"][j], name=f"mm_co_{tag}")
    return y, (xb, c, cq, ckv, kr, qn, qr, kv, o, lse, uq_n, uq_r, ukv_p)


def _mixer_c_bwd(dzb, res, w, j, cos, sin, tag, grads):
    xb, c, cq, ckv, kr, qn, qr, kv, o, lse, uq_n, uq_r, ukv_p = res
    S = xb.shape[0]
    H = C_HEADS
    do = _mm(dzb, w["c_w_o"][j], tb=True, out_dtype=BF16, name=f"mm_dco_{tag}")
    grads["c_w_o"][j] = _mm(o, dzb, ta=True, tk=GRAD_TK, name=f"mm_gco_{tag}")
    delta = _mla_delta(do, o, name=f"mla_delta_{tag}")
    dqn, dqr = _mla_flash_dq(qn, qr, kv, kr, do, lse, delta, name=f"mla_dq_{tag}")
    dkn, dv, dkr_h = _mla_flash_dkv(qn, qr, kv, kr, do, lse, delta, name=f"mla_dkv_{tag}")
    dkv = jnp.concatenate([dkn, dv], axis=1)
    dqr_flat = _rope_heads(jnp.transpose(dqr, (1, 0, 2)).reshape(S, H * C_ROPE), cos, sin, transpose=True, name=f"rope_dqr_{tag}")
    g_uq_n = _mm(cq, dqn, ta=True, tk=GRAD_TK, name=f"mm_guqn_{tag}")
    g_uq_r = _mm(cq, dqr_flat, ta=True, tk=GRAD_TK, name=f"mm_guqr_{tag}")
    g_ukv = _mm(ckv, dkv, ta=True, tk=GRAD_TK, name=f"mm_gukv_{tag}")
    grads["c_w_uq"][j] = jnp.concatenate([g_uq_n.reshape(C_Q_RANK, H, C_NOPE), g_uq_r.reshape(C_Q_RANK, H, C_ROPE)],
                                         axis=2).reshape(C_Q_RANK, H * (C_NOPE + C_ROPE))
    grads["c_w_ukv"][j] = jnp.concatenate([g_ukv[:, :H * C_NOPE].reshape(C_KV_RANK, H, C_NOPE),
                                           g_ukv[:, H * C_NOPE:].reshape(C_KV_RANK, H, C_V)],
                                          axis=2).reshape(C_KV_RANK, H * (C_NOPE + C_V))
    dcq_a = _mm(dqn, uq_n, tb=True, name=f"mm_dcqa_{tag}")
    dcq_b = _mm(dqr_flat, uq_r, tb=True, name=f"mm_dcqb_{tag}")
    dckv = _mm(dkv, ukv_p, tb=True, name=f"mm_dckv_{tag}")
    dc, dqg, dkvg = _mla_post_bwd(c, dcq_a, dcq_b, dckv, dkr_h, w["c_q_norm"][j][None], w["c_kv_norm"][j][None], cos, sin,
                                  name=f"mla_post_{tag}")
    grads["c_q_norm"][j], grads["c_kv_norm"][j] = dqg[0], dkvg[0]
    grads["c_w_down"][j] = _mm(xb, dc, ta=True, tk=GRAD_TK, name=f"mm_gcdown_{tag}")
    return _mm(dc, w["c_w_down"][j], tb=True, name=f"mm_dxc_{tag}")


def _local_step(x, mem, target, w):
    S = x.shape[0]
    cos, sin = _rope_tables(S)
    grads = {n: [None] * w[n].shape[0] for n in WEIGHTS if n != "mem_w_kv"}
    mkv = _mm(mem, w["mem_w_kv"], out_dtype=BF16, tm=MEM_LEN, name="mm_memkv")

    xs, xb = x, x.astype(BF16)
    saved = []
    for i in range(DEPTH):
        kind, j = i % N_MIXERS, i // N_MIXERS
        tag = f"l{i}"
        if kind == 0:
            y, res = _mixer_a_fwd(xb, w, j, cos, sin, tag)
        elif kind == 1:
            y, res = _mixer_b_fwd(xb, w, j, tag)
        else:
            y, res = _mixer_c_fwd(xb, w, j, cos, sin, tag)
        x1, x1b, xh1, rs1 = _ln_fwd(xs, y, w["ln_g"][i, 0][None], w["ln_b"][i, 0][None], name=f"ln1_{tag}")
        q = _mm(x1b, w["x_w_q"][i], out_dtype=BF16, name=f"mm_xq_{tag}")
        o = _xattn_fwd(q, mkv, name=f"xattn_fwd_{tag}")
        y2 = _mm(o, w["x_w_o"][i], name=f"mm_xo_{tag}")
        x2, x2b, xh2, rs2 = _ln_fwd(x1, y2, w["ln_g"][i, 1][None], w["ln_b"][i, 1][None], name=f"ln2_{tag}")
        w_up = _ffn_perm(w["f_w_up"][i])
        cwp, cbp = _ffn_perm(w["f_conv_w"][i]), _ffn_perm(w["f_conv_b"][i][None])
        hh = _mm(x2b, w_up, out_dtype=BF16, tn=FFN_TC, name=f"mm_up_{tag}")
        a = _ffn_act_fwd(hh, cwp, cbp, name=f"ffn_act_{tag}")
        y3 = _mm(a, w["f_w_down"][i], name=f"mm_down_{tag}")
        x3, x3b, xh3, rs3 = _ln_fwd(x2, y3, w["ln_g"][i, 2][None], w["ln_b"][i, 2][None], name=f"ln3_{tag}")
        saved.append((res, (xh1, rs1, x1b), (q, o, xh2, rs2, x2b), (w_up, cwp, cbp, hh, a, xh3, rs3)))
        xs, xb = x3, x3b

    d2, loss = _loss_fwd(xs, target, name="loss")
    d1 = None

    dmkv = None
    ln_dg = [[None] * 3 for _ in range(DEPTH)]
    ln_db = [[None] * 3 for _ in range(DEPTH)]
    for i in reversed(range(DEPTH)):
        kind, j = i % N_MIXERS, i // N_MIXERS
        tag = f"l{i}"
        res, (xh1, rs1, x1b), (q, o, xh2, rs2, x2b), (w_up, cwp, cbp, hh, a, xh3, rs3) = saved[i]
        dz3, dz3b, ln_dg[i][2], ln_db[i][2] = _ln_bwd(d1, d2, xh3, rs3, w["ln_g"][i, 2][None], name=f"ln3_bwd_{tag}")
        da = _mm(dz3b, w["f_w_down"][i], tb=True, out_dtype=BF16, name=f"mm_ddown_{tag}")
        grads["f_w_down"][i] = _mm(a, dz3b, ta=True, tm=FFN_TC, tk=GRAD_TK, name=f"mm_gdown_{tag}")
        dh, dcw, dcb = _ffn_act_bwd(hh, da, cwp, cbp, name=f"ffn_act_bwd_{tag}")
        grads["f_conv_w"][i], grads["f_conv_b"][i] = _ffn_unperm(dcw), _ffn_unperm(dcb)[0]
        grads["f_w_up"][i] = _ffn_unperm(_mm(x2b, dh, ta=True, tn=FFN_TC, tk=GRAD_TK, name=f"mm_gup_{tag}"))
        dx2 = _mm(dh, w_up, tb=True, tm=512, name=f"mm_dxf_{tag}")

        dz2, dz2b, ln_dg[i][1], ln_db[i][1] = _ln_bwd(dz3, dx2, xh2, rs2, w["ln_g"][i, 1][None], name=f"ln2_bwd_{tag}")
        do = _mm(dz2b, w["x_w_o"][i], tb=True, out_dtype=BF16, name=f"mm_dxo_{tag}")
        grads["x_w_o"][i] = _mm(o, dz2b, ta=True, tk=GRAD_TK, name=f"mm_gxo_{tag}")
        dq, dmkv_i = _xattn_bwd(q, mkv, do, name=f"xattn_bwd_{tag}")
        dmkv = dmkv_i if dmkv is None else dmkv + dmkv_i
        grads["x_w_q"][i] = _mm(x1b, dq, ta=True, tk=GRAD_TK, name=f"mm_gxq_{tag}")
        dx1 = _mm(dq, w["x_w_q"][i], tb=True, name=f"mm_dxq_{tag}")

        dz1, dz1b, ln_dg[i][0], ln_db[i][0] = _ln_bwd(dz2, dx1, xh1, rs1, w["ln_g"][i, 0][None], name=f"ln1_bwd_{tag}")
        if kind == 0:
            dx0 = _mixer_a_bwd(dz1b, res, w, j, cos, sin, tag, grads)
        elif kind == 1:
            dx0 = _mixer_b_bwd(dz1b, res, w, j, tag, grads)
        else:
            dx0 = _mixer_c_bwd(dz1b, res, w, j, cos, sin, tag, grads)
        d1, d2 = dz1, dx0

    grad_x = _axpy(d1, d2, name="grad_x")
    out = {n: jnp.stack(g, axis=0) for n, g in grads.items() if n not in ("ln_g", "ln_b")}
    out["mem_w_kv"] = _mm(mem, dmkv, ta=True, tm=512, name="mm_gmemkv")
    out["ln_g"] = jnp.stack([jnp.concatenate(r, axis=0) for r in ln_dg], axis=0)
    out["ln_b"] = jnp.stack([jnp.concatenate(r, axis=0) for r in ln_db], axis=0)
    return loss, grad_x, out


def kernel(x, mem, a_w_qkv, a_sinks, a_w_o, b_w_in, b_conv_w, b_conv_b, b_w_rgate, b_b_rgate, b_w_igate, b_b_igate, b_lambda, b_w_o, c_w_down, c_q_norm, c_kv_norm, c_w_uq, c_w_ukv, c_w_o, mem_w_kv, x_w_q, x_w_o, f_w_up, f_conv_w, f_conv_b, f_w_down, ln_g, ln_b, loss_target, m_a_w_qkv, m_a_sinks, m_a_w_o, m_b_w_in, m_b_conv_w, m_b_conv_b, m_b_w_rgate, m_b_b_rgate, m_b_w_igate, m_b_b_igate, m_b_lambda, m_b_w_o, m_c_w_down, m_c_q_norm, m_c_kv_norm, m_c_w_uq, m_c_w_ukv, m_c_w_o, m_mem_w_kv, m_x_w_q, m_x_w_o, m_f_w_up, m_f_conv_w, m_f_conv_b, m_f_w_down, m_ln_g, m_ln_b, v_a_w_qkv, v_a_sinks, v_a_w_o, v_b_w_in, v_b_conv_w, v_b_conv_b, v_b_w_rgate, v_b_b_rgate, v_b_w_igate, v_b_b_igate, v_b_lambda, v_b_w_o, v_c_w_down, v_c_q_norm, v_c_kv_norm, v_c_w_uq, v_c_w_ukv, v_c_w_o, v_mem_w_kv, v_x_w_q, v_x_w_o, v_f_w_up, v_f_conv_w, v_f_conv_b, v_f_w_down, v_ln_g, v_ln_b):
    loc = locals()
    shard = {n: loc[n] for n in WEIGHTS}
    mom = {n: loc["m_" + n] for n in WEIGHTS}
    var = {n: loc["v_" + n] for n in WEIGHTS}
    names = [n for n, _ in SHARDED]
    axis = dict(SHARDED)
    big, small = names[:N_BIG], names[N_BIG:]

    def gather(group, dtype, tag):
        got = _all_gather_chips(_pack([shard[n] for n in group], dtype), name=f"gather_{tag}")
        per_chip = [_unpack(got[s], [shard[n].shape for n in group]) for s in range(4)]
        return {n: jnp.concatenate([per_chip[s][k] for s in range(4)], axis=axis[n]) for k, n in enumerate(group)}

    w = {**gather(big, BF16, "big"), **gather(small, F32, "small")}
    for n in REPLICATED:
        w[n] = shard[n]

    loss, grad_x, g = _local_step(x[0], mem[0], loss_target[0], w)

    g4 = jnp.stack([_pack([jnp.split(g[n], 4, axis=axis[n])[s] for n in names], F32) for s in range(4)], axis=0)
    recv = _scatter_chips(g4, name="scatter_grads")
    chip = (2 * lax.axis_index("x") + lax.axis_index("y")).astype(jnp.int32).reshape(1)
    part = _sum_partials(g4, recv, chip, name="sum_partials")
    part_sib = _swap_cores(part, name="swap_cores")
    shapes = [shard[n].shape for n in names]
    res = _adamw(_pack([shard[n] for n in names], F32), _pack([mom[n] for n in names], F32),
                 _pack([var[n] for n in names], F32), part, part_sib, name="adamw_sharded")
    grad_o, delta_o, m_o, v_o = [dict(zip(names, _unpack(r, shapes))) for r in res]

    rshapes = [shard[n].shape for n in REPLICATED]
    vec = _pack([g[n] for n in REPLICATED] + [loss], F32, cols=LANES, row_mult=8)
    tot = _all_reduce_small(vec, name="allreduce_small")
    loss_tot = _unpack(tot, rshapes + [(1, 1)], cols=LANES)[-1].reshape(())
    rpack = lambda d: _pack([d[n] for n in REPLICATED] + [jnp.zeros((1, 1), F32)], F32, cols=LANES, row_mult=8)
    res = _adamw(rpack(shard), rpack(mom), rpack(var), tot, None, name="adamw_replicated")
    for d, r in zip((grad_o, delta_o, m_o, v_o), res):
        d.update(dict(zip(REPLICATED, _unpack(r, rshapes, cols=LANES))))

    return (loss_tot, grad_x[None], *[grad_o[n] for n in WEIGHTS], *[delta_o[n] for n in WEIGHTS],
            *[m_o[n] for n in WEIGHTS], *[v_o[n] for n in WEIGHTS])
```

```python
import functools
import math

import numpy as np
import jax
import jax.numpy as jnp
from jax import lax
from jax.experimental import pallas as pl
from jax.experimental.pallas import tpu as pltpu

F32 = jnp.float32
BF16 = jnp.bfloat16
MESH = pl.DeviceIdType.MESH

D_MODEL = 1024
DEPTH = 4
N_MIXERS = 3
MEM_LEN = 256
BLOCK = 128
ROPE_THETA = 10000.0
NEG = -1e30
LN_EPS = 1e-5
RMS_EPS = 1e-6
A_HEADS, A_KV_HEADS, A_HEAD_DIM = 16, 4, 64
LRU_BLOCKS, LRU_BLOCK_W, LRU_CONV, LRU_C = 4, 256, 4, 8.0
C_HEADS, C_NOPE, C_ROPE, C_V, C_Q_RANK, C_KV_RANK = 8, 128, 64, 128, 384, 256
X_HEADS, X_HEAD_DIM = 4, 256
D_FF, FFN_CONV = 2816, 3
ALPHA = (2.0 * DEPTH) ** 0.25
ADAM_LR, ADAM_B1, ADAM_B2, ADAM_EPS, ADAM_WD, ADAM_STEP = 0.001, 0.9, 0.999, 1e-08, 0.01, 10

VMEM_LIMIT = 56 * 2 ** 20
LANES = 128
PACK_COLS = 1024
ROW_T = 512
ACT_T = 256
LRU_T = 256
FLASH_T = 512
FFN_TC = 1408
MM_T = 1024
GRAD_TK = 1024

C11 = (((1,), (1,)), ((), ()))
C00 = (((0,), (0,)), ((), ()))

SHARDED = [
    ("a_w_qkv", 2), ("a_w_o", 1), ("b_w_in", 2), ("b_w_rgate", 2), ("b_w_igate", 2), ("b_w_o", 1), ("c_w_down", 1),
    ("c_w_uq", 2), ("c_w_ukv", 2), ("c_w_o", 1), ("mem_w_kv", 1), ("x_w_q", 1), ("x_w_o", 1), ("f_w_up", 2),
    ("f_w_down", 1),
    ("b_conv_w", 2), ("c_q_norm", 1), ("c_kv_norm", 1), ("f_conv_w", 2), ("ln_g", 2), ("ln_b", 2),
]
N_BIG = 15
REPLICATED = ["a_sinks", "b_conv_b", "b_b_rgate", "b_b_igate", "b_lambda", "f_conv_b"]
WEIGHTS = ["a_w_qkv", "a_sinks", "a_w_o", "b_w_in", "b_conv_w", "b_conv_b", "b_w_rgate", "b_b_rgate", "b_w_igate",
           "b_b_igate", "b_lambda", "b_w_o", "c_w_down", "c_q_norm", "c_kv_norm", "c_w_uq", "c_w_ukv", "c_w_o",
           "mem_w_kv", "x_w_q", "x_w_o", "f_w_up", "f_conv_w", "f_conv_b", "f_w_down", "ln_g", "ln_b"]


def _params(*sem):
    return pltpu.CompilerParams(dimension_semantics=sem, vmem_limit_bytes=VMEM_LIMIT)


def _sds(shape, dtype):
    return jax.ShapeDtypeStruct(tuple(shape), dtype)


def _mm(a, b, *, name, ta=False, tb=False, out_dtype=F32, tm=None, tn=None, tk=None):
    (K, M) = a.shape if ta else a.shape[::-1]
    (N, K2) = b.shape if tb else b.shape[::-1]
    assert K == K2, (a.shape, b.shape, ta, tb)
    tm = min(tm or MM_T, M)
    tn = min(tn or N, N)
    tk = min(tk or K, K)
    assert M % tm == 0 and N % tn == 0 and K % tk == 0, (M, N, K, tm, tn, tk)
    nk = K // tk
    use_acc = nk > 1 and out_dtype != F32
    dims = (((0 if ta else 1,), (1 if tb else 0,)), ((), ()))

    def body(a_ref, b_ref, o_ref, *scratch):
        p = lax.dot_general(a_ref[...].astype(BF16), b_ref[...].astype(BF16), dims, preferred_element_type=F32)
        if nk == 1:
            o_ref[...] = p.astype(out_dtype)
        else:
            acc = scratch[0] if use_acc else o_ref
            k = pl.program_id(2)

            @pl.when(k == 0)
            def _():
                acc[...] = p

            @pl.when(k > 0)
            def _():
                acc[...] += p

            if use_acc:
                @pl.when(k == nk - 1)
                def _():
                    o_ref[...] = acc[...].astype(out_dtype)

    a_spec = pl.BlockSpec((tk, tm), lambda i, j, k: (k, i)) if ta else pl.BlockSpec((tm, tk), lambda i, j, k: (i, k))
    b_spec = pl.BlockSpec((tn, tk), lambda i, j, k: (j, k)) if tb else pl.BlockSpec((tk, tn), lambda i, j, k: (k, j))
    return pl.pallas_call(
        body, name=name, grid=(M // tm, N // tn, nk), in_specs=[a_spec, b_spec],
        out_specs=pl.BlockSpec((tm, tn), lambda i, j, k: (i, j)), out_shape=_sds((M, N), out_dtype),
        scratch_shapes=[pltpu.VMEM((tm, tn), F32)] if use_acc else [],
        compiler_params=_params("parallel", "parallel", "arbitrary"),
    )(a, b)


def _shift_down(cur, prev8, d):
    rolled = pltpu.roll(cur, d, 0)
    rid = lax.broadcasted_iota(jnp.int32, prev8.shape, 0)
    head = jnp.where(rid < d, pltpu.roll(prev8, d, 0), rolled[0:8])
    return jnp.concatenate([head, rolled[8:]], axis=0)


def _shift_up(cur, next8, d):
    n = cur.shape[0]
    rolled = pltpu.roll(cur, n - d, 0)
    rid = lax.broadcasted_iota(jnp.int32, next8.shape, 0)
    tail = jnp.where(rid >= 8 - d, pltpu.roll(next8, 8 - d, 0), rolled[n - 8:n])
    return jnp.concatenate([rolled[0:n - 8], tail], axis=0)


def _swap_halves(x):
    w = x.shape[-1]
    if w == 64:
        return jnp.concatenate([x[:, 32:64], x[:, 0:32]], axis=1)
    lane = lax.broadcasted_iota(jnp.int32, x.shape, 1)
    return jnp.where((lane % 64) < 32, pltpu.roll(x, w - 32, 1), pltpu.roll(x, 32, 1))


def _tile_lanes(t, w):
    return t if w == t.shape[-1] else jnp.concatenate([t] * (w // t.shape[-1]), axis=1)


def _rope(x, cos, sin):
    w = x.shape[-1]
    if w == 64:
        cos, sin = cos[:, :64], sin[:, :64]
    else:
        cos, sin = _tile_lanes(cos, w), _tile_lanes(sin, w)
    return x * cos + _swap_halves(x) * sin


def _rope_t(x, cos, sin):
    w = x.shape[-1]
    if w == 64:
        cos, sin = cos[:, :64], sin[:, :64]
    else:
        cos, sin = _tile_lanes(cos, w), _tile_lanes(sin, w)
    return x * cos - _swap_halves(x) * sin


def _sigmoid(x):
    return 1.0 / (1.0 + jnp.exp(-x))


def _gelu_and_grad(x):
    c0, c1 = math.sqrt(2.0 / math.pi), 0.044715
    t = jnp.tanh(c0 * (x + c1 * x * x * x))
    g = 0.5 * x * (1.0 + t)
    dg = 0.5 * (1.0 + t) + 0.5 * x * (1.0 - t * t) * c0 * (1.0 + 3.0 * c1 * x * x)
    return g, dg


def _neg_expm1(x):
    series = -x * (1.0 + x * (0.5 + x * (1.0 / 6.0 + x * (1.0 / 24.0 + x * (1.0 / 120.0)))))
    return jnp.where(x > -0.1, series, 1.0 - jnp.exp(x))


def _softplus_neg(lam):
    z = -lam
    e = jnp.exp(-jnp.abs(z))
    log1p = jnp.where(e < 0.01, e * (1.0 - e * (0.5 - e * (1.0 / 3.0))), jnp.log(1.0 + e))
    sp = jnp.maximum(z, 0.0) + log1p
    dsp = -_sigmoid(z)
    return sp, dsp


def _ln_fwd(x, y, g, b, *, name):
    S, D = x.shape
    tm = min(ROW_T, S)

    def body(x_ref, y_ref, g_ref, b_ref, o_ref, ob_ref, xh_ref, rs_ref):
        z = ALPHA * x_ref[...] + y_ref[...]
        mu = jnp.mean(z, axis=-1, keepdims=True)
        zc = z - mu
        var = jnp.mean(zc * zc, axis=-1, keepdims=True)
        r = lax.rsqrt(var + LN_EPS)
        xh = zc * r
        o = xh * g_ref[...] + b_ref[...]
        o_ref[...] = o
        ob_ref[...] = o.astype(BF16)
        xh_ref[...] = xh
        rs_ref[...] = r

    row = pl.BlockSpec((tm, D), lambda i: (i, 0))
    vec = pl.BlockSpec((1, D), lambda i: (0, 0))
    return pl.pallas_call(
        body, name=name, grid=(S // tm,), in_specs=[row, row, vec, vec],
        out_specs=[row, row, row, pl.BlockSpec((tm, 1), lambda i: (i, 0))],
        out_shape=[_sds((S, D), F32), _sds((S, D), BF16), _sds((S, D), F32), _sds((S, 1), F32)],
        compiler_params=_params("parallel"),
    )(x, y, g, b)


def _ln_bwd(d1, d2, xh, rs, g, *, name):
    S, D = xh.shape
    tm = min(ROW_T, S)
    has_d1 = d1 is not None

    def body(*refs):
        if has_d1:
            d1_ref, d2_ref, xh_ref, rs_ref, g_ref, dz_ref, dzb_ref, dg_ref, db_ref = refs
            dout = ALPHA * d1_ref[...] + d2_ref[...]
        else:
            d2_ref, xh_ref, rs_ref, g_ref, dz_ref, dzb_ref, dg_ref, db_ref = refs
            dout = d2_ref[...]
        xh_v = xh_ref[...]
        dxh = dout * g_ref[...]
        m1 = jnp.mean(dxh, axis=-1, keepdims=True)
        m2 = jnp.mean(dxh * xh_v, axis=-1, keepdims=True)
        dz = rs_ref[...] * (dxh - m1 - xh_v * m2)
        dz_ref[...] = dz
        dzb_ref[...] = dz.astype(BF16)

        @pl.when(pl.program_id(0) == 0)
        def _():
            dg_ref[...] = jnp.zeros_like(dg_ref)
            db_ref[...] = jnp.zeros_like(db_ref)

        dg_ref[...] += jnp.sum(dout * xh_v, axis=0, keepdims=True)
        db_ref[...] += jnp.sum(dout, axis=0, keepdims=True)

    row = pl.BlockSpec((tm, D), lambda i: (i, 0))
    vec = pl.BlockSpec((1, D), lambda i: (0, 0))
    ins = ([row] if has_d1 else []) + [row, row, pl.BlockSpec((tm, 1), lambda i: (i, 0)), vec]
    args = ([d1] if has_d1 else []) + [d2, xh, rs, g]
    return pl.pallas_call(
        body, name=name, grid=(S // tm,), in_specs=ins, out_specs=[row, row, vec, vec],
        out_shape=[_sds((S, D), F32), _sds((S, D), BF16), _sds((1, D), F32), _sds((1, D), F32)],
        compiler_params=_params("arbitrary"),
    )(*args)


def _loss_fwd(y, target, *, name):
    S, D = y.shape
    tm = min(ROW_T, S)

    def body(y_ref, t_ref, d_ref, l_ref):
        e = y_ref[...] - t_ref[...]
        d_ref[...] = e * (1.0 / D)

        @pl.when(pl.program_id(0) == 0)
        def _():
            l_ref[...] = jnp.zeros_like(l_ref)

        part = jnp.sum(e * e, axis=0, keepdims=True)
        l_ref[...] += (0.5 / D) * jnp.sum(part, axis=1, keepdims=True)

    row = pl.BlockSpec((tm, D), lambda i: (i, 0))
    return pl.pallas_call(
        body, name=name, grid=(S // tm,), in_specs=[row, row],
        out_specs=[row, pl.BlockSpec((1, 1), lambda i: (0, 0))], out_shape=[_sds((S, D), F32), _sds((1, 1), F32)],
        compiler_params=_params("arbitrary"),
    )(y, target)


def _axpy(d1, d2, *, name):
    S, D = d1.shape
    tm = min(ROW_T, S)

    def body(a_ref, b_ref, o_ref):
        o_ref[...] = ALPHA * a_ref[...] + b_ref[...]

    row = pl.BlockSpec((tm, D), lambda i: (i, 0))
    return pl.pallas_call(body, name=name, grid=(S // tm,), in_specs=[row, row], out_specs=row,
                          out_shape=_sds((S, D), F32), compiler_params=_params("parallel"))(d1, d2)


def _ffn_act_fwd(h, cw, cb, *, name):
    S, W = h.shape
    tc = FFN_TC
    nj = W // (2 * tc)
    tm = min(ACT_T, S)

    def body(h_ref, w_ref, b_ref, a_ref, carry):
        @pl.when(pl.program_id(1) == 0)
        def _():
            carry[...] = jnp.zeros_like(carry)

        cur = h_ref[...].astype(F32)
        prev8 = carry[...]
        hc = cur * w_ref[2:3, :] + _shift_down(cur, prev8, 1) * w_ref[1:2, :] + _shift_down(cur, prev8, 2) * w_ref[0:1, :]
        hc = hc + b_ref[...]
        carry[...] = cur[tm - 8:tm]
        hg, hu = hc[:, :tc], hc[:, tc:]
        a_ref[...] = (hg * _sigmoid(hg) * hu).astype(BF16)

    return pl.pallas_call(
        body, name=name, grid=(nj, S // tm),
        in_specs=[pl.BlockSpec((tm, 2 * tc), lambda j, i: (i, j)), pl.BlockSpec((3, 2 * tc), lambda j, i: (0, j)),
                  pl.BlockSpec((1, 2 * tc), lambda j, i: (0, j))],
        out_specs=pl.BlockSpec((tm, tc), lambda j, i: (i, j)), out_shape=_sds((S, W // 2), BF16),
        scratch_shapes=[pltpu.VMEM((8, 2 * tc), F32)],
        compiler_params=_params("parallel", "arbitrary"),
    )(h, cw, cb)


def _ffn_act_bwd(h, da, cw, cb, *, name):
    S, W = h.shape
    tc = FFN_TC
    nj = W // (2 * tc)
    tm = min(ACT_T, S)
    ni = S // tm

    def body(h_ref, hp_ref, da_ref, w_ref, b_ref, dh_ref, dw_ref, db_ref, carry):
        i = pl.program_id(1)
        r = ni - 1 - i

        @pl.when(i == 0)
        def _():
            carry[...] = jnp.zeros_like(carry)
            dw_ref[...] = jnp.zeros_like(dw_ref)
            db_ref[...] = jnp.zeros_like(db_ref)

        cur = h_ref[...].astype(F32)
        prev8 = jnp.where(r > 0, hp_ref[8:16, :].astype(F32), 0.0)
        sh = [cur, _shift_down(cur, prev8, 1), _shift_down(cur, prev8, 2)]
        hc = sh[0] * w_ref[2:3, :] + sh[1] * w_ref[1:2, :] + sh[2] * w_ref[0:1, :] + b_ref[...]
        hg, hu = hc[:, :tc], hc[:, tc:]
        d = da_ref[...].astype(F32)
        sg = _sigmoid(hg)
        dg = d * hu * (sg * (1.0 + hg * (1.0 - sg)))
        du = d * (hg * sg)
        dhc = jnp.concatenate([dg, du], axis=1)
        db_ref[...] += jnp.sum(dhc, axis=0, keepdims=True)
        for k in range(3):
            dw_ref[k:k + 1, :] += jnp.sum(dhc * sh[2 - k], axis=0, keepdims=True)
        next8 = carry[...]
        dh = dhc * w_ref[2:3, :] + _shift_up(dhc, next8, 1) * w_ref[1:2, :] + _shift_up(dhc, next8, 2) * w_ref[0:1, :]
        carry[...] = dhc[0:8]
        dh_ref[...] = dh.astype(BF16)

    rev = lambda j, i: (ni - 1 - i, j)
    return pl.pallas_call(
        body, name=name, grid=(nj, ni),
        in_specs=[pl.BlockSpec((tm, 2 * tc), rev),
                  pl.BlockSpec((16, 2 * tc), lambda j, i: (jnp.maximum((ni - 1 - i) * (tm // 16) - 1, 0), j)),
                  pl.BlockSpec((tm, tc), rev), pl.BlockSpec((3, 2 * tc), lambda j, i: (0, j)),
                  pl.BlockSpec((1, 2 * tc), lambda j, i: (0, j))],
        out_specs=[pl.BlockSpec((tm, 2 * tc), rev), pl.BlockSpec((3, 2 * tc), lambda j, i: (0, j)),
                   pl.BlockSpec((1, 2 * tc), lambda j, i: (0, j))],
        out_shape=[_sds((S, W), BF16), _sds((3, W), F32), _sds((1, W), F32)],
        scratch_shapes=[pltpu.VMEM((8, 2 * tc), F32)],
        compiler_params=_params("parallel", "arbitrary"),
    )(h, h, da, cw, cb)


def _xattn_probs(q, k):
    s = lax.dot_general(q, k, C11, preferred_element_type=F32) * (X_HEAD_DIM ** -0.5)
    p = jnp.exp(s - jnp.max(s, axis=-1, keepdims=True))
    return p / jnp.sum(p, axis=-1, keepdims=True)


def _xattn_fwd(q, mkv, *, name):
    S, D = q.shape
    tm = min(ROW_T, S)

    def body(q_ref, k_ref, v_ref, o_ref):
        for h in range(X_HEADS):
            sl = slice(h * X_HEAD_DIM, (h + 1) * X_HEAD_DIM)
            p = _xattn_probs(q_ref[:, sl], k_ref[:, sl])
            o_ref[:, sl] = jnp.dot(p.astype(BF16), v_ref[:, sl], preferred_element_type=F32).astype(BF16)

    return pl.pallas_call(
        body, name=name, grid=(S // tm,),
        in_specs=[pl.BlockSpec((tm, D), lambda i: (i, 0)), pl.BlockSpec((MEM_LEN, D), lambda i: (0, 0)),
                  pl.BlockSpec((MEM_LEN, D), lambda i: (0, 1))],
        out_specs=pl.BlockSpec((tm, D), lambda i: (i, 0)), out_shape=_sds((S, D), BF16),
        compiler_params=_params("parallel"),
    )(q, mkv, mkv)


def _xattn_bwd(q, mkv, do, *, name):
    S, D = q.shape
    tm = min(ROW_T, S)
    scale = X_HEAD_DIM ** -0.5

    def body(q_ref, k_ref, v_ref, do_ref, dq_ref, dkv_ref):
        @pl.when(pl.program_id(0) == 0)
        def _():
            dkv_ref[...] = jnp.zeros_like(dkv_ref)

        for h in range(X_HEADS):
            sl = slice(h * X_HEAD_DIM, (h + 1) * X_HEAD_DIM)
            sv = slice(D + h * X_HEAD_DIM, D + (h + 1) * X_HEAD_DIM)
            qh, kh, vh, doh = q_ref[:, sl], k_ref[:, sl], v_ref[:, sl], do_ref[:, sl]
            p = _xattn_probs(qh, kh)
            dp = lax.dot_general(doh, vh, C11, preferred_element_type=F32)
            ds = (p * (dp - jnp.sum(p * dp, axis=-1, keepdims=True)) * scale).astype(BF16)
            dq_ref[:, sl] = jnp.dot(ds, kh, preferred_element_type=F32).astype(BF16)
            dkv_ref[:, sl] += lax.dot_general(ds, qh, C00, preferred_element_type=F32)
            dkv_ref[:, sv] += lax.dot_general(p.astype(BF16), doh, C00, preferred_element_type=F32)

    row = pl.BlockSpec((tm, D), lambda i: (i, 0))
    return pl.pallas_call(
        body, name=name, grid=(S // tm,),
        in_specs=[row, pl.BlockSpec((MEM_LEN, D), lambda i: (0, 0)), pl.BlockSpec((MEM_LEN, D), lambda i: (0, 1)), row],
        out_specs=[row, pl.BlockSpec((MEM_LEN, 2 * D), lambda i: (0, 0))],
        out_shape=[_sds((S, D), BF16), _sds((MEM_LEN, 2 * D), F32)],
        compiler_params=_params("arbitrary"),
    )(q, mkv, mkv, do)


def _rope_cols(x, cos, sin, n_rope, *, name):
    S, W = x.shape
    tm = min(ROW_T, S)

    def body(x_ref, c_ref, s_ref, o_ref):
        o_ref[:, :n_rope] = _rope(x_ref[:, :n_rope], c_ref[...], s_ref[...]).astype(BF16)
        if n_rope < W:
            o_ref[:, n_rope:] = x_ref[:, n_rope:].astype(BF16)

    row = pl.BlockSpec((tm, W), lambda i: (i, 0))
    tab = pl.BlockSpec((tm, LANES), lambda i: (i, 0))
    return pl.pallas_call(body, name=name, grid=(S // tm,), in_specs=[row, tab, tab], out_specs=row,
                          out_shape=_sds((S, W), BF16), compiler_params=_params("parallel"))(x, cos, sin)


def _swa_band(n):
    qi = lax.broadcasted_iota(jnp.int32, (BLOCK, 2 * BLOCK), 0)
    kj = lax.broadcasted_iota(jnp.int32, (BLOCK, 2 * BLOCK), 1)
    first = jnp.where(n > 0, 0, BLOCK)
    return ((kj < BLOCK) & (kj > qi + first)) | ((kj >= BLOCK) & (kj - BLOCK <= qi))


def _swa_probs(q, k, band, sink):
    s = lax.dot_general(q, k, C11, preferred_element_type=F32) * (A_HEAD_DIM ** -0.5)
    s = jnp.where(band, s, NEG)
    m = jnp.maximum(jnp.max(s, axis=-1, keepdims=True), sink)
    p = jnp.exp(s - m)
    e_sink = jnp.exp(sink - m)
    den = jnp.sum(p, axis=-1, keepdims=True) + e_sink
    return p / den, e_sink / den


def _swa_specs():
    nq, nkv = A_HEADS * A_HEAD_DIM, A_KV_HEADS * A_HEAD_DIM
    kb, vb = nq // nkv, nq // nkv + 1
    prev = lambda n: jnp.maximum(n - 1, 0)
    return [pl.BlockSpec((BLOCK, nq), lambda n: (n, 0)),
            pl.BlockSpec((BLOCK, nkv), lambda n: (n, kb)), pl.BlockSpec((BLOCK, nkv), lambda n: (prev(n), kb)),
            pl.BlockSpec((BLOCK, nkv), lambda n: (n, vb)), pl.BlockSpec((BLOCK, nkv), lambda n: (prev(n), vb)),
            pl.BlockSpec(memory_space=pltpu.SMEM)]


def _swa_fwd(qkv, sinks, *, name):
    S = qkv.shape[0]
    hd, grp = A_HEAD_DIM, A_HEADS // A_KV_HEADS

    def body(q_ref, kc_ref, kp_ref, vc_ref, vp_ref, sink_ref, o_ref):
        band = _swa_band(pl.program_id(0))
        qa, kc, kp, vc, vp = q_ref[...], kc_ref[...], kp_ref[...], vc_ref[...], vp_ref[...]
        for hk in range(A_KV_HEADS):
            ks = slice(hk * hd, (hk + 1) * hd)
            k = jnp.concatenate([kp[:, ks], kc[:, ks]], axis=0)
            v = jnp.concatenate([vp[:, ks], vc[:, ks]], axis=0)
            for gi in range(grp):
                h = hk * grp + gi
                p, _ = _swa_probs(qa[:, h * hd:(h + 1) * hd], k, band, sink_ref[h])
                o_ref[:, h * hd:(h + 1) * hd] = jnp.dot(p.astype(BF16), v, preferred_element_type=F32).astype(BF16)

    return pl.pallas_call(
        body, name=name, grid=(S // BLOCK,), in_specs=_swa_specs(),
        out_specs=pl.BlockSpec((BLOCK, A_HEADS * hd), lambda n: (n, 0)), out_shape=_sds((S, A_HEADS * hd), BF16),
        compiler_params=_params("parallel"),
    )(qkv, qkv, qkv, qkv, qkv, sinks)


def _swa_bwd(qkv, sinks, do, cos, sin, *, name):
    S = qkv.shape[0]
    hd, grp = A_HEAD_DIM, A_HEADS // A_KV_HEADS
    nq, nkv = A_HEADS * hd, A_KV_HEADS * hd
    scale = hd ** -0.5

    def body(q_ref, kc_ref, kp_ref, vc_ref, vp_ref, sink_ref, do_ref, c_ref, s_ref, dq_ref, dc_ref, dp_ref, ds_ref, dq_s):
        @pl.when(pl.program_id(0) == 0)
        def _():
            ds_ref[...] = jnp.zeros_like(ds_ref)

        band = _swa_band(pl.program_id(0))
        lane = lax.broadcasted_iota(jnp.int32, (1, LANES), 1)
        qa, kc, kp, vc, vp, doa = q_ref[...], kc_ref[...], kp_ref[...], vc_ref[...], vp_ref[...], do_ref[...]
        dsink = jnp.zeros((1, LANES), F32)
        for hk in range(A_KV_HEADS):
            ks = slice(hk * hd, (hk + 1) * hd)
            k = jnp.concatenate([kp[:, ks], kc[:, ks]], axis=0)
            v = jnp.concatenate([vp[:, ks], vc[:, ks]], axis=0)
            dk = jnp.zeros((2 * BLOCK, hd), F32)
            dv = jnp.zeros((2 * BLOCK, hd), F32)
            for gi in range(grp):
                h = hk * grp + gi
                hs = slice(h * hd, (h + 1) * hd)
                qh, doh = qa[:, hs], doa[:, hs]
                p, p_sink = _swa_probs(qh, k, band, sink_ref[h])
                dpr = lax.dot_general(doh, v, C11, preferred_element_type=F32)
                delta = jnp.sum(p * dpr, axis=-1, keepdims=True)
                dsc = (p * (dpr - delta) * scale).astype(BF16)
                dq_s[:, hs] = jnp.dot(dsc, k, preferred_element_type=F32)
                dk = dk + lax.dot_general(dsc, qh, C00, preferred_element_type=F32)
                dv = dv + lax.dot_general(p.astype(BF16), doh, C00, preferred_element_type=F32)
                dsink = dsink + jnp.where(lane == h, -jnp.sum(p_sink * delta, axis=0, keepdims=True), 0.0)
            dp_ref[:, ks] = dk[:BLOCK]
            dc_ref[:, ks] = dk[BLOCK:]
            dp_ref[:, nkv + hk * hd:nkv + (hk + 1) * hd] = dv[:BLOCK]
            dc_ref[:, nkv + hk * hd:nkv + (hk + 1) * hd] = dv[BLOCK:]
        ds_ref[...] += dsink
        dq_ref[...] = _rope_t(dq_s[...], c_ref[...], s_ref[...]).astype(BF16)

    tab = pl.BlockSpec((BLOCK, LANES), lambda n: (n, 0))
    blk = lambda w: pl.BlockSpec((BLOCK, w), lambda n: (n, 0))
    return pl.pallas_call(
        body, name=name, grid=(S // BLOCK,), in_specs=_swa_specs() + [blk(nq), tab, tab],
        out_specs=[blk(nq), blk(2 * nkv), blk(2 * nkv), pl.BlockSpec((1, LANES), lambda n: (0, 0))],
        out_shape=[_sds((S, nq), BF16), _sds((S, 2 * nkv), F32), _sds((S, 2 * nkv), F32), _sds((1, LANES), F32)],
        scratch_shapes=[pltpu.VMEM((BLOCK, nq), F32)],
        compiler_params=_params("arbitrary"),
    )(qkv, qkv, qkv, qkv, qkv, sinks, do, cos, sin)


def _swa_dqkv(dq, dcur, dprev, cos, sin, *, name):
    S, nq = dq.shape
    nkv = dcur.shape[1] // 2
    nb = S // BLOCK

    def body(dq_ref, dc_ref, dp_ref, c_ref, s_ref, o_ref):
        o_ref[:, :nq] = dq_ref[...]
        d = dc_ref[...] + jnp.where(pl.program_id(0) < nb - 1, dp_ref[...], 0.0)
        o_ref[:, nq:nq + nkv] = _rope_t(d[:, :nkv], c_ref[...], s_ref[...]).astype(BF16)
        o_ref[:, nq + nkv:] = d[:, nkv:].astype(BF16)

    tab = pl.BlockSpec((BLOCK, LANES), lambda m: (m, 0))
    blk = lambda w: pl.BlockSpec((BLOCK, w), lambda m: (m, 0))
    return pl.pallas_call(
        body, name=name, grid=(nb,),
        in_specs=[blk(nq), blk(2 * nkv), pl.BlockSpec((BLOCK, 2 * nkv), lambda m: (jnp.minimum(m + 1, nb - 1), 0)), tab, tab],
        out_specs=blk(nq + 2 * nkv), out_shape=_sds((S, nq + 2 * nkv), BF16), compiler_params=_params("parallel"),
    )(dq, dcur, dprev, cos, sin)


def _lru_gates(u, wri_ref, br, bi, sp):
    ub = u.astype(BF16)
    rs, igs = [], []
    for hb in range(LRU_BLOCKS):
        sl = slice(hb * LRU_BLOCK_W, (hb + 1) * LRU_BLOCK_W)
        ri = jnp.dot(ub[:, sl], wri_ref[hb], preferred_element_type=F32)
        rs.append(ri[:, :LRU_BLOCK_W])
        igs.append(ri[:, LRU_BLOCK_W:])
    r = _sigmoid(jnp.concatenate(rs, axis=1) + br)
    ig = _sigmoid(jnp.concatenate(igs, axis=1) + bi)
    la = -LRU_C * r * sp
    a = jnp.exp(la)
    sq = jnp.sqrt(_neg_expm1(2.0 * la))
    return r, ig, a, sq


def _lru_fwd(xw, cw, cb, wri, br, bi, lam, *, name):
    S = xw.shape[0]
    W = D_MODEL
    tm = min(LRU_T, S)

    def body(gate_ref, up_ref, cw_ref, cb_ref, wri_ref, br_ref, bi_ref, lam_ref, y_ref, u_ref, h_ref, cu, ch, a_s, b_s):
        @pl.when(pl.program_id(0) == 0)
        def _():
            cu[...] = jnp.zeros_like(cu)
            ch[...] = jnp.zeros_like(ch)

        up = up_ref[...]
        prev8 = cu[...]
        u = up * cw_ref[3:4, :] + cb_ref[...]
        for d in range(1, LRU_CONV):
            u = u + _shift_down(up, prev8, d) * cw_ref[3 - d:4 - d, :]
        cu[...] = up[tm - 8:tm]
        u_ref[...] = u
        sp, _ = _softplus_neg(lam_ref[...])
        _, ig, a, sq = _lru_gates(u, wri_ref, br_ref[...], bi_ref[...], sp)
        a_s[...] = a
        b_s[...] = sq * (ig * u)
        rid = lax.broadcasted_iota(jnp.int32, (8, W), 0)

        def tile(t, h):
            r0 = pl.multiple_of(t * 8, 8)
            at, bt = a_s[pl.ds(r0, 8), :], b_s[pl.ds(r0, 8), :]
            out = jnp.zeros((8, W), F32)
            for j in range(8):
                h = at[j:j + 1, :] * h + bt[j:j + 1, :]
                out = jnp.where(rid == j, h, out)
            h_ref[pl.ds(r0, 8), :] = out
            return h

        ch[0:1, :] = lax.fori_loop(0, tm // 8, tile, ch[0:1, :])
        g, _ = _gelu_and_grad(gate_ref[...])
        y_ref[...] = (h_ref[...] * g).astype(BF16)

    row = pl.BlockSpec((tm, W), lambda i: (i, 0))
    full = lambda shape: pl.BlockSpec(shape, lambda i: (0,) * len(shape))
    return pl.pallas_call(
        body, name=name, grid=(S // tm,),
        in_specs=[row, pl.BlockSpec((tm, W), lambda i: (i, 1)), full((LRU_CONV, W)), full((1, W)),
                  full((LRU_BLOCKS, LRU_BLOCK_W, 2 * LRU_BLOCK_W)), full((1, W)), full((1, W)), full((1, W))],
        out_specs=[row, row, row], out_shape=[_sds((S, W), BF16), _sds((S, W), F32), _sds((S, W), F32)],
        scratch_shapes=[pltpu.VMEM((8, W), F32), pltpu.VMEM((8, W), F32), pltpu.VMEM((tm, W), F32), pltpu.VMEM((tm, W), F32)],
        compiler_params=_params("arbitrary"),
    )(xw, xw, cw, cb, wri, br, bi, lam)


def _lru_bwd(xw, u, h, dy, cw, wri, br, bi, lam, *, name):
    S = xw.shape[0]
    W = D_MODEL
    tm = min(LRU_T, S)
    nb = S // tm

    def body(gate_ref, up_ref, upp_ref, u_ref, h_ref, hp_ref, dy_ref, cw_ref, wri_ref, br_ref, bi_ref, lam_ref,
             dxw_ref, dcw_ref, dcb_ref, dwri_ref, dbr_ref, dbi_ref, dlam_ref, cg, cdu, a_s, d_s, g_s):
        i = pl.program_id(0)
        r_blk = nb - 1 - i

        @pl.when(i == 0)
        def _():
            cg[...] = jnp.zeros_like(cg)
            cdu[...] = jnp.zeros_like(cdu)
            for ref in (dcw_ref, dcb_ref, dwri_ref, dbr_ref, dbi_ref, dlam_ref):
                ref[...] = jnp.zeros_like(ref)

        u = u_ref[...]
        hv = h_ref[...]
        sp, dsp = _softplus_neg(lam_ref[...])
        r, ig, a, sq = _lru_gates(u, wri_ref, br_ref[...], bi_ref[...], sp)
        dy = dy_ref[...].astype(F32)
        g, dgelu = _gelu_and_grad(gate_ref[...])
        dxw_ref[:, :W] = (dy * hv * dgelu).astype(BF16)
        a_s[...] = a
        d_s[...] = dy * g
        rid = lax.broadcasted_iota(jnp.int32, (8, W), 0)

        def tile(t, c):
            r0 = pl.multiple_of((tm // 8 - 1 - t) * 8, 8)
            at, dt = a_s[pl.ds(r0, 8), :], d_s[pl.ds(r0, 8), :]
            out = jnp.zeros((8, W), F32)
            for j in range(7, -1, -1):
                gt = dt[j:j + 1, :] + c
                c = at[j:j + 1, :] * gt
                out = jnp.where(rid == j, gt, out)
            g_s[pl.ds(r0, 8), :] = out
            return c

        cg[0:1, :] = lax.fori_loop(0, tm // 8, tile, cg[0:1, :])
        gt = g_s[...]
        hprev8 = jnp.where(r_blk > 0, hp_ref[...], 0.0)
        da = gt * _shift_down(hv, hprev8, 1)
        iu = ig * u
        d_iu = gt * sq
        dla = da * a - (gt * iu) * (a * a) / sq
        dlam_ref[...] += jnp.sum(dla * r, axis=0, keepdims=True) * (-LRU_C) * dsp
        dr_pre = dla * (-LRU_C) * sp * r * (1.0 - r)
        di_pre = d_iu * u * ig * (1.0 - ig)
        dbr_ref[...] += jnp.sum(dr_pre, axis=0, keepdims=True)
        dbi_ref[...] += jnp.sum(di_pre, axis=0, keepdims=True)
        ub = u.astype(BF16)
        dus = []
        for hb in range(LRU_BLOCKS):
            sl = slice(hb * LRU_BLOCK_W, (hb + 1) * LRU_BLOCK_W)
            dri = jnp.concatenate([dr_pre[:, sl], di_pre[:, sl]], axis=1).astype(BF16)
            dus.append(lax.dot_general(dri, wri_ref[hb], C11, preferred_element_type=F32))
            dwri_ref[hb] += lax.dot_general(ub[:, sl], dri, C00, preferred_element_type=F32)
        du = d_iu * ig + jnp.concatenate(dus, axis=1)
        dcb_ref[...] += jnp.sum(du, axis=0, keepdims=True)
        up = up_ref[...]
        upprev8 = jnp.where(r_blk > 0, upp_ref[...], 0.0)
        dcw_ref[3:4, :] += jnp.sum(du * up, axis=0, keepdims=True)
        for d in range(1, LRU_CONV):
            dcw_ref[3 - d:4 - d, :] += jnp.sum(du * _shift_down(up, upprev8, d), axis=0, keepdims=True)
        next8 = cdu[...]
        dup = du * cw_ref[3:4, :]
        for d in range(1, LRU_CONV):
            dup = dup + _shift_up(du, next8, d) * cw_ref[3 - d:4 - d, :]
        cdu[...] = du[0:8]
        dxw_ref[:, W:] = dup.astype(BF16)

    rev = lambda c: (lambda i: (nb - 1 - i, c))
    halo = lambda c: (lambda i: (jnp.maximum((nb - 1 - i) * (tm // 8) - 1, 0), c))
    full = lambda shape: pl.BlockSpec(shape, lambda i: (0,) * len(shape))
    vec = full((1, W))
    return pl.pallas_call(
        body, name=name, grid=(nb,),
        in_specs=[pl.BlockSpec((tm, W), rev(0)), pl.BlockSpec((tm, W), rev(1)), pl.BlockSpec((8, W), halo(1)),
                  pl.BlockSpec((tm, W), rev(0)), pl.BlockSpec((tm, W), rev(0)), pl.BlockSpec((8, W), halo(0)),
                  pl.BlockSpec((tm, W), rev(0)), full((LRU_CONV, W)), full((LRU_BLOCKS, LRU_BLOCK_W, 2 * LRU_BLOCK_W)),
                  vec, vec, vec],
        out_specs=[pl.BlockSpec((tm, 2 * W), rev(0)), full((LRU_CONV, W)), vec,
                   full((LRU_BLOCKS, LRU_BLOCK_W, 2 * LRU_BLOCK_W)), vec, vec, vec],
        out_shape=[_sds((S, 2 * W), BF16), _sds((LRU_CONV, W), F32), _sds((1, W), F32),
                   _sds((LRU_BLOCKS, LRU_BLOCK_W, 2 * LRU_BLOCK_W), F32), _sds((1, W), F32), _sds((1, W), F32),
                   _sds((1, W), F32)],
        scratch_shapes=[pltpu.VMEM((8, W), F32), pltpu.VMEM((8, W), F32), pltpu.VMEM((tm, W), F32),
                        pltpu.VMEM((tm, W), F32), pltpu.VMEM((tm, W), F32)],
        compiler_params=_params("arbitrary"),
    )(xw, xw, xw, u, h, h, dy, cw, wri, br, bi, lam)


def _rms(x, g):
    r = lax.rsqrt(jnp.mean(x * x, axis=-1, keepdims=True) + RMS_EPS)
    return x * r * g, r


def _mla_pre(c, qg, kvg, cos, sin, *, name):
    S = c.shape[0]
    tm = min(ROW_T, S)
    q0, k0 = C_Q_RANK, C_Q_RANK + C_KV_RANK

    def body(c_ref, qg_ref, kvg_ref, cs_ref, sn_ref, cq_ref, ckv_ref, kr_ref):
        cq_ref[...] = _rms(c_ref[:, :q0], qg_ref[...])[0].astype(BF16)
        ckv_ref[...] = _rms(c_ref[:, q0:k0], kvg_ref[...])[0].astype(BF16)
        kr_ref[...] = _rope(c_ref[:, k0:], cs_ref[...], sn_ref[...]).astype(BF16)

    blk = lambda w: pl.BlockSpec((tm, w), lambda i: (i, 0))
    vec = lambda w: pl.BlockSpec((1, w), lambda i: (0, 0))
    return pl.pallas_call(
        body, name=name, grid=(S // tm,),
        in_specs=[blk(c.shape[1]), vec(C_Q_RANK), vec(C_KV_RANK), blk(LANES), blk(LANES)],
        out_specs=[blk(C_Q_RANK), blk(C_KV_RANK), blk(C_ROPE)],
        out_shape=[_sds((S, C_Q_RANK), BF16), _sds((S, C_KV_RANK), BF16), _sds((S, C_ROPE), BF16)],
        compiler_params=_params("parallel"),
    )(c, qg, kvg, cos, sin)


def _mla_post_bwd(c, dcq_a, dcq_b, dckv, dkr_h, qg, kvg, cos, sin, *, name):
    S = c.shape[0]
    tm = min(ROW_T, S)
    q0, k0 = C_Q_RANK, C_Q_RANK + C_KV_RANK

    def rms_bwd(x, g, dy):
        r = lax.rsqrt(jnp.mean(x * x, axis=-1, keepdims=True) + RMS_EPS)
        uu = dy * g
        dx = r * uu - x * (r * r * r) * jnp.mean(uu * x, axis=-1, keepdims=True)
        return dx, jnp.sum(dy * x * r, axis=0, keepdims=True)

    def body(c_ref, da_ref, db_ref, dkv_ref, dkr_ref, qg_ref, kvg_ref, cs_ref, sn_ref, dc_ref, dqg_ref, dkvg_ref):
        @pl.when(pl.program_id(0) == 0)
        def _():
            dqg_ref[...] = jnp.zeros_like(dqg_ref)
            dkvg_ref[...] = jnp.zeros_like(dkvg_ref)

        dx, dg = rms_bwd(c_ref[:, :q0], qg_ref[...], da_ref[...] + db_ref[...])
        dc_ref[:, :q0] = dx.astype(BF16)
        dqg_ref[...] += dg
        dx, dg = rms_bwd(c_ref[:, q0:k0], kvg_ref[...], dkv_ref[...])
        dc_ref[:, q0:k0] = dx.astype(BF16)
        dkvg_ref[...] += dg
        dkr = dkr_ref[0]
        for hh in range(1, C_HEADS):
            dkr = dkr + dkr_ref[hh]
        dc_ref[:, k0:] = _rope_t(dkr, cs_ref[...], sn_ref[...]).astype(BF16)

    blk = lambda w: pl.BlockSpec((tm, w), lambda i: (i, 0))
    vec = lambda w: pl.BlockSpec((1, w), lambda i: (0, 0))
    return pl.pallas_call(
        body, name=name, grid=(S // tm,),
        in_specs=[blk(c.shape[1]), blk(C_Q_RANK), blk(C_Q_RANK), blk(C_KV_RANK),
                  pl.BlockSpec((C_HEADS, tm, C_ROPE), lambda i: (0, i, 0)), vec(C_Q_RANK), vec(C_KV_RANK), blk(LANES), blk(LANES)],
        out_specs=[blk(c.shape[1]), vec(C_Q_RANK), vec(C_KV_RANK)],
        out_shape=[_sds(c.shape, BF16), _sds((1, C_Q_RANK), F32), _sds((1, C_KV_RANK), F32)],
        compiler_params=_params("arbitrary"),
    )(c, dcq_a, dcq_b, dckv, dkr_h, qg, kvg, cos, sin)


def _rope_heads(x, cos, sin, *, transpose, name):
    S, W = x.shape
    tm = min(ROW_T, S)
    fn = _rope_t if transpose else _rope

    def body(x_ref, c_ref, s_ref, o_ref):
        o_ref[...] = fn(x_ref[...].astype(F32), c_ref[...], s_ref[...]).astype(BF16)

    row = pl.BlockSpec((tm, W), lambda i: (i, 0))
    tab = pl.BlockSpec((tm, LANES), lambda i: (i, 0))
    return pl.pallas_call(body, name=name, grid=(S // tm,), in_specs=[row, tab, tab], out_specs=row,
                          out_shape=_sds((S, W), BF16), compiler_params=_params("parallel"))(x, cos, sin)


MLA_SCALE = (C_NOPE + C_ROPE) ** -0.5
LOG2E = 1.4426950408889634


def _mla_scores2(qn, qr, kn, kr, diagonal):
    s = lax.dot_general(qn, kn, C11, preferred_element_type=F32) + lax.dot_general(qr, kr, C11, preferred_element_type=F32)
    s = s * (MLA_SCALE * LOG2E)
    if diagonal:
        row = lax.broadcasted_iota(jnp.int32, s.shape, 0)
        col = lax.broadcasted_iota(jnp.int32, s.shape, 1)
        s = jnp.where(col <= row, s, NEG)
    return s


def _causal_pairs(n, query_major):
    if query_major:
        pairs = [(i, j) for i in range(n) for j in range(i + 1)]
    else:
        pairs = [(i, j) for j in range(n) for i in range(j, n)]
    return jnp.asarray([p[0] for p in pairs], jnp.int32), jnp.asarray([p[1] for p in pairs], jnp.int32)


def _mla_flash_fwd(qn, qr, kv, kr, *, name):
    S = qn.shape[0]
    H, t = C_HEADS, min(FLASH_T, S)
    qi, kj = _causal_pairs(S // t, True)

    def body(qi_ref, kj_ref, qn_ref, qr_ref, kn_ref, v_ref, kr_ref, o_ref, lse_ref, m_s, l_s, acc):
        p_id = pl.program_id(1)
        i, j = qi_ref[p_id], kj_ref[p_id]

        @pl.when(j == 0)
        def _():
            m_s[...] = jnp.full_like(m_s, NEG)
            l_s[...] = jnp.zeros_like(l_s)
            acc[...] = jnp.zeros_like(acc)

        def step(diagonal):
            s = _mla_scores2(qn_ref[...], qr_ref[...], kn_ref[...], kr_ref[...], diagonal)
            m_prev = m_s[...]
            m_new = jnp.maximum(m_prev, jnp.max(s, axis=-1, keepdims=True))
            corr = jnp.exp2(m_prev - m_new)
            p = jnp.exp2(s - m_new[:, 0:1])
            l_s[...] = corr * l_s[...] + jnp.sum(p, axis=-1, keepdims=True)
            acc[...] = corr * acc[...] + jnp.dot(p.astype(BF16), v_ref[...], preferred_element_type=F32)
            m_s[...] = m_new

        @pl.when(j < i)
        def _():
            step(False)

        @pl.when(j == i)
        def _():
            step(True)
            o_ref[...] = (acc[...] / l_s[...]).astype(BF16)
            lse_ref[...] = m_s[...] + jnp.log2(l_s[...])

    return pl.pallas_call(
        body, name=name,
        grid_spec=pltpu.PrefetchScalarGridSpec(
            num_scalar_prefetch=2, grid=(H, qi.shape[0]),
            in_specs=[pl.BlockSpec((t, C_NOPE), lambda h, p, qi, kj: (qi[p], h)),
                      pl.BlockSpec((None, t, C_ROPE), lambda h, p, qi, kj: (h, qi[p], 0)),
                      pl.BlockSpec((t, C_NOPE), lambda h, p, qi, kj: (kj[p], h)),
                      pl.BlockSpec((t, C_V), lambda h, p, qi, kj: (kj[p], H + h)),
                      pl.BlockSpec((t, C_ROPE), lambda h, p, qi, kj: (kj[p], 0))],
            out_specs=[pl.BlockSpec((t, C_V), lambda h, p, qi, kj: (qi[p], h)),
                       pl.BlockSpec((t, LANES), lambda h, p, qi, kj: (qi[p], h))],
            scratch_shapes=[pltpu.VMEM((t, LANES), F32), pltpu.VMEM((t, LANES), F32), pltpu.VMEM((t, C_V), F32)]),
        out_shape=[_sds((S, H * C_V), BF16), _sds((S, H * LANES), F32)],
        compiler_params=_params("parallel", "arbitrary"),
    )(qi, kj, qn, qr, kv, kv, kr)


def _mla_delta(do, o, *, name):
    S, W = do.shape
    tm = min(ROW_T, S)

    def body(do_ref, o_ref, d_ref):
        for h in range(C_HEADS):
            sl = slice(h * C_V, (h + 1) * C_V)
            d = jnp.sum(do_ref[:, sl].astype(F32) * o_ref[:, sl].astype(F32), axis=-1, keepdims=True)
            d_ref[:, sl] = jnp.broadcast_to(d, (tm, C_V))

    row = pl.BlockSpec((tm, W), lambda i: (i, 0))
    return pl.pallas_call(body, name=name, grid=(S // tm,), in_specs=[row, row], out_specs=row,
                          out_shape=_sds((S, W), F32), compiler_params=_params("parallel"))(do, o)


def _mla_flash_dq(qn, qr, kv, kr, do, lse, delta, *, name):
    S = qn.shape[0]
    H, t = C_HEADS, min(FLASH_T, S)
    qi, kj = _causal_pairs(S // t, True)

    def body(qi_ref, kj_ref, qn_ref, qr_ref, kn_ref, v_ref, kr_ref, do_ref, lse_ref, dl_ref, dqn_ref, dqr_ref, an, ar):
        p_id = pl.program_id(1)
        i, j = qi_ref[p_id], kj_ref[p_id]

        @pl.when(j == 0)
        def _():
            an[...] = jnp.zeros_like(an)
            ar[...] = jnp.zeros_like(ar)

        def step(diagonal):
            s = _mla_scores2(qn_ref[...], qr_ref[...], kn_ref[...], kr_ref[...], diagonal)
            p = jnp.exp2(s - lse_ref[:, 0:1])
            dp = lax.dot_general(do_ref[...], v_ref[...], C11, preferred_element_type=F32)
            ds = (p * (dp - dl_ref[:, 0:1])).astype(BF16)
            an[...] += jnp.dot(ds, kn_ref[...], preferred_element_type=F32)
            ar[...] += jnp.dot(ds, kr_ref[...], preferred_element_type=F32)

        @pl.when(j < i)
        def _():
            step(False)

        @pl.when(j == i)
        def _():
            step(True)
            dqn_ref[...] = (an[...] * MLA_SCALE).astype(BF16)
            dqr_ref[...] = ar[...] * MLA_SCALE

    qb = pl.BlockSpec((t, C_NOPE), lambda h, p, qi, kj: (qi[p], h))
    qrb = pl.BlockSpec((None, t, C_ROPE), lambda h, p, qi, kj: (h, qi[p], 0))
    return pl.pallas_call(
        body, name=name,
        grid_spec=pltpu.PrefetchScalarGridSpec(
            num_scalar_prefetch=2, grid=(H, qi.shape[0]),
            in_specs=[qb, qrb, pl.BlockSpec((t, C_NOPE), lambda h, p, qi, kj: (kj[p], h)),
                      pl.BlockSpec((t, C_V), lambda h, p, qi, kj: (kj[p], H + h)),
                      pl.BlockSpec((t, C_ROPE), lambda h, p, qi, kj: (kj[p], 0)), qb, qb, qb],
            out_specs=[qb, qrb],
            scratch_shapes=[pltpu.VMEM((t, C_NOPE), F32), pltpu.VMEM((t, C_ROPE), F32)]),
        out_shape=[_sds((S, H * C_NOPE), BF16), _sds((H, S, C_ROPE), F32)],
        compiler_params=_params("parallel", "arbitrary"),
    )(qi, kj, qn, qr, kv, kv, kr, do, lse, delta)


def _mla_flash_dkv(qn, qr, kv, kr, do, lse, delta, *, name):
    S = qn.shape[0]
    H, t = C_HEADS, min(FLASH_T, S)
    n = S // t
    qi, kj = _causal_pairs(n, False)

    def body(qi_ref, kj_ref, qn_ref, qr_ref, kn_ref, v_ref, kr_ref, do_ref, lse_ref, dl_ref, dkn_ref, dv_ref, dkr_ref, akn, av, akr):
        p_id = pl.program_id(1)
        i, j = qi_ref[p_id], kj_ref[p_id]

        def step(diagonal):
            s = _mla_scores2(qn_ref[...], qr_ref[...], kn_ref[...], kr_ref[...], diagonal)
            p = jnp.exp2(s - lse_ref[:, 0:1])
            dp = lax.dot_general(do_ref[...], v_ref[...], C11, preferred_element_type=F32)
            ds = (p * (dp - dl_ref[:, 0:1])).astype(BF16)
            av[...] += lax.dot_general(p.astype(BF16), do_ref[...], C00, preferred_element_type=F32)
            akn[...] += lax.dot_general(ds, qn_ref[...], C00, preferred_element_type=F32)
            akr[...] += lax.dot_general(ds, qr_ref[...], C00, preferred_element_type=F32)

        @pl.when(i == j)
        def _():
            akn[...] = jnp.zeros_like(akn)
            av[...] = jnp.zeros_like(av)
            akr[...] = jnp.zeros_like(akr)
            step(True)

        @pl.when(i > j)
        def _():
            step(False)

        @pl.when(i == n - 1)
        def _():
            dkn_ref[...] = (akn[...] * MLA_SCALE).astype(BF16)
            dv_ref[...] = av[...].astype(BF16)
            dkr_ref[...] = akr[...] * MLA_SCALE

    qb = pl.BlockSpec((t, C_NOPE), lambda h, p, qi, kj: (qi[p], h))
    kb = pl.BlockSpec((t, C_NOPE), lambda h, p, qi, kj: (kj[p], h))
    krb = pl.BlockSpec((None, t, C_ROPE), lambda h, p, qi, kj: (h, kj[p], 0))
    return pl.pallas_call(
        body, name=name,
        grid_spec=pltpu.PrefetchScalarGridSpec(
            num_scalar_prefetch=2, grid=(H, qi.shape[0]),
            in_specs=[qb, pl.BlockSpec((None, t, C_ROPE), lambda h, p, qi, kj: (h, qi[p], 0)), kb,
                      pl.BlockSpec((t, C_V), lambda h, p, qi, kj: (kj[p], H + h)),
                      pl.BlockSpec((t, C_ROPE), lambda h, p, qi, kj: (kj[p], 0)), qb, qb, qb],
            out_specs=[kb, kb, krb],
            scratch_shapes=[pltpu.VMEM((t, C_NOPE), F32), pltpu.VMEM((t, C_V), F32), pltpu.VMEM((t, C_ROPE), F32)]),
        out_shape=[_sds((S, H * C_NOPE), BF16), _sds((S, H * C_V), BF16), _sds((H, S, C_ROPE), F32)],
        compiler_params=_params("parallel", "arbitrary"),
    )(qi, kj, qn, qr, kv, kv, kr, do, lse, delta)


def _place():
    return lax.axis_index("x"), lax.axis_index("y"), lax.axis_index("c")


def _other_chips(x, y):
    return [(1 - x, y), (x, 1 - y), (1 - x, 1 - y)]


def _all_gather_chips(p, *, name):
    R, C = p.shape

    def body(p_ref, o_ref, send_sems, recv_sems, local_sem):
        x, y, c = _place()
        me = 2 * x + y
        local = pltpu.make_async_copy(p_ref, o_ref.at[me], local_sem)
        local.start()
        copies = [pltpu.make_async_remote_copy(src_ref=p_ref, dst_ref=o_ref.at[me], send_sem=send_sems.at[k],
                                               recv_sem=recv_sems.at[k], device_id=(px, py, c), device_id_type=MESH)
                  for k, (px, py) in enumerate(_other_chips(x, y))]
        for cp in copies:
            cp.start()
        for cp in copies:
            cp.wait()
        local.wait()

    any_spec = pl.BlockSpec(memory_space=pl.ANY)
    return pl.pallas_call(
        body, name=name, in_specs=[any_spec], out_specs=any_spec, out_shape=_sds((4, R, C), p.dtype),
        scratch_shapes=[pltpu.SemaphoreType.DMA((3,)), pltpu.SemaphoreType.DMA((3,)), pltpu.SemaphoreType.DMA(())],
    )(p)


def _shard_of(ref, axis, pos, size):
    idx = [slice(None)] * len(ref.shape)
    idx[axis] = pl.ds(pos * size, size)
    return ref.at[tuple(idx)]


def _shard_pos(chip, swapped):
    return (chip % 2) * 2 + chip // 2 if swapped else chip


def _gather_weights(shards, axes, swapped, *, name):
    n = len(shards)
    sizes = [s.shape[a] for s, a in zip(shards, axes)]
    full = [tuple(4 * d if i == a else d for i, d in enumerate(s.shape)) for s, a in zip(shards, axes)]

    def body(*refs):
        ins, outs = refs[:n], refs[n:2 * n]
        send_sems, recv_sems, local_sems = refs[2 * n:]
        x, y, c = _place()
        me = 2 * x + y
        copies = []
        for k in range(n):
            dst = _shard_of(outs[k], axes[k], _shard_pos(me, swapped[k]), sizes[k])
            local = pltpu.make_async_copy(ins[k], dst, local_sems.at[k])
            local.start()
            copies.append(local)
            for j, (px, py) in enumerate(_other_chips(x, y)):
                cp = pltpu.make_async_remote_copy(src_ref=ins[k], dst_ref=dst, send_sem=send_sems.at[3 * k + j],
                                                  recv_sem=recv_sems.at[3 * k + j], device_id=(px, py, c), device_id_type=MESH)
                cp.start()
                copies.append(cp)
        for cp in copies:
            cp.wait()

    any_spec = pl.BlockSpec(memory_space=pl.ANY)
    return pl.pallas_call(
        body, name=name, in_specs=[any_spec] * n, out_specs=[any_spec] * n,
        out_shape=[_sds(f, s.dtype) for f, s in zip(full, shards)],
        scratch_shapes=[pltpu.SemaphoreType.DMA((3 * n,)), pltpu.SemaphoreType.DMA((3 * n,)), pltpu.SemaphoreType.DMA((n,))],
    )(*shards)


def _scatter_grads(grads, axes, swapped, *, name):
    n = len(grads)
    sizes = [g.shape[a] // 4 for g, a in zip(grads, axes)]
    shard = [tuple(d // 4 if i == a else d for i, d in enumerate(g.shape)) for g, a in zip(grads, axes)]

    def body(*refs):
        ins, outs = refs[:n], refs[n:2 * n]
        send_sems, recv_sems, local_sems = refs[2 * n:]
        x, y, c = _place()
        me = 2 * x + y
        copies = []
        for k in range(n):
            own = _shard_of(ins[k], axes[k], _shard_pos(me, swapped[k]), sizes[k])
            local = pltpu.make_async_copy(own, outs[k].at[3], local_sems.at[k])
            local.start()
            copies.append(local)
            for j, (px, py) in enumerate(_other_chips(x, y)):
                src = _shard_of(ins[k], axes[k], _shard_pos(2 * px + py, swapped[k]), sizes[k])
                cp = pltpu.make_async_remote_copy(src_ref=src, dst_ref=outs[k].at[j], send_sem=send_sems.at[3 * k + j],
                                                  recv_sem=recv_sems.at[3 * k + j], device_id=(px, py, c), device_id_type=MESH)
                cp.start()
                copies.append(cp)
        for cp in copies:
            cp.wait()

    any_spec = pl.BlockSpec(memory_space=pl.ANY)
    return pl.pallas_call(
        body, name=name, in_specs=[any_spec] * n, out_specs=[any_spec] * n,
        out_shape=[_sds((4,) + s, g.dtype) for s, g in zip(shard, grads)],
        scratch_shapes=[pltpu.SemaphoreType.DMA((3 * n,)), pltpu.SemaphoreType.DMA((3 * n,)), pltpu.SemaphoreType.DMA((n,))],
    )(*grads)


def _row_tile(rows, cols, budget=2 ** 20):
    best = None
    for t in range(8, rows + 1, 8):
        if rows % t == 0 and t * cols * 4 <= budget:
            best = t
    return best or rows


def _sum_partials(recv, *, name):
    _, R, C = recv.shape
    tr = _row_tile(R, C)

    def body(own_ref, r0_ref, r1_ref, r2_ref, o_ref):
        f = lambda ref: ref[...].astype(F32)
        o_ref[...] = ((f(own_ref) + f(r0_ref)) + f(r1_ref)) + f(r2_ref)

    rspec = lambda k: pl.BlockSpec((None, tr, C), lambda i: (k, i, 0))
    return pl.pallas_call(
        body, name=name, grid=(R // tr,), in_specs=[rspec(3), rspec(0), rspec(1), rspec(2)],
        out_specs=pl.BlockSpec((tr, C), lambda i: (i, 0)), out_shape=_sds((R, C), F32), compiler_params=_params("parallel"),
    )(recv, recv, recv, recv)


def _swap_cores(parts, *, name):
    n = len(parts)

    def body(*refs):
        ins, outs = refs[:n], refs[n:2 * n]
        send_sems, recv_sems = refs[2 * n:]
        x, y, c = _place()
        copies = [pltpu.make_async_remote_copy(src_ref=ins[k], dst_ref=outs[k], send_sem=send_sems.at[k], recv_sem=recv_sems.at[k],
                                               device_id=(x, y, 1 - c), device_id_type=MESH) for k in range(n)]
        for cp in copies:
            cp.start()
        for cp in copies:
            cp.wait()

    any_spec = pl.BlockSpec(memory_space=pl.ANY)
    return pl.pallas_call(
        body, name=name, in_specs=[any_spec] * n, out_specs=[any_spec] * n, out_shape=[_sds(p.shape, p.dtype) for p in parts],
        scratch_shapes=[pltpu.SemaphoreType.DMA((n,)), pltpu.SemaphoreType.DMA((n,))],
    )(*parts)


def _all_reduce_small(v, *, name):
    r, C = v.shape

    def body(v_ref, o_ref, buf, send_sems, recv_sems):
        x, y, c = _place()
        me = 4 * x + 2 * y + c
        buf[me] = v_ref[...]
        peers = []
        for k in range(1, 8):
            kx, ky, kc = (k >> 2) & 1, (k >> 1) & 1, k & 1
            px = 1 - x if kx else x
            py = 1 - y if ky else y
            pc = 1 - c if kc else c
            peers.append((px, py, pc))
        copies = []
        for k, peer in enumerate(peers):
            cp = pltpu.make_async_remote_copy(src_ref=v_ref, dst_ref=buf.at[me], send_sem=send_sems.at[k],
                                              recv_sem=recv_sems.at[me], device_id=peer, device_id_type=MESH)
            cp.start()
            copies.append(cp)
        for k, (px, py, pc) in enumerate(peers):
            src = 4 * px + 2 * py + pc
            pltpu.make_async_remote_copy(src_ref=v_ref, dst_ref=buf.at[src], send_sem=send_sems.at[k],
                                         recv_sem=recv_sems.at[src], device_id=peers[k], device_id_type=MESH).wait_recv()
        for cp in copies:
            cp.wait_send()
        acc = buf[0]
        for d in range(1, 8):
            acc = acc + buf[d]
        o_ref[...] = acc

    vm = pl.BlockSpec(memory_space=pltpu.VMEM)
    return pl.pallas_call(
        body, name=name, in_specs=[vm], out_specs=vm, out_shape=_sds((r, C), F32),
        scratch_shapes=[pltpu.VMEM((8, r, C), F32), pltpu.SemaphoreType.DMA((7,)), pltpu.SemaphoreType.DMA((8,))],
    )(v)


def _adamw(w, m, v, ga, gb, *, name):
    R, C = w.shape
    tr = _row_tile(R, C)
    has_b = gb is not None
    c1 = 1.0 / (1.0 - ADAM_B1 ** ADAM_STEP)
    c2 = 1.0 / (1.0 - ADAM_B2 ** ADAM_STEP)

    def body(*refs):
        if has_b:
            w_ref, m_ref, v_ref, ga_ref, gb_ref, g_ref, d_ref, nm_ref, nv_ref = refs
            g = ga_ref[...] + gb_ref[...]
        else:
            w_ref, m_ref, v_ref, ga_ref, g_ref, d_ref, nm_ref, nv_ref = refs
            g = ga_ref[...]
        nm = ADAM_B1 * m_ref[...] + (1.0 - ADAM_B1) * g
        nv = ADAM_B2 * v_ref[...] + (1.0 - ADAM_B2) * (g * g)
        g_ref[...] = g
        nm_ref[...] = nm
        nv_ref[...] = nv
        d_ref[...] = -ADAM_LR * ((nm * c1) / (jnp.sqrt(nv * c2) + ADAM_EPS) + ADAM_WD * w_ref[...])

    blk = pl.BlockSpec((tr, C), lambda i: (i, 0))
    n_in = 5 if has_b else 4
    args = (w, m, v, ga) + ((gb,) if has_b else ())
    return pl.pallas_call(body, name=name, grid=(R // tr,), in_specs=[blk] * n_in, out_specs=[blk] * 4,
                          out_shape=[_sds((R, C), F32)] * 4, compiler_params=_params("parallel"))(*args)


def _seg_rows(n, cols):
    return -(-n // (16 * cols)) * 16


def _pack(arrays, dtype, cols=PACK_COLS, row_mult=512):
    parts, rows = [], 0
    for a in arrays:
        n = int(np.prod(a.shape))
        r = _seg_rows(n, cols)
        flat = a.reshape(-1).astype(dtype)
        if r * cols != n:
            flat = jnp.pad(flat, (0, r * cols - n))
        parts.append(flat.reshape(r, cols))
        rows += r
    pad = -rows % row_mult
    if pad:
        parts.append(jnp.zeros((pad, cols), dtype))
    return jnp.concatenate(parts, axis=0)


def _unpack(packed, shapes, cols=PACK_COLS):
    out, r0 = [], 0
    for shp in shapes:
        n = int(np.prod(shp))
        used = -(-n // cols)
        out.append(packed[r0:r0 + used].reshape(-1)[:n].reshape(shp))
        r0 += _seg_rows(n, cols)
    return out


def _rope_tables(seq):
    inv = 1.0 / (ROPE_THETA ** (jnp.arange(0, 64, 2, dtype=F32) / 64))
    ang = jnp.arange(seq, dtype=F32)[:, None] * inv[None, :]
    cos, sin = jnp.cos(ang), jnp.sin(ang)
    cos128 = jnp.concatenate([cos, cos, cos, cos], axis=1)
    sin128 = jnp.concatenate([-sin, sin, -sin, sin], axis=1)
    return cos128, sin128


def _ffn_perm(a):
    lead = a.shape[:-1]
    nj = D_FF // FFN_TC
    return jnp.swapaxes(a.reshape(lead + (2, nj, FFN_TC)), -3, -2).reshape(lead + (2 * D_FF,))


def _ffn_unperm(a):
    lead = a.shape[:-1]
    nj = D_FF // FFN_TC
    return jnp.swapaxes(a.reshape(lead + (nj, 2, FFN_TC)), -3, -2).reshape(lead + (2 * D_FF,))


def _mixer_a_fwd(xb, w, j, cos, sin, tag):
    qkv = _mm(xb, w["a_w_qkv"][j], name=f"mm_qkv_{tag}")
    qkv_r = _rope_cols(qkv, cos, sin, (A_HEADS + A_KV_HEADS) * A_HEAD_DIM, name=f"rope_qkv_{tag}")
    o = _swa_fwd(qkv_r, w["a_sinks"][j], name=f"swa_fwd_{tag}")
    y = _mm(o, w["a_w_o"][j], name=f"mm_ao_{tag}")
    return y, (xb, qkv_r, o)


def _mixer_a_bwd(dzb, res, w, j, cos, sin, tag, grads):
    xb, qkv_r, o = res
    do = _mm(dzb, w["a_w_o"][j], tb=True, out_dtype=BF16, name=f"mm_dao_{tag}")
    grads["a_w_o"][j] = _mm(o, dzb, ta=True, out_dtype=BF16, tk=GRAD_TK, name=f"mm_gao_{tag}")
    dq, dcur, dprev, dsink = _swa_bwd(qkv_r, w["a_sinks"][j], do, cos, sin, name=f"swa_bwd_{tag}")
    grads["a_sinks"][j] = dsink[0, :A_HEADS]
    dqkv = _swa_dqkv(dq, dcur, dprev, cos, sin, name=f"swa_dqkv_{tag}")
    grads["a_w_qkv"][j] = _mm(xb, dqkv, ta=True, out_dtype=BF16, tk=GRAD_TK, name=f"mm_gqkv_{tag}")
    return _mm(dqkv, w["a_w_qkv"][j], tb=True, name=f"mm_dxa_{tag}")


def _mixer_b_fwd(xb, w, j, tag):
    xw = _mm(xb, w["b_w_in"][j], name=f"mm_bin_{tag}")
    wri = jnp.concatenate([w["b_w_rgate"][j], w["b_w_igate"][j]], axis=-1)
    y, u, h = _lru_fwd(xw, w["b_conv_w"][j], w["b_conv_b"][j][None], wri, w["b_b_rgate"][j][None],
                       w["b_b_igate"][j][None], w["b_lambda"][j][None], name=f"lru_fwd_{tag}")
    out = _mm(y, w["b_w_o"][j], name=f"mm_bo_{tag}")
    return out, (xb, xw, wri, u, h, y)


def _mixer_b_bwd(dzb, res, w, j, tag, grads):
    xb, xw, wri, u, h, y = res
    dy = _mm(dzb, w["b_w_o"][j], tb=True, out_dtype=BF16, name=f"mm_dbo_{tag}")
    grads["b_w_o"][j] = _mm(y, dzb, ta=True, out_dtype=BF16, tk=GRAD_TK, name=f"mm_gbo_{tag}")
    dxw, dcw, dcb, dwri, dbr, dbi, dlam = _lru_bwd(
        xw, u, h, dy, w["b_conv_w"][j], wri, w["b_b_rgate"][j][None], w["b_b_igate"][j][None], w["b_lambda"][j][None],
        name=f"lru_bwd_{tag}")
    grads["b_conv_w"][j], grads["b_conv_b"][j] = dcw, dcb[0]
    grads["b_w_rgate"][j], grads["b_w_igate"][j] = dwri[..., :LRU_BLOCK_W].astype(BF16), dwri[..., LRU_BLOCK_W:].astype(BF16)
    grads["b_b_rgate"][j], grads["b_b_igate"][j], grads["b_lambda"][j] = dbr[0], dbi[0], dlam[0]
    grads["b_w_in"][j] = _mm(xb, dxw, ta=True, out_dtype=BF16, tk=GRAD_TK, name=f"mm_gbin_{tag}")
    return _mm(dxw, w["b_w_in"][j], tb=True, name=f"mm_dxb_{tag}")


def _mla_weights(w, j):
    H = C_HEADS
    uq = w["c_w_uq"][j].reshape(C_Q_RANK, H, C_NOPE + C_ROPE)
    ukv = w["c_w_ukv"][j].reshape(C_KV_RANK, H, C_NOPE + C_V)
    uq_n = uq[:, :, :C_NOPE].reshape(C_Q_RANK, H * C_NOPE)
    uq_r = uq[:, :, C_NOPE:].reshape(C_Q_RANK, H * C_ROPE)
    ukv_p = jnp.concatenate([ukv[:, :, :C_NOPE].reshape(C_KV_RANK, H * C_NOPE),
                             ukv[:, :, C_NOPE:].reshape(C_KV_RANK, H * C_V)], axis=1)
    return uq_n, uq_r, ukv_p


def _mixer_c_fwd(xb, w, j, cos, sin, tag):
    S = xb.shape[0]
    H = C_HEADS
    uq_n, uq_r, ukv_p = _mla_weights(w, j)
    c = _mm(xb, w["c_w_down"][j], name=f"mm_cdown_{tag}")
    cq, ckv, kr = _mla_pre(c, w["c_q_norm"][j][None], w["c_kv_norm"][j][None], cos, sin, name=f"mla_pre_{tag}")
    qn = _mm(cq, uq_n, out_dtype=BF16, name=f"mm_uqn_{tag}")
    qr_flat = _rope_heads(_mm(cq, uq_r, name=f"mm_uqr_{tag}"), cos, sin, transpose=False, name=f"rope_qr_{tag}")
    qr = jnp.transpose(qr_flat.reshape(S, H, C_ROPE), (1, 0, 2))
    kv = _mm(ckv, ukv_p, out_dtype=BF16, name=f"mm_ukv_{tag}")
    o, lse = _mla_flash_fwd(qn, qr, kv, kr, name=f"mla_fwd_{tag}")
    y = _mm(o, w["c_w_o"][j], name=f"mm_co_{tag}")
    return y, (xb, c, cq, ckv, kr, qn, qr, kv, o, lse, uq_n, uq_r, ukv_p)


def _mixer_c_bwd(dzb, res, w, j, cos, sin, tag, grads):
    xb, c, cq, ckv, kr, qn, qr, kv, o, lse, uq_n, uq_r, ukv_p = res
    S = xb.shape[0]
    H = C_HEADS
    do = _mm(dzb, w["c_w_o"][j], tb=True, out_dtype=BF16, name=f"mm_dco_{tag}")
    grads["c_w_o"][j] = _mm(o, dzb, ta=True, out_dtype=BF16, tk=GRAD_TK, name=f"mm_gco_{tag}")
    delta = _mla_delta(do, o, name=f"mla_delta_{tag}")
    dqn, dqr = _mla_flash_dq(qn, qr, kv, kr, do, lse, delta, name=f"mla_dq_{tag}")
    dkn, dv, dkr_h = _mla_flash_dkv(qn, qr, kv, kr, do, lse, delta, name=f"mla_dkv_{tag}")
    dkv = jnp.concatenate([dkn, dv], axis=1)
    dqr_flat = _rope_heads(jnp.transpose(dqr, (1, 0, 2)).reshape(S, H * C_ROPE), cos, sin, transpose=True, name=f"rope_dqr_{tag}")
    g_uq_n = _mm(cq, dqn, ta=True, out_dtype=BF16, tk=GRAD_TK, name=f"mm_guqn_{tag}")
    g_uq_r = _mm(cq, dqr_flat, ta=True, out_dtype=BF16, tk=GRAD_TK, name=f"mm_guqr_{tag}")
    g_ukv = _mm(ckv, dkv, ta=True, out_dtype=BF16, tk=GRAD_TK, name=f"mm_gukv_{tag}")
    grads["c_w_uq"][j] = jnp.concatenate([g_uq_n.reshape(C_Q_RANK, H, C_NOPE), g_uq_r.reshape(C_Q_RANK, H, C_ROPE)],
                                         axis=2).reshape(C_Q_RANK, H * (C_NOPE + C_ROPE))
    grads["c_w_ukv"][j] = jnp.concatenate([g_ukv[:, :H * C_NOPE].reshape(C_KV_RANK, H, C_NOPE),
                                           g_ukv[:, H * C_NOPE:].reshape(C_KV_RANK, H, C_V)],
                                          axis=2).reshape(C_KV_RANK, H * (C_NOPE + C_V))
    dcq_a = _mm(dqn, uq_n, tb=True, name=f"mm_dcqa_{tag}")
    dcq_b = _mm(dqr_flat, uq_r, tb=True, name=f"mm_dcqb_{tag}")
    dckv = _mm(dkv, ukv_p, tb=True, name=f"mm_dckv_{tag}")
    dc, dqg, dkvg = _mla_post_bwd(c, dcq_a, dcq_b, dckv, dkr_h, w["c_q_norm"][j][None], w["c_kv_norm"][j][None], cos, sin,
                                  name=f"mla_post_{tag}")
    grads["c_q_norm"][j], grads["c_kv_norm"][j] = dqg[0], dkvg[0]
    grads["c_w_down"][j] = _mm(xb, dc, ta=True, out_dtype=BF16, tk=GRAD_TK, name=f"mm_gcdown_{tag}")
    return _mm(dc, w["c_w_down"][j], tb=True, name=f"mm_dxc_{tag}")


def _local_step(x, mem, target, w):
    S = x.shape[0]
    cos, sin = _rope_tables(S)
    grads = {n: [None] * w[n].shape[0] for n in WEIGHTS if n != "mem_w_kv"}
    mkv = _mm(mem, w["mem_w_kv"], out_dtype=BF16, tm=MEM_LEN, name="mm_memkv")

    xs, xb = x, x.astype(BF16)
    saved = []
    for i in range(DEPTH):
        kind, j = i % N_MIXERS, i // N_MIXERS
        tag = f"l{i}"
        if kind == 0:
            y, res = _mixer_a_fwd(xb, w, j, cos, sin, tag)
        elif kind == 1:
            y, res = _mixer_b_fwd(xb, w, j, tag)
        else:
            y, res = _mixer_c_fwd(xb, w, j, cos, sin, tag)
        x1, x1b, xh1, rs1 = _ln_fwd(xs, y, w["ln_g"][i, 0][None], w["ln_b"][i, 0][None], name=f"ln1_{tag}")
        q = _mm(x1b, w["x_w_q"][i], out_dtype=BF16, name=f"mm_xq_{tag}")
        o = _xattn_fwd(q, mkv, name=f"xattn_fwd_{tag}")
        y2 = _mm(o, w["x_w_o"][i], name=f"mm_xo_{tag}")
        x2, x2b, xh2, rs2 = _ln_fwd(x1, y2, w["ln_g"][i, 1][None], w["ln_b"][i, 1][None], name=f"ln2_{tag}")
        w_up = w["f_w_up"][i]
        cwp, cbp = _ffn_perm(w["f_conv_w"][i]), _ffn_perm(w["f_conv_b"][i][None])
        hh = _mm(x2b, w_up, out_dtype=BF16, tn=FFN_TC, name=f"mm_up_{tag}")
        a = _ffn_act_fwd(hh, cwp, cbp, name=f"ffn_act_{tag}")
        y3 = _mm(a, w["f_w_down"][i], name=f"mm_down_{tag}")
        x3, x3b, xh3, rs3 = _ln_fwd(x2, y3, w["ln_g"][i, 2][None], w["ln_b"][i, 2][None], name=f"ln3_{tag}")
        saved.append((res, (xh1, rs1, x1b), (q, o, xh2, rs2, x2b), (w_up, cwp, cbp, hh, a, xh3, rs3)))
        xs, xb = x3, x3b

    d2, loss = _loss_fwd(xs, target, name="loss")
    d1 = None

    dmkv = None
    ln_dg = [[None] * 3 for _ in range(DEPTH)]
    ln_db = [[None] * 3 for _ in range(DEPTH)]
    for i in reversed(range(DEPTH)):
        kind, j = i % N_MIXERS, i // N_MIXERS
        tag = f"l{i}"
        res, (xh1, rs1, x1b), (q, o, xh2, rs2, x2b), (w_up, cwp, cbp, hh, a, xh3, rs3) = saved[i]
        dz3, dz3b, ln_dg[i][2], ln_db[i][2] = _ln_bwd(d1, d2, xh3, rs3, w["ln_g"][i, 2][None], name=f"ln3_bwd_{tag}")
        da = _mm(dz3b, w["f_w_down"][i], tb=True, out_dtype=BF16, name=f"mm_ddown_{tag}")
        grads["f_w_down"][i] = _mm(a, dz3b, ta=True, out_dtype=BF16, tm=FFN_TC, tk=GRAD_TK, name=f"mm_gdown_{tag}")
        dh, dcw, dcb = _ffn_act_bwd(hh, da, cwp, cbp, name=f"ffn_act_bwd_{tag}")
        grads["f_conv_w"][i], grads["f_conv_b"][i] = _ffn_unperm(dcw), _ffn_unperm(dcb)[0]
        grads["f_w_up"][i] = _mm(x2b, dh, ta=True, out_dtype=BF16, tn=FFN_TC, tk=GRAD_TK, name=f"mm_gup_{tag}")
        dx2 = _mm(dh, w_up, tb=True, tm=512, name=f"mm_dxf_{tag}")

        dz2, dz2b, ln_dg[i][1], ln_db[i][1] = _ln_bwd(dz3, dx2, xh2, rs2, w["ln_g"][i, 1][None], name=f"ln2_bwd_{tag}")
        do = _mm(dz2b, w["x_w_o"][i], tb=True, out_dtype=BF16, name=f"mm_dxo_{tag}")
        grads["x_w_o"][i] = _mm(o, dz2b, ta=True, out_dtype=BF16, tk=GRAD_TK, name=f"mm_gxo_{tag}")
        dq, dmkv_i = _xattn_bwd(q, mkv, do, name=f"xattn_bwd_{tag}")
        dmkv = dmkv_i if dmkv is None else dmkv + dmkv_i
        grads["x_w_q"][i] = _mm(x1b, dq, ta=True, out_dtype=BF16, tk=GRAD_TK, name=f"mm_gxq_{tag}")
        dx1 = _mm(dq, w["x_w_q"][i], tb=True, name=f"mm_dxq_{tag}")

        dz1, dz1b, ln_dg[i][0], ln_db[i][0] = _ln_bwd(dz2, dx1, xh1, rs1, w["ln_g"][i, 0][None], name=f"ln1_bwd_{tag}")
        if kind == 0:
            dx0 = _mixer_a_bwd(dz1b, res, w, j, cos, sin, tag, grads)
        elif kind == 1:
            dx0 = _mixer_b_bwd(dz1b, res, w, j, tag, grads)
        else:
            dx0 = _mixer_c_bwd(dz1b, res, w, j, cos, sin, tag, grads)
        d1, d2 = dz1, dx0

    grad_x = _axpy(d1, d2, name="grad_x")
    out = {n: jnp.stack(g, axis=0) for n, g in grads.items() if n not in ("ln_g", "ln_b")}
    out["mem_w_kv"] = _mm(mem, dmkv, ta=True, out_dtype=BF16, tm=512, name="mm_gmemkv")
    out["ln_g"] = jnp.stack([jnp.concatenate(r, axis=0) for r in ln_dg], axis=0)
    out["ln_b"] = jnp.stack([jnp.concatenate(r, axis=0) for r in ln_db], axis=0)
    return loss, grad_x, out


def kernel(x, mem, a_w_qkv, a_sinks, a_w_o, b_w_in, b_conv_w, b_conv_b, b_w_rgate, b_b_rgate, b_w_igate, b_b_igate, b_lambda, b_w_o, c_w_down, c_q_norm, c_kv_norm, c_w_uq, c_w_ukv, c_w_o, mem_w_kv, x_w_q, x_w_o, f_w_up, f_conv_w, f_conv_b, f_w_down, ln_g, ln_b, loss_target, m_a_w_qkv, m_a_sinks, m_a_w_o, m_b_w_in, m_b_conv_w, m_b_conv_b, m_b_w_rgate, m_b_b_rgate, m_b_w_igate, m_b_b_igate, m_b_lambda, m_b_w_o, m_c_w_down, m_c_q_norm, m_c_kv_norm, m_c_w_uq, m_c_w_ukv, m_c_w_o, m_mem_w_kv, m_x_w_q, m_x_w_o, m_f_w_up, m_f_conv_w, m_f_conv_b, m_f_w_down, m_ln_g, m_ln_b, v_a_w_qkv, v_a_sinks, v_a_w_o, v_b_w_in, v_b_conv_w, v_b_conv_b, v_b_w_rgate, v_b_b_rgate, v_b_w_igate, v_b_b_igate, v_b_lambda, v_b_w_o, v_c_w_down, v_c_q_norm, v_c_kv_norm, v_c_w_uq, v_c_w_ukv, v_c_w_o, v_mem_w_kv, v_x_w_q, v_x_w_o, v_f_w_up, v_f_conv_w, v_f_conv_b, v_f_w_down, v_ln_g, v_ln_b):
    loc = locals()
    shard = {n: loc[n] for n in WEIGHTS}
    mom = {n: loc["m_" + n] for n in WEIGHTS}
    var = {n: loc["v_" + n] for n in WEIGHTS}
    names = [n for n, _ in SHARDED]
    axis = dict(SHARDED)
    big, small = names[:N_BIG], names[N_BIG:]

    axes = [axis[n] for n in big]
    swapped = [n == "f_w_up" for n in big]
    chip = 2 * lax.axis_index("x") + lax.axis_index("y")

    w = dict(zip(big, _gather_weights([shard[n].astype(BF16) for n in big], axes, swapped, name="gather_big")))
    got = _all_gather_chips(_pack([shard[n] for n in small], F32), name="gather_small")
    per_chip = [_unpack(got[s], [shard[n].shape for n in small]) for s in range(4)]
    for k, n in enumerate(small):
        w[n] = jnp.concatenate([per_chip[s][k] for s in range(4)], axis=axis[n])
    for n in REPLICATED:
        w[n] = shard[n]

    loss, grad_x, g = _local_step(x[0], mem[0], loss_target[0], w)

    recv = _scatter_grads([g[n] for n in big], axes, swapped, name="scatter_grads")
    view = {n: (int(np.prod(shard[n].shape[:-1])), shard[n].shape[-1]) for n in big}
    parts = [_sum_partials(r.reshape((4,) + view[n]), name=f"sum_{n}") for n, r in zip(big, recv)]
    sibs = _swap_cores(parts, name="swap_cores")
    grad_o, delta_o, m_o, v_o = {}, {}, {}, {}
    for n, part, sib in zip(big, parts, sibs):
        res = _adamw(shard[n].reshape(view[n]), mom[n].reshape(view[n]), var[n].reshape(view[n]), part, sib, name=f"adamw_{n}")
        for d, r in zip((grad_o, delta_o, m_o, v_o), res):
            d[n] = r.reshape(shard[n].shape)

    rest = small + REPLICATED
    vec = _pack([g[n] for n in rest] + [loss], F32, cols=LANES, row_mult=8)
    tot = _unpack(_all_reduce_small(vec, name="allreduce_small"), [g[n].shape for n in rest] + [(1, 1)], cols=LANES)
    loss_tot = tot[-1].reshape(())
    mine = {n: t for n, t in zip(rest, tot)}
    for n in small:
        size = shard[n].shape[axis[n]]
        mine[n] = lax.dynamic_slice_in_dim(mine[n], chip * size, size, axis=axis[n])
    rpack = lambda d: _pack([d[n] for n in rest], F32, cols=LANES, row_mult=8)
    res = _adamw(rpack(shard), rpack(mom), rpack(var), rpack(mine), None, name="adamw_small")
    for d, r in zip((grad_o, delta_o, m_o, v_o), res):
        d.update(dict(zip(rest, _unpack(r, [shard[n].shape for n in rest], cols=LANES))))

    return (loss_tot, grad_x[None], *[grad_o[n] for n in WEIGHTS], *[delta_o[n] for n in WEIGHTS],
            *[m_o[n] for n in WEIGHTS], *[v_o[n] for n in WEIGHTS])
```

```python
import functools
import math

import numpy as np
import jax
import jax.numpy as jnp
from jax import lax
from jax.experimental import pallas as pl
from jax.experimental.pallas import tpu as pltpu

F32 = jnp.float32
BF16 = jnp.bfloat16
MESH = pl.DeviceIdType.MESH

D_MODEL = 1024
DEPTH = 4
N_MIXERS = 3
MEM_LEN = 256
BLOCK = 128
ROPE_THETA = 10000.0
NEG = -1e30
LN_EPS = 1e-5
RMS_EPS = 1e-6
A_HEADS, A_KV_HEADS, A_HEAD_DIM = 16, 4, 64
LRU_BLOCKS, LRU_BLOCK_W, LRU_CONV, LRU_C = 4, 256, 4, 8.0
C_HEADS, C_NOPE, C_ROPE, C_V, C_Q_RANK, C_KV_RANK = 8, 128, 64, 128, 384, 256
X_HEADS, X_HEAD_DIM = 4, 256
D_FF, FFN_CONV = 2816, 3
ALPHA = (2.0 * DEPTH) ** 0.25
ADAM_LR, ADAM_B1, ADAM_B2, ADAM_EPS, ADAM_WD, ADAM_STEP = 0.001, 0.9, 0.999, 1e-08, 0.01, 10

VMEM_LIMIT = 56 * 2 ** 20
LANES = 128
PACK_COLS = 1024
ROW_T = 512
ACT_T = 256
LRU_T = 256
FLASH_T = 512
FFN_TC = 1408
MM_T = 1024
GRAD_TK = 1024

C11 = (((1,), (1,)), ((), ()))
C00 = (((0,), (0,)), ((), ()))

SHARDED = [
    ("a_w_qkv", 2), ("a_w_o", 1), ("b_w_in", 2), ("b_w_rgate", 2), ("b_w_igate", 2), ("b_w_o", 1), ("c_w_down", 1),
    ("c_w_uq", 2), ("c_w_ukv", 2), ("c_w_o", 1), ("mem_w_kv", 1), ("x_w_q", 1), ("x_w_o", 1), ("f_w_up", 2),
    ("f_w_down", 1),
    ("b_conv_w", 2), ("c_q_norm", 1), ("c_kv_norm", 1), ("f_conv_w", 2), ("ln_g", 2), ("ln_b", 2),
]
N_BIG = 15
REPLICATED = ["a_sinks", "b_conv_b", "b_b_rgate", "b_b_igate", "b_lambda", "f_conv_b"]
WEIGHTS = ["a_w_qkv", "a_sinks", "a_w_o", "b_w_in", "b_conv_w", "b_conv_b", "b_w_rgate", "b_b_rgate", "b_w_igate",
           "b_b_igate", "b_lambda", "b_w_o", "c_w_down", "c_q_norm", "c_kv_norm", "c_w_uq", "c_w_ukv", "c_w_o",
           "mem_w_kv", "x_w_q", "x_w_o", "f_w_up", "f_conv_w", "f_conv_b", "f_w_down", "ln_g", "ln_b"]


def _params(*sem):
    return pltpu.CompilerParams(dimension_semantics=sem, vmem_limit_bytes=VMEM_LIMIT)


def _sds(shape, dtype):
    return jax.ShapeDtypeStruct(tuple(shape), dtype)


def _mm(a, b, *, name, ta=False, tb=False, out_dtype=F32, tm=None, tn=None, tk=None):
    (K, M) = a.shape if ta else a.shape[::-1]
    (N, K2) = b.shape if tb else b.shape[::-1]
    assert K == K2, (a.shape, b.shape, ta, tb)
    tm = min(tm or MM_T, M)
    tn = min(tn or N, N)
    tk = min(tk or K, K)
    assert M % tm == 0 and N % tn == 0 and K % tk == 0, (M, N, K, tm, tn, tk)
    nk = K // tk
    use_acc = nk > 1 and out_dtype != F32
    dims = (((0 if ta else 1,), (1 if tb else 0,)), ((), ()))

    def body(a_ref, b_ref, o_ref, *scratch):
        p = lax.dot_general(a_ref[...].astype(BF16), b_ref[...].astype(BF16), dims, preferred_element_type=F32)
        if nk == 1:
            o_ref[...] = p.astype(out_dtype)
        else:
            acc = scratch[0] if use_acc else o_ref
            k = pl.program_id(2)

            @pl.when(k == 0)
            def _():
                acc[...] = p

            @pl.when(k > 0)
            def _():
                acc[...] += p

            if use_acc:
                @pl.when(k == nk - 1)
                def _():
                    o_ref[...] = acc[...].astype(out_dtype)

    a_spec = pl.BlockSpec((tk, tm), lambda i, j, k: (k, i)) if ta else pl.BlockSpec((tm, tk), lambda i, j, k: (i, k))
    b_spec = pl.BlockSpec((tn, tk), lambda i, j, k: (j, k)) if tb else pl.BlockSpec((tk, tn), lambda i, j, k: (k, j))
    return pl.pallas_call(
        body, name=name, grid=(M // tm, N // tn, nk), in_specs=[a_spec, b_spec],
        out_specs=pl.BlockSpec((tm, tn), lambda i, j, k: (i, j)), out_shape=_sds((M, N), out_dtype),
        scratch_shapes=[pltpu.VMEM((tm, tn), F32)] if use_acc else [],
        compiler_params=_params("parallel", "parallel", "arbitrary"),
    )(a, b)


def _shift_down(cur, prev8, d):
    rolled = pltpu.roll(cur, d, 0)
    rid = lax.broadcasted_iota(jnp.int32, prev8.shape, 0)
    head = jnp.where(rid < d, pltpu.roll(prev8, d, 0), rolled[0:8])
    return jnp.concatenate([head, rolled[8:]], axis=0)


def _shift_up(cur, next8, d):
    n = cur.shape[0]
    rolled = pltpu.roll(cur, n - d, 0)
    rid = lax.broadcasted_iota(jnp.int32, next8.shape, 0)
    tail = jnp.where(rid >= 8 - d, pltpu.roll(next8, 8 - d, 0), rolled[n - 8:n])
    return jnp.concatenate([rolled[0:n - 8], tail], axis=0)


def _swap_halves(x):
    w = x.shape[-1]
    if w == 64:
        return jnp.concatenate([x[:, 32:64], x[:, 0:32]], axis=1)
    lane = lax.broadcasted_iota(jnp.int32, x.shape, 1)
    return jnp.where((lane % 64) < 32, pltpu.roll(x, w - 32, 1), pltpu.roll(x, 32, 1))


def _tile_lanes(t, w):
    return t if w == t.shape[-1] else jnp.concatenate([t] * (w // t.shape[-1]), axis=1)


def _rope(x, cos, sin):
    w = x.shape[-1]
    if w == 64:
        cos, sin = cos[:, :64], sin[:, :64]
    else:
        cos, sin = _tile_lanes(cos, w), _tile_lanes(sin, w)
    return x * cos + _swap_halves(x) * sin


def _rope_t(x, cos, sin):
    w = x.shape[-1]
    if w == 64:
        cos, sin = cos[:, :64], sin[:, :64]
    else:
        cos, sin = _tile_lanes(cos, w), _tile_lanes(sin, w)
    return x * cos - _swap_halves(x) * sin


def _sigmoid(x):
    return 1.0 / (1.0 + jnp.exp(-x))


def _gelu_and_grad(x):
    c0, c1 = math.sqrt(2.0 / math.pi), 0.044715
    t = jnp.tanh(c0 * (x + c1 * x * x * x))
    g = 0.5 * x * (1.0 + t)
    dg = 0.5 * (1.0 + t) + 0.5 * x * (1.0 - t * t) * c0 * (1.0 + 3.0 * c1 * x * x)
    return g, dg


def _neg_expm1(x):
    series = -x * (1.0 + x * (0.5 + x * (1.0 / 6.0 + x * (1.0 / 24.0 + x * (1.0 / 120.0)))))
    return jnp.where(x > -0.1, series, 1.0 - jnp.exp(x))


def _softplus_neg(lam):
    z = -lam
    e = jnp.exp(-jnp.abs(z))
    log1p = jnp.where(e < 0.01, e * (1.0 - e * (0.5 - e * (1.0 / 3.0))), jnp.log(1.0 + e))
    sp = jnp.maximum(z, 0.0) + log1p
    dsp = -_sigmoid(z)
    return sp, dsp


def _ln_fwd(x, y, g, b, *, name):
    S, D = x.shape
    tm = min(ROW_T, S)

    def body(x_ref, y_ref, g_ref, b_ref, o_ref, ob_ref, xh_ref, rs_ref):
        z = ALPHA * x_ref[...] + y_ref[...]
        mu = jnp.mean(z, axis=-1, keepdims=True)
        zc = z - mu
        var = jnp.mean(zc * zc, axis=-1, keepdims=True)
        r = lax.rsqrt(var + LN_EPS)
        xh = zc * r
        o = xh * g_ref[...] + b_ref[...]
        o_ref[...] = o
        ob_ref[...] = o.astype(BF16)
        xh_ref[...] = xh
        rs_ref[...] = r

    row = pl.BlockSpec((tm, D), lambda i: (i, 0))
    vec = pl.BlockSpec((1, D), lambda i: (0, 0))
    return pl.pallas_call(
        body, name=name, grid=(S // tm,), in_specs=[row, row, vec, vec],
        out_specs=[row, row, row, pl.BlockSpec((tm, 1), lambda i: (i, 0))],
        out_shape=[_sds((S, D), F32), _sds((S, D), BF16), _sds((S, D), F32), _sds((S, 1), F32)],
        compiler_params=_params("parallel"),
    )(x, y, g, b)


def _ln_bwd(d1, d2, xh, rs, g, *, name):
    S, D = xh.shape
    tm = min(ROW_T, S)
    has_d1 = d1 is not None

    def body(*refs):
        if has_d1:
            d1_ref, d2_ref, xh_ref, rs_ref, g_ref, dz_ref, dzb_ref, dg_ref, db_ref = refs
            dout = ALPHA * d1_ref[...] + d2_ref[...]
        else:
            d2_ref, xh_ref, rs_ref, g_ref, dz_ref, dzb_ref, dg_ref, db_ref = refs
            dout = d2_ref[...]
        xh_v = xh_ref[...]
        dxh = dout * g_ref[...]
        m1 = jnp.mean(dxh, axis=-1, keepdims=True)
        m2 = jnp.mean(dxh * xh_v, axis=-1, keepdims=True)
        dz = rs_ref[...] * (dxh - m1 - xh_v * m2)
        dz_ref[...] = dz
        dzb_ref[...] = dz.astype(BF16)

        @pl.when(pl.program_id(0) == 0)
        def _():
            dg_ref[...] = jnp.zeros_like(dg_ref)
            db_ref[...] = jnp.zeros_like(db_ref)

        dg_ref[...] += jnp.sum(dout * xh_v, axis=0, keepdims=True)
        db_ref[...] += jnp.sum(dout, axis=0, keepdims=True)

    row = pl.BlockSpec((tm, D), lambda i: (i, 0))
    vec = pl.BlockSpec((1, D), lambda i: (0, 0))
    ins = ([row] if has_d1 else []) + [row, row, pl.BlockSpec((tm, 1), lambda i: (i, 0)), vec]
    args = ([d1] if has_d1 else []) + [d2, xh, rs, g]
    return pl.pallas_call(
        body, name=name, grid=(S // tm,), in_specs=ins, out_specs=[row, row, vec, vec],
        out_shape=[_sds((S, D), F32), _sds((S, D), BF16), _sds((1, D), F32), _sds((1, D), F32)],
        compiler_params=_params("arbitrary"),
    )(*args)


def _loss_fwd(y, target, *, name):
    S, D = y.shape
    tm = min(ROW_T, S)

    def body(y_ref, t_ref, d_ref, l_ref):
        e = y_ref[...] - t_ref[...]
        d_ref[...] = e * (1.0 / D)

        @pl.when(pl.program_id(0) == 0)
        def _():
            l_ref[...] = jnp.zeros_like(l_ref)

        part = jnp.sum(e * e, axis=0, keepdims=True)
        l_ref[...] += (0.5 / D) * jnp.sum(part, axis=1, keepdims=True)

    row = pl.BlockSpec((tm, D), lambda i: (i, 0))
    return pl.pallas_call(
        body, name=name, grid=(S // tm,), in_specs=[row, row],
        out_specs=[row, pl.BlockSpec((1, 1), lambda i: (0, 0))], out_shape=[_sds((S, D), F32), _sds((1, 1), F32)],
        compiler_params=_params("arbitrary"),
    )(y, target)


def _axpy(d1, d2, *, name):
    S, D = d1.shape
    tm = min(ROW_T, S)

    def body(a_ref, b_ref, o_ref):
        o_ref[...] = ALPHA * a_ref[...] + b_ref[...]

    row = pl.BlockSpec((tm, D), lambda i: (i, 0))
    return pl.pallas_call(body, name=name, grid=(S // tm,), in_specs=[row, row], out_specs=row,
                          out_shape=_sds((S, D), F32), compiler_params=_params("parallel"))(d1, d2)


def _ffn_act_fwd(h, cw, cb, *, name):
    S, W = h.shape
    tc = FFN_TC
    nj = W // (2 * tc)
    tm = min(ACT_T, S)

    def body(h_ref, w_ref, b_ref, a_ref, carry):
        @pl.when(pl.program_id(1) == 0)
        def _():
            carry[...] = jnp.zeros_like(carry)

        cur = h_ref[...].astype(F32)
        prev8 = carry[...]
        hc = cur * w_ref[2:3, :] + _shift_down(cur, prev8, 1) * w_ref[1:2, :] + _shift_down(cur, prev8, 2) * w_ref[0:1, :]
        hc = hc + b_ref[...]
        carry[...] = cur[tm - 8:tm]
        hg, hu = hc[:, :tc], hc[:, tc:]
        a_ref[...] = (hg * _sigmoid(hg) * hu).astype(BF16)

    return pl.pallas_call(
        body, name=name, grid=(nj, S // tm),
        in_specs=[pl.BlockSpec((tm, 2 * tc), lambda j, i: (i, j)), pl.BlockSpec((3, 2 * tc), lambda j, i: (0, j)),
                  pl.BlockSpec((1, 2 * tc), lambda j, i: (0, j))],
        out_specs=pl.BlockSpec((tm, tc), lambda j, i: (i, j)), out_shape=_sds((S, W // 2), BF16),
        scratch_shapes=[pltpu.VMEM((8, 2 * tc), F32)],
        compiler_params=_params("parallel", "arbitrary"),
    )(h, cw, cb)


def _ffn_act_bwd(h, da, cw, cb, *, name):
    S, W = h.shape
    tc = FFN_TC
    nj = W // (2 * tc)
    tm = min(ACT_T, S)
    ni = S // tm

    def body(h_ref, hp_ref, da_ref, w_ref, b_ref, dh_ref, dw_ref, db_ref, carry):
        i = pl.program_id(1)
        r = ni - 1 - i

        @pl.when(i == 0)
        def _():
            carry[...] = jnp.zeros_like(carry)
            dw_ref[...] = jnp.zeros_like(dw_ref)
            db_ref[...] = jnp.zeros_like(db_ref)

        cur = h_ref[...].astype(F32)
        prev8 = jnp.where(r > 0, hp_ref[8:16, :].astype(F32), 0.0)
        sh = [cur, _shift_down(cur, prev8, 1), _shift_down(cur, prev8, 2)]
        hc = sh[0] * w_ref[2:3, :] + sh[1] * w_ref[1:2, :] + sh[2] * w_ref[0:1, :] + b_ref[...]
        hg, hu = hc[:, :tc], hc[:, tc:]
        d = da_ref[...].astype(F32)
        sg = _sigmoid(hg)
        dg = d * hu * (sg * (1.0 + hg * (1.0 - sg)))
        du = d * (hg * sg)
        dhc = jnp.concatenate([dg, du], axis=1)
        db_ref[...] += jnp.sum(dhc, axis=0, keepdims=True)
        for k in range(3):
            dw_ref[k:k + 1, :] += jnp.sum(dhc * sh[2 - k], axis=0, keepdims=True)
        next8 = carry[...]
        dh = dhc * w_ref[2:3, :] + _shift_up(dhc, next8, 1) * w_ref[1:2, :] + _shift_up(dhc, next8, 2) * w_ref[0:1, :]
        carry[...] = dhc[0:8]
        dh_ref[...] = dh.astype(BF16)

    rev = lambda j, i: (ni - 1 - i, j)
    return pl.pallas_call(
        body, name=name, grid=(nj, ni),
        in_specs=[pl.BlockSpec((tm, 2 * tc), rev),
                  pl.BlockSpec((16, 2 * tc), lambda j, i: (jnp.maximum((ni - 1 - i) * (tm // 16) - 1, 0), j)),
                  pl.BlockSpec((tm, tc), rev), pl.BlockSpec((3, 2 * tc), lambda j, i: (0, j)),
                  pl.BlockSpec((1, 2 * tc), lambda j, i: (0, j))],
        out_specs=[pl.BlockSpec((tm, 2 * tc), rev), pl.BlockSpec((3, 2 * tc), lambda j, i: (0, j)),
                   pl.BlockSpec((1, 2 * tc), lambda j, i: (0, j))],
        out_shape=[_sds((S, W), BF16), _sds((3, W), F32), _sds((1, W), F32)],
        scratch_shapes=[pltpu.VMEM((8, 2 * tc), F32)],
        compiler_params=_params("parallel", "arbitrary"),
    )(h, h, da, cw, cb)


def _xattn_probs(q, k):
    s = lax.dot_general(q, k, C11, preferred_element_type=F32) * (X_HEAD_DIM ** -0.5)
    p = jnp.exp(s - jnp.max(s, axis=-1, keepdims=True))
    return p / jnp.sum(p, axis=-1, keepdims=True)


def _xattn_fwd(q, mkv, *, name):
    S, D = q.shape
    tm = min(ROW_T, S)

    def body(q_ref, k_ref, v_ref, o_ref):
        for h in range(X_HEADS):
            sl = slice(h * X_HEAD_DIM, (h + 1) * X_HEAD_DIM)
            p = _xattn_probs(q_ref[:, sl], k_ref[:, sl])
            o_ref[:, sl] = jnp.dot(p.astype(BF16), v_ref[:, sl], preferred_element_type=F32).astype(BF16)

    return pl.pallas_call(
        body, name=name, grid=(S // tm,),
        in_specs=[pl.BlockSpec((tm, D), lambda i: (i, 0)), pl.BlockSpec((MEM_LEN, D), lambda i: (0, 0)),
                  pl.BlockSpec((MEM_LEN, D), lambda i: (0, 1))],
        out_specs=pl.BlockSpec((tm, D), lambda i: (i, 0)), out_shape=_sds((S, D), BF16),
        compiler_params=_params("parallel"),
    )(q, mkv, mkv)


def _xattn_bwd(q, mkv, do, *, name):
    S, D = q.shape
    tm = min(ROW_T, S)
    scale = X_HEAD_DIM ** -0.5

    def body(q_ref, k_ref, v_ref, do_ref, dq_ref, dkv_ref):
        @pl.when(pl.program_id(0) == 0)
        def _():
            dkv_ref[...] = jnp.zeros_like(dkv_ref)

        for h in range(X_HEADS):
            sl = slice(h * X_HEAD_DIM, (h + 1) * X_HEAD_DIM)
            sv = slice(D + h * X_HEAD_DIM, D + (h + 1) * X_HEAD_DIM)
            qh, kh, vh, doh = q_ref[:, sl], k_ref[:, sl], v_ref[:, sl], do_ref[:, sl]
            p = _xattn_probs(qh, kh)
            dp = lax.dot_general(doh, vh, C11, preferred_element_type=F32)
            ds = (p * (dp - jnp.sum(p * dp, axis=-1, keepdims=True)) * scale).astype(BF16)
            dq_ref[:, sl] = jnp.dot(ds, kh, preferred_element_type=F32).astype(BF16)
            dkv_ref[:, sl] += lax.dot_general(ds, qh, C00, preferred_element_type=F32)
            dkv_ref[:, sv] += lax.dot_general(p.astype(BF16), doh, C00, preferred_element_type=F32)

    row = pl.BlockSpec((tm, D), lambda i: (i, 0))
    return pl.pallas_call(
        body, name=name, grid=(S // tm,),
        in_specs=[row, pl.BlockSpec((MEM_LEN, D), lambda i: (0, 0)), pl.BlockSpec((MEM_LEN, D), lambda i: (0, 1)), row],
        out_specs=[row, pl.BlockSpec((MEM_LEN, 2 * D), lambda i: (0, 0))],
        out_shape=[_sds((S, D), BF16), _sds((MEM_LEN, 2 * D), F32)],
        compiler_params=_params("arbitrary"),
    )(q, mkv, mkv, do)


def _rope_cols(x, cos, sin, n_rope, *, name):
    S, W = x.shape
    tm = min(ROW_T, S)

    def body(x_ref, c_ref, s_ref, o_ref):
        o_ref[:, :n_rope] = _rope(x_ref[:, :n_rope], c_ref[...], s_ref[...]).astype(BF16)
        if n_rope < W:
            o_ref[:, n_rope:] = x_ref[:, n_rope:].astype(BF16)

    row = pl.BlockSpec((tm, W), lambda i: (i, 0))
    tab = pl.BlockSpec((tm, LANES), lambda i: (i, 0))
    return pl.pallas_call(body, name=name, grid=(S // tm,), in_specs=[row, tab, tab], out_specs=row,
                          out_shape=_sds((S, W), BF16), compiler_params=_params("parallel"))(x, cos, sin)


def _swa_band(n):
    qi = lax.broadcasted_iota(jnp.int32, (BLOCK, 2 * BLOCK), 0)
    kj = lax.broadcasted_iota(jnp.int32, (BLOCK, 2 * BLOCK), 1)
    first = jnp.where(n > 0, 0, BLOCK)
    return ((kj < BLOCK) & (kj > qi + first)) | ((kj >= BLOCK) & (kj - BLOCK <= qi))


def _swa_probs(q, k, band, sink):
    s = lax.dot_general(q, k, C11, preferred_element_type=F32) * (A_HEAD_DIM ** -0.5)
    s = jnp.where(band, s, NEG)
    m = jnp.maximum(jnp.max(s, axis=-1, keepdims=True), sink)
    p = jnp.exp(s - m)
    e_sink = jnp.exp(sink - m)
    den = jnp.sum(p, axis=-1, keepdims=True) + e_sink
    return p / den, e_sink / den


def _swa_specs():
    nq, nkv = A_HEADS * A_HEAD_DIM, A_KV_HEADS * A_HEAD_DIM
    kb, vb = nq // nkv, nq // nkv + 1
    prev = lambda n: jnp.maximum(n - 1, 0)
    return [pl.BlockSpec((BLOCK, nq), lambda n: (n, 0)),
            pl.BlockSpec((BLOCK, nkv), lambda n: (n, kb)), pl.BlockSpec((BLOCK, nkv), lambda n: (prev(n), kb)),
            pl.BlockSpec((BLOCK, nkv), lambda n: (n, vb)), pl.BlockSpec((BLOCK, nkv), lambda n: (prev(n), vb)),
            pl.BlockSpec(memory_space=pltpu.SMEM)]


def _swa_fwd(qkv, sinks, *, name):
    S = qkv.shape[0]
    hd, grp = A_HEAD_DIM, A_HEADS // A_KV_HEADS

    def body(q_ref, kc_ref, kp_ref, vc_ref, vp_ref, sink_ref, o_ref):
        band = _swa_band(pl.program_id(0))
        qa, kc, kp, vc, vp = q_ref[...], kc_ref[...], kp_ref[...], vc_ref[...], vp_ref[...]
        for hk in range(A_KV_HEADS):
            ks = slice(hk * hd, (hk + 1) * hd)
            k = jnp.concatenate([kp[:, ks], kc[:, ks]], axis=0)
            v = jnp.concatenate([vp[:, ks], vc[:, ks]], axis=0)
            for gi in range(grp):
                h = hk * grp + gi
                p, _ = _swa_probs(qa[:, h * hd:(h + 1) * hd], k, band, sink_ref[h])
                o_ref[:, h * hd:(h + 1) * hd] = jnp.dot(p.astype(BF16), v, preferred_element_type=F32).astype(BF16)

    return pl.pallas_call(
        body, name=name, grid=(S // BLOCK,), in_specs=_swa_specs(),
        out_specs=pl.BlockSpec((BLOCK, A_HEADS * hd), lambda n: (n, 0)), out_shape=_sds((S, A_HEADS * hd), BF16),
        compiler_params=_params("parallel"),
    )(qkv, qkv, qkv, qkv, qkv, sinks)


def _swa_bwd(qkv, sinks, do, cos, sin, *, name):
    S = qkv.shape[0]
    hd, grp = A_HEAD_DIM, A_HEADS // A_KV_HEADS
    nq, nkv = A_HEADS * hd, A_KV_HEADS * hd
    scale = hd ** -0.5

    def body(q_ref, kc_ref, kp_ref, vc_ref, vp_ref, sink_ref, do_ref, c_ref, s_ref, dq_ref, dc_ref, dp_ref, ds_ref, dq_s):
        @pl.when(pl.program_id(0) == 0)
        def _():
            ds_ref[...] = jnp.zeros_like(ds_ref)

        band = _swa_band(pl.program_id(0))
        lane = lax.broadcasted_iota(jnp.int32, (1, LANES), 1)
        qa, kc, kp, vc, vp, doa = q_ref[...], kc_ref[...], kp_ref[...], vc_ref[...], vp_ref[...], do_ref[...]
        dsink = jnp.zeros((1, LANES), F32)
        for hk in range(A_KV_HEADS):
            ks = slice(hk * hd, (hk + 1) * hd)
            k = jnp.concatenate([kp[:, ks], kc[:, ks]], axis=0)
            v = jnp.concatenate([vp[:, ks], vc[:, ks]], axis=0)
            dk = jnp.zeros((2 * BLOCK, hd), F32)
            dv = jnp.zeros((2 * BLOCK, hd), F32)
            for gi in range(grp):
                h = hk * grp + gi
                hs = slice(h * hd, (h + 1) * hd)
                qh, doh = qa[:, hs], doa[:, hs]
                p, p_sink = _swa_probs(qh, k, band, sink_ref[h])
                dpr = lax.dot_general(doh, v, C11, preferred_element_type=F32)
                delta = jnp.sum(p * dpr, axis=-1, keepdims=True)
                dsc = (p * (dpr - delta) * scale).astype(BF16)
                dq_s[:, hs] = jnp.dot(dsc, k, preferred_element_type=F32)
                dk = dk + lax.dot_general(dsc, qh, C00, preferred_element_type=F32)
                dv = dv + lax.dot_general(p.astype(BF16), doh, C00, preferred_element_type=F32)
                dsink = dsink + jnp.where(lane == h, -jnp.sum(p_sink * delta, axis=0, keepdims=True), 0.0)
            dp_ref[:, ks] = dk[:BLOCK]
            dc_ref[:, ks] = dk[BLOCK:]
            dp_ref[:, nkv + hk * hd:nkv + (hk + 1) * hd] = dv[:BLOCK]
            dc_ref[:, nkv + hk * hd:nkv + (hk + 1) * hd] = dv[BLOCK:]
        ds_ref[...] += dsink
        dq_ref[...] = _rope_t(dq_s[...], c_ref[...], s_ref[...]).astype(BF16)

    tab = pl.BlockSpec((BLOCK, LANES), lambda n: (n, 0))
    blk = lambda w: pl.BlockSpec((BLOCK, w), lambda n: (n, 0))
    return pl.pallas_call(
        body, name=name, grid=(S // BLOCK,), in_specs=_swa_specs() + [blk(nq), tab, tab],
        out_specs=[blk(nq), blk(2 * nkv), blk(2 * nkv), pl.BlockSpec((1, LANES), lambda n: (0, 0))],
        out_shape=[_sds((S, nq), BF16), _sds((S, 2 * nkv), F32), _sds((S, 2 * nkv), F32), _sds((1, LANES), F32)],
        scratch_shapes=[pltpu.VMEM((BLOCK, nq), F32)],
        compiler_params=_params("arbitrary"),
    )(qkv, qkv, qkv, qkv, qkv, sinks, do, cos, sin)


def _swa_dqkv(dq, dcur, dprev, cos, sin, *, name):
    S, nq = dq.shape
    nkv = dcur.shape[1] // 2
    nb = S // BLOCK

    def body(dq_ref, dc_ref, dp_ref, c_ref, s_ref, o_ref):
        o_ref[:, :nq] = dq_ref[...]
        d = dc_ref[...] + jnp.where(pl.program_id(0) < nb - 1, dp_ref[...], 0.0)
        o_ref[:, nq:nq + nkv] = _rope_t(d[:, :nkv], c_ref[...], s_ref[...]).astype(BF16)
        o_ref[:, nq + nkv:] = d[:, nkv:].astype(BF16)

    tab = pl.BlockSpec((BLOCK, LANES), lambda m: (m, 0))
    blk = lambda w: pl.BlockSpec((BLOCK, w), lambda m: (m, 0))
    return pl.pallas_call(
        body, name=name, grid=(nb,),
        in_specs=[blk(nq), blk(2 * nkv), pl.BlockSpec((BLOCK, 2 * nkv), lambda m: (jnp.minimum(m + 1, nb - 1), 0)), tab, tab],
        out_specs=blk(nq + 2 * nkv), out_shape=_sds((S, nq + 2 * nkv), BF16), compiler_params=_params("parallel"),
    )(dq, dcur, dprev, cos, sin)


def _lru_gates(u, wri_ref, br, bi, sp):
    ub = u.astype(BF16)
    rs, igs = [], []
    for hb in range(LRU_BLOCKS):
        sl = slice(hb * LRU_BLOCK_W, (hb + 1) * LRU_BLOCK_W)
        ri = jnp.dot(ub[:, sl], wri_ref[hb], preferred_element_type=F32)
        rs.append(ri[:, :LRU_BLOCK_W])
        igs.append(ri[:, LRU_BLOCK_W:])
    r = _sigmoid(jnp.concatenate(rs, axis=1) + br)
    ig = _sigmoid(jnp.concatenate(igs, axis=1) + bi)
    la = -LRU_C * r * sp
    a = jnp.exp(la)
    sq = jnp.sqrt(_neg_expm1(2.0 * la))
    return r, ig, a, sq


def _lru_fwd(xw, cw, cb, wri, br, bi, lam, *, name):
    S = xw.shape[0]
    W = D_MODEL
    tm = min(LRU_T, S)

    def body(gate_ref, up_ref, cw_ref, cb_ref, wri_ref, br_ref, bi_ref, lam_ref, y_ref, u_ref, h_ref, cu, ch, a_s, b_s):
        @pl.when(pl.program_id(0) == 0)
        def _():
            cu[...] = jnp.zeros_like(cu)
            ch[...] = jnp.zeros_like(ch)

        up = up_ref[...]
        prev8 = cu[...]
        u = up * cw_ref[3:4, :] + cb_ref[...]
        for d in range(1, LRU_CONV):
            u = u + _shift_down(up, prev8, d) * cw_ref[3 - d:4 - d, :]
        cu[...] = up[tm - 8:tm]
        u_ref[...] = u
        sp, _ = _softplus_neg(lam_ref[...])
        _, ig, a, sq = _lru_gates(u, wri_ref, br_ref[...], bi_ref[...], sp)
        a_s[...] = a
        b_s[...] = sq * (ig * u)
        rid = lax.broadcasted_iota(jnp.int32, (8, W), 0)

        def tile(t, h):
            r0 = pl.multiple_of(t * 8, 8)
            at, bt = a_s[pl.ds(r0, 8), :], b_s[pl.ds(r0, 8), :]
            out = jnp.zeros((8, W), F32)
            for j in range(8):
                h = at[j:j + 1, :] * h + bt[j:j + 1, :]
                out = jnp.where(rid == j, h, out)
            h_ref[pl.ds(r0, 8), :] = out
            return h

        ch[0:1, :] = lax.fori_loop(0, tm // 8, tile, ch[0:1, :])
        g, _ = _gelu_and_grad(gate_ref[...])
        y_ref[...] = (h_ref[...] * g).astype(BF16)

    row = pl.BlockSpec((tm, W), lambda i: (i, 0))
    full = lambda shape: pl.BlockSpec(shape, lambda i: (0,) * len(shape))
    return pl.pallas_call(
        body, name=name, grid=(S // tm,),
        in_specs=[row, pl.BlockSpec((tm, W), lambda i: (i, 1)), full((LRU_CONV, W)), full((1, W)),
                  full((LRU_BLOCKS, LRU_BLOCK_W, 2 * LRU_BLOCK_W)), full((1, W)), full((1, W)), full((1, W))],
        out_specs=[row, row, row], out_shape=[_sds((S, W), BF16), _sds((S, W), F32), _sds((S, W), F32)],
        scratch_shapes=[pltpu.VMEM((8, W), F32), pltpu.VMEM((8, W), F32), pltpu.VMEM((tm, W), F32), pltpu.VMEM((tm, W), F32)],
        compiler_params=_params("arbitrary"),
    )(xw, xw, cw, cb, wri, br, bi, lam)


def _lru_bwd(xw, u, h, dy, cw, wri, br, bi, lam, *, name):
    S = xw.shape[0]
    W = D_MODEL
    tm = min(LRU_T, S)
    nb = S // tm

    def body(gate_ref, up_ref, upp_ref, u_ref, h_ref, hp_ref, dy_ref, cw_ref, wri_ref, br_ref, bi_ref, lam_ref,
             dxw_ref, dcw_ref, dcb_ref, dwri_ref, dbr_ref, dbi_ref, dlam_ref, cg, cdu, a_s, d_s, g_s):
        i = pl.program_id(0)
        r_blk = nb - 1 - i

        @pl.when(i == 0)
        def _():
            cg[...] = jnp.zeros_like(cg)
            cdu[...] = jnp.zeros_like(cdu)
            for ref in (dcw_ref, dcb_ref, dwri_ref, dbr_ref, dbi_ref, dlam_ref):
                ref[...] = jnp.zeros_like(ref)

        u = u_ref[...]
        hv = h_ref[...]
        sp, dsp = _softplus_neg(lam_ref[...])
        r, ig, a, sq = _lru_gates(u, wri_ref, br_ref[...], bi_ref[...], sp)
        dy = dy_ref[...].astype(F32)
        g, dgelu = _gelu_and_grad(gate_ref[...])
        dxw_ref[:, :W] = (dy * hv * dgelu).astype(BF16)
        a_s[...] = a
        d_s[...] = dy * g
        rid = lax.broadcasted_iota(jnp.int32, (8, W), 0)

        def tile(t, c):
            r0 = pl.multiple_of((tm // 8 - 1 - t) * 8, 8)
            at, dt = a_s[pl.ds(r0, 8), :], d_s[pl.ds(r0, 8), :]
            out = jnp.zeros((8, W), F32)
            for j in range(7, -1, -1):
                gt = dt[j:j + 1, :] + c
                c = at[j:j + 1, :] * gt
                out = jnp.where(rid == j, gt, out)
            g_s[pl.ds(r0, 8), :] = out
            return c

        cg[0:1, :] = lax.fori_loop(0, tm // 8, tile, cg[0:1, :])
        gt = g_s[...]
        hprev8 = jnp.where(r_blk > 0, hp_ref[...], 0.0)
        da = gt * _shift_down(hv, hprev8, 1)
        iu = ig * u
        d_iu = gt * sq
        dla = da * a - (gt * iu) * (a * a) / sq
        dlam_ref[...] += jnp.sum(dla * r, axis=0, keepdims=True) * (-LRU_C) * dsp
        dr_pre = dla * (-LRU_C) * sp * r * (1.0 - r)
        di_pre = d_iu * u * ig * (1.0 - ig)
        dbr_ref[...] += jnp.sum(dr_pre, axis=0, keepdims=True)
        dbi_ref[...] += jnp.sum(di_pre, axis=0, keepdims=True)
        ub = u.astype(BF16)
        dus = []
        for hb in range(LRU_BLOCKS):
            sl = slice(hb * LRU_BLOCK_W, (hb + 1) * LRU_BLOCK_W)
            dri = jnp.concatenate([dr_pre[:, sl], di_pre[:, sl]], axis=1).astype(BF16)
            dus.append(lax.dot_general(dri, wri_ref[hb], C11, preferred_element_type=F32))
            dwri_ref[hb] += lax.dot_general(ub[:, sl], dri, C00, preferred_element_type=F32)
        du = d_iu * ig + jnp.concatenate(dus, axis=1)
        dcb_ref[...] += jnp.sum(du, axis=0, keepdims=True)
        up = up_ref[...]
        upprev8 = jnp.where(r_blk > 0, upp_ref[...], 0.0)
        dcw_ref[3:4, :] += jnp.sum(du * up, axis=0, keepdims=True)
        for d in range(1, LRU_CONV):
            dcw_ref[3 - d:4 - d, :] += jnp.sum(du * _shift_down(up, upprev8, d), axis=0, keepdims=True)
        next8 = cdu[...]
        dup = du * cw_ref[3:4, :]
        for d in range(1, LRU_CONV):
            dup = dup + _shift_up(du, next8, d) * cw_ref[3 - d:4 - d, :]
        cdu[...] = du[0:8]
        dxw_ref[:, W:] = dup.astype(BF16)

    rev = lambda c: (lambda i: (nb - 1 - i, c))
    halo = lambda c: (lambda i: (jnp.maximum((nb - 1 - i) * (tm // 8) - 1, 0), c))
    full = lambda shape: pl.BlockSpec(shape, lambda i: (0,) * len(shape))
    vec = full((1, W))
    return pl.pallas_call(
        body, name=name, grid=(nb,),
        in_specs=[pl.BlockSpec((tm, W), rev(0)), pl.BlockSpec((tm, W), rev(1)), pl.BlockSpec((8, W), halo(1)),
                  pl.BlockSpec((tm, W), rev(0)), pl.BlockSpec((tm, W), rev(0)), pl.BlockSpec((8, W), halo(0)),
                  pl.BlockSpec((tm, W), rev(0)), full((LRU_CONV, W)), full((LRU_BLOCKS, LRU_BLOCK_W, 2 * LRU_BLOCK_W)),
                  vec, vec, vec],
        out_specs=[pl.BlockSpec((tm, 2 * W), rev(0)), full((LRU_CONV, W)), vec,
                   full((LRU_BLOCKS, LRU_BLOCK_W, 2 * LRU_BLOCK_W)), vec, vec, vec],
        out_shape=[_sds((S, 2 * W), BF16), _sds((LRU_CONV, W), F32), _sds((1, W), F32),
                   _sds((LRU_BLOCKS, LRU_BLOCK_W, 2 * LRU_BLOCK_W), F32), _sds((1, W), F32), _sds((1, W), F32),
                   _sds((1, W), F32)],
        scratch_shapes=[pltpu.VMEM((8, W), F32), pltpu.VMEM((8, W), F32), pltpu.VMEM((tm, W), F32),
                        pltpu.VMEM((tm, W), F32), pltpu.VMEM((tm, W), F32)],
        compiler_params=_params("arbitrary"),
    )(xw, xw, xw, u, h, h, dy, cw, wri, br, bi, lam)


def _rms(x, g):
    r = lax.rsqrt(jnp.mean(x * x, axis=-1, keepdims=True) + RMS_EPS)
    return x * r * g, r


def _mla_pre(c, qg, kvg, cos, sin, *, name):
    S = c.shape[0]
    tm = min(ROW_T, S)
    q0, k0 = C_Q_RANK, C_Q_RANK + C_KV_RANK

    def body(c_ref, qg_ref, kvg_ref, cs_ref, sn_ref, cq_ref, ckv_ref, kr_ref):
        cq_ref[...] = _rms(c_ref[:, :q0], qg_ref[...])[0].astype(BF16)
        ckv_ref[...] = _rms(c_ref[:, q0:k0], kvg_ref[...])[0].astype(BF16)
        kr_ref[...] = _rope(c_ref[:, k0:], cs_ref[...], sn_ref[...]).astype(BF16)

    blk = lambda w: pl.BlockSpec((tm, w), lambda i: (i, 0))
    vec = lambda w: pl.BlockSpec((1, w), lambda i: (0, 0))
    return pl.pallas_call(
        body, name=name, grid=(S // tm,),
        in_specs=[blk(c.shape[1]), vec(C_Q_RANK), vec(C_KV_RANK), blk(LANES), blk(LANES)],
        out_specs=[blk(C_Q_RANK), blk(C_KV_RANK), blk(C_ROPE)],
        out_shape=[_sds((S, C_Q_RANK), BF16), _sds((S, C_KV_RANK), BF16), _sds((S, C_ROPE), BF16)],
        compiler_params=_params("parallel"),
    )(c, qg, kvg, cos, sin)


def _mla_post_bwd(c, dcq_a, dcq_b, dckv, dkr_h, qg, kvg, cos, sin, *, name):
    S = c.shape[0]
    tm = min(ROW_T, S)
    q0, k0 = C_Q_RANK, C_Q_RANK + C_KV_RANK

    def rms_bwd(x, g, dy):
        r = lax.rsqrt(jnp.mean(x * x, axis=-1, keepdims=True) + RMS_EPS)
        uu = dy * g
        dx = r * uu - x * (r * r * r) * jnp.mean(uu * x, axis=-1, keepdims=True)
        return dx, jnp.sum(dy * x * r, axis=0, keepdims=True)

    def body(c_ref, da_ref, db_ref, dkv_ref, dkr_ref, qg_ref, kvg_ref, cs_ref, sn_ref, dc_ref, dqg_ref, dkvg_ref):
        @pl.when(pl.program_id(0) == 0)
        def _():
            dqg_ref[...] = jnp.zeros_like(dqg_ref)
            dkvg_ref[...] = jnp.zeros_like(dkvg_ref)

        dx, dg = rms_bwd(c_ref[:, :q0], qg_ref[...], da_ref[...] + db_ref[...])
        dc_ref[:, :q0] = dx.astype(BF16)
        dqg_ref[...] += dg
        dx, dg = rms_bwd(c_ref[:, q0:k0], kvg_ref[...], dkv_ref[...])
        dc_ref[:, q0:k0] = dx.astype(BF16)
        dkvg_ref[...] += dg
        dkr = dkr_ref[0]
        for hh in range(1, C_HEADS):
            dkr = dkr + dkr_ref[hh]
        dc_ref[:, k0:] = _rope_t(dkr, cs_ref[...], sn_ref[...]).astype(BF16)

    blk = lambda w: pl.BlockSpec((tm, w), lambda i: (i, 0))
    vec = lambda w: pl.BlockSpec((1, w), lambda i: (0, 0))
    return pl.pallas_call(
        body, name=name, grid=(S // tm,),
        in_specs=[blk(c.shape[1]), blk(C_Q_RANK), blk(C_Q_RANK), blk(C_KV_RANK),
                  pl.BlockSpec((C_HEADS, tm, C_ROPE), lambda i: (0, i, 0)), vec(C_Q_RANK), vec(C_KV_RANK), blk(LANES), blk(LANES)],
        out_specs=[blk(c.shape[1]), vec(C_Q_RANK), vec(C_KV_RANK)],
        out_shape=[_sds(c.shape, BF16), _sds((1, C_Q_RANK), F32), _sds((1, C_KV_RANK), F32)],
        compiler_params=_params("arbitrary"),
    )(c, dcq_a, dcq_b, dckv, dkr_h, qg, kvg, cos, sin)


def _rope_heads(x, cos, sin, *, transpose, name):
    S, W = x.shape
    tm = min(ROW_T, S)
    fn = _rope_t if transpose else _rope

    def body(x_ref, c_ref, s_ref, o_ref):
        o_ref[...] = fn(x_ref[...].astype(F32), c_ref[...], s_ref[...]).astype(BF16)

    row = pl.BlockSpec((tm, W), lambda i: (i, 0))
    tab = pl.BlockSpec((tm, LANES), lambda i: (i, 0))
    return pl.pallas_call(body, name=name, grid=(S // tm,), in_specs=[row, tab, tab], out_specs=row,
                          out_shape=_sds((S, W), BF16), compiler_params=_params("parallel"))(x, cos, sin)


MLA_GROUP = 2
MLA_SCALE = (C_NOPE + C_ROPE) ** -0.5
LOG2E = 1.4426950408889634


def _mla_scores2(qn, qr, kn, kr, diagonal):
    s = lax.dot_general(qn, kn, C11, preferred_element_type=F32) + lax.dot_general(qr, kr, C11, preferred_element_type=F32)
    s = s * (MLA_SCALE * LOG2E)
    if diagonal:
        row = lax.broadcasted_iota(jnp.int32, s.shape, 0)
        col = lax.broadcasted_iota(jnp.int32, s.shape, 1)
        s = jnp.where(col <= row, s, NEG)
    return s


def _causal_pairs(n, query_major):
    if query_major:
        pairs = [(i, j) for i in range(n) for j in range(i + 1)]
    else:
        pairs = [(i, j) for j in range(n) for i in range(j, n)]
    return jnp.asarray([p[0] for p in pairs], jnp.int32), jnp.asarray([p[1] for p in pairs], jnp.int32)


def _mla_flash_fwd(qn, qr, kv, kr, *, name):
    S = qn.shape[0]
    H, G, t = C_HEADS, MLA_GROUP, min(FLASH_T, S)
    qi, kj = _causal_pairs(S // t, True)

    def body(qi_ref, kj_ref, qn_ref, qr_ref, kn_ref, v_ref, kr_ref, o_ref, lse_ref, m_s, l_s, acc):
        p_id = pl.program_id(1)
        i, j = qi_ref[p_id], kj_ref[p_id]

        @pl.when(j == 0)
        def _():
            m_s[...] = jnp.full_like(m_s, NEG)
            l_s[...] = jnp.zeros_like(l_s)
            acc[...] = jnp.zeros_like(acc)

        def step(diagonal):
            for hh in range(G):
                sl = slice(hh * LANES, (hh + 1) * LANES)
                s = _mla_scores2(qn_ref[:, sl], qr_ref[hh], kn_ref[:, sl], kr_ref[...], diagonal)
                m_prev = m_s[:, sl]
                m_new = jnp.maximum(m_prev, jnp.max(s, axis=-1, keepdims=True))
                corr = jnp.exp2(m_prev - m_new)
                p = jnp.exp2(s - m_new[:, 0:1])
                l_s[:, sl] = corr * l_s[:, sl] + jnp.sum(p, axis=-1, keepdims=True)
                acc[:, sl] = corr * acc[:, sl] + jnp.dot(p.astype(BF16), v_ref[:, sl], preferred_element_type=F32)
                m_s[:, sl] = m_new

        @pl.when(j < i)
        def _():
            step(False)

        @pl.when(j == i)
        def _():
            step(True)
            o_ref[...] = (acc[...] / l_s[...]).astype(BF16)
            lse_ref[...] = m_s[...] + jnp.log2(l_s[...])

    wide = lambda which, off: pl.BlockSpec((t, G * LANES), lambda h, p, qi, kj: ((qi if which == "q" else kj)[p], off + h))
    return pl.pallas_call(
        body, name=name,
        grid_spec=pltpu.PrefetchScalarGridSpec(
            num_scalar_prefetch=2, grid=(H // G, qi.shape[0]),
            in_specs=[wide("q", 0), pl.BlockSpec((G, t, C_ROPE), lambda h, p, qi, kj: (h, qi[p], 0)),
                      wide("k", 0), wide("k", H // G), pl.BlockSpec((t, C_ROPE), lambda h, p, qi, kj: (kj[p], 0))],
            out_specs=[wide("q", 0), wide("q", 0)],
            scratch_shapes=[pltpu.VMEM((t, G * LANES), F32)] * 3),
        out_shape=[_sds((S, H * C_V), BF16), _sds((S, H * LANES), F32)],
        compiler_params=_params("parallel", "arbitrary"),
    )(qi, kj, qn, qr, kv, kv, kr)


def _mla_delta(do, o, *, name):
    S, W = do.shape
    tm = min(ROW_T, S)

    def body(do_ref, o_ref, d_ref):
        for h in range(C_HEADS):
            sl = slice(h * C_V, (h + 1) * C_V)
            d = jnp.sum(do_ref[:, sl].astype(F32) * o_ref[:, sl].astype(F32), axis=-1, keepdims=True)
            d_ref[:, sl] = jnp.broadcast_to(d, (tm, C_V))

    row = pl.BlockSpec((tm, W), lambda i: (i, 0))
    return pl.pallas_call(body, name=name, grid=(S // tm,), in_specs=[row, row], out_specs=row,
                          out_shape=_sds((S, W), F32), compiler_params=_params("parallel"))(do, o)


def _mla_flash_dq(qn, qr, kv, kr, do, lse, delta, *, name):
    S = qn.shape[0]
    H, G, t = C_HEADS, MLA_GROUP, min(FLASH_T, S)
    qi, kj = _causal_pairs(S // t, True)

    def body(qi_ref, kj_ref, qn_ref, qr_ref, kn_ref, v_ref, kr_ref, do_ref, lse_ref, dl_ref, dqn_ref, dqr_ref, an, ar):
        p_id = pl.program_id(1)
        i, j = qi_ref[p_id], kj_ref[p_id]

        @pl.when(j == 0)
        def _():
            an[...] = jnp.zeros_like(an)
            ar[...] = jnp.zeros_like(ar)

        def step(diagonal):
            for hh in range(G):
                sl = slice(hh * LANES, (hh + 1) * LANES)
                s = _mla_scores2(qn_ref[:, sl], qr_ref[hh], kn_ref[:, sl], kr_ref[...], diagonal)
                p = jnp.exp2(s - lse_ref[:, hh * LANES:hh * LANES + 1])
                dp = lax.dot_general(do_ref[:, sl], v_ref[:, sl], C11, preferred_element_type=F32)
                ds = (p * (dp - dl_ref[:, hh * LANES:hh * LANES + 1])).astype(BF16)
                an[:, sl] += jnp.dot(ds, kn_ref[:, sl], preferred_element_type=F32)
                ar[hh] += jnp.dot(ds, kr_ref[...], preferred_element_type=F32)

        @pl.when(j < i)
        def _():
            step(False)

        @pl.when(j == i)
        def _():
            step(True)
            dqn_ref[...] = (an[...] * MLA_SCALE).astype(BF16)
            dqr_ref[...] = ar[...] * MLA_SCALE

    wide = lambda which, off: pl.BlockSpec((t, G * LANES), lambda h, p, qi, kj: ((qi if which == "q" else kj)[p], off + h))
    qrb = pl.BlockSpec((G, t, C_ROPE), lambda h, p, qi, kj: (h, qi[p], 0))
    return pl.pallas_call(
        body, name=name,
        grid_spec=pltpu.PrefetchScalarGridSpec(
            num_scalar_prefetch=2, grid=(H // G, qi.shape[0]),
            in_specs=[wide("q", 0), qrb, wide("k", 0), wide("k", H // G),
                      pl.BlockSpec((t, C_ROPE), lambda h, p, qi, kj: (kj[p], 0)), wide("q", 0), wide("q", 0), wide("q", 0)],
            out_specs=[wide("q", 0), qrb],
            scratch_shapes=[pltpu.VMEM((t, G * LANES), F32), pltpu.VMEM((G, t, C_ROPE), F32)]),
        out_shape=[_sds((S, H * C_NOPE), BF16), _sds((H, S, C_ROPE), F32)],
        compiler_params=_params("parallel", "arbitrary"),
    )(qi, kj, qn, qr, kv, kv, kr, do, lse, delta)


def _mla_flash_dkv(qn, qr, kv, kr, do, lse, delta, *, name):
    S = qn.shape[0]
    H, G, t = C_HEADS, MLA_GROUP, min(FLASH_T, S)
    n = S // t
    qi, kj = _causal_pairs(n, False)

    def body(qi_ref, kj_ref, qn_ref, qr_ref, kn_ref, v_ref, kr_ref, do_ref, lse_ref, dl_ref, dkn_ref, dv_ref, dkr_ref, akn, av, akr):
        p_id = pl.program_id(1)
        i, j = qi_ref[p_id], kj_ref[p_id]

        def step(diagonal):
            for hh in range(G):
                sl = slice(hh * LANES, (hh + 1) * LANES)
                s = _mla_scores2(qn_ref[:, sl], qr_ref[hh], kn_ref[:, sl], kr_ref[...], diagonal)
                p = jnp.exp2(s - lse_ref[:, hh * LANES:hh * LANES + 1])
                dp = lax.dot_general(do_ref[:, sl], v_ref[:, sl], C11, preferred_element_type=F32)
                ds = (p * (dp - dl_ref[:, hh * LANES:hh * LANES + 1])).astype(BF16)
                av[:, sl] += lax.dot_general(p.astype(BF16), do_ref[:, sl], C00, preferred_element_type=F32)
                akn[:, sl] += lax.dot_general(ds, qn_ref[:, sl], C00, preferred_element_type=F32)
                akr[hh] += lax.dot_general(ds, qr_ref[hh], C00, preferred_element_type=F32)

        @pl.when(i == j)
        def _():
            akn[...] = jnp.zeros_like(akn)
            av[...] = jnp.zeros_like(av)
            akr[...] = jnp.zeros_like(akr)
            step(True)

        @pl.when(i > j)
        def _():
            step(False)

        @pl.when(i == n - 1)
        def _():
            dkn_ref[...] = (akn[...] * MLA_SCALE).astype(BF16)
            dv_ref[...] = av[...].astype(BF16)
            dkr_ref[...] = akr[...] * MLA_SCALE

    wide = lambda which, off: pl.BlockSpec((t, G * LANES), lambda h, p, qi, kj: ((qi if which == "q" else kj)[p], off + h))
    krb = pl.BlockSpec((G, t, C_ROPE), lambda h, p, qi, kj: (h, kj[p], 0))
    return pl.pallas_call(
        body, name=name,
        grid_spec=pltpu.PrefetchScalarGridSpec(
            num_scalar_prefetch=2, grid=(H // G, qi.shape[0]),
            in_specs=[wide("q", 0), pl.BlockSpec((G, t, C_ROPE), lambda h, p, qi, kj: (h, qi[p], 0)), wide("k", 0),
                      wide("k", H // G), pl.BlockSpec((t, C_ROPE), lambda h, p, qi, kj: (kj[p], 0)),
                      wide("q", 0), wide("q", 0), wide("q", 0)],
            out_specs=[wide("k", 0), wide("k", 0), krb],
            scratch_shapes=[pltpu.VMEM((t, G * LANES), F32), pltpu.VMEM((t, G * LANES), F32), pltpu.VMEM((G, t, C_ROPE), F32)]),
        out_shape=[_sds((S, H * C_NOPE), BF16), _sds((S, H * C_V), BF16), _sds((H, S, C_ROPE), F32)],
        compiler_params=_params("parallel", "arbitrary"),
    )(qi, kj, qn, qr, kv, kv, kr, do, lse, delta)


def _place():
    return lax.axis_index("x"), lax.axis_index("y"), lax.axis_index("c")


def _other_chips(x, y):
    return [(1 - x, y), (x, 1 - y), (1 - x, 1 - y)]


def _all_gather_chips(p, *, name):
    R, C = p.shape

    def body(p_ref, o_ref, send_sems, recv_sems, local_sem):
        x, y, c = _place()
        me = 2 * x + y
        local = pltpu.make_async_copy(p_ref, o_ref.at[me], local_sem)
        local.start()
        copies = [pltpu.make_async_remote_copy(src_ref=p_ref, dst_ref=o_ref.at[me], send_sem=send_sems.at[k],
                                               recv_sem=recv_sems.at[k], device_id=(px, py, c), device_id_type=MESH)
                  for k, (px, py) in enumerate(_other_chips(x, y))]
        for cp in copies:
            cp.start()
        for cp in copies:
            cp.wait()
        local.wait()

    any_spec = pl.BlockSpec(memory_space=pl.ANY)
    return pl.pallas_call(
        body, name=name, in_specs=[any_spec], out_specs=any_spec, out_shape=_sds((4, R, C), p.dtype),
        scratch_shapes=[pltpu.SemaphoreType.DMA((3,)), pltpu.SemaphoreType.DMA((3,)), pltpu.SemaphoreType.DMA(())],
    )(p)


def _shard_of(ref, axis, pos, size):
    idx = [slice(None)] * len(ref.shape)
    idx[axis] = pl.ds(pos * size, size)
    return ref.at[tuple(idx)]


def _shard_pos(chip, swapped):
    return (chip % 2) * 2 + chip // 2 if swapped else chip


def _gather_weights(shards, axes, swapped, *, name):
    n = len(shards)
    sizes = [s.shape[a] for s, a in zip(shards, axes)]
    halves = [s.shape[-2] // 2 for s in shards]
    full = [tuple(4 * d if i == a else d for i, d in enumerate(s.shape)) for s, a in zip(shards, axes)]

    def half_of(ref, k, half, chip=None):
        nd = len(ref.shape)
        split = nd - 2
        idx = [slice(None)] * nd
        start = half * halves[k]
        if chip is not None:
            pos = _shard_pos(chip, swapped[k]) * sizes[k]
            if axes[k] == split:
                start = start + pos
            else:
                idx[axes[k]] = pl.ds(pos, sizes[k])
        idx[split] = pl.ds(start, halves[k])
        return ref.at[tuple(idx)]

    def body(*refs):
        ins, outs = refs[:n], refs[n:2 * n]
        send_sems, recv_sems, pass_send_sems, pass_recv_sems, local_sems = refs[2 * n:]
        x, y, c = _place()
        me = 2 * x + y
        chips = _other_chips(x, y)
        waits = []
        for k in range(n):
            local = pltpu.make_async_copy(ins[k], _shard_of(outs[k], axes[k], _shard_pos(me, swapped[k]), sizes[k]), local_sems.at[k])
            local.start()
            waits.append(local.wait)
            for j, (px, py) in enumerate(chips):
                cp = pltpu.make_async_remote_copy(src_ref=half_of(ins[k], k, c), dst_ref=half_of(outs[k], k, c, chip=me),
                                                  send_sem=send_sems.at[3 * k + j], recv_sem=recv_sems.at[3 * k + j],
                                                  device_id=(px, py, c), device_id_type=MESH)
                cp.start()
                waits.append(cp.wait_send)
        for k in range(n):
            for j, (px, py) in enumerate(chips):
                landed = half_of(outs[k], k, c, chip=2 * px + py)
                pltpu.make_async_remote_copy(src_ref=half_of(ins[k], k, c), dst_ref=landed, send_sem=send_sems.at[3 * k + j],
                                             recv_sem=recv_sems.at[3 * k + j], device_id=(px, py, c), device_id_type=MESH).wait_recv()
                on = pltpu.make_async_remote_copy(src_ref=landed, dst_ref=landed, send_sem=pass_send_sems.at[3 * k + j],
                                                  recv_sem=pass_recv_sems.at[3 * k + j], device_id=(x, y, 1 - c), device_id_type=MESH)
                on.start()
                waits.append(on.wait_send)
        for k in range(n):
            for j, (px, py) in enumerate(chips):
                other = half_of(outs[k], k, 1 - c, chip=2 * px + py)
                pltpu.make_async_remote_copy(src_ref=other, dst_ref=other, send_sem=pass_send_sems.at[3 * k + j],
                                             recv_sem=pass_recv_sems.at[3 * k + j], device_id=(x, y, 1 - c), device_id_type=MESH).wait_recv()
        for w in waits:
            w()

    any_spec = pl.BlockSpec(memory_space=pl.ANY)
    return pl.pallas_call(
        body, name=name, in_specs=[any_spec] * n, out_specs=[any_spec] * n,
        out_shape=[_sds(f, s.dtype) for f, s in zip(full, shards)],
        scratch_shapes=[pltpu.SemaphoreType.DMA((3 * n,))] * 4 + [pltpu.SemaphoreType.DMA((n,))],
    )(*shards)


def _scatter_grads(grads, axes, swapped, *, name):
    n = len(grads)
    sizes = [g.shape[a] // 4 for g, a in zip(grads, axes)]
    shard = [tuple(d // 4 if i == a else d for i, d in enumerate(g.shape)) for g, a in zip(grads, axes)]

    def body(*refs):
        ins, outs = refs[:n], refs[n:2 * n]
        send_sems, recv_sems, local_sems = refs[2 * n:]
        x, y, c = _place()
        me = 2 * x + y
        copies = []
        for k in range(n):
            own = _shard_of(ins[k], axes[k], _shard_pos(me, swapped[k]), sizes[k])
            local = pltpu.make_async_copy(own, outs[k].at[3], local_sems.at[k])
            local.start()
            copies.append(local)
            for j, (px, py) in enumerate(_other_chips(x, y)):
                src = _shard_of(ins[k], axes[k], _shard_pos(2 * px + py, swapped[k]), sizes[k])
                cp = pltpu.make_async_remote_copy(src_ref=src, dst_ref=outs[k].at[j], send_sem=send_sems.at[3 * k + j],
                                                  recv_sem=recv_sems.at[3 * k + j], device_id=(px, py, c), device_id_type=MESH)
                cp.start()
                copies.append(cp)
        for cp in copies:
            cp.wait()

    any_spec = pl.BlockSpec(memory_space=pl.ANY)
    return pl.pallas_call(
        body, name=name, in_specs=[any_spec] * n, out_specs=[any_spec] * n,
        out_shape=[_sds((4,) + s, g.dtype) for s, g in zip(shard, grads)],
        scratch_shapes=[pltpu.SemaphoreType.DMA((3 * n,)), pltpu.SemaphoreType.DMA((3 * n,)), pltpu.SemaphoreType.DMA((n,))],
    )(*grads)


def _row_tile(rows, cols, budget=2 ** 20):
    best = None
    for t in range(8, rows + 1, 8):
        if rows % t == 0 and t * cols * 4 <= budget:
            best = t
    return best or rows


def _sum_partials(recv, *, name):
    _, R, C = recv.shape
    tr = _row_tile(R, C)

    def body(own_ref, r0_ref, r1_ref, r2_ref, o_ref):
        f = lambda ref: ref[...].astype(F32)
        o_ref[...] = ((f(own_ref) + f(r0_ref)) + f(r1_ref)) + f(r2_ref)

    rspec = lambda k: pl.BlockSpec((None, tr, C), lambda i: (k, i, 0))
    return pl.pallas_call(
        body, name=name, grid=(R // tr,), in_specs=[rspec(3), rspec(0), rspec(1), rspec(2)],
        out_specs=pl.BlockSpec((tr, C), lambda i: (i, 0)), out_shape=_sds((R, C), F32), compiler_params=_params("parallel"),
    )(recv, recv, recv, recv)


def _swap_cores(parts, *, name):
    n = len(parts)

    def body(*refs):
        ins, outs = refs[:n], refs[n:2 * n]
        send_sems, recv_sems = refs[2 * n:]
        x, y, c = _place()
        copies = [pltpu.make_async_remote_copy(src_ref=ins[k], dst_ref=outs[k], send_sem=send_sems.at[k], recv_sem=recv_sems.at[k],
                                               device_id=(x, y, 1 - c), device_id_type=MESH) for k in range(n)]
        for cp in copies:
            cp.start()
        for cp in copies:
            cp.wait()

    any_spec = pl.BlockSpec(memory_space=pl.ANY)
    return pl.pallas_call(
        body, name=name, in_specs=[any_spec] * n, out_specs=[any_spec] * n, out_shape=[_sds(p.shape, p.dtype) for p in parts],
        scratch_shapes=[pltpu.SemaphoreType.DMA((n,)), pltpu.SemaphoreType.DMA((n,))],
    )(*parts)


def _all_reduce_small(v, *, name):
    r, C = v.shape

    def body(v_ref, o_ref, buf, send_sems, recv_sems):
        x, y, c = _place()
        me = 4 * x + 2 * y + c
        buf[me] = v_ref[...]
        peers = []
        for k in range(1, 8):
            kx, ky, kc = (k >> 2) & 1, (k >> 1) & 1, k & 1
            px = 1 - x if kx else x
            py = 1 - y if ky else y
            pc = 1 - c if kc else c
            peers.append((px, py, pc))
        copies = []
        for k, peer in enumerate(peers):
            cp = pltpu.make_async_remote_copy(src_ref=v_ref, dst_ref=buf.at[me], send_sem=send_sems.at[k],
                                              recv_sem=recv_sems.at[me], device_id=peer, device_id_type=MESH)
            cp.start()
            copies.append(cp)
        for k, (px, py, pc) in enumerate(peers):
            src = 4 * px + 2 * py + pc
            pltpu.make_async_remote_copy(src_ref=v_ref, dst_ref=buf.at[src], send_sem=send_sems.at[k],
                                         recv_sem=recv_sems.at[src], device_id=peers[k], device_id_type=MESH).wait_recv()
        for cp in copies:
            cp.wait_send()
        acc = buf[0]
        for d in range(1, 8):
            acc = acc + buf[d]
        o_ref[...] = acc

    vm = pl.BlockSpec(memory_space=pltpu.VMEM)
    return pl.pallas_call(
        body, name=name, in_specs=[vm], out_specs=vm, out_shape=_sds((r, C), F32),
        scratch_shapes=[pltpu.VMEM((8, r, C), F32), pltpu.SemaphoreType.DMA((7,)), pltpu.SemaphoreType.DMA((8,))],
    )(v)


def _adamw(w, m, v, ga, gb, *, name):
    R, C = w.shape
    tr = _row_tile(R, C)
    has_b = gb is not None
    c1 = 1.0 / (1.0 - ADAM_B1 ** ADAM_STEP)
    c2 = 1.0 / (1.0 - ADAM_B2 ** ADAM_STEP)

    def body(*refs):
        if has_b:
            w_ref, m_ref, v_ref, ga_ref, gb_ref, g_ref, d_ref, nm_ref, nv_ref = refs
            g = ga_ref[...] + gb_ref[...]
        else:
            w_ref, m_ref, v_ref, ga_ref, g_ref, d_ref, nm_ref, nv_ref = refs
            g = ga_ref[...]
        nm = ADAM_B1 * m_ref[...] + (1.0 - ADAM_B1) * g
        nv = ADAM_B2 * v_ref[...] + (1.0 - ADAM_B2) * (g * g)
        g_ref[...] = g
        nm_ref[...] = nm
        nv_ref[...] = nv
        d_ref[...] = -ADAM_LR * ((nm * c1) / (jnp.sqrt(nv * c2) + ADAM_EPS) + ADAM_WD * w_ref[...])

    blk = pl.BlockSpec((tr, C), lambda i: (i, 0))
    n_in = 5 if has_b else 4
    args = (w, m, v, ga) + ((gb,) if has_b else ())
    return pl.pallas_call(body, name=name, grid=(R // tr,), in_specs=[blk] * n_in, out_specs=[blk] * 4,
                          out_shape=[_sds((R, C), F32)] * 4, compiler_params=_params("parallel"))(*args)


def _seg_rows(n, cols):
    return -(-n // (16 * cols)) * 16


def _pack(arrays, dtype, cols=PACK_COLS, row_mult=512):
    parts, rows = [], 0
    for a in arrays:
        n = int(np.prod(a.shape))
        r = _seg_rows(n, cols)
        flat = a.reshape(-1).astype(dtype)
        if r * cols != n:
            flat = jnp.pad(flat, (0, r * cols - n))
        parts.append(flat.reshape(r, cols))
        rows += r
    pad = -rows % row_mult
    if pad:
        parts.append(jnp.zeros((pad, cols), dtype))
    return jnp.concatenate(parts, axis=0)


def _unpack(packed, shapes, cols=PACK_COLS):
    out, r0 = [], 0
    for shp in shapes:
        n = int(np.prod(shp))
        used = -(-n // cols)
        out.append(packed[r0:r0 + used].reshape(-1)[:n].reshape(shp))
        r0 += _seg_rows(n, cols)
    return out


def _rope_tables(seq):
    inv = 1.0 / (ROPE_THETA ** (jnp.arange(0, 64, 2, dtype=F32) / 64))
    ang = jnp.arange(seq, dtype=F32)[:, None] * inv[None, :]
    cos, sin = jnp.cos(ang), jnp.sin(ang)
    cos128 = jnp.concatenate([cos, cos, cos, cos], axis=1)
    sin128 = jnp.concatenate([-sin, sin, -sin, sin], axis=1)
    return cos128, sin128


def _ffn_perm(a):
    lead = a.shape[:-1]
    nj = D_FF // FFN_TC
    return jnp.swapaxes(a.reshape(lead + (2, nj, FFN_TC)), -3, -2).reshape(lead + (2 * D_FF,))


def _ffn_unperm(a):
    lead = a.shape[:-1]
    nj = D_FF // FFN_TC
    return jnp.swapaxes(a.reshape(lead + (nj, 2, FFN_TC)), -3, -2).reshape(lead + (2 * D_FF,))


def _mixer_a_fwd(xb, w, j, cos, sin, tag):
    qkv = _mm(xb, w["a_w_qkv"][j], name=f"mm_qkv_{tag}")
    qkv_r = _rope_cols(qkv, cos, sin, (A_HEADS + A_KV_HEADS) * A_HEAD_DIM, name=f"rope_qkv_{tag}")
    o = _swa_fwd(qkv_r, w["a_sinks"][j], name=f"swa_fwd_{tag}")
    y = _mm(o, w["a_w_o"][j], name=f"mm_ao_{tag}")
    return y, (xb, qkv_r, o)


def _mixer_a_bwd(dzb, res, w, j, cos, sin, tag, grads):
    xb, qkv_r, o = res
    do = _mm(dzb, w["a_w_o"][j], tb=True, out_dtype=BF16, name=f"mm_dao_{tag}")
    grads["a_w_o"][j] = _mm(o, dzb, ta=True, out_dtype=BF16, tk=GRAD_TK, name=f"mm_gao_{tag}")
    dq, dcur, dprev, dsink = _swa_bwd(qkv_r, w["a_sinks"][j], do, cos, sin, name=f"swa_bwd_{tag}")
    grads["a_sinks"][j] = dsink[0, :A_HEADS]
    dqkv = _swa_dqkv(dq, dcur, dprev, cos, sin, name=f"swa_dqkv_{tag}")
    grads["a_w_qkv"][j] = _mm(xb, dqkv, ta=True, out_dtype=BF16, tk=GRAD_TK, name=f"mm_gqkv_{tag}")
    return _mm(dqkv, w["a_w_qkv"][j], tb=True, name=f"mm_dxa_{tag}")


def _mixer_b_fwd(xb, w, j, tag):
    xw = _mm(xb, w["b_w_in"][j], name=f"mm_bin_{tag}")
    wri = jnp.concatenate([w["b_w_rgate"][j], w["b_w_igate"][j]], axis=-1)
    y, u, h = _lru_fwd(xw, w["b_conv_w"][j], w["b_conv_b"][j][None], wri, w["b_b_rgate"][j][None],
                       w["b_b_igate"][j][None], w["b_lambda"][j][None], name=f"lru_fwd_{tag}")
    out = _mm(y, w["b_w_o"][j], name=f"mm_bo_{tag}")
    return out, (xb, xw, wri, u, h, y)


def _mixer_b_bwd(dzb, res, w, j, tag, grads):
    xb, xw, wri, u, h, y = res
    dy = _mm(dzb, w["b_w_o"][j], tb=True, out_dtype=BF16, name=f"mm_dbo_{tag}")
    grads["b_w_o"][j] = _mm(y, dzb, ta=True, out_dtype=BF16, tk=GRAD_TK, name=f"mm_gbo_{tag}")
    dxw, dcw, dcb, dwri, dbr, dbi, dlam = _lru_bwd(
        xw, u, h, dy, w["b_conv_w"][j], wri, w["b_b_rgate"][j][None], w["b_b_igate"][j][None], w["b_lambda"][j][None],
        name=f"lru_bwd_{tag}")
    grads["b_conv_w"][j], grads["b_conv_b"][j] = dcw, dcb[0]
    grads["b_w_rgate"][j], grads["b_w_igate"][j] = dwri[..., :LRU_BLOCK_W].astype(BF16), dwri[..., LRU_BLOCK_W:].astype(BF16)
    grads["b_b_rgate"][j], grads["b_b_igate"][j], grads["b_lambda"][j] = dbr[0], dbi[0], dlam[0]
    grads["b_w_in"][j] = _mm(xb, dxw, ta=True, out_dtype=BF16, tk=GRAD_TK, name=f"mm_gbin_{tag}")
    return _mm(dxw, w["b_w_in"][j], tb=True, name=f"mm_dxb_{tag}")


def _mla_weights(w, j):
    H = C_HEADS
    uq = w["c_w_uq"][j].reshape(C_Q_RANK, H, C_NOPE + C_ROPE)
    ukv = w["c_w_ukv"][j].reshape(C_KV_RANK, H, C_NOPE + C_V)
    uq_n = uq[:, :, :C_NOPE].reshape(C_Q_RANK, H * C_NOPE)
    uq_r = uq[:, :, C_NOPE:].reshape(C_Q_RANK, H * C_ROPE)
    ukv_p = jnp.concatenate([ukv[:, :, :C_NOPE].reshape(C_KV_RANK, H * C_NOPE),
                             ukv[:, :, C_NOPE:].reshape(C_KV_RANK, H * C_V)], axis=1)
    return uq_n, uq_r, ukv_p


def _mixer_c_fwd(xb, w, j, cos, sin, tag):
    S = xb.shape[0]
    H = C_HEADS
    uq_n, uq_r, ukv_p = _mla_weights(w, j)
    c = _mm(xb, w["c_w_down"][j], name=f"mm_cdown_{tag}")
    cq, ckv, kr = _mla_pre(c, w["c_q_norm"][j][None], w["c_kv_norm"][j][None], cos, sin, name=f"mla_pre_{tag}")
    qn = _mm(cq, uq_n, out_dtype=BF16, name=f"mm_uqn_{tag}")
    qr_flat = _rope_heads(_mm(cq, uq_r, name=f"mm_uqr_{tag}"), cos, sin, transpose=False, name=f"rope_qr_{tag}")
    qr = jnp.transpose(qr_flat.reshape(S, H, C_ROPE), (1, 0, 2))
    kv = _mm(ckv, ukv_p, out_dtype=BF16, name=f"mm_ukv_{tag}")
    o, lse = _mla_flash_fwd(qn, qr, kv, kr, name=f"mla_fwd_{tag}")
    y = _mm(o, w["c_w_o"][j], name=f"mm_co_{tag}")
    return y, (xb, c, cq, ckv, kr, qn, qr, kv, o, lse, uq_n, uq_r, ukv_p)


def _mixer_c_bwd(dzb, res, w, j, cos, sin, tag, grads):
    xb, c, cq, ckv, kr, qn, qr, kv, o, lse, uq_n, uq_r, ukv_p = res
    S = xb.shape[0]
    H = C_HEADS
    do = _mm(dzb, w["c_w_o"][j], tb=True, out_dtype=BF16, name=f"mm_dco_{tag}")
    grads["c_w_o"][j] = _mm(o, dzb, ta=True, out_dtype=BF16, tk=GRAD_TK, name=f"mm_gco_{tag}")
    delta = _mla_delta(do, o, name=f"mla_delta_{tag}")
    dqn, dqr = _mla_flash_dq(qn, qr, kv, kr, do, lse, delta, name=f"mla_dq_{tag}")
    dkn, dv, dkr_h = _mla_flash_dkv(qn, qr, kv, kr, do, lse, delta, name=f"mla_dkv_{tag}")
    dkv = jnp.concatenate([dkn, dv], axis=1)
    dqr_flat = _rope_heads(jnp.transpose(dqr, (1, 0, 2)).reshape(S, H * C_ROPE), cos, sin, transpose=True, name=f"rope_dqr_{tag}")
    g_uq_n = _mm(cq, dqn, ta=True, out_dtype=BF16, tk=GRAD_TK, name=f"mm_guqn_{tag}")
    g_uq_r = _mm(cq, dqr_flat, ta=True, out_dtype=BF16, tk=GRAD_TK, name=f"mm_guqr_{tag}")
    g_ukv = _mm(ckv, dkv, ta=True, out_dtype=BF16, tk=GRAD_TK, name=f"mm_gukv_{tag}")
    grads["c_w_uq"][j] = jnp.concatenate([g_uq_n.reshape(C_Q_RANK, H, C_NOPE), g_uq_r.reshape(C_Q_RANK, H, C_ROPE)],
                                         axis=2).reshape(C_Q_RANK, H * (C_NOPE + C_ROPE))
    grads["c_w_ukv"][j] = jnp.concatenate([g_ukv[:, :H * C_NOPE].reshape(C_KV_RANK, H, C_NOPE),
                                           g_ukv[:, H * C_NOPE:].reshape(C_KV_RANK, H, C_V)],
                                          axis=2).reshape(C_KV_RANK, H * (C_NOPE + C_V))
    dcq_a = _mm(dqn, uq_n, tb=True, name=f"mm_dcqa_{tag}")
    dcq_b = _mm(dqr_flat, uq_r, tb=True, name=f"mm_dcqb_{tag}")
    dckv = _mm(dkv, ukv_p, tb=True, name=f"mm_dckv_{tag}")
    dc, dqg, dkvg = _mla_post_bwd(c, dcq_a, dcq_b, dckv, dkr_h, w["c_q_norm"][j][None], w["c_kv_norm"][j][None], cos, sin,
                                  name=f"mla_post_{tag}")
    grads["c_q_norm"][j], grads["c_kv_norm"][j] = dqg[0], dkvg[0]
    grads["c_w_down"][j] = _mm(xb, dc, ta=True, out_dtype=BF16, tk=GRAD_TK, name=f"mm_gcdown_{tag}")
    return _mm(dc, w["c_w_down"][j], tb=True, name=f"mm_dxc_{tag}")


def _local_step(x, mem, target, w):
    S = x.shape[0]
    cos, sin = _rope_tables(S)
    grads = {n: [None] * w[n].shape[0] for n in WEIGHTS if n != "mem_w_kv"}
    mkv = _mm(mem, w["mem_w_kv"], out_dtype=BF16, tm=MEM_LEN, name="mm_memkv")

    xs, xb = x, x.astype(BF16)
    saved = []
    for i in range(DEPTH):
        kind, j = i % N_MIXERS, i // N_MIXERS
        tag = f"l{i}"
        if kind == 0:
            y, res = _mixer_a_fwd(xb, w, j, cos, sin, tag)
        elif kind == 1:
            y, res = _mixer_b_fwd(xb, w, j, tag)
        else:
            y, res = _mixer_c_fwd(xb, w, j, cos, sin, tag)
        x1, x1b, xh1, rs1 = _ln_fwd(xs, y, w["ln_g"][i, 0][None], w["ln_b"][i, 0][None], name=f"ln1_{tag}")
        q = _mm(x1b, w["x_w_q"][i], out_dtype=BF16, name=f"mm_xq_{tag}")
        o = _xattn_fwd(q, mkv, name=f"xattn_fwd_{tag}")
        y2 = _mm(o, w["x_w_o"][i], name=f"mm_xo_{tag}")
        x2, x2b, xh2, rs2 = _ln_fwd(x1, y2, w["ln_g"][i, 1][None], w["ln_b"][i, 1][None], name=f"ln2_{tag}")
        w_up = w["f_w_up"][i]
        cwp, cbp = _ffn_perm(w["f_conv_w"][i]), _ffn_perm(w["f_conv_b"][i][None])
        hh = _mm(x2b, w_up, out_dtype=BF16, tn=FFN_TC, name=f"mm_up_{tag}")
        a = _ffn_act_fwd(hh, cwp, cbp, name=f"ffn_act_{tag}")
        y3 = _mm(a, w["f_w_down"][i], name=f"mm_down_{tag}")
        x3, x3b, xh3, rs3 = _ln_fwd(x2, y3, w["ln_g"][i, 2][None], w["ln_b"][i, 2][None], name=f"ln3_{tag}")
        saved.append((res, (xh1, rs1, x1b), (q, o, xh2, rs2, x2b), (w_up, cwp, cbp, hh, a, xh3, rs3)))
        xs, xb = x3, x3b

    d2, loss = _loss_fwd(xs, target, name="loss")
    d1 = None

    dmkv = None
    ln_dg = [[None] * 3 for _ in range(DEPTH)]
    ln_db = [[None] * 3 for _ in range(DEPTH)]
    for i in reversed(range(DEPTH)):
        kind, j = i % N_MIXERS, i // N_MIXERS
        tag = f"l{i}"
        res, (xh1, rs1, x1b), (q, o, xh2, rs2, x2b), (w_up, cwp, cbp, hh, a, xh3, rs3) = saved[i]
        dz3, dz3b, ln_dg[i][2], ln_db[i][2] = _ln_bwd(d1, d2, xh3, rs3, w["ln_g"][i, 2][None], name=f"ln3_bwd_{tag}")
        da = _mm(dz3b, w["f_w_down"][i], tb=True, out_dtype=BF16, name=f"mm_ddown_{tag}")
        grads["f_w_down"][i] = _mm(a, dz3b, ta=True, out_dtype=BF16, tm=FFN_TC, tk=GRAD_TK, name=f"mm_gdown_{tag}")
        dh, dcw, dcb = _ffn_act_bwd(hh, da, cwp, cbp, name=f"ffn_act_bwd_{tag}")
        grads["f_conv_w"][i], grads["f_conv_b"][i] = _ffn_unperm(dcw), _ffn_unperm(dcb)[0]
        grads["f_w_up"][i] = _mm(x2b, dh, ta=True, out_dtype=BF16, tn=FFN_TC, tk=GRAD_TK, name=f"mm_gup_{tag}")
        dx2 = _mm(dh, w_up, tb=True, tm=512, name=f"mm_dxf_{tag}")

        dz2, dz2b, ln_dg[i][1], ln_db[i][1] = _ln_bwd(dz3, dx2, xh2, rs2, w["ln_g"][i, 1][None], name=f"ln2_bwd_{tag}")
        do = _mm(dz2b, w["x_w_o"][i], tb=True, out_dtype=BF16, name=f"mm_dxo_{tag}")
        grads["x_w_o"][i] = _mm(o, dz2b, ta=True, out_dtype=BF16, tk=GRAD_TK, name=f"mm_gxo_{tag}")
        dq, dmkv_i = _xattn_bwd(q, mkv, do, name=f"xattn_bwd_{tag}")
        dmkv = dmkv_i if dmkv is None else dmkv + dmkv_i
        grads["x_w_q"][i] = _mm(x1b, dq, ta=True, out_dtype=BF16, tk=GRAD_TK, name=f"mm_gxq_{tag}")
        dx1 = _mm(dq, w["x_w_q"][i], tb=True, name=f"mm_dxq_{tag}")

        dz1, dz1b, ln_dg[i][0], ln_db[i][0] = _ln_bwd(dz2, dx1, xh1, rs1, w["ln_g"][i, 0][None], name=f"ln1_bwd_{tag}")
        if kind == 0:
            dx0 = _mixer_a_bwd(dz1b, res, w, j, cos, sin, tag, grads)
        elif kind == 1:
            dx0 = _mixer_b_bwd(dz1b, res, w, j, tag, grads)
        else:
            dx0 = _mixer_c_bwd(dz1b, res, w, j, cos, sin, tag, grads)
        d1, d2 = dz1, dx0

    grad_x = _axpy(d1, d2, name="grad_x")
    out = {n: jnp.stack(g, axis=0) for n, g in grads.items() if n not in ("ln_g", "ln_b")}
    out["mem_w_kv"] = _mm(mem, dmkv, ta=True, out_dtype=BF16, tm=512, name="mm_gmemkv")
    out["ln_g"] = jnp.stack([jnp.concatenate(r, axis=0) for r in ln_dg], axis=0)
    out["ln_b"] = jnp.stack([jnp.concatenate(r, axis=0) for r in ln_db], axis=0)
    return loss, grad_x, out


def kernel(x, mem, a_w_qkv, a_sinks, a_w_o, b_w_in, b_conv_w, b_conv_b, b_w_rgate, b_b_rgate, b_w_igate, b_b_igate, b_lambda, b_w_o, c_w_down, c_q_norm, c_kv_norm, c_w_uq, c_w_ukv, c_w_o, mem_w_kv, x_w_q, x_w_o, f_w_up, f_conv_w, f_conv_b, f_w_down, ln_g, ln_b, loss_target, m_a_w_qkv, m_a_sinks, m_a_w_o, m_b_w_in, m_b_conv_w, m_b_conv_b, m_b_w_rgate, m_b_b_rgate, m_b_w_igate, m_b_b_igate, m_b_lambda, m_b_w_o, m_c_w_down, m_c_q_norm, m_c_kv_norm, m_c_w_uq, m_c_w_ukv, m_c_w_o, m_mem_w_kv, m_x_w_q, m_x_w_o, m_f_w_up, m_f_conv_w, m_f_conv_b, m_f_w_down, m_ln_g, m_ln_b, v_a_w_qkv, v_a_sinks, v_a_w_o, v_b_w_in, v_b_conv_w, v_b_conv_b, v_b_w_rgate, v_b_b_rgate, v_b_w_igate, v_b_b_igate, v_b_lambda, v_b_w_o, v_c_w_down, v_c_q_norm, v_c_kv_norm, v_c_w_uq, v_c_w_ukv, v_c_w_o, v_mem_w_kv, v_x_w_q, v_x_w_o, v_f_w_up, v_f_conv_w, v_f_conv_b, v_f_w_down, v_ln_g, v_ln_b):
    loc = locals()
    shard = {n: loc[n] for n in WEIGHTS}
    mom = {n: loc["m_" + n] for n in WEIGHTS}
    var = {n: loc["v_" + n] for n in WEIGHTS}
    names = [n for n, _ in SHARDED]
    axis = dict(SHARDED)
    big, small = names[:N_BIG], names[N_BIG:]

    axes = [axis[n] for n in big]
    swapped = [n == "f_w_up" for n in big]
    chip = 2 * lax.axis_index("x") + lax.axis_index("y")

    w = dict(zip(big, _gather_weights([shard[n].astype(BF16) for n in big], axes, swapped, name="gather_big")))
    got = _all_gather_chips(_pack([shard[n] for n in small], F32), name="gather_small")
    per_chip = [_unpack(got[s], [shard[n].shape for n in small]) for s in range(4)]
    for k, n in enumerate(small):
        w[n] = jnp.concatenate([per_chip[s][k] for s in range(4)], axis=axis[n])
    for n in REPLICATED:
        w[n] = shard[n]

    loss, grad_x, g = _local_step(x[0], mem[0], loss_target[0], w)

    recv = _scatter_grads([g[n] for n in big], axes, swapped, name="scatter_grads")
    view = {n: (int(np.prod(shard[n].shape[:-1])), shard[n].shape[-1]) for n in big}
    parts = [_sum_partials(r.reshape((4,) + view[n]), name=f"sum_{n}") for n, r in zip(big, recv)]
    sibs = _swap_cores(parts, name="swap_cores")
    grad_o, delta_o, m_o, v_o = {}, {}, {}, {}
    for n, part, sib in zip(big, parts, sibs):
        res = _adamw(shard[n].reshape(view[n]), mom[n].reshape(view[n]), var[n].reshape(view[n]), part, sib, name=f"adamw_{n}")
        for d, r in zip((grad_o, delta_o, m_o, v_o), res):
            d[n] = r.reshape(shard[n].shape)

    rest = small + REPLICATED
    vec = _pack([g[n] for n in rest] + [loss], F32, cols=LANES, row_mult=8)
    tot = _unpack(_all_reduce_small(vec, name="allreduce_small"), [g[n].shape for n in rest] + [(1, 1)], cols=LANES)
    loss_tot = tot[-1].reshape(())
    mine = {n: t for n, t in zip(rest, tot)}
    for n in small:
        size = shard[n].shape[axis[n]]
        mine[n] = lax.dynamic_slice_in_dim(mine[n], chip * size, size, axis=axis[n])
    rpack = lambda d: _pack([d[n] for n in rest], F32, cols=LANES, row_mult=8)
    res = _adamw(rpack(shard), rpack(mom), rpack(var), rpack(mine), None, name="adamw_small")
    for d, r in zip((grad_o, delta_o, m_o, v_o), res):
        d.update(dict(zip(rest, _unpack(r, [shard[n].shape for n in rest], cols=LANES))))

    return (loss_tot, grad_x[None], *[grad_o[n] for n in WEIGHTS], *[delta_o[n] for n in WEIGHTS],
            *[m_o[n] for n in WEIGHTS], *[v_o[n] for n in WEIGHTS])
```

```python
import functools
import math

import numpy as np
import jax
import jax.numpy as jnp
from jax import lax
from jax.experimental import pallas as pl
from jax.experimental.pallas import tpu as pltpu

F32 = jnp.float32
BF16 = jnp.bfloat16
MESH = pl.DeviceIdType.MESH

D_MODEL = 1024
DEPTH = 4
N_MIXERS = 3
MEM_LEN = 256
BLOCK = 128
ROPE_THETA = 10000.0
NEG = -1e30
LN_EPS = 1e-5
RMS_EPS = 1e-6
A_HEADS, A_KV_HEADS, A_HEAD_DIM = 16, 4, 64
LRU_BLOCKS, LRU_BLOCK_W, LRU_CONV, LRU_C = 4, 256, 4, 8.0
C_HEADS, C_NOPE, C_ROPE, C_V, C_Q_RANK, C_KV_RANK = 8, 128, 64, 128, 384, 256
X_HEADS, X_HEAD_DIM = 4, 256
D_FF, FFN_CONV = 2816, 3
ALPHA = (2.0 * DEPTH) ** 0.25
ADAM_LR, ADAM_B1, ADAM_B2, ADAM_EPS, ADAM_WD, ADAM_STEP = 0.001, 0.9, 0.999, 1e-08, 0.01, 10

VMEM_LIMIT = 56 * 2 ** 20
LANES = 128
PACK_COLS = 1024
ROW_T = 512
ACT_T = 256
LRU_T = 256
FLASH_T = 512
FFN_TC = 1408
MM_T = 1024
GRAD_TK = 1024

C11 = (((1,), (1,)), ((), ()))
C00 = (((0,), (0,)), ((), ()))

SHARDED = [
    ("a_w_qkv", 2), ("a_w_o", 1), ("b_w_in", 2), ("b_w_rgate", 2), ("b_w_igate", 2), ("b_w_o", 1), ("c_w_down", 1),
    ("c_w_uq", 2), ("c_w_ukv", 2), ("c_w_o", 1), ("mem_w_kv", 1), ("x_w_q", 1), ("x_w_o", 1), ("f_w_up", 2),
    ("f_w_down", 1),
    ("b_conv_w", 2), ("c_q_norm", 1), ("c_kv_norm", 1), ("f_conv_w", 2), ("ln_g", 2), ("ln_b", 2),
]
N_BIG = 15
REPLICATED = ["a_sinks", "b_conv_b", "b_b_rgate", "b_b_igate", "b_lambda", "f_conv_b"]
WEIGHTS = ["a_w_qkv", "a_sinks", "a_w_o", "b_w_in", "b_conv_w", "b_conv_b", "b_w_rgate", "b_b_rgate", "b_w_igate",
           "b_b_igate", "b_lambda", "b_w_o", "c_w_down", "c_q_norm", "c_kv_norm", "c_w_uq", "c_w_ukv", "c_w_o",
           "mem_w_kv", "x_w_q", "x_w_o", "f_w_up", "f_conv_w", "f_conv_b", "f_w_down", "ln_g", "ln_b"]


def _params(*sem):
    return pltpu.CompilerParams(dimension_semantics=sem, vmem_limit_bytes=VMEM_LIMIT)


def _sds(shape, dtype):
    return jax.ShapeDtypeStruct(tuple(shape), dtype)


def _mm(a, b, *, name, ta=False, tb=False, out_dtype=F32, tm=None, tn=None, tk=None):
    (K, M) = a.shape if ta else a.shape[::-1]
    (N, K2) = b.shape if tb else b.shape[::-1]
    assert K == K2, (a.shape, b.shape, ta, tb)
    tm = min(tm or MM_T, M)
    tn = min(tn or N, N)
    tk = min(tk or K, K)
    assert M % tm == 0 and N % tn == 0 and K % tk == 0, (M, N, K, tm, tn, tk)
    nk = K // tk
    use_acc = nk > 1 and out_dtype != F32
    dims = (((0 if ta else 1,), (1 if tb else 0,)), ((), ()))

    def body(a_ref, b_ref, o_ref, *scratch):
        p = lax.dot_general(a_ref[...].astype(BF16), b_ref[...].astype(BF16), dims, preferred_element_type=F32)
        if nk == 1:
            o_ref[...] = p.astype(out_dtype)
        else:
            acc = scratch[0] if use_acc else o_ref
            k = pl.program_id(2)

            @pl.when(k == 0)
            def _():
                acc[...] = p

            @pl.when(k > 0)
            def _():
                acc[...] += p

            if use_acc:
                @pl.when(k == nk - 1)
                def _():
                    o_ref[...] = acc[...].astype(out_dtype)

    a_spec = pl.BlockSpec((tk, tm), lambda i, j, k: (k, i)) if ta else pl.BlockSpec((tm, tk), lambda i, j, k: (i, k))
    b_spec = pl.BlockSpec((tn, tk), lambda i, j, k: (j, k)) if tb else pl.BlockSpec((tk, tn), lambda i, j, k: (k, j))
    return pl.pallas_call(
        body, name=name, grid=(M // tm, N // tn, nk), in_specs=[a_spec, b_spec],
        out_specs=pl.BlockSpec((tm, tn), lambda i, j, k: (i, j)), out_shape=_sds((M, N), out_dtype),
        scratch_shapes=[pltpu.VMEM((tm, tn), F32)] if use_acc else [],
        compiler_params=_params("parallel", "parallel", "arbitrary"),
    )(a, b)


def _shift_down(cur, prev8, d):
    rolled = pltpu.roll(cur, d, 0)
    rid = lax.broadcasted_iota(jnp.int32, prev8.shape, 0)
    head = jnp.where(rid < d, pltpu.roll(prev8, d, 0), rolled[0:8])
    return jnp.concatenate([head, rolled[8:]], axis=0)


def _shift_up(cur, next8, d):
    n = cur.shape[0]
    rolled = pltpu.roll(cur, n - d, 0)
    rid = lax.broadcasted_iota(jnp.int32, next8.shape, 0)
    tail = jnp.where(rid >= 8 - d, pltpu.roll(next8, 8 - d, 0), rolled[n - 8:n])
    return jnp.concatenate([rolled[0:n - 8], tail], axis=0)


def _swap_halves(x):
    w = x.shape[-1]
    if w == 64:
        return jnp.concatenate([x[:, 32:64], x[:, 0:32]], axis=1)
    lane = lax.broadcasted_iota(jnp.int32, x.shape, 1)
    return jnp.where((lane % 64) < 32, pltpu.roll(x, w - 32, 1), pltpu.roll(x, 32, 1))


def _tile_lanes(t, w):
    return t if w == t.shape[-1] else jnp.concatenate([t] * (w // t.shape[-1]), axis=1)


def _rope(x, cos, sin):
    w = x.shape[-1]
    if w == 64:
        cos, sin = cos[:, :64], sin[:, :64]
    else:
        cos, sin = _tile_lanes(cos, w), _tile_lanes(sin, w)
    return x * cos + _swap_halves(x) * sin


def _rope_t(x, cos, sin):
    w = x.shape[-1]
    if w == 64:
        cos, sin = cos[:, :64], sin[:, :64]
    else:
        cos, sin = _tile_lanes(cos, w), _tile_lanes(sin, w)
    return x * cos - _swap_halves(x) * sin


def _sigmoid(x):
    return 1.0 / (1.0 + jnp.exp(-x))


def _gelu_and_grad(x):
    c0, c1 = math.sqrt(2.0 / math.pi), 0.044715
    t = jnp.tanh(c0 * (x + c1 * x * x * x))
    g = 0.5 * x * (1.0 + t)
    dg = 0.5 * (1.0 + t) + 0.5 * x * (1.0 - t * t) * c0 * (1.0 + 3.0 * c1 * x * x)
    return g, dg


def _neg_expm1(x):
    series = -x * (1.0 + x * (0.5 + x * (1.0 / 6.0 + x * (1.0 / 24.0 + x * (1.0 / 120.0)))))
    return jnp.where(x > -0.1, series, 1.0 - jnp.exp(x))


def _softplus_neg(lam):
    z = -lam
    e = jnp.exp(-jnp.abs(z))
    log1p = jnp.where(e < 0.01, e * (1.0 - e * (0.5 - e * (1.0 / 3.0))), jnp.log(1.0 + e))
    sp = jnp.maximum(z, 0.0) + log1p
    dsp = -_sigmoid(z)
    return sp, dsp


def _ln_fwd(x, y, g, b, *, name):
    S, D = x.shape
    tm = min(ROW_T, S)

    def body(x_ref, y_ref, g_ref, b_ref, o_ref, ob_ref, xh_ref, rs_ref):
        z = ALPHA * x_ref[...] + y_ref[...]
        mu = jnp.mean(z, axis=-1, keepdims=True)
        zc = z - mu
        var = jnp.mean(zc * zc, axis=-1, keepdims=True)
        r = lax.rsqrt(var + LN_EPS)
        xh = zc * r
        o = xh * g_ref[...] + b_ref[...]
        o_ref[...] = o
        ob_ref[...] = o.astype(BF16)
        xh_ref[...] = xh
        rs_ref[...] = r

    row = pl.BlockSpec((tm, D), lambda i: (i, 0))
    vec = pl.BlockSpec((1, D), lambda i: (0, 0))
    return pl.pallas_call(
        body, name=name, grid=(S // tm,), in_specs=[row, row, vec, vec],
        out_specs=[row, row, row, pl.BlockSpec((tm, 1), lambda i: (i, 0))],
        out_shape=[_sds((S, D), F32), _sds((S, D), BF16), _sds((S, D), F32), _sds((S, 1), F32)],
        compiler_params=_params("parallel"),
    )(x, y, g, b)


def _ln_bwd(d1, d2, xh, rs, g, *, name):
    S, D = xh.shape
    tm = min(ROW_T, S)
    has_d1 = d1 is not None

    def body(*refs):
        if has_d1:
            d1_ref, d2_ref, xh_ref, rs_ref, g_ref, dz_ref, dzb_ref, dg_ref, db_ref = refs
            dout = ALPHA * d1_ref[...] + d2_ref[...]
        else:
            d2_ref, xh_ref, rs_ref, g_ref, dz_ref, dzb_ref, dg_ref, db_ref = refs
            dout = d2_ref[...]
        xh_v = xh_ref[...]
        dxh = dout * g_ref[...]
        m1 = jnp.mean(dxh, axis=-1, keepdims=True)
        m2 = jnp.mean(dxh * xh_v, axis=-1, keepdims=True)
        dz = rs_ref[...] * (dxh - m1 - xh_v * m2)
        dz_ref[...] = dz
        dzb_ref[...] = dz.astype(BF16)

        @pl.when(pl.program_id(0) == 0)
        def _():
            dg_ref[...] = jnp.zeros_like(dg_ref)
            db_ref[...] = jnp.zeros_like(db_ref)

        dg_ref[...] += jnp.sum(dout * xh_v, axis=0, keepdims=True)
        db_ref[...] += jnp.sum(dout, axis=0, keepdims=True)

    row = pl.BlockSpec((tm, D), lambda i: (i, 0))
    vec = pl.BlockSpec((1, D), lambda i: (0, 0))
    ins = ([row] if has_d1 else []) + [row, row, pl.BlockSpec((tm, 1), lambda i: (i, 0)), vec]
    args = ([d1] if has_d1 else []) + [d2, xh, rs, g]
    return pl.pallas_call(
        body, name=name, grid=(S // tm,), in_specs=ins, out_specs=[row, row, vec, vec],
        out_shape=[_sds((S, D), F32), _sds((S, D), BF16), _sds((1, D), F32), _sds((1, D), F32)],
        compiler_params=_params("arbitrary"),
    )(*args)


def _loss_fwd(y, target, *, name):
    S, D = y.shape
    tm = min(ROW_T, S)

    def body(y_ref, t_ref, d_ref, l_ref):
        e = y_ref[...] - t_ref[...]
        d_ref[...] = e * (1.0 / D)

        @pl.when(pl.program_id(0) == 0)
        def _():
            l_ref[...] = jnp.zeros_like(l_ref)

        part = jnp.sum(e * e, axis=0, keepdims=True)
        l_ref[...] += (0.5 / D) * jnp.sum(part, axis=1, keepdims=True)

    row = pl.BlockSpec((tm, D), lambda i: (i, 0))
    return pl.pallas_call(
        body, name=name, grid=(S // tm,), in_specs=[row, row],
        out_specs=[row, pl.BlockSpec((1, 1), lambda i: (0, 0))], out_shape=[_sds((S, D), F32), _sds((1, 1), F32)],
        compiler_params=_params("arbitrary"),
    )(y, target)


def _axpy(d1, d2, *, name):
    S, D = d1.shape
    tm = min(ROW_T, S)

    def body(a_ref, b_ref, o_ref):
        o_ref[...] = ALPHA * a_ref[...] + b_ref[...]

    row = pl.BlockSpec((tm, D), lambda i: (i, 0))
    return pl.pallas_call(body, name=name, grid=(S // tm,), in_specs=[row, row], out_specs=row,
                          out_shape=_sds((S, D), F32), compiler_params=_params("parallel"))(d1, d2)


def _ffn_act_fwd(h, cw, cb, *, name, comm=None):
    S, W = h.shape
    tc = FFN_TC
    nj = W // (2 * tc)
    tm = min(ACT_T, S)

    def body(h_ref, w_ref, b_ref, a_ref, carry):
        @pl.when(pl.program_id(1) == 0)
        def _():
            carry[...] = jnp.zeros_like(carry)

        cur = h_ref[...].astype(F32)
        prev8 = carry[...]
        hc = cur * w_ref[2:3, :] + _shift_down(cur, prev8, 1) * w_ref[1:2, :] + _shift_down(cur, prev8, 2) * w_ref[0:1, :]
        hc = hc + b_ref[...]
        carry[...] = cur[tm - 8:tm]
        hg, hu = hc[:, :tc], hc[:, tc:]
        a_ref[...] = (hg * _sigmoid(hg) * hu).astype(BF16)

    return _call(
        body, name=name, grid=(nj, S // tm),
        in_specs=[pl.BlockSpec((tm, 2 * tc), lambda j, i: (i, j)), pl.BlockSpec((3, 2 * tc), lambda j, i: (0, j)),
                  pl.BlockSpec((1, 2 * tc), lambda j, i: (0, j))],
        out_specs=[pl.BlockSpec((tm, tc), lambda j, i: (i, j))], out_shape=[_sds((S, W // 2), BF16)],
        scratch_shapes=[pltpu.VMEM((8, 2 * tc), F32)],
        params=_params("parallel", "arbitrary"), comm=comm,
    )(h, cw, cb)[0]


def _ffn_act_bwd(h, da, cw, cb, *, name, comm=None):
    S, W = h.shape
    tc = FFN_TC
    nj = W // (2 * tc)
    tm = min(ACT_T, S)
    ni = S // tm

    def body(h_ref, hp_ref, da_ref, w_ref, b_ref, dh_ref, dw_ref, db_ref, carry):
        i = pl.program_id(1)
        r = ni - 1 - i

        @pl.when(i == 0)
        def _():
            carry[...] = jnp.zeros_like(carry)
            dw_ref[...] = jnp.zeros_like(dw_ref)
            db_ref[...] = jnp.zeros_like(db_ref)

        cur = h_ref[...].astype(F32)
        prev8 = jnp.where(r > 0, hp_ref[8:16, :].astype(F32), 0.0)
        sh = [cur, _shift_down(cur, prev8, 1), _shift_down(cur, prev8, 2)]
        hc = sh[0] * w_ref[2:3, :] + sh[1] * w_ref[1:2, :] + sh[2] * w_ref[0:1, :] + b_ref[...]
        hg, hu = hc[:, :tc], hc[:, tc:]
        d = da_ref[...].astype(F32)
        sg = _sigmoid(hg)
        dg = d * hu * (sg * (1.0 + hg * (1.0 - sg)))
        du = d * (hg * sg)
        dhc = jnp.concatenate([dg, du], axis=1)
        db_ref[...] += jnp.sum(dhc, axis=0, keepdims=True)
        for k in range(3):
            dw_ref[k:k + 1, :] += jnp.sum(dhc * sh[2 - k], axis=0, keepdims=True)
        next8 = carry[...]
        dh = dhc * w_ref[2:3, :] + _shift_up(dhc, next8, 1) * w_ref[1:2, :] + _shift_up(dhc, next8, 2) * w_ref[0:1, :]
        carry[...] = dhc[0:8]
        dh_ref[...] = dh.astype(BF16)

    rev = lambda j, i: (ni - 1 - i, j)
    return _call(
        body, name=name, grid=(nj, ni),
        in_specs=[pl.BlockSpec((tm, 2 * tc), rev),
                  pl.BlockSpec((16, 2 * tc), lambda j, i: (jnp.maximum((ni - 1 - i) * (tm // 16) - 1, 0), j)),
                  pl.BlockSpec((tm, tc), rev), pl.BlockSpec((3, 2 * tc), lambda j, i: (0, j)),
                  pl.BlockSpec((1, 2 * tc), lambda j, i: (0, j))],
        out_specs=[pl.BlockSpec((tm, 2 * tc), rev), pl.BlockSpec((3, 2 * tc), lambda j, i: (0, j)),
                   pl.BlockSpec((1, 2 * tc), lambda j, i: (0, j))],
        out_shape=[_sds((S, W), BF16), _sds((3, W), F32), _sds((1, W), F32)],
        scratch_shapes=[pltpu.VMEM((8, 2 * tc), F32)],
        params=_params("parallel", "arbitrary"), comm=comm,
    )(h, h, da, cw, cb)


def _xattn_probs(q, k):
    s = lax.dot_general(q, k, C11, preferred_element_type=F32) * (X_HEAD_DIM ** -0.5)
    p = jnp.exp(s - jnp.max(s, axis=-1, keepdims=True))
    return p / jnp.sum(p, axis=-1, keepdims=True)


def _xattn_fwd(q, mkv, *, name):
    S, D = q.shape
    tm = min(ROW_T, S)

    def body(q_ref, k_ref, v_ref, o_ref):
        for h in range(X_HEADS):
            sl = slice(h * X_HEAD_DIM, (h + 1) * X_HEAD_DIM)
            p = _xattn_probs(q_ref[:, sl], k_ref[:, sl])
            o_ref[:, sl] = jnp.dot(p.astype(BF16), v_ref[:, sl], preferred_element_type=F32).astype(BF16)

    return pl.pallas_call(
        body, name=name, grid=(S // tm,),
        in_specs=[pl.BlockSpec((tm, D), lambda i: (i, 0)), pl.BlockSpec((MEM_LEN, D), lambda i: (0, 0)),
                  pl.BlockSpec((MEM_LEN, D), lambda i: (0, 1))],
        out_specs=pl.BlockSpec((tm, D), lambda i: (i, 0)), out_shape=_sds((S, D), BF16),
        compiler_params=_params("parallel"),
    )(q, mkv, mkv)


def _xattn_bwd(q, mkv, do, *, name):
    S, D = q.shape
    tm = min(ROW_T, S)
    scale = X_HEAD_DIM ** -0.5

    def body(q_ref, k_ref, v_ref, do_ref, dq_ref, dkv_ref):
        @pl.when(pl.program_id(0) == 0)
        def _():
            dkv_ref[...] = jnp.zeros_like(dkv_ref)

        for h in range(X_HEADS):
            sl = slice(h * X_HEAD_DIM, (h + 1) * X_HEAD_DIM)
            sv = slice(D + h * X_HEAD_DIM, D + (h + 1) * X_HEAD_DIM)
            qh, kh, vh, doh = q_ref[:, sl], k_ref[:, sl], v_ref[:, sl], do_ref[:, sl]
            p = _xattn_probs(qh, kh)
            dp = lax.dot_general(doh, vh, C11, preferred_element_type=F32)
            ds = (p * (dp - jnp.sum(p * dp, axis=-1, keepdims=True)) * scale).astype(BF16)
            dq_ref[:, sl] = jnp.dot(ds, kh, preferred_element_type=F32).astype(BF16)
            dkv_ref[:, sl] += lax.dot_general(ds, qh, C00, preferred_element_type=F32)
            dkv_ref[:, sv] += lax.dot_general(p.astype(BF16), doh, C00, preferred_element_type=F32)

    row = pl.BlockSpec((tm, D), lambda i: (i, 0))
    return pl.pallas_call(
        body, name=name, grid=(S // tm,),
        in_specs=[row, pl.BlockSpec((MEM_LEN, D), lambda i: (0, 0)), pl.BlockSpec((MEM_LEN, D), lambda i: (0, 1)), row],
        out_specs=[row, pl.BlockSpec((MEM_LEN, 2 * D), lambda i: (0, 0))],
        out_shape=[_sds((S, D), BF16), _sds((MEM_LEN, 2 * D), F32)],
        compiler_params=_params("arbitrary"),
    )(q, mkv, mkv, do)


def _rope_cols(x, cos, sin, n_rope, *, name):
    S, W = x.shape
    tm = min(ROW_T, S)

    def body(x_ref, c_ref, s_ref, o_ref):
        o_ref[:, :n_rope] = _rope(x_ref[:, :n_rope], c_ref[...], s_ref[...]).astype(BF16)
        if n_rope < W:
            o_ref[:, n_rope:] = x_ref[:, n_rope:].astype(BF16)

    row = pl.BlockSpec((tm, W), lambda i: (i, 0))
    tab = pl.BlockSpec((tm, LANES), lambda i: (i, 0))
    return pl.pallas_call(body, name=name, grid=(S // tm,), in_specs=[row, tab, tab], out_specs=row,
                          out_shape=_sds((S, W), BF16), compiler_params=_params("parallel"))(x, cos, sin)


def _swa_band(n):
    qi = lax.broadcasted_iota(jnp.int32, (BLOCK, 2 * BLOCK), 0)
    kj = lax.broadcasted_iota(jnp.int32, (BLOCK, 2 * BLOCK), 1)
    first = jnp.where(n > 0, 0, BLOCK)
    return ((kj < BLOCK) & (kj > qi + first)) | ((kj >= BLOCK) & (kj - BLOCK <= qi))


def _swa_probs(q, k, band, sink):
    s = lax.dot_general(q, k, C11, preferred_element_type=F32) * (A_HEAD_DIM ** -0.5)
    s = jnp.where(band, s, NEG)
    m = jnp.maximum(jnp.max(s, axis=-1, keepdims=True), sink)
    p = jnp.exp(s - m)
    e_sink = jnp.exp(sink - m)
    den = jnp.sum(p, axis=-1, keepdims=True) + e_sink
    return p / den, e_sink / den


def _swa_specs():
    nq, nkv = A_HEADS * A_HEAD_DIM, A_KV_HEADS * A_HEAD_DIM
    kb, vb = nq // nkv, nq // nkv + 1
    prev = lambda n: jnp.maximum(n - 1, 0)
    return [pl.BlockSpec((BLOCK, nq), lambda n: (n, 0)),
            pl.BlockSpec((BLOCK, nkv), lambda n: (n, kb)), pl.BlockSpec((BLOCK, nkv), lambda n: (prev(n), kb)),
            pl.BlockSpec((BLOCK, nkv), lambda n: (n, vb)), pl.BlockSpec((BLOCK, nkv), lambda n: (prev(n), vb)),
            pl.BlockSpec(memory_space=pltpu.SMEM)]


def _swa_fwd(qkv, sinks, *, name, comm=None):
    S = qkv.shape[0]
    hd, grp = A_HEAD_DIM, A_HEADS // A_KV_HEADS

    def body(q_ref, kc_ref, kp_ref, vc_ref, vp_ref, sink_ref, o_ref):
        band = _swa_band(pl.program_id(0))
        qa, kc, kp, vc, vp = q_ref[...], kc_ref[...], kp_ref[...], vc_ref[...], vp_ref[...]
        for hk in range(A_KV_HEADS):
            ks = slice(hk * hd, (hk + 1) * hd)
            k = jnp.concatenate([kp[:, ks], kc[:, ks]], axis=0)
            v = jnp.concatenate([vp[:, ks], vc[:, ks]], axis=0)
            for gi in range(grp):
                h = hk * grp + gi
                p, _ = _swa_probs(qa[:, h * hd:(h + 1) * hd], k, band, sink_ref[h])
                o_ref[:, h * hd:(h + 1) * hd] = jnp.dot(p.astype(BF16), v, preferred_element_type=F32).astype(BF16)

    return _call(
        body, name=name, grid=(S // BLOCK,), in_specs=_swa_specs(),
        out_specs=[pl.BlockSpec((BLOCK, A_HEADS * hd), lambda n: (n, 0))], out_shape=[_sds((S, A_HEADS * hd), BF16)],
        params=_params("arbitrary"), comm=comm,
    )(qkv, qkv, qkv, qkv, qkv, sinks)[0]


def _swa_bwd(qkv, sinks, do, cos, sin, *, name, comm=None):
    S = qkv.shape[0]
    hd, grp = A_HEAD_DIM, A_HEADS // A_KV_HEADS
    nq, nkv = A_HEADS * hd, A_KV_HEADS * hd
    scale = hd ** -0.5

    def body(q_ref, kc_ref, kp_ref, vc_ref, vp_ref, sink_ref, do_ref, c_ref, s_ref, dq_ref, dc_ref, dp_ref, ds_ref, dq_s):
        @pl.when(pl.program_id(0) == 0)
        def _():
            ds_ref[...] = jnp.zeros_like(ds_ref)

        band = _swa_band(pl.program_id(0))
        lane = lax.broadcasted_iota(jnp.int32, (1, LANES), 1)
        qa, kc, kp, vc, vp, doa = q_ref[...], kc_ref[...], kp_ref[...], vc_ref[...], vp_ref[...], do_ref[...]
        dsink = jnp.zeros((1, LANES), F32)
        for hk in range(A_KV_HEADS):
            ks = slice(hk * hd, (hk + 1) * hd)
            k = jnp.concatenate([kp[:, ks], kc[:, ks]], axis=0)
            v = jnp.concatenate([vp[:, ks], vc[:, ks]], axis=0)
            dk = jnp.zeros((2 * BLOCK, hd), F32)
            dv = jnp.zeros((2 * BLOCK, hd), F32)
            for gi in range(grp):
                h = hk * grp + gi
                hs = slice(h * hd, (h + 1) * hd)
                qh, doh = qa[:, hs], doa[:, hs]
                p, p_sink = _swa_probs(qh, k, band, sink_ref[h])
                dpr = lax.dot_general(doh, v, C11, preferred_element_type=F32)
                delta = jnp.sum(p * dpr, axis=-1, keepdims=True)
                dsc = (p * (dpr - delta) * scale).astype(BF16)
                dq_s[:, hs] = jnp.dot(dsc, k, preferred_element_type=F32)
                dk = dk + lax.dot_general(dsc, qh, C00, preferred_element_type=F32)
                dv = dv + lax.dot_general(p.astype(BF16), doh, C00, preferred_element_type=F32)
                dsink = dsink + jnp.where(lane == h, -jnp.sum(p_sink * delta, axis=0, keepdims=True), 0.0)
            dp_ref[:, ks] = dk[:BLOCK]
            dc_ref[:, ks] = dk[BLOCK:]
            dp_ref[:, nkv + hk * hd:nkv + (hk + 1) * hd] = dv[:BLOCK]
            dc_ref[:, nkv + hk * hd:nkv + (hk + 1) * hd] = dv[BLOCK:]
        ds_ref[...] += dsink
        dq_ref[...] = _rope_t(dq_s[...], c_ref[...], s_ref[...]).astype(BF16)

    tab = pl.BlockSpec((BLOCK, LANES), lambda n: (n, 0))
    blk = lambda w: pl.BlockSpec((BLOCK, w), lambda n: (n, 0))
    return _call(
        body, name=name, grid=(S // BLOCK,), in_specs=_swa_specs() + [blk(nq), tab, tab],
        out_specs=[blk(nq), blk(2 * nkv), blk(2 * nkv), pl.BlockSpec((1, LANES), lambda n: (0, 0))],
        out_shape=[_sds((S, nq), BF16), _sds((S, 2 * nkv), F32), _sds((S, 2 * nkv), F32), _sds((1, LANES), F32)],
        scratch_shapes=[pltpu.VMEM((BLOCK, nq), F32)],
        params=_params("arbitrary"), comm=comm,
    )(qkv, qkv, qkv, qkv, qkv, sinks, do, cos, sin)


def _swa_dqkv(dq, dcur, dprev, cos, sin, *, name):
    S, nq = dq.shape
    nkv = dcur.shape[1] // 2
    nb = S // BLOCK

    def body(dq_ref, dc_ref, dp_ref, c_ref, s_ref, o_ref):
        o_ref[:, :nq] = dq_ref[...]
        d = dc_ref[...] + jnp.where(pl.program_id(0) < nb - 1, dp_ref[...], 0.0)
        o_ref[:, nq:nq + nkv] = _rope_t(d[:, :nkv], c_ref[...], s_ref[...]).astype(BF16)
        o_ref[:, nq + nkv:] = d[:, nkv:].astype(BF16)

    tab = pl.BlockSpec((BLOCK, LANES), lambda m: (m, 0))
    blk = lambda w: pl.BlockSpec((BLOCK, w), lambda m: (m, 0))
    return pl.pallas_call(
        body, name=name, grid=(nb,),
        in_specs=[blk(nq), blk(2 * nkv), pl.BlockSpec((BLOCK, 2 * nkv), lambda m: (jnp.minimum(m + 1, nb - 1), 0)), tab, tab],
        out_specs=blk(nq + 2 * nkv), out_shape=_sds((S, nq + 2 * nkv), BF16), compiler_params=_params("parallel"),
    )(dq, dcur, dprev, cos, sin)


def _lru_gates(u, wri_ref, br, bi, sp):
    ub = u.astype(BF16)
    rs, igs = [], []
    for hb in range(LRU_BLOCKS):
        sl = slice(hb * LRU_BLOCK_W, (hb + 1) * LRU_BLOCK_W)
        ri = jnp.dot(ub[:, sl], wri_ref[hb], preferred_element_type=F32)
        rs.append(ri[:, :LRU_BLOCK_W])
        igs.append(ri[:, LRU_BLOCK_W:])
    r = _sigmoid(jnp.concatenate(rs, axis=1) + br)
    ig = _sigmoid(jnp.concatenate(igs, axis=1) + bi)
    la = -LRU_C * r * sp
    a = jnp.exp(la)
    sq = jnp.sqrt(_neg_expm1(2.0 * la))
    return r, ig, a, sq


def _lru_fwd(xw, cw, cb, wri, br, bi, lam, *, name, comm=None):
    S = xw.shape[0]
    W = D_MODEL
    tm = min(LRU_T, S)

    def body(gate_ref, up_ref, cw_ref, cb_ref, wri_ref, br_ref, bi_ref, lam_ref, y_ref, u_ref, h_ref, cu, ch, a_s, b_s):
        @pl.when(pl.program_id(0) == 0)
        def _():
            cu[...] = jnp.zeros_like(cu)
            ch[...] = jnp.zeros_like(ch)

        up = up_ref[...]
        prev8 = cu[...]
        u = up * cw_ref[3:4, :] + cb_ref[...]
        for d in range(1, LRU_CONV):
            u = u + _shift_down(up, prev8, d) * cw_ref[3 - d:4 - d, :]
        cu[...] = up[tm - 8:tm]
        u_ref[...] = u
        sp, _ = _softplus_neg(lam_ref[...])
        _, ig, a, sq = _lru_gates(u, wri_ref, br_ref[...], bi_ref[...], sp)
        a_s[...] = a
        b_s[...] = sq * (ig * u)
        rid = lax.broadcasted_iota(jnp.int32, (8, W), 0)

        def tile(t, h):
            r0 = pl.multiple_of(t * 8, 8)
            at, bt = a_s[pl.ds(r0, 8), :], b_s[pl.ds(r0, 8), :]
            out = jnp.zeros((8, W), F32)
            for j in range(8):
                h = at[j:j + 1, :] * h + bt[j:j + 1, :]
                out = jnp.where(rid == j, h, out)
            h_ref[pl.ds(r0, 8), :] = out
            return h

        ch[0:1, :] = lax.fori_loop(0, tm // 8, tile, ch[0:1, :])
        g, _ = _gelu_and_grad(gate_ref[...])
        y_ref[...] = (h_ref[...] * g).astype(BF16)

    row = pl.BlockSpec((tm, W), lambda i: (i, 0))
    full = lambda shape: pl.BlockSpec(shape, lambda i: (0,) * len(shape))
    return _call(
        body, name=name, grid=(S // tm,),
        in_specs=[row, pl.BlockSpec((tm, W), lambda i: (i, 1)), full((LRU_CONV, W)), full((1, W)),
                  full((LRU_BLOCKS, LRU_BLOCK_W, 2 * LRU_BLOCK_W)), full((1, W)), full((1, W)), full((1, W))],
        out_specs=[row, row, row], out_shape=[_sds((S, W), BF16), _sds((S, W), F32), _sds((S, W), F32)],
        scratch_shapes=[pltpu.VMEM((8, W), F32), pltpu.VMEM((8, W), F32), pltpu.VMEM((tm, W), F32), pltpu.VMEM((tm, W), F32)],
        params=_params("arbitrary"), comm=comm,
    )(xw, xw, cw, cb, wri, br, bi, lam)


def _lru_bwd(xw, u, h, dy, cw, wri, br, bi, lam, *, name):
    S = xw.shape[0]
    W = D_MODEL
    tm = min(LRU_T, S)
    nb = S // tm

    def body(gate_ref, up_ref, upp_ref, u_ref, h_ref, hp_ref, dy_ref, cw_ref, wri_ref, br_ref, bi_ref, lam_ref,
             dxw_ref, dcw_ref, dcb_ref, dwri_ref, dbr_ref, dbi_ref, dlam_ref, cg, cdu, a_s, d_s, g_s):
        i = pl.program_id(0)
        r_blk = nb - 1 - i

        @pl.when(i == 0)
        def _():
            cg[...] = jnp.zeros_like(cg)
            cdu[...] = jnp.zeros_like(cdu)
            for ref in (dcw_ref, dcb_ref, dwri_ref, dbr_ref, dbi_ref, dlam_ref):
                ref[...] = jnp.zeros_like(ref)

        u = u_ref[...]
        hv = h_ref[...]
        sp, dsp = _softplus_neg(lam_ref[...])
        r, ig, a, sq = _lru_gates(u, wri_ref, br_ref[...], bi_ref[...], sp)
        dy = dy_ref[...].astype(F32)
        g, dgelu = _gelu_and_grad(gate_ref[...])
        dxw_ref[:, :W] = (dy * hv * dgelu).astype(BF16)
        a_s[...] = a
        d_s[...] = dy * g
        rid = lax.broadcasted_iota(jnp.int32, (8, W), 0)

        def tile(t, c):
            r0 = pl.multiple_of((tm // 8 - 1 - t) * 8, 8)
            at, dt = a_s[pl.ds(r0, 8), :], d_s[pl.ds(r0, 8), :]
            out = jnp.zeros((8, W), F32)
            for j in range(7, -1, -1):
                gt = dt[j:j + 1, :] + c
                c = at[j:j + 1, :] * gt
                out = jnp.where(rid == j, gt, out)
            g_s[pl.ds(r0, 8), :] = out
            return c

        cg[0:1, :] = lax.fori_loop(0, tm // 8, tile, cg[0:1, :])
        gt = g_s[...]
        hprev8 = jnp.where(r_blk > 0, hp_ref[...], 0.0)
        da = gt * _shift_down(hv, hprev8, 1)
        iu = ig * u
        d_iu = gt * sq
        dla = da * a - (gt * iu) * (a * a) / sq
        dlam_ref[...] += jnp.sum(dla * r, axis=0, keepdims=True) * (-LRU_C) * dsp
        dr_pre = dla * (-LRU_C) * sp * r * (1.0 - r)
        di_pre = d_iu * u * ig * (1.0 - ig)
        dbr_ref[...] += jnp.sum(dr_pre, axis=0, keepdims=True)
        dbi_ref[...] += jnp.sum(di_pre, axis=0, keepdims=True)
        ub = u.astype(BF16)
        dus = []
        for hb in range(LRU_BLOCKS):
            sl = slice(hb * LRU_BLOCK_W, (hb + 1) * LRU_BLOCK_W)
            dri = jnp.concatenate([dr_pre[:, sl], di_pre[:, sl]], axis=1).astype(BF16)
            dus.append(lax.dot_general(dri, wri_ref[hb], C11, preferred_element_type=F32))
            dwri_ref[hb] += lax.dot_general(ub[:, sl], dri, C00, preferred_element_type=F32)
        du = d_iu * ig + jnp.concatenate(dus, axis=1)
        dcb_ref[...] += jnp.sum(du, axis=0, keepdims=True)
        up = up_ref[...]
        upprev8 = jnp.where(r_blk > 0, upp_ref[...], 0.0)
        dcw_ref[3:4, :] += jnp.sum(du * up, axis=0, keepdims=True)
        for d in range(1, LRU_CONV):
            dcw_ref[3 - d:4 - d, :] += jnp.sum(du * _shift_down(up, upprev8, d), axis=0, keepdims=True)
        next8 = cdu[...]
        dup = du * cw_ref[3:4, :]
        for d in range(1, LRU_CONV):
            dup = dup + _shift_up(du, next8, d) * cw_ref[3 - d:4 - d, :]
        cdu[...] = du[0:8]
        dxw_ref[:, W:] = dup.astype(BF16)

    rev = lambda c: (lambda i: (nb - 1 - i, c))
    halo = lambda c: (lambda i: (jnp.maximum((nb - 1 - i) * (tm // 8) - 1, 0), c))
    full = lambda shape: pl.BlockSpec(shape, lambda i: (0,) * len(shape))
    vec = full((1, W))
    return pl.pallas_call(
        body, name=name, grid=(nb,),
        in_specs=[pl.BlockSpec((tm, W), rev(0)), pl.BlockSpec((tm, W), rev(1)), pl.BlockSpec((8, W), halo(1)),
                  pl.BlockSpec((tm, W), rev(0)), pl.BlockSpec((tm, W), rev(0)), pl.BlockSpec((8, W), halo(0)),
                  pl.BlockSpec((tm, W), rev(0)), full((LRU_CONV, W)), full((LRU_BLOCKS, LRU_BLOCK_W, 2 * LRU_BLOCK_W)),
                  vec, vec, vec],
        out_specs=[pl.BlockSpec((tm, 2 * W), rev(0)), full((LRU_CONV, W)), vec,
                   full((LRU_BLOCKS, LRU_BLOCK_W, 2 * LRU_BLOCK_W)), vec, vec, vec],
        out_shape=[_sds((S, 2 * W), BF16), _sds((LRU_CONV, W), F32), _sds((1, W), F32),
                   _sds((LRU_BLOCKS, LRU_BLOCK_W, 2 * LRU_BLOCK_W), F32), _sds((1, W), F32), _sds((1, W), F32),
                   _sds((1, W), F32)],
        scratch_shapes=[pltpu.VMEM((8, W), F32), pltpu.VMEM((8, W), F32), pltpu.VMEM((tm, W), F32),
                        pltpu.VMEM((tm, W), F32), pltpu.VMEM((tm, W), F32)],
        compiler_params=_params("arbitrary"),
    )(xw, xw, xw, u, h, h, dy, cw, wri, br, bi, lam)


def _rms(x, g):
    r = lax.rsqrt(jnp.mean(x * x, axis=-1, keepdims=True) + RMS_EPS)
    return x * r * g, r


def _mla_pre(c, qg, kvg, cos, sin, *, name):
    S = c.shape[0]
    tm = min(ROW_T, S)
    q0, k0 = C_Q_RANK, C_Q_RANK + C_KV_RANK

    def body(c_ref, qg_ref, kvg_ref, cs_ref, sn_ref, cq_ref, ckv_ref, kr_ref):
        cq_ref[...] = _rms(c_ref[:, :q0], qg_ref[...])[0].astype(BF16)
        ckv_ref[...] = _rms(c_ref[:, q0:k0], kvg_ref[...])[0].astype(BF16)
        kr_ref[...] = _rope(c_ref[:, k0:], cs_ref[...], sn_ref[...]).astype(BF16)

    blk = lambda w: pl.BlockSpec((tm, w), lambda i: (i, 0))
    vec = lambda w: pl.BlockSpec((1, w), lambda i: (0, 0))
    return pl.pallas_call(
        body, name=name, grid=(S // tm,),
        in_specs=[blk(c.shape[1]), vec(C_Q_RANK), vec(C_KV_RANK), blk(LANES), blk(LANES)],
        out_specs=[blk(C_Q_RANK), blk(C_KV_RANK), blk(C_ROPE)],
        out_shape=[_sds((S, C_Q_RANK), BF16), _sds((S, C_KV_RANK), BF16), _sds((S, C_ROPE), BF16)],
        compiler_params=_params("parallel"),
    )(c, qg, kvg, cos, sin)


def _mla_post_bwd(c, dcq_a, dcq_b, dckv, dkr_h, qg, kvg, cos, sin, *, name):
    S = c.shape[0]
    tm = min(ROW_T, S)
    q0, k0 = C_Q_RANK, C_Q_RANK + C_KV_RANK

    def rms_bwd(x, g, dy):
        r = lax.rsqrt(jnp.mean(x * x, axis=-1, keepdims=True) + RMS_EPS)
        uu = dy * g
        dx = r * uu - x * (r * r * r) * jnp.mean(uu * x, axis=-1, keepdims=True)
        return dx, jnp.sum(dy * x * r, axis=0, keepdims=True)

    def body(c_ref, da_ref, db_ref, dkv_ref, dkr_ref, qg_ref, kvg_ref, cs_ref, sn_ref, dc_ref, dqg_ref, dkvg_ref):
        @pl.when(pl.program_id(0) == 0)
        def _():
            dqg_ref[...] = jnp.zeros_like(dqg_ref)
            dkvg_ref[...] = jnp.zeros_like(dkvg_ref)

        dx, dg = rms_bwd(c_ref[:, :q0], qg_ref[...], da_ref[...] + db_ref[...])
        dc_ref[:, :q0] = dx.astype(BF16)
        dqg_ref[...] += dg
        dx, dg = rms_bwd(c_ref[:, q0:k0], kvg_ref[...], dkv_ref[...])
        dc_ref[:, q0:k0] = dx.astype(BF16)
        dkvg_ref[...] += dg
        dkr = dkr_ref[0]
        for hh in range(1, C_HEADS):
            dkr = dkr + dkr_ref[hh]
        dc_ref[:, k0:] = _rope_t(dkr, cs_ref[...], sn_ref[...]).astype(BF16)

    blk = lambda w: pl.BlockSpec((tm, w), lambda i: (i, 0))
    vec = lambda w: pl.BlockSpec((1, w), lambda i: (0, 0))
    return pl.pallas_call(
        body, name=name, grid=(S // tm,),
        in_specs=[blk(c.shape[1]), blk(C_Q_RANK), blk(C_Q_RANK), blk(C_KV_RANK),
                  pl.BlockSpec((C_HEADS, tm, C_ROPE), lambda i: (0, i, 0)), vec(C_Q_RANK), vec(C_KV_RANK), blk(LANES), blk(LANES)],
        out_specs=[blk(c.shape[1]), vec(C_Q_RANK), vec(C_KV_RANK)],
        out_shape=[_sds(c.shape, BF16), _sds((1, C_Q_RANK), F32), _sds((1, C_KV_RANK), F32)],
        compiler_params=_params("arbitrary"),
    )(c, dcq_a, dcq_b, dckv, dkr_h, qg, kvg, cos, sin)


def _rope_heads(x, cos, sin, *, transpose, name):
    S, W = x.shape
    tm = min(ROW_T, S)
    fn = _rope_t if transpose else _rope

    def body(x_ref, c_ref, s_ref, o_ref):
        o_ref[...] = fn(x_ref[...].astype(F32), c_ref[...], s_ref[...]).astype(BF16)

    row = pl.BlockSpec((tm, W), lambda i: (i, 0))
    tab = pl.BlockSpec((tm, LANES), lambda i: (i, 0))
    return pl.pallas_call(body, name=name, grid=(S // tm,), in_specs=[row, tab, tab], out_specs=row,
                          out_shape=_sds((S, W), BF16), compiler_params=_params("parallel"))(x, cos, sin)


MLA_GROUP = 2
MLA_SCALE = (C_NOPE + C_ROPE) ** -0.5
LOG2E = 1.4426950408889634


def _mla_scores2(qn, qr, kn, kr, diagonal):
    s = lax.dot_general(qn, kn, C11, preferred_element_type=F32) + lax.dot_general(qr, kr, C11, preferred_element_type=F32)
    s = s * (MLA_SCALE * LOG2E)
    if diagonal:
        row = lax.broadcasted_iota(jnp.int32, s.shape, 0)
        col = lax.broadcasted_iota(jnp.int32, s.shape, 1)
        s = jnp.where(col <= row, s, NEG)
    return s


def _causal_pairs(n, query_major):
    if query_major:
        pairs = [(i, j) for i in range(n) for j in range(i + 1)]
    else:
        pairs = [(i, j) for j in range(n) for i in range(j, n)]
    return jnp.asarray([p[0] for p in pairs], jnp.int32), jnp.asarray([p[1] for p in pairs], jnp.int32)


def _mla_flash_fwd(qn, qr, kv, kr, *, name):
    S = qn.shape[0]
    H, G, t = C_HEADS, MLA_GROUP, min(FLASH_T, S)
    qi, kj = _causal_pairs(S // t, True)

    def body(qi_ref, kj_ref, qn_ref, qr_ref, kn_ref, v_ref, kr_ref, o_ref, lse_ref, m_s, l_s, acc):
        p_id = pl.program_id(1)
        i, j = qi_ref[p_id], kj_ref[p_id]

        @pl.when(j == 0)
        def _():
            m_s[...] = jnp.full_like(m_s, NEG)
            l_s[...] = jnp.zeros_like(l_s)
            acc[...] = jnp.zeros_like(acc)

        def step(diagonal):
            for hh in range(G):
                sl = slice(hh * LANES, (hh + 1) * LANES)
                s = _mla_scores2(qn_ref[:, sl], qr_ref[hh], kn_ref[:, sl], kr_ref[...], diagonal)
                m_prev = m_s[:, sl]
                m_new = jnp.maximum(m_prev, jnp.max(s, axis=-1, keepdims=True))
                corr = jnp.exp2(m_prev - m_new)
                p = jnp.exp2(s - m_new[:, 0:1])
                l_s[:, sl] = corr * l_s[:, sl] + jnp.sum(p, axis=-1, keepdims=True)
                acc[:, sl] = corr * acc[:, sl] + jnp.dot(p.astype(BF16), v_ref[:, sl], preferred_element_type=F32)
                m_s[:, sl] = m_new

        @pl.when(j < i)
        def _():
            step(False)

        @pl.when(j == i)
        def _():
            step(True)
            o_ref[...] = (acc[...] / l_s[...]).astype(BF16)
            lse_ref[...] = m_s[...] + jnp.log2(l_s[...])

    wide = lambda which, off: pl.BlockSpec((t, G * LANES), lambda h, p, qi, kj: ((qi if which == "q" else kj)[p], off + h))
    return pl.pallas_call(
        body, name=name,
        grid_spec=pltpu.PrefetchScalarGridSpec(
            num_scalar_prefetch=2, grid=(H // G, qi.shape[0]),
            in_specs=[wide("q", 0), pl.BlockSpec((G, t, C_ROPE), lambda h, p, qi, kj: (h, qi[p], 0)),
                      wide("k", 0), wide("k", H // G), pl.BlockSpec((t, C_ROPE), lambda h, p, qi, kj: (kj[p], 0))],
            out_specs=[wide("q", 0), wide("q", 0)],
            scratch_shapes=[pltpu.VMEM((t, G * LANES), F32)] * 3),
        out_shape=[_sds((S, H * C_V), BF16), _sds((S, H * LANES), F32)],
        compiler_params=_params("parallel", "arbitrary"),
    )(qi, kj, qn, qr, kv, kv, kr)


def _mla_delta(do, o, *, name):
    S, W = do.shape
    tm = min(ROW_T, S)

    def body(do_ref, o_ref, d_ref):
        for h in range(C_HEADS):
            sl = slice(h * C_V, (h + 1) * C_V)
            d = jnp.sum(do_ref[:, sl].astype(F32) * o_ref[:, sl].astype(F32), axis=-1, keepdims=True)
            d_ref[:, sl] = jnp.broadcast_to(d, (tm, C_V))

    row = pl.BlockSpec((tm, W), lambda i: (i, 0))
    return pl.pallas_call(body, name=name, grid=(S // tm,), in_specs=[row, row], out_specs=row,
                          out_shape=_sds((S, W), F32), compiler_params=_params("parallel"))(do, o)


def _mla_flash_dq(qn, qr, kv, kr, do, lse, delta, *, name):
    S = qn.shape[0]
    H, G, t = C_HEADS, MLA_GROUP, min(FLASH_T, S)
    qi, kj = _causal_pairs(S // t, True)

    def body(qi_ref, kj_ref, qn_ref, qr_ref, kn_ref, v_ref, kr_ref, do_ref, lse_ref, dl_ref, dqn_ref, dqr_ref, an, ar):
        p_id = pl.program_id(1)
        i, j = qi_ref[p_id], kj_ref[p_id]

        @pl.when(j == 0)
        def _():
            an[...] = jnp.zeros_like(an)
            ar[...] = jnp.zeros_like(ar)

        def step(diagonal):
            for hh in range(G):
                sl = slice(hh * LANES, (hh + 1) * LANES)
                s = _mla_scores2(qn_ref[:, sl], qr_ref[hh], kn_ref[:, sl], kr_ref[...], diagonal)
                p = jnp.exp2(s - lse_ref[:, hh * LANES:hh * LANES + 1])
                dp = lax.dot_general(do_ref[:, sl], v_ref[:, sl], C11, preferred_element_type=F32)
                ds = (p * (dp - dl_ref[:, hh * LANES:hh * LANES + 1])).astype(BF16)
                an[:, sl] += jnp.dot(ds, kn_ref[:, sl], preferred_element_type=F32)
                ar[hh] += jnp.dot(ds, kr_ref[...], preferred_element_type=F32)

        @pl.when(j < i)
        def _():
            step(False)

        @pl.when(j == i)
        def _():
            step(True)
            dqn_ref[...] = (an[...] * MLA_SCALE).astype(BF16)
            dqr_ref[...] = ar[...] * MLA_SCALE

    wide = lambda which, off: pl.BlockSpec((t, G * LANES), lambda h, p, qi, kj: ((qi if which == "q" else kj)[p], off + h))
    qrb = pl.BlockSpec((G, t, C_ROPE), lambda h, p, qi, kj: (h, qi[p], 0))
    return pl.pallas_call(
        body, name=name,
        grid_spec=pltpu.PrefetchScalarGridSpec(
            num_scalar_prefetch=2, grid=(H // G, qi.shape[0]),
            in_specs=[wide("q", 0), qrb, wide("k", 0), wide("k", H // G),
                      pl.BlockSpec((t, C_ROPE), lambda h, p, qi, kj: (kj[p], 0)), wide("q", 0), wide("q", 0), wide("q", 0)],
            out_specs=[wide("q", 0), qrb],
            scratch_shapes=[pltpu.VMEM((t, G * LANES), F32), pltpu.VMEM((G, t, C_ROPE), F32)]),
        out_shape=[_sds((S, H * C_NOPE), BF16), _sds((H, S, C_ROPE), F32)],
        compiler_params=_params("parallel", "arbitrary"),
    )(qi, kj, qn, qr, kv, kv, kr, do, lse, delta)


def _mla_flash_dkv(qn, qr, kv, kr, do, lse, delta, *, name):
    S = qn.shape[0]
    H, G, t = C_HEADS, MLA_GROUP, min(FLASH_T, S)
    n = S // t
    qi, kj = _causal_pairs(n, False)

    def body(qi_ref, kj_ref, qn_ref, qr_ref, kn_ref, v_ref, kr_ref, do_ref, lse_ref, dl_ref, dkn_ref, dv_ref, dkr_ref, akn, av, akr):
        p_id = pl.program_id(1)
        i, j = qi_ref[p_id], kj_ref[p_id]

        def step(diagonal):
            for hh in range(G):
                sl = slice(hh * LANES, (hh + 1) * LANES)
                s = _mla_scores2(qn_ref[:, sl], qr_ref[hh], kn_ref[:, sl], kr_ref[...], diagonal)
                p = jnp.exp2(s - lse_ref[:, hh * LANES:hh * LANES + 1])
                dp = lax.dot_general(do_ref[:, sl], v_ref[:, sl], C11, preferred_element_type=F32)
                ds = (p * (dp - dl_ref[:, hh * LANES:hh * LANES + 1])).astype(BF16)
                av[:, sl] += lax.dot_general(p.astype(BF16), do_ref[:, sl], C00, preferred_element_type=F32)
                akn[:, sl] += lax.dot_general(ds, qn_ref[:, sl], C00, preferred_element_type=F32)
                akr[hh] += lax.dot_general(ds, qr_ref[hh], C00, preferred_element_type=F32)

        @pl.when(i == j)
        def _():
            akn[...] = jnp.zeros_like(akn)
            av[...] = jnp.zeros_like(av)
            akr[...] = jnp.zeros_like(akr)
            step(True)

        @pl.when(i > j)
        def _():
            step(False)

        @pl.when(i == n - 1)
        def _():
            dkn_ref[...] = (akn[...] * MLA_SCALE).astype(BF16)
            dv_ref[...] = av[...].astype(BF16)
            dkr_ref[...] = akr[...] * MLA_SCALE

    wide = lambda which, off: pl.BlockSpec((t, G * LANES), lambda h, p, qi, kj: ((qi if which == "q" else kj)[p], off + h))
    krb = pl.BlockSpec((G, t, C_ROPE), lambda h, p, qi, kj: (h, kj[p], 0))
    return pl.pallas_call(
        body, name=name,
        grid_spec=pltpu.PrefetchScalarGridSpec(
            num_scalar_prefetch=2, grid=(H // G, qi.shape[0]),
            in_specs=[wide("q", 0), pl.BlockSpec((G, t, C_ROPE), lambda h, p, qi, kj: (h, qi[p], 0)), wide("k", 0),
                      wide("k", H // G), pl.BlockSpec((t, C_ROPE), lambda h, p, qi, kj: (kj[p], 0)),
                      wide("q", 0), wide("q", 0), wide("q", 0)],
            out_specs=[wide("k", 0), wide("k", 0), krb],
            scratch_shapes=[pltpu.VMEM((t, G * LANES), F32), pltpu.VMEM((t, G * LANES), F32), pltpu.VMEM((G, t, C_ROPE), F32)]),
        out_shape=[_sds((S, H * C_NOPE), BF16), _sds((S, H * C_V), BF16), _sds((H, S, C_ROPE), F32)],
        compiler_params=_params("parallel", "arbitrary"),
    )(qi, kj, qn, qr, kv, kv, kr, do, lse, delta)


def _place():
    return lax.axis_index("x"), lax.axis_index("y"), lax.axis_index("c")


def _other_chips(x, y):
    return [(1 - x, y), (x, 1 - y), (1 - x, 1 - y)]


def _all_gather_chips(p, *, name):
    R, C = p.shape

    def body(p_ref, o_ref, send_sems, recv_sems, local_sem):
        x, y, c = _place()
        me = 2 * x + y
        local = pltpu.make_async_copy(p_ref, o_ref.at[me], local_sem)
        local.start()
        copies = [pltpu.make_async_remote_copy(src_ref=p_ref, dst_ref=o_ref.at[me], send_sem=send_sems.at[k],
                                               recv_sem=recv_sems.at[k], device_id=(px, py, c), device_id_type=MESH)
                  for k, (px, py) in enumerate(_other_chips(x, y))]
        for cp in copies:
            cp.start()
        for cp in copies:
            cp.wait()
        local.wait()

    any_spec = pl.BlockSpec(memory_space=pl.ANY)
    return pl.pallas_call(
        body, name=name, in_specs=[any_spec], out_specs=any_spec, out_shape=_sds((4, R, C), p.dtype),
        scratch_shapes=[pltpu.SemaphoreType.DMA((3,)), pltpu.SemaphoreType.DMA((3,)), pltpu.SemaphoreType.DMA(())],
    )(p)


def _shard_of(ref, axis, pos, size):
    idx = [slice(None)] * len(ref.shape)
    idx[axis] = pl.ds(pos * size, size)
    return ref.at[tuple(idx)]


def _shard_pos(chip, swapped):
    return (chip % 2) * 2 + chip // 2 if swapped else chip


class _Comm:
    def __init__(self, inputs, out_shapes, sems, start, finish, deliver):
        self.inputs, self.out_shapes, self.sems = list(inputs), list(out_shapes), list(sems)
        self.start, self.finish, self.deliver = start, finish, deliver


def _call(body, *, name, grid, in_specs, out_specs, out_shape, scratch_shapes=(), params, comm=None):
    in_specs, out_specs, out_shape, scratch_shapes = list(in_specs), list(out_specs), list(out_shape), list(scratch_shapes)
    if comm is None:
        return pl.pallas_call(body, name=name, grid=grid, in_specs=in_specs, out_specs=out_specs, out_shape=out_shape,
                              scratch_shapes=scratch_shapes, compiler_params=params)
    n_in, n_out, n_scr = len(in_specs), len(out_specs), len(scratch_shapes)
    c_in, c_out = len(comm.inputs), len(comm.out_shapes)

    def hosted(*refs):
        a, rest = refs[:n_in], refs[n_in:]
        cin, rest = rest[:c_in], rest[c_in:]
        o, rest = rest[:n_out], rest[n_out:]
        cout, rest = rest[:c_out], rest[c_out:]
        scr, sems = rest[:n_scr], rest[n_scr:]
        first = functools.reduce(jnp.logical_and, [pl.program_id(d) == 0 for d in range(len(grid))])
        last = functools.reduce(jnp.logical_and, [pl.program_id(d) == grid[d] - 1 for d in range(len(grid))])

        @pl.when(first)
        def _():
            comm.start(cin, cout, sems)

        body(*a, *o, *scr)

        @pl.when(last)
        def _():
            comm.finish(cin, cout, sems)

    any_spec = pl.BlockSpec(memory_space=pl.ANY)
    call = pl.pallas_call(
        hosted, name=name, grid=grid, in_specs=in_specs + [any_spec] * c_in, out_specs=out_specs + [any_spec] * c_out,
        out_shape=out_shape + comm.out_shapes, scratch_shapes=scratch_shapes + comm.sems, compiler_params=params)

    def run(*args):
        outs = call(*args, *comm.inputs)
        comm.deliver(outs[n_out:])
        return outs[:n_out]

    return run


def _run_comm(comm, *, name):
    c_in, c_out = len(comm.inputs), len(comm.out_shapes)

    def body(*refs):
        cin, cout, sems = refs[:c_in], refs[c_in:c_in + c_out], refs[c_in + c_out:]
        comm.start(cin, cout, sems)
        comm.finish(cin, cout, sems)

    any_spec = pl.BlockSpec(memory_space=pl.ANY)
    outs = pl.pallas_call(body, name=name, in_specs=[any_spec] * c_in, out_specs=[any_spec] * c_out,
                          out_shape=comm.out_shapes, scratch_shapes=comm.sems)(*comm.inputs)
    comm.deliver(outs)


def _gather_comm(items, deliver):
    n = len(items)
    shard_shapes = [a.shape if j is None else a.shape[1:] for a, j, _, _ in items]
    axes = [ax for _, _, ax, _ in items]
    swapped = [sw for _, _, _, sw in items]
    sizes = [s[a] for s, a in zip(shard_shapes, axes)]
    halves = [s[-2] // 2 for s in shard_shapes]
    full = [tuple(4 * d if i == a else d for i, d in enumerate(s)) for s, a in zip(shard_shapes, axes)]

    def mine(ins, k):
        j = items[k][1]
        return ins[k] if j is None else ins[k].at[j]

    def half_of(ref, k, half, chip=None):
        nd = len(ref.shape)
        split = nd - 2
        idx = [slice(None)] * nd
        start = half * halves[k]
        if chip is not None:
            pos = _shard_pos(chip, swapped[k]) * sizes[k]
            if axes[k] == split:
                start = start + pos
            else:
                idx[axes[k]] = pl.ds(pos, sizes[k])
        idx[split] = pl.ds(start, halves[k])
        return ref.at[tuple(idx)]

    def local_copy(ins, outs, sems, k, me):
        return pltpu.make_async_copy(mine(ins, k), _shard_of(outs[k], axes[k], _shard_pos(me, swapped[k]), sizes[k]), sems[4].at[k])

    def ici_copy(ins, outs, sems, k, j, peer, c, landing_chip):
        return pltpu.make_async_remote_copy(
            src_ref=half_of(mine(ins, k), k, c), dst_ref=half_of(outs[k], k, c, chip=landing_chip), send_sem=sems[0].at[3 * k + j],
            recv_sem=sems[1].at[3 * k + j], device_id=(peer[0], peer[1], c), device_id_type=MESH)

    def pass_copy(outs, sems, k, j, half, chip, sibling):
        region = half_of(outs[k], k, half, chip=chip)
        return pltpu.make_async_remote_copy(src_ref=region, dst_ref=region, send_sem=sems[2].at[3 * k + j],
                                            recv_sem=sems[3].at[3 * k + j], device_id=sibling, device_id_type=MESH)

    def start(ins, outs, sems):
        x, y, c = _place()
        me = 2 * x + y
        for k in range(n):
            local_copy(ins, outs, sems, k, me).start()
            for j, peer in enumerate(_other_chips(x, y)):
                ici_copy(ins, outs, sems, k, j, peer, c, me).start()

    def finish(ins, outs, sems):
        x, y, c = _place()
        me = 2 * x + y
        chips = _other_chips(x, y)
        sibling = (x, y, 1 - c)
        for k in range(n):
            for j, peer in enumerate(chips):
                ici_copy(ins, outs, sems, k, j, peer, c, 2 * peer[0] + peer[1]).wait_recv()
                pass_copy(outs, sems, k, j, c, 2 * peer[0] + peer[1], sibling).start()
        for k in range(n):
            for j, peer in enumerate(chips):
                pass_copy(outs, sems, k, j, 1 - c, 2 * peer[0] + peer[1], sibling).wait_recv()
        for k in range(n):
            local_copy(ins, outs, sems, k, me).wait()
            for j, peer in enumerate(chips):
                ici_copy(ins, outs, sems, k, j, peer, c, me).wait_send()
                pass_copy(outs, sems, k, j, c, 2 * peer[0] + peer[1], sibling).wait_send()

    return _Comm([a for a, _, _, _ in items], [_sds(f, a.dtype) for f, (a, _, _, _) in zip(full, items)],
                 [pltpu.SemaphoreType.DMA((3 * n,))] * 4 + [pltpu.SemaphoreType.DMA((n,))], start, finish, deliver)


def _scatter_comm(items, deliver):
    n = len(items)
    axes = [ax for _, ax, _ in items]
    swapped = [sw for _, _, sw in items]
    sizes = [g.shape[a] // 4 for g, a, _ in items]
    shard = [tuple(d // 4 if i == a else d for i, d in enumerate(g.shape)) for g, a, _ in items]

    def copies(ins, outs, sems):
        x, y, c = _place()
        me = 2 * x + y
        out = []
        for k in range(n):
            own = _shard_of(ins[k], axes[k], _shard_pos(me, swapped[k]), sizes[k])
            out.append(pltpu.make_async_copy(own, outs[k].at[3], sems[2].at[k]))
            for j, (px, py) in enumerate(_other_chips(x, y)):
                src = _shard_of(ins[k], axes[k], _shard_pos(2 * px + py, swapped[k]), sizes[k])
                out.append(pltpu.make_async_remote_copy(src_ref=src, dst_ref=outs[k].at[j], send_sem=sems[0].at[3 * k + j],
                                                        recv_sem=sems[1].at[3 * k + j], device_id=(px, py, c), device_id_type=MESH))
        return out

    def start(ins, outs, sems):
        for cp in copies(ins, outs, sems):
            cp.start()

    def finish(ins, outs, sems):
        for cp in copies(ins, outs, sems):
            cp.wait()

    return _Comm([g for g, _, _ in items], [_sds((4,) + s, g.dtype) for s, (g, _, _) in zip(shard, items)],
                 [pltpu.SemaphoreType.DMA((3 * n,)), pltpu.SemaphoreType.DMA((3 * n,)), pltpu.SemaphoreType.DMA((n,))],
                 start, finish, deliver)


def _row_tile(rows, cols, budget=2 ** 20):
    best = None
    for t in range(8, rows + 1, 8):
        if rows % t == 0 and t * cols * 4 <= budget:
            best = t
    return best or rows


def _sum_partials(recv, into, layer, layers, *, name):
    _, R, C = recv.shape
    tr = _row_tile(R, C)
    nt = R // tr

    def body(own_ref, r0_ref, r1_ref, r2_ref, *rest):
        f = lambda ref: ref[...].astype(F32)
        rest[-1][...] = ((f(own_ref) + f(r0_ref)) + f(r1_ref)) + f(r2_ref)

    rspec = lambda k: pl.BlockSpec((None, tr, C), lambda i: (k, i, 0))
    extra = [] if into is None else [pl.BlockSpec(memory_space=pl.ANY)]
    return pl.pallas_call(
        body, name=name, grid=(nt,), in_specs=[rspec(3), rspec(0), rspec(1), rspec(2)] + extra,
        out_specs=pl.BlockSpec((tr, C), lambda i: (layer * nt + i, 0)), out_shape=_sds((layers * R, C), F32),
        input_output_aliases={} if into is None else {4: 0}, compiler_params=_params("parallel"),
    )(recv, recv, recv, recv, *([] if into is None else [into]))


def _swap_cores(parts, *, name):
    n = len(parts)

    def body(*refs):
        ins, outs = refs[:n], refs[n:2 * n]
        send_sems, recv_sems = refs[2 * n:]
        x, y, c = _place()
        copies = [pltpu.make_async_remote_copy(src_ref=ins[k], dst_ref=outs[k], send_sem=send_sems.at[k], recv_sem=recv_sems.at[k],
                                               device_id=(x, y, 1 - c), device_id_type=MESH) for k in range(n)]
        for cp in copies:
            cp.start()
        for cp in copies:
            cp.wait()

    any_spec = pl.BlockSpec(memory_space=pl.ANY)
    return pl.pallas_call(
        body, name=name, in_specs=[any_spec] * n, out_specs=[any_spec] * n, out_shape=[_sds(p.shape, p.dtype) for p in parts],
        scratch_shapes=[pltpu.SemaphoreType.DMA((n,)), pltpu.SemaphoreType.DMA((n,))],
    )(*parts)


def _all_reduce_small(v, *, name):
    r, C = v.shape

    def body(v_ref, o_ref, buf, send_sems, recv_sems):
        x, y, c = _place()
        me = 4 * x + 2 * y + c
        buf[me] = v_ref[...]
        peers = []
        for k in range(1, 8):
            kx, ky, kc = (k >> 2) & 1, (k >> 1) & 1, k & 1
            px = 1 - x if kx else x
            py = 1 - y if ky else y
            pc = 1 - c if kc else c
            peers.append((px, py, pc))
        copies = []
        for k, peer in enumerate(peers):
            cp = pltpu.make_async_remote_copy(src_ref=v_ref, dst_ref=buf.at[me], send_sem=send_sems.at[k],
                                              recv_sem=recv_sems.at[me], device_id=peer, device_id_type=MESH)
            cp.start()
            copies.append(cp)
        for k, (px, py, pc) in enumerate(peers):
            src = 4 * px + 2 * py + pc
            pltpu.make_async_remote_copy(src_ref=v_ref, dst_ref=buf.at[src], send_sem=send_sems.at[k],
                                         recv_sem=recv_sems.at[src], device_id=peers[k], device_id_type=MESH).wait_recv()
        for cp in copies:
            cp.wait_send()
        acc = buf[0]
        for d in range(1, 8):
            acc = acc + buf[d]
        o_ref[...] = acc

    vm = pl.BlockSpec(memory_space=pltpu.VMEM)
    return pl.pallas_call(
        body, name=name, in_specs=[vm], out_specs=vm, out_shape=_sds((r, C), F32),
        scratch_shapes=[pltpu.VMEM((8, r, C), F32), pltpu.SemaphoreType.DMA((7,)), pltpu.SemaphoreType.DMA((8,))],
    )(v)


def _adamw(w, m, v, ga, gb, *, name):
    R, C = w.shape
    tr = _row_tile(R, C)
    has_b = gb is not None
    c1 = 1.0 / (1.0 - ADAM_B1 ** ADAM_STEP)
    c2 = 1.0 / (1.0 - ADAM_B2 ** ADAM_STEP)

    def body(*refs):
        if has_b:
            w_ref, m_ref, v_ref, ga_ref, gb_ref, g_ref, d_ref, nm_ref, nv_ref = refs
            g = ga_ref[...] + gb_ref[...]
        else:
            w_ref, m_ref, v_ref, ga_ref, g_ref, d_ref, nm_ref, nv_ref = refs
            g = ga_ref[...]
        nm = ADAM_B1 * m_ref[...] + (1.0 - ADAM_B1) * g
        nv = ADAM_B2 * v_ref[...] + (1.0 - ADAM_B2) * (g * g)
        g_ref[...] = g
        nm_ref[...] = nm
        nv_ref[...] = nv
        d_ref[...] = -ADAM_LR * ((nm * c1) / (jnp.sqrt(nv * c2) + ADAM_EPS) + ADAM_WD * w_ref[...])

    blk = pl.BlockSpec((tr, C), lambda i: (i, 0))
    n_in = 5 if has_b else 4
    args = (w, m, v, ga) + ((gb,) if has_b else ())
    return pl.pallas_call(body, name=name, grid=(R // tr,), in_specs=[blk] * n_in, out_specs=[blk] * 4,
                          out_shape=[_sds((R, C), F32)] * 4, compiler_params=_params("parallel"))(*args)


def _seg_rows(n, cols):
    return -(-n // (16 * cols)) * 16


def _pack(arrays, dtype, cols=PACK_COLS, row_mult=512):
    parts, rows = [], 0
    for a in arrays:
        n = int(np.prod(a.shape))
        r = _seg_rows(n, cols)
        flat = a.reshape(-1).astype(dtype)
        if r * cols != n:
            flat = jnp.pad(flat, (0, r * cols - n))
        parts.append(flat.reshape(r, cols))
        rows += r
    pad = -rows % row_mult
    if pad:
        parts.append(jnp.zeros((pad, cols), dtype))
    return jnp.concatenate(parts, axis=0)


def _unpack(packed, shapes, cols=PACK_COLS):
    out, r0 = [], 0
    for shp in shapes:
        n = int(np.prod(shp))
        used = -(-n // cols)
        out.append(packed[r0:r0 + used].reshape(-1)[:n].reshape(shp))
        r0 += _seg_rows(n, cols)
    return out


def _rope_tables(seq):
    inv = 1.0 / (ROPE_THETA ** (jnp.arange(0, 64, 2, dtype=F32) / 64))
    ang = jnp.arange(seq, dtype=F32)[:, None] * inv[None, :]
    cos, sin = jnp.cos(ang), jnp.sin(ang)
    cos128 = jnp.concatenate([cos, cos, cos, cos], axis=1)
    sin128 = jnp.concatenate([-sin, sin, -sin, sin], axis=1)
    return cos128, sin128


def _ffn_perm(a):
    lead = a.shape[:-1]
    nj = D_FF // FFN_TC
    return jnp.swapaxes(a.reshape(lead + (2, nj, FFN_TC)), -3, -2).reshape(lead + (2 * D_FF,))


def _ffn_unperm(a):
    lead = a.shape[:-1]
    nj = D_FF // FFN_TC
    return jnp.swapaxes(a.reshape(lead + (nj, 2, FFN_TC)), -3, -2).reshape(lead + (2 * D_FF,))


def _mixer_a_fwd(xb, w, j, cos, sin, tag, comm):
    qkv = _mm(xb, w["a_w_qkv"][j], name=f"mm_qkv_{tag}")
    qkv_r = _rope_cols(qkv, cos, sin, (A_HEADS + A_KV_HEADS) * A_HEAD_DIM, name=f"rope_qkv_{tag}")
    o = _swa_fwd(qkv_r, w["a_sinks"][j], name=f"swa_fwd_{tag}", comm=comm)
    y = _mm(o, w["a_w_o"][j], name=f"mm_ao_{tag}")
    return y, (xb, qkv_r, o)


def _mixer_a_bwd(dzb, res, w, j, cos, sin, tag, grads, comm):
    xb, qkv_r, o = res
    do = _mm(dzb, w["a_w_o"][j], tb=True, out_dtype=BF16, name=f"mm_dao_{tag}")
    grads["a_w_o"][j] = _mm(o, dzb, ta=True, out_dtype=BF16, tk=GRAD_TK, name=f"mm_gao_{tag}")
    dq, dcur, dprev, dsink = _swa_bwd(qkv_r, w["a_sinks"][j], do, cos, sin, name=f"swa_bwd_{tag}", comm=comm)
    grads["a_sinks"][j] = dsink[0, :A_HEADS]
    dqkv = _swa_dqkv(dq, dcur, dprev, cos, sin, name=f"swa_dqkv_{tag}")
    grads["a_w_qkv"][j] = _mm(xb, dqkv, ta=True, out_dtype=BF16, tk=GRAD_TK, name=f"mm_gqkv_{tag}")
    return _mm(dqkv, w["a_w_qkv"][j], tb=True, name=f"mm_dxa_{tag}")


def _mixer_b_fwd(xb, w, j, tag, comm):
    xw = _mm(xb, w["b_w_in"][j], name=f"mm_bin_{tag}")
    wri = jnp.concatenate([w["b_w_rgate"][j], w["b_w_igate"][j]], axis=-1)
    y, u, h = _lru_fwd(xw, w["b_conv_w"][j], w["b_conv_b"][j][None], wri, w["b_b_rgate"][j][None],
                       w["b_b_igate"][j][None], w["b_lambda"][j][None], name=f"lru_fwd_{tag}", comm=comm)
    out = _mm(y, w["b_w_o"][j], name=f"mm_bo_{tag}")
    return out, (xb, xw, wri, u, h, y)


def _mixer_b_bwd(dzb, res, w, j, tag, grads):
    xb, xw, wri, u, h, y = res
    dy = _mm(dzb, w["b_w_o"][j], tb=True, out_dtype=BF16, name=f"mm_dbo_{tag}")
    grads["b_w_o"][j] = _mm(y, dzb, ta=True, out_dtype=BF16, tk=GRAD_TK, name=f"mm_gbo_{tag}")
    dxw, dcw, dcb, dwri, dbr, dbi, dlam = _lru_bwd(
        xw, u, h, dy, w["b_conv_w"][j], wri, w["b_b_rgate"][j][None], w["b_b_igate"][j][None], w["b_lambda"][j][None],
        name=f"lru_bwd_{tag}")
    grads["b_conv_w"][j], grads["b_conv_b"][j] = dcw, dcb[0]
    grads["b_w_rgate"][j], grads["b_w_igate"][j] = dwri[..., :LRU_BLOCK_W].astype(BF16), dwri[..., LRU_BLOCK_W:].astype(BF16)
    grads["b_b_rgate"][j], grads["b_b_igate"][j], grads["b_lambda"][j] = dbr[0], dbi[0], dlam[0]
    grads["b_w_in"][j] = _mm(xb, dxw, ta=True, out_dtype=BF16, tk=GRAD_TK, name=f"mm_gbin_{tag}")
    return _mm(dxw, w["b_w_in"][j], tb=True, name=f"mm_dxb_{tag}")


def _mla_weights(w, j):
    H = C_HEADS
    uq = w["c_w_uq"][j].reshape(C_Q_RANK, H, C_NOPE + C_ROPE)
    ukv = w["c_w_ukv"][j].reshape(C_KV_RANK, H, C_NOPE + C_V)
    uq_n = uq[:, :, :C_NOPE].reshape(C_Q_RANK, H * C_NOPE)
    uq_r = uq[:, :, C_NOPE:].reshape(C_Q_RANK, H * C_ROPE)
    ukv_p = jnp.concatenate([ukv[:, :, :C_NOPE].reshape(C_KV_RANK, H * C_NOPE),
                             ukv[:, :, C_NOPE:].reshape(C_KV_RANK, H * C_V)], axis=1)
    return uq_n, uq_r, ukv_p


def _mixer_c_fwd(xb, w, j, cos, sin, tag):
    S = xb.shape[0]
    H = C_HEADS
    uq_n, uq_r, ukv_p = _mla_weights(w, j)
    c = _mm(xb, w["c_w_down"][j], name=f"mm_cdown_{tag}")
    cq, ckv, kr = _mla_pre(c, w["c_q_norm"][j][None], w["c_kv_norm"][j][None], cos, sin, name=f"mla_pre_{tag}")
    qn = _mm(cq, uq_n, out_dtype=BF16, name=f"mm_uqn_{tag}")
    qr_flat = _rope_heads(_mm(cq, uq_r, name=f"mm_uqr_{tag}"), cos, sin, transpose=False, name=f"rope_qr_{tag}")
    qr = jnp.transpose(qr_flat.reshape(S, H, C_ROPE), (1, 0, 2))
    kv = _mm(ckv, ukv_p, out_dtype=BF16, name=f"mm_ukv_{tag}")
    o, lse = _mla_flash_fwd(qn, qr, kv, kr, name=f"mla_fwd_{tag}")
    y = _mm(o, w["c_w_o"][j], name=f"mm_co_{tag}")
    return y, (xb, c, cq, ckv, kr, qn, qr, kv, o, lse, uq_n, uq_r, ukv_p)


def _mixer_c_bwd(dzb, res, w, j, cos, sin, tag, grads):
    xb, c, cq, ckv, kr, qn, qr, kv, o, lse, uq_n, uq_r, ukv_p = res
    S = xb.shape[0]
    H = C_HEADS
    do = _mm(dzb, w["c_w_o"][j], tb=True, out_dtype=BF16, name=f"mm_dco_{tag}")
    grads["c_w_o"][j] = _mm(o, dzb, ta=True, out_dtype=BF16, tk=GRAD_TK, name=f"mm_gco_{tag}")
    delta = _mla_delta(do, o, name=f"mla_delta_{tag}")
    dqn, dqr = _mla_flash_dq(qn, qr, kv, kr, do, lse, delta, name=f"mla_dq_{tag}")
    dkn, dv, dkr_h = _mla_flash_dkv(qn, qr, kv, kr, do, lse, delta, name=f"mla_dkv_{tag}")
    dkv = jnp.concatenate([dkn, dv], axis=1)
    dqr_flat = _rope_heads(jnp.transpose(dqr, (1, 0, 2)).reshape(S, H * C_ROPE), cos, sin, transpose=True, name=f"rope_dqr_{tag}")
    g_uq_n = _mm(cq, dqn, ta=True, out_dtype=BF16, tk=GRAD_TK, name=f"mm_guqn_{tag}")
    g_uq_r = _mm(cq, dqr_flat, ta=True, out_dtype=BF16, tk=GRAD_TK, name=f"mm_guqr_{tag}")
    g_ukv = _mm(ckv, dkv, ta=True, out_dtype=BF16, tk=GRAD_TK, name=f"mm_gukv_{tag}")
    grads["c_w_uq"][j] = jnp.concatenate([g_uq_n.reshape(C_Q_RANK, H, C_NOPE), g_uq_r.reshape(C_Q_RANK, H, C_ROPE)],
                                         axis=2).reshape(C_Q_RANK, H * (C_NOPE + C_ROPE))
    grads["c_w_ukv"][j] = jnp.concatenate([g_ukv[:, :H * C_NOPE].reshape(C_KV_RANK, H, C_NOPE),
                                           g_ukv[:, H * C_NOPE:].reshape(C_KV_RANK, H, C_V)],
                                          axis=2).reshape(C_KV_RANK, H * (C_NOPE + C_V))
    dcq_a = _mm(dqn, uq_n, tb=True, name=f"mm_dcqa_{tag}")
    dcq_b = _mm(dqr_flat, uq_r, tb=True, name=f"mm_dcqb_{tag}")
    dckv = _mm(dkv, ukv_p, tb=True, name=f"mm_dckv_{tag}")
    dc, dqg, dkvg = _mla_post_bwd(c, dcq_a, dcq_b, dckv, dkr_h, w["c_q_norm"][j][None], w["c_kv_norm"][j][None], cos, sin,
                                  name=f"mla_post_{tag}")
    grads["c_q_norm"][j], grads["c_kv_norm"][j] = dqg[0], dkvg[0]
    grads["c_w_down"][j] = _mm(xb, dc, ta=True, out_dtype=BF16, tk=GRAD_TK, name=f"mm_gcdown_{tag}")
    return _mm(dc, w["c_w_down"][j], tb=True, name=f"mm_dxc_{tag}")


def _layer_big(i, mixer=True, rest=True):
    kind, j = i % N_MIXERS, i // N_MIXERS
    own = [[("a_w_qkv", j), ("a_w_o", j)], [("b_w_in", j), ("b_w_rgate", j), ("b_w_igate", j), ("b_w_o", j)],
           [("c_w_down", j), ("c_w_uq", j), ("c_w_ukv", j), ("c_w_o", j)]][kind]
    return (own if mixer else []) + ([("x_w_q", i), ("x_w_o", i), ("f_w_up", i), ("f_w_down", i)] if rest else [])


def _local_step(x, mem, target, w, n_layers, gathers, scatter):
    S = x.shape[0]
    cos, sin = _rope_tables(S)
    grads = {n: [None] * n_layers[n] for n in WEIGHTS if n != "mem_w_kv"}
    mkv = _mm(mem, w["mem_w_kv"], out_dtype=BF16, tm=MEM_LEN, name="mm_memkv")

    xs, xb = x, x.astype(BF16)
    saved = []
    for i in range(DEPTH):
        kind, j = i % N_MIXERS, i // N_MIXERS
        tag = f"l{i}"
        if kind == 0:
            y, res = _mixer_a_fwd(xb, w, j, cos, sin, tag, gathers.get(f"swa_fwd_{tag}"))
        elif kind == 1:
            y, res = _mixer_b_fwd(xb, w, j, tag, gathers.get(f"lru_fwd_{tag}"))
        else:
            y, res = _mixer_c_fwd(xb, w, j, cos, sin, tag)
        x1, x1b, xh1, rs1 = _ln_fwd(xs, y, w["ln_g"][i, 0][None], w["ln_b"][i, 0][None], name=f"ln1_{tag}")
        q = _mm(x1b, w["x_w_q"][i], out_dtype=BF16, name=f"mm_xq_{tag}")
        o = _xattn_fwd(q, mkv, name=f"xattn_fwd_{tag}")
        y2 = _mm(o, w["x_w_o"][i], name=f"mm_xo_{tag}")
        x2, x2b, xh2, rs2 = _ln_fwd(x1, y2, w["ln_g"][i, 1][None], w["ln_b"][i, 1][None], name=f"ln2_{tag}")
        w_up = w["f_w_up"][i]
        cwp, cbp = _ffn_perm(w["f_conv_w"][i]), _ffn_perm(w["f_conv_b"][i][None])
        hh = _mm(x2b, w_up, out_dtype=BF16, tn=FFN_TC, name=f"mm_up_{tag}")
        a = _ffn_act_fwd(hh, cwp, cbp, name=f"ffn_act_{tag}", comm=gathers.get(f"ffn_act_{tag}"))
        y3 = _mm(a, w["f_w_down"][i], name=f"mm_down_{tag}")
        x3, x3b, xh3, rs3 = _ln_fwd(x2, y3, w["ln_g"][i, 2][None], w["ln_b"][i, 2][None], name=f"ln3_{tag}")
        saved.append((res, (xh1, rs1, x1b), (q, o, xh2, rs2, x2b), (w_up, cwp, cbp, hh, a, xh3, rs3)))
        xs, xb = x3, x3b

    d2, loss = _loss_fwd(xs, target, name="loss")
    d1 = None

    dmkv = None
    ln_dg = [[None] * 3 for _ in range(DEPTH)]
    ln_db = [[None] * 3 for _ in range(DEPTH)]
    for i in reversed(range(DEPTH)):
        kind, j = i % N_MIXERS, i // N_MIXERS
        tag = f"l{i}"
        res, (xh1, rs1, x1b), (q, o, xh2, rs2, x2b), (w_up, cwp, cbp, hh, a, xh3, rs3) = saved[i]
        dz3, dz3b, ln_dg[i][2], ln_db[i][2] = _ln_bwd(d1, d2, xh3, rs3, w["ln_g"][i, 2][None], name=f"ln3_bwd_{tag}")
        da = _mm(dz3b, w["f_w_down"][i], tb=True, out_dtype=BF16, name=f"mm_ddown_{tag}")
        grads["f_w_down"][i] = _mm(a, dz3b, ta=True, out_dtype=BF16, tm=FFN_TC, tk=GRAD_TK, name=f"mm_gdown_{tag}")
        later = scatter(_layer_big(i + 1), grads) if i + 1 < DEPTH else None
        dh, dcw, dcb = _ffn_act_bwd(hh, da, cwp, cbp, name=f"ffn_act_bwd_{tag}", comm=later)
        grads["f_conv_w"][i], grads["f_conv_b"][i] = _ffn_unperm(dcw), _ffn_unperm(dcb)[0]
        grads["f_w_up"][i] = _mm(x2b, dh, ta=True, out_dtype=BF16, tn=FFN_TC, tk=GRAD_TK, name=f"mm_gup_{tag}")
        dx2 = _mm(dh, w_up, tb=True, tm=512, name=f"mm_dxf_{tag}")

        dz2, dz2b, ln_dg[i][1], ln_db[i][1] = _ln_bwd(dz3, dx2, xh2, rs2, w["ln_g"][i, 1][None], name=f"ln2_bwd_{tag}")
        do = _mm(dz2b, w["x_w_o"][i], tb=True, out_dtype=BF16, name=f"mm_dxo_{tag}")
        grads["x_w_o"][i] = _mm(o, dz2b, ta=True, out_dtype=BF16, tk=GRAD_TK, name=f"mm_gxo_{tag}")
        dq, dmkv_i = _xattn_bwd(q, mkv, do, name=f"xattn_bwd_{tag}")
        dmkv = dmkv_i if dmkv is None else dmkv + dmkv_i
        grads["x_w_q"][i] = _mm(x1b, dq, ta=True, out_dtype=BF16, tk=GRAD_TK, name=f"mm_gxq_{tag}")
        dx1 = _mm(dq, w["x_w_q"][i], tb=True, name=f"mm_dxq_{tag}")

        dz1, dz1b, ln_dg[i][0], ln_db[i][0] = _ln_bwd(dz2, dx1, xh1, rs1, w["ln_g"][i, 0][None], name=f"ln1_bwd_{tag}")
        if kind == 0:
            done = scatter(_layer_big(i, mixer=False), grads) if i == 0 else None
            dx0 = _mixer_a_bwd(dz1b, res, w, j, cos, sin, tag, grads, done)
        elif kind == 1:
            dx0 = _mixer_b_bwd(dz1b, res, w, j, tag, grads)
        else:
            dx0 = _mixer_c_bwd(dz1b, res, w, j, cos, sin, tag, grads)
        d1, d2 = dz1, dx0

    grad_x = _axpy(d1, d2, name="grad_x")
    big = [n for n, _ in SHARDED[:N_BIG]]
    out = {n: (g if n in big else jnp.stack(g, axis=0)) for n, g in grads.items() if n not in ("ln_g", "ln_b")}
    out["mem_w_kv"] = _mm(mem, dmkv, ta=True, out_dtype=BF16, tm=512, name="mm_gmemkv")
    out["ln_g"] = jnp.stack([jnp.concatenate(r, axis=0) for r in ln_dg], axis=0)
    out["ln_b"] = jnp.stack([jnp.concatenate(r, axis=0) for r in ln_db], axis=0)
    return loss, grad_x, out


def kernel(x, mem, a_w_qkv, a_sinks, a_w_o, b_w_in, b_conv_w, b_conv_b, b_w_rgate, b_b_rgate, b_w_igate, b_b_igate, b_lambda, b_w_o, c_w_down, c_q_norm, c_kv_norm, c_w_uq, c_w_ukv, c_w_o, mem_w_kv, x_w_q, x_w_o, f_w_up, f_conv_w, f_conv_b, f_w_down, ln_g, ln_b, loss_target, m_a_w_qkv, m_a_sinks, m_a_w_o, m_b_w_in, m_b_conv_w, m_b_conv_b, m_b_w_rgate, m_b_b_rgate, m_b_w_igate, m_b_b_igate, m_b_lambda, m_b_w_o, m_c_w_down, m_c_q_norm, m_c_kv_norm, m_c_w_uq, m_c_w_ukv, m_c_w_o, m_mem_w_kv, m_x_w_q, m_x_w_o, m_f_w_up, m_f_conv_w, m_f_conv_b, m_f_w_down, m_ln_g, m_ln_b, v_a_w_qkv, v_a_sinks, v_a_w_o, v_b_w_in, v_b_conv_w, v_b_conv_b, v_b_w_rgate, v_b_b_rgate, v_b_w_igate, v_b_b_igate, v_b_lambda, v_b_w_o, v_c_w_down, v_c_q_norm, v_c_kv_norm, v_c_w_uq, v_c_w_ukv, v_c_w_o, v_mem_w_kv, v_x_w_q, v_x_w_o, v_f_w_up, v_f_conv_w, v_f_conv_b, v_f_w_down, v_ln_g, v_ln_b):
    loc = locals()
    shard = {n: loc[n] for n in WEIGHTS}
    mom = {n: loc["m_" + n] for n in WEIGHTS}
    var = {n: loc["v_" + n] for n in WEIGHTS}
    names = [n for n, _ in SHARDED]
    axis = dict(SHARDED)
    big, small = names[:N_BIG], names[N_BIG:]

    chip = 2 * lax.axis_index("x") + lax.axis_index("y")
    n_layers = {n: shard[n].shape[0] for n in WEIGHTS if n != "mem_w_kv"}
    layer_axis = lambda n, j: axis[n] - (0 if j is None else 1)

    shard_b = {n: shard[n].astype(BF16) for n in big}
    w = {n: [None] * n_layers[n] for n in big if n != "mem_w_kv"}

    def gather(pairs):
        def deliver(outs):
            for (n, j), o in zip(pairs, outs):
                if j is None:
                    w[n] = o
                else:
                    w[n][j] = o
        return _gather_comm([(shard_b[n], j, layer_axis(n, j), n == "f_w_up") for n, j in pairs], deliver)

    _run_comm(gather(_layer_big(0) + [("mem_w_kv", None)]), name="gather_l0")
    gathers = {"swa_fwd_l0": gather(_layer_big(1)), "lru_fwd_l1": gather(_layer_big(2)), "ffn_act_l1": gather(_layer_big(3))}
    got = _all_gather_chips(_pack([shard[n] for n in small], F32), name="gather_small")
    per_chip = [_unpack(got[s], [shard[n].shape for n in small]) for s in range(4)]
    for k, n in enumerate(small):
        w[n] = jnp.concatenate([per_chip[s][k] for s in range(4)], axis=axis[n])
    for n in REPLICATED:
        w[n] = shard[n]

    recv = {}

    def scatter(pairs, grads):
        def deliver(outs):
            recv.update(dict(zip(pairs, outs)))
        return _scatter_comm([(grads[n] if j is None else grads[n][j], layer_axis(n, j), n == "f_w_up") for n, j in pairs], deliver)

    loss, grad_x, g = _local_step(x[0], mem[0], loss_target[0], w, n_layers, gathers, scatter)
    _run_comm(scatter(_layer_big(0, rest=False) + [("mem_w_kv", None)], g), name="scatter_last")

    view = {n: (int(np.prod(shard[n].shape[:-1])), shard[n].shape[-1]) for n in big}
    parts = []
    for n in big:
        layers = n_layers.get(n, 1)
        part = None
        for j in range(layers):
            r = recv[(n, j if n in n_layers else None)]
            part = _sum_partials(r.reshape(4, view[n][0] // layers, view[n][1]), part, j, layers, name=f"sum_{n}_{j}")
        parts.append(part)
    sibs = _swap_cores(parts, name="swap_cores")
    grad_o, delta_o, m_o, v_o = {}, {}, {}, {}
    for n, part, sib in zip(big, parts, sibs):
        res = _adamw(shard[n].reshape(view[n]), mom[n].reshape(view[n]), var[n].reshape(view[n]), part, sib, name=f"adamw_{n}")
        for d, r in zip((grad_o, delta_o, m_o, v_o), res):
            d[n] = r.reshape(shard[n].shape)

    rest = small + REPLICATED
    vec = _pack([g[n] for n in rest] + [loss], F32, cols=LANES, row_mult=8)
    tot = _unpack(_all_reduce_small(vec, name="allreduce_small"), [g[n].shape for n in rest] + [(1, 1)], cols=LANES)
    loss_tot = tot[-1].reshape(())
    mine = {n: t for n, t in zip(rest, tot)}
    for n in small:
        size = shard[n].shape[axis[n]]
        mine[n] = lax.dynamic_slice_in_dim(mine[n], chip * size, size, axis=axis[n])
    rpack = lambda d: _pack([d[n] for n in rest], F32, cols=LANES, row_mult=8)
    res = _adamw(rpack(shard), rpack(mom), rpack(var), rpack(mine), None, name="adamw_small")
    for d, r in zip((grad_o, delta_o, m_o, v_o), res):
        d.update(dict(zip(rest, _unpack(r, [shard[n].shape for n in rest], cols=LANES))))

    return (loss_tot, grad_x[None], *[grad_o[n] for n in WEIGHTS], *[delta_o[n] for n in WEIGHTS],
            *[m_o[n] for n in WEIGHTS], *[v_o[n] for n in WEIGHTS])
```

```python
import functools
import math

import numpy as np
import jax
import jax.numpy as jnp
from jax import lax
from jax.experimental import pallas as pl
from jax.experimental.pallas import tpu as pltpu

F32 = jnp.float32
BF16 = jnp.bfloat16
MESH = pl.DeviceIdType.MESH

D_MODEL = 1024
DEPTH = 4
N_MIXERS = 3
MEM_LEN = 256
BLOCK = 128
ROPE_THETA = 10000.0
NEG = -1e30
LN_EPS = 1e-5
RMS_EPS = 1e-6
A_HEADS, A_KV_HEADS, A_HEAD_DIM = 16, 4, 64
LRU_BLOCKS, LRU_BLOCK_W, LRU_CONV, LRU_C = 4, 256, 4, 8.0
C_HEADS, C_NOPE, C_ROPE, C_V, C_Q_RANK, C_KV_RANK = 8, 128, 64, 128, 384, 256
X_HEADS, X_HEAD_DIM = 4, 256
D_FF, FFN_CONV = 2816, 3
ALPHA = (2.0 * DEPTH) ** 0.25
ADAM_LR, ADAM_B1, ADAM_B2, ADAM_EPS, ADAM_WD, ADAM_STEP = 0.001, 0.9, 0.999, 1e-08, 0.01, 10

VMEM_LIMIT = 56 * 2 ** 20
LANES = 128
PACK_COLS = 1024
ROW_T = 512
ACT_T = 256
LRU_T = 256
FLASH_T = 512
FFN_TC = 1408
MM_T = 1024
GRAD_TK = 1024

C11 = (((1,), (1,)), ((), ()))
C00 = (((0,), (0,)), ((), ()))

SHARDED = [
    ("a_w_qkv", 2), ("a_w_o", 1), ("b_w_in", 2), ("b_w_rgate", 2), ("b_w_igate", 2), ("b_w_o", 1), ("c_w_down", 1),
    ("c_w_uq", 2), ("c_w_ukv", 2), ("c_w_o", 1), ("mem_w_kv", 1), ("x_w_q", 1), ("x_w_o", 1), ("f_w_up", 2),
    ("f_w_down", 1),
    ("b_conv_w", 2), ("c_q_norm", 1), ("c_kv_norm", 1), ("f_conv_w", 2), ("ln_g", 2), ("ln_b", 2),
]
N_BIG = 15
REPLICATED = ["a_sinks", "b_conv_b", "b_b_rgate", "b_b_igate", "b_lambda", "f_conv_b"]
WEIGHTS = ["a_w_qkv", "a_sinks", "a_w_o", "b_w_in", "b_conv_w", "b_conv_b", "b_w_rgate", "b_b_rgate", "b_w_igate",
           "b_b_igate", "b_lambda", "b_w_o", "c_w_down", "c_q_norm", "c_kv_norm", "c_w_uq", "c_w_ukv", "c_w_o",
           "mem_w_kv", "x_w_q", "x_w_o", "f_w_up", "f_conv_w", "f_conv_b", "f_w_down", "ln_g", "ln_b"]


def _params(*sem):
    return pltpu.CompilerParams(dimension_semantics=sem, vmem_limit_bytes=VMEM_LIMIT)


def _sds(shape, dtype):
    return jax.ShapeDtypeStruct(tuple(shape), dtype)


def _mm(a, b, *, name, ta=False, tb=False, out_dtype=F32, tm=None, tn=None, tk=None, comm=None):
    (K, M) = a.shape if ta else a.shape[::-1]
    (N, K2) = b.shape if tb else b.shape[::-1]
    assert K == K2, (a.shape, b.shape, ta, tb)
    tm = min(tm or MM_T, M)
    tn = min(tn or N, N)
    tk = min(tk or K, K)
    assert M % tm == 0 and N % tn == 0 and K % tk == 0, (M, N, K, tm, tn, tk)
    nk = K // tk
    use_acc = nk > 1 and out_dtype != F32
    dims = (((0 if ta else 1,), (1 if tb else 0,)), ((), ()))

    def body(a_ref, b_ref, o_ref, *scratch):
        p = lax.dot_general(a_ref[...].astype(BF16), b_ref[...].astype(BF16), dims, preferred_element_type=F32)
        if nk == 1:
            o_ref[...] = p.astype(out_dtype)
        else:
            acc = scratch[0] if use_acc else o_ref
            k = pl.program_id(2)

            @pl.when(k == 0)
            def _():
                acc[...] = p

            @pl.when(k > 0)
            def _():
                acc[...] += p

            if use_acc:
                @pl.when(k == nk - 1)
                def _():
                    o_ref[...] = acc[...].astype(out_dtype)

    a_spec = pl.BlockSpec((tk, tm), lambda i, j, k: (k, i)) if ta else pl.BlockSpec((tm, tk), lambda i, j, k: (i, k))
    b_spec = pl.BlockSpec((tn, tk), lambda i, j, k: (j, k)) if tb else pl.BlockSpec((tk, tn), lambda i, j, k: (k, j))
    return _call(
        body, name=name, grid=(M // tm, N // tn, nk), in_specs=[a_spec, b_spec],
        out_specs=[pl.BlockSpec((tm, tn), lambda i, j, k: (i, j))], out_shape=[_sds((M, N), out_dtype)],
        scratch_shapes=[pltpu.VMEM((tm, tn), F32)] if use_acc else [],
        params=_params("parallel", "parallel", "arbitrary"), comm=comm,
    )(a, b)[0]


def _shift_down(cur, prev8, d):
    rolled = pltpu.roll(cur, d, 0)
    rid = lax.broadcasted_iota(jnp.int32, prev8.shape, 0)
    head = jnp.where(rid < d, pltpu.roll(prev8, d, 0), rolled[0:8])
    return jnp.concatenate([head, rolled[8:]], axis=0)


def _shift_up(cur, next8, d):
    n = cur.shape[0]
    rolled = pltpu.roll(cur, n - d, 0)
    rid = lax.broadcasted_iota(jnp.int32, next8.shape, 0)
    tail = jnp.where(rid >= 8 - d, pltpu.roll(next8, 8 - d, 0), rolled[n - 8:n])
    return jnp.concatenate([rolled[0:n - 8], tail], axis=0)


def _swap_halves(x):
    w = x.shape[-1]
    if w == 64:
        return jnp.concatenate([x[:, 32:64], x[:, 0:32]], axis=1)
    lane = lax.broadcasted_iota(jnp.int32, x.shape, 1)
    return jnp.where((lane % 64) < 32, pltpu.roll(x, w - 32, 1), pltpu.roll(x, 32, 1))


def _tile_lanes(t, w):
    return t if w == t.shape[-1] else jnp.concatenate([t] * (w // t.shape[-1]), axis=1)


def _rope(x, cos, sin):
    w = x.shape[-1]
    if w == 64:
        cos, sin = cos[:, :64], sin[:, :64]
    else:
        cos, sin = _tile_lanes(cos, w), _tile_lanes(sin, w)
    return x * cos + _swap_halves(x) * sin


def _rope_t(x, cos, sin):
    w = x.shape[-1]
    if w == 64:
        cos, sin = cos[:, :64], sin[:, :64]
    else:
        cos, sin = _tile_lanes(cos, w), _tile_lanes(sin, w)
    return x * cos - _swap_halves(x) * sin


def _sigmoid(x):
    return 1.0 / (1.0 + jnp.exp(-x))


def _gelu_and_grad(x):
    c0, c1 = math.sqrt(2.0 / math.pi), 0.044715
    t = jnp.tanh(c0 * (x + c1 * x * x * x))
    g = 0.5 * x * (1.0 + t)
    dg = 0.5 * (1.0 + t) + 0.5 * x * (1.0 - t * t) * c0 * (1.0 + 3.0 * c1 * x * x)
    return g, dg


def _neg_expm1(x):
    series = -x * (1.0 + x * (0.5 + x * (1.0 / 6.0 + x * (1.0 / 24.0 + x * (1.0 / 120.0)))))
    return jnp.where(x > -0.1, series, 1.0 - jnp.exp(x))


def _softplus_neg(lam):
    z = -lam
    e = jnp.exp(-jnp.abs(z))
    log1p = jnp.where(e < 0.01, e * (1.0 - e * (0.5 - e * (1.0 / 3.0))), jnp.log(1.0 + e))
    sp = jnp.maximum(z, 0.0) + log1p
    dsp = -_sigmoid(z)
    return sp, dsp


def _ln_fwd(x, y, g, b, *, name):
    S, D = x.shape
    tm = min(ROW_T, S)

    def body(x_ref, y_ref, g_ref, b_ref, o_ref, ob_ref, xh_ref, rs_ref):
        z = ALPHA * x_ref[...] + y_ref[...]
        mu = jnp.mean(z, axis=-1, keepdims=True)
        zc = z - mu
        var = jnp.mean(zc * zc, axis=-1, keepdims=True)
        r = lax.rsqrt(var + LN_EPS)
        xh = zc * r
        o = xh * g_ref[...] + b_ref[...]
        o_ref[...] = o
        ob_ref[...] = o.astype(BF16)
        xh_ref[...] = xh
        rs_ref[...] = r

    row = pl.BlockSpec((tm, D), lambda i: (i, 0))
    vec = pl.BlockSpec((1, D), lambda i: (0, 0))
    return pl.pallas_call(
        body, name=name, grid=(S // tm,), in_specs=[row, row, vec, vec],
        out_specs=[row, row, row, pl.BlockSpec((tm, 1), lambda i: (i, 0))],
        out_shape=[_sds((S, D), F32), _sds((S, D), BF16), _sds((S, D), F32), _sds((S, 1), F32)],
        compiler_params=_params("parallel"),
    )(x, y, g, b)


def _ln_bwd(d1, d2, xh, rs, g, *, name):
    S, D = xh.shape
    tm = min(ROW_T, S)
    has_d1 = d1 is not None

    def body(*refs):
        if has_d1:
            d1_ref, d2_ref, xh_ref, rs_ref, g_ref, dz_ref, dzb_ref, dg_ref, db_ref = refs
            dout = ALPHA * d1_ref[...] + d2_ref[...]
        else:
            d2_ref, xh_ref, rs_ref, g_ref, dz_ref, dzb_ref, dg_ref, db_ref = refs
            dout = d2_ref[...]
        xh_v = xh_ref[...]
        dxh = dout * g_ref[...]
        m1 = jnp.mean(dxh, axis=-1, keepdims=True)
        m2 = jnp.mean(dxh * xh_v, axis=-1, keepdims=True)
        dz = rs_ref[...] * (dxh - m1 - xh_v * m2)
        dz_ref[...] = dz
        dzb_ref[...] = dz.astype(BF16)

        @pl.when(pl.program_id(0) == 0)
        def _():
            dg_ref[...] = jnp.zeros_like(dg_ref)
            db_ref[...] = jnp.zeros_like(db_ref)

        dg_ref[...] += jnp.sum(dout * xh_v, axis=0, keepdims=True)
        db_ref[...] += jnp.sum(dout, axis=0, keepdims=True)

    row = pl.BlockSpec((tm, D), lambda i: (i, 0))
    vec = pl.BlockSpec((1, D), lambda i: (0, 0))
    ins = ([row] if has_d1 else []) + [row, row, pl.BlockSpec((tm, 1), lambda i: (i, 0)), vec]
    args = ([d1] if has_d1 else []) + [d2, xh, rs, g]
    return pl.pallas_call(
        body, name=name, grid=(S // tm,), in_specs=ins, out_specs=[row, row, vec, vec],
        out_shape=[_sds((S, D), F32), _sds((S, D), BF16), _sds((1, D), F32), _sds((1, D), F32)],
        compiler_params=_params("arbitrary"),
    )(*args)


def _loss_fwd(y, target, *, name):
    S, D = y.shape
    tm = min(ROW_T, S)

    def body(y_ref, t_ref, d_ref, l_ref):
        e = y_ref[...] - t_ref[...]
        d_ref[...] = e * (1.0 / D)

        @pl.when(pl.program_id(0) == 0)
        def _():
            l_ref[...] = jnp.zeros_like(l_ref)

        part = jnp.sum(e * e, axis=0, keepdims=True)
        l_ref[...] += (0.5 / D) * jnp.sum(part, axis=1, keepdims=True)

    row = pl.BlockSpec((tm, D), lambda i: (i, 0))
    return pl.pallas_call(
        body, name=name, grid=(S // tm,), in_specs=[row, row],
        out_specs=[row, pl.BlockSpec((1, 1), lambda i: (0, 0))], out_shape=[_sds((S, D), F32), _sds((1, 1), F32)],
        compiler_params=_params("arbitrary"),
    )(y, target)


def _axpy(d1, d2, *, name):
    S, D = d1.shape
    tm = min(ROW_T, S)

    def body(a_ref, b_ref, o_ref):
        o_ref[...] = ALPHA * a_ref[...] + b_ref[...]

    row = pl.BlockSpec((tm, D), lambda i: (i, 0))
    return pl.pallas_call(body, name=name, grid=(S // tm,), in_specs=[row, row], out_specs=row,
                          out_shape=_sds((S, D), F32), compiler_params=_params("parallel"))(d1, d2)


def _ffn_act_fwd(h, cw, cb, *, name, comm=None):
    S, W = h.shape
    tc = FFN_TC
    nj = W // (2 * tc)
    tm = min(ACT_T, S)

    def body(h_ref, w_ref, b_ref, a_ref, carry):
        @pl.when(pl.program_id(1) == 0)
        def _():
            carry[...] = jnp.zeros_like(carry)

        cur = h_ref[...].astype(F32)
        prev8 = carry[...]
        hc = cur * w_ref[2:3, :] + _shift_down(cur, prev8, 1) * w_ref[1:2, :] + _shift_down(cur, prev8, 2) * w_ref[0:1, :]
        hc = hc + b_ref[...]
        carry[...] = cur[tm - 8:tm]
        hg, hu = hc[:, :tc], hc[:, tc:]
        a_ref[...] = (hg * _sigmoid(hg) * hu).astype(BF16)

    return _call(
        body, name=name, grid=(nj, S // tm),
        in_specs=[pl.BlockSpec((tm, 2 * tc), lambda j, i: (i, j)), pl.BlockSpec((3, 2 * tc), lambda j, i: (0, j)),
                  pl.BlockSpec((1, 2 * tc), lambda j, i: (0, j))],
        out_specs=[pl.BlockSpec((tm, tc), lambda j, i: (i, j))], out_shape=[_sds((S, W // 2), BF16)],
        scratch_shapes=[pltpu.VMEM((8, 2 * tc), F32)],
        params=_params("parallel", "arbitrary"), comm=comm,
    )(h, cw, cb)[0]


def _ffn_act_bwd(h, da, cw, cb, *, name, comm=None):
    S, W = h.shape
    tc = FFN_TC
    nj = W // (2 * tc)
    tm = min(ACT_T, S)
    ni = S // tm

    def body(h_ref, hp_ref, da_ref, w_ref, b_ref, dh_ref, dw_ref, db_ref, carry):
        i = pl.program_id(1)
        r = ni - 1 - i

        @pl.when(i == 0)
        def _():
            carry[...] = jnp.zeros_like(carry)
            dw_ref[...] = jnp.zeros_like(dw_ref)
            db_ref[...] = jnp.zeros_like(db_ref)

        cur = h_ref[...].astype(F32)
        prev8 = jnp.where(r > 0, hp_ref[8:16, :].astype(F32), 0.0)
        sh = [cur, _shift_down(cur, prev8, 1), _shift_down(cur, prev8, 2)]
        hc = sh[0] * w_ref[2:3, :] + sh[1] * w_ref[1:2, :] + sh[2] * w_ref[0:1, :] + b_ref[...]
        hg, hu = hc[:, :tc], hc[:, tc:]
        d = da_ref[...].astype(F32)
        sg = _sigmoid(hg)
        dg = d * hu * (sg * (1.0 + hg * (1.0 - sg)))
        du = d * (hg * sg)
        dhc = jnp.concatenate([dg, du], axis=1)
        db_ref[...] += jnp.sum(dhc, axis=0, keepdims=True)
        for k in range(3):
            dw_ref[k:k + 1, :] += jnp.sum(dhc * sh[2 - k], axis=0, keepdims=True)
        next8 = carry[...]
        dh = dhc * w_ref[2:3, :] + _shift_up(dhc, next8, 1) * w_ref[1:2, :] + _shift_up(dhc, next8, 2) * w_ref[0:1, :]
        carry[...] = dhc[0:8]
        dh_ref[...] = dh.astype(BF16)

    rev = lambda j, i: (ni - 1 - i, j)
    return _call(
        body, name=name, grid=(nj, ni),
        in_specs=[pl.BlockSpec((tm, 2 * tc), rev),
                  pl.BlockSpec((16, 2 * tc), lambda j, i: (jnp.maximum((ni - 1 - i) * (tm // 16) - 1, 0), j)),
                  pl.BlockSpec((tm, tc), rev), pl.BlockSpec((3, 2 * tc), lambda j, i: (0, j)),
                  pl.BlockSpec((1, 2 * tc), lambda j, i: (0, j))],
        out_specs=[pl.BlockSpec((tm, 2 * tc), rev), pl.BlockSpec((3, 2 * tc), lambda j, i: (0, j)),
                   pl.BlockSpec((1, 2 * tc), lambda j, i: (0, j))],
        out_shape=[_sds((S, W), BF16), _sds((3, W), F32), _sds((1, W), F32)],
        scratch_shapes=[pltpu.VMEM((8, 2 * tc), F32)],
        params=_params("parallel", "arbitrary"), comm=comm,
    )(h, h, da, cw, cb)


def _xattn_probs(q, k):
    s = lax.dot_general(q, k, C11, preferred_element_type=F32) * (X_HEAD_DIM ** -0.5)
    p = jnp.exp(s - jnp.max(s, axis=-1, keepdims=True))
    return p / jnp.sum(p, axis=-1, keepdims=True)


def _xattn_fwd(q, mkv, *, name):
    S, D = q.shape
    tm = min(ROW_T, S)

    def body(q_ref, k_ref, v_ref, o_ref):
        for h in range(X_HEADS):
            sl = slice(h * X_HEAD_DIM, (h + 1) * X_HEAD_DIM)
            p = _xattn_probs(q_ref[:, sl], k_ref[:, sl])
            o_ref[:, sl] = jnp.dot(p.astype(BF16), v_ref[:, sl], preferred_element_type=F32).astype(BF16)

    return pl.pallas_call(
        body, name=name, grid=(S // tm,),
        in_specs=[pl.BlockSpec((tm, D), lambda i: (i, 0)), pl.BlockSpec((MEM_LEN, D), lambda i: (0, 0)),
                  pl.BlockSpec((MEM_LEN, D), lambda i: (0, 1))],
        out_specs=pl.BlockSpec((tm, D), lambda i: (i, 0)), out_shape=_sds((S, D), BF16),
        compiler_params=_params("parallel"),
    )(q, mkv, mkv)


def _xattn_bwd(q, mkv, do, *, name):
    S, D = q.shape
    tm = min(ROW_T, S)
    scale = X_HEAD_DIM ** -0.5

    def body(q_ref, k_ref, v_ref, do_ref, dq_ref, dkv_ref):
        @pl.when(pl.program_id(0) == 0)
        def _():
            dkv_ref[...] = jnp.zeros_like(dkv_ref)

        for h in range(X_HEADS):
            sl = slice(h * X_HEAD_DIM, (h + 1) * X_HEAD_DIM)
            sv = slice(D + h * X_HEAD_DIM, D + (h + 1) * X_HEAD_DIM)
            qh, kh, vh, doh = q_ref[:, sl], k_ref[:, sl], v_ref[:, sl], do_ref[:, sl]
            p = _xattn_probs(qh, kh)
            dp = lax.dot_general(doh, vh, C11, preferred_element_type=F32)
            ds = (p * (dp - jnp.sum(p * dp, axis=-1, keepdims=True)) * scale).astype(BF16)
            dq_ref[:, sl] = jnp.dot(ds, kh, preferred_element_type=F32).astype(BF16)
            dkv_ref[:, sl] += lax.dot_general(ds, qh, C00, preferred_element_type=F32)
            dkv_ref[:, sv] += lax.dot_general(p.astype(BF16), doh, C00, preferred_element_type=F32)

    row = pl.BlockSpec((tm, D), lambda i: (i, 0))
    return pl.pallas_call(
        body, name=name, grid=(S // tm,),
        in_specs=[row, pl.BlockSpec((MEM_LEN, D), lambda i: (0, 0)), pl.BlockSpec((MEM_LEN, D), lambda i: (0, 1)), row],
        out_specs=[row, pl.BlockSpec((MEM_LEN, 2 * D), lambda i: (0, 0))],
        out_shape=[_sds((S, D), BF16), _sds((MEM_LEN, 2 * D), F32)],
        compiler_params=_params("arbitrary"),
    )(q, mkv, mkv, do)


def _rope_cols(x, cos, sin, n_rope, *, name, comm=None):
    S, W = x.shape
    tm = min(ROW_T, S)

    def body(x_ref, c_ref, s_ref, o_ref):
        o_ref[:, :n_rope] = _rope(x_ref[:, :n_rope], c_ref[...], s_ref[...]).astype(BF16)
        if n_rope < W:
            o_ref[:, n_rope:] = x_ref[:, n_rope:].astype(BF16)

    row = pl.BlockSpec((tm, W), lambda i: (i, 0))
    tab = pl.BlockSpec((tm, LANES), lambda i: (i, 0))
    return _call(body, name=name, grid=(S // tm,), in_specs=[row, tab, tab], out_specs=[row],
                 out_shape=[_sds((S, W), BF16)], params=_params("arbitrary"), comm=comm)(x, cos, sin)[0]


def _swa_band(n, stacked):
    qi = jnp.bitwise_and(lax.broadcasted_iota(jnp.int32, (stacked * BLOCK, 2 * BLOCK), 0), BLOCK - 1)
    kj = lax.broadcasted_iota(jnp.int32, (stacked * BLOCK, 2 * BLOCK), 1)
    first = jnp.where(n > 0, 0, BLOCK)
    return ((kj < BLOCK) & (kj > qi + first)) | ((kj >= BLOCK) & (kj - BLOCK <= qi))


def _swa_sink_rows(sink_ref, heads):
    row = lax.broadcasted_iota(jnp.int32, (len(heads) * BLOCK, 1), 0)
    col = jnp.full(row.shape, sink_ref[heads[-1]], F32)
    for gi in range(len(heads) - 2, -1, -1):
        col = jnp.where(row < (gi + 1) * BLOCK, sink_ref[heads[gi]], col)
    return col


def _swa_probs(q, k, band, sink):
    s = lax.dot_general(q, k, C11, preferred_element_type=F32) * (A_HEAD_DIM ** -0.5)
    s = jnp.where(band, s, NEG)
    m = jnp.maximum(jnp.max(s, axis=-1, keepdims=True), sink)
    p = jnp.exp(s - m)
    e_sink = jnp.exp(sink - m)
    den = jnp.sum(p, axis=-1, keepdims=True) + e_sink
    return p / den, e_sink / den


def _swa_specs():
    nq, nkv = A_HEADS * A_HEAD_DIM, A_KV_HEADS * A_HEAD_DIM
    kb, vb = nq // nkv, nq // nkv + 1
    prev = lambda n: jnp.maximum(n - 1, 0)
    return [pl.BlockSpec((BLOCK, nq), lambda n: (n, 0)),
            pl.BlockSpec((BLOCK, nkv), lambda n: (n, kb)), pl.BlockSpec((BLOCK, nkv), lambda n: (prev(n), kb)),
            pl.BlockSpec((BLOCK, nkv), lambda n: (n, vb)), pl.BlockSpec((BLOCK, nkv), lambda n: (prev(n), vb)),
            pl.BlockSpec(memory_space=pltpu.SMEM)]


def _swa_fwd(qkv, sinks, *, name, comm=None):
    S = qkv.shape[0]
    hd, grp = A_HEAD_DIM, A_HEADS // A_KV_HEADS

    def body(q_ref, kc_ref, kp_ref, vc_ref, vp_ref, sink_ref, o_ref):
        band = _swa_band(pl.program_id(0), grp)
        qa, kc, kp, vc, vp = q_ref[...], kc_ref[...], kp_ref[...], vc_ref[...], vp_ref[...]
        for hk in range(A_KV_HEADS):
            ks = slice(hk * hd, (hk + 1) * hd)
            k = jnp.concatenate([kp[:, ks], kc[:, ks]], axis=0)
            v = jnp.concatenate([vp[:, ks], vc[:, ks]], axis=0)
            heads = [hk * grp + gi for gi in range(grp)]
            q = jnp.concatenate([qa[:, h * hd:(h + 1) * hd] for h in heads], axis=0)
            p, _ = _swa_probs(q, k, band, _swa_sink_rows(sink_ref, heads))
            o = jnp.dot(p.astype(BF16), v, preferred_element_type=F32).astype(BF16)
            for gi, h in enumerate(heads):
                o_ref[:, h * hd:(h + 1) * hd] = o[gi * BLOCK:(gi + 1) * BLOCK]

    return _call(
        body, name=name, grid=(S // BLOCK,), in_specs=_swa_specs(),
        out_specs=[pl.BlockSpec((BLOCK, A_HEADS * hd), lambda n: (n, 0))], out_shape=[_sds((S, A_HEADS * hd), BF16)],
        params=_params("arbitrary"), comm=comm,
    )(qkv, qkv, qkv, qkv, qkv, sinks)[0]


def _swa_bwd(qkv, sinks, do, cos, sin, *, name, comm=None):
    S = qkv.shape[0]
    hd, grp = A_HEAD_DIM, A_HEADS // A_KV_HEADS
    nq, nkv = A_HEADS * hd, A_KV_HEADS * hd
    scale = hd ** -0.5

    def body(q_ref, kc_ref, kp_ref, vc_ref, vp_ref, sink_ref, do_ref, c_ref, s_ref, dq_ref, dc_ref, dp_ref, ds_ref, dq_s):
        @pl.when(pl.program_id(0) == 0)
        def _():
            ds_ref[...] = jnp.zeros_like(ds_ref)

        band = _swa_band(pl.program_id(0), grp)
        lane = lax.broadcasted_iota(jnp.int32, (1, LANES), 1)
        qa, kc, kp, vc, vp, doa = q_ref[...], kc_ref[...], kp_ref[...], vc_ref[...], vp_ref[...], do_ref[...]
        dsink = jnp.zeros((1, LANES), F32)
        for hk in range(A_KV_HEADS):
            ks = slice(hk * hd, (hk + 1) * hd)
            k = jnp.concatenate([kp[:, ks], kc[:, ks]], axis=0)
            v = jnp.concatenate([vp[:, ks], vc[:, ks]], axis=0)
            heads = [hk * grp + gi for gi in range(grp)]
            q = jnp.concatenate([qa[:, h * hd:(h + 1) * hd] for h in heads], axis=0)
            dog = jnp.concatenate([doa[:, h * hd:(h + 1) * hd] for h in heads], axis=0)
            p, p_sink = _swa_probs(q, k, band, _swa_sink_rows(sink_ref, heads))
            dpr = lax.dot_general(dog, v, C11, preferred_element_type=F32)
            delta = jnp.sum(p * dpr, axis=-1, keepdims=True)
            dsc = (p * (dpr - delta) * scale).astype(BF16)
            dqg = jnp.dot(dsc, k, preferred_element_type=F32)
            dk = lax.dot_general(dsc, q, C00, preferred_element_type=F32)
            dv = lax.dot_general(p.astype(BF16), dog, C00, preferred_element_type=F32)
            sink_term = p_sink * delta
            for gi, h in enumerate(heads):
                rows = slice(gi * BLOCK, (gi + 1) * BLOCK)
                dq_s[:, h * hd:(h + 1) * hd] = dqg[rows]
                dsink = dsink + jnp.where(lane == h, -jnp.sum(sink_term[rows], axis=0, keepdims=True), 0.0)
            dp_ref[:, ks] = dk[:BLOCK]
            dc_ref[:, ks] = dk[BLOCK:]
            dp_ref[:, nkv + hk * hd:nkv + (hk + 1) * hd] = dv[:BLOCK]
            dc_ref[:, nkv + hk * hd:nkv + (hk + 1) * hd] = dv[BLOCK:]
        ds_ref[...] += dsink
        dq_ref[...] = _rope_t(dq_s[...], c_ref[...], s_ref[...]).astype(BF16)

    tab = pl.BlockSpec((BLOCK, LANES), lambda n: (n, 0))
    blk = lambda w: pl.BlockSpec((BLOCK, w), lambda n: (n, 0))
    return _call(
        body, name=name, grid=(S // BLOCK,), in_specs=_swa_specs() + [blk(nq), tab, tab],
        out_specs=[blk(nq), blk(2 * nkv), blk(2 * nkv), pl.BlockSpec((1, LANES), lambda n: (0, 0))],
        out_shape=[_sds((S, nq), BF16), _sds((S, 2 * nkv), F32), _sds((S, 2 * nkv), F32), _sds((1, LANES), F32)],
        scratch_shapes=[pltpu.VMEM((BLOCK, nq), F32)],
        params=_params("arbitrary"), comm=comm,
    )(qkv, qkv, qkv, qkv, qkv, sinks, do, cos, sin)


def _swa_dqkv(dq, dcur, dprev, cos, sin, *, name):
    S, nq = dq.shape
    nkv = dcur.shape[1] // 2
    nb = S // BLOCK

    def body(dq_ref, dc_ref, dp_ref, c_ref, s_ref, o_ref):
        o_ref[:, :nq] = dq_ref[...]
        d = dc_ref[...] + jnp.where(pl.program_id(0) < nb - 1, dp_ref[...], 0.0)
        o_ref[:, nq:nq + nkv] = _rope_t(d[:, :nkv], c_ref[...], s_ref[...]).astype(BF16)
        o_ref[:, nq + nkv:] = d[:, nkv:].astype(BF16)

    tab = pl.BlockSpec((BLOCK, LANES), lambda m: (m, 0))
    blk = lambda w: pl.BlockSpec((BLOCK, w), lambda m: (m, 0))
    return pl.pallas_call(
        body, name=name, grid=(nb,),
        in_specs=[blk(nq), blk(2 * nkv), pl.BlockSpec((BLOCK, 2 * nkv), lambda m: (jnp.minimum(m + 1, nb - 1), 0)), tab, tab],
        out_specs=blk(nq + 2 * nkv), out_shape=_sds((S, nq + 2 * nkv), BF16), compiler_params=_params("parallel"),
    )(dq, dcur, dprev, cos, sin)


def _lru_gates(u, wri_ref, br, bi, sp):
    ub = u.astype(BF16)
    rs, igs = [], []
    for hb in range(LRU_BLOCKS):
        sl = slice(hb * LRU_BLOCK_W, (hb + 1) * LRU_BLOCK_W)
        ri = jnp.dot(ub[:, sl], wri_ref[hb], preferred_element_type=F32)
        rs.append(ri[:, :LRU_BLOCK_W])
        igs.append(ri[:, LRU_BLOCK_W:])
    r = _sigmoid(jnp.concatenate(rs, axis=1) + br)
    ig = _sigmoid(jnp.concatenate(igs, axis=1) + bi)
    la = -LRU_C * r * sp
    a = jnp.exp(la)
    sq = jnp.sqrt(_neg_expm1(2.0 * la))
    return r, ig, a, sq


def _lru_fwd(xw, cw, cb, wri, br, bi, lam, *, name, comm=None):
    S = xw.shape[0]
    W = D_MODEL
    tm = min(LRU_T, S)

    def body(gate_ref, up_ref, cw_ref, cb_ref, wri_ref, br_ref, bi_ref, lam_ref, y_ref, u_ref, h_ref, cu, ch, a_s, b_s):
        @pl.when(pl.program_id(0) == 0)
        def _():
            cu[...] = jnp.zeros_like(cu)
            ch[...] = jnp.zeros_like(ch)

        up = up_ref[...]
        prev8 = cu[...]
        u = up * cw_ref[3:4, :] + cb_ref[...]
        for d in range(1, LRU_CONV):
            u = u + _shift_down(up, prev8, d) * cw_ref[3 - d:4 - d, :]
        cu[...] = up[tm - 8:tm]
        u_ref[...] = u
        sp, _ = _softplus_neg(lam_ref[...])
        _, ig, a, sq = _lru_gates(u, wri_ref, br_ref[...], bi_ref[...], sp)
        a_s[...] = a
        b_s[...] = sq * (ig * u)
        rid = lax.broadcasted_iota(jnp.int32, (8, W), 0)

        def tile(t, h):
            r0 = pl.multiple_of(t * 8, 8)
            at, bt = a_s[pl.ds(r0, 8), :], b_s[pl.ds(r0, 8), :]
            out = jnp.zeros((8, W), F32)
            for j in range(8):
                h = at[j:j + 1, :] * h + bt[j:j + 1, :]
                out = jnp.where(rid == j, h, out)
            h_ref[pl.ds(r0, 8), :] = out
            return h

        ch[0:1, :] = lax.fori_loop(0, tm // 8, tile, ch[0:1, :])
        g, _ = _gelu_and_grad(gate_ref[...])
        y_ref[...] = (h_ref[...] * g).astype(BF16)

    row = pl.BlockSpec((tm, W), lambda i: (i, 0))
    full = lambda shape: pl.BlockSpec(shape, lambda i: (0,) * len(shape))
    return _call(
        body, name=name, grid=(S // tm,),
        in_specs=[row, pl.BlockSpec((tm, W), lambda i: (i, 1)), full((LRU_CONV, W)), full((1, W)),
                  full((LRU_BLOCKS, LRU_BLOCK_W, 2 * LRU_BLOCK_W)), full((1, W)), full((1, W)), full((1, W))],
        out_specs=[row, row, row], out_shape=[_sds((S, W), BF16), _sds((S, W), F32), _sds((S, W), F32)],
        scratch_shapes=[pltpu.VMEM((8, W), F32), pltpu.VMEM((8, W), F32), pltpu.VMEM((tm, W), F32), pltpu.VMEM((tm, W), F32)],
        params=_params("arbitrary"), comm=comm,
    )(xw, xw, cw, cb, wri, br, bi, lam)


def _lru_bwd(xw, u, h, dy, cw, wri, br, bi, lam, *, name):
    S = xw.shape[0]
    W = D_MODEL
    tm = min(LRU_T, S)
    nb = S // tm

    def body(gate_ref, up_ref, upp_ref, u_ref, h_ref, hp_ref, dy_ref, cw_ref, wri_ref, br_ref, bi_ref, lam_ref,
             dxw_ref, dcw_ref, dcb_ref, dwri_ref, dbr_ref, dbi_ref, dlam_ref, cg, cdu, a_s, d_s, g_s):
        i = pl.program_id(0)
        r_blk = nb - 1 - i

        @pl.when(i == 0)
        def _():
            cg[...] = jnp.zeros_like(cg)
            cdu[...] = jnp.zeros_like(cdu)
            for ref in (dcw_ref, dcb_ref, dwri_ref, dbr_ref, dbi_ref, dlam_ref):
                ref[...] = jnp.zeros_like(ref)

        u = u_ref[...]
        hv = h_ref[...]
        sp, dsp = _softplus_neg(lam_ref[...])
        r, ig, a, sq = _lru_gates(u, wri_ref, br_ref[...], bi_ref[...], sp)
        dy = dy_ref[...].astype(F32)
        g, dgelu = _gelu_and_grad(gate_ref[...])
        dxw_ref[:, :W] = (dy * hv * dgelu).astype(BF16)
        a_s[...] = a
        d_s[...] = dy * g
        rid = lax.broadcasted_iota(jnp.int32, (8, W), 0)

        def tile(t, c):
            r0 = pl.multiple_of((tm // 8 - 1 - t) * 8, 8)
            at, dt = a_s[pl.ds(r0, 8), :], d_s[pl.ds(r0, 8), :]
            out = jnp.zeros((8, W), F32)
            for j in range(7, -1, -1):
                gt = dt[j:j + 1, :] + c
                c = at[j:j + 1, :] * gt
                out = jnp.where(rid == j, gt, out)
            g_s[pl.ds(r0, 8), :] = out
            return c

        cg[0:1, :] = lax.fori_loop(0, tm // 8, tile, cg[0:1, :])
        gt = g_s[...]
        hprev8 = jnp.where(r_blk > 0, hp_ref[...], 0.0)
        da = gt * _shift_down(hv, hprev8, 1)
        iu = ig * u
        d_iu = gt * sq
        dla = da * a - (gt * iu) * (a * a) / sq
        dlam_ref[...] += jnp.sum(dla * r, axis=0, keepdims=True) * (-LRU_C) * dsp
        dr_pre = dla * (-LRU_C) * sp * r * (1.0 - r)
        di_pre = d_iu * u * ig * (1.0 - ig)
        dbr_ref[...] += jnp.sum(dr_pre, axis=0, keepdims=True)
        dbi_ref[...] += jnp.sum(di_pre, axis=0, keepdims=True)
        ub = u.astype(BF16)
        dus = []
        for hb in range(LRU_BLOCKS):
            sl = slice(hb * LRU_BLOCK_W, (hb + 1) * LRU_BLOCK_W)
            dri = jnp.concatenate([dr_pre[:, sl], di_pre[:, sl]], axis=1).astype(BF16)
            dus.append(lax.dot_general(dri, wri_ref[hb], C11, preferred_element_type=F32))
            dwri_ref[hb] += lax.dot_general(ub[:, sl], dri, C00, preferred_element_type=F32)
        du = d_iu * ig + jnp.concatenate(dus, axis=1)
        dcb_ref[...] += jnp.sum(du, axis=0, keepdims=True)
        up = up_ref[...]
        upprev8 = jnp.where(r_blk > 0, upp_ref[...], 0.0)
        dcw_ref[3:4, :] += jnp.sum(du * up, axis=0, keepdims=True)
        for d in range(1, LRU_CONV):
            dcw_ref[3 - d:4 - d, :] += jnp.sum(du * _shift_down(up, upprev8, d), axis=0, keepdims=True)
        next8 = cdu[...]
        dup = du * cw_ref[3:4, :]
        for d in range(1, LRU_CONV):
            dup = dup + _shift_up(du, next8, d) * cw_ref[3 - d:4 - d, :]
        cdu[...] = du[0:8]
        dxw_ref[:, W:] = dup.astype(BF16)

    rev = lambda c: (lambda i: (nb - 1 - i, c))
    halo = lambda c: (lambda i: (jnp.maximum((nb - 1 - i) * (tm // 8) - 1, 0), c))
    full = lambda shape: pl.BlockSpec(shape, lambda i: (0,) * len(shape))
    vec = full((1, W))
    return pl.pallas_call(
        body, name=name, grid=(nb,),
        in_specs=[pl.BlockSpec((tm, W), rev(0)), pl.BlockSpec((tm, W), rev(1)), pl.BlockSpec((8, W), halo(1)),
                  pl.BlockSpec((tm, W), rev(0)), pl.BlockSpec((tm, W), rev(0)), pl.BlockSpec((8, W), halo(0)),
                  pl.BlockSpec((tm, W), rev(0)), full((LRU_CONV, W)), full((LRU_BLOCKS, LRU_BLOCK_W, 2 * LRU_BLOCK_W)),
                  vec, vec, vec],
        out_specs=[pl.BlockSpec((tm, 2 * W), rev(0)), full((LRU_CONV, W)), vec,
                   full((LRU_BLOCKS, LRU_BLOCK_W, 2 * LRU_BLOCK_W)), vec, vec, vec],
        out_shape=[_sds((S, 2 * W), BF16), _sds((LRU_CONV, W), F32), _sds((1, W), F32),
                   _sds((LRU_BLOCKS, LRU_BLOCK_W, 2 * LRU_BLOCK_W), F32), _sds((1, W), F32), _sds((1, W), F32),
                   _sds((1, W), F32)],
        scratch_shapes=[pltpu.VMEM((8, W), F32), pltpu.VMEM((8, W), F32), pltpu.VMEM((tm, W), F32),
                        pltpu.VMEM((tm, W), F32), pltpu.VMEM((tm, W), F32)],
        compiler_params=_params("arbitrary"),
    )(xw, xw, xw, u, h, h, dy, cw, wri, br, bi, lam)


def _rms(x, g):
    r = lax.rsqrt(jnp.mean(x * x, axis=-1, keepdims=True) + RMS_EPS)
    return x * r * g, r


def _mla_pre(c, qg, kvg, cos, sin, *, name):
    S = c.shape[0]
    tm = min(ROW_T, S)
    q0, k0 = C_Q_RANK, C_Q_RANK + C_KV_RANK

    def body(c_ref, qg_ref, kvg_ref, cs_ref, sn_ref, cq_ref, ckv_ref, kr_ref):
        cq_ref[...] = _rms(c_ref[:, :q0], qg_ref[...])[0].astype(BF16)
        ckv_ref[...] = _rms(c_ref[:, q0:k0], kvg_ref[...])[0].astype(BF16)
        kr_ref[...] = _rope(c_ref[:, k0:], cs_ref[...], sn_ref[...]).astype(BF16)

    blk = lambda w: pl.BlockSpec((tm, w), lambda i: (i, 0))
    vec = lambda w: pl.BlockSpec((1, w), lambda i: (0, 0))
    return pl.pallas_call(
        body, name=name, grid=(S // tm,),
        in_specs=[blk(c.shape[1]), vec(C_Q_RANK), vec(C_KV_RANK), blk(LANES), blk(LANES)],
        out_specs=[blk(C_Q_RANK), blk(C_KV_RANK), blk(C_ROPE)],
        out_shape=[_sds((S, C_Q_RANK), BF16), _sds((S, C_KV_RANK), BF16), _sds((S, C_ROPE), BF16)],
        compiler_params=_params("parallel"),
    )(c, qg, kvg, cos, sin)


def _mla_post_bwd(c, dcq_a, dcq_b, dckv, dkr_h, qg, kvg, cos, sin, *, name):
    S = c.shape[0]
    tm = min(ROW_T, S)
    q0, k0 = C_Q_RANK, C_Q_RANK + C_KV_RANK

    def rms_bwd(x, g, dy):
        r = lax.rsqrt(jnp.mean(x * x, axis=-1, keepdims=True) + RMS_EPS)
        uu = dy * g
        dx = r * uu - x * (r * r * r) * jnp.mean(uu * x, axis=-1, keepdims=True)
        return dx, jnp.sum(dy * x * r, axis=0, keepdims=True)

    def body(c_ref, da_ref, db_ref, dkv_ref, dkr_ref, qg_ref, kvg_ref, cs_ref, sn_ref, dc_ref, dqg_ref, dkvg_ref):
        @pl.when(pl.program_id(0) == 0)
        def _():
            dqg_ref[...] = jnp.zeros_like(dqg_ref)
            dkvg_ref[...] = jnp.zeros_like(dkvg_ref)

        dx, dg = rms_bwd(c_ref[:, :q0], qg_ref[...], da_ref[...] + db_ref[...])
        dc_ref[:, :q0] = dx.astype(BF16)
        dqg_ref[...] += dg
        dx, dg = rms_bwd(c_ref[:, q0:k0], kvg_ref[...], dkv_ref[...])
        dc_ref[:, q0:k0] = dx.astype(BF16)
        dkvg_ref[...] += dg
        dkr = dkr_ref[0]
        for hh in range(1, C_HEADS):
            dkr = dkr + dkr_ref[hh]
        dc_ref[:, k0:] = _rope_t(dkr, cs_ref[...], sn_ref[...]).astype(BF16)

    blk = lambda w: pl.BlockSpec((tm, w), lambda i: (i, 0))
    vec = lambda w: pl.BlockSpec((1, w), lambda i: (0, 0))
    return pl.pallas_call(
        body, name=name, grid=(S // tm,),
        in_specs=[blk(c.shape[1]), blk(C_Q_RANK), blk(C_Q_RANK), blk(C_KV_RANK),
                  pl.BlockSpec((C_HEADS, tm, C_ROPE), lambda i: (0, i, 0)), vec(C_Q_RANK), vec(C_KV_RANK), blk(LANES), blk(LANES)],
        out_specs=[blk(c.shape[1]), vec(C_Q_RANK), vec(C_KV_RANK)],
        out_shape=[_sds(c.shape, BF16), _sds((1, C_Q_RANK), F32), _sds((1, C_KV_RANK), F32)],
        compiler_params=_params("arbitrary"),
    )(c, dcq_a, dcq_b, dckv, dkr_h, qg, kvg, cos, sin)


def _rope_heads(x, cos, sin, *, transpose, name):
    S, W = x.shape
    tm = min(ROW_T, S)
    fn = _rope_t if transpose else _rope

    def body(x_ref, c_ref, s_ref, o_ref):
        o_ref[...] = fn(x_ref[...].astype(F32), c_ref[...], s_ref[...]).astype(BF16)

    row = pl.BlockSpec((tm, W), lambda i: (i, 0))
    tab = pl.BlockSpec((tm, LANES), lambda i: (i, 0))
    return pl.pallas_call(body, name=name, grid=(S // tm,), in_specs=[row, tab, tab], out_specs=row,
                          out_shape=_sds((S, W), BF16), compiler_params=_params("parallel"))(x, cos, sin)


MLA_GROUP = 2
MLA_SCALE = (C_NOPE + C_ROPE) ** -0.5
LOG2E = 1.4426950408889634


def _mla_scores2(qn, qr, kn, kr, diagonal):
    s = lax.dot_general(qn, kn, C11, preferred_element_type=F32) + lax.dot_general(qr, kr, C11, preferred_element_type=F32)
    s = s * (MLA_SCALE * LOG2E)
    if diagonal:
        row = lax.broadcasted_iota(jnp.int32, s.shape, 0)
        col = lax.broadcasted_iota(jnp.int32, s.shape, 1)
        s = jnp.where(col <= row, s, NEG)
    return s


def _causal_pairs(n, query_major):
    if query_major:
        pairs = [(i, j) for i in range(n) for j in range(i + 1)]
    else:
        pairs = [(i, j) for j in range(n) for i in range(j, n)]
    return jnp.asarray([p[0] for p in pairs], jnp.int32), jnp.asarray([p[1] for p in pairs], jnp.int32)


def _mla_flash_fwd(qn, qr, kv, kr, *, name):
    S = qn.shape[0]
    H, G, t = C_HEADS, MLA_GROUP, min(FLASH_T, S)
    qi, kj = _causal_pairs(S // t, True)

    def body(qi_ref, kj_ref, qn_ref, qr_ref, kn_ref, v_ref, kr_ref, o_ref, lse_ref, *scr):
        m_s, l_s, acc = scr[:G], scr[G:2 * G], scr[2 * G:]
        p_id = pl.program_id(1)
        i, j = qi_ref[p_id], kj_ref[p_id]
        sls = [slice(hh * LANES, (hh + 1) * LANES) for hh in range(G)]

        @pl.when(j == 0)
        def _():
            for hh in range(G):
                m_s[hh][...] = jnp.full_like(m_s[hh], NEG)
                l_s[hh][...] = jnp.zeros_like(l_s[hh])
                acc[hh][...] = jnp.zeros_like(acc[hh])

        def step(diagonal):
            ss = [_mla_scores2(qn_ref[:, sls[hh]], qr_ref[hh], kn_ref[:, sls[hh]], kr_ref[...], diagonal) for hh in range(G)]
            for hh in range(G):
                m_prev = m_s[hh][...]
                m_new = jnp.maximum(m_prev, jnp.max(ss[hh], axis=-1, keepdims=True))
                corr = jnp.exp2(m_prev - m_new)
                p = jnp.exp2(ss[hh] - m_new[:, 0:1])
                l_s[hh][...] = corr * l_s[hh][...] + jnp.sum(p, axis=-1, keepdims=True)
                acc[hh][...] = corr * acc[hh][...] + jnp.dot(p.astype(BF16), v_ref[:, sls[hh]], preferred_element_type=F32)
                m_s[hh][...] = m_new

        @pl.when(j < i)
        def _():
            step(False)

        @pl.when(j == i)
        def _():
            step(True)
            for hh in range(G):
                o_ref[:, sls[hh]] = (acc[hh][...] / l_s[hh][...]).astype(BF16)
                lse_ref[:, sls[hh]] = m_s[hh][...] + jnp.log2(l_s[hh][...])

    wide = lambda which, off: pl.BlockSpec((t, G * LANES), lambda h, p, qi, kj: ((qi if which == "q" else kj)[p], off + h))
    return pl.pallas_call(
        body, name=name,
        grid_spec=pltpu.PrefetchScalarGridSpec(
            num_scalar_prefetch=2, grid=(H // G, qi.shape[0]),
            in_specs=[wide("q", 0), pl.BlockSpec((G, t, C_ROPE), lambda h, p, qi, kj: (h, qi[p], 0)),
                      wide("k", 0), wide("k", H // G), pl.BlockSpec((t, C_ROPE), lambda h, p, qi, kj: (kj[p], 0))],
            out_specs=[wide("q", 0), wide("q", 0)],
            scratch_shapes=[pltpu.VMEM((t, LANES), F32)] * (3 * G)),
        out_shape=[_sds((S, H * C_V), BF16), _sds((S, H * LANES), F32)],
        compiler_params=_params("parallel", "arbitrary"),
    )(qi, kj, qn, qr, kv, kv, kr)


def _mla_delta(do, o, *, name):
    S, W = do.shape
    tm = min(ROW_T, S)

    def body(do_ref, o_ref, d_ref):
        for h in range(C_HEADS):
            sl = slice(h * C_V, (h + 1) * C_V)
            d = jnp.sum(do_ref[:, sl].astype(F32) * o_ref[:, sl].astype(F32), axis=-1, keepdims=True)
            d_ref[:, sl] = jnp.broadcast_to(d, (tm, C_V))

    row = pl.BlockSpec((tm, W), lambda i: (i, 0))
    return pl.pallas_call(body, name=name, grid=(S // tm,), in_specs=[row, row], out_specs=row,
                          out_shape=_sds((S, W), F32), compiler_params=_params("parallel"))(do, o)


def _mla_flash_dq(qn, qr, kv, kr, do, lse, delta, *, name):
    S = qn.shape[0]
    H, G, t = C_HEADS, MLA_GROUP, min(FLASH_T, S)
    qi, kj = _causal_pairs(S // t, True)

    def body(qi_ref, kj_ref, qn_ref, qr_ref, kn_ref, v_ref, kr_ref, do_ref, lse_ref, dl_ref, dqn_ref, dqr_ref, an, ar):
        p_id = pl.program_id(1)
        i, j = qi_ref[p_id], kj_ref[p_id]

        @pl.when(j == 0)
        def _():
            an[...] = jnp.zeros_like(an)
            ar[...] = jnp.zeros_like(ar)

        def step(diagonal):
            for hh in range(G):
                sl = slice(hh * LANES, (hh + 1) * LANES)
                s = _mla_scores2(qn_ref[:, sl], qr_ref[hh], kn_ref[:, sl], kr_ref[...], diagonal)
                p = jnp.exp2(s - lse_ref[:, hh * LANES:hh * LANES + 1])
                dp = lax.dot_general(do_ref[:, sl], v_ref[:, sl], C11, preferred_element_type=F32)
                ds = (p * (dp - dl_ref[:, hh * LANES:hh * LANES + 1])).astype(BF16)
                an[:, sl] += jnp.dot(ds, kn_ref[:, sl], preferred_element_type=F32)
                ar[hh] += jnp.dot(ds, kr_ref[...], preferred_element_type=F32)

        @pl.when(j < i)
        def _():
            step(False)

        @pl.when(j == i)
        def _():
            step(True)
            dqn_ref[...] = (an[...] * MLA_SCALE).astype(BF16)
            dqr_ref[...] = ar[...] * MLA_SCALE

    wide = lambda which, off: pl.BlockSpec((t, G * LANES), lambda h, p, qi, kj: ((qi if which == "q" else kj)[p], off + h))
    qrb = pl.BlockSpec((G, t, C_ROPE), lambda h, p, qi, kj: (h, qi[p], 0))
    return pl.pallas_call(
        body, name=name,
        grid_spec=pltpu.PrefetchScalarGridSpec(
            num_scalar_prefetch=2, grid=(H // G, qi.shape[0]),
            in_specs=[wide("q", 0), qrb, wide("k", 0), wide("k", H // G),
                      pl.BlockSpec((t, C_ROPE), lambda h, p, qi, kj: (kj[p], 0)), wide("q", 0), wide("q", 0), wide("q", 0)],
            out_specs=[wide("q", 0), qrb],
            scratch_shapes=[pltpu.VMEM((t, G * LANES), F32), pltpu.VMEM((G, t, C_ROPE), F32)]),
        out_shape=[_sds((S, H * C_NOPE), BF16), _sds((H, S, C_ROPE), F32)],
        compiler_params=_params("parallel", "arbitrary"),
    )(qi, kj, qn, qr, kv, kv, kr, do, lse, delta)


def _mla_flash_dkv(qn, qr, kv, kr, do, lse, delta, *, name):
    S = qn.shape[0]
    H, G, t = C_HEADS, MLA_GROUP, min(FLASH_T, S)
    n = S // t
    qi, kj = _causal_pairs(n, False)

    def body(qi_ref, kj_ref, qn_ref, qr_ref, kn_ref, v_ref, kr_ref, do_ref, lse_ref, dl_ref, dkn_ref, dv_ref, dkr_ref, akn, av, akr):
        p_id = pl.program_id(1)
        i, j = qi_ref[p_id], kj_ref[p_id]

        def step(diagonal):
            for hh in range(G):
                sl = slice(hh * LANES, (hh + 1) * LANES)
                s = _mla_scores2(qn_ref[:, sl], qr_ref[hh], kn_ref[:, sl], kr_ref[...], diagonal)
                p = jnp.exp2(s - lse_ref[:, hh * LANES:hh * LANES + 1])
                dp = lax.dot_general(do_ref[:, sl], v_ref[:, sl], C11, preferred_element_type=F32)
                ds = (p * (dp - dl_ref[:, hh * LANES:hh * LANES + 1])).astype(BF16)
                av[:, sl] += lax.dot_general(p.astype(BF16), do_ref[:, sl], C00, preferred_element_type=F32)
                akn[:, sl] += lax.dot_general(ds, qn_ref[:, sl], C00, preferred_element_type=F32)
                akr[hh] += lax.dot_general(ds, qr_ref[hh], C00, preferred_element_type=F32)

        @pl.when(i == j)
        def _():
            akn[...] = jnp.zeros_like(akn)
            av[...] = jnp.zeros_like(av)
            akr[...] = jnp.zeros_like(akr)
            step(True)

        @pl.when(i > j)
        def _():
            step(False)

        @pl.when(i == n - 1)
        def _():
            dkn_ref[...] = (akn[...] * MLA_SCALE).astype(BF16)
            dv_ref[...] = av[...].astype(BF16)
            dkr_ref[...] = akr[...] * MLA_SCALE

    wide = lambda which, off: pl.BlockSpec((t, G * LANES), lambda h, p, qi, kj: ((qi if which == "q" else kj)[p], off + h))
    krb = pl.BlockSpec((G, t, C_ROPE), lambda h, p, qi, kj: (h, kj[p], 0))
    return pl.pallas_call(
        body, name=name,
        grid_spec=pltpu.PrefetchScalarGridSpec(
            num_scalar_prefetch=2, grid=(H // G, qi.shape[0]),
            in_specs=[wide("q", 0), pl.BlockSpec((G, t, C_ROPE), lambda h, p, qi, kj: (h, qi[p], 0)), wide("k", 0),
                      wide("k", H // G), pl.BlockSpec((t, C_ROPE), lambda h, p, qi, kj: (kj[p], 0)),
                      wide("q", 0), wide("q", 0), wide("q", 0)],
            out_specs=[wide("k", 0), wide("k", 0), krb],
            scratch_shapes=[pltpu.VMEM((t, G * LANES), F32), pltpu.VMEM((t, G * LANES), F32), pltpu.VMEM((G, t, C_ROPE), F32)]),
        out_shape=[_sds((S, H * C_NOPE), BF16), _sds((S, H * C_V), BF16), _sds((H, S, C_ROPE), F32)],
        compiler_params=_params("parallel", "arbitrary"),
    )(qi, kj, qn, qr, kv, kv, kr, do, lse, delta)


def _place():
    return lax.axis_index("x"), lax.axis_index("y"), lax.axis_index("c")


def _other_chips(x, y):
    return [(1 - x, y), (x, 1 - y), (1 - x, 1 - y)]


def _all_gather_chips(p, *, name):
    R, C = p.shape

    def body(p_ref, o_ref, send_sems, recv_sems, local_sem):
        x, y, c = _place()
        me = 2 * x + y
        local = pltpu.make_async_copy(p_ref, o_ref.at[me], local_sem)
        local.start()
        copies = [pltpu.make_async_remote_copy(src_ref=p_ref, dst_ref=o_ref.at[me], send_sem=send_sems.at[k],
                                               recv_sem=recv_sems.at[k], device_id=(px, py, c), device_id_type=MESH)
                  for k, (px, py) in enumerate(_other_chips(x, y))]
        for cp in copies:
            cp.start()
        for cp in copies:
            cp.wait()
        local.wait()

    any_spec = pl.BlockSpec(memory_space=pl.ANY)
    return pl.pallas_call(
        body, name=name, in_specs=[any_spec], out_specs=any_spec, out_shape=_sds((4, R, C), p.dtype),
        scratch_shapes=[pltpu.SemaphoreType.DMA((3,)), pltpu.SemaphoreType.DMA((3,)), pltpu.SemaphoreType.DMA(())],
    )(p)


def _shard_of(ref, axis, pos, size):
    idx = [slice(None)] * len(ref.shape)
    idx[axis] = pl.ds(pos * size, size)
    return ref.at[tuple(idx)]


def _shard_pos(chip, swapped):
    return (chip % 2) * 2 + chip // 2 if swapped else chip


class _Comm:
    def __init__(self, inputs, out_shapes, sems, start, finish, deliver):
        self.inputs, self.out_shapes, self.sems = list(inputs), list(out_shapes), list(sems)
        self.start, self.finish, self.deliver = start, finish, deliver


def _call(body, *, name, grid, in_specs, out_specs, out_shape, scratch_shapes=(), params, comm=None):
    in_specs, out_specs, out_shape, scratch_shapes = list(in_specs), list(out_specs), list(out_shape), list(scratch_shapes)
    if comm is None:
        return pl.pallas_call(body, name=name, grid=grid, in_specs=in_specs, out_specs=out_specs, out_shape=out_shape,
                              scratch_shapes=scratch_shapes, compiler_params=params)
    n_in, n_out, n_scr = len(in_specs), len(out_specs), len(scratch_shapes)
    c_in, c_out = len(comm.inputs), len(comm.out_shapes)

    def hosted(*refs):
        a, rest = refs[:n_in], refs[n_in:]
        cin, rest = rest[:c_in], rest[c_in:]
        o, rest = rest[:n_out], rest[n_out:]
        cout, rest = rest[:c_out], rest[c_out:]
        scr, sems = rest[:n_scr], rest[n_scr:]
        first = functools.reduce(jnp.logical_and, [pl.program_id(d) == 0 for d in range(len(grid))])
        last = functools.reduce(jnp.logical_and, [pl.program_id(d) == grid[d] - 1 for d in range(len(grid))])

        @pl.when(first)
        def _():
            comm.start(cin, cout, sems)

        body(*a, *o, *scr)

        @pl.when(last)
        def _():
            comm.finish(cin, cout, sems)

    any_spec = pl.BlockSpec(memory_space=pl.ANY)
    call = pl.pallas_call(
        hosted, name=name, grid=grid, in_specs=in_specs + [any_spec] * c_in, out_specs=out_specs + [any_spec] * c_out,
        out_shape=out_shape + comm.out_shapes, scratch_shapes=scratch_shapes + comm.sems, compiler_params=params)

    def run(*args):
        outs = call(*args, *comm.inputs)
        comm.deliver(outs[n_out:])
        return outs[:n_out]

    return run


def _run_comm(comm, *, name):
    c_in, c_out = len(comm.inputs), len(comm.out_shapes)

    def body(*refs):
        cin, cout, sems = refs[:c_in], refs[c_in:c_in + c_out], refs[c_in + c_out:]
        comm.start(cin, cout, sems)
        comm.finish(cin, cout, sems)

    any_spec = pl.BlockSpec(memory_space=pl.ANY)
    outs = pl.pallas_call(body, name=name, in_specs=[any_spec] * c_in, out_specs=[any_spec] * c_out,
                          out_shape=comm.out_shapes, scratch_shapes=comm.sems)(*comm.inputs)
    comm.deliver(outs)


def _gather_comm(items, deliver):
    n = len(items)
    shard_shapes = [a.shape if j is None else a.shape[1:] for a, j, _, _ in items]
    axes = [ax for _, _, ax, _ in items]
    swapped = [sw for _, _, _, sw in items]
    sizes = [s[a] for s, a in zip(shard_shapes, axes)]
    halves = [s[-2] // 2 for s in shard_shapes]
    full = [tuple(4 * d if i == a else d for i, d in enumerate(s)) for s, a in zip(shard_shapes, axes)]

    def mine(ins, k):
        j = items[k][1]
        return ins[k] if j is None else ins[k].at[j]

    def half_of(ref, k, half, chip=None):
        nd = len(ref.shape)
        split = nd - 2
        idx = [slice(None)] * nd
        start = half * halves[k]
        if chip is not None:
            pos = _shard_pos(chip, swapped[k]) * sizes[k]
            if axes[k] == split:
                start = start + pos
            else:
                idx[axes[k]] = pl.ds(pos, sizes[k])
        idx[split] = pl.ds(start, halves[k])
        return ref.at[tuple(idx)]

    def local_copy(ins, outs, sems, k, me):
        return pltpu.make_async_copy(mine(ins, k), _shard_of(outs[k], axes[k], _shard_pos(me, swapped[k]), sizes[k]), sems[4].at[k])

    def ici_copy(ins, outs, sems, k, j, peer, c, landing_chip):
        return pltpu.make_async_remote_copy(
            src_ref=half_of(mine(ins, k), k, c), dst_ref=half_of(outs[k], k, c, chip=landing_chip), send_sem=sems[0].at[3 * k + j],
            recv_sem=sems[1].at[3 * k + j], device_id=(peer[0], peer[1], c), device_id_type=MESH)

    def pass_copy(outs, sems, k, j, half, chip, sibling):
        region = half_of(outs[k], k, half, chip=chip)
        return pltpu.make_async_remote_copy(src_ref=region, dst_ref=region, send_sem=sems[2].at[3 * k + j],
                                            recv_sem=sems[3].at[3 * k + j], device_id=sibling, device_id_type=MESH)

    def start(ins, outs, sems):
        x, y, c = _place()
        me = 2 * x + y
        for k in range(n):
            local_copy(ins, outs, sems, k, me).start()
            for j, peer in enumerate(_other_chips(x, y)):
                ici_copy(ins, outs, sems, k, j, peer, c, me).start()

    def finish(ins, outs, sems):
        x, y, c = _place()
        me = 2 * x + y
        chips = _other_chips(x, y)
        sibling = (x, y, 1 - c)
        for k in range(n):
            for j, peer in enumerate(chips):
                ici_copy(ins, outs, sems, k, j, peer, c, 2 * peer[0] + peer[1]).wait_recv()
                pass_copy(outs, sems, k, j, c, 2 * peer[0] + peer[1], sibling).start()
        for k in range(n):
            for j, peer in enumerate(chips):
                pass_copy(outs, sems, k, j, 1 - c, 2 * peer[0] + peer[1], sibling).wait_recv()
        for k in range(n):
            local_copy(ins, outs, sems, k, me).wait()
            for j, peer in enumerate(chips):
                ici_copy(ins, outs, sems, k, j, peer, c, me).wait_send()
                pass_copy(outs, sems, k, j, c, 2 * peer[0] + peer[1], sibling).wait_send()

    return _Comm([a for a, _, _, _ in items], [_sds(f, a.dtype) for f, (a, _, _, _) in zip(full, items)],
                 [pltpu.SemaphoreType.DMA((3 * n,))] * 4 + [pltpu.SemaphoreType.DMA((n,))], start, finish, deliver)


def _scatter_comm(items, deliver):
    n = len(items)
    axes = [ax for _, ax, _ in items]
    swapped = [sw for _, _, sw in items]
    sizes = [g.shape[a] // 4 for g, a, _ in items]
    shard = [tuple(d // 4 if i == a else d for i, d in enumerate(g.shape)) for g, a, _ in items]

    def copies(ins, outs, sems):
        x, y, c = _place()
        me = 2 * x + y
        out = []
        for k in range(n):
            own = _shard_of(ins[k], axes[k], _shard_pos(me, swapped[k]), sizes[k])
            out.append(pltpu.make_async_copy(own, outs[k].at[3], sems[2].at[k]))
            for j, (px, py) in enumerate(_other_chips(x, y)):
                src = _shard_of(ins[k], axes[k], _shard_pos(2 * px + py, swapped[k]), sizes[k])
                out.append(pltpu.make_async_remote_copy(src_ref=src, dst_ref=outs[k].at[j], send_sem=sems[0].at[3 * k + j],
                                                        recv_sem=sems[1].at[3 * k + j], device_id=(px, py, c), device_id_type=MESH))
        return out

    def start(ins, outs, sems):
        for cp in copies(ins, outs, sems):
            cp.start()

    def finish(ins, outs, sems):
        for cp in copies(ins, outs, sems):
            cp.wait()

    return _Comm([g for g, _, _ in items], [_sds((4,) + s, g.dtype) for s, (g, _, _) in zip(shard, items)],
                 [pltpu.SemaphoreType.DMA((3 * n,)), pltpu.SemaphoreType.DMA((3 * n,)), pltpu.SemaphoreType.DMA((n,))],
                 start, finish, deliver)


def _row_tile(rows, cols, budget=2 ** 20):
    best = None
    for t in range(8, rows + 1, 8):
        if rows % t == 0 and t * cols * 4 <= budget:
            best = t
    return best or rows


def _sum_partials(recv, into, layer, layers, *, name):
    _, R, C = recv.shape
    tr = _row_tile(R, C)
    nt = R // tr

    def body(own_ref, r0_ref, r1_ref, r2_ref, *rest):
        f = lambda ref: ref[...].astype(F32)
        rest[-1][...] = ((f(own_ref) + f(r0_ref)) + f(r1_ref)) + f(r2_ref)

    rspec = lambda k: pl.BlockSpec((None, tr, C), lambda i: (k, i, 0))
    extra = [] if into is None else [pl.BlockSpec(memory_space=pl.ANY)]
    return pl.pallas_call(
        body, name=name, grid=(nt,), in_specs=[rspec(3), rspec(0), rspec(1), rspec(2)] + extra,
        out_specs=pl.BlockSpec((tr, C), lambda i: (layer * nt + i, 0)), out_shape=_sds((layers * R, C), F32),
        input_output_aliases={} if into is None else {4: 0}, compiler_params=_params("parallel"),
    )(recv, recv, recv, recv, *([] if into is None else [into]))


def _swap_cores(parts, *, name):
    n = len(parts)

    def body(*refs):
        ins, outs = refs[:n], refs[n:2 * n]
        send_sems, recv_sems = refs[2 * n:]
        x, y, c = _place()
        copies = [pltpu.make_async_remote_copy(src_ref=ins[k], dst_ref=outs[k], send_sem=send_sems.at[k], recv_sem=recv_sems.at[k],
                                               device_id=(x, y, 1 - c), device_id_type=MESH) for k in range(n)]
        for cp in copies:
            cp.start()
        for cp in copies:
            cp.wait()

    any_spec = pl.BlockSpec(memory_space=pl.ANY)
    return pl.pallas_call(
        body, name=name, in_specs=[any_spec] * n, out_specs=[any_spec] * n, out_shape=[_sds(p.shape, p.dtype) for p in parts],
        scratch_shapes=[pltpu.SemaphoreType.DMA((n,)), pltpu.SemaphoreType.DMA((n,))],
    )(*parts)


def _all_reduce_small(v, *, name):
    r, C = v.shape

    def body(v_ref, o_ref, buf, send_sems, recv_sems):
        x, y, c = _place()
        me = 4 * x + 2 * y + c
        buf[me] = v_ref[...]
        peers = []
        for k in range(1, 8):
            kx, ky, kc = (k >> 2) & 1, (k >> 1) & 1, k & 1
            px = 1 - x if kx else x
            py = 1 - y if ky else y
            pc = 1 - c if kc else c
            peers.append((px, py, pc))
        copies = []
        for k, peer in enumerate(peers):
            cp = pltpu.make_async_remote_copy(src_ref=v_ref, dst_ref=buf.at[me], send_sem=send_sems.at[k],
                                              recv_sem=recv_sems.at[me], device_id=peer, device_id_type=MESH)
            cp.start()
            copies.append(cp)
        for k, (px, py, pc) in enumerate(peers):
            src = 4 * px + 2 * py + pc
            pltpu.make_async_remote_copy(src_ref=v_ref, dst_ref=buf.at[src], send_sem=send_sems.at[k],
                                         recv_sem=recv_sems.at[src], device_id=peers[k], device_id_type=MESH).wait_recv()
        for cp in copies:
            cp.wait_send()
        acc = buf[0]
        for d in range(1, 8):
            acc = acc + buf[d]
        o_ref[...] = acc

    vm = pl.BlockSpec(memory_space=pltpu.VMEM)
    return pl.pallas_call(
        body, name=name, in_specs=[vm], out_specs=vm, out_shape=_sds((r, C), F32),
        scratch_shapes=[pltpu.VMEM((8, r, C), F32), pltpu.SemaphoreType.DMA((7,)), pltpu.SemaphoreType.DMA((8,))],
    )(v)


def _adamw(w, m, v, ga, gb, *, name):
    R, C = w.shape
    tr = _row_tile(R, C)
    has_b = gb is not None
    c1 = 1.0 / (1.0 - ADAM_B1 ** ADAM_STEP)
    c2 = 1.0 / (1.0 - ADAM_B2 ** ADAM_STEP)

    def body(*refs):
        if has_b:
            w_ref, m_ref, v_ref, ga_ref, gb_ref, g_ref, d_ref, nm_ref, nv_ref = refs
            g = ga_ref[...] + gb_ref[...]
        else:
            w_ref, m_ref, v_ref, ga_ref, g_ref, d_ref, nm_ref, nv_ref = refs
            g = ga_ref[...]
        nm = ADAM_B1 * m_ref[...] + (1.0 - ADAM_B1) * g
        nv = ADAM_B2 * v_ref[...] + (1.0 - ADAM_B2) * (g * g)
        g_ref[...] = g
        nm_ref[...] = nm
        nv_ref[...] = nv
        d_ref[...] = -ADAM_LR * ((nm * c1) / (jnp.sqrt(nv * c2) + ADAM_EPS) + ADAM_WD * w_ref[...])

    blk = pl.BlockSpec((tr, C), lambda i: (i, 0))
    n_in = 5 if has_b else 4
    args = (w, m, v, ga) + ((gb,) if has_b else ())
    return pl.pallas_call(body, name=name, grid=(R // tr,), in_specs=[blk] * n_in, out_specs=[blk] * 4,
                          out_shape=[_sds((R, C), F32)] * 4, compiler_params=_params("parallel"))(*args)


def _seg_rows(n, cols):
    return -(-n // (16 * cols)) * 16


def _pack(arrays, dtype, cols=PACK_COLS, row_mult=512):
    parts, rows = [], 0
    for a in arrays:
        n = int(np.prod(a.shape))
        r = _seg_rows(n, cols)
        flat = a.reshape(-1).astype(dtype)
        if r * cols != n:
            flat = jnp.pad(flat, (0, r * cols - n))
        parts.append(flat.reshape(r, cols))
        rows += r
    pad = -rows % row_mult
    if pad:
        parts.append(jnp.zeros((pad, cols), dtype))
    return jnp.concatenate(parts, axis=0)


def _unpack(packed, shapes, cols=PACK_COLS):
    out, r0 = [], 0
    for shp in shapes:
        n = int(np.prod(shp))
        used = -(-n // cols)
        out.append(packed[r0:r0 + used].reshape(-1)[:n].reshape(shp))
        r0 += _seg_rows(n, cols)
    return out


def _rope_tables(seq):
    inv = 1.0 / (ROPE_THETA ** (jnp.arange(0, 64, 2, dtype=F32) / 64))
    ang = jnp.arange(seq, dtype=F32)[:, None] * inv[None, :]
    cos, sin = jnp.cos(ang), jnp.sin(ang)
    cos128 = jnp.concatenate([cos, cos, cos, cos], axis=1)
    sin128 = jnp.concatenate([-sin, sin, -sin, sin], axis=1)
    return cos128, sin128


def _ffn_perm(a):
    lead = a.shape[:-1]
    nj = D_FF // FFN_TC
    return jnp.swapaxes(a.reshape(lead + (2, nj, FFN_TC)), -3, -2).reshape(lead + (2 * D_FF,))


def _ffn_unperm(a):
    lead = a.shape[:-1]
    nj = D_FF // FFN_TC
    return jnp.swapaxes(a.reshape(lead + (nj, 2, FFN_TC)), -3, -2).reshape(lead + (2 * D_FF,))


def _mixer_a_fwd(xb, w, j, cos, sin, tag, gathers):
    qkv = _mm(xb, w["a_w_qkv"][j], name=f"mm_qkv_{tag}")
    qkv_r = _rope_cols(qkv, cos, sin, (A_HEADS + A_KV_HEADS) * A_HEAD_DIM, name=f"rope_qkv_{tag}",
                       comm=gathers.get(f"rope_qkv_{tag}"))
    o = _swa_fwd(qkv_r, w["a_sinks"][j], name=f"swa_fwd_{tag}", comm=gathers.get(f"swa_fwd_{tag}"))
    y = _mm(o, w["a_w_o"][j], name=f"mm_ao_{tag}")
    return y, (xb, qkv_r, o)


def _mixer_a_bwd(dzb, res, w, j, cos, sin, tag, grads, make_comm):
    xb, qkv_r, o = res
    do = _mm(dzb, w["a_w_o"][j], tb=True, out_dtype=BF16, name=f"mm_dao_{tag}")
    grads["a_w_o"][j] = _mm(o, dzb, ta=True, out_dtype=BF16, tk=GRAD_TK, name=f"mm_gao_{tag}")
    dq, dcur, dprev, dsink = _swa_bwd(qkv_r, w["a_sinks"][j], do, cos, sin, name=f"swa_bwd_{tag}", comm=make_comm())
    grads["a_sinks"][j] = dsink[0, :A_HEADS]
    dqkv = _swa_dqkv(dq, dcur, dprev, cos, sin, name=f"swa_dqkv_{tag}")
    grads["a_w_qkv"][j] = _mm(xb, dqkv, ta=True, out_dtype=BF16, tk=GRAD_TK, name=f"mm_gqkv_{tag}")
    return _mm(dqkv, w["a_w_qkv"][j], tb=True, name=f"mm_dxa_{tag}")


def _mixer_b_fwd(xb, w, j, tag, gathers):
    xw = _mm(xb, w["b_w_in"][j], name=f"mm_bin_{tag}")
    wri = jnp.concatenate([w["b_w_rgate"][j], w["b_w_igate"][j]], axis=-1)
    y, u, h = _lru_fwd(xw, w["b_conv_w"][j], w["b_conv_b"][j][None], wri, w["b_b_rgate"][j][None],
                       w["b_b_igate"][j][None], w["b_lambda"][j][None], name=f"lru_fwd_{tag}", comm=gathers.get(f"lru_fwd_{tag}"))
    out = _mm(y, w["b_w_o"][j], name=f"mm_bo_{tag}")
    return out, (xb, xw, wri, u, h, y)


def _mixer_b_bwd(dzb, res, w, j, tag, grads):
    xb, xw, wri, u, h, y = res
    dy = _mm(dzb, w["b_w_o"][j], tb=True, out_dtype=BF16, name=f"mm_dbo_{tag}")
    grads["b_w_o"][j] = _mm(y, dzb, ta=True, out_dtype=BF16, tk=GRAD_TK, name=f"mm_gbo_{tag}")
    dxw, dcw, dcb, dwri, dbr, dbi, dlam = _lru_bwd(
        xw, u, h, dy, w["b_conv_w"][j], wri, w["b_b_rgate"][j][None], w["b_b_igate"][j][None], w["b_lambda"][j][None],
        name=f"lru_bwd_{tag}")
    grads["b_conv_w"][j], grads["b_conv_b"][j] = dcw, dcb[0]
    grads["b_w_rgate"][j], grads["b_w_igate"][j] = dwri[..., :LRU_BLOCK_W].astype(BF16), dwri[..., LRU_BLOCK_W:].astype(BF16)
    grads["b_b_rgate"][j], grads["b_b_igate"][j], grads["b_lambda"][j] = dbr[0], dbi[0], dlam[0]
    grads["b_w_in"][j] = _mm(xb, dxw, ta=True, out_dtype=BF16, tk=GRAD_TK, name=f"mm_gbin_{tag}")
    return _mm(dxw, w["b_w_in"][j], tb=True, name=f"mm_dxb_{tag}")


def _mla_weights(w, j):
    H = C_HEADS
    uq = w["c_w_uq"][j].reshape(C_Q_RANK, H, C_NOPE + C_ROPE)
    ukv = w["c_w_ukv"][j].reshape(C_KV_RANK, H, C_NOPE + C_V)
    uq_n = uq[:, :, :C_NOPE].reshape(C_Q_RANK, H * C_NOPE)
    uq_r = uq[:, :, C_NOPE:].reshape(C_Q_RANK, H * C_ROPE)
    ukv_p = jnp.concatenate([ukv[:, :, :C_NOPE].reshape(C_KV_RANK, H * C_NOPE),
                             ukv[:, :, C_NOPE:].reshape(C_KV_RANK, H * C_V)], axis=1)
    return uq_n, uq_r, ukv_p


def _mixer_c_fwd(xb, w, j, cos, sin, tag):
    S = xb.shape[0]
    H = C_HEADS
    uq_n, uq_r, ukv_p = _mla_weights(w, j)
    c = _mm(xb, w["c_w_down"][j], name=f"mm_cdown_{tag}")
    cq, ckv, kr = _mla_pre(c, w["c_q_norm"][j][None], w["c_kv_norm"][j][None], cos, sin, name=f"mla_pre_{tag}")
    qn = _mm(cq, uq_n, out_dtype=BF16, name=f"mm_uqn_{tag}")
    qr_flat = _rope_heads(_mm(cq, uq_r, name=f"mm_uqr_{tag}"), cos, sin, transpose=False, name=f"rope_qr_{tag}")
    qr = jnp.transpose(qr_flat.reshape(S, H, C_ROPE), (1, 0, 2))
    kv = _mm(ckv, ukv_p, out_dtype=BF16, name=f"mm_ukv_{tag}")
    o, lse = _mla_flash_fwd(qn, qr, kv, kr, name=f"mla_fwd_{tag}")
    y = _mm(o, w["c_w_o"][j], name=f"mm_co_{tag}")
    return y, (xb, c, cq, ckv, kr, qn, qr, kv, o, lse, uq_n, uq_r, ukv_p)


def _mixer_c_bwd(dzb, res, w, j, cos, sin, tag, grads):
    xb, c, cq, ckv, kr, qn, qr, kv, o, lse, uq_n, uq_r, ukv_p = res
    S = xb.shape[0]
    H = C_HEADS
    do = _mm(dzb, w["c_w_o"][j], tb=True, out_dtype=BF16, name=f"mm_dco_{tag}")
    grads["c_w_o"][j] = _mm(o, dzb, ta=True, out_dtype=BF16, tk=GRAD_TK, name=f"mm_gco_{tag}")
    delta = _mla_delta(do, o, name=f"mla_delta_{tag}")
    dqn, dqr = _mla_flash_dq(qn, qr, kv, kr, do, lse, delta, name=f"mla_dq_{tag}")
    dkn, dv, dkr_h = _mla_flash_dkv(qn, qr, kv, kr, do, lse, delta, name=f"mla_dkv_{tag}")
    dkv = jnp.concatenate([dkn, dv], axis=1)
    dqr_flat = _rope_heads(jnp.transpose(dqr, (1, 0, 2)).reshape(S, H * C_ROPE), cos, sin, transpose=True, name=f"rope_dqr_{tag}")
    g_uq_n = _mm(cq, dqn, ta=True, out_dtype=BF16, tk=GRAD_TK, name=f"mm_guqn_{tag}")
    g_uq_r = _mm(cq, dqr_flat, ta=True, out_dtype=BF16, tk=GRAD_TK, name=f"mm_guqr_{tag}")
    g_ukv = _mm(ckv, dkv, ta=True, out_dtype=BF16, tk=GRAD_TK, name=f"mm_gukv_{tag}")
    grads["c_w_uq"][j] = jnp.concatenate([g_uq_n.reshape(C_Q_RANK, H, C_NOPE), g_uq_r.reshape(C_Q_RANK, H, C_ROPE)],
                                         axis=2).reshape(C_Q_RANK, H * (C_NOPE + C_ROPE))
    grads["c_w_ukv"][j] = jnp.concatenate([g_ukv[:, :H * C_NOPE].reshape(C_KV_RANK, H, C_NOPE),
                                           g_ukv[:, H * C_NOPE:].reshape(C_KV_RANK, H, C_V)],
                                          axis=2).reshape(C_KV_RANK, H * (C_NOPE + C_V))
    dcq_a = _mm(dqn, uq_n, tb=True, name=f"mm_dcqa_{tag}")
    dcq_b = _mm(dqr_flat, uq_r, tb=True, name=f"mm_dcqb_{tag}")
    dckv = _mm(dkv, ukv_p, tb=True, name=f"mm_dckv_{tag}")
    dc, dqg, dkvg = _mla_post_bwd(c, dcq_a, dcq_b, dckv, dkr_h, w["c_q_norm"][j][None], w["c_kv_norm"][j][None], cos, sin,
                                  name=f"mla_post_{tag}")
    grads["c_q_norm"][j], grads["c_kv_norm"][j] = dqg[0], dkvg[0]
    grads["c_w_down"][j] = _mm(xb, dc, ta=True, out_dtype=BF16, tk=GRAD_TK, name=f"mm_gcdown_{tag}")
    return _mm(dc, w["c_w_down"][j], tb=True, name=f"mm_dxc_{tag}")


def _layer_big(i, mixer=True, rest=True):
    kind, j = i % N_MIXERS, i // N_MIXERS
    own = [[("a_w_qkv", j), ("a_w_o", j)], [("b_w_in", j), ("b_w_rgate", j), ("b_w_igate", j), ("b_w_o", j)],
           [("c_w_down", j), ("c_w_uq", j), ("c_w_ukv", j), ("c_w_o", j)]][kind]
    return (own if mixer else []) + ([("x_w_q", i), ("x_w_o", i), ("f_w_up", i), ("f_w_down", i)] if rest else [])


def _local_step(x, mem, target, w, n_layers, gathers, scatter):
    S = x.shape[0]
    cos, sin = _rope_tables(S)
    grads = {n: [None] * n_layers[n] for n in WEIGHTS if n != "mem_w_kv"}
    mkv = _mm(mem, w["mem_w_kv"], out_dtype=BF16, tm=MEM_LEN, name="mm_memkv")

    xs, xb = x, x.astype(BF16)
    saved = []
    for i in range(DEPTH):
        kind, j = i % N_MIXERS, i // N_MIXERS
        tag = f"l{i}"
        if kind == 0:
            y, res = _mixer_a_fwd(xb, w, j, cos, sin, tag, gathers)
        elif kind == 1:
            y, res = _mixer_b_fwd(xb, w, j, tag, gathers)
        else:
            y, res = _mixer_c_fwd(xb, w, j, cos, sin, tag)
        x1, x1b, xh1, rs1 = _ln_fwd(xs, y, w["ln_g"][i, 0][None], w["ln_b"][i, 0][None], name=f"ln1_{tag}")
        q = _mm(x1b, w["x_w_q"][i], out_dtype=BF16, name=f"mm_xq_{tag}")
        o = _xattn_fwd(q, mkv, name=f"xattn_fwd_{tag}")
        y2 = _mm(o, w["x_w_o"][i], name=f"mm_xo_{tag}")
        x2, x2b, xh2, rs2 = _ln_fwd(x1, y2, w["ln_g"][i, 1][None], w["ln_b"][i, 1][None], name=f"ln2_{tag}")
        w_up = w["f_w_up"][i]
        cwp, cbp = _ffn_perm(w["f_conv_w"][i]), _ffn_perm(w["f_conv_b"][i][None])
        hh = _mm(x2b, w_up, out_dtype=BF16, tn=FFN_TC, name=f"mm_up_{tag}", comm=gathers.get(f"mm_up_{tag}"))
        a = _ffn_act_fwd(hh, cwp, cbp, name=f"ffn_act_{tag}", comm=gathers.get(f"ffn_act_{tag}"))
        y3 = _mm(a, w["f_w_down"][i], name=f"mm_down_{tag}", comm=gathers.get(f"mm_down_{tag}"))
        x3, x3b, xh3, rs3 = _ln_fwd(x2, y3, w["ln_g"][i, 2][None], w["ln_b"][i, 2][None], name=f"ln3_{tag}")
        saved.append((res, (xh1, rs1, x1b), (q, o, xh2, rs2, x2b), (w_up, cwp, cbp, hh, a, xh3, rs3)))
        xs, xb = x3, x3b

    d2, loss = _loss_fwd(xs, target, name="loss")
    d1 = None

    dmkv = None
    ln_dg = [[None] * 3 for _ in range(DEPTH)]
    ln_db = [[None] * 3 for _ in range(DEPTH)]
    for i in reversed(range(DEPTH)):
        kind, j = i % N_MIXERS, i // N_MIXERS
        tag = f"l{i}"
        res, (xh1, rs1, x1b), (q, o, xh2, rs2, x2b), (w_up, cwp, cbp, hh, a, xh3, rs3) = saved[i]
        dz3, dz3b, ln_dg[i][2], ln_db[i][2] = _ln_bwd(d1, d2, xh3, rs3, w["ln_g"][i, 2][None], name=f"ln3_bwd_{tag}")
        da = _mm(dz3b, w["f_w_down"][i], tb=True, out_dtype=BF16, name=f"mm_ddown_{tag}")
        grads["f_w_down"][i] = _mm(a, dz3b, ta=True, out_dtype=BF16, tm=FFN_TC, tk=GRAD_TK, name=f"mm_gdown_{tag}")
        later = scatter(_layer_big(i + 1), grads) if i + 1 < DEPTH else None
        dh, dcw, dcb = _ffn_act_bwd(hh, da, cwp, cbp, name=f"ffn_act_bwd_{tag}", comm=later)
        grads["f_conv_w"][i], grads["f_conv_b"][i] = _ffn_unperm(dcw), _ffn_unperm(dcb)[0]
        grads["f_w_up"][i] = _mm(x2b, dh, ta=True, out_dtype=BF16, tn=FFN_TC, tk=GRAD_TK, name=f"mm_gup_{tag}")
        dx2 = _mm(dh, w_up, tb=True, tm=512, name=f"mm_dxf_{tag}")

        dz2, dz2b, ln_dg[i][1], ln_db[i][1] = _ln_bwd(dz3, dx2, xh2, rs2, w["ln_g"][i, 1][None], name=f"ln2_bwd_{tag}")
        do = _mm(dz2b, w["x_w_o"][i], tb=True, out_dtype=BF16, name=f"mm_dxo_{tag}")
        grads["x_w_o"][i] = _mm(o, dz2b, ta=True, out_dtype=BF16, tk=GRAD_TK, name=f"mm_gxo_{tag}")
        dq, dmkv_i = _xattn_bwd(q, mkv, do, name=f"xattn_bwd_{tag}")
        dmkv = dmkv_i if dmkv is None else dmkv + dmkv_i
        if i == 0:
            g_mem = _mm(mem, dmkv, ta=True, out_dtype=BF16, tm=512, name="mm_gmemkv")
        grads["x_w_q"][i] = _mm(x1b, dq, ta=True, out_dtype=BF16, tk=GRAD_TK, name=f"mm_gxq_{tag}")
        dx1 = _mm(dq, w["x_w_q"][i], tb=True, name=f"mm_dxq_{tag}")

        dz1, dz1b, ln_dg[i][0], ln_db[i][0] = _ln_bwd(dz2, dx1, xh1, rs1, w["ln_g"][i, 0][None], name=f"ln1_bwd_{tag}")
        if kind == 0:
            if i == 0:
                grads["mem_w_kv"] = g_mem
                done = lambda: scatter(_layer_big(0, mixer=False) + [("a_w_o", 0), ("mem_w_kv", None)], grads)
            else:
                done = lambda: None
            dx0 = _mixer_a_bwd(dz1b, res, w, j, cos, sin, tag, grads, done)
        elif kind == 1:
            dx0 = _mixer_b_bwd(dz1b, res, w, j, tag, grads)
        else:
            dx0 = _mixer_c_bwd(dz1b, res, w, j, cos, sin, tag, grads)
        d1, d2 = dz1, dx0

    grad_x = _axpy(d1, d2, name="grad_x")
    big = [n for n, _ in SHARDED[:N_BIG]]
    out = {n: (g if n in big else jnp.stack(g, axis=0)) for n, g in grads.items() if n not in ("ln_g", "ln_b")}
    out["ln_g"] = jnp.stack([jnp.concatenate(r, axis=0) for r in ln_dg], axis=0)
    out["ln_b"] = jnp.stack([jnp.concatenate(r, axis=0) for r in ln_db], axis=0)
    return loss, grad_x, out


def kernel(x, mem, a_w_qkv, a_sinks, a_w_o, b_w_in, b_conv_w, b_conv_b, b_w_rgate, b_b_rgate, b_w_igate, b_b_igate, b_lambda, b_w_o, c_w_down, c_q_norm, c_kv_norm, c_w_uq, c_w_ukv, c_w_o, mem_w_kv, x_w_q, x_w_o, f_w_up, f_conv_w, f_conv_b, f_w_down, ln_g, ln_b, loss_target, m_a_w_qkv, m_a_sinks, m_a_w_o, m_b_w_in, m_b_conv_w, m_b_conv_b, m_b_w_rgate, m_b_b_rgate, m_b_w_igate, m_b_b_igate, m_b_lambda, m_b_w_o, m_c_w_down, m_c_q_norm, m_c_kv_norm, m_c_w_uq, m_c_w_ukv, m_c_w_o, m_mem_w_kv, m_x_w_q, m_x_w_o, m_f_w_up, m_f_conv_w, m_f_conv_b, m_f_w_down, m_ln_g, m_ln_b, v_a_w_qkv, v_a_sinks, v_a_w_o, v_b_w_in, v_b_conv_w, v_b_conv_b, v_b_w_rgate, v_b_b_rgate, v_b_w_igate, v_b_b_igate, v_b_lambda, v_b_w_o, v_c_w_down, v_c_q_norm, v_c_kv_norm, v_c_w_uq, v_c_w_ukv, v_c_w_o, v_mem_w_kv, v_x_w_q, v_x_w_o, v_f_w_up, v_f_conv_w, v_f_conv_b, v_f_w_down, v_ln_g, v_ln_b):
    loc = locals()
    shard = {n: loc[n] for n in WEIGHTS}
    mom = {n: loc["m_" + n] for n in WEIGHTS}
    var = {n: loc["v_" + n] for n in WEIGHTS}
    names = [n for n, _ in SHARDED]
    axis = dict(SHARDED)
    big, small = names[:N_BIG], names[N_BIG:]

    chip = 2 * lax.axis_index("x") + lax.axis_index("y")
    n_layers = {n: shard[n].shape[0] for n in WEIGHTS if n != "mem_w_kv"}
    layer_axis = lambda n, j: axis[n] - (0 if j is None else 1)

    shard_b = {n: shard[n].astype(BF16) for n in big}
    w = {n: [None] * n_layers[n] for n in big if n != "mem_w_kv"}

    small_pack = _pack([shard[n] for n in small], F32)
    shard_b["small"] = small_pack

    def gather(pairs):
        def deliver(outs):
            for (n, j), o in zip(pairs, outs):
                if j is None:
                    w[n] = o
                else:
                    w[n][j] = o
        return _gather_comm([(shard_b[n], j, 0 if n == "small" else layer_axis(n, j), n == "f_w_up") for n, j in pairs], deliver)

    _run_comm(gather([("a_w_qkv", 0), ("a_w_o", 0), ("mem_w_kv", None), ("small", None)]), name="gather_first")
    gathers = {
        "rope_qkv_l0": gather([("x_w_q", 0), ("x_w_o", 0)]),
        "swa_fwd_l0": gather([("f_w_up", 0), ("f_w_down", 0)]),
        "mm_up_l0": gather(_layer_big(1, rest=False) + [("x_w_q", 1), ("x_w_o", 1)]),
        "ffn_act_l0": gather([("f_w_up", 1)]),
        "mm_down_l0": gather([("f_w_down", 1)]),
        "lru_fwd_l1": gather(_layer_big(2, rest=False) + [("x_w_q", 2), ("x_w_o", 2)]),
        "mm_up_l1": gather([("f_w_up", 2)]),
        "ffn_act_l1": gather([("f_w_down", 2)] + _layer_big(3, rest=False)),
        "mm_up_l2": gather([("f_w_up", 3)]),
        "ffn_act_l2": gather([("f_w_down", 3), ("x_w_q", 3), ("x_w_o", 3)]),
    }
    got = w.pop("small").reshape((4,) + small_pack.shape)
    per_chip = [_unpack(got[s], [shard[n].shape for n in small]) for s in range(4)]
    for k, n in enumerate(small):
        w[n] = jnp.concatenate([per_chip[s][k] for s in range(4)], axis=axis[n])
    for n in REPLICATED:
        w[n] = shard[n]

    recv = {}

    def scatter(pairs, grads):
        def deliver(outs):
            recv.update(dict(zip(pairs, outs)))
        return _scatter_comm([(grads[n] if j is None else grads[n][j], layer_axis(n, j), n == "f_w_up") for n, j in pairs], deliver)

    loss, grad_x, g = _local_step(x[0], mem[0], loss_target[0], w, n_layers, gathers, scatter)
    _run_comm(scatter([("a_w_qkv", 0)], g), name="scatter_last")

    view = {n: (int(np.prod(shard[n].shape[:-1])), shard[n].shape[-1]) for n in big}
    parts = []
    for n in big:
        layers = n_layers.get(n, 1)
        part = None
        for j in range(layers):
            r = recv[(n, j if n in n_layers else None)]
            part = _sum_partials(r.reshape(4, view[n][0] // layers, view[n][1]), part, j, layers, name=f"sum_{n}_{j}")
        parts.append(part)
    sibs = _swap_cores(parts, name="swap_cores")
    grad_o, delta_o, m_o, v_o = {}, {}, {}, {}
    for n, part, sib in zip(big, parts, sibs):
        res = _adamw(shard[n].reshape(view[n]), mom[n].reshape(view[n]), var[n].reshape(view[n]), part, sib, name=f"adamw_{n}")
        for d, r in zip((grad_o, delta_o, m_o, v_o), res):
            d[n] = r.reshape(shard[n].shape)

    rest = small + REPLICATED
    vec = _pack([g[n] for n in rest] + [loss], F32, cols=LANES, row_mult=8)
    tot = _unpack(_all_reduce_small(vec, name="allreduce_small"), [g[n].shape for n in rest] + [(1, 1)], cols=LANES)
    loss_tot = tot[-1].reshape(())
    mine = {n: t for n, t in zip(rest, tot)}
    for n in small:
        size = shard[n].shape[axis[n]]
        mine[n] = lax.dynamic_slice_in_dim(mine[n], chip * size, size, axis=axis[n])
    rpack = lambda d: _pack([d[n] for n in rest], F32, cols=LANES, row_mult=8)
    res = _adamw(rpack(shard), rpack(mom), rpack(var), rpack(mine), None, name="adamw_small")
    for d, r in zip((grad_o, delta_o, m_o, v_o), res):
        d.update(dict(zip(rest, _unpack(r, [shard[n].shape for n in rest], cols=LANES))))

    return (loss_tot, grad_x[None], *[grad_o[n] for n in WEIGHTS], *[delta_o[n] for n in WEIGHTS],
            *[m_o[n] for n in WEIGHTS], *[v_o[n] for n in WEIGHTS])
```

```python
import functools
import math

import numpy as np
import jax
import jax.numpy as jnp
from jax import lax
from jax.experimental import pallas as pl
from jax.experimental.pallas import tpu as pltpu

F32 = jnp.float32
BF16 = jnp.bfloat16
MESH = pl.DeviceIdType.MESH

D_MODEL = 1024
DEPTH = 4
N_MIXERS = 3
MEM_LEN = 256
BLOCK = 128
ROPE_THETA = 10000.0
NEG = -1e30
LN_EPS = 1e-5
RMS_EPS = 1e-6
A_HEADS, A_KV_HEADS, A_HEAD_DIM = 16, 4, 64
LRU_BLOCKS, LRU_BLOCK_W, LRU_CONV, LRU_C = 4, 256, 4, 8.0
C_HEADS, C_NOPE, C_ROPE, C_V, C_Q_RANK, C_KV_RANK = 8, 128, 64, 128, 384, 256
X_HEADS, X_HEAD_DIM = 4, 256
D_FF, FFN_CONV = 2816, 3
ALPHA = (2.0 * DEPTH) ** 0.25
ADAM_LR, ADAM_B1, ADAM_B2, ADAM_EPS, ADAM_WD, ADAM_STEP = 0.001, 0.9, 0.999, 1e-08, 0.01, 10

VMEM_LIMIT = 56 * 2 ** 20
LANES = 128
PACK_COLS = 1024
ROW_T = 512
ACT_T = 256
LRU_T = 256
FLASH_T = 512
FFN_TC = 1408
MM_T = 1024
GRAD_TK = 1024

C11 = (((1,), (1,)), ((), ()))
C00 = (((0,), (0,)), ((), ()))

SHARDED = [
    ("a_w_qkv", 2), ("a_w_o", 1), ("b_w_in", 2), ("b_w_rgate", 2), ("b_w_igate", 2), ("b_w_o", 1), ("c_w_down", 1),
    ("c_w_uq", 2), ("c_w_ukv", 2), ("c_w_o", 1), ("mem_w_kv", 1), ("x_w_q", 1), ("x_w_o", 1), ("f_w_up", 2),
    ("f_w_down", 1),
    ("b_conv_w", 2), ("c_q_norm", 1), ("c_kv_norm", 1), ("f_conv_w", 2), ("ln_g", 2), ("ln_b", 2),
]
N_BIG = 15
REPLICATED = ["a_sinks", "b_conv_b", "b_b_rgate", "b_b_igate", "b_lambda", "f_conv_b"]
WEIGHTS = ["a_w_qkv", "a_sinks", "a_w_o", "b_w_in", "b_conv_w", "b_conv_b", "b_w_rgate", "b_b_rgate", "b_w_igate",
           "b_b_igate", "b_lambda", "b_w_o", "c_w_down", "c_q_norm", "c_kv_norm", "c_w_uq", "c_w_ukv", "c_w_o",
           "mem_w_kv", "x_w_q", "x_w_o", "f_w_up", "f_conv_w", "f_conv_b", "f_w_down", "ln_g", "ln_b"]


def _params(*sem):
    return pltpu.CompilerParams(dimension_semantics=sem, vmem_limit_bytes=VMEM_LIMIT)


def _sds(shape, dtype):
    return jax.ShapeDtypeStruct(tuple(shape), dtype)


def _mm(a, b, *, name, ta=False, tb=False, out_dtype=F32, tm=None, tn=None, tk=None, comm=None):
    (K, M) = a.shape if ta else a.shape[::-1]
    (N, K2) = b.shape if tb else b.shape[::-1]
    assert K == K2, (a.shape, b.shape, ta, tb)
    tm = min(tm or MM_T, M)
    tn = min(tn or N, N)
    tk = min(tk or K, K)
    assert M % tm == 0 and N % tn == 0 and K % tk == 0, (M, N, K, tm, tn, tk)
    nk = K // tk
    use_acc = nk > 1 and out_dtype != F32
    dims = (((0 if ta else 1,), (1 if tb else 0,)), ((), ()))

    def body(a_ref, b_ref, o_ref, *scratch):
        p = lax.dot_general(a_ref[...].astype(BF16), b_ref[...].astype(BF16), dims, preferred_element_type=F32)
        if nk == 1:
            o_ref[...] = p.astype(out_dtype)
        else:
            acc = scratch[0] if use_acc else o_ref
            k = pl.program_id(2)

            @pl.when(k == 0)
            def _():
                acc[...] = p

            @pl.when(k > 0)
            def _():
                acc[...] += p

            if use_acc:
                @pl.when(k == nk - 1)
                def _():
                    o_ref[...] = acc[...].astype(out_dtype)

    a_spec = pl.BlockSpec((tk, tm), lambda i, j, k: (k, i)) if ta else pl.BlockSpec((tm, tk), lambda i, j, k: (i, k))
    b_spec = pl.BlockSpec((tn, tk), lambda i, j, k: (j, k)) if tb else pl.BlockSpec((tk, tn), lambda i, j, k: (k, j))
    return _call(
        body, name=name, grid=(M // tm, N // tn, nk), in_specs=[a_spec, b_spec],
        out_specs=[pl.BlockSpec((tm, tn), lambda i, j, k: (i, j))], out_shape=[_sds((M, N), out_dtype)],
        scratch_shapes=[pltpu.VMEM((tm, tn), F32)] if use_acc else [],
        params=_params("parallel", "parallel", "arbitrary"), comm=comm,
    )(a, b)[0]


def _shift_down(cur, prev8, d):
    rolled = pltpu.roll(cur, d, 0)
    rid = lax.broadcasted_iota(jnp.int32, prev8.shape, 0)
    head = jnp.where(rid < d, pltpu.roll(prev8, d, 0), rolled[0:8])
    return jnp.concatenate([head, rolled[8:]], axis=0)


def _shift_up(cur, next8, d):
    n = cur.shape[0]
    rolled = pltpu.roll(cur, n - d, 0)
    rid = lax.broadcasted_iota(jnp.int32, next8.shape, 0)
    tail = jnp.where(rid >= 8 - d, pltpu.roll(next8, 8 - d, 0), rolled[n - 8:n])
    return jnp.concatenate([rolled[0:n - 8], tail], axis=0)


def _swap_halves(x):
    w = x.shape[-1]
    if w == 64:
        return jnp.concatenate([x[:, 32:64], x[:, 0:32]], axis=1)
    lane = lax.broadcasted_iota(jnp.int32, x.shape, 1)
    return jnp.where((lane % 64) < 32, pltpu.roll(x, w - 32, 1), pltpu.roll(x, 32, 1))


def _tile_lanes(t, w):
    return t if w == t.shape[-1] else jnp.concatenate([t] * (w // t.shape[-1]), axis=1)


def _rope(x, cos, sin):
    w = x.shape[-1]
    if w == 64:
        cos, sin = cos[:, :64], sin[:, :64]
    else:
        cos, sin = _tile_lanes(cos, w), _tile_lanes(sin, w)
    return x * cos + _swap_halves(x) * sin


def _rope_t(x, cos, sin):
    w = x.shape[-1]
    if w == 64:
        cos, sin = cos[:, :64], sin[:, :64]
    else:
        cos, sin = _tile_lanes(cos, w), _tile_lanes(sin, w)
    return x * cos - _swap_halves(x) * sin


def _sigmoid(x):
    return 1.0 / (1.0 + jnp.exp(-x))


def _gelu_and_grad(x):
    c0, c1 = math.sqrt(2.0 / math.pi), 0.044715
    t = jnp.tanh(c0 * (x + c1 * x * x * x))
    g = 0.5 * x * (1.0 + t)
    dg = 0.5 * (1.0 + t) + 0.5 * x * (1.0 - t * t) * c0 * (1.0 + 3.0 * c1 * x * x)
    return g, dg


def _neg_expm1(x):
    series = -x * (1.0 + x * (0.5 + x * (1.0 / 6.0 + x * (1.0 / 24.0 + x * (1.0 / 120.0)))))
    return jnp.where(x > -0.1, series, 1.0 - jnp.exp(x))


def _softplus_neg(lam):
    z = -lam
    e = jnp.exp(-jnp.abs(z))
    log1p = jnp.where(e < 0.01, e * (1.0 - e * (0.5 - e * (1.0 / 3.0))), jnp.log(1.0 + e))
    sp = jnp.maximum(z, 0.0) + log1p
    dsp = -_sigmoid(z)
    return sp, dsp


def _ln_fwd(x, y, g, b, *, name):
    S, D = x.shape
    tm = min(ROW_T, S)

    def body(x_ref, y_ref, g_ref, b_ref, o_ref, ob_ref, xh_ref, rs_ref):
        z = ALPHA * x_ref[...] + y_ref[...]
        mu = jnp.mean(z, axis=-1, keepdims=True)
        zc = z - mu
        var = jnp.mean(zc * zc, axis=-1, keepdims=True)
        r = lax.rsqrt(var + LN_EPS)
        xh = zc * r
        o = xh * g_ref[...] + b_ref[...]
        o_ref[...] = o
        ob_ref[...] = o.astype(BF16)
        xh_ref[...] = xh
        rs_ref[...] = r

    row = pl.BlockSpec((tm, D), lambda i: (i, 0))
    vec = pl.BlockSpec((1, D), lambda i: (0, 0))
    return pl.pallas_call(
        body, name=name, grid=(S // tm,), in_specs=[row, row, vec, vec],
        out_specs=[row, row, row, pl.BlockSpec((tm, 1), lambda i: (i, 0))],
        out_shape=[_sds((S, D), F32), _sds((S, D), BF16), _sds((S, D), F32), _sds((S, 1), F32)],
        compiler_params=_params("parallel"),
    )(x, y, g, b)


def _ln_bwd(d1, d2, xh, rs, g, *, name):
    S, D = xh.shape
    tm = min(ROW_T, S)
    has_d1 = d1 is not None

    def body(*refs):
        if has_d1:
            d1_ref, d2_ref, xh_ref, rs_ref, g_ref, dz_ref, dzb_ref, dg_ref, db_ref = refs
            dout = ALPHA * d1_ref[...] + d2_ref[...]
        else:
            d2_ref, xh_ref, rs_ref, g_ref, dz_ref, dzb_ref, dg_ref, db_ref = refs
            dout = d2_ref[...]
        xh_v = xh_ref[...]
        dxh = dout * g_ref[...]
        m1 = jnp.mean(dxh, axis=-1, keepdims=True)
        m2 = jnp.mean(dxh * xh_v, axis=-1, keepdims=True)
        dz = rs_ref[...] * (dxh - m1 - xh_v * m2)
        dz_ref[...] = dz
        dzb_ref[...] = dz.astype(BF16)

        @pl.when(pl.program_id(0) == 0)
        def _():
            dg_ref[...] = jnp.zeros_like(dg_ref)
            db_ref[...] = jnp.zeros_like(db_ref)

        dg_ref[...] += jnp.sum(dout * xh_v, axis=0, keepdims=True)
        db_ref[...] += jnp.sum(dout, axis=0, keepdims=True)

    row = pl.BlockSpec((tm, D), lambda i: (i, 0))
    vec = pl.BlockSpec((1, D), lambda i: (0, 0))
    ins = ([row] if has_d1 else []) + [row, row, pl.BlockSpec((tm, 1), lambda i: (i, 0)), vec]
    args = ([d1] if has_d1 else []) + [d2, xh, rs, g]
    return pl.pallas_call(
        body, name=name, grid=(S // tm,), in_specs=ins, out_specs=[row, row, vec, vec],
        out_shape=[_sds((S, D), F32), _sds((S, D), BF16), _sds((1, D), F32), _sds((1, D), F32)],
        compiler_params=_params("arbitrary"),
    )(*args)


def _mm_ln_fwd(a, b, x, g, beta, *, name, comm=None):
    S, K = a.shape
    D = b.shape[1]
    tm = min(ROW_T, S)

    def body(a_ref, b_ref, x_ref, g_ref, beta_ref, o_ref, ob_ref, xh_ref, rs_ref):
        y = jnp.dot(a_ref[...].astype(BF16), b_ref[...].astype(BF16), preferred_element_type=F32)
        z = ALPHA * x_ref[...] + y
        mu = jnp.mean(z, axis=-1, keepdims=True)
        zc = z - mu
        var = jnp.mean(zc * zc, axis=-1, keepdims=True)
        r = lax.rsqrt(var + LN_EPS)
        xh = zc * r
        o = xh * g_ref[...] + beta_ref[...]
        o_ref[...] = o
        ob_ref[...] = o.astype(BF16)
        xh_ref[...] = xh
        rs_ref[...] = r

    row = pl.BlockSpec((tm, D), lambda i: (i, 0))
    vec = pl.BlockSpec((1, D), lambda i: (0, 0))
    return _call(
        body, name=name, grid=(S // tm,),
        in_specs=[pl.BlockSpec((tm, K), lambda i: (i, 0)), pl.BlockSpec((K, D), lambda i: (0, 0)), row, vec, vec],
        out_specs=[row, row, row, pl.BlockSpec((tm, 1), lambda i: (i, 0))],
        out_shape=[_sds((S, D), F32), _sds((S, D), BF16), _sds((S, D), F32), _sds((S, 1), F32)],
        params=_params("arbitrary"), comm=comm,
    )(a, b, x, g, beta)


def _mm_ln_bwd(d1, da, wt, xh, rs, g, *, name):
    S, D = xh.shape
    K = da.shape[1]
    tm = min(ROW_T // 2, S)

    def body(d1_ref, da_ref, wt_ref, xh_ref, rs_ref, g_ref, dz_ref, dzb_ref, dg_ref, db_ref):
        d2 = lax.dot_general(da_ref[...].astype(BF16), wt_ref[...].astype(BF16), C11, preferred_element_type=F32)
        dout = ALPHA * d1_ref[...] + d2
        xh_v = xh_ref[...]
        dxh = dout * g_ref[...]
        m1 = jnp.mean(dxh, axis=-1, keepdims=True)
        m2 = jnp.mean(dxh * xh_v, axis=-1, keepdims=True)
        dz = rs_ref[...] * (dxh - m1 - xh_v * m2)
        dz_ref[...] = dz
        dzb_ref[...] = dz.astype(BF16)

        @pl.when(pl.program_id(0) == 0)
        def _():
            dg_ref[...] = jnp.zeros_like(dg_ref)
            db_ref[...] = jnp.zeros_like(db_ref)

        dg_ref[...] += jnp.sum(dout * xh_v, axis=0, keepdims=True)
        db_ref[...] += jnp.sum(dout, axis=0, keepdims=True)

    row = pl.BlockSpec((tm, D), lambda i: (i, 0))
    vec = pl.BlockSpec((1, D), lambda i: (0, 0))
    return pl.pallas_call(
        body, name=name, grid=(S // tm,),
        in_specs=[row, pl.BlockSpec((tm, K), lambda i: (i, 0)), pl.BlockSpec((D, K), lambda i: (0, 0)), row,
                  pl.BlockSpec((tm, 1), lambda i: (i, 0)), vec],
        out_specs=[row, row, vec, vec],
        out_shape=[_sds((S, D), F32), _sds((S, D), BF16), _sds((1, D), F32), _sds((1, D), F32)],
        compiler_params=_params("arbitrary"),
    )(d1, da, wt, xh, rs, g)


def _loss_fwd(y, target, *, name):
    S, D = y.shape
    tm = min(ROW_T, S)

    def body(y_ref, t_ref, d_ref, l_ref):
        e = y_ref[...] - t_ref[...]
        d_ref[...] = e * (1.0 / D)

        @pl.when(pl.program_id(0) == 0)
        def _():
            l_ref[...] = jnp.zeros_like(l_ref)

        part = jnp.sum(e * e, axis=0, keepdims=True)
        l_ref[...] += (0.5 / D) * jnp.sum(part, axis=1, keepdims=True)

    row = pl.BlockSpec((tm, D), lambda i: (i, 0))
    return pl.pallas_call(
        body, name=name, grid=(S // tm,), in_specs=[row, row],
        out_specs=[row, pl.BlockSpec((1, 1), lambda i: (0, 0))], out_shape=[_sds((S, D), F32), _sds((1, 1), F32)],
        compiler_params=_params("arbitrary"),
    )(y, target)


def _axpy(d1, d2, *, name):
    S, D = d1.shape
    tm = min(ROW_T, S)

    def body(a_ref, b_ref, o_ref):
        o_ref[...] = ALPHA * a_ref[...] + b_ref[...]

    row = pl.BlockSpec((tm, D), lambda i: (i, 0))
    return pl.pallas_call(body, name=name, grid=(S // tm,), in_specs=[row, row], out_specs=row,
                          out_shape=_sds((S, D), F32), compiler_params=_params("parallel"))(d1, d2)


def _ffn_act_fwd(h, cw, cb, *, name, comm=None):
    S, W = h.shape
    tc = FFN_TC
    nj = W // (2 * tc)
    tm = min(ACT_T, S)

    def body(h_ref, w_ref, b_ref, a_ref, carry):
        @pl.when(pl.program_id(1) == 0)
        def _():
            carry[...] = jnp.zeros_like(carry)

        cur = h_ref[...].astype(F32)
        prev8 = carry[...]
        hc = cur * w_ref[2:3, :] + _shift_down(cur, prev8, 1) * w_ref[1:2, :] + _shift_down(cur, prev8, 2) * w_ref[0:1, :]
        hc = hc + b_ref[...]
        carry[...] = cur[tm - 8:tm]
        hg, hu = hc[:, :tc], hc[:, tc:]
        a_ref[...] = (hg * _sigmoid(hg) * hu).astype(BF16)

    return _call(
        body, name=name, grid=(nj, S // tm),
        in_specs=[pl.BlockSpec((tm, 2 * tc), lambda j, i: (i, j)), pl.BlockSpec((3, 2 * tc), lambda j, i: (0, j)),
                  pl.BlockSpec((1, 2 * tc), lambda j, i: (0, j))],
        out_specs=[pl.BlockSpec((tm, tc), lambda j, i: (i, j))], out_shape=[_sds((S, W // 2), BF16)],
        scratch_shapes=[pltpu.VMEM((8, 2 * tc), F32)],
        params=_params("parallel", "arbitrary"), comm=comm,
    )(h, cw, cb)[0]


def _ffn_act_bwd(h, da, cw, cb, *, name, comm=None):
    S, W = h.shape
    tc = FFN_TC
    nj = W // (2 * tc)
    tm = min(ACT_T, S)
    ni = S // tm

    def body(h_ref, hp_ref, da_ref, w_ref, b_ref, dh_ref, dw_ref, db_ref, carry):
        i = pl.program_id(1)
        r = ni - 1 - i

        @pl.when(i == 0)
        def _():
            carry[...] = jnp.zeros_like(carry)
            dw_ref[...] = jnp.zeros_like(dw_ref)
            db_ref[...] = jnp.zeros_like(db_ref)

        cur = h_ref[...].astype(F32)
        prev8 = jnp.where(r > 0, hp_ref[8:16, :].astype(F32), 0.0)
        sh = [cur, _shift_down(cur, prev8, 1), _shift_down(cur, prev8, 2)]
        hc = sh[0] * w_ref[2:3, :] + sh[1] * w_ref[1:2, :] + sh[2] * w_ref[0:1, :] + b_ref[...]
        hg, hu = hc[:, :tc], hc[:, tc:]
        d = da_ref[...].astype(F32)
        sg = _sigmoid(hg)
        dg = d * hu * (sg * (1.0 + hg * (1.0 - sg)))
        du = d * (hg * sg)
        dhc = jnp.concatenate([dg, du], axis=1)
        db_ref[...] += jnp.sum(dhc, axis=0, keepdims=True)
        for k in range(3):
            dw_ref[k:k + 1, :] += jnp.sum(dhc * sh[2 - k], axis=0, keepdims=True)
        next8 = carry[...]
        dh = dhc * w_ref[2:3, :] + _shift_up(dhc, next8, 1) * w_ref[1:2, :] + _shift_up(dhc, next8, 2) * w_ref[0:1, :]
        carry[...] = dhc[0:8]
        dh_ref[...] = dh.astype(BF16)

    rev = lambda j, i: (ni - 1 - i, j)
    return _call(
        body, name=name, grid=(nj, ni),
        in_specs=[pl.BlockSpec((tm, 2 * tc), rev),
                  pl.BlockSpec((16, 2 * tc), lambda j, i: (jnp.maximum((ni - 1 - i) * (tm // 16) - 1, 0), j)),
                  pl.BlockSpec((tm, tc), rev), pl.BlockSpec((3, 2 * tc), lambda j, i: (0, j)),
                  pl.BlockSpec((1, 2 * tc), lambda j, i: (0, j))],
        out_specs=[pl.BlockSpec((tm, 2 * tc), rev), pl.BlockSpec((3, 2 * tc), lambda j, i: (0, j)),
                   pl.BlockSpec((1, 2 * tc), lambda j, i: (0, j))],
        out_shape=[_sds((S, W), BF16), _sds((3, W), F32), _sds((1, W), F32)],
        scratch_shapes=[pltpu.VMEM((8, 2 * tc), F32)],
        params=_params("parallel", "arbitrary"), comm=comm,
    )(h, h, da, cw, cb)


def _xattn_probs(q, k):
    s = lax.dot_general(q, k, C11, preferred_element_type=F32) * (X_HEAD_DIM ** -0.5)
    p = jnp.exp(s - jnp.max(s, axis=-1, keepdims=True))
    return p / jnp.sum(p, axis=-1, keepdims=True)


def _xattn_fwd(q, mkv, *, name):
    S, D = q.shape
    tm = min(ROW_T, S)

    def body(q_ref, k_ref, v_ref, o_ref):
        for h in range(X_HEADS):
            sl = slice(h * X_HEAD_DIM, (h + 1) * X_HEAD_DIM)
            p = _xattn_probs(q_ref[:, sl], k_ref[:, sl])
            o_ref[:, sl] = jnp.dot(p.astype(BF16), v_ref[:, sl], preferred_element_type=F32).astype(BF16)

    return pl.pallas_call(
        body, name=name, grid=(S // tm,),
        in_specs=[pl.BlockSpec((tm, D), lambda i: (i, 0)), pl.BlockSpec((MEM_LEN, D), lambda i: (0, 0)),
                  pl.BlockSpec((MEM_LEN, D), lambda i: (0, 1))],
        out_specs=pl.BlockSpec((tm, D), lambda i: (i, 0)), out_shape=_sds((S, D), BF16),
        compiler_params=_params("parallel"),
    )(q, mkv, mkv)


def _xattn_bwd(q, mkv, do, *, name):
    S, D = q.shape
    tm = min(ROW_T, S)
    scale = X_HEAD_DIM ** -0.5

    def body(q_ref, k_ref, v_ref, do_ref, dq_ref, dkv_ref):
        @pl.when(pl.program_id(0) == 0)
        def _():
            dkv_ref[...] = jnp.zeros_like(dkv_ref)

        for h in range(X_HEADS):
            sl = slice(h * X_HEAD_DIM, (h + 1) * X_HEAD_DIM)
            sv = slice(D + h * X_HEAD_DIM, D + (h + 1) * X_HEAD_DIM)
            qh, kh, vh, doh = q_ref[:, sl], k_ref[:, sl], v_ref[:, sl], do_ref[:, sl]
            p = _xattn_probs(qh, kh)
            dp = lax.dot_general(doh, vh, C11, preferred_element_type=F32)
            ds = (p * (dp - jnp.sum(p * dp, axis=-1, keepdims=True)) * scale).astype(BF16)
            dq_ref[:, sl] = jnp.dot(ds, kh, preferred_element_type=F32).astype(BF16)
            dkv_ref[:, sl] += lax.dot_general(ds, qh, C00, preferred_element_type=F32)
            dkv_ref[:, sv] += lax.dot_general(p.astype(BF16), doh, C00, preferred_element_type=F32)

    row = pl.BlockSpec((tm, D), lambda i: (i, 0))
    return pl.pallas_call(
        body, name=name, grid=(S // tm,),
        in_specs=[row, pl.BlockSpec((MEM_LEN, D), lambda i: (0, 0)), pl.BlockSpec((MEM_LEN, D), lambda i: (0, 1)), row],
        out_specs=[row, pl.BlockSpec((MEM_LEN, 2 * D), lambda i: (0, 0))],
        out_shape=[_sds((S, D), BF16), _sds((MEM_LEN, 2 * D), F32)],
        compiler_params=_params("arbitrary"),
    )(q, mkv, mkv, do)


def _rope_cols(x, cos, sin, n_rope, *, name, comm=None):
    S, W = x.shape
    tm = min(ROW_T, S)

    def body(x_ref, c_ref, s_ref, o_ref):
        o_ref[:, :n_rope] = _rope(x_ref[:, :n_rope], c_ref[...], s_ref[...]).astype(BF16)
        if n_rope < W:
            o_ref[:, n_rope:] = x_ref[:, n_rope:].astype(BF16)

    row = pl.BlockSpec((tm, W), lambda i: (i, 0))
    tab = pl.BlockSpec((tm, LANES), lambda i: (i, 0))
    return _call(body, name=name, grid=(S // tm,), in_specs=[row, tab, tab], out_specs=[row],
                 out_shape=[_sds((S, W), BF16)], params=_params("arbitrary"), comm=comm)(x, cos, sin)[0]


def _swa_band(n, stacked):
    qi = jnp.bitwise_and(lax.broadcasted_iota(jnp.int32, (stacked * BLOCK, 2 * BLOCK), 0), BLOCK - 1)
    kj = lax.broadcasted_iota(jnp.int32, (stacked * BLOCK, 2 * BLOCK), 1)
    first = jnp.where(n > 0, 0, BLOCK)
    return ((kj < BLOCK) & (kj > qi + first)) | ((kj >= BLOCK) & (kj - BLOCK <= qi))


def _swa_sink_rows(sink_ref, heads):
    row = lax.broadcasted_iota(jnp.int32, (len(heads) * BLOCK, 1), 0)
    col = jnp.full(row.shape, sink_ref[heads[-1]], F32)
    for gi in range(len(heads) - 2, -1, -1):
        col = jnp.where(row < (gi + 1) * BLOCK, sink_ref[heads[gi]], col)
    return col


def _swa_probs(q, k, band, sink):
    s = lax.dot_general(q, k, C11, preferred_element_type=F32) * (A_HEAD_DIM ** -0.5)
    s = jnp.where(band, s, NEG)
    m = jnp.maximum(jnp.max(s, axis=-1, keepdims=True), sink)
    p = jnp.exp(s - m)
    e_sink = jnp.exp(sink - m)
    den = jnp.sum(p, axis=-1, keepdims=True) + e_sink
    return p / den, e_sink / den


def _swa_specs():
    nq, nkv = A_HEADS * A_HEAD_DIM, A_KV_HEADS * A_HEAD_DIM
    kb, vb = nq // nkv, nq // nkv + 1
    prev = lambda n: jnp.maximum(n - 1, 0)
    return [pl.BlockSpec((BLOCK, nq), lambda n: (n, 0)),
            pl.BlockSpec((BLOCK, nkv), lambda n: (n, kb)), pl.BlockSpec((BLOCK, nkv), lambda n: (prev(n), kb)),
            pl.BlockSpec((BLOCK, nkv), lambda n: (n, vb)), pl.BlockSpec((BLOCK, nkv), lambda n: (prev(n), vb)),
            pl.BlockSpec(memory_space=pltpu.SMEM)]


def _swa_fwd(qkv, sinks, *, name, comm=None):
    S = qkv.shape[0]
    hd, grp = A_HEAD_DIM, A_HEADS // A_KV_HEADS

    def body(q_ref, kc_ref, kp_ref, vc_ref, vp_ref, sink_ref, o_ref):
        band = _swa_band(pl.program_id(0), grp)
        qa, kc, kp, vc, vp = q_ref[...], kc_ref[...], kp_ref[...], vc_ref[...], vp_ref[...]
        for hk in range(A_KV_HEADS):
            ks = slice(hk * hd, (hk + 1) * hd)
            k = jnp.concatenate([kp[:, ks], kc[:, ks]], axis=0)
            v = jnp.concatenate([vp[:, ks], vc[:, ks]], axis=0)
            heads = [hk * grp + gi for gi in range(grp)]
            q = jnp.concatenate([qa[:, h * hd:(h + 1) * hd] for h in heads], axis=0)
            p, _ = _swa_probs(q, k, band, _swa_sink_rows(sink_ref, heads))
            o = jnp.dot(p.astype(BF16), v, preferred_element_type=F32).astype(BF16)
            for gi, h in enumerate(heads):
                o_ref[:, h * hd:(h + 1) * hd] = o[gi * BLOCK:(gi + 1) * BLOCK]

    return _call(
        body, name=name, grid=(S // BLOCK,), in_specs=_swa_specs(),
        out_specs=[pl.BlockSpec((BLOCK, A_HEADS * hd), lambda n: (n, 0))], out_shape=[_sds((S, A_HEADS * hd), BF16)],
        params=_params("arbitrary"), comm=comm,
    )(qkv, qkv, qkv, qkv, qkv, sinks)[0]


def _swa_bwd(qkv, sinks, do, cos, sin, *, name, comm=None):
    S = qkv.shape[0]
    hd, grp = A_HEAD_DIM, A_HEADS // A_KV_HEADS
    nq, nkv = A_HEADS * hd, A_KV_HEADS * hd
    scale = hd ** -0.5

    def body(q_ref, kc_ref, kp_ref, vc_ref, vp_ref, sink_ref, do_ref, c_ref, s_ref, dq_ref, dc_ref, dp_ref, ds_ref, dq_s):
        @pl.when(pl.program_id(0) == 0)
        def _():
            ds_ref[...] = jnp.zeros_like(ds_ref)

        band = _swa_band(pl.program_id(0), grp)
        lane = lax.broadcasted_iota(jnp.int32, (1, LANES), 1)
        qa, kc, kp, vc, vp, doa = q_ref[...], kc_ref[...], kp_ref[...], vc_ref[...], vp_ref[...], do_ref[...]
        dsink = jnp.zeros((1, LANES), F32)
        for hk in range(A_KV_HEADS):
            ks = slice(hk * hd, (hk + 1) * hd)
            k = jnp.concatenate([kp[:, ks], kc[:, ks]], axis=0)
            v = jnp.concatenate([vp[:, ks], vc[:, ks]], axis=0)
            heads = [hk * grp + gi for gi in range(grp)]
            q = jnp.concatenate([qa[:, h * hd:(h + 1) * hd] for h in heads], axis=0)
            dog = jnp.concatenate([doa[:, h * hd:(h + 1) * hd] for h in heads], axis=0)
            p, p_sink = _swa_probs(q, k, band, _swa_sink_rows(sink_ref, heads))
            dpr = lax.dot_general(dog, v, C11, preferred_element_type=F32)
            delta = jnp.sum(p * dpr, axis=-1, keepdims=True)
            dsc = (p * (dpr - delta) * scale).astype(BF16)
            dqg = jnp.dot(dsc, k, preferred_element_type=F32)
            dk = lax.dot_general(dsc, q, C00, preferred_element_type=F32)
            dv = lax.dot_general(p.astype(BF16), dog, C00, preferred_element_type=F32)
            sink_term = p_sink * delta
            for gi, h in enumerate(heads):
                rows = slice(gi * BLOCK, (gi + 1) * BLOCK)
                dq_s[:, h * hd:(h + 1) * hd] = dqg[rows]
                dsink = dsink + jnp.where(lane == h, -jnp.sum(sink_term[rows], axis=0, keepdims=True), 0.0)
            dp_ref[:, ks] = dk[:BLOCK]
            dc_ref[:, ks] = dk[BLOCK:]
            dp_ref[:, nkv + hk * hd:nkv + (hk + 1) * hd] = dv[:BLOCK]
            dc_ref[:, nkv + hk * hd:nkv + (hk + 1) * hd] = dv[BLOCK:]
        ds_ref[...] += dsink
        dq_ref[...] = _rope_t(dq_s[...], c_ref[...], s_ref[...]).astype(BF16)

    tab = pl.BlockSpec((BLOCK, LANES), lambda n: (n, 0))
    blk = lambda w: pl.BlockSpec((BLOCK, w), lambda n: (n, 0))
    return _call(
        body, name=name, grid=(S // BLOCK,), in_specs=_swa_specs() + [blk(nq), tab, tab],
        out_specs=[blk(nq), blk(2 * nkv), blk(2 * nkv), pl.BlockSpec((1, LANES), lambda n: (0, 0))],
        out_shape=[_sds((S, nq), BF16), _sds((S, 2 * nkv), F32), _sds((S, 2 * nkv), F32), _sds((1, LANES), F32)],
        scratch_shapes=[pltpu.VMEM((BLOCK, nq), F32)],
        params=_params("arbitrary"), comm=comm,
    )(qkv, qkv, qkv, qkv, qkv, sinks, do, cos, sin)


def _swa_dqkv(dq, dcur, dprev, cos, sin, *, name):
    S, nq = dq.shape
    nkv = dcur.shape[1] // 2
    nb = S // BLOCK

    def body(dq_ref, dc_ref, dp_ref, c_ref, s_ref, o_ref):
        o_ref[:, :nq] = dq_ref[...]
        d = dc_ref[...] + jnp.where(pl.program_id(0) < nb - 1, dp_ref[...], 0.0)
        o_ref[:, nq:nq + nkv] = _rope_t(d[:, :nkv], c_ref[...], s_ref[...]).astype(BF16)
        o_ref[:, nq + nkv:] = d[:, nkv:].astype(BF16)

    tab = pl.BlockSpec((BLOCK, LANES), lambda m: (m, 0))
    blk = lambda w: pl.BlockSpec((BLOCK, w), lambda m: (m, 0))
    return pl.pallas_call(
        body, name=name, grid=(nb,),
        in_specs=[blk(nq), blk(2 * nkv), pl.BlockSpec((BLOCK, 2 * nkv), lambda m: (jnp.minimum(m + 1, nb - 1), 0)), tab, tab],
        out_specs=blk(nq + 2 * nkv), out_shape=_sds((S, nq + 2 * nkv), BF16), compiler_params=_params("parallel"),
    )(dq, dcur, dprev, cos, sin)


def _lru_gates(u, wri_ref, br, bi, sp):
    ub = u.astype(BF16)
    rs, igs = [], []
    for hb in range(LRU_BLOCKS):
        sl = slice(hb * LRU_BLOCK_W, (hb + 1) * LRU_BLOCK_W)
        ri = jnp.dot(ub[:, sl], wri_ref[hb], preferred_element_type=F32)
        rs.append(ri[:, :LRU_BLOCK_W])
        igs.append(ri[:, LRU_BLOCK_W:])
    r = _sigmoid(jnp.concatenate(rs, axis=1) + br)
    ig = _sigmoid(jnp.concatenate(igs, axis=1) + bi)
    la = -LRU_C * r * sp
    a = jnp.exp(la)
    sq = jnp.sqrt(_neg_expm1(2.0 * la))
    return r, ig, a, sq


def _lru_fwd(xw, cw, cb, wri, br, bi, lam, *, name, comm=None):
    S = xw.shape[0]
    W = D_MODEL
    tm = min(LRU_T, S)

    def body(gate_ref, up_ref, cw_ref, cb_ref, wri_ref, br_ref, bi_ref, lam_ref, y_ref, u_ref, h_ref, cu, ch, a_s, b_s):
        @pl.when(pl.program_id(0) == 0)
        def _():
            cu[...] = jnp.zeros_like(cu)
            ch[...] = jnp.zeros_like(ch)

        up = up_ref[...]
        prev8 = cu[...]
        u = up * cw_ref[3:4, :] + cb_ref[...]
        for d in range(1, LRU_CONV):
            u = u + _shift_down(up, prev8, d) * cw_ref[3 - d:4 - d, :]
        cu[...] = up[tm - 8:tm]
        u_ref[...] = u
        sp, _ = _softplus_neg(lam_ref[...])
        _, ig, a, sq = _lru_gates(u, wri_ref, br_ref[...], bi_ref[...], sp)
        a_s[...] = a
        b_s[...] = sq * (ig * u)
        rid = lax.broadcasted_iota(jnp.int32, (8, W), 0)

        def tile(t, h):
            r0 = pl.multiple_of(t * 8, 8)
            at, bt = a_s[pl.ds(r0, 8), :], b_s[pl.ds(r0, 8), :]
            out = jnp.zeros((8, W), F32)
            for j in range(8):
                h = at[j:j + 1, :] * h + bt[j:j + 1, :]
                out = jnp.where(rid == j, h, out)
            h_ref[pl.ds(r0, 8), :] = out
            return h

        ch[0:1, :] = lax.fori_loop(0, tm // 8, tile, ch[0:1, :])
        g, _ = _gelu_and_grad(gate_ref[...])
        y_ref[...] = (h_ref[...] * g).astype(BF16)

    row = pl.BlockSpec((tm, W), lambda i: (i, 0))
    full = lambda shape: pl.BlockSpec(shape, lambda i: (0,) * len(shape))
    return _call(
        body, name=name, grid=(S // tm,),
        in_specs=[row, pl.BlockSpec((tm, W), lambda i: (i, 1)), full((LRU_CONV, W)), full((1, W)),
                  full((LRU_BLOCKS, LRU_BLOCK_W, 2 * LRU_BLOCK_W)), full((1, W)), full((1, W)), full((1, W))],
        out_specs=[row, row, row], out_shape=[_sds((S, W), BF16), _sds((S, W), F32), _sds((S, W), F32)],
        scratch_shapes=[pltpu.VMEM((8, W), F32), pltpu.VMEM((8, W), F32), pltpu.VMEM((tm, W), F32), pltpu.VMEM((tm, W), F32)],
        params=_params("arbitrary"), comm=comm,
    )(xw, xw, cw, cb, wri, br, bi, lam)


def _lru_bwd(xw, u, h, dy, cw, wri, br, bi, lam, *, name):
    S = xw.shape[0]
    W = D_MODEL
    tm = min(LRU_T, S)
    nb = S // tm

    def body(gate_ref, up_ref, upp_ref, u_ref, h_ref, hp_ref, dy_ref, cw_ref, wri_ref, br_ref, bi_ref, lam_ref,
             dxw_ref, dcw_ref, dcb_ref, dwri_ref, dbr_ref, dbi_ref, dlam_ref, cg, cdu, a_s, d_s, g_s):
        i = pl.program_id(0)
        r_blk = nb - 1 - i

        @pl.when(i == 0)
        def _():
            cg[...] = jnp.zeros_like(cg)
            cdu[...] = jnp.zeros_like(cdu)
            for ref in (dcw_ref, dcb_ref, dwri_ref, dbr_ref, dbi_ref, dlam_ref):
                ref[...] = jnp.zeros_like(ref)

        u = u_ref[...]
        hv = h_ref[...]
        sp, dsp = _softplus_neg(lam_ref[...])
        r, ig, a, sq = _lru_gates(u, wri_ref, br_ref[...], bi_ref[...], sp)
        dy = dy_ref[...].astype(F32)
        g, dgelu = _gelu_and_grad(gate_ref[...])
        dxw_ref[:, :W] = (dy * hv * dgelu).astype(BF16)
        a_s[...] = a
        d_s[...] = dy * g
        rid = lax.broadcasted_iota(jnp.int32, (8, W), 0)

        def tile(t, c):
            r0 = pl.multiple_of((tm // 8 - 1 - t) * 8, 8)
            at, dt = a_s[pl.ds(r0, 8), :], d_s[pl.ds(r0, 8), :]
            out = jnp.zeros((8, W), F32)
            for j in range(7, -1, -1):
                gt = dt[j:j + 1, :] + c
                c = at[j:j + 1, :] * gt
                out = jnp.where(rid == j, gt, out)
            g_s[pl.ds(r0, 8), :] = out
            return c

        cg[0:1, :] = lax.fori_loop(0, tm // 8, tile, cg[0:1, :])
        gt = g_s[...]
        hprev8 = jnp.where(r_blk > 0, hp_ref[...], 0.0)
        da = gt * _shift_down(hv, hprev8, 1)
        iu = ig * u
        d_iu = gt * sq
        dla = da * a - (gt * iu) * (a * a) / sq
        dlam_ref[...] += jnp.sum(dla * r, axis=0, keepdims=True) * (-LRU_C) * dsp
        dr_pre = dla * (-LRU_C) * sp * r * (1.0 - r)
        di_pre = d_iu * u * ig * (1.0 - ig)
        dbr_ref[...] += jnp.sum(dr_pre, axis=0, keepdims=True)
        dbi_ref[...] += jnp.sum(di_pre, axis=0, keepdims=True)
        ub = u.astype(BF16)
        dus = []
        for hb in range(LRU_BLOCKS):
            sl = slice(hb * LRU_BLOCK_W, (hb + 1) * LRU_BLOCK_W)
            dri = jnp.concatenate([dr_pre[:, sl], di_pre[:, sl]], axis=1).astype(BF16)
            dus.append(lax.dot_general(dri, wri_ref[hb], C11, preferred_element_type=F32))
            dwri_ref[hb] += lax.dot_general(ub[:, sl], dri, C00, preferred_element_type=F32)
        du = d_iu * ig + jnp.concatenate(dus, axis=1)
        dcb_ref[...] += jnp.sum(du, axis=0, keepdims=True)
        up = up_ref[...]
        upprev8 = jnp.where(r_blk > 0, upp_ref[...], 0.0)
        dcw_ref[3:4, :] += jnp.sum(du * up, axis=0, keepdims=True)
        for d in range(1, LRU_CONV):
            dcw_ref[3 - d:4 - d, :] += jnp.sum(du * _shift_down(up, upprev8, d), axis=0, keepdims=True)
        next8 = cdu[...]
        dup = du * cw_ref[3:4, :]
        for d in range(1, LRU_CONV):
            dup = dup + _shift_up(du, next8, d) * cw_ref[3 - d:4 - d, :]
        cdu[...] = du[0:8]
        dxw_ref[:, W:] = dup.astype(BF16)

    rev = lambda c: (lambda i: (nb - 1 - i, c))
    halo = lambda c: (lambda i: (jnp.maximum((nb - 1 - i) * (tm // 8) - 1, 0), c))
    full = lambda shape: pl.BlockSpec(shape, lambda i: (0,) * len(shape))
    vec = full((1, W))
    return pl.pallas_call(
        body, name=name, grid=(nb,),
        in_specs=[pl.BlockSpec((tm, W), rev(0)), pl.BlockSpec((tm, W), rev(1)), pl.BlockSpec((8, W), halo(1)),
                  pl.BlockSpec((tm, W), rev(0)), pl.BlockSpec((tm, W), rev(0)), pl.BlockSpec((8, W), halo(0)),
                  pl.BlockSpec((tm, W), rev(0)), full((LRU_CONV, W)), full((LRU_BLOCKS, LRU_BLOCK_W, 2 * LRU_BLOCK_W)),
                  vec, vec, vec],
        out_specs=[pl.BlockSpec((tm, 2 * W), rev(0)), full((LRU_CONV, W)), vec,
                   full((LRU_BLOCKS, LRU_BLOCK_W, 2 * LRU_BLOCK_W)), vec, vec, vec],
        out_shape=[_sds((S, 2 * W), BF16), _sds((LRU_CONV, W), F32), _sds((1, W), F32),
                   _sds((LRU_BLOCKS, LRU_BLOCK_W, 2 * LRU_BLOCK_W), F32), _sds((1, W), F32), _sds((1, W), F32),
                   _sds((1, W), F32)],
        scratch_shapes=[pltpu.VMEM((8, W), F32), pltpu.VMEM((8, W), F32), pltpu.VMEM((tm, W), F32),
                        pltpu.VMEM((tm, W), F32), pltpu.VMEM((tm, W), F32)],
        compiler_params=_params("arbitrary"),
    )(xw, xw, xw, u, h, h, dy, cw, wri, br, bi, lam)


def _rms(x, g):
    r = lax.rsqrt(jnp.mean(x * x, axis=-1, keepdims=True) + RMS_EPS)
    return x * r * g, r


def _mla_pre(c, qg, kvg, cos, sin, *, name):
    S = c.shape[0]
    tm = min(ROW_T, S)
    q0, k0 = C_Q_RANK, C_Q_RANK + C_KV_RANK

    def body(c_ref, qg_ref, kvg_ref, cs_ref, sn_ref, cq_ref, ckv_ref, kr_ref):
        cq_ref[...] = _rms(c_ref[:, :q0], qg_ref[...])[0].astype(BF16)
        ckv_ref[...] = _rms(c_ref[:, q0:k0], kvg_ref[...])[0].astype(BF16)
        kr_ref[...] = _rope(c_ref[:, k0:], cs_ref[...], sn_ref[...]).astype(BF16)

    blk = lambda w: pl.BlockSpec((tm, w), lambda i: (i, 0))
    vec = lambda w: pl.BlockSpec((1, w), lambda i: (0, 0))
    return pl.pallas_call(
        body, name=name, grid=(S // tm,),
        in_specs=[blk(c.shape[1]), vec(C_Q_RANK), vec(C_KV_RANK), blk(LANES), blk(LANES)],
        out_specs=[blk(C_Q_RANK), blk(C_KV_RANK), blk(C_ROPE)],
        out_shape=[_sds((S, C_Q_RANK), BF16), _sds((S, C_KV_RANK), BF16), _sds((S, C_ROPE), BF16)],
        compiler_params=_params("parallel"),
    )(c, qg, kvg, cos, sin)


def _mla_post_bwd(c, dcq_a, dcq_b, dckv, dkr_h, qg, kvg, cos, sin, *, name):
    S = c.shape[0]
    tm = min(ROW_T, S)
    q0, k0 = C_Q_RANK, C_Q_RANK + C_KV_RANK

    def rms_bwd(x, g, dy):
        r = lax.rsqrt(jnp.mean(x * x, axis=-1, keepdims=True) + RMS_EPS)
        uu = dy * g
        dx = r * uu - x * (r * r * r) * jnp.mean(uu * x, axis=-1, keepdims=True)
        return dx, jnp.sum(dy * x * r, axis=0, keepdims=True)

    def body(c_ref, da_ref, db_ref, dkv_ref, dkr_ref, qg_ref, kvg_ref, cs_ref, sn_ref, dc_ref, dqg_ref, dkvg_ref):
        @pl.when(pl.program_id(0) == 0)
        def _():
            dqg_ref[...] = jnp.zeros_like(dqg_ref)
            dkvg_ref[...] = jnp.zeros_like(dkvg_ref)

        dx, dg = rms_bwd(c_ref[:, :q0], qg_ref[...], da_ref[...] + db_ref[...])
        dc_ref[:, :q0] = dx.astype(BF16)
        dqg_ref[...] += dg
        dx, dg = rms_bwd(c_ref[:, q0:k0], kvg_ref[...], dkv_ref[...])
        dc_ref[:, q0:k0] = dx.astype(BF16)
        dkvg_ref[...] += dg
        dkr = dkr_ref[0]
        for hh in range(1, C_HEADS):
            dkr = dkr + dkr_ref[hh]
        dc_ref[:, k0:] = _rope_t(dkr, cs_ref[...], sn_ref[...]).astype(BF16)

    blk = lambda w: pl.BlockSpec((tm, w), lambda i: (i, 0))
    vec = lambda w: pl.BlockSpec((1, w), lambda i: (0, 0))
    return pl.pallas_call(
        body, name=name, grid=(S // tm,),
        in_specs=[blk(c.shape[1]), blk(C_Q_RANK), blk(C_Q_RANK), blk(C_KV_RANK),
                  pl.BlockSpec((C_HEADS, tm, C_ROPE), lambda i: (0, i, 0)), vec(C_Q_RANK), vec(C_KV_RANK), blk(LANES), blk(LANES)],
        out_specs=[blk(c.shape[1]), vec(C_Q_RANK), vec(C_KV_RANK)],
        out_shape=[_sds(c.shape, BF16), _sds((1, C_Q_RANK), F32), _sds((1, C_KV_RANK), F32)],
        compiler_params=_params("arbitrary"),
    )(c, dcq_a, dcq_b, dckv, dkr_h, qg, kvg, cos, sin)


def _rope_heads(x, cos, sin, *, transpose, name):
    S, W = x.shape
    tm = min(ROW_T, S)
    fn = _rope_t if transpose else _rope

    def body(x_ref, c_ref, s_ref, o_ref):
        o_ref[...] = fn(x_ref[...].astype(F32), c_ref[...], s_ref[...]).astype(BF16)

    row = pl.BlockSpec((tm, W), lambda i: (i, 0))
    tab = pl.BlockSpec((tm, LANES), lambda i: (i, 0))
    return pl.pallas_call(body, name=name, grid=(S // tm,), in_specs=[row, tab, tab], out_specs=row,
                          out_shape=_sds((S, W), BF16), compiler_params=_params("parallel"))(x, cos, sin)


MLA_GROUP = 4
MLA_SCALE = (C_NOPE + C_ROPE) ** -0.5
LOG2E = 1.4426950408889634


def _mla_scores2(qn, qr, kn, kr, diagonal):
    s = lax.dot_general(qn, kn, C11, preferred_element_type=F32) + lax.dot_general(qr, kr, C11, preferred_element_type=F32)
    s = s * (MLA_SCALE * LOG2E)
    if diagonal:
        row = lax.broadcasted_iota(jnp.int32, s.shape, 0)
        col = lax.broadcasted_iota(jnp.int32, s.shape, 1)
        s = jnp.where(col <= row, s, NEG)
    return s


def _causal_pairs(n, query_major):
    if query_major:
        pairs = [(i, j) for i in range(n) for j in range(i + 1)]
    else:
        pairs = [(i, j) for j in range(n) for i in range(j, n)]
    return jnp.asarray([p[0] for p in pairs], jnp.int32), jnp.asarray([p[1] for p in pairs], jnp.int32)


def _mla_flash_fwd(qn, qr, kv, kr, *, name):
    S = qn.shape[0]
    H, G, t = C_HEADS, MLA_GROUP, min(FLASH_T, S)
    qi, kj = _causal_pairs(S // t, True)

    def body(qi_ref, kj_ref, qn_ref, qr_ref, kn_ref, v_ref, kr_ref, o_ref, lse_ref, *scr):
        m_s, l_s, acc = scr[:G], scr[G:2 * G], scr[2 * G:]
        p_id = pl.program_id(1)
        i, j = qi_ref[p_id], kj_ref[p_id]
        sls = [slice(hh * LANES, (hh + 1) * LANES) for hh in range(G)]

        @pl.when(j == 0)
        def _():
            for hh in range(G):
                m_s[hh][...] = jnp.full_like(m_s[hh], NEG)
                l_s[hh][...] = jnp.zeros_like(l_s[hh])
                acc[hh][...] = jnp.zeros_like(acc[hh])

        def step(diagonal):
            ss = [_mla_scores2(qn_ref[:, sls[hh]], qr_ref[hh], kn_ref[:, sls[hh]], kr_ref[...], diagonal) for hh in range(G)]
            for hh in range(G):
                m_prev = m_s[hh][...]
                m_new = jnp.maximum(m_prev, jnp.max(ss[hh], axis=-1, keepdims=True))
                corr = jnp.exp2(m_prev - m_new)
                p = jnp.exp2(ss[hh] - m_new[:, 0:1])
                l_s[hh][...] = corr * l_s[hh][...] + jnp.sum(p, axis=-1, keepdims=True)
                acc[hh][...] = corr * acc[hh][...] + jnp.dot(p.astype(BF16), v_ref[:, sls[hh]], preferred_element_type=F32)
                m_s[hh][...] = m_new

        @pl.when(j < i)
        def _():
            step(False)

        @pl.when(j == i)
        def _():
            step(True)
            for hh in range(G):
                o_ref[:, sls[hh]] = (acc[hh][...] / l_s[hh][...]).astype(BF16)
                lse_ref[:, sls[hh]] = m_s[hh][...] + jnp.log2(l_s[hh][...])

    wide = lambda which, off: pl.BlockSpec((t, G * LANES), lambda h, p, qi, kj: ((qi if which == "q" else kj)[p], off + h))
    return pl.pallas_call(
        body, name=name,
        grid_spec=pltpu.PrefetchScalarGridSpec(
            num_scalar_prefetch=2, grid=(H // G, qi.shape[0]),
            in_specs=[wide("q", 0), pl.BlockSpec((G, t, C_ROPE), lambda h, p, qi, kj: (h, qi[p], 0)),
                      wide("k", 0), wide("k", H // G), pl.BlockSpec((t, C_ROPE), lambda h, p, qi, kj: (kj[p], 0))],
            out_specs=[wide("q", 0), wide("q", 0)],
            scratch_shapes=[pltpu.VMEM((t, LANES), F32)] * (3 * G)),
        out_shape=[_sds((S, H * C_V), BF16), _sds((S, H * LANES), F32)],
        compiler_params=_params("parallel", "arbitrary"),
    )(qi, kj, qn, qr, kv, kv, kr)


def _mla_delta(do, o, *, name):
    S, W = do.shape
    tm = min(ROW_T, S)

    def body(do_ref, o_ref, d_ref):
        for h in range(C_HEADS):
            sl = slice(h * C_V, (h + 1) * C_V)
            d = jnp.sum(do_ref[:, sl].astype(F32) * o_ref[:, sl].astype(F32), axis=-1, keepdims=True)
            d_ref[:, sl] = jnp.broadcast_to(d, (tm, C_V))

    row = pl.BlockSpec((tm, W), lambda i: (i, 0))
    return pl.pallas_call(body, name=name, grid=(S // tm,), in_specs=[row, row], out_specs=row,
                          out_shape=_sds((S, W), F32), compiler_params=_params("parallel"))(do, o)


def _mla_flash_dq(qn, qr, kv, kr, do, lse, delta, *, name):
    S = qn.shape[0]
    H, G, t = C_HEADS, MLA_GROUP, min(FLASH_T, S)
    qi, kj = _causal_pairs(S // t, True)

    def body(qi_ref, kj_ref, qn_ref, qr_ref, kn_ref, v_ref, kr_ref, do_ref, lse_ref, dl_ref, dqn_ref, dqr_ref, an, ar):
        p_id = pl.program_id(1)
        i, j = qi_ref[p_id], kj_ref[p_id]

        @pl.when(j == 0)
        def _():
            an[...] = jnp.zeros_like(an)
            ar[...] = jnp.zeros_like(ar)

        def step(diagonal):
            for hh in range(G):
                sl = slice(hh * LANES, (hh + 1) * LANES)
                s = _mla_scores2(qn_ref[:, sl], qr_ref[hh], kn_ref[:, sl], kr_ref[...], diagonal)
                p = jnp.exp2(s - lse_ref[:, hh * LANES:hh * LANES + 1])
                dp = lax.dot_general(do_ref[:, sl], v_ref[:, sl], C11, preferred_element_type=F32)
                ds = (p * (dp - dl_ref[:, hh * LANES:hh * LANES + 1])).astype(BF16)
                an[:, sl] += jnp.dot(ds, kn_ref[:, sl], preferred_element_type=F32)
                ar[hh] += jnp.dot(ds, kr_ref[...], preferred_element_type=F32)

        @pl.when(j < i)
        def _():
            step(False)

        @pl.when(j == i)
        def _():
            step(True)
            dqn_ref[...] = (an[...] * MLA_SCALE).astype(BF16)
            dqr_ref[...] = ar[...] * MLA_SCALE

    wide = lambda which, off: pl.BlockSpec((t, G * LANES), lambda h, p, qi, kj: ((qi if which == "q" else kj)[p], off + h))
    qrb = pl.BlockSpec((G, t, C_ROPE), lambda h, p, qi, kj: (h, qi[p], 0))
    return pl.pallas_call(
        body, name=name,
        grid_spec=pltpu.PrefetchScalarGridSpec(
            num_scalar_prefetch=2, grid=(H // G, qi.shape[0]),
            in_specs=[wide("q", 0), qrb, wide("k", 0), wide("k", H // G),
                      pl.BlockSpec((t, C_ROPE), lambda h, p, qi, kj: (kj[p], 0)), wide("q", 0), wide("q", 0), wide("q", 0)],
            out_specs=[wide("q", 0), qrb],
            scratch_shapes=[pltpu.VMEM((t, G * LANES), F32), pltpu.VMEM((G, t, C_ROPE), F32)]),
        out_shape=[_sds((S, H * C_NOPE), BF16), _sds((H, S, C_ROPE), F32)],
        compiler_params=_params("parallel", "arbitrary"),
    )(qi, kj, qn, qr, kv, kv, kr, do, lse, delta)


def _mla_flash_dkv(qn, qr, kv, kr, do, lse, delta, *, name):
    S = qn.shape[0]
    H, G, t = C_HEADS, MLA_GROUP, min(FLASH_T, S)
    n = S // t
    qi, kj = _causal_pairs(n, False)

    def body(qi_ref, kj_ref, qn_ref, qr_ref, kn_ref, v_ref, kr_ref, do_ref, lse_ref, dl_ref, dkn_ref, dv_ref, dkr_ref, akn, av, akr):
        p_id = pl.program_id(1)
        i, j = qi_ref[p_id], kj_ref[p_id]

        def step(diagonal):
            for hh in range(G):
                sl = slice(hh * LANES, (hh + 1) * LANES)
                s = _mla_scores2(qn_ref[:, sl], qr_ref[hh], kn_ref[:, sl], kr_ref[...], diagonal)
                p = jnp.exp2(s - lse_ref[:, hh * LANES:hh * LANES + 1])
                dp = lax.dot_general(do_ref[:, sl], v_ref[:, sl], C11, preferred_element_type=F32)
                ds = (p * (dp - dl_ref[:, hh * LANES:hh * LANES + 1])).astype(BF16)
                av[:, sl] += lax.dot_general(p.astype(BF16), do_ref[:, sl], C00, preferred_element_type=F32)
                akn[:, sl] += lax.dot_general(ds, qn_ref[:, sl], C00, preferred_element_type=F32)
                akr[hh] += lax.dot_general(ds, qr_ref[hh], C00, preferred_element_type=F32)

        @pl.when(i == j)
        def _():
            akn[...] = jnp.zeros_like(akn)
            av[...] = jnp.zeros_like(av)
            akr[...] = jnp.zeros_like(akr)
            step(True)

        @pl.when(i > j)
        def _():
            step(False)

        @pl.when(i == n - 1)
        def _():
            dkn_ref[...] = (akn[...] * MLA_SCALE).astype(BF16)
            dv_ref[...] = av[...].astype(BF16)
            dkr_ref[...] = akr[...] * MLA_SCALE

    wide = lambda which, off: pl.BlockSpec((t, G * LANES), lambda h, p, qi, kj: ((qi if which == "q" else kj)[p], off + h))
    krb = pl.BlockSpec((G, t, C_ROPE), lambda h, p, qi, kj: (h, kj[p], 0))
    return pl.pallas_call(
        body, name=name,
        grid_spec=pltpu.PrefetchScalarGridSpec(
            num_scalar_prefetch=2, grid=(H // G, qi.shape[0]),
            in_specs=[wide("q", 0), pl.BlockSpec((G, t, C_ROPE), lambda h, p, qi, kj: (h, qi[p], 0)), wide("k", 0),
                      wide("k", H // G), pl.BlockSpec((t, C_ROPE), lambda h, p, qi, kj: (kj[p], 0)),
                      wide("q", 0), wide("q", 0), wide("q", 0)],
            out_specs=[wide("k", 0), wide("k", 0), krb],
            scratch_shapes=[pltpu.VMEM((t, G * LANES), F32), pltpu.VMEM((t, G * LANES), F32), pltpu.VMEM((G, t, C_ROPE), F32)]),
        out_shape=[_sds((S, H * C_NOPE), BF16), _sds((S, H * C_V), BF16), _sds((H, S, C_ROPE), F32)],
        compiler_params=_params("parallel", "arbitrary"),
    )(qi, kj, qn, qr, kv, kv, kr, do, lse, delta)


def _place():
    return lax.axis_index("x"), lax.axis_index("y"), lax.axis_index("c")


def _other_chips(x, y):
    return [(1 - x, y), (x, 1 - y), (1 - x, 1 - y)]


def _all_gather_chips(p, *, name):
    R, C = p.shape

    def body(p_ref, o_ref, send_sems, recv_sems, local_sem):
        x, y, c = _place()
        me = 2 * x + y
        local = pltpu.make_async_copy(p_ref, o_ref.at[me], local_sem)
        local.start()
        copies = [pltpu.make_async_remote_copy(src_ref=p_ref, dst_ref=o_ref.at[me], send_sem=send_sems.at[k],
                                               recv_sem=recv_sems.at[k], device_id=(px, py, c), device_id_type=MESH)
                  for k, (px, py) in enumerate(_other_chips(x, y))]
        for cp in copies:
            cp.start()
        for cp in copies:
            cp.wait()
        local.wait()

    any_spec = pl.BlockSpec(memory_space=pl.ANY)
    return pl.pallas_call(
        body, name=name, in_specs=[any_spec], out_specs=any_spec, out_shape=_sds((4, R, C), p.dtype),
        scratch_shapes=[pltpu.SemaphoreType.DMA((3,)), pltpu.SemaphoreType.DMA((3,)), pltpu.SemaphoreType.DMA(())],
    )(p)


def _shard_of(ref, axis, pos, size):
    idx = [slice(None)] * len(ref.shape)
    idx[axis] = pl.ds(pos * size, size)
    return ref.at[tuple(idx)]


def _shard_pos(chip, swapped):
    return (chip % 2) * 2 + chip // 2 if swapped else chip


class _Comm:
    def __init__(self, inputs, out_shapes, sems, start, finish, deliver):
        self.inputs, self.out_shapes, self.sems = list(inputs), list(out_shapes), list(sems)
        self.start, self.finish, self.deliver = start, finish, deliver


def _call(body, *, name, grid, in_specs, out_specs, out_shape, scratch_shapes=(), params, comm=None):
    in_specs, out_specs, out_shape, scratch_shapes = list(in_specs), list(out_specs), list(out_shape), list(scratch_shapes)
    if comm is None:
        return pl.pallas_call(body, name=name, grid=grid, in_specs=in_specs, out_specs=out_specs, out_shape=out_shape,
                              scratch_shapes=scratch_shapes, compiler_params=params)
    n_in, n_out, n_scr = len(in_specs), len(out_specs), len(scratch_shapes)
    c_in, c_out = len(comm.inputs), len(comm.out_shapes)

    def hosted(*refs):
        a, rest = refs[:n_in], refs[n_in:]
        cin, rest = rest[:c_in], rest[c_in:]
        o, rest = rest[:n_out], rest[n_out:]
        cout, rest = rest[:c_out], rest[c_out:]
        scr, sems = rest[:n_scr], rest[n_scr:]
        first = functools.reduce(jnp.logical_and, [pl.program_id(d) == 0 for d in range(len(grid))])
        last = functools.reduce(jnp.logical_and, [pl.program_id(d) == grid[d] - 1 for d in range(len(grid))])

        @pl.when(first)
        def _():
            comm.start(cin, cout, sems)

        body(*a, *o, *scr)

        @pl.when(last)
        def _():
            comm.finish(cin, cout, sems)

    any_spec = pl.BlockSpec(memory_space=pl.ANY)
    call = pl.pallas_call(
        hosted, name=name, grid=grid, in_specs=in_specs + [any_spec] * c_in, out_specs=out_specs + [any_spec] * c_out,
        out_shape=out_shape + comm.out_shapes, scratch_shapes=scratch_shapes + comm.sems, compiler_params=params)

    def run(*args):
        outs = call(*args, *comm.inputs)
        comm.deliver(outs[n_out:])
        return outs[:n_out]

    return run


def _run_comm(comm, *, name):
    c_in, c_out = len(comm.inputs), len(comm.out_shapes)

    def body(*refs):
        cin, cout, sems = refs[:c_in], refs[c_in:c_in + c_out], refs[c_in + c_out:]
        comm.start(cin, cout, sems)
        comm.finish(cin, cout, sems)

    any_spec = pl.BlockSpec(memory_space=pl.ANY)
    outs = pl.pallas_call(body, name=name, in_specs=[any_spec] * c_in, out_specs=[any_spec] * c_out,
                          out_shape=comm.out_shapes, scratch_shapes=comm.sems)(*comm.inputs)
    comm.deliver(outs)


def _gather_comm(items, deliver):
    n = len(items)
    shard_shapes = [a.shape if j is None else a.shape[1:] for a, j, _, _ in items]
    axes = [ax for _, _, ax, _ in items]
    swapped = [sw for _, _, _, sw in items]
    sizes = [s[a] for s, a in zip(shard_shapes, axes)]
    halves = [s[-2] // 2 for s in shard_shapes]
    full = [tuple(4 * d if i == a else d for i, d in enumerate(s)) for s, a in zip(shard_shapes, axes)]

    def mine(ins, k):
        j = items[k][1]
        return ins[k] if j is None else ins[k].at[j]

    def half_of(ref, k, half, chip=None):
        nd = len(ref.shape)
        split = nd - 2
        idx = [slice(None)] * nd
        start = half * halves[k]
        if chip is not None:
            pos = _shard_pos(chip, swapped[k]) * sizes[k]
            if axes[k] == split:
                start = start + pos
            else:
                idx[axes[k]] = pl.ds(pos, sizes[k])
        idx[split] = pl.ds(start, halves[k])
        return ref.at[tuple(idx)]

    def local_copy(ins, outs, sems, k, me):
        return pltpu.make_async_copy(mine(ins, k), _shard_of(outs[k], axes[k], _shard_pos(me, swapped[k]), sizes[k]), sems[4].at[k])

    def ici_copy(ins, outs, sems, k, j, peer, c, landing_chip):
        return pltpu.make_async_remote_copy(
            src_ref=half_of(mine(ins, k), k, c), dst_ref=half_of(outs[k], k, c, chip=landing_chip), send_sem=sems[0].at[3 * k + j],
            recv_sem=sems[1].at[3 * k + j], device_id=(peer[0], peer[1], c), device_id_type=MESH)

    def pass_copy(outs, sems, k, j, half, chip, sibling):
        region = half_of(outs[k], k, half, chip=chip)
        return pltpu.make_async_remote_copy(src_ref=region, dst_ref=region, send_sem=sems[2].at[3 * k + j],
                                            recv_sem=sems[3].at[3 * k + j], device_id=sibling, device_id_type=MESH)

    def start(ins, outs, sems):
        x, y, c = _place()
        me = 2 * x + y
        for k in range(n):
            local_copy(ins, outs, sems, k, me).start()
            for j, peer in enumerate(_other_chips(x, y)):
                ici_copy(ins, outs, sems, k, j, peer, c, me).start()

    def finish(ins, outs, sems):
        x, y, c = _place()
        me = 2 * x + y
        chips = _other_chips(x, y)
        sibling = (x, y, 1 - c)
        for k in range(n):
            for j, peer in enumerate(chips):
                ici_copy(ins, outs, sems, k, j, peer, c, 2 * peer[0] + peer[1]).wait_recv()
                pass_copy(outs, sems, k, j, c, 2 * peer[0] + peer[1], sibling).start()
        for k in range(n):
            for j, peer in enumerate(chips):
                pass_copy(outs, sems, k, j, 1 - c, 2 * peer[0] + peer[1], sibling).wait_recv()
        for k in range(n):
            local_copy(ins, outs, sems, k, me).wait()
            for j, peer in enumerate(chips):
                ici_copy(ins, outs, sems, k, j, peer, c, me).wait_send()
                pass_copy(outs, sems, k, j, c, 2 * peer[0] + peer[1], sibling).wait_send()

    return _Comm([a for a, _, _, _ in items], [_sds(f, a.dtype) for f, (a, _, _, _) in zip(full, items)],
                 [pltpu.SemaphoreType.DMA((3 * n,))] * 4 + [pltpu.SemaphoreType.DMA((n,))], start, finish, deliver)


def _scatter_comm(items, deliver):
    n = len(items)
    axes = [ax for _, ax, _ in items]
    swapped = [sw for _, _, sw in items]
    sizes = [g.shape[a] // 4 for g, a, _ in items]
    shard = [tuple(d // 4 if i == a else d for i, d in enumerate(g.shape)) for g, a, _ in items]

    def copies(ins, outs, sems):
        x, y, c = _place()
        me = 2 * x + y
        out = []
        for k in range(n):
            own = _shard_of(ins[k], axes[k], _shard_pos(me, swapped[k]), sizes[k])
            out.append(pltpu.make_async_copy(own, outs[k].at[3], sems[2].at[k]))
            for j, (px, py) in enumerate(_other_chips(x, y)):
                src = _shard_of(ins[k], axes[k], _shard_pos(2 * px + py, swapped[k]), sizes[k])
                out.append(pltpu.make_async_remote_copy(src_ref=src, dst_ref=outs[k].at[j], send_sem=sems[0].at[3 * k + j],
                                                        recv_sem=sems[1].at[3 * k + j], device_id=(px, py, c), device_id_type=MESH))
        return out

    def start(ins, outs, sems):
        for cp in copies(ins, outs, sems):
            cp.start()

    def finish(ins, outs, sems):
        for cp in copies(ins, outs, sems):
            cp.wait()

    return _Comm([g for g, _, _ in items], [_sds((4,) + s, g.dtype) for s, (g, _, _) in zip(shard, items)],
                 [pltpu.SemaphoreType.DMA((3 * n,)), pltpu.SemaphoreType.DMA((3 * n,)), pltpu.SemaphoreType.DMA((n,))],
                 start, finish, deliver)


def _row_tile(rows, cols, budget=2 ** 20):
    best = None
    for t in range(8, rows + 1, 8):
        if rows % t == 0 and t * cols * 4 <= budget:
            best = t
    return best or rows


def _sum_partials(recv, into, layer, layers, *, name):
    _, R, C = recv.shape
    tr = _row_tile(R, C)
    nt = R // tr

    def body(own_ref, r0_ref, r1_ref, r2_ref, *rest):
        f = lambda ref: ref[...].astype(F32)
        rest[-1][...] = ((f(own_ref) + f(r0_ref)) + f(r1_ref)) + f(r2_ref)

    rspec = lambda k: pl.BlockSpec((None, tr, C), lambda i: (k, i, 0))
    extra = [] if into is None else [pl.BlockSpec(memory_space=pl.ANY)]
    return pl.pallas_call(
        body, name=name, grid=(nt,), in_specs=[rspec(3), rspec(0), rspec(1), rspec(2)] + extra,
        out_specs=pl.BlockSpec((tr, C), lambda i: (layer * nt + i, 0)), out_shape=_sds((layers * R, C), F32),
        input_output_aliases={} if into is None else {4: 0}, compiler_params=_params("parallel"),
    )(recv, recv, recv, recv, *([] if into is None else [into]))


def _swap_cores(parts, *, name):
    n = len(parts)

    def body(*refs):
        ins, outs = refs[:n], refs[n:2 * n]
        send_sems, recv_sems = refs[2 * n:]
        x, y, c = _place()
        copies = [pltpu.make_async_remote_copy(src_ref=ins[k], dst_ref=outs[k], send_sem=send_sems.at[k], recv_sem=recv_sems.at[k],
                                               device_id=(x, y, 1 - c), device_id_type=MESH) for k in range(n)]
        for cp in copies:
            cp.start()
        for cp in copies:
            cp.wait()

    any_spec = pl.BlockSpec(memory_space=pl.ANY)
    return pl.pallas_call(
        body, name=name, in_specs=[any_spec] * n, out_specs=[any_spec] * n, out_shape=[_sds(p.shape, p.dtype) for p in parts],
        scratch_shapes=[pltpu.SemaphoreType.DMA((n,)), pltpu.SemaphoreType.DMA((n,))],
    )(*parts)


def _all_reduce_small(v, *, name):
    r, C = v.shape

    def body(v_ref, o_ref, buf, send_sems, recv_sems):
        x, y, c = _place()
        me = 4 * x + 2 * y + c
        buf[me] = v_ref[...]
        peers = []
        for k in range(1, 8):
            kx, ky, kc = (k >> 2) & 1, (k >> 1) & 1, k & 1
            px = 1 - x if kx else x
            py = 1 - y if ky else y
            pc = 1 - c if kc else c
            peers.append((px, py, pc))
        copies = []
        for k, peer in enumerate(peers):
            cp = pltpu.make_async_remote_copy(src_ref=v_ref, dst_ref=buf.at[me], send_sem=send_sems.at[k],
                                              recv_sem=recv_sems.at[me], device_id=peer, device_id_type=MESH)
            cp.start()
            copies.append(cp)
        for k, (px, py, pc) in enumerate(peers):
            src = 4 * px + 2 * py + pc
            pltpu.make_async_remote_copy(src_ref=v_ref, dst_ref=buf.at[src], send_sem=send_sems.at[k],
                                         recv_sem=recv_sems.at[src], device_id=peers[k], device_id_type=MESH).wait_recv()
        for cp in copies:
            cp.wait_send()
        acc = buf[0]
        for d in range(1, 8):
            acc = acc + buf[d]
        o_ref[...] = acc

    vm = pl.BlockSpec(memory_space=pltpu.VMEM)
    return pl.pallas_call(
        body, name=name, in_specs=[vm], out_specs=vm, out_shape=_sds((r, C), F32),
        scratch_shapes=[pltpu.VMEM((8, r, C), F32), pltpu.SemaphoreType.DMA((7,)), pltpu.SemaphoreType.DMA((8,))],
    )(v)


def _adamw(w, m, v, ga, gb, *, name):
    R, C = w.shape
    tr = _row_tile(R, C)
    has_b = gb is not None
    c1 = 1.0 / (1.0 - ADAM_B1 ** ADAM_STEP)
    c2 = 1.0 / (1.0 - ADAM_B2 ** ADAM_STEP)

    def body(*refs):
        if has_b:
            w_ref, m_ref, v_ref, ga_ref, gb_ref, g_ref, d_ref, nm_ref, nv_ref = refs
            g = ga_ref[...] + gb_ref[...]
        else:
            w_ref, m_ref, v_ref, ga_ref, g_ref, d_ref, nm_ref, nv_ref = refs
            g = ga_ref[...]
        nm = ADAM_B1 * m_ref[...] + (1.0 - ADAM_B1) * g
        nv = ADAM_B2 * v_ref[...] + (1.0 - ADAM_B2) * (g * g)
        g_ref[...] = g
        nm_ref[...] = nm
        nv_ref[...] = nv
        d_ref[...] = -ADAM_LR * ((nm * c1) / (jnp.sqrt(nv * c2) + ADAM_EPS) + ADAM_WD * w_ref[...])

    blk = pl.BlockSpec((tr, C), lambda i: (i, 0))
    n_in = 5 if has_b else 4
    args = (w, m, v, ga) + ((gb,) if has_b else ())
    return pl.pallas_call(body, name=name, grid=(R // tr,), in_specs=[blk] * n_in, out_specs=[blk] * 4,
                          out_shape=[_sds((R, C), F32)] * 4, compiler_params=_params("parallel"))(*args)


def _seg_rows(n, cols):
    return -(-n // (16 * cols)) * 16


def _pack(arrays, dtype, cols=PACK_COLS, row_mult=512):
    parts, rows = [], 0
    for a in arrays:
        n = int(np.prod(a.shape))
        r = _seg_rows(n, cols)
        flat = a.reshape(-1).astype(dtype)
        if r * cols != n:
            flat = jnp.pad(flat, (0, r * cols - n))
        parts.append(flat.reshape(r, cols))
        rows += r
    pad = -rows % row_mult
    if pad:
        parts.append(jnp.zeros((pad, cols), dtype))
    return jnp.concatenate(parts, axis=0)


def _unpack(packed, shapes, cols=PACK_COLS):
    out, r0 = [], 0
    for shp in shapes:
        n = int(np.prod(shp))
        used = -(-n // cols)
        out.append(packed[r0:r0 + used].reshape(-1)[:n].reshape(shp))
        r0 += _seg_rows(n, cols)
    return out


def _rope_tables(seq):
    inv = 1.0 / (ROPE_THETA ** (jnp.arange(0, 64, 2, dtype=F32) / 64))
    ang = jnp.arange(seq, dtype=F32)[:, None] * inv[None, :]
    cos, sin = jnp.cos(ang), jnp.sin(ang)
    cos128 = jnp.concatenate([cos, cos, cos, cos], axis=1)
    sin128 = jnp.concatenate([-sin, sin, -sin, sin], axis=1)
    return cos128, sin128


def _ffn_perm(a):
    lead = a.shape[:-1]
    nj = D_FF // FFN_TC
    return jnp.swapaxes(a.reshape(lead + (2, nj, FFN_TC)), -3, -2).reshape(lead + (2 * D_FF,))


def _ffn_unperm(a):
    lead = a.shape[:-1]
    nj = D_FF // FFN_TC
    return jnp.swapaxes(a.reshape(lead + (nj, 2, FFN_TC)), -3, -2).reshape(lead + (2 * D_FF,))


def _mixer_a_fwd(xb, w, j, cos, sin, tag, gathers):
    qkv = _mm(xb, w["a_w_qkv"][j], name=f"mm_qkv_{tag}")
    qkv_r = _rope_cols(qkv, cos, sin, (A_HEADS + A_KV_HEADS) * A_HEAD_DIM, name=f"rope_qkv_{tag}",
                       comm=gathers.get(f"rope_qkv_{tag}"))
    o = _swa_fwd(qkv_r, w["a_sinks"][j], name=f"swa_fwd_{tag}", comm=gathers.get(f"swa_fwd_{tag}"))
    return (o, w["a_w_o"][j]), (xb, qkv_r, o)


def _mixer_a_bwd(dzb, res, w, j, cos, sin, tag, grads, make_comm):
    xb, qkv_r, o = res
    do = _mm(dzb, w["a_w_o"][j], tb=True, out_dtype=BF16, name=f"mm_dao_{tag}")
    grads["a_w_o"][j] = _mm(o, dzb, ta=True, out_dtype=BF16, tk=GRAD_TK, name=f"mm_gao_{tag}")
    dq, dcur, dprev, dsink = _swa_bwd(qkv_r, w["a_sinks"][j], do, cos, sin, name=f"swa_bwd_{tag}", comm=make_comm())
    grads["a_sinks"][j] = dsink[0, :A_HEADS]
    dqkv = _swa_dqkv(dq, dcur, dprev, cos, sin, name=f"swa_dqkv_{tag}")
    grads["a_w_qkv"][j] = _mm(xb, dqkv, ta=True, out_dtype=BF16, tk=GRAD_TK, name=f"mm_gqkv_{tag}")
    return dqkv, w["a_w_qkv"][j]


def _mixer_b_fwd(xb, w, j, tag, gathers):
    xw = _mm(xb, w["b_w_in"][j], name=f"mm_bin_{tag}")
    wri = jnp.concatenate([w["b_w_rgate"][j], w["b_w_igate"][j]], axis=-1)
    y, u, h = _lru_fwd(xw, w["b_conv_w"][j], w["b_conv_b"][j][None], wri, w["b_b_rgate"][j][None],
                       w["b_b_igate"][j][None], w["b_lambda"][j][None], name=f"lru_fwd_{tag}", comm=gathers.get(f"lru_fwd_{tag}"))
    return (y, w["b_w_o"][j]), (xb, xw, wri, u, h, y)


def _mixer_b_bwd(dzb, res, w, j, tag, grads):
    xb, xw, wri, u, h, y = res
    dy = _mm(dzb, w["b_w_o"][j], tb=True, out_dtype=BF16, name=f"mm_dbo_{tag}")
    grads["b_w_o"][j] = _mm(y, dzb, ta=True, out_dtype=BF16, tk=GRAD_TK, name=f"mm_gbo_{tag}")
    dxw, dcw, dcb, dwri, dbr, dbi, dlam = _lru_bwd(
        xw, u, h, dy, w["b_conv_w"][j], wri, w["b_b_rgate"][j][None], w["b_b_igate"][j][None], w["b_lambda"][j][None],
        name=f"lru_bwd_{tag}")
    grads["b_conv_w"][j], grads["b_conv_b"][j] = dcw, dcb[0]
    grads["b_w_rgate"][j], grads["b_w_igate"][j] = dwri[..., :LRU_BLOCK_W].astype(BF16), dwri[..., LRU_BLOCK_W:].astype(BF16)
    grads["b_b_rgate"][j], grads["b_b_igate"][j], grads["b_lambda"][j] = dbr[0], dbi[0], dlam[0]
    grads["b_w_in"][j] = _mm(xb, dxw, ta=True, out_dtype=BF16, tk=GRAD_TK, name=f"mm_gbin_{tag}")
    return dxw, w["b_w_in"][j]


def _mla_weights(w, j):
    H = C_HEADS
    uq = w["c_w_uq"][j].reshape(C_Q_RANK, H, C_NOPE + C_ROPE)
    ukv = w["c_w_ukv"][j].reshape(C_KV_RANK, H, C_NOPE + C_V)
    uq_n = uq[:, :, :C_NOPE].reshape(C_Q_RANK, H * C_NOPE)
    uq_r = uq[:, :, C_NOPE:].reshape(C_Q_RANK, H * C_ROPE)
    ukv_p = jnp.concatenate([ukv[:, :, :C_NOPE].reshape(C_KV_RANK, H * C_NOPE),
                             ukv[:, :, C_NOPE:].reshape(C_KV_RANK, H * C_V)], axis=1)
    return uq_n, uq_r, ukv_p


def _mixer_c_fwd(xb, w, j, cos, sin, tag):
    S = xb.shape[0]
    H = C_HEADS
    uq_n, uq_r, ukv_p = _mla_weights(w, j)
    c = _mm(xb, w["c_w_down"][j], name=f"mm_cdown_{tag}")
    cq, ckv, kr = _mla_pre(c, w["c_q_norm"][j][None], w["c_kv_norm"][j][None], cos, sin, name=f"mla_pre_{tag}")
    qn = _mm(cq, uq_n, out_dtype=BF16, name=f"mm_uqn_{tag}")
    qr_flat = _rope_heads(_mm(cq, uq_r, name=f"mm_uqr_{tag}"), cos, sin, transpose=False, name=f"rope_qr_{tag}")
    qr = jnp.transpose(qr_flat.reshape(S, H, C_ROPE), (1, 0, 2))
    kv = _mm(ckv, ukv_p, out_dtype=BF16, name=f"mm_ukv_{tag}")
    o, lse = _mla_flash_fwd(qn, qr, kv, kr, name=f"mla_fwd_{tag}")
    return (o, w["c_w_o"][j]), (xb, c, cq, ckv, kr, qn, qr, kv, o, lse, uq_n, uq_r, ukv_p)


def _mixer_c_bwd(dzb, res, w, j, cos, sin, tag, grads):
    xb, c, cq, ckv, kr, qn, qr, kv, o, lse, uq_n, uq_r, ukv_p = res
    S = xb.shape[0]
    H = C_HEADS
    do = _mm(dzb, w["c_w_o"][j], tb=True, out_dtype=BF16, name=f"mm_dco_{tag}")
    grads["c_w_o"][j] = _mm(o, dzb, ta=True, out_dtype=BF16, tk=GRAD_TK, name=f"mm_gco_{tag}")
    delta = _mla_delta(do, o, name=f"mla_delta_{tag}")
    dqn, dqr = _mla_flash_dq(qn, qr, kv, kr, do, lse, delta, name=f"mla_dq_{tag}")
    dkn, dv, dkr_h = _mla_flash_dkv(qn, qr, kv, kr, do, lse, delta, name=f"mla_dkv_{tag}")
    dkv = jnp.concatenate([dkn, dv], axis=1)
    dqr_flat = _rope_heads(jnp.transpose(dqr, (1, 0, 2)).reshape(S, H * C_ROPE), cos, sin, transpose=True, name=f"rope_dqr_{tag}")
    g_uq_n = _mm(cq, dqn, ta=True, out_dtype=BF16, tk=GRAD_TK, name=f"mm_guqn_{tag}")
    g_uq_r = _mm(cq, dqr_flat, ta=True, out_dtype=BF16, tk=GRAD_TK, name=f"mm_guqr_{tag}")
    g_ukv = _mm(ckv, dkv, ta=True, out_dtype=BF16, tk=GRAD_TK, name=f"mm_gukv_{tag}")
    grads["c_w_uq"][j] = jnp.concatenate([g_uq_n.reshape(C_Q_RANK, H, C_NOPE), g_uq_r.reshape(C_Q_RANK, H, C_ROPE)],
                                         axis=2).reshape(C_Q_RANK, H * (C_NOPE + C_ROPE))
    grads["c_w_ukv"][j] = jnp.concatenate([g_ukv[:, :H * C_NOPE].reshape(C_KV_RANK, H, C_NOPE),
                                           g_ukv[:, H * C_NOPE:].reshape(C_KV_RANK, H, C_V)],
                                          axis=2).reshape(C_KV_RANK, H * (C_NOPE + C_V))
    dcq_a = _mm(dqn, uq_n, tb=True, name=f"mm_dcqa_{tag}")
    dcq_b = _mm(dqr_flat, uq_r, tb=True, name=f"mm_dcqb_{tag}")
    dckv = _mm(dkv, ukv_p, tb=True, name=f"mm_dckv_{tag}")
    dc, dqg, dkvg = _mla_post_bwd(c, dcq_a, dcq_b, dckv, dkr_h, w["c_q_norm"][j][None], w["c_kv_norm"][j][None], cos, sin,
                                  name=f"mla_post_{tag}")
    grads["c_q_norm"][j], grads["c_kv_norm"][j] = dqg[0], dkvg[0]
    grads["c_w_down"][j] = _mm(xb, dc, ta=True, out_dtype=BF16, tk=GRAD_TK, name=f"mm_gcdown_{tag}")
    return dc, w["c_w_down"][j]


def _layer_big(i, mixer=True, rest=True):
    kind, j = i % N_MIXERS, i // N_MIXERS
    own = [[("a_w_qkv", j), ("a_w_o", j)], [("b_w_in", j), ("b_w_rgate", j), ("b_w_igate", j), ("b_w_o", j)],
           [("c_w_down", j), ("c_w_uq", j), ("c_w_ukv", j), ("c_w_o", j)]][kind]
    return (own if mixer else []) + ([("x_w_q", i), ("x_w_o", i), ("f_w_up", i), ("f_w_down", i)] if rest else [])


def _local_step(x, mem, target, w, n_layers, gathers, scatter):
    S = x.shape[0]
    cos, sin = _rope_tables(S)
    grads = {n: [None] * n_layers[n] for n in WEIGHTS if n != "mem_w_kv"}
    mkv = _mm(mem, w["mem_w_kv"], out_dtype=BF16, tm=MEM_LEN, name="mm_memkv")

    xs, xb = x, x.astype(BF16)
    saved = []
    for i in range(DEPTH):
        kind, j = i % N_MIXERS, i // N_MIXERS
        tag = f"l{i}"
        if kind == 0:
            (act, w_out), res = _mixer_a_fwd(xb, w, j, cos, sin, tag, gathers)
        elif kind == 1:
            (act, w_out), res = _mixer_b_fwd(xb, w, j, tag, gathers)
        else:
            (act, w_out), res = _mixer_c_fwd(xb, w, j, cos, sin, tag)
        x1, x1b, xh1, rs1 = _mm_ln_fwd(act, w_out, xs, w["ln_g"][i, 0][None], w["ln_b"][i, 0][None], name=f"ln1_{tag}")
        q = _mm(x1b, w["x_w_q"][i], out_dtype=BF16, name=f"mm_xq_{tag}")
        o = _xattn_fwd(q, mkv, name=f"xattn_fwd_{tag}")
        x2, x2b, xh2, rs2 = _mm_ln_fwd(o, w["x_w_o"][i], x1, w["ln_g"][i, 1][None], w["ln_b"][i, 1][None], name=f"ln2_{tag}")
        w_up = w["f_w_up"][i]
        cwp, cbp = _ffn_perm(w["f_conv_w"][i]), _ffn_perm(w["f_conv_b"][i][None])
        hh = _mm(x2b, w_up, out_dtype=BF16, tn=FFN_TC, name=f"mm_up_{tag}", comm=gathers.get(f"mm_up_{tag}"))
        a = _ffn_act_fwd(hh, cwp, cbp, name=f"ffn_act_{tag}", comm=gathers.get(f"ffn_act_{tag}"))
        x3, x3b, xh3, rs3 = _mm_ln_fwd(a, w["f_w_down"][i], x2, w["ln_g"][i, 2][None], w["ln_b"][i, 2][None],
                                       name=f"ln3_{tag}", comm=gathers.get(f"ln3_{tag}"))
        saved.append((res, (xh1, rs1, x1b), (q, o, xh2, rs2, x2b), (w_up, cwp, cbp, hh, a, xh3, rs3)))
        xs, xb = x3, x3b

    dloss, loss = _loss_fwd(xs, target, name="loss")

    dmkv = None
    ln_dg = [[None] * 3 for _ in range(DEPTH)]
    ln_db = [[None] * 3 for _ in range(DEPTH)]
    for i in reversed(range(DEPTH)):
        kind, j = i % N_MIXERS, i // N_MIXERS
        tag = f"l{i}"
        res, (xh1, rs1, x1b), (q, o, xh2, rs2, x2b), (w_up, cwp, cbp, hh, a, xh3, rs3) = saved[i]
        if i == DEPTH - 1:
            dz3, dz3b, ln_dg[i][2], ln_db[i][2] = _ln_bwd(None, dloss, xh3, rs3, w["ln_g"][i, 2][None], name=f"ln3_bwd_{tag}")
        else:
            dz3, dz3b, ln_dg[i][2], ln_db[i][2] = _mm_ln_bwd(dz1, d_in, w_in, xh3, rs3, w["ln_g"][i, 2][None], name=f"ln3_bwd_{tag}")
        da = _mm(dz3b, w["f_w_down"][i], tb=True, out_dtype=BF16, name=f"mm_ddown_{tag}")
        grads["f_w_down"][i] = _mm(a, dz3b, ta=True, out_dtype=BF16, tm=FFN_TC, tk=GRAD_TK, name=f"mm_gdown_{tag}")
        later = scatter(_layer_big(i + 1), grads) if i + 1 < DEPTH else None
        dh, dcw, dcb = _ffn_act_bwd(hh, da, cwp, cbp, name=f"ffn_act_bwd_{tag}", comm=later)
        grads["f_conv_w"][i], grads["f_conv_b"][i] = _ffn_unperm(dcw), _ffn_unperm(dcb)[0]
        grads["f_w_up"][i] = _mm(x2b, dh, ta=True, out_dtype=BF16, tn=FFN_TC, tk=GRAD_TK, name=f"mm_gup_{tag}")

        dz2, dz2b, ln_dg[i][1], ln_db[i][1] = _mm_ln_bwd(dz3, dh, w_up, xh2, rs2, w["ln_g"][i, 1][None], name=f"ln2_bwd_{tag}")
        do = _mm(dz2b, w["x_w_o"][i], tb=True, out_dtype=BF16, name=f"mm_dxo_{tag}")
        grads["x_w_o"][i] = _mm(o, dz2b, ta=True, out_dtype=BF16, tk=GRAD_TK, name=f"mm_gxo_{tag}")
        dq, dmkv_i = _xattn_bwd(q, mkv, do, name=f"xattn_bwd_{tag}")
        dmkv = dmkv_i if dmkv is None else dmkv + dmkv_i
        if i == 0:
            g_mem = _mm(mem, dmkv, ta=True, out_dtype=BF16, tm=512, name="mm_gmemkv")
        grads["x_w_q"][i] = _mm(x1b, dq, ta=True, out_dtype=BF16, tk=GRAD_TK, name=f"mm_gxq_{tag}")

        dz1, dz1b, ln_dg[i][0], ln_db[i][0] = _mm_ln_bwd(dz2, dq, w["x_w_q"][i], xh1, rs1, w["ln_g"][i, 0][None], name=f"ln1_bwd_{tag}")
        if kind == 0:
            if i == 0:
                grads["mem_w_kv"] = g_mem
                done = lambda: scatter(_layer_big(0, mixer=False) + [("a_w_o", 0), ("mem_w_kv", None)], grads)
            else:
                done = lambda: None
            d_in, w_in = _mixer_a_bwd(dz1b, res, w, j, cos, sin, tag, grads, done)
        elif kind == 1:
            d_in, w_in = _mixer_b_bwd(dz1b, res, w, j, tag, grads)
        else:
            d_in, w_in = _mixer_c_bwd(dz1b, res, w, j, cos, sin, tag, grads)

    grad_x = _axpy(dz1, _mm(d_in, w_in, tb=True, name="mm_dx_l0"), name="grad_x")
    big = [n for n, _ in SHARDED[:N_BIG]]
    out = {n: (g if n in big else jnp.stack(g, axis=0)) for n, g in grads.items() if n not in ("ln_g", "ln_b")}
    out["ln_g"] = jnp.stack([jnp.concatenate(r, axis=0) for r in ln_dg], axis=0)
    out["ln_b"] = jnp.stack([jnp.concatenate(r, axis=0) for r in ln_db], axis=0)
    return loss, grad_x, out


def kernel(x, mem, a_w_qkv, a_sinks, a_w_o, b_w_in, b_conv_w, b_conv_b, b_w_rgate, b_b_rgate, b_w_igate, b_b_igate, b_lambda, b_w_o, c_w_down, c_q_norm, c_kv_norm, c_w_uq, c_w_ukv, c_w_o, mem_w_kv, x_w_q, x_w_o, f_w_up, f_conv_w, f_conv_b, f_w_down, ln_g, ln_b, loss_target, m_a_w_qkv, m_a_sinks, m_a_w_o, m_b_w_in, m_b_conv_w, m_b_conv_b, m_b_w_rgate, m_b_b_rgate, m_b_w_igate, m_b_b_igate, m_b_lambda, m_b_w_o, m_c_w_down, m_c_q_norm, m_c_kv_norm, m_c_w_uq, m_c_w_ukv, m_c_w_o, m_mem_w_kv, m_x_w_q, m_x_w_o, m_f_w_up, m_f_conv_w, m_f_conv_b, m_f_w_down, m_ln_g, m_ln_b, v_a_w_qkv, v_a_sinks, v_a_w_o, v_b_w_in, v_b_conv_w, v_b_conv_b, v_b_w_rgate, v_b_b_rgate, v_b_w_igate, v_b_b_igate, v_b_lambda, v_b_w_o, v_c_w_down, v_c_q_norm, v_c_kv_norm, v_c_w_uq, v_c_w_ukv, v_c_w_o, v_mem_w_kv, v_x_w_q, v_x_w_o, v_f_w_up, v_f_conv_w, v_f_conv_b, v_f_w_down, v_ln_g, v_ln_b):
    loc = locals()
    shard = {n: loc[n] for n in WEIGHTS}
    mom = {n: loc["m_" + n] for n in WEIGHTS}
    var = {n: loc["v_" + n] for n in WEIGHTS}
    names = [n for n, _ in SHARDED]
    axis = dict(SHARDED)
    big, small = names[:N_BIG], names[N_BIG:]

    chip = 2 * lax.axis_index("x") + lax.axis_index("y")
    n_layers = {n: shard[n].shape[0] for n in WEIGHTS if n != "mem_w_kv"}
    layer_axis = lambda n, j: axis[n] - (0 if j is None else 1)

    shard_b = {n: shard[n].astype(BF16) for n in big}
    w = {n: [None] * n_layers[n] for n in big if n != "mem_w_kv"}

    small_pack = _pack([shard[n] for n in small], F32)
    shard_b["small"] = small_pack

    def gather(pairs):
        def deliver(outs):
            for (n, j), o in zip(pairs, outs):
                if j is None:
                    w[n] = o
                else:
                    w[n][j] = o
        return _gather_comm([(shard_b[n], j, 0 if n == "small" else layer_axis(n, j), n == "f_w_up") for n, j in pairs], deliver)

    _run_comm(gather([("a_w_qkv", 0), ("a_w_o", 0), ("mem_w_kv", None), ("small", None)]), name="gather_first")
    gathers = {
        "rope_qkv_l0": gather([("x_w_q", 0), ("x_w_o", 0)]),
        "swa_fwd_l0": gather([("f_w_up", 0), ("f_w_down", 0)]),
        "mm_up_l0": gather(_layer_big(1, rest=False) + [("x_w_q", 1), ("x_w_o", 1)]),
        "ffn_act_l0": gather([("f_w_up", 1)]),
        "ln3_l0": gather([("f_w_down", 1)]),
        "lru_fwd_l1": gather(_layer_big(2, rest=False) + [("x_w_q", 2), ("x_w_o", 2)]),
        "mm_up_l1": gather([("f_w_up", 2)]),
        "ffn_act_l1": gather([("f_w_down", 2)] + _layer_big(3, rest=False)),
        "mm_up_l2": gather([("f_w_up", 3)]),
        "ffn_act_l2": gather([("f_w_down", 3), ("x_w_q", 3), ("x_w_o", 3)]),
    }
    got = w.pop("small").reshape((4,) + small_pack.shape)
    per_chip = [_unpack(got[s], [shard[n].shape for n in small]) for s in range(4)]
    for k, n in enumerate(small):
        w[n] = jnp.concatenate([per_chip[s][k] for s in range(4)], axis=axis[n])
    for n in REPLICATED:
        w[n] = shard[n]

    recv = {}

    def scatter(pairs, grads):
        def deliver(outs):
            recv.update(dict(zip(pairs, outs)))
        return _scatter_comm([(grads[n] if j is None else grads[n][j], layer_axis(n, j), n == "f_w_up") for n, j in pairs], deliver)

    loss, grad_x, g = _local_step(x[0], mem[0], loss_target[0], w, n_layers, gathers, scatter)
    _run_comm(scatter([("a_w_qkv", 0)], g), name="scatter_last")

    view = {n: (int(np.prod(shard[n].shape[:-1])), shard[n].shape[-1]) for n in big}
    parts = []
    for n in big:
        layers = n_layers.get(n, 1)
        part = None
        for j in range(layers):
            r = recv[(n, j if n in n_layers else None)]
            part = _sum_partials(r.reshape(4, view[n][0] // layers, view[n][1]), part, j, layers, name=f"sum_{n}_{j}")
        parts.append(part)
    sibs = _swap_cores(parts, name="swap_cores")
    grad_o, delta_o, m_o, v_o = {}, {}, {}, {}
    for n, part, sib in zip(big, parts, sibs):
        res = _adamw(shard[n].reshape(view[n]), mom[n].reshape(view[n]), var[n].reshape(view[n]), part, sib, name=f"adamw_{n}")
        for d, r in zip((grad_o, delta_o, m_o, v_o), res):
            d[n] = r.reshape(shard[n].shape)

    rest = small + REPLICATED
    vec = _pack([g[n] for n in rest] + [loss], F32, cols=LANES, row_mult=8)
    tot = _unpack(_all_reduce_small(vec, name="allreduce_small"), [g[n].shape for n in rest] + [(1, 1)], cols=LANES)
    loss_tot = tot[-1].reshape(())
    mine = {n: t for n, t in zip(rest, tot)}
    for n in small:
        size = shard[n].shape[axis[n]]
        mine[n] = lax.dynamic_slice_in_dim(mine[n], chip * size, size, axis=axis[n])
    rpack = lambda d: _pack([d[n] for n in rest], F32, cols=LANES, row_mult=8)
    res = _adamw(rpack(shard), rpack(mom), rpack(var), rpack(mine), None, name="adamw_small")
    for d, r in zip((grad_o, delta_o, m_o, v_o), res):
        d.update(dict(zip(rest, _unpack(r, [shard[n].shape for n in rest], cols=LANES))))

    return (loss_tot, grad_x[None], *[grad_o[n] for n in WEIGHTS], *[delta_o[n] for n in WEIGHTS],
            *[m_o[n] for n in WEIGHTS], *[v_o[n] for n in WEIGHTS])
```

```python
import functools
import math

import numpy as np
import jax
import jax.numpy as jnp
from jax import lax
from jax.experimental import pallas as pl
from jax.experimental.pallas import tpu as pltpu

F32 = jnp.float32
BF16 = jnp.bfloat16
MESH = pl.DeviceIdType.MESH

D_MODEL = 1024
DEPTH = 4
N_MIXERS = 3
MEM_LEN = 256
BLOCK = 128
ROPE_THETA = 10000.0
NEG = -1e30
LN_EPS = 1e-5
RMS_EPS = 1e-6
A_HEADS, A_KV_HEADS, A_HEAD_DIM = 16, 4, 64
LRU_BLOCKS, LRU_BLOCK_W, LRU_CONV, LRU_C = 4, 256, 4, 8.0
C_HEADS, C_NOPE, C_ROPE, C_V, C_Q_RANK, C_KV_RANK = 8, 128, 64, 128, 384, 256
X_HEADS, X_HEAD_DIM = 4, 256
D_FF, FFN_CONV = 2816, 3
ALPHA = (2.0 * DEPTH) ** 0.25
ADAM_LR, ADAM_B1, ADAM_B2, ADAM_EPS, ADAM_WD, ADAM_STEP = 0.001, 0.9, 0.999, 1e-08, 0.01, 10

VMEM_LIMIT = 56 * 2 ** 20
LANES = 128
PACK_COLS = 1024
ROW_T = 512
ACT_T = 256
LRU_T = 256
FLASH_T = 512
FFN_TC = 1408
MM_T = 1024
GRAD_TK = 2048

C11 = (((1,), (1,)), ((), ()))
C00 = (((0,), (0,)), ((), ()))

SHARDED = [
    ("a_w_qkv", 2), ("a_w_o", 1), ("b_w_in", 2), ("b_w_rgate", 2), ("b_w_igate", 2), ("b_w_o", 1), ("c_w_down", 1),
    ("c_w_uq", 2), ("c_w_ukv", 2), ("c_w_o", 1), ("mem_w_kv", 1), ("x_w_q", 1), ("x_w_o", 1), ("f_w_up", 2),
    ("f_w_down", 1),
    ("b_conv_w", 2), ("c_q_norm", 1), ("c_kv_norm", 1), ("f_conv_w", 2), ("ln_g", 2), ("ln_b", 2),
]
N_BIG = 15
REPLICATED = ["a_sinks", "b_conv_b", "b_b_rgate", "b_b_igate", "b_lambda", "f_conv_b"]
WEIGHTS = ["a_w_qkv", "a_sinks", "a_w_o", "b_w_in", "b_conv_w", "b_conv_b", "b_w_rgate", "b_b_rgate", "b_w_igate",
           "b_b_igate", "b_lambda", "b_w_o", "c_w_down", "c_q_norm", "c_kv_norm", "c_w_uq", "c_w_ukv", "c_w_o",
           "mem_w_kv", "x_w_q", "x_w_o", "f_w_up", "f_conv_w", "f_conv_b", "f_w_down", "ln_g", "ln_b"]


def _params(*sem):
    return pltpu.CompilerParams(dimension_semantics=sem, vmem_limit_bytes=VMEM_LIMIT)


def _sds(shape, dtype):
    return jax.ShapeDtypeStruct(tuple(shape), dtype)


def _mm(a, b, *, name, ta=False, tb=False, out_dtype=F32, tm=None, tn=None, tk=None, comm=None):
    (K, M) = a.shape if ta else a.shape[::-1]
    (N, K2) = b.shape if tb else b.shape[::-1]
    assert K == K2, (a.shape, b.shape, ta, tb)
    tm = min(tm or MM_T, M)
    tn = min(tn or N, N)
    tk = min(tk or K, K)
    assert M % tm == 0 and N % tn == 0 and K % tk == 0, (M, N, K, tm, tn, tk)
    nk = K // tk
    use_acc = nk > 1 and out_dtype != F32
    dims = (((0 if ta else 1,), (1 if tb else 0,)), ((), ()))

    def body(a_ref, b_ref, o_ref, *scratch):
        p = lax.dot_general(a_ref[...].astype(BF16), b_ref[...].astype(BF16), dims, preferred_element_type=F32)
        if nk == 1:
            o_ref[...] = p.astype(out_dtype)
        else:
            acc = scratch[0] if use_acc else o_ref
            k = pl.program_id(2)

            @pl.when(k == 0)
            def _():
                acc[...] = p

            @pl.when(k > 0)
            def _():
                acc[...] += p

            if use_acc:
                @pl.when(k == nk - 1)
                def _():
                    o_ref[...] = acc[...].astype(out_dtype)

    a_spec = pl.BlockSpec((tk, tm), lambda i, j, k: (k, i)) if ta else pl.BlockSpec((tm, tk), lambda i, j, k: (i, k))
    b_spec = pl.BlockSpec((tn, tk), lambda i, j, k: (j, k)) if tb else pl.BlockSpec((tk, tn), lambda i, j, k: (k, j))
    return _call(
        body, name=name, grid=(M // tm, N // tn, nk), in_specs=[a_spec, b_spec],
        out_specs=[pl.BlockSpec((tm, tn), lambda i, j, k: (i, j))], out_shape=[_sds((M, N), out_dtype)],
        scratch_shapes=[pltpu.VMEM((tm, tn), F32)] if use_acc else [],
        params=_params("parallel", "parallel", "arbitrary"), comm=comm,
    )(a, b)[0]


def _shift_down(cur, prev8, d):
    rolled = pltpu.roll(cur, d, 0)
    rid = lax.broadcasted_iota(jnp.int32, prev8.shape, 0)
    head = jnp.where(rid < d, pltpu.roll(prev8, d, 0), rolled[0:8])
    return jnp.concatenate([head, rolled[8:]], axis=0)


def _shift_up(cur, next8, d):
    n = cur.shape[0]
    rolled = pltpu.roll(cur, n - d, 0)
    rid = lax.broadcasted_iota(jnp.int32, next8.shape, 0)
    tail = jnp.where(rid >= 8 - d, pltpu.roll(next8, 8 - d, 0), rolled[n - 8:n])
    return jnp.concatenate([rolled[0:n - 8], tail], axis=0)


def _swap_halves(x):
    w = x.shape[-1]
    if w == 64:
        return jnp.concatenate([x[:, 32:64], x[:, 0:32]], axis=1)
    lane = lax.broadcasted_iota(jnp.int32, x.shape, 1)
    return jnp.where((lane % 64) < 32, pltpu.roll(x, w - 32, 1), pltpu.roll(x, 32, 1))


def _tile_lanes(t, w):
    return t if w == t.shape[-1] else jnp.concatenate([t] * (w // t.shape[-1]), axis=1)


def _rope(x, cos, sin):
    w = x.shape[-1]
    if w == 64:
        cos, sin = cos[:, :64], sin[:, :64]
    else:
        cos, sin = _tile_lanes(cos, w), _tile_lanes(sin, w)
    return x * cos + _swap_halves(x) * sin


def _rope_t(x, cos, sin):
    w = x.shape[-1]
    if w == 64:
        cos, sin = cos[:, :64], sin[:, :64]
    else:
        cos, sin = _tile_lanes(cos, w), _tile_lanes(sin, w)
    return x * cos - _swap_halves(x) * sin


def _sigmoid(x):
    return 1.0 / (1.0 + jnp.exp(-x))


def _gelu_and_grad(x):
    c0, c1 = math.sqrt(2.0 / math.pi), 0.044715
    t = jnp.tanh(c0 * (x + c1 * x * x * x))
    g = 0.5 * x * (1.0 + t)
    dg = 0.5 * (1.0 + t) + 0.5 * x * (1.0 - t * t) * c0 * (1.0 + 3.0 * c1 * x * x)
    return g, dg


def _neg_expm1(x):
    series = -x * (1.0 + x * (0.5 + x * (1.0 / 6.0 + x * (1.0 / 24.0 + x * (1.0 / 120.0)))))
    return jnp.where(x > -0.1, series, 1.0 - jnp.exp(x))


def _softplus_neg(lam):
    z = -lam
    e = jnp.exp(-jnp.abs(z))
    log1p = jnp.where(e < 0.01, e * (1.0 - e * (0.5 - e * (1.0 / 3.0))), jnp.log(1.0 + e))
    sp = jnp.maximum(z, 0.0) + log1p
    dsp = -_sigmoid(z)
    return sp, dsp


def _ln_fwd(x, y, g, b, *, name):
    S, D = x.shape
    tm = min(ROW_T, S)

    def body(x_ref, y_ref, g_ref, b_ref, o_ref, ob_ref, xh_ref, rs_ref):
        z = ALPHA * x_ref[...] + y_ref[...]
        mu = jnp.mean(z, axis=-1, keepdims=True)
        zc = z - mu
        var = jnp.mean(zc * zc, axis=-1, keepdims=True)
        r = lax.rsqrt(var + LN_EPS)
        xh = zc * r
        o = xh * g_ref[...] + b_ref[...]
        o_ref[...] = o
        ob_ref[...] = o.astype(BF16)
        xh_ref[...] = xh
        rs_ref[...] = r

    row = pl.BlockSpec((tm, D), lambda i: (i, 0))
    vec = pl.BlockSpec((1, D), lambda i: (0, 0))
    return pl.pallas_call(
        body, name=name, grid=(S // tm,), in_specs=[row, row, vec, vec],
        out_specs=[row, row, row, pl.BlockSpec((tm, 1), lambda i: (i, 0))],
        out_shape=[_sds((S, D), F32), _sds((S, D), BF16), _sds((S, D), F32), _sds((S, 1), F32)],
        compiler_params=_params("parallel"),
    )(x, y, g, b)


def _ln_bwd(d1, d2, xh, rs, g, *, name):
    S, D = xh.shape
    tm = min(ROW_T, S)
    has_d1 = d1 is not None

    def body(*refs):
        if has_d1:
            d1_ref, d2_ref, xh_ref, rs_ref, g_ref, dz_ref, dzb_ref, dg_ref, db_ref = refs
            dout = ALPHA * d1_ref[...] + d2_ref[...]
        else:
            d2_ref, xh_ref, rs_ref, g_ref, dz_ref, dzb_ref, dg_ref, db_ref = refs
            dout = d2_ref[...]
        xh_v = xh_ref[...]
        dxh = dout * g_ref[...]
        m1 = jnp.mean(dxh, axis=-1, keepdims=True)
        m2 = jnp.mean(dxh * xh_v, axis=-1, keepdims=True)
        dz = rs_ref[...] * (dxh - m1 - xh_v * m2)
        dz_ref[...] = dz
        dzb_ref[...] = dz.astype(BF16)

        @pl.when(pl.program_id(0) == 0)
        def _():
            dg_ref[...] = jnp.zeros_like(dg_ref)
            db_ref[...] = jnp.zeros_like(db_ref)

        dg_ref[...] += jnp.sum(dout * xh_v, axis=0, keepdims=True)
        db_ref[...] += jnp.sum(dout, axis=0, keepdims=True)

    row = pl.BlockSpec((tm, D), lambda i: (i, 0))
    vec = pl.BlockSpec((1, D), lambda i: (0, 0))
    ins = ([row] if has_d1 else []) + [row, row, pl.BlockSpec((tm, 1), lambda i: (i, 0)), vec]
    args = ([d1] if has_d1 else []) + [d2, xh, rs, g]
    return pl.pallas_call(
        body, name=name, grid=(S // tm,), in_specs=ins, out_specs=[row, row, vec, vec],
        out_shape=[_sds((S, D), F32), _sds((S, D), BF16), _sds((1, D), F32), _sds((1, D), F32)],
        compiler_params=_params("arbitrary"),
    )(*args)


def _mm_ln_fwd(a, b, x, g, beta, *, name, comm=None):
    S, K = a.shape
    D = b.shape[1]
    tm = min(ROW_T, S)

    def body(a_ref, b_ref, x_ref, g_ref, beta_ref, o_ref, ob_ref, xh_ref, rs_ref):
        y = jnp.dot(a_ref[...].astype(BF16), b_ref[...].astype(BF16), preferred_element_type=F32)
        z = ALPHA * x_ref[...] + y
        mu = jnp.mean(z, axis=-1, keepdims=True)
        zc = z - mu
        var = jnp.mean(zc * zc, axis=-1, keepdims=True)
        r = lax.rsqrt(var + LN_EPS)
        xh = zc * r
        o = xh * g_ref[...] + beta_ref[...]
        o_ref[...] = o
        ob_ref[...] = o.astype(BF16)
        xh_ref[...] = xh
        rs_ref[...] = r

    row = pl.BlockSpec((tm, D), lambda i: (i, 0))
    vec = pl.BlockSpec((1, D), lambda i: (0, 0))
    return _call(
        body, name=name, grid=(S // tm,),
        in_specs=[pl.BlockSpec((tm, K), lambda i: (i, 0)), pl.BlockSpec((K, D), lambda i: (0, 0)), row, vec, vec],
        out_specs=[row, row, row, pl.BlockSpec((tm, 1), lambda i: (i, 0))],
        out_shape=[_sds((S, D), F32), _sds((S, D), BF16), _sds((S, D), F32), _sds((S, 1), F32)],
        params=_params("arbitrary"), comm=comm,
    )(a, b, x, g, beta)


def _mm_ln_bwd(d1, da, wt, xh, rs, g, *, name):
    S, D = xh.shape
    K = da.shape[1]
    tm = min(ROW_T // 2, S)

    def body(d1_ref, da_ref, wt_ref, xh_ref, rs_ref, g_ref, dz_ref, dzb_ref, dg_ref, db_ref):
        d2 = lax.dot_general(da_ref[...].astype(BF16), wt_ref[...].astype(BF16), C11, preferred_element_type=F32)
        dout = ALPHA * d1_ref[...] + d2
        xh_v = xh_ref[...]
        dxh = dout * g_ref[...]
        m1 = jnp.mean(dxh, axis=-1, keepdims=True)
        m2 = jnp.mean(dxh * xh_v, axis=-1, keepdims=True)
        dz = rs_ref[...] * (dxh - m1 - xh_v * m2)
        dz_ref[...] = dz
        dzb_ref[...] = dz.astype(BF16)

        @pl.when(pl.program_id(0) == 0)
        def _():
            dg_ref[...] = jnp.zeros_like(dg_ref)
            db_ref[...] = jnp.zeros_like(db_ref)

        dg_ref[...] += jnp.sum(dout * xh_v, axis=0, keepdims=True)
        db_ref[...] += jnp.sum(dout, axis=0, keepdims=True)

    row = pl.BlockSpec((tm, D), lambda i: (i, 0))
    vec = pl.BlockSpec((1, D), lambda i: (0, 0))
    return pl.pallas_call(
        body, name=name, grid=(S // tm,),
        in_specs=[row, pl.BlockSpec((tm, K), lambda i: (i, 0)), pl.BlockSpec((D, K), lambda i: (0, 0)), row,
                  pl.BlockSpec((tm, 1), lambda i: (i, 0)), vec],
        out_specs=[row, row, vec, vec],
        out_shape=[_sds((S, D), F32), _sds((S, D), BF16), _sds((1, D), F32), _sds((1, D), F32)],
        compiler_params=_params("arbitrary"),
    )(d1, da, wt, xh, rs, g)


def _loss_fwd(y, target, *, name):
    S, D = y.shape
    tm = min(ROW_T, S)

    def body(y_ref, t_ref, d_ref, l_ref):
        e = y_ref[...] - t_ref[...]
        d_ref[...] = e * (1.0 / D)

        @pl.when(pl.program_id(0) == 0)
        def _():
            l_ref[...] = jnp.zeros_like(l_ref)

        part = jnp.sum(e * e, axis=0, keepdims=True)
        l_ref[...] += (0.5 / D) * jnp.sum(part, axis=1, keepdims=True)

    row = pl.BlockSpec((tm, D), lambda i: (i, 0))
    return pl.pallas_call(
        body, name=name, grid=(S // tm,), in_specs=[row, row],
        out_specs=[row, pl.BlockSpec((1, 1), lambda i: (0, 0))], out_shape=[_sds((S, D), F32), _sds((1, 1), F32)],
        compiler_params=_params("arbitrary"),
    )(y, target)


def _axpy(d1, d2, *, name):
    S, D = d1.shape
    tm = min(ROW_T, S)

    def body(a_ref, b_ref, o_ref):
        o_ref[...] = ALPHA * a_ref[...] + b_ref[...]

    row = pl.BlockSpec((tm, D), lambda i: (i, 0))
    return pl.pallas_call(body, name=name, grid=(S // tm,), in_specs=[row, row], out_specs=row,
                          out_shape=_sds((S, D), F32), compiler_params=_params("parallel"))(d1, d2)


def _ffn_act_fwd(h, cw, cb, *, name, comm=None):
    S, W = h.shape
    tc = FFN_TC
    nj = W // (2 * tc)
    tm = min(ACT_T, S)

    def body(h_ref, w_ref, b_ref, a_ref, carry):
        @pl.when(pl.program_id(1) == 0)
        def _():
            carry[...] = jnp.zeros_like(carry)

        cur = h_ref[...].astype(F32)
        prev8 = carry[...]
        hc = cur * w_ref[2:3, :] + _shift_down(cur, prev8, 1) * w_ref[1:2, :] + _shift_down(cur, prev8, 2) * w_ref[0:1, :]
        hc = hc + b_ref[...]
        carry[...] = cur[tm - 8:tm]
        hg, hu = hc[:, :tc], hc[:, tc:]
        a_ref[...] = (hg * _sigmoid(hg) * hu).astype(BF16)

    return _call(
        body, name=name, grid=(nj, S // tm),
        in_specs=[pl.BlockSpec((tm, 2 * tc), lambda j, i: (i, j)), pl.BlockSpec((3, 2 * tc), lambda j, i: (0, j)),
                  pl.BlockSpec((1, 2 * tc), lambda j, i: (0, j))],
        out_specs=[pl.BlockSpec((tm, tc), lambda j, i: (i, j))], out_shape=[_sds((S, W // 2), BF16)],
        scratch_shapes=[pltpu.VMEM((8, 2 * tc), F32)],
        params=_params("parallel", "arbitrary"), comm=comm,
    )(h, cw, cb)[0]


def _ffn_act_bwd(h, da, cw, cb, *, name, comm=None):
    S, W = h.shape
    tc = FFN_TC
    nj = W // (2 * tc)
    tm = min(ACT_T, S)
    ni = S // tm

    def body(h_ref, hp_ref, da_ref, w_ref, b_ref, dh_ref, dw_ref, db_ref, carry):
        i = pl.program_id(1)
        r = ni - 1 - i

        @pl.when(i == 0)
        def _():
            carry[...] = jnp.zeros_like(carry)
            dw_ref[...] = jnp.zeros_like(dw_ref)
            db_ref[...] = jnp.zeros_like(db_ref)

        cur = h_ref[...].astype(F32)
        prev8 = jnp.where(r > 0, hp_ref[8:16, :].astype(F32), 0.0)
        sh = [cur, _shift_down(cur, prev8, 1), _shift_down(cur, prev8, 2)]
        hc = sh[0] * w_ref[2:3, :] + sh[1] * w_ref[1:2, :] + sh[2] * w_ref[0:1, :] + b_ref[...]
        hg, hu = hc[:, :tc], hc[:, tc:]
        d = da_ref[...].astype(F32)
        sg = _sigmoid(hg)
        dg = d * hu * (sg * (1.0 + hg * (1.0 - sg)))
        du = d * (hg * sg)
        dhc = jnp.concatenate([dg, du], axis=1)
        db_ref[...] += jnp.sum(dhc, axis=0, keepdims=True)
        for k in range(3):
            dw_ref[k:k + 1, :] += jnp.sum(dhc * sh[2 - k], axis=0, keepdims=True)
        next8 = carry[...]
        dh = dhc * w_ref[2:3, :] + _shift_up(dhc, next8, 1) * w_ref[1:2, :] + _shift_up(dhc, next8, 2) * w_ref[0:1, :]
        carry[...] = dhc[0:8]
        dh_ref[...] = dh.astype(BF16)

    rev = lambda j, i: (ni - 1 - i, j)
    return _call(
        body, name=name, grid=(nj, ni),
        in_specs=[pl.BlockSpec((tm, 2 * tc), rev),
                  pl.BlockSpec((16, 2 * tc), lambda j, i: (jnp.maximum((ni - 1 - i) * (tm // 16) - 1, 0), j)),
                  pl.BlockSpec((tm, tc), rev), pl.BlockSpec((3, 2 * tc), lambda j, i: (0, j)),
                  pl.BlockSpec((1, 2 * tc), lambda j, i: (0, j))],
        out_specs=[pl.BlockSpec((tm, 2 * tc), rev), pl.BlockSpec((3, 2 * tc), lambda j, i: (0, j)),
                   pl.BlockSpec((1, 2 * tc), lambda j, i: (0, j))],
        out_shape=[_sds((S, W), BF16), _sds((3, W), F32), _sds((1, W), F32)],
        scratch_shapes=[pltpu.VMEM((8, 2 * tc), F32)],
        params=_params("parallel", "arbitrary"), comm=comm,
    )(h, h, da, cw, cb)


def _xattn_softmax(qk):
    s = qk * (X_HEAD_DIM ** -0.5)
    p = jnp.exp(s - jnp.max(s, axis=-1, keepdims=True))
    return p / jnp.sum(p, axis=-1, keepdims=True)


def _xattn_fwd(q, mkv, *, name):
    S, D = q.shape
    tm = min(ROW_T, S)

    def body(q_ref, k_ref, v_ref, o_ref):
        sls = [slice(h * X_HEAD_DIM, (h + 1) * X_HEAD_DIM) for h in range(X_HEADS)]
        scores = lambda h: lax.dot_general(q_ref[:, sls[h]], k_ref[:, sls[h]], C11, preferred_element_type=F32)
        nxt = scores(0)
        for h in range(X_HEADS):
            qk = nxt
            if h + 1 < X_HEADS:
                nxt = scores(h + 1)
            p = _xattn_softmax(qk)
            o_ref[:, sls[h]] = jnp.dot(p.astype(BF16), v_ref[:, sls[h]], preferred_element_type=F32).astype(BF16)

    return pl.pallas_call(
        body, name=name, grid=(S // tm,),
        in_specs=[pl.BlockSpec((tm, D), lambda i: (i, 0)), pl.BlockSpec((MEM_LEN, D), lambda i: (0, 0)),
                  pl.BlockSpec((MEM_LEN, D), lambda i: (0, 1))],
        out_specs=pl.BlockSpec((tm, D), lambda i: (i, 0)), out_shape=_sds((S, D), BF16),
        compiler_params=_params("parallel"),
    )(q, mkv, mkv)


def _xattn_bwd(q, mkv, do, *, name):
    S, D = q.shape
    tm = min(ROW_T, S)
    scale = X_HEAD_DIM ** -0.5

    def body(q_ref, k_ref, v_ref, do_ref, dq_ref, dkv_ref):
        @pl.when(pl.program_id(0) == 0)
        def _():
            dkv_ref[...] = jnp.zeros_like(dkv_ref)

        def products(h):
            sl = slice(h * X_HEAD_DIM, (h + 1) * X_HEAD_DIM)
            return (lax.dot_general(q_ref[:, sl], k_ref[:, sl], C11, preferred_element_type=F32),
                    lax.dot_general(do_ref[:, sl], v_ref[:, sl], C11, preferred_element_type=F32))

        nxt = products(0)
        for h in range(X_HEADS):
            qk, dp = nxt
            if h + 1 < X_HEADS:
                nxt = products(h + 1)
            sl = slice(h * X_HEAD_DIM, (h + 1) * X_HEAD_DIM)
            sv = slice(D + h * X_HEAD_DIM, D + (h + 1) * X_HEAD_DIM)
            qh, kh, doh = q_ref[:, sl], k_ref[:, sl], do_ref[:, sl]
            p = _xattn_softmax(qk)
            ds = (p * (dp - jnp.sum(p * dp, axis=-1, keepdims=True)) * scale).astype(BF16)
            dq_ref[:, sl] = jnp.dot(ds, kh, preferred_element_type=F32).astype(BF16)
            dkv_ref[:, sl] += lax.dot_general(ds, qh, C00, preferred_element_type=F32)
            dkv_ref[:, sv] += lax.dot_general(p.astype(BF16), doh, C00, preferred_element_type=F32)

    row = pl.BlockSpec((tm, D), lambda i: (i, 0))
    return pl.pallas_call(
        body, name=name, grid=(S // tm,),
        in_specs=[row, pl.BlockSpec((MEM_LEN, D), lambda i: (0, 0)), pl.BlockSpec((MEM_LEN, D), lambda i: (0, 1)), row],
        out_specs=[row, pl.BlockSpec((MEM_LEN, 2 * D), lambda i: (0, 0))],
        out_shape=[_sds((S, D), BF16), _sds((MEM_LEN, 2 * D), F32)],
        compiler_params=_params("arbitrary"),
    )(q, mkv, mkv, do)


def _rope_cols(x, cos, sin, n_rope, *, name, comm=None):
    S, W = x.shape
    tm = min(ROW_T, S)

    def body(x_ref, c_ref, s_ref, o_ref):
        o_ref[:, :n_rope] = _rope(x_ref[:, :n_rope], c_ref[...], s_ref[...]).astype(BF16)
        if n_rope < W:
            o_ref[:, n_rope:] = x_ref[:, n_rope:].astype(BF16)

    row = pl.BlockSpec((tm, W), lambda i: (i, 0))
    tab = pl.BlockSpec((tm, LANES), lambda i: (i, 0))
    return _call(body, name=name, grid=(S // tm,), in_specs=[row, tab, tab], out_specs=[row],
                 out_shape=[_sds((S, W), BF16)], params=_params("arbitrary"), comm=comm)(x, cos, sin)[0]


def _swa_band(n, stacked):
    qi = jnp.bitwise_and(lax.broadcasted_iota(jnp.int32, (stacked * BLOCK, 2 * BLOCK), 0), BLOCK - 1)
    kj = lax.broadcasted_iota(jnp.int32, (stacked * BLOCK, 2 * BLOCK), 1)
    first = jnp.where(n > 0, 0, BLOCK)
    return ((kj < BLOCK) & (kj > qi + first)) | ((kj >= BLOCK) & (kj - BLOCK <= qi))


def _swa_sink_rows(sink_ref, heads):
    row = lax.broadcasted_iota(jnp.int32, (len(heads) * BLOCK, 1), 0)
    col = jnp.full(row.shape, sink_ref[heads[-1]], F32)
    for gi in range(len(heads) - 2, -1, -1):
        col = jnp.where(row < (gi + 1) * BLOCK, sink_ref[heads[gi]], col)
    return col


def _swa_softmax(qk, band, sink):
    s = jnp.where(band, qk * (A_HEAD_DIM ** -0.5), NEG)
    m = jnp.maximum(jnp.max(s, axis=-1, keepdims=True), sink)
    p = jnp.exp(s - m)
    e_sink = jnp.exp(sink - m)
    den = jnp.sum(p, axis=-1, keepdims=True) + e_sink
    return p / den, e_sink / den


def _swa_specs():
    nq, nkv = A_HEADS * A_HEAD_DIM, A_KV_HEADS * A_HEAD_DIM
    kb, vb = nq // nkv, nq // nkv + 1
    prev = lambda n: jnp.maximum(n - 1, 0)
    return [pl.BlockSpec((BLOCK, nq), lambda n: (n, 0)),
            pl.BlockSpec((BLOCK, nkv), lambda n: (n, kb)), pl.BlockSpec((BLOCK, nkv), lambda n: (prev(n), kb)),
            pl.BlockSpec((BLOCK, nkv), lambda n: (n, vb)), pl.BlockSpec((BLOCK, nkv), lambda n: (prev(n), vb)),
            pl.BlockSpec(memory_space=pltpu.SMEM)]


def _swa_fwd(qkv, sinks, *, name, comm=None):
    S = qkv.shape[0]
    hd, grp = A_HEAD_DIM, A_HEADS // A_KV_HEADS

    def body(q_ref, kc_ref, kp_ref, vc_ref, vp_ref, sink_ref, o_ref):
        band = _swa_band(pl.program_id(0), grp)
        qa, kc, kp, vc, vp = q_ref[...], kc_ref[...], kp_ref[...], vc_ref[...], vp_ref[...]
        def products(hk):
            ks = slice(hk * hd, (hk + 1) * hd)
            k = jnp.concatenate([kp[:, ks], kc[:, ks]], axis=0)
            v = jnp.concatenate([vp[:, ks], vc[:, ks]], axis=0)
            q = jnp.concatenate([qa[:, h * hd:(h + 1) * hd] for h in range(hk * grp, (hk + 1) * grp)], axis=0)
            return v, lax.dot_general(q, k, C11, preferred_element_type=F32)

        nxt = products(0)
        for hk in range(A_KV_HEADS):
            v, s = nxt
            if hk + 1 < A_KV_HEADS:
                nxt = products(hk + 1)
            heads = [hk * grp + gi for gi in range(grp)]
            p, _ = _swa_softmax(s, band, _swa_sink_rows(sink_ref, heads))
            o = jnp.dot(p.astype(BF16), v, preferred_element_type=F32).astype(BF16)
            for gi, h in enumerate(heads):
                o_ref[:, h * hd:(h + 1) * hd] = o[gi * BLOCK:(gi + 1) * BLOCK]

    return _call(
        body, name=name, grid=(S // BLOCK,), in_specs=_swa_specs(),
        out_specs=[pl.BlockSpec((BLOCK, A_HEADS * hd), lambda n: (n, 0))], out_shape=[_sds((S, A_HEADS * hd), BF16)],
        params=_params("arbitrary"), comm=comm,
    )(qkv, qkv, qkv, qkv, qkv, sinks)[0]


def _swa_bwd(qkv, sinks, do, cos, sin, *, name, comm=None):
    S = qkv.shape[0]
    hd, grp = A_HEAD_DIM, A_HEADS // A_KV_HEADS
    nq, nkv = A_HEADS * hd, A_KV_HEADS * hd
    scale = hd ** -0.5

    def body(q_ref, kc_ref, kp_ref, vc_ref, vp_ref, sink_ref, do_ref, c_ref, s_ref, dq_ref, dc_ref, dp_ref, ds_ref, dq_s):
        @pl.when(pl.program_id(0) == 0)
        def _():
            ds_ref[...] = jnp.zeros_like(ds_ref)

        band = _swa_band(pl.program_id(0), grp)
        lane = lax.broadcasted_iota(jnp.int32, (1, LANES), 1)
        qa, kc, kp, vc, vp, doa = q_ref[...], kc_ref[...], kp_ref[...], vc_ref[...], vp_ref[...], do_ref[...]
        dsink = jnp.zeros((1, LANES), F32)
        def products(hk):
            ks = slice(hk * hd, (hk + 1) * hd)
            k = jnp.concatenate([kp[:, ks], kc[:, ks]], axis=0)
            v = jnp.concatenate([vp[:, ks], vc[:, ks]], axis=0)
            q = jnp.concatenate([qa[:, h * hd:(h + 1) * hd] for h in range(hk * grp, (hk + 1) * grp)], axis=0)
            dog = jnp.concatenate([doa[:, h * hd:(h + 1) * hd] for h in range(hk * grp, (hk + 1) * grp)], axis=0)
            return (k, q, dog, lax.dot_general(q, k, C11, preferred_element_type=F32),
                    lax.dot_general(dog, v, C11, preferred_element_type=F32))

        nxt = products(0)
        for hk in range(A_KV_HEADS):
            k, q, dog, s, dpr = nxt
            if hk + 1 < A_KV_HEADS:
                nxt = products(hk + 1)
            ks = slice(hk * hd, (hk + 1) * hd)
            heads = [hk * grp + gi for gi in range(grp)]
            p, p_sink = _swa_softmax(s, band, _swa_sink_rows(sink_ref, heads))
            delta = jnp.sum(p * dpr, axis=-1, keepdims=True)
            dsc = (p * (dpr - delta) * scale).astype(BF16)
            dqg = jnp.dot(dsc, k, preferred_element_type=F32)
            dk = lax.dot_general(dsc, q, C00, preferred_element_type=F32)
            dv = lax.dot_general(p.astype(BF16), dog, C00, preferred_element_type=F32)
            sink_term = p_sink * delta
            for gi, h in enumerate(heads):
                rows = slice(gi * BLOCK, (gi + 1) * BLOCK)
                dq_s[:, h * hd:(h + 1) * hd] = dqg[rows]
                dsink = dsink + jnp.where(lane == h, -jnp.sum(sink_term[rows], axis=0, keepdims=True), 0.0)
            dp_ref[:, ks] = dk[:BLOCK]
            dc_ref[:, ks] = dk[BLOCK:]
            dp_ref[:, nkv + hk * hd:nkv + (hk + 1) * hd] = dv[:BLOCK]
            dc_ref[:, nkv + hk * hd:nkv + (hk + 1) * hd] = dv[BLOCK:]
        ds_ref[...] += dsink
        dq_ref[...] = _rope_t(dq_s[...], c_ref[...], s_ref[...]).astype(BF16)

    tab = pl.BlockSpec((BLOCK, LANES), lambda n: (n, 0))
    blk = lambda w: pl.BlockSpec((BLOCK, w), lambda n: (n, 0))
    return _call(
        body, name=name, grid=(S // BLOCK,), in_specs=_swa_specs() + [blk(nq), tab, tab],
        out_specs=[blk(nq), blk(2 * nkv), blk(2 * nkv), pl.BlockSpec((1, LANES), lambda n: (0, 0))],
        out_shape=[_sds((S, nq), BF16), _sds((S, 2 * nkv), F32), _sds((S, 2 * nkv), F32), _sds((1, LANES), F32)],
        scratch_shapes=[pltpu.VMEM((BLOCK, nq), F32)],
        params=_params("arbitrary"), comm=comm,
    )(qkv, qkv, qkv, qkv, qkv, sinks, do, cos, sin)


def _swa_dqkv(dq, dcur, dprev, cos, sin, *, name):
    S, nq = dq.shape
    nkv = dcur.shape[1] // 2
    nb = S // BLOCK

    def body(dq_ref, dc_ref, dp_ref, c_ref, s_ref, o_ref):
        o_ref[:, :nq] = dq_ref[...]
        d = dc_ref[...] + jnp.where(pl.program_id(0) < nb - 1, dp_ref[...], 0.0)
        o_ref[:, nq:nq + nkv] = _rope_t(d[:, :nkv], c_ref[...], s_ref[...]).astype(BF16)
        o_ref[:, nq + nkv:] = d[:, nkv:].astype(BF16)

    tab = pl.BlockSpec((BLOCK, LANES), lambda m: (m, 0))
    blk = lambda w: pl.BlockSpec((BLOCK, w), lambda m: (m, 0))
    return pl.pallas_call(
        body, name=name, grid=(nb,),
        in_specs=[blk(nq), blk(2 * nkv), pl.BlockSpec((BLOCK, 2 * nkv), lambda m: (jnp.minimum(m + 1, nb - 1), 0)), tab, tab],
        out_specs=blk(nq + 2 * nkv), out_shape=_sds((S, nq + 2 * nkv), BF16), compiler_params=_params("parallel"),
    )(dq, dcur, dprev, cos, sin)


def _lru_gates(u, wri_ref, br, bi, sp):
    ub = u.astype(BF16)
    rs, igs = [], []
    for hb in range(LRU_BLOCKS):
        sl = slice(hb * LRU_BLOCK_W, (hb + 1) * LRU_BLOCK_W)
        ri = jnp.dot(ub[:, sl], wri_ref[hb], preferred_element_type=F32)
        rs.append(ri[:, :LRU_BLOCK_W])
        igs.append(ri[:, LRU_BLOCK_W:])
    r = _sigmoid(jnp.concatenate(rs, axis=1) + br)
    ig = _sigmoid(jnp.concatenate(igs, axis=1) + bi)
    la = -LRU_C * r * sp
    a = jnp.exp(la)
    sq = jnp.sqrt(_neg_expm1(2.0 * la))
    return r, ig, a, sq


def _lru_fwd(xw, cw, cb, wri, br, bi, lam, *, name, comm=None):
    S = xw.shape[0]
    W = D_MODEL
    tm = min(LRU_T, S)

    def body(gate_ref, up_ref, cw_ref, cb_ref, wri_ref, br_ref, bi_ref, lam_ref, y_ref, u_ref, h_ref, cu, ch, a_s, b_s):
        @pl.when(pl.program_id(0) == 0)
        def _():
            cu[...] = jnp.zeros_like(cu)
            ch[...] = jnp.zeros_like(ch)

        up = up_ref[...]
        prev8 = cu[...]
        u = up * cw_ref[3:4, :] + cb_ref[...]
        for d in range(1, LRU_CONV):
            u = u + _shift_down(up, prev8, d) * cw_ref[3 - d:4 - d, :]
        cu[...] = up[tm - 8:tm]
        u_ref[...] = u
        sp, _ = _softplus_neg(lam_ref[...])
        _, ig, a, sq = _lru_gates(u, wri_ref, br_ref[...], bi_ref[...], sp)
        a_s[...] = a
        b_s[...] = sq * (ig * u)
        rid = lax.broadcasted_iota(jnp.int32, (8, W), 0)

        def tile(t, h):
            r0 = pl.multiple_of(t * 8, 8)
            at, bt = a_s[pl.ds(r0, 8), :], b_s[pl.ds(r0, 8), :]
            out = jnp.zeros((8, W), F32)
            for j in range(8):
                h = at[j:j + 1, :] * h + bt[j:j + 1, :]
                out = jnp.where(rid == j, h, out)
            h_ref[pl.ds(r0, 8), :] = out
            return h

        ch[0:1, :] = lax.fori_loop(0, tm // 8, tile, ch[0:1, :])
        g, _ = _gelu_and_grad(gate_ref[...])
        y_ref[...] = (h_ref[...] * g).astype(BF16)

    row = pl.BlockSpec((tm, W), lambda i: (i, 0))
    full = lambda shape: pl.BlockSpec(shape, lambda i: (0,) * len(shape))
    return _call(
        body, name=name, grid=(S // tm,),
        in_specs=[row, pl.BlockSpec((tm, W), lambda i: (i, 1)), full((LRU_CONV, W)), full((1, W)),
                  full((LRU_BLOCKS, LRU_BLOCK_W, 2 * LRU_BLOCK_W)), full((1, W)), full((1, W)), full((1, W))],
        out_specs=[row, row, row], out_shape=[_sds((S, W), BF16), _sds((S, W), F32), _sds((S, W), F32)],
        scratch_shapes=[pltpu.VMEM((8, W), F32), pltpu.VMEM((8, W), F32), pltpu.VMEM((tm, W), F32), pltpu.VMEM((tm, W), F32)],
        params=_params("arbitrary"), comm=comm,
    )(xw, xw, cw, cb, wri, br, bi, lam)


def _lru_bwd(xw, u, h, dy, cw, wri, br, bi, lam, *, name):
    S = xw.shape[0]
    W = D_MODEL
    tm = min(LRU_T, S)
    nb = S // tm

    def body(gate_ref, up_ref, upp_ref, u_ref, h_ref, hp_ref, dy_ref, cw_ref, wri_ref, br_ref, bi_ref, lam_ref,
             dxw_ref, dcw_ref, dcb_ref, dwri_ref, dbr_ref, dbi_ref, dlam_ref, cg, cdu, a_s, d_s, g_s):
        i = pl.program_id(0)
        r_blk = nb - 1 - i

        @pl.when(i == 0)
        def _():
            cg[...] = jnp.zeros_like(cg)
            cdu[...] = jnp.zeros_like(cdu)
            for ref in (dcw_ref, dcb_ref, dwri_ref, dbr_ref, dbi_ref, dlam_ref):
                ref[...] = jnp.zeros_like(ref)

        u = u_ref[...]
        hv = h_ref[...]
        sp, dsp = _softplus_neg(lam_ref[...])
        r, ig, a, sq = _lru_gates(u, wri_ref, br_ref[...], bi_ref[...], sp)
        dy = dy_ref[...].astype(F32)
        g, dgelu = _gelu_and_grad(gate_ref[...])
        dxw_ref[:, :W] = (dy * hv * dgelu).astype(BF16)
        a_s[...] = a
        d_s[...] = dy * g
        rid = lax.broadcasted_iota(jnp.int32, (8, W), 0)

        def tile(t, c):
            r0 = pl.multiple_of((tm // 8 - 1 - t) * 8, 8)
            at, dt = a_s[pl.ds(r0, 8), :], d_s[pl.ds(r0, 8), :]
            out = jnp.zeros((8, W), F32)
            for j in range(7, -1, -1):
                gt = dt[j:j + 1, :] + c
                c = at[j:j + 1, :] * gt
                out = jnp.where(rid == j, gt, out)
            g_s[pl.ds(r0, 8), :] = out
            return c

        cg[0:1, :] = lax.fori_loop(0, tm // 8, tile, cg[0:1, :])
        gt = g_s[...]
        hprev8 = jnp.where(r_blk > 0, hp_ref[...], 0.0)
        da = gt * _shift_down(hv, hprev8, 1)
        iu = ig * u
        d_iu = gt * sq
        dla = da * a - (gt * iu) * (a * a) / sq
        dlam_ref[...] += jnp.sum(dla * r, axis=0, keepdims=True) * (-LRU_C) * dsp
        dr_pre = dla * (-LRU_C) * sp * r * (1.0 - r)
        di_pre = d_iu * u * ig * (1.0 - ig)
        dbr_ref[...] += jnp.sum(dr_pre, axis=0, keepdims=True)
        dbi_ref[...] += jnp.sum(di_pre, axis=0, keepdims=True)
        ub = u.astype(BF16)
        dus = []
        for hb in range(LRU_BLOCKS):
            sl = slice(hb * LRU_BLOCK_W, (hb + 1) * LRU_BLOCK_W)
            dri = jnp.concatenate([dr_pre[:, sl], di_pre[:, sl]], axis=1).astype(BF16)
            dus.append(lax.dot_general(dri, wri_ref[hb], C11, preferred_element_type=F32))
            dwri_ref[hb] += lax.dot_general(ub[:, sl], dri, C00, preferred_element_type=F32)
        du = d_iu * ig + jnp.concatenate(dus, axis=1)
        dcb_ref[...] += jnp.sum(du, axis=0, keepdims=True)
        up = up_ref[...]
        upprev8 = jnp.where(r_blk > 0, upp_ref[...], 0.0)
        dcw_ref[3:4, :] += jnp.sum(du * up, axis=0, keepdims=True)
        for d in range(1, LRU_CONV):
            dcw_ref[3 - d:4 - d, :] += jnp.sum(du * _shift_down(up, upprev8, d), axis=0, keepdims=True)
        next8 = cdu[...]
        dup = du * cw_ref[3:4, :]
        for d in range(1, LRU_CONV):
            dup = dup + _shift_up(du, next8, d) * cw_ref[3 - d:4 - d, :]
        cdu[...] = du[0:8]
        dxw_ref[:, W:] = dup.astype(BF16)

    rev = lambda c: (lambda i: (nb - 1 - i, c))
    halo = lambda c: (lambda i: (jnp.maximum((nb - 1 - i) * (tm // 8) - 1, 0), c))
    full = lambda shape: pl.BlockSpec(shape, lambda i: (0,) * len(shape))
    vec = full((1, W))
    return pl.pallas_call(
        body, name=name, grid=(nb,),
        in_specs=[pl.BlockSpec((tm, W), rev(0)), pl.BlockSpec((tm, W), rev(1)), pl.BlockSpec((8, W), halo(1)),
                  pl.BlockSpec((tm, W), rev(0)), pl.BlockSpec((tm, W), rev(0)), pl.BlockSpec((8, W), halo(0)),
                  pl.BlockSpec((tm, W), rev(0)), full((LRU_CONV, W)), full((LRU_BLOCKS, LRU_BLOCK_W, 2 * LRU_BLOCK_W)),
                  vec, vec, vec],
        out_specs=[pl.BlockSpec((tm, 2 * W), rev(0)), full((LRU_CONV, W)), vec,
                   full((LRU_BLOCKS, LRU_BLOCK_W, 2 * LRU_BLOCK_W)), vec, vec, vec],
        out_shape=[_sds((S, 2 * W), BF16), _sds((LRU_CONV, W), F32), _sds((1, W), F32),
                   _sds((LRU_BLOCKS, LRU_BLOCK_W, 2 * LRU_BLOCK_W), F32), _sds((1, W), F32), _sds((1, W), F32),
                   _sds((1, W), F32)],
        scratch_shapes=[pltpu.VMEM((8, W), F32), pltpu.VMEM((8, W), F32), pltpu.VMEM((tm, W), F32),
                        pltpu.VMEM((tm, W), F32), pltpu.VMEM((tm, W), F32)],
        compiler_params=_params("arbitrary"),
    )(xw, xw, xw, u, h, h, dy, cw, wri, br, bi, lam)


def _rms(x, g):
    r = lax.rsqrt(jnp.mean(x * x, axis=-1, keepdims=True) + RMS_EPS)
    return x * r * g, r


def _mla_pre(c, qg, kvg, cos, sin, *, name):
    S = c.shape[0]
    tm = min(ROW_T, S)
    q0, k0 = C_Q_RANK, C_Q_RANK + C_KV_RANK

    def body(c_ref, qg_ref, kvg_ref, cs_ref, sn_ref, cq_ref, ckv_ref, kr_ref):
        cq_ref[...] = _rms(c_ref[:, :q0], qg_ref[...])[0].astype(BF16)
        ckv_ref[...] = _rms(c_ref[:, q0:k0], kvg_ref[...])[0].astype(BF16)
        kr_ref[...] = _rope(c_ref[:, k0:], cs_ref[...], sn_ref[...]).astype(BF16)

    blk = lambda w: pl.BlockSpec((tm, w), lambda i: (i, 0))
    vec = lambda w: pl.BlockSpec((1, w), lambda i: (0, 0))
    return pl.pallas_call(
        body, name=name, grid=(S // tm,),
        in_specs=[blk(c.shape[1]), vec(C_Q_RANK), vec(C_KV_RANK), blk(LANES), blk(LANES)],
        out_specs=[blk(C_Q_RANK), blk(C_KV_RANK), blk(C_ROPE)],
        out_shape=[_sds((S, C_Q_RANK), BF16), _sds((S, C_KV_RANK), BF16), _sds((S, C_ROPE), BF16)],
        compiler_params=_params("parallel"),
    )(c, qg, kvg, cos, sin)


def _mla_post_bwd(c, dcq_a, dcq_b, dckv, dkr_h, qg, kvg, cos, sin, *, name):
    S = c.shape[0]
    tm = min(ROW_T, S)
    q0, k0 = C_Q_RANK, C_Q_RANK + C_KV_RANK

    def rms_bwd(x, g, dy):
        r = lax.rsqrt(jnp.mean(x * x, axis=-1, keepdims=True) + RMS_EPS)
        uu = dy * g
        dx = r * uu - x * (r * r * r) * jnp.mean(uu * x, axis=-1, keepdims=True)
        return dx, jnp.sum(dy * x * r, axis=0, keepdims=True)

    def body(c_ref, da_ref, db_ref, dkv_ref, dkr_ref, qg_ref, kvg_ref, cs_ref, sn_ref, dc_ref, dqg_ref, dkvg_ref):
        @pl.when(pl.program_id(0) == 0)
        def _():
            dqg_ref[...] = jnp.zeros_like(dqg_ref)
            dkvg_ref[...] = jnp.zeros_like(dkvg_ref)

        dx, dg = rms_bwd(c_ref[:, :q0], qg_ref[...], da_ref[...] + db_ref[...])
        dc_ref[:, :q0] = dx.astype(BF16)
        dqg_ref[...] += dg
        dx, dg = rms_bwd(c_ref[:, q0:k0], kvg_ref[...], dkv_ref[...])
        dc_ref[:, q0:k0] = dx.astype(BF16)
        dkvg_ref[...] += dg
        dkr = dkr_ref[0]
        for hh in range(1, C_HEADS):
            dkr = dkr + dkr_ref[hh]
        dc_ref[:, k0:] = _rope_t(dkr, cs_ref[...], sn_ref[...]).astype(BF16)

    blk = lambda w: pl.BlockSpec((tm, w), lambda i: (i, 0))
    vec = lambda w: pl.BlockSpec((1, w), lambda i: (0, 0))
    return pl.pallas_call(
        body, name=name, grid=(S // tm,),
        in_specs=[blk(c.shape[1]), blk(C_Q_RANK), blk(C_Q_RANK), blk(C_KV_RANK),
                  pl.BlockSpec((C_HEADS, tm, C_ROPE), lambda i: (0, i, 0)), vec(C_Q_RANK), vec(C_KV_RANK), blk(LANES), blk(LANES)],
        out_specs=[blk(c.shape[1]), vec(C_Q_RANK), vec(C_KV_RANK)],
        out_shape=[_sds(c.shape, BF16), _sds((1, C_Q_RANK), F32), _sds((1, C_KV_RANK), F32)],
        compiler_params=_params("arbitrary"),
    )(c, dcq_a, dcq_b, dckv, dkr_h, qg, kvg, cos, sin)


def _rope_heads(x, cos, sin, *, transpose, name):
    S, W = x.shape
    tm = min(ROW_T, S)
    fn = _rope_t if transpose else _rope

    def body(x_ref, c_ref, s_ref, o_ref):
        o_ref[...] = fn(x_ref[...].astype(F32), c_ref[...], s_ref[...]).astype(BF16)

    row = pl.BlockSpec((tm, W), lambda i: (i, 0))
    tab = pl.BlockSpec((tm, LANES), lambda i: (i, 0))
    return pl.pallas_call(body, name=name, grid=(S // tm,), in_specs=[row, tab, tab], out_specs=row,
                          out_shape=_sds((S, W), BF16), compiler_params=_params("parallel"))(x, cos, sin)


MLA_GROUP = 4
MLA_SCALE = (C_NOPE + C_ROPE) ** -0.5
LOG2E = 1.4426950408889634


def _mla_cat(nope, rope):
    return jnp.concatenate([nope, rope], axis=1)


def _mla_scores2(qn, qr, kn, kr, diagonal):
    s = lax.dot_general(_mla_cat(qn, qr), _mla_cat(kn, kr), C11, preferred_element_type=F32)
    s = s * (MLA_SCALE * LOG2E)
    if diagonal:
        row = lax.broadcasted_iota(jnp.int32, s.shape, 0)
        col = lax.broadcasted_iota(jnp.int32, s.shape, 1)
        s = jnp.where(col <= row, s, NEG)
    return s


def _causal_pairs(n, query_major):
    if query_major:
        pairs = [(i, j) for i in range(n) for j in range(i + 1)]
    else:
        pairs = [(i, j) for j in range(n) for i in range(j, n)]
    return jnp.asarray([p[0] for p in pairs], jnp.int32), jnp.asarray([p[1] for p in pairs], jnp.int32)


def _mla_flash_fwd(qn, qr, kv, kr, *, name):
    S = qn.shape[0]
    H, G, t = C_HEADS, MLA_GROUP, min(FLASH_T, S)
    qi, kj = _causal_pairs(S // t, True)

    def body(qi_ref, kj_ref, qn_ref, qr_ref, kn_ref, v_ref, kr_ref, o_ref, lse_ref, *scr):
        m_s, l_s, acc = scr[:G], scr[G:2 * G], scr[2 * G:]
        p_id = pl.program_id(1)
        i, j = qi_ref[p_id], kj_ref[p_id]
        sls = [slice(hh * LANES, (hh + 1) * LANES) for hh in range(G)]

        @pl.when(j == 0)
        def _():
            for hh in range(G):
                m_s[hh][...] = jnp.full_like(m_s[hh], NEG)
                l_s[hh][...] = jnp.zeros_like(l_s[hh])
                acc[hh][...] = jnp.zeros_like(acc[hh])

        def step(diagonal):
            scores = lambda hh: _mla_scores2(qn_ref[:, sls[hh]], qr_ref[hh], kn_ref[:, sls[hh]], kr_ref[...], diagonal)
            s_next = scores(0)
            for hh in range(G):
                s = s_next
                if hh + 1 < G:
                    s_next = scores(hh + 1)
                m_prev = m_s[hh][...]
                m_new = jnp.maximum(m_prev, jnp.max(s, axis=-1, keepdims=True))
                corr = jnp.exp2(m_prev - m_new)
                p = jnp.exp2(s - m_new[:, 0:1])
                l_s[hh][...] = corr * l_s[hh][...] + jnp.sum(p, axis=-1, keepdims=True)
                acc[hh][...] = corr * acc[hh][...] + jnp.dot(p.astype(BF16), v_ref[:, sls[hh]], preferred_element_type=F32)
                m_s[hh][...] = m_new

        @pl.when(j < i)
        def _():
            step(False)

        @pl.when(j == i)
        def _():
            step(True)
            for hh in range(G):
                o_ref[:, sls[hh]] = (acc[hh][...] / l_s[hh][...]).astype(BF16)
                lse_ref[:, sls[hh]] = m_s[hh][...] + jnp.log2(l_s[hh][...])

    wide = lambda which, off: pl.BlockSpec((t, G * LANES), lambda h, p, qi, kj: ((qi if which == "q" else kj)[p], off + h))
    return pl.pallas_call(
        body, name=name,
        grid_spec=pltpu.PrefetchScalarGridSpec(
            num_scalar_prefetch=2, grid=(H // G, qi.shape[0]),
            in_specs=[wide("q", 0), pl.BlockSpec((G, t, C_ROPE), lambda h, p, qi, kj: (h, qi[p], 0)),
                      wide("k", 0), wide("k", H // G), pl.BlockSpec((t, C_ROPE), lambda h, p, qi, kj: (kj[p], 0))],
            out_specs=[wide("q", 0), wide("q", 0)],
            scratch_shapes=[pltpu.VMEM((t, LANES), F32)] * (3 * G)),
        out_shape=[_sds((S, H * C_V), BF16), _sds((S, H * LANES), F32)],
        compiler_params=_params("parallel", "arbitrary"),
    )(qi, kj, qn, qr, kv, kv, kr)


def _mla_delta(do, o, *, name):
    S, W = do.shape
    tm = min(ROW_T, S)

    def body(do_ref, o_ref, d_ref):
        for h in range(C_HEADS):
            sl = slice(h * C_V, (h + 1) * C_V)
            d = jnp.sum(do_ref[:, sl].astype(F32) * o_ref[:, sl].astype(F32), axis=-1, keepdims=True)
            d_ref[:, sl] = jnp.broadcast_to(d, (tm, C_V))

    row = pl.BlockSpec((tm, W), lambda i: (i, 0))
    return pl.pallas_call(body, name=name, grid=(S // tm,), in_specs=[row, row], out_specs=row,
                          out_shape=_sds((S, W), F32), compiler_params=_params("parallel"))(do, o)


def _mla_flash_dq(qn, qr, kv, kr, do, lse, delta, *, name):
    S = qn.shape[0]
    H, G, t = C_HEADS, MLA_GROUP, min(FLASH_T, S)
    qi, kj = _causal_pairs(S // t, True)

    def body(qi_ref, kj_ref, qn_ref, qr_ref, kn_ref, v_ref, kr_ref, do_ref, lse_ref, dl_ref, dqn_ref, dqr_ref, acc):
        p_id = pl.program_id(1)
        i, j = qi_ref[p_id], kj_ref[p_id]
        sls = [slice(hh * LANES, (hh + 1) * LANES) for hh in range(G)]

        @pl.when(j == 0)
        def _():
            acc[...] = jnp.zeros_like(acc)

        def step(diagonal):
            def products(hh):
                s = _mla_scores2(qn_ref[:, sls[hh]], qr_ref[hh], kn_ref[:, sls[hh]], kr_ref[...], diagonal)
                return s, lax.dot_general(do_ref[:, sls[hh]], v_ref[:, sls[hh]], C11, preferred_element_type=F32)

            nxt = products(0)
            for hh in range(G):
                s, dp = nxt
                if hh + 1 < G:
                    nxt = products(hh + 1)
                p = jnp.exp2(s - lse_ref[:, hh * LANES:hh * LANES + 1])
                ds = (p * (dp - dl_ref[:, hh * LANES:hh * LANES + 1])).astype(BF16)
                acc[hh] += jnp.dot(ds, _mla_cat(kn_ref[:, sls[hh]], kr_ref[...]), preferred_element_type=F32)

        @pl.when(j < i)
        def _():
            step(False)

        @pl.when(j == i)
        def _():
            step(True)
            for hh in range(G):
                dqn_ref[:, sls[hh]] = (acc[hh, :, :C_NOPE] * MLA_SCALE).astype(BF16)
                dqr_ref[hh] = acc[hh, :, C_NOPE:] * MLA_SCALE

    wide = lambda which, off: pl.BlockSpec((t, G * LANES), lambda h, p, qi, kj: ((qi if which == "q" else kj)[p], off + h))
    qrb = pl.BlockSpec((G, t, C_ROPE), lambda h, p, qi, kj: (h, qi[p], 0))
    return pl.pallas_call(
        body, name=name,
        grid_spec=pltpu.PrefetchScalarGridSpec(
            num_scalar_prefetch=2, grid=(H // G, qi.shape[0]),
            in_specs=[wide("q", 0), qrb, wide("k", 0), wide("k", H // G),
                      pl.BlockSpec((t, C_ROPE), lambda h, p, qi, kj: (kj[p], 0)), wide("q", 0), wide("q", 0), wide("q", 0)],
            out_specs=[wide("q", 0), qrb],
            scratch_shapes=[pltpu.VMEM((G, t, C_NOPE + C_ROPE), F32)]),
        out_shape=[_sds((S, H * C_NOPE), BF16), _sds((H, S, C_ROPE), F32)],
        compiler_params=_params("parallel", "arbitrary"),
    )(qi, kj, qn, qr, kv, kv, kr, do, lse, delta)


def _mla_flash_dkv(qn, qr, kv, kr, do, lse, delta, *, name):
    S = qn.shape[0]
    H, G, t = C_HEADS, MLA_GROUP, min(FLASH_T, S)
    n = S // t
    qi, kj = _causal_pairs(n, False)

    def body(qi_ref, kj_ref, qn_ref, qr_ref, kn_ref, v_ref, kr_ref, do_ref, lse_ref, dl_ref, dkn_ref, dv_ref, dkr_ref, ak, av):
        p_id = pl.program_id(1)
        i, j = qi_ref[p_id], kj_ref[p_id]
        sls = [slice(hh * LANES, (hh + 1) * LANES) for hh in range(G)]

        def step(diagonal):
            def products(hh):
                s = _mla_scores2(qn_ref[:, sls[hh]], qr_ref[hh], kn_ref[:, sls[hh]], kr_ref[...], diagonal)
                return s, lax.dot_general(do_ref[:, sls[hh]], v_ref[:, sls[hh]], C11, preferred_element_type=F32)

            nxt = products(0)
            for hh in range(G):
                s, dp = nxt
                if hh + 1 < G:
                    nxt = products(hh + 1)
                sl = sls[hh]
                p = jnp.exp2(s - lse_ref[:, hh * LANES:hh * LANES + 1])
                ds = (p * (dp - dl_ref[:, hh * LANES:hh * LANES + 1])).astype(BF16)
                av[:, sl] += lax.dot_general(p.astype(BF16), do_ref[:, sl], C00, preferred_element_type=F32)
                ak[hh] += lax.dot_general(ds, _mla_cat(qn_ref[:, sl], qr_ref[hh]), C00, preferred_element_type=F32)

        @pl.when(i == j)
        def _():
            ak[...] = jnp.zeros_like(ak)
            av[...] = jnp.zeros_like(av)
            step(True)

        @pl.when(i > j)
        def _():
            step(False)

        @pl.when(i == n - 1)
        def _():
            dv_ref[...] = av[...].astype(BF16)
            for hh in range(G):
                dkn_ref[:, sls[hh]] = (ak[hh, :, :C_NOPE] * MLA_SCALE).astype(BF16)
                dkr_ref[hh] = ak[hh, :, C_NOPE:] * MLA_SCALE

    wide = lambda which, off: pl.BlockSpec((t, G * LANES), lambda h, p, qi, kj: ((qi if which == "q" else kj)[p], off + h))
    krb = pl.BlockSpec((G, t, C_ROPE), lambda h, p, qi, kj: (h, kj[p], 0))
    return pl.pallas_call(
        body, name=name,
        grid_spec=pltpu.PrefetchScalarGridSpec(
            num_scalar_prefetch=2, grid=(H // G, qi.shape[0]),
            in_specs=[wide("q", 0), pl.BlockSpec((G, t, C_ROPE), lambda h, p, qi, kj: (h, qi[p], 0)), wide("k", 0),
                      wide("k", H // G), pl.BlockSpec((t, C_ROPE), lambda h, p, qi, kj: (kj[p], 0)),
                      wide("q", 0), wide("q", 0), wide("q", 0)],
            out_specs=[wide("k", 0), wide("k", 0), krb],
            scratch_shapes=[pltpu.VMEM((G, t, C_NOPE + C_ROPE), F32), pltpu.VMEM((t, G * LANES), F32)]),
        out_shape=[_sds((S, H * C_NOPE), BF16), _sds((S, H * C_V), BF16), _sds((H, S, C_ROPE), F32)],
        compiler_params=_params("parallel", "arbitrary"),
    )(qi, kj, qn, qr, kv, kv, kr, do, lse, delta)


def _place():
    return lax.axis_index("x"), lax.axis_index("y"), lax.axis_index("c")


def _other_chips(x, y):
    return [(1 - x, y), (x, 1 - y), (1 - x, 1 - y)]


def _all_gather_chips(p, *, name):
    R, C = p.shape

    def body(p_ref, o_ref, send_sems, recv_sems, local_sem):
        x, y, c = _place()
        me = 2 * x + y
        local = pltpu.make_async_copy(p_ref, o_ref.at[me], local_sem)
        local.start()
        copies = [pltpu.make_async_remote_copy(src_ref=p_ref, dst_ref=o_ref.at[me], send_sem=send_sems.at[k],
                                               recv_sem=recv_sems.at[k], device_id=(px, py, c), device_id_type=MESH)
                  for k, (px, py) in enumerate(_other_chips(x, y))]
        for cp in copies:
            cp.start()
        for cp in copies:
            cp.wait()
        local.wait()

    any_spec = pl.BlockSpec(memory_space=pl.ANY)
    return pl.pallas_call(
        body, name=name, in_specs=[any_spec], out_specs=any_spec, out_shape=_sds((4, R, C), p.dtype),
        scratch_shapes=[pltpu.SemaphoreType.DMA((3,)), pltpu.SemaphoreType.DMA((3,)), pltpu.SemaphoreType.DMA(())],
    )(p)


def _shard_of(ref, axis, pos, size):
    idx = [slice(None)] * len(ref.shape)
    idx[axis] = pl.ds(pos * size, size)
    return ref.at[tuple(idx)]


def _shard_pos(chip, swapped):
    return (chip % 2) * 2 + chip // 2 if swapped else chip


class _Comm:
    def __init__(self, inputs, out_shapes, sems, start, finish, deliver):
        self.inputs, self.out_shapes, self.sems = list(inputs), list(out_shapes), list(sems)
        self.start, self.finish, self.deliver = start, finish, deliver


def _call(body, *, name, grid, in_specs, out_specs, out_shape, scratch_shapes=(), params, comm=None):
    in_specs, out_specs, out_shape, scratch_shapes = list(in_specs), list(out_specs), list(out_shape), list(scratch_shapes)
    if comm is None:
        return pl.pallas_call(body, name=name, grid=grid, in_specs=in_specs, out_specs=out_specs, out_shape=out_shape,
                              scratch_shapes=scratch_shapes, compiler_params=params)
    n_in, n_out, n_scr = len(in_specs), len(out_specs), len(scratch_shapes)
    c_in, c_out = len(comm.inputs), len(comm.out_shapes)

    def hosted(*refs):
        a, rest = refs[:n_in], refs[n_in:]
        cin, rest = rest[:c_in], rest[c_in:]
        o, rest = rest[:n_out], rest[n_out:]
        cout, rest = rest[:c_out], rest[c_out:]
        scr, sems = rest[:n_scr], rest[n_scr:]
        first = functools.reduce(jnp.logical_and, [pl.program_id(d) == 0 for d in range(len(grid))])
        last = functools.reduce(jnp.logical_and, [pl.program_id(d) == grid[d] - 1 for d in range(len(grid))])

        @pl.when(first)
        def _():
            comm.start(cin, cout, sems)

        body(*a, *o, *scr)

        @pl.when(last)
        def _():
            comm.finish(cin, cout, sems)

    any_spec = pl.BlockSpec(memory_space=pl.ANY)
    call = pl.pallas_call(
        hosted, name=name, grid=grid, in_specs=in_specs + [any_spec] * c_in, out_specs=out_specs + [any_spec] * c_out,
        out_shape=out_shape + comm.out_shapes, scratch_shapes=scratch_shapes + comm.sems, compiler_params=params)

    def run(*args):
        outs = call(*args, *comm.inputs)
        comm.deliver(outs[n_out:])
        return outs[:n_out]

    return run


def _run_comm(comm, *, name):
    c_in, c_out = len(comm.inputs), len(comm.out_shapes)

    def body(*refs):
        cin, cout, sems = refs[:c_in], refs[c_in:c_in + c_out], refs[c_in + c_out:]
        comm.start(cin, cout, sems)
        comm.finish(cin, cout, sems)

    any_spec = pl.BlockSpec(memory_space=pl.ANY)
    outs = pl.pallas_call(body, name=name, in_specs=[any_spec] * c_in, out_specs=[any_spec] * c_out,
                          out_shape=comm.out_shapes, scratch_shapes=comm.sems)(*comm.inputs)
    comm.deliver(outs)


def _gather_comm(items, deliver):
    n = len(items)
    shard_shapes = [a.shape if j is None else a.shape[1:] for a, j, _, _ in items]
    axes = [ax for _, _, ax, _ in items]
    swapped = [sw for _, _, _, sw in items]
    sizes = [s[a] for s, a in zip(shard_shapes, axes)]
    halves = [s[-2] // 2 for s in shard_shapes]
    full = [tuple(4 * d if i == a else d for i, d in enumerate(s)) for s, a in zip(shard_shapes, axes)]

    def mine(ins, k):
        j = items[k][1]
        return ins[k] if j is None else ins[k].at[j]

    def half_of(ref, k, half, chip=None):
        nd = len(ref.shape)
        split = nd - 2
        idx = [slice(None)] * nd
        start = half * halves[k]
        if chip is not None:
            pos = _shard_pos(chip, swapped[k]) * sizes[k]
            if axes[k] == split:
                start = start + pos
            else:
                idx[axes[k]] = pl.ds(pos, sizes[k])
        idx[split] = pl.ds(start, halves[k])
        return ref.at[tuple(idx)]

    def local_copy(ins, outs, sems, k, me):
        return pltpu.make_async_copy(mine(ins, k), _shard_of(outs[k], axes[k], _shard_pos(me, swapped[k]), sizes[k]), sems[4].at[k])

    def ici_copy(ins, outs, sems, k, j, peer, c, landing_chip):
        return pltpu.make_async_remote_copy(
            src_ref=half_of(mine(ins, k), k, c), dst_ref=half_of(outs[k], k, c, chip=landing_chip), send_sem=sems[0].at[3 * k + j],
            recv_sem=sems[1].at[3 * k + j], device_id=(peer[0], peer[1], c), device_id_type=MESH)

    def pass_copy(outs, sems, k, j, half, chip, sibling):
        region = half_of(outs[k], k, half, chip=chip)
        return pltpu.make_async_remote_copy(src_ref=region, dst_ref=region, send_sem=sems[2].at[3 * k + j],
                                            recv_sem=sems[3].at[3 * k + j], device_id=sibling, device_id_type=MESH)

    def start(ins, outs, sems):
        x, y, c = _place()
        me = 2 * x + y
        for k in range(n):
            local_copy(ins, outs, sems, k, me).start()
            for j, peer in enumerate(_other_chips(x, y)):
                ici_copy(ins, outs, sems, k, j, peer, c, me).start()

    def finish(ins, outs, sems):
        x, y, c = _place()
        me = 2 * x + y
        chips = _other_chips(x, y)
        sibling = (x, y, 1 - c)
        for k in range(n):
            for j, peer in enumerate(chips):
                ici_copy(ins, outs, sems, k, j, peer, c, 2 * peer[0] + peer[1]).wait_recv()
                pass_copy(outs, sems, k, j, c, 2 * peer[0] + peer[1], sibling).start()
        for k in range(n):
            for j, peer in enumerate(chips):
                pass_copy(outs, sems, k, j, 1 - c, 2 * peer[0] + peer[1], sibling).wait_recv()
        for k in range(n):
            local_copy(ins, outs, sems, k, me).wait()
            for j, peer in enumerate(chips):
                ici_copy(ins, outs, sems, k, j, peer, c, me).wait_send()
                pass_copy(outs, sems, k, j, c, 2 * peer[0] + peer[1], sibling).wait_send()

    return _Comm([a for a, _, _, _ in items], [_sds(f, a.dtype) for f, (a, _, _, _) in zip(full, items)],
                 [pltpu.SemaphoreType.DMA((3 * n,))] * 4 + [pltpu.SemaphoreType.DMA((n,))], start, finish, deliver)


def _scatter_comm(items, deliver):
    n = len(items)
    axes = [ax for _, ax, _ in items]
    swapped = [sw for _, _, sw in items]
    sizes = [g.shape[a] // 4 for g, a, _ in items]
    shard = [tuple(d // 4 if i == a else d for i, d in enumerate(g.shape)) for g, a, _ in items]

    def copies(ins, outs, sems):
        x, y, c = _place()
        me = 2 * x + y
        out = []
        for k in range(n):
            own = _shard_of(ins[k], axes[k], _shard_pos(me, swapped[k]), sizes[k])
            out.append(pltpu.make_async_copy(own, outs[k].at[3], sems[2].at[k]))
            for j, (px, py) in enumerate(_other_chips(x, y)):
                src = _shard_of(ins[k], axes[k], _shard_pos(2 * px + py, swapped[k]), sizes[k])
                out.append(pltpu.make_async_remote_copy(src_ref=src, dst_ref=outs[k].at[j], send_sem=sems[0].at[3 * k + j],
                                                        recv_sem=sems[1].at[3 * k + j], device_id=(px, py, c), device_id_type=MESH))
        return out

    def start(ins, outs, sems):
        for cp in copies(ins, outs, sems):
            cp.start()

    def finish(ins, outs, sems):
        for cp in copies(ins, outs, sems):
            cp.wait()

    return _Comm([g for g, _, _ in items], [_sds((4,) + s, g.dtype) for s, (g, _, _) in zip(shard, items)],
                 [pltpu.SemaphoreType.DMA((3 * n,)), pltpu.SemaphoreType.DMA((3 * n,)), pltpu.SemaphoreType.DMA((n,))],
                 start, finish, deliver)


def _row_tile(rows, cols, budget=2 ** 20):
    best = None
    for t in range(8, rows + 1, 8):
        if rows % t == 0 and t * cols * 4 <= budget:
            best = t
    return best or rows


def _sum_partials(recv, into, layer, layers, *, name):
    _, R, C = recv.shape
    tr = _row_tile(R, C)
    nt = R // tr

    def body(own_ref, r0_ref, r1_ref, r2_ref, *rest):
        f = lambda ref: ref[...].astype(F32)
        rest[-1][...] = ((f(own_ref) + f(r0_ref)) + f(r1_ref)) + f(r2_ref)

    rspec = lambda k: pl.BlockSpec((None, tr, C), lambda i: (k, i, 0))
    extra = [] if into is None else [pl.BlockSpec(memory_space=pl.ANY)]
    return pl.pallas_call(
        body, name=name, grid=(nt,), in_specs=[rspec(3), rspec(0), rspec(1), rspec(2)] + extra,
        out_specs=pl.BlockSpec((tr, C), lambda i: (layer * nt + i, 0)), out_shape=_sds((layers * R, C), F32),
        input_output_aliases={} if into is None else {4: 0}, compiler_params=_params("parallel"),
    )(recv, recv, recv, recv, *([] if into is None else [into]))


def _swap_cores(parts, *, name):
    n = len(parts)

    def body(*refs):
        ins, outs = refs[:n], refs[n:2 * n]
        send_sems, recv_sems = refs[2 * n:]
        x, y, c = _place()
        copies = [pltpu.make_async_remote_copy(src_ref=ins[k], dst_ref=outs[k], send_sem=send_sems.at[k], recv_sem=recv_sems.at[k],
                                               device_id=(x, y, 1 - c), device_id_type=MESH) for k in range(n)]
        for cp in copies:
            cp.start()
        for cp in copies:
            cp.wait()

    any_spec = pl.BlockSpec(memory_space=pl.ANY)
    return pl.pallas_call(
        body, name=name, in_specs=[any_spec] * n, out_specs=[any_spec] * n, out_shape=[_sds(p.shape, p.dtype) for p in parts],
        scratch_shapes=[pltpu.SemaphoreType.DMA((n,)), pltpu.SemaphoreType.DMA((n,))],
    )(*parts)


def _all_reduce_small(v, *, name):
    r, C = v.shape

    def body(v_ref, o_ref, buf, send_sems, recv_sems):
        x, y, c = _place()
        me = 4 * x + 2 * y + c
        buf[me] = v_ref[...]
        peers = []
        for k in range(1, 8):
            kx, ky, kc = (k >> 2) & 1, (k >> 1) & 1, k & 1
            px = 1 - x if kx else x
            py = 1 - y if ky else y
            pc = 1 - c if kc else c
            peers.append((px, py, pc))
        copies = []
        for k, peer in enumerate(peers):
            cp = pltpu.make_async_remote_copy(src_ref=v_ref, dst_ref=buf.at[me], send_sem=send_sems.at[k],
                                              recv_sem=recv_sems.at[me], device_id=peer, device_id_type=MESH)
            cp.start()
            copies.append(cp)
        for k, (px, py, pc) in enumerate(peers):
            src = 4 * px + 2 * py + pc
            pltpu.make_async_remote_copy(src_ref=v_ref, dst_ref=buf.at[src], send_sem=send_sems.at[k],
                                         recv_sem=recv_sems.at[src], device_id=peers[k], device_id_type=MESH).wait_recv()
        for cp in copies:
            cp.wait_send()
        acc = buf[0]
        for d in range(1, 8):
            acc = acc + buf[d]
        o_ref[...] = acc

    vm = pl.BlockSpec(memory_space=pltpu.VMEM)
    return pl.pallas_call(
        body, name=name, in_specs=[vm], out_specs=vm, out_shape=_sds((r, C), F32),
        scratch_shapes=[pltpu.VMEM((8, r, C), F32), pltpu.SemaphoreType.DMA((7,)), pltpu.SemaphoreType.DMA((8,))],
    )(v)


def _adamw(w, m, v, ga, gb, *, name):
    R, C = w.shape
    tr = _row_tile(R, C)
    has_b = gb is not None
    c1 = 1.0 / (1.0 - ADAM_B1 ** ADAM_STEP)
    c2 = 1.0 / (1.0 - ADAM_B2 ** ADAM_STEP)

    def body(*refs):
        if has_b:
            w_ref, m_ref, v_ref, ga_ref, gb_ref, g_ref, d_ref, nm_ref, nv_ref = refs
            g = ga_ref[...] + gb_ref[...]
        else:
            w_ref, m_ref, v_ref, ga_ref, g_ref, d_ref, nm_ref, nv_ref = refs
            g = ga_ref[...]
        nm = ADAM_B1 * m_ref[...] + (1.0 - ADAM_B1) * g
        nv = ADAM_B2 * v_ref[...] + (1.0 - ADAM_B2) * (g * g)
        g_ref[...] = g
        nm_ref[...] = nm
        nv_ref[...] = nv
        d_ref[...] = -ADAM_LR * ((nm * c1) / (jnp.sqrt(nv * c2) + ADAM_EPS) + ADAM_WD * w_ref[...])

    blk = pl.BlockSpec((tr, C), lambda i: (i, 0))
    n_in = 5 if has_b else 4
    args = (w, m, v, ga) + ((gb,) if has_b else ())
    return pl.pallas_call(body, name=name, grid=(R // tr,), in_specs=[blk] * n_in, out_specs=[blk] * 4,
                          out_shape=[_sds((R, C), F32)] * 4, compiler_params=_params("parallel"))(*args)


def _seg_rows(n, cols):
    return -(-n // (16 * cols)) * 16


def _pack(arrays, dtype, cols=PACK_COLS, row_mult=512):
    parts, rows = [], 0
    for a in arrays:
        n = int(np.prod(a.shape))
        r = _seg_rows(n, cols)
        flat = a.reshape(-1).astype(dtype)
        if r * cols != n:
            flat = jnp.pad(flat, (0, r * cols - n))
        parts.append(flat.reshape(r, cols))
        rows += r
    pad = -rows % row_mult
    if pad:
        parts.append(jnp.zeros((pad, cols), dtype))
    return jnp.concatenate(parts, axis=0)


def _unpack(packed, shapes, cols=PACK_COLS):
    out, r0 = [], 0
    for shp in shapes:
        n = int(np.prod(shp))
        used = -(-n // cols)
        out.append(packed[r0:r0 + used].reshape(-1)[:n].reshape(shp))
        r0 += _seg_rows(n, cols)
    return out


def _rope_tables(seq):
    inv = 1.0 / (ROPE_THETA ** (jnp.arange(0, 64, 2, dtype=F32) / 64))
    ang = jnp.arange(seq, dtype=F32)[:, None] * inv[None, :]
    cos, sin = jnp.cos(ang), jnp.sin(ang)
    cos128 = jnp.concatenate([cos, cos, cos, cos], axis=1)
    sin128 = jnp.concatenate([-sin, sin, -sin, sin], axis=1)
    return cos128, sin128


def _ffn_perm(a):
    lead = a.shape[:-1]
    nj = D_FF // FFN_TC
    return jnp.swapaxes(a.reshape(lead + (2, nj, FFN_TC)), -3, -2).reshape(lead + (2 * D_FF,))


def _ffn_unperm(a):
    lead = a.shape[:-1]
    nj = D_FF // FFN_TC
    return jnp.swapaxes(a.reshape(lead + (nj, 2, FFN_TC)), -3, -2).reshape(lead + (2 * D_FF,))


def _mixer_a_fwd(xb, w, j, cos, sin, tag, gathers):
    qkv = _mm(xb, w["a_w_qkv"][j], name=f"mm_qkv_{tag}", comm=gathers.get(f"mm_qkv_{tag}"))
    qkv_r = _rope_cols(qkv, cos, sin, (A_HEADS + A_KV_HEADS) * A_HEAD_DIM, name=f"rope_qkv_{tag}",
                       comm=gathers.get(f"rope_qkv_{tag}"))
    o = _swa_fwd(qkv_r, w["a_sinks"][j], name=f"swa_fwd_{tag}", comm=gathers.get(f"swa_fwd_{tag}"))
    return (o, w["a_w_o"][j]), (xb, qkv_r, o)


def _mixer_a_bwd(dzb, res, w, j, cos, sin, tag, grads, make_comm):
    xb, qkv_r, o = res
    do = _mm(dzb, w["a_w_o"][j], tb=True, out_dtype=BF16, name=f"mm_dao_{tag}")
    grads["a_w_o"][j] = _mm(o, dzb, ta=True, out_dtype=BF16, tk=GRAD_TK, name=f"mm_gao_{tag}")
    dq, dcur, dprev, dsink = _swa_bwd(qkv_r, w["a_sinks"][j], do, cos, sin, name=f"swa_bwd_{tag}", comm=make_comm())
    grads["a_sinks"][j] = dsink[0, :A_HEADS]
    dqkv = _swa_dqkv(dq, dcur, dprev, cos, sin, name=f"swa_dqkv_{tag}")
    grads["a_w_qkv"][j] = _mm(xb, dqkv, ta=True, out_dtype=BF16, tk=GRAD_TK, name=f"mm_gqkv_{tag}")
    return dqkv, w["a_w_qkv"][j]


def _mixer_b_fwd(xb, w, j, tag, gathers):
    xw = _mm(xb, w["b_w_in"][j], name=f"mm_bin_{tag}")
    wri = jnp.concatenate([w["b_w_rgate"][j], w["b_w_igate"][j]], axis=-1)
    y, u, h = _lru_fwd(xw, w["b_conv_w"][j], w["b_conv_b"][j][None], wri, w["b_b_rgate"][j][None],
                       w["b_b_igate"][j][None], w["b_lambda"][j][None], name=f"lru_fwd_{tag}", comm=gathers.get(f"lru_fwd_{tag}"))
    return (y, w["b_w_o"][j]), (xb, xw, wri, u, h, y)


def _mixer_b_bwd(dzb, res, w, j, tag, grads):
    xb, xw, wri, u, h, y = res
    dy = _mm(dzb, w["b_w_o"][j], tb=True, out_dtype=BF16, name=f"mm_dbo_{tag}")
    grads["b_w_o"][j] = _mm(y, dzb, ta=True, out_dtype=BF16, tk=GRAD_TK, name=f"mm_gbo_{tag}")
    dxw, dcw, dcb, dwri, dbr, dbi, dlam = _lru_bwd(
        xw, u, h, dy, w["b_conv_w"][j], wri, w["b_b_rgate"][j][None], w["b_b_igate"][j][None], w["b_lambda"][j][None],
        name=f"lru_bwd_{tag}")
    grads["b_conv_w"][j], grads["b_conv_b"][j] = dcw, dcb[0]
    grads["b_w_rgate"][j], grads["b_w_igate"][j] = dwri[..., :LRU_BLOCK_W].astype(BF16), dwri[..., LRU_BLOCK_W:].astype(BF16)
    grads["b_b_rgate"][j], grads["b_b_igate"][j], grads["b_lambda"][j] = dbr[0], dbi[0], dlam[0]
    grads["b_w_in"][j] = _mm(xb, dxw, ta=True, out_dtype=BF16, tk=GRAD_TK, name=f"mm_gbin_{tag}")
    return dxw, w["b_w_in"][j]


def _mla_weights(w, j):
    H = C_HEADS
    uq = w["c_w_uq"][j].reshape(C_Q_RANK, H, C_NOPE + C_ROPE)
    ukv = w["c_w_ukv"][j].reshape(C_KV_RANK, H, C_NOPE + C_V)
    uq_n = uq[:, :, :C_NOPE].reshape(C_Q_RANK, H * C_NOPE)
    uq_r = uq[:, :, C_NOPE:].reshape(C_Q_RANK, H * C_ROPE)
    ukv_p = jnp.concatenate([ukv[:, :, :C_NOPE].reshape(C_KV_RANK, H * C_NOPE),
                             ukv[:, :, C_NOPE:].reshape(C_KV_RANK, H * C_V)], axis=1)
    return uq_n, uq_r, ukv_p


def _mixer_c_fwd(xb, w, j, cos, sin, tag):
    S = xb.shape[0]
    H = C_HEADS
    uq_n, uq_r, ukv_p = _mla_weights(w, j)
    c = _mm(xb, w["c_w_down"][j], name=f"mm_cdown_{tag}")
    cq, ckv, kr = _mla_pre(c, w["c_q_norm"][j][None], w["c_kv_norm"][j][None], cos, sin, name=f"mla_pre_{tag}")
    qn = _mm(cq, uq_n, out_dtype=BF16, name=f"mm_uqn_{tag}")
    qr_flat = _rope_heads(_mm(cq, uq_r, name=f"mm_uqr_{tag}"), cos, sin, transpose=False, name=f"rope_qr_{tag}")
    qr = jnp.transpose(qr_flat.reshape(S, H, C_ROPE), (1, 0, 2))
    kv = _mm(ckv, ukv_p, out_dtype=BF16, name=f"mm_ukv_{tag}")
    o, lse = _mla_flash_fwd(qn, qr, kv, kr, name=f"mla_fwd_{tag}")
    return (o, w["c_w_o"][j]), (xb, c, cq, ckv, kr, qn, qr, kv, o, lse, uq_n, uq_r, ukv_p)


def _mixer_c_bwd(dzb, res, w, j, cos, sin, tag, grads):
    xb, c, cq, ckv, kr, qn, qr, kv, o, lse, uq_n, uq_r, ukv_p = res
    S = xb.shape[0]
    H = C_HEADS
    do = _mm(dzb, w["c_w_o"][j], tb=True, out_dtype=BF16, name=f"mm_dco_{tag}")
    grads["c_w_o"][j] = _mm(o, dzb, ta=True, out_dtype=BF16, tk=GRAD_TK, name=f"mm_gco_{tag}")
    delta = _mla_delta(do, o, name=f"mla_delta_{tag}")
    dqn, dqr = _mla_flash_dq(qn, qr, kv, kr, do, lse, delta, name=f"mla_dq_{tag}")
    dkn, dv, dkr_h = _mla_flash_dkv(qn, qr, kv, kr, do, lse, delta, name=f"mla_dkv_{tag}")
    dkv = jnp.concatenate([dkn, dv], axis=1)
    dqr_flat = _rope_heads(jnp.transpose(dqr, (1, 0, 2)).reshape(S, H * C_ROPE), cos, sin, transpose=True, name=f"rope_dqr_{tag}")
    g_uq_n = _mm(cq, dqn, ta=True, out_dtype=BF16, tk=GRAD_TK, name=f"mm_guqn_{tag}")
    g_uq_r = _mm(cq, dqr_flat, ta=True, out_dtype=BF16, tk=GRAD_TK, name=f"mm_guqr_{tag}")
    g_ukv = _mm(ckv, dkv, ta=True, out_dtype=BF16, tk=GRAD_TK, name=f"mm_gukv_{tag}")
    grads["c_w_uq"][j] = jnp.concatenate([g_uq_n.reshape(C_Q_RANK, H, C_NOPE), g_uq_r.reshape(C_Q_RANK, H, C_ROPE)],
                                         axis=2).reshape(C_Q_RANK, H * (C_NOPE + C_ROPE))
    grads["c_w_ukv"][j] = jnp.concatenate([g_ukv[:, :H * C_NOPE].reshape(C_KV_RANK, H, C_NOPE),
                                           g_ukv[:, H * C_NOPE:].reshape(C_KV_RANK, H, C_V)],
                                          axis=2).reshape(C_KV_RANK, H * (C_NOPE + C_V))
    dcq_a = _mm(dqn, uq_n, tb=True, name=f"mm_dcqa_{tag}")
    dcq_b = _mm(dqr_flat, uq_r, tb=True, name=f"mm_dcqb_{tag}")
    dckv = _mm(dkv, ukv_p, tb=True, name=f"mm_dckv_{tag}")
    dc, dqg, dkvg = _mla_post_bwd(c, dcq_a, dcq_b, dckv, dkr_h, w["c_q_norm"][j][None], w["c_kv_norm"][j][None], cos, sin,
                                  name=f"mla_post_{tag}")
    grads["c_q_norm"][j], grads["c_kv_norm"][j] = dqg[0], dkvg[0]
    grads["c_w_down"][j] = _mm(xb, dc, ta=True, out_dtype=BF16, tk=GRAD_TK, name=f"mm_gcdown_{tag}")
    return dc, w["c_w_down"][j]


def _layer_big(i, mixer=True, rest=True):
    kind, j = i % N_MIXERS, i // N_MIXERS
    own = [[("a_w_qkv", j), ("a_w_o", j)], [("b_w_in", j), ("b_w_rgate", j), ("b_w_igate", j), ("b_w_o", j)],
           [("c_w_down", j), ("c_w_uq", j), ("c_w_ukv", j), ("c_w_o", j)]][kind]
    return (own if mixer else []) + ([("x_w_q", i), ("x_w_o", i), ("f_w_up", i), ("f_w_down", i)] if rest else [])


def _local_step(x, mem, target, w, n_layers, gathers, scatter):
    S = x.shape[0]
    cos, sin = _rope_tables(S)
    grads = {n: [None] * n_layers[n] for n in WEIGHTS if n != "mem_w_kv"}

    xs, xb = x, x.astype(BF16)
    saved = []
    for i in range(DEPTH):
        kind, j = i % N_MIXERS, i // N_MIXERS
        tag = f"l{i}"
        if kind == 0:
            (act, w_out), res = _mixer_a_fwd(xb, w, j, cos, sin, tag, gathers)
        elif kind == 1:
            (act, w_out), res = _mixer_b_fwd(xb, w, j, tag, gathers)
        else:
            (act, w_out), res = _mixer_c_fwd(xb, w, j, cos, sin, tag)
        x1, x1b, xh1, rs1 = _mm_ln_fwd(act, w_out, xs, w["ln_g"][i, 0][None], w["ln_b"][i, 0][None], name=f"ln1_{tag}",
                                       comm=gathers.get(f"ln1_{tag}"))
        if i == 0:
            mkv = _mm(mem, w["mem_w_kv"], out_dtype=BF16, tm=MEM_LEN, name="mm_memkv")
        q = _mm(x1b, w["x_w_q"][i], out_dtype=BF16, name=f"mm_xq_{tag}")
        o = _xattn_fwd(q, mkv, name=f"xattn_fwd_{tag}")
        x2, x2b, xh2, rs2 = _mm_ln_fwd(o, w["x_w_o"][i], x1, w["ln_g"][i, 1][None], w["ln_b"][i, 1][None], name=f"ln2_{tag}")
        w_up = w["f_w_up"][i]
        cwp, cbp = _ffn_perm(w["f_conv_w"][i]), _ffn_perm(w["f_conv_b"][i][None])
        hh = _mm(x2b, w_up, out_dtype=BF16, tm=2 * MM_T, tn=FFN_TC, name=f"mm_up_{tag}", comm=gathers.get(f"mm_up_{tag}"))
        a = _ffn_act_fwd(hh, cwp, cbp, name=f"ffn_act_{tag}", comm=gathers.get(f"ffn_act_{tag}"))
        x3, x3b, xh3, rs3 = _mm_ln_fwd(a, w["f_w_down"][i], x2, w["ln_g"][i, 2][None], w["ln_b"][i, 2][None],
                                       name=f"ln3_{tag}", comm=gathers.get(f"ln3_{tag}"))
        saved.append((res, (xh1, rs1, x1b), (q, o, xh2, rs2, x2b), (w_up, cwp, cbp, hh, a, xh3, rs3)))
        xs, xb = x3, x3b

    dloss, loss = _loss_fwd(xs, target, name="loss")

    dmkv = None
    ln_dg = [[None] * 3 for _ in range(DEPTH)]
    ln_db = [[None] * 3 for _ in range(DEPTH)]
    for i in reversed(range(DEPTH)):
        kind, j = i % N_MIXERS, i // N_MIXERS
        tag = f"l{i}"
        res, (xh1, rs1, x1b), (q, o, xh2, rs2, x2b), (w_up, cwp, cbp, hh, a, xh3, rs3) = saved[i]
        if i == DEPTH - 1:
            dz3, dz3b, ln_dg[i][2], ln_db[i][2] = _ln_bwd(None, dloss, xh3, rs3, w["ln_g"][i, 2][None], name=f"ln3_bwd_{tag}")
        else:
            dz3, dz3b, ln_dg[i][2], ln_db[i][2] = _mm_ln_bwd(dz1, d_in, w_in, xh3, rs3, w["ln_g"][i, 2][None], name=f"ln3_bwd_{tag}")
        da = _mm(dz3b, w["f_w_down"][i], tb=True, out_dtype=BF16, name=f"mm_ddown_{tag}")
        grads["f_w_down"][i] = _mm(a, dz3b, ta=True, out_dtype=BF16, tm=FFN_TC, tk=GRAD_TK, name=f"mm_gdown_{tag}")
        later = scatter(_layer_big(i + 1), grads) if i + 1 < DEPTH else None
        dh, dcw, dcb = _ffn_act_bwd(hh, da, cwp, cbp, name=f"ffn_act_bwd_{tag}", comm=later)
        grads["f_conv_w"][i], grads["f_conv_b"][i] = _ffn_unperm(dcw), _ffn_unperm(dcb)[0]
        grads["f_w_up"][i] = _mm(x2b, dh, ta=True, out_dtype=BF16, tn=FFN_TC, tk=GRAD_TK, name=f"mm_gup_{tag}")

        dz2, dz2b, ln_dg[i][1], ln_db[i][1] = _mm_ln_bwd(dz3, dh, w_up, xh2, rs2, w["ln_g"][i, 1][None], name=f"ln2_bwd_{tag}")
        do = _mm(dz2b, w["x_w_o"][i], tb=True, out_dtype=BF16, name=f"mm_dxo_{tag}")
        grads["x_w_o"][i] = _mm(o, dz2b, ta=True, out_dtype=BF16, tk=GRAD_TK, name=f"mm_gxo_{tag}")
        dq, dmkv_i = _xattn_bwd(q, mkv, do, name=f"xattn_bwd_{tag}")
        dmkv = dmkv_i if dmkv is None else dmkv + dmkv_i
        if i == 0:
            g_mem = _mm(mem, dmkv, ta=True, out_dtype=BF16, tm=512, name="mm_gmemkv")
        grads["x_w_q"][i] = _mm(x1b, dq, ta=True, out_dtype=BF16, tk=GRAD_TK, name=f"mm_gxq_{tag}")

        dz1, dz1b, ln_dg[i][0], ln_db[i][0] = _mm_ln_bwd(dz2, dq, w["x_w_q"][i], xh1, rs1, w["ln_g"][i, 0][None], name=f"ln1_bwd_{tag}")
        if kind == 0:
            if i == 0:
                grads["mem_w_kv"] = g_mem
                done = lambda: scatter(_layer_big(0, mixer=False) + [("a_w_o", 0), ("mem_w_kv", None)], grads)
            else:
                done = lambda: None
            d_in, w_in = _mixer_a_bwd(dz1b, res, w, j, cos, sin, tag, grads, done)
        elif kind == 1:
            d_in, w_in = _mixer_b_bwd(dz1b, res, w, j, tag, grads)
        else:
            d_in, w_in = _mixer_c_bwd(dz1b, res, w, j, cos, sin, tag, grads)

    grad_x = _axpy(dz1, _mm(d_in, w_in, tb=True, name="mm_dx_l0"), name="grad_x")
    big = [n for n, _ in SHARDED[:N_BIG]]
    out = {n: (g if n in big else jnp.stack(g, axis=0)) for n, g in grads.items() if n not in ("ln_g", "ln_b")}
    out["ln_g"] = jnp.stack([jnp.concatenate(r, axis=0) for r in ln_dg], axis=0)
    out["ln_b"] = jnp.stack([jnp.concatenate(r, axis=0) for r in ln_db], axis=0)
    return loss, grad_x, out


def kernel(x, mem, a_w_qkv, a_sinks, a_w_o, b_w_in, b_conv_w, b_conv_b, b_w_rgate, b_b_rgate, b_w_igate, b_b_igate, b_lambda, b_w_o, c_w_down, c_q_norm, c_kv_norm, c_w_uq, c_w_ukv, c_w_o, mem_w_kv, x_w_q, x_w_o, f_w_up, f_conv_w, f_conv_b, f_w_down, ln_g, ln_b, loss_target, m_a_w_qkv, m_a_sinks, m_a_w_o, m_b_w_in, m_b_conv_w, m_b_conv_b, m_b_w_rgate, m_b_b_rgate, m_b_w_igate, m_b_b_igate, m_b_lambda, m_b_w_o, m_c_w_down, m_c_q_norm, m_c_kv_norm, m_c_w_uq, m_c_w_ukv, m_c_w_o, m_mem_w_kv, m_x_w_q, m_x_w_o, m_f_w_up, m_f_conv_w, m_f_conv_b, m_f_w_down, m_ln_g, m_ln_b, v_a_w_qkv, v_a_sinks, v_a_w_o, v_b_w_in, v_b_conv_w, v_b_conv_b, v_b_w_rgate, v_b_b_rgate, v_b_w_igate, v_b_b_igate, v_b_lambda, v_b_w_o, v_c_w_down, v_c_q_norm, v_c_kv_norm, v_c_w_uq, v_c_w_ukv, v_c_w_o, v_mem_w_kv, v_x_w_q, v_x_w_o, v_f_w_up, v_f_conv_w, v_f_conv_b, v_f_w_down, v_ln_g, v_ln_b):
    loc = locals()
    shard = {n: loc[n] for n in WEIGHTS}
    mom = {n: loc["m_" + n] for n in WEIGHTS}
    var = {n: loc["v_" + n] for n in WEIGHTS}
    names = [n for n, _ in SHARDED]
    axis = dict(SHARDED)
    big, small = names[:N_BIG], names[N_BIG:]

    chip = 2 * lax.axis_index("x") + lax.axis_index("y")
    n_layers = {n: shard[n].shape[0] for n in WEIGHTS if n != "mem_w_kv"}
    layer_axis = lambda n, j: axis[n] - (0 if j is None else 1)

    shard_b = {n: shard[n].astype(BF16) for n in big}
    w = {n: [None] * n_layers[n] for n in big if n != "mem_w_kv"}

    small_pack = _pack([shard[n] for n in small], F32)
    shard_b["small"] = small_pack

    def gather(pairs):
        def deliver(outs):
            for (n, j), o in zip(pairs, outs):
                if j is None:
                    w[n] = o
                else:
                    w[n][j] = o
        return _gather_comm([(shard_b[n], j, 0 if n == "small" else layer_axis(n, j), n == "f_w_up") for n, j in pairs], deliver)

    _run_comm(gather([("a_w_qkv", 0), ("small", None)]), name="gather_first")
    gathers = {
        "mm_qkv_l0": gather([("a_w_o", 0)]),
        "rope_qkv_l0": gather([("x_w_q", 0), ("mem_w_kv", None)]),
        "swa_fwd_l0": gather([("x_w_o", 0), ("f_w_up", 0)]),
        "ln1_l0": gather([("f_w_down", 0)]),
        "mm_up_l0": gather(_layer_big(1, rest=False) + [("x_w_q", 1), ("x_w_o", 1)]),
        "ffn_act_l0": gather([("f_w_up", 1)]),
        "ln3_l0": gather([("f_w_down", 1)]),
        "lru_fwd_l1": gather(_layer_big(2, rest=False) + [("x_w_q", 2), ("x_w_o", 2)]),
        "mm_up_l1": gather([("f_w_up", 2)]),
        "ffn_act_l1": gather([("f_w_down", 2)] + _layer_big(3, rest=False)),
        "mm_up_l2": gather([("f_w_up", 3)]),
        "ffn_act_l2": gather([("f_w_down", 3), ("x_w_q", 3), ("x_w_o", 3)]),
    }
    got = w.pop("small").reshape((4,) + small_pack.shape)
    per_chip = [_unpack(got[s], [shard[n].shape for n in small]) for s in range(4)]
    for k, n in enumerate(small):
        w[n] = jnp.concatenate([per_chip[s][k] for s in range(4)], axis=axis[n])
    for n in REPLICATED:
        w[n] = shard[n]

    recv = {}

    def scatter(pairs, grads):
        def deliver(outs):
            recv.update(dict(zip(pairs, outs)))
        return _scatter_comm([(grads[n] if j is None else grads[n][j], layer_axis(n, j), n == "f_w_up") for n, j in pairs], deliver)

    loss, grad_x, g = _local_step(x[0], mem[0], loss_target[0], w, n_layers, gathers, scatter)
    _run_comm(scatter([("a_w_qkv", 0)], g), name="scatter_last")

    view = {n: (int(np.prod(shard[n].shape[:-1])), shard[n].shape[-1]) for n in big}
    parts = []
    for n in big:
        layers = n_layers.get(n, 1)
        part = None
        for j in range(layers):
            r = recv[(n, j if n in n_layers else None)]
            part = _sum_partials(r.reshape(4, view[n][0] // layers, view[n][1]), part, j, layers, name=f"sum_{n}_{j}")
        parts.append(part)
    sibs = _swap_cores(parts, name="swap_cores")
    grad_o, delta_o, m_o, v_o = {}, {}, {}, {}
    for n, part, sib in zip(big, parts, sibs):
        res = _adamw(shard[n].reshape(view[n]), mom[n].reshape(view[n]), var[n].reshape(view[n]), part, sib, name=f"adamw_{n}")
        for d, r in zip((grad_o, delta_o, m_o, v_o), res):
            d[n] = r.reshape(shard[n].shape)

    rest = small + REPLICATED
    vec = _pack([g[n] for n in rest] + [loss], F32, cols=LANES, row_mult=8)
    tot = _unpack(_all_reduce_small(vec, name="allreduce_small"), [g[n].shape for n in rest] + [(1, 1)], cols=LANES)
    loss_tot = tot[-1].reshape(())
    mine = {n: t for n, t in zip(rest, tot)}
    for n in small:
        size = shard[n].shape[axis[n]]
        mine[n] = lax.dynamic_slice_in_dim(mine[n], chip * size, size, axis=axis[n])
    rpack = lambda d: _pack([d[n] for n in rest], F32, cols=LANES, row_mult=8)
    res = _adamw(rpack(shard), rpack(mom), rpack(var), rpack(mine), None, name="adamw_small")
    for d, r in zip((grad_o, delta_o, m_o, v_o), res):
        d.update(dict(zip(rest, _unpack(r, [shard[n].shape for n in rest], cols=LANES))))

    return (loss_tot, grad_x[None], *[grad_o[n] for n in WEIGHTS], *[delta_o[n] for n in WEIGHTS],
            *[m_o[n] for n in WEIGHTS], *[v_o[n] for n in WEIGHTS])
```

```python
import functools
import math

import numpy as np
import jax
import jax.numpy as jnp
from jax import lax
from jax.experimental import pallas as pl
from jax.experimental.pallas import tpu as pltpu

F32 = jnp.float32
BF16 = jnp.bfloat16
MESH = pl.DeviceIdType.MESH

D_MODEL = 1024
DEPTH = 4
N_MIXERS = 3
MEM_LEN = 256
BLOCK = 128
ROPE_THETA = 10000.0
NEG = -1e30
LN_EPS = 1e-5
RMS_EPS = 1e-6
A_HEADS, A_KV_HEADS, A_HEAD_DIM = 16, 4, 64
LRU_BLOCKS, LRU_BLOCK_W, LRU_CONV, LRU_C = 4, 256, 4, 8.0
C_HEADS, C_NOPE, C_ROPE, C_V, C_Q_RANK, C_KV_RANK = 8, 128, 64, 128, 384, 256
X_HEADS, X_HEAD_DIM = 4, 256
D_FF, FFN_CONV = 2816, 3
ALPHA = (2.0 * DEPTH) ** 0.25
ADAM_LR, ADAM_B1, ADAM_B2, ADAM_EPS, ADAM_WD, ADAM_STEP = 0.001, 0.9, 0.999, 1e-08, 0.01, 10

VMEM_LIMIT = 56 * 2 ** 20
LANES = 128
PACK_COLS = 1024
ROW_T = 512
ACT_T = 256
LRU_T = 256
FLASH_T = 512
FFN_TC = 1408
MM_T = 1024
GRAD_TK = 2048

C11 = (((1,), (1,)), ((), ()))
C00 = (((0,), (0,)), ((), ()))

SHARDED = [
    ("a_w_qkv", 2), ("a_w_o", 1), ("b_w_in", 2), ("b_w_rgate", 2), ("b_w_igate", 2), ("b_w_o", 1), ("c_w_down", 1),
    ("c_w_uq", 2), ("c_w_ukv", 2), ("c_w_o", 1), ("mem_w_kv", 1), ("x_w_q", 1), ("x_w_o", 1), ("f_w_up", 2),
    ("f_w_down", 1),
    ("b_conv_w", 2), ("c_q_norm", 1), ("c_kv_norm", 1), ("f_conv_w", 2), ("ln_g", 2), ("ln_b", 2),
]
N_BIG = 15
REPLICATED = ["a_sinks", "b_conv_b", "b_b_rgate", "b_b_igate", "b_lambda", "f_conv_b"]
WEIGHTS = ["a_w_qkv", "a_sinks", "a_w_o", "b_w_in", "b_conv_w", "b_conv_b", "b_w_rgate", "b_b_rgate", "b_w_igate",
           "b_b_igate", "b_lambda", "b_w_o", "c_w_down", "c_q_norm", "c_kv_norm", "c_w_uq", "c_w_ukv", "c_w_o",
           "mem_w_kv", "x_w_q", "x_w_o", "f_w_up", "f_conv_w", "f_conv_b", "f_w_down", "ln_g", "ln_b"]


def _params(*sem):
    return pltpu.CompilerParams(dimension_semantics=sem, vmem_limit_bytes=VMEM_LIMIT)


def _sds(shape, dtype):
    return jax.ShapeDtypeStruct(tuple(shape), dtype)


def _mm(a, b, *, name, ta=False, tb=False, out_dtype=F32, tm=None, tn=None, tk=None, comm=None):
    (K, M) = a.shape if ta else a.shape[::-1]
    (N, K2) = b.shape if tb else b.shape[::-1]
    assert K == K2, (a.shape, b.shape, ta, tb)
    tm = min(tm or MM_T, M)
    tn = min(tn or N, N)
    tk = min(tk or K, K)
    assert M % tm == 0 and N % tn == 0 and K % tk == 0, (M, N, K, tm, tn, tk)
    nk = K // tk
    use_acc = nk > 1 and out_dtype != F32
    dims = (((0 if ta else 1,), (1 if tb else 0,)), ((), ()))

    def body(a_ref, b_ref, o_ref, *scratch):
        p = lax.dot_general(a_ref[...].astype(BF16), b_ref[...].astype(BF16), dims, preferred_element_type=F32)
        if nk == 1:
            o_ref[...] = p.astype(out_dtype)
        else:
            acc = scratch[0] if use_acc else o_ref
            k = pl.program_id(2)

            @pl.when(k == 0)
            def _():
                acc[...] = p

            @pl.when(k > 0)
            def _():
                acc[...] += p

            if use_acc:
                @pl.when(k == nk - 1)
                def _():
                    o_ref[...] = acc[...].astype(out_dtype)

    a_spec = pl.BlockSpec((tk, tm), lambda i, j, k: (k, i)) if ta else pl.BlockSpec((tm, tk), lambda i, j, k: (i, k))
    b_spec = pl.BlockSpec((tn, tk), lambda i, j, k: (j, k)) if tb else pl.BlockSpec((tk, tn), lambda i, j, k: (k, j))
    return _call(
        body, name=name, grid=(M // tm, N // tn, nk), in_specs=[a_spec, b_spec],
        out_specs=[pl.BlockSpec((tm, tn), lambda i, j, k: (i, j))], out_shape=[_sds((M, N), out_dtype)],
        scratch_shapes=[pltpu.VMEM((tm, tn), F32)] if use_acc else [],
        params=_params("parallel", "parallel", "arbitrary"), comm=comm,
    )(a, b)[0]


def _shift_down(cur, prev8, d):
    rolled = pltpu.roll(cur, d, 0)
    rid = lax.broadcasted_iota(jnp.int32, prev8.shape, 0)
    head = jnp.where(rid < d, pltpu.roll(prev8, d, 0), rolled[0:8])
    return jnp.concatenate([head, rolled[8:]], axis=0)


def _shift_up(cur, next8, d):
    n = cur.shape[0]
    rolled = pltpu.roll(cur, n - d, 0)
    rid = lax.broadcasted_iota(jnp.int32, next8.shape, 0)
    tail = jnp.where(rid >= 8 - d, pltpu.roll(next8, 8 - d, 0), rolled[n - 8:n])
    return jnp.concatenate([rolled[0:n - 8], tail], axis=0)


def _swap_halves(x):
    w = x.shape[-1]
    if w == 64:
        return jnp.concatenate([x[:, 32:64], x[:, 0:32]], axis=1)
    lane = lax.broadcasted_iota(jnp.int32, x.shape, 1)
    return jnp.where((lane % 64) < 32, pltpu.roll(x, w - 32, 1), pltpu.roll(x, 32, 1))


def _tile_lanes(t, w):
    return t if w == t.shape[-1] else jnp.concatenate([t] * (w // t.shape[-1]), axis=1)


def _rope(x, cos, sin):
    w = x.shape[-1]
    if w == 64:
        cos, sin = cos[:, :64], sin[:, :64]
    else:
        cos, sin = _tile_lanes(cos, w), _tile_lanes(sin, w)
    return x * cos + _swap_halves(x) * sin


def _rope_t(x, cos, sin):
    w = x.shape[-1]
    if w == 64:
        cos, sin = cos[:, :64], sin[:, :64]
    else:
        cos, sin = _tile_lanes(cos, w), _tile_lanes(sin, w)
    return x * cos - _swap_halves(x) * sin


def _sigmoid(x):
    return 1.0 / (1.0 + jnp.exp(-x))


def _gelu_and_grad(x):
    c0, c1 = math.sqrt(2.0 / math.pi), 0.044715
    t = jnp.tanh(c0 * (x + c1 * x * x * x))
    g = 0.5 * x * (1.0 + t)
    dg = 0.5 * (1.0 + t) + 0.5 * x * (1.0 - t * t) * c0 * (1.0 + 3.0 * c1 * x * x)
    return g, dg


def _neg_expm1(x):
    series = -x * (1.0 + x * (0.5 + x * (1.0 / 6.0 + x * (1.0 / 24.0 + x * (1.0 / 120.0)))))
    return jnp.where(x > -0.1, series, 1.0 - jnp.exp(x))


def _softplus_neg(lam):
    z = -lam
    e = jnp.exp(-jnp.abs(z))
    log1p = jnp.where(e < 0.01, e * (1.0 - e * (0.5 - e * (1.0 / 3.0))), jnp.log(1.0 + e))
    sp = jnp.maximum(z, 0.0) + log1p
    dsp = -_sigmoid(z)
    return sp, dsp


def _ln_fwd(x, y, g, b, *, name):
    S, D = x.shape
    tm = min(ROW_T, S)

    def body(x_ref, y_ref, g_ref, b_ref, o_ref, ob_ref, xh_ref, rs_ref):
        z = ALPHA * x_ref[...] + y_ref[...]
        mu = jnp.mean(z, axis=-1, keepdims=True)
        zc = z - mu
        var = jnp.mean(zc * zc, axis=-1, keepdims=True)
        r = lax.rsqrt(var + LN_EPS)
        xh = zc * r
        o = xh * g_ref[...] + b_ref[...]
        o_ref[...] = o
        ob_ref[...] = o.astype(BF16)
        xh_ref[...] = xh
        rs_ref[...] = r

    row = pl.BlockSpec((tm, D), lambda i: (i, 0))
    vec = pl.BlockSpec((1, D), lambda i: (0, 0))
    return pl.pallas_call(
        body, name=name, grid=(S // tm,), in_specs=[row, row, vec, vec],
        out_specs=[row, row, row, pl.BlockSpec((tm, 1), lambda i: (i, 0))],
        out_shape=[_sds((S, D), F32), _sds((S, D), BF16), _sds((S, D), F32), _sds((S, 1), F32)],
        compiler_params=_params("parallel"),
    )(x, y, g, b)


def _ln_bwd(d1, d2, xh, rs, g, *, name):
    S, D = xh.shape
    tm = min(ROW_T, S)
    has_d1 = d1 is not None

    def body(*refs):
        if has_d1:
            d1_ref, d2_ref, xh_ref, rs_ref, g_ref, dz_ref, dzb_ref, dg_ref, db_ref = refs
            dout = ALPHA * d1_ref[...] + d2_ref[...]
        else:
            d2_ref, xh_ref, rs_ref, g_ref, dz_ref, dzb_ref, dg_ref, db_ref = refs
            dout = d2_ref[...]
        xh_v = xh_ref[...]
        dxh = dout * g_ref[...]
        m1 = jnp.mean(dxh, axis=-1, keepdims=True)
        m2 = jnp.mean(dxh * xh_v, axis=-1, keepdims=True)
        dz = rs_ref[...] * (dxh - m1 - xh_v * m2)
        dz_ref[...] = dz
        dzb_ref[...] = dz.astype(BF16)

        @pl.when(pl.program_id(0) == 0)
        def _():
            dg_ref[...] = jnp.zeros_like(dg_ref)
            db_ref[...] = jnp.zeros_like(db_ref)

        dg_ref[...] += jnp.sum(dout * xh_v, axis=0, keepdims=True)
        db_ref[...] += jnp.sum(dout, axis=0, keepdims=True)

    row = pl.BlockSpec((tm, D), lambda i: (i, 0))
    vec = pl.BlockSpec((1, D), lambda i: (0, 0))
    ins = ([row] if has_d1 else []) + [row, row, pl.BlockSpec((tm, 1), lambda i: (i, 0)), vec]
    args = ([d1] if has_d1 else []) + [d2, xh, rs, g]
    return pl.pallas_call(
        body, name=name, grid=(S // tm,), in_specs=ins, out_specs=[row, row, vec, vec],
        out_shape=[_sds((S, D), F32), _sds((S, D), BF16), _sds((1, D), F32), _sds((1, D), F32)],
        compiler_params=_params("arbitrary"),
    )(*args)


def _mm_ln_fwd(a, b, x, g, beta, *, name, comm=None):
    S, K = a.shape
    D = b.shape[1]
    tm = min(ROW_T, S)

    def body(a_ref, b_ref, x_ref, g_ref, beta_ref, o_ref, ob_ref, xh_ref, rs_ref):
        y = jnp.dot(a_ref[...].astype(BF16), b_ref[...].astype(BF16), preferred_element_type=F32)
        z = ALPHA * x_ref[...] + y
        mu = jnp.mean(z, axis=-1, keepdims=True)
        zc = z - mu
        var = jnp.mean(zc * zc, axis=-1, keepdims=True)
        r = lax.rsqrt(var + LN_EPS)
        xh = zc * r
        o = xh * g_ref[...] + beta_ref[...]
        o_ref[...] = o
        ob_ref[...] = o.astype(BF16)
        xh_ref[...] = xh
        rs_ref[...] = r

    row = pl.BlockSpec((tm, D), lambda i: (i, 0))
    vec = pl.BlockSpec((1, D), lambda i: (0, 0))
    return _call(
        body, name=name, grid=(S // tm,),
        in_specs=[pl.BlockSpec((tm, K), lambda i: (i, 0)), pl.BlockSpec((K, D), lambda i: (0, 0)), row, vec, vec],
        out_specs=[row, row, row, pl.BlockSpec((tm, 1), lambda i: (i, 0))],
        out_shape=[_sds((S, D), F32), _sds((S, D), BF16), _sds((S, D), F32), _sds((S, 1), F32)],
        params=_params("arbitrary"), comm=comm,
    )(a, b, x, g, beta)


def _mm_ln_bwd(d1, da, wt, xh, rs, g, *, name):
    S, D = xh.shape
    K = da.shape[1]
    tm = min(ROW_T // 2, S)

    def body(d1_ref, da_ref, wt_ref, xh_ref, rs_ref, g_ref, dz_ref, dzb_ref, dg_ref, db_ref):
        d2 = lax.dot_general(da_ref[...].astype(BF16), wt_ref[...].astype(BF16), C11, preferred_element_type=F32)
        dout = ALPHA * d1_ref[...] + d2
        xh_v = xh_ref[...]
        dxh = dout * g_ref[...]
        m1 = jnp.mean(dxh, axis=-1, keepdims=True)
        m2 = jnp.mean(dxh * xh_v, axis=-1, keepdims=True)
        dz = rs_ref[...] * (dxh - m1 - xh_v * m2)
        dz_ref[...] = dz
        dzb_ref[...] = dz.astype(BF16)

        @pl.when(pl.program_id(0) == 0)
        def _():
            dg_ref[...] = jnp.zeros_like(dg_ref)
            db_ref[...] = jnp.zeros_like(db_ref)

        dg_ref[...] += jnp.sum(dout * xh_v, axis=0, keepdims=True)
        db_ref[...] += jnp.sum(dout, axis=0, keepdims=True)

    row = pl.BlockSpec((tm, D), lambda i: (i, 0))
    vec = pl.BlockSpec((1, D), lambda i: (0, 0))
    return pl.pallas_call(
        body, name=name, grid=(S // tm,),
        in_specs=[row, pl.BlockSpec((tm, K), lambda i: (i, 0)), pl.BlockSpec((D, K), lambda i: (0, 0)), row,
                  pl.BlockSpec((tm, 1), lambda i: (i, 0)), vec],
        out_specs=[row, row, vec, vec],
        out_shape=[_sds((S, D), F32), _sds((S, D), BF16), _sds((1, D), F32), _sds((1, D), F32)],
        compiler_params=_params("arbitrary"),
    )(d1, da, wt, xh, rs, g)


def _loss_fwd(y, target, *, name):
    S, D = y.shape
    tm = min(ROW_T, S)

    def body(y_ref, t_ref, d_ref, l_ref):
        e = y_ref[...] - t_ref[...]
        d_ref[...] = e * (1.0 / D)

        @pl.when(pl.program_id(0) == 0)
        def _():
            l_ref[...] = jnp.zeros_like(l_ref)

        part = jnp.sum(e * e, axis=0, keepdims=True)
        l_ref[...] += (0.5 / D) * jnp.sum(part, axis=1, keepdims=True)

    row = pl.BlockSpec((tm, D), lambda i: (i, 0))
    return pl.pallas_call(
        body, name=name, grid=(S // tm,), in_specs=[row, row],
        out_specs=[row, pl.BlockSpec((1, 1), lambda i: (0, 0))], out_shape=[_sds((S, D), F32), _sds((1, 1), F32)],
        compiler_params=_params("arbitrary"),
    )(y, target)


def _axpy(d1, d2, *, name):
    S, D = d1.shape
    tm = min(ROW_T, S)

    def body(a_ref, b_ref, o_ref):
        o_ref[...] = ALPHA * a_ref[...] + b_ref[...]

    row = pl.BlockSpec((tm, D), lambda i: (i, 0))
    return pl.pallas_call(body, name=name, grid=(S // tm,), in_specs=[row, row], out_specs=row,
                          out_shape=_sds((S, D), F32), compiler_params=_params("parallel"))(d1, d2)


def _ffn_act_fwd(h, cw, cb, *, name, comm=None):
    S, W = h.shape
    tc = FFN_TC
    nj = W // (2 * tc)
    tm = min(ACT_T, S)

    def body(h_ref, w_ref, b_ref, a_ref, carry):
        @pl.when(pl.program_id(1) == 0)
        def _():
            carry[...] = jnp.zeros_like(carry)

        cur = h_ref[...].astype(F32)
        prev8 = carry[...]
        hc = cur * w_ref[2:3, :] + _shift_down(cur, prev8, 1) * w_ref[1:2, :] + _shift_down(cur, prev8, 2) * w_ref[0:1, :]
        hc = hc + b_ref[...]
        carry[...] = cur[tm - 8:tm]
        hg, hu = hc[:, :tc], hc[:, tc:]
        a_ref[...] = (hg * _sigmoid(hg) * hu).astype(BF16)

    return _call(
        body, name=name, grid=(nj, S // tm),
        in_specs=[pl.BlockSpec((tm, 2 * tc), lambda j, i: (i, j)), pl.BlockSpec((3, 2 * tc), lambda j, i: (0, j)),
                  pl.BlockSpec((1, 2 * tc), lambda j, i: (0, j))],
        out_specs=[pl.BlockSpec((tm, tc), lambda j, i: (i, j))], out_shape=[_sds((S, W // 2), BF16)],
        scratch_shapes=[pltpu.VMEM((8, 2 * tc), F32)],
        params=_params("parallel", "arbitrary"), comm=comm,
    )(h, cw, cb)[0]


def _ffn_act_bwd(h, da, cw, cb, *, name, comm=None):
    S, W = h.shape
    tc = FFN_TC
    nj = W // (2 * tc)
    tm = min(ACT_T, S)
    ni = S // tm

    def body(h_ref, hp_ref, da_ref, w_ref, b_ref, dh_ref, dw_ref, db_ref, carry):
        i = pl.program_id(1)
        r = ni - 1 - i

        @pl.when(i == 0)
        def _():
            carry[...] = jnp.zeros_like(carry)
            dw_ref[...] = jnp.zeros_like(dw_ref)
            db_ref[...] = jnp.zeros_like(db_ref)

        cur = h_ref[...].astype(F32)
        prev8 = jnp.where(r > 0, hp_ref[8:16, :].astype(F32), 0.0)
        sh = [cur, _shift_down(cur, prev8, 1), _shift_down(cur, prev8, 2)]
        hc = sh[0] * w_ref[2:3, :] + sh[1] * w_ref[1:2, :] + sh[2] * w_ref[0:1, :] + b_ref[...]
        hg, hu = hc[:, :tc], hc[:, tc:]
        d = da_ref[...].astype(F32)
        sg = _sigmoid(hg)
        dg = d * hu * (sg * (1.0 + hg * (1.0 - sg)))
        du = d * (hg * sg)
        dhc = jnp.concatenate([dg, du], axis=1)
        db_ref[...] += jnp.sum(dhc, axis=0, keepdims=True)
        for k in range(3):
            dw_ref[k:k + 1, :] += jnp.sum(dhc * sh[2 - k], axis=0, keepdims=True)
        next8 = carry[...]
        dh = dhc * w_ref[2:3, :] + _shift_up(dhc, next8, 1) * w_ref[1:2, :] + _shift_up(dhc, next8, 2) * w_ref[0:1, :]
        carry[...] = dhc[0:8]
        dh_ref[...] = dh.astype(BF16)

    rev = lambda j, i: (ni - 1 - i, j)
    return _call(
        body, name=name, grid=(nj, ni),
        in_specs=[pl.BlockSpec((tm, 2 * tc), rev),
                  pl.BlockSpec((16, 2 * tc), lambda j, i: (jnp.maximum((ni - 1 - i) * (tm // 16) - 1, 0), j)),
                  pl.BlockSpec((tm, tc), rev), pl.BlockSpec((3, 2 * tc), lambda j, i: (0, j)),
                  pl.BlockSpec((1, 2 * tc), lambda j, i: (0, j))],
        out_specs=[pl.BlockSpec((tm, 2 * tc), rev), pl.BlockSpec((3, 2 * tc), lambda j, i: (0, j)),
                   pl.BlockSpec((1, 2 * tc), lambda j, i: (0, j))],
        out_shape=[_sds((S, W), BF16), _sds((3, W), F32), _sds((1, W), F32)],
        scratch_shapes=[pltpu.VMEM((8, 2 * tc), F32)],
        params=_params("parallel", "arbitrary"), comm=comm,
    )(h, h, da, cw, cb)


def _xattn_softmax(qk):
    s = qk * (X_HEAD_DIM ** -0.5)
    p = jnp.exp(s - jnp.max(s, axis=-1, keepdims=True))
    return p / jnp.sum(p, axis=-1, keepdims=True)


def _xattn_fwd(q, mkv, *, name):
    S, D = q.shape
    tm = min(ROW_T, S)

    def body(q_ref, k_ref, v_ref, o_ref):
        sls = [slice(h * X_HEAD_DIM, (h + 1) * X_HEAD_DIM) for h in range(X_HEADS)]
        scores = lambda h: lax.dot_general(q_ref[:, sls[h]], k_ref[:, sls[h]], C11, preferred_element_type=F32)
        nxt = scores(0)
        for h in range(X_HEADS):
            qk = nxt
            if h + 1 < X_HEADS:
                nxt = scores(h + 1)
            p = _xattn_softmax(qk)
            o_ref[:, sls[h]] = jnp.dot(p.astype(BF16), v_ref[:, sls[h]], preferred_element_type=F32).astype(BF16)

    return pl.pallas_call(
        body, name=name, grid=(S // tm,),
        in_specs=[pl.BlockSpec((tm, D), lambda i: (i, 0)), pl.BlockSpec((MEM_LEN, D), lambda i: (0, 0)),
                  pl.BlockSpec((MEM_LEN, D), lambda i: (0, 1))],
        out_specs=pl.BlockSpec((tm, D), lambda i: (i, 0)), out_shape=_sds((S, D), BF16),
        compiler_params=_params("parallel"),
    )(q, mkv, mkv)


def _xattn_bwd(q, mkv, do, *, name):
    S, D = q.shape
    tm = min(ROW_T, S)
    scale = X_HEAD_DIM ** -0.5

    def body(q_ref, k_ref, v_ref, do_ref, dq_ref, dkv_ref):
        @pl.when(pl.program_id(0) == 0)
        def _():
            dkv_ref[...] = jnp.zeros_like(dkv_ref)

        def products(h):
            sl = slice(h * X_HEAD_DIM, (h + 1) * X_HEAD_DIM)
            return (lax.dot_general(q_ref[:, sl], k_ref[:, sl], C11, preferred_element_type=F32),
                    lax.dot_general(do_ref[:, sl], v_ref[:, sl], C11, preferred_element_type=F32))

        nxt = products(0)
        for h in range(X_HEADS):
            qk, dp = nxt
            if h + 1 < X_HEADS:
                nxt = products(h + 1)
            sl = slice(h * X_HEAD_DIM, (h + 1) * X_HEAD_DIM)
            sv = slice(D + h * X_HEAD_DIM, D + (h + 1) * X_HEAD_DIM)
            qh, kh, doh = q_ref[:, sl], k_ref[:, sl], do_ref[:, sl]
            p = _xattn_softmax(qk)
            ds = (p * (dp - jnp.sum(p * dp, axis=-1, keepdims=True)) * scale).astype(BF16)
            dq_ref[:, sl] = jnp.dot(ds, kh, preferred_element_type=F32).astype(BF16)
            dkv_ref[:, sl] += lax.dot_general(ds, qh, C00, preferred_element_type=F32)
            dkv_ref[:, sv] += lax.dot_general(p.astype(BF16), doh, C00, preferred_element_type=F32)

    row = pl.BlockSpec((tm, D), lambda i: (i, 0))
    return pl.pallas_call(
        body, name=name, grid=(S // tm,),
        in_specs=[row, pl.BlockSpec((MEM_LEN, D), lambda i: (0, 0)), pl.BlockSpec((MEM_LEN, D), lambda i: (0, 1)), row],
        out_specs=[row, pl.BlockSpec((MEM_LEN, 2 * D), lambda i: (0, 0))],
        out_shape=[_sds((S, D), BF16), _sds((MEM_LEN, 2 * D), F32)],
        compiler_params=_params("arbitrary"),
    )(q, mkv, mkv, do)


def _rope_cols(x, cos, sin, n_rope, *, name, comm=None):
    S, W = x.shape
    tm = min(ROW_T, S)

    def body(x_ref, c_ref, s_ref, o_ref):
        o_ref[:, :n_rope] = _rope(x_ref[:, :n_rope], c_ref[...], s_ref[...]).astype(BF16)
        if n_rope < W:
            o_ref[:, n_rope:] = x_ref[:, n_rope:].astype(BF16)

    row = pl.BlockSpec((tm, W), lambda i: (i, 0))
    tab = pl.BlockSpec((tm, LANES), lambda i: (i, 0))
    return _call(body, name=name, grid=(S // tm,), in_specs=[row, tab, tab], out_specs=[row],
                 out_shape=[_sds((S, W), BF16)], params=_params("arbitrary"), comm=comm)(x, cos, sin)[0]


def _swa_band(n, stacked):
    qi = jnp.bitwise_and(lax.broadcasted_iota(jnp.int32, (stacked * BLOCK, 2 * BLOCK), 0), BLOCK - 1)
    kj = lax.broadcasted_iota(jnp.int32, (stacked * BLOCK, 2 * BLOCK), 1)
    first = jnp.where(n > 0, 0, BLOCK)
    return ((kj < BLOCK) & (kj > qi + first)) | ((kj >= BLOCK) & (kj - BLOCK <= qi))


def _swa_sink_rows(sink_ref, heads):
    row = lax.broadcasted_iota(jnp.int32, (len(heads) * BLOCK, 1), 0)
    col = jnp.full(row.shape, sink_ref[heads[-1]], F32)
    for gi in range(len(heads) - 2, -1, -1):
        col = jnp.where(row < (gi + 1) * BLOCK, sink_ref[heads[gi]], col)
    return col


def _swa_softmax(qk, band, sink):
    s = jnp.where(band, qk * (A_HEAD_DIM ** -0.5), NEG)
    m = jnp.maximum(jnp.max(s, axis=-1, keepdims=True), sink)
    p = jnp.exp(s - m)
    e_sink = jnp.exp(sink - m)
    den = jnp.sum(p, axis=-1, keepdims=True) + e_sink
    return p / den, e_sink / den


def _swa_specs():
    nq, nkv = A_HEADS * A_HEAD_DIM, A_KV_HEADS * A_HEAD_DIM
    kb, vb = nq // nkv, nq // nkv + 1
    prev = lambda n: jnp.maximum(n - 1, 0)
    return [pl.BlockSpec((BLOCK, nq), lambda n: (n, 0)),
            pl.BlockSpec((BLOCK, nkv), lambda n: (n, kb)), pl.BlockSpec((BLOCK, nkv), lambda n: (prev(n), kb)),
            pl.BlockSpec((BLOCK, nkv), lambda n: (n, vb)), pl.BlockSpec((BLOCK, nkv), lambda n: (prev(n), vb)),
            pl.BlockSpec(memory_space=pltpu.SMEM)]


def _swa_fwd(qkv, sinks, *, name, comm=None):
    S = qkv.shape[0]
    hd, grp = A_HEAD_DIM, A_HEADS // A_KV_HEADS

    def body(q_ref, kc_ref, kp_ref, vc_ref, vp_ref, sink_ref, o_ref):
        band = _swa_band(pl.program_id(0), grp)
        qa, kc, kp, vc, vp = q_ref[...], kc_ref[...], kp_ref[...], vc_ref[...], vp_ref[...]
        def products(hk):
            ks = slice(hk * hd, (hk + 1) * hd)
            k = jnp.concatenate([kp[:, ks], kc[:, ks]], axis=0)
            v = jnp.concatenate([vp[:, ks], vc[:, ks]], axis=0)
            q = jnp.concatenate([qa[:, h * hd:(h + 1) * hd] for h in range(hk * grp, (hk + 1) * grp)], axis=0)
            return v, lax.dot_general(q, k, C11, preferred_element_type=F32)

        nxt = products(0)
        for hk in range(A_KV_HEADS):
            v, s = nxt
            if hk + 1 < A_KV_HEADS:
                nxt = products(hk + 1)
            heads = [hk * grp + gi for gi in range(grp)]
            p, _ = _swa_softmax(s, band, _swa_sink_rows(sink_ref, heads))
            o = jnp.dot(p.astype(BF16), v, preferred_element_type=F32).astype(BF16)
            for gi, h in enumerate(heads):
                o_ref[:, h * hd:(h + 1) * hd] = o[gi * BLOCK:(gi + 1) * BLOCK]

    return _call(
        body, name=name, grid=(S // BLOCK,), in_specs=_swa_specs(),
        out_specs=[pl.BlockSpec((BLOCK, A_HEADS * hd), lambda n: (n, 0))], out_shape=[_sds((S, A_HEADS * hd), BF16)],
        params=_params("arbitrary"), comm=comm,
    )(qkv, qkv, qkv, qkv, qkv, sinks)[0]


def _swa_bwd(qkv, sinks, do, cos, sin, *, name, comm=None):
    S = qkv.shape[0]
    hd, grp = A_HEAD_DIM, A_HEADS // A_KV_HEADS
    nq, nkv = A_HEADS * hd, A_KV_HEADS * hd
    scale = hd ** -0.5

    def body(q_ref, kc_ref, kp_ref, vc_ref, vp_ref, sink_ref, do_ref, c_ref, s_ref, dq_ref, dc_ref, dp_ref, ds_ref, dq_s):
        @pl.when(pl.program_id(0) == 0)
        def _():
            ds_ref[...] = jnp.zeros_like(ds_ref)

        band = _swa_band(pl.program_id(0), grp)
        lane = lax.broadcasted_iota(jnp.int32, (1, LANES), 1)
        qa, kc, kp, vc, vp, doa = q_ref[...], kc_ref[...], kp_ref[...], vc_ref[...], vp_ref[...], do_ref[...]
        dsink = jnp.zeros((1, LANES), F32)
        def products(hk):
            ks = slice(hk * hd, (hk + 1) * hd)
            k = jnp.concatenate([kp[:, ks], kc[:, ks]], axis=0)
            v = jnp.concatenate([vp[:, ks], vc[:, ks]], axis=0)
            q = jnp.concatenate([qa[:, h * hd:(h + 1) * hd] for h in range(hk * grp, (hk + 1) * grp)], axis=0)
            dog = jnp.concatenate([doa[:, h * hd:(h + 1) * hd] for h in range(hk * grp, (hk + 1) * grp)], axis=0)
            return (k, q, dog, lax.dot_general(q, k, C11, preferred_element_type=F32),
                    lax.dot_general(dog, v, C11, preferred_element_type=F32))

        nxt = products(0)
        for hk in range(A_KV_HEADS):
            k, q, dog, s, dpr = nxt
            if hk + 1 < A_KV_HEADS:
                nxt = products(hk + 1)
            ks = slice(hk * hd, (hk + 1) * hd)
            heads = [hk * grp + gi for gi in range(grp)]
            p, p_sink = _swa_softmax(s, band, _swa_sink_rows(sink_ref, heads))
            delta = jnp.sum(p * dpr, axis=-1, keepdims=True)
            dsc = (p * (dpr - delta) * scale).astype(BF16)
            dqg = jnp.dot(dsc, k, preferred_element_type=F32)
            dk = lax.dot_general(dsc, q, C00, preferred_element_type=F32)
            dv = lax.dot_general(p.astype(BF16), dog, C00, preferred_element_type=F32)
            sink_term = p_sink * delta
            for gi, h in enumerate(heads):
                rows = slice(gi * BLOCK, (gi + 1) * BLOCK)
                dq_s[:, h * hd:(h + 1) * hd] = dqg[rows]
                dsink = dsink + jnp.where(lane == h, -jnp.sum(sink_term[rows], axis=0, keepdims=True), 0.0)
            dp_ref[:, ks] = dk[:BLOCK]
            dc_ref[:, ks] = dk[BLOCK:]
            dp_ref[:, nkv + hk * hd:nkv + (hk + 1) * hd] = dv[:BLOCK]
            dc_ref[:, nkv + hk * hd:nkv + (hk + 1) * hd] = dv[BLOCK:]
        ds_ref[...] += dsink
        dq_ref[...] = _rope_t(dq_s[...], c_ref[...], s_ref[...]).astype(BF16)

    tab = pl.BlockSpec((BLOCK, LANES), lambda n: (n, 0))
    blk = lambda w: pl.BlockSpec((BLOCK, w), lambda n: (n, 0))
    return _call(
        body, name=name, grid=(S // BLOCK,), in_specs=_swa_specs() + [blk(nq), tab, tab],
        out_specs=[blk(nq), blk(2 * nkv), blk(2 * nkv), pl.BlockSpec((1, LANES), lambda n: (0, 0))],
        out_shape=[_sds((S, nq), BF16), _sds((S, 2 * nkv), F32), _sds((S, 2 * nkv), F32), _sds((1, LANES), F32)],
        scratch_shapes=[pltpu.VMEM((BLOCK, nq), F32)],
        params=_params("arbitrary"), comm=comm,
    )(qkv, qkv, qkv, qkv, qkv, sinks, do, cos, sin)


def _swa_dqkv(dq, dcur, dprev, cos, sin, *, name):
    S, nq = dq.shape
    nkv = dcur.shape[1] // 2
    nb = S // BLOCK

    def body(dq_ref, dc_ref, dp_ref, c_ref, s_ref, o_ref):
        o_ref[:, :nq] = dq_ref[...]
        d = dc_ref[...] + jnp.where(pl.program_id(0) < nb - 1, dp_ref[...], 0.0)
        o_ref[:, nq:nq + nkv] = _rope_t(d[:, :nkv], c_ref[...], s_ref[...]).astype(BF16)
        o_ref[:, nq + nkv:] = d[:, nkv:].astype(BF16)

    tab = pl.BlockSpec((BLOCK, LANES), lambda m: (m, 0))
    blk = lambda w: pl.BlockSpec((BLOCK, w), lambda m: (m, 0))
    return pl.pallas_call(
        body, name=name, grid=(nb,),
        in_specs=[blk(nq), blk(2 * nkv), pl.BlockSpec((BLOCK, 2 * nkv), lambda m: (jnp.minimum(m + 1, nb - 1), 0)), tab, tab],
        out_specs=blk(nq + 2 * nkv), out_shape=_sds((S, nq + 2 * nkv), BF16), compiler_params=_params("parallel"),
    )(dq, dcur, dprev, cos, sin)


def _lru_gates(u, wri_ref, br, bi, sp):
    ub = u.astype(BF16)
    rs, igs = [], []
    for hb in range(LRU_BLOCKS):
        sl = slice(hb * LRU_BLOCK_W, (hb + 1) * LRU_BLOCK_W)
        ri = jnp.dot(ub[:, sl], wri_ref[hb], preferred_element_type=F32)
        rs.append(ri[:, :LRU_BLOCK_W])
        igs.append(ri[:, LRU_BLOCK_W:])
    r = _sigmoid(jnp.concatenate(rs, axis=1) + br)
    ig = _sigmoid(jnp.concatenate(igs, axis=1) + bi)
    la = -LRU_C * r * sp
    a = jnp.exp(la)
    sq = jnp.sqrt(_neg_expm1(2.0 * la))
    return r, ig, a, sq


def _lru_fwd(xw, cw, cb, wri, br, bi, lam, *, name, comm=None):
    S = xw.shape[0]
    W = D_MODEL
    tm = min(LRU_T, S)

    def body(gate_ref, up_ref, cw_ref, cb_ref, wri_ref, br_ref, bi_ref, lam_ref, y_ref, u_ref, h_ref, cu, ch, a_s, b_s):
        @pl.when(pl.program_id(0) == 0)
        def _():
            cu[...] = jnp.zeros_like(cu)
            ch[...] = jnp.zeros_like(ch)

        up = up_ref[...]
        prev8 = cu[...]
        u = up * cw_ref[3:4, :] + cb_ref[...]
        for d in range(1, LRU_CONV):
            u = u + _shift_down(up, prev8, d) * cw_ref[3 - d:4 - d, :]
        cu[...] = up[tm - 8:tm]
        u_ref[...] = u
        sp, _ = _softplus_neg(lam_ref[...])
        _, ig, a, sq = _lru_gates(u, wri_ref, br_ref[...], bi_ref[...], sp)
        a_s[...] = a
        b_s[...] = sq * (ig * u)
        rid = lax.broadcasted_iota(jnp.int32, (8, W), 0)

        def tile(t, h):
            r0 = pl.multiple_of(t * 8, 8)
            at, bt = a_s[pl.ds(r0, 8), :], b_s[pl.ds(r0, 8), :]
            out = jnp.zeros((8, W), F32)
            for j in range(8):
                h = at[j:j + 1, :] * h + bt[j:j + 1, :]
                out = jnp.where(rid == j, h, out)
            h_ref[pl.ds(r0, 8), :] = out
            return h

        ch[0:1, :] = lax.fori_loop(0, tm // 8, tile, ch[0:1, :])
        g, _ = _gelu_and_grad(gate_ref[...])
        y_ref[...] = (h_ref[...] * g).astype(BF16)

    row = pl.BlockSpec((tm, W), lambda i: (i, 0))
    full = lambda shape: pl.BlockSpec(shape, lambda i: (0,) * len(shape))
    return _call(
        body, name=name, grid=(S // tm,),
        in_specs=[row, pl.BlockSpec((tm, W), lambda i: (i, 1)), full((LRU_CONV, W)), full((1, W)),
                  full((LRU_BLOCKS, LRU_BLOCK_W, 2 * LRU_BLOCK_W)), full((1, W)), full((1, W)), full((1, W))],
        out_specs=[row, row, row], out_shape=[_sds((S, W), BF16), _sds((S, W), F32), _sds((S, W), F32)],
        scratch_shapes=[pltpu.VMEM((8, W), F32), pltpu.VMEM((8, W), F32), pltpu.VMEM((tm, W), F32), pltpu.VMEM((tm, W), F32)],
        params=_params("arbitrary"), comm=comm,
    )(xw, xw, cw, cb, wri, br, bi, lam)


def _lru_bwd(xw, u, h, dy, cw, wri, br, bi, lam, *, name):
    S = xw.shape[0]
    W = D_MODEL
    tm = min(LRU_T, S)
    nb = S // tm

    def body(gate_ref, up_ref, upp_ref, u_ref, h_ref, hp_ref, dy_ref, cw_ref, wri_ref, br_ref, bi_ref, lam_ref,
             dxw_ref, dcw_ref, dcb_ref, dwri_ref, dbr_ref, dbi_ref, dlam_ref, cg, cdu, a_s, d_s, g_s):
        i = pl.program_id(0)
        r_blk = nb - 1 - i

        @pl.when(i == 0)
        def _():
            cg[...] = jnp.zeros_like(cg)
            cdu[...] = jnp.zeros_like(cdu)
            for ref in (dcw_ref, dcb_ref, dwri_ref, dbr_ref, dbi_ref, dlam_ref):
                ref[...] = jnp.zeros_like(ref)

        u = u_ref[...]
        hv = h_ref[...]
        sp, dsp = _softplus_neg(lam_ref[...])
        r, ig, a, sq = _lru_gates(u, wri_ref, br_ref[...], bi_ref[...], sp)
        dy = dy_ref[...].astype(F32)
        g, dgelu = _gelu_and_grad(gate_ref[...])
        dxw_ref[:, :W] = (dy * hv * dgelu).astype(BF16)
        a_s[...] = a
        d_s[...] = dy * g
        rid = lax.broadcasted_iota(jnp.int32, (8, W), 0)

        def tile(t, c):
            r0 = pl.multiple_of((tm // 8 - 1 - t) * 8, 8)
            at, dt = a_s[pl.ds(r0, 8), :], d_s[pl.ds(r0, 8), :]
            out = jnp.zeros((8, W), F32)
            for j in range(7, -1, -1):
                gt = dt[j:j + 1, :] + c
                c = at[j:j + 1, :] * gt
                out = jnp.where(rid == j, gt, out)
            g_s[pl.ds(r0, 8), :] = out
            return c

        cg[0:1, :] = lax.fori_loop(0, tm // 8, tile, cg[0:1, :])
        gt = g_s[...]
        hprev8 = jnp.where(r_blk > 0, hp_ref[...], 0.0)
        da = gt * _shift_down(hv, hprev8, 1)
        iu = ig * u
        d_iu = gt * sq
        dla = da * a - (gt * iu) * (a * a) / sq
        dlam_ref[...] += jnp.sum(dla * r, axis=0, keepdims=True) * (-LRU_C) * dsp
        dr_pre = dla * (-LRU_C) * sp * r * (1.0 - r)
        di_pre = d_iu * u * ig * (1.0 - ig)
        dbr_ref[...] += jnp.sum(dr_pre, axis=0, keepdims=True)
        dbi_ref[...] += jnp.sum(di_pre, axis=0, keepdims=True)
        ub = u.astype(BF16)
        dus = []
        for hb in range(LRU_BLOCKS):
            sl = slice(hb * LRU_BLOCK_W, (hb + 1) * LRU_BLOCK_W)
            dri = jnp.concatenate([dr_pre[:, sl], di_pre[:, sl]], axis=1).astype(BF16)
            dus.append(lax.dot_general(dri, wri_ref[hb], C11, preferred_element_type=F32))
            dwri_ref[hb] += lax.dot_general(ub[:, sl], dri, C00, preferred_element_type=F32)
        du = d_iu * ig + jnp.concatenate(dus, axis=1)
        dcb_ref[...] += jnp.sum(du, axis=0, keepdims=True)
        up = up_ref[...]
        upprev8 = jnp.where(r_blk > 0, upp_ref[...], 0.0)
        dcw_ref[3:4, :] += jnp.sum(du * up, axis=0, keepdims=True)
        for d in range(1, LRU_CONV):
            dcw_ref[3 - d:4 - d, :] += jnp.sum(du * _shift_down(up, upprev8, d), axis=0, keepdims=True)
        next8 = cdu[...]
        dup = du * cw_ref[3:4, :]
        for d in range(1, LRU_CONV):
            dup = dup + _shift_up(du, next8, d) * cw_ref[3 - d:4 - d, :]
        cdu[...] = du[0:8]
        dxw_ref[:, W:] = dup.astype(BF16)

    rev = lambda c: (lambda i: (nb - 1 - i, c))
    halo = lambda c: (lambda i: (jnp.maximum((nb - 1 - i) * (tm // 8) - 1, 0), c))
    full = lambda shape: pl.BlockSpec(shape, lambda i: (0,) * len(shape))
    vec = full((1, W))
    return pl.pallas_call(
        body, name=name, grid=(nb,),
        in_specs=[pl.BlockSpec((tm, W), rev(0)), pl.BlockSpec((tm, W), rev(1)), pl.BlockSpec((8, W), halo(1)),
                  pl.BlockSpec((tm, W), rev(0)), pl.BlockSpec((tm, W), rev(0)), pl.BlockSpec((8, W), halo(0)),
                  pl.BlockSpec((tm, W), rev(0)), full((LRU_CONV, W)), full((LRU_BLOCKS, LRU_BLOCK_W, 2 * LRU_BLOCK_W)),
                  vec, vec, vec],
        out_specs=[pl.BlockSpec((tm, 2 * W), rev(0)), full((LRU_CONV, W)), vec,
                   full((LRU_BLOCKS, LRU_BLOCK_W, 2 * LRU_BLOCK_W)), vec, vec, vec],
        out_shape=[_sds((S, 2 * W), BF16), _sds((LRU_CONV, W), F32), _sds((1, W), F32),
                   _sds((LRU_BLOCKS, LRU_BLOCK_W, 2 * LRU_BLOCK_W), F32), _sds((1, W), F32), _sds((1, W), F32),
                   _sds((1, W), F32)],
        scratch_shapes=[pltpu.VMEM((8, W), F32), pltpu.VMEM((8, W), F32), pltpu.VMEM((tm, W), F32),
                        pltpu.VMEM((tm, W), F32), pltpu.VMEM((tm, W), F32)],
        compiler_params=_params("arbitrary"),
    )(xw, xw, xw, u, h, h, dy, cw, wri, br, bi, lam)


def _rms(x, g):
    r = lax.rsqrt(jnp.mean(x * x, axis=-1, keepdims=True) + RMS_EPS)
    return x * r * g, r


def _mla_pre(c, qg, kvg, cos, sin, *, name):
    S = c.shape[0]
    tm = min(ROW_T, S)
    q0, k0 = C_Q_RANK, C_Q_RANK + C_KV_RANK

    def body(c_ref, qg_ref, kvg_ref, cs_ref, sn_ref, cq_ref, ckv_ref, kr_ref):
        cq_ref[...] = _rms(c_ref[:, :q0], qg_ref[...])[0].astype(BF16)
        ckv_ref[...] = _rms(c_ref[:, q0:k0], kvg_ref[...])[0].astype(BF16)
        kr_ref[...] = _rope(c_ref[:, k0:], cs_ref[...], sn_ref[...]).astype(BF16)

    blk = lambda w: pl.BlockSpec((tm, w), lambda i: (i, 0))
    vec = lambda w: pl.BlockSpec((1, w), lambda i: (0, 0))
    return pl.pallas_call(
        body, name=name, grid=(S // tm,),
        in_specs=[blk(c.shape[1]), vec(C_Q_RANK), vec(C_KV_RANK), blk(LANES), blk(LANES)],
        out_specs=[blk(C_Q_RANK), blk(C_KV_RANK), blk(C_ROPE)],
        out_shape=[_sds((S, C_Q_RANK), BF16), _sds((S, C_KV_RANK), BF16), _sds((S, C_ROPE), BF16)],
        compiler_params=_params("parallel"),
    )(c, qg, kvg, cos, sin)


def _mla_post_bwd(c, dcq_a, dcq_b, dckv, dkr_h, qg, kvg, cos, sin, *, name):
    S = c.shape[0]
    tm = min(ROW_T, S)
    q0, k0 = C_Q_RANK, C_Q_RANK + C_KV_RANK

    def rms_bwd(x, g, dy):
        r = lax.rsqrt(jnp.mean(x * x, axis=-1, keepdims=True) + RMS_EPS)
        uu = dy * g
        dx = r * uu - x * (r * r * r) * jnp.mean(uu * x, axis=-1, keepdims=True)
        return dx, jnp.sum(dy * x * r, axis=0, keepdims=True)

    def body(c_ref, da_ref, db_ref, dkv_ref, dkr_ref, qg_ref, kvg_ref, cs_ref, sn_ref, dc_ref, dqg_ref, dkvg_ref):
        @pl.when(pl.program_id(0) == 0)
        def _():
            dqg_ref[...] = jnp.zeros_like(dqg_ref)
            dkvg_ref[...] = jnp.zeros_like(dkvg_ref)

        dx, dg = rms_bwd(c_ref[:, :q0], qg_ref[...], da_ref[...] + db_ref[...])
        dc_ref[:, :q0] = dx.astype(BF16)
        dqg_ref[...] += dg
        dx, dg = rms_bwd(c_ref[:, q0:k0], kvg_ref[...], dkv_ref[...])
        dc_ref[:, q0:k0] = dx.astype(BF16)
        dkvg_ref[...] += dg
        dkr = dkr_ref[0]
        for hh in range(1, C_HEADS):
            dkr = dkr + dkr_ref[hh]
        dc_ref[:, k0:] = _rope_t(dkr, cs_ref[...], sn_ref[...]).astype(BF16)

    blk = lambda w: pl.BlockSpec((tm, w), lambda i: (i, 0))
    vec = lambda w: pl.BlockSpec((1, w), lambda i: (0, 0))
    return pl.pallas_call(
        body, name=name, grid=(S // tm,),
        in_specs=[blk(c.shape[1]), blk(C_Q_RANK), blk(C_Q_RANK), blk(C_KV_RANK),
                  pl.BlockSpec((C_HEADS, tm, C_ROPE), lambda i: (0, i, 0)), vec(C_Q_RANK), vec(C_KV_RANK), blk(LANES), blk(LANES)],
        out_specs=[blk(c.shape[1]), vec(C_Q_RANK), vec(C_KV_RANK)],
        out_shape=[_sds(c.shape, BF16), _sds((1, C_Q_RANK), F32), _sds((1, C_KV_RANK), F32)],
        compiler_params=_params("arbitrary"),
    )(c, dcq_a, dcq_b, dckv, dkr_h, qg, kvg, cos, sin)


def _rope_heads(x, cos, sin, *, transpose, name):
    S, W = x.shape
    tm = min(ROW_T, S)
    fn = _rope_t if transpose else _rope

    def body(x_ref, c_ref, s_ref, o_ref):
        o_ref[...] = fn(x_ref[...].astype(F32), c_ref[...], s_ref[...]).astype(BF16)

    row = pl.BlockSpec((tm, W), lambda i: (i, 0))
    tab = pl.BlockSpec((tm, LANES), lambda i: (i, 0))
    return pl.pallas_call(body, name=name, grid=(S // tm,), in_specs=[row, tab, tab], out_specs=row,
                          out_shape=_sds((S, W), BF16), compiler_params=_params("parallel"))(x, cos, sin)


MLA_GROUP = 4
MLA_SCALE = (C_NOPE + C_ROPE) ** -0.5
LOG2E = 1.4426950408889634


def _mla_cat(nope, rope):
    return jnp.concatenate([nope, rope], axis=1)


def _mla_scores2(qn, qr, kn, kr, diagonal):
    s = lax.dot_general(_mla_cat(qn, qr), _mla_cat(kn, kr), C11, preferred_element_type=F32)
    s = s * (MLA_SCALE * LOG2E)
    if diagonal:
        row = lax.broadcasted_iota(jnp.int32, s.shape, 0)
        col = lax.broadcasted_iota(jnp.int32, s.shape, 1)
        s = jnp.where(col <= row, s, NEG)
    return s


def _causal_pairs(n, query_major):
    if query_major:
        pairs = [(i, j) for i in range(n) for j in range(i + 1)]
    else:
        pairs = [(i, j) for j in range(n) for i in range(j, n)]
    return jnp.asarray([p[0] for p in pairs], jnp.int32), jnp.asarray([p[1] for p in pairs], jnp.int32)


def _mla_flash_fwd(qn, qr, kv, kr, *, name):
    S = qn.shape[0]
    H, G, t = C_HEADS, MLA_GROUP, min(FLASH_T, S)
    qi, kj = _causal_pairs(S // t, True)

    def body(qi_ref, kj_ref, qn_ref, qr_ref, kn_ref, v_ref, kr_ref, o_ref, lse_ref, *scr):
        m_s, l_s, acc = scr[:G], scr[G:2 * G], scr[2 * G:]
        p_id = pl.program_id(1)
        i, j = qi_ref[p_id], kj_ref[p_id]
        sls = [slice(hh * LANES, (hh + 1) * LANES) for hh in range(G)]

        @pl.when(j == 0)
        def _():
            for hh in range(G):
                m_s[hh][...] = jnp.full_like(m_s[hh], NEG)
                l_s[hh][...] = jnp.zeros_like(l_s[hh])
                acc[hh][...] = jnp.zeros_like(acc[hh])

        def step(diagonal):
            scores = lambda hh: _mla_scores2(qn_ref[:, sls[hh]], qr_ref[hh], kn_ref[:, sls[hh]], kr_ref[...], diagonal)
            s_next = scores(0)
            for hh in range(G):
                s = s_next
                if hh + 1 < G:
                    s_next = scores(hh + 1)
                m_prev = m_s[hh][...]
                m_new = jnp.maximum(m_prev, jnp.max(s, axis=-1, keepdims=True))
                corr = jnp.exp2(m_prev - m_new)
                pb = jnp.exp2(s - m_new[:, 0:1]).astype(BF16)
                l_s[hh][...] = corr * l_s[hh][...] + jnp.dot(pb, jnp.ones((pb.shape[1], LANES), BF16), preferred_element_type=F32)
                acc[hh][...] = corr * acc[hh][...] + jnp.dot(pb, v_ref[:, sls[hh]], preferred_element_type=F32)
                m_s[hh][...] = m_new

        @pl.when(j < i)
        def _():
            step(False)

        @pl.when(j == i)
        def _():
            step(True)
            for hh in range(G):
                o_ref[:, sls[hh]] = (acc[hh][...] / l_s[hh][...]).astype(BF16)
                lse_ref[:, sls[hh]] = m_s[hh][...] + jnp.log2(l_s[hh][...])

    wide = lambda which, off: pl.BlockSpec((t, G * LANES), lambda h, p, qi, kj: ((qi if which == "q" else kj)[p], off + h))
    return pl.pallas_call(
        body, name=name,
        grid_spec=pltpu.PrefetchScalarGridSpec(
            num_scalar_prefetch=2, grid=(H // G, qi.shape[0]),
            in_specs=[wide("q", 0), pl.BlockSpec((G, t, C_ROPE), lambda h, p, qi, kj: (h, qi[p], 0)),
                      wide("k", 0), wide("k", H // G), pl.BlockSpec((t, C_ROPE), lambda h, p, qi, kj: (kj[p], 0))],
            out_specs=[wide("q", 0), wide("q", 0)],
            scratch_shapes=[pltpu.VMEM((t, LANES), F32)] * (3 * G)),
        out_shape=[_sds((S, H * C_V), BF16), _sds((S, H * LANES), F32)],
        compiler_params=_params("parallel", "arbitrary"),
    )(qi, kj, qn, qr, kv, kv, kr)


def _mla_delta(do, o, *, name):
    S, W = do.shape
    tm = min(ROW_T, S)

    def body(do_ref, o_ref, d_ref):
        for h in range(C_HEADS):
            sl = slice(h * C_V, (h + 1) * C_V)
            d = jnp.sum(do_ref[:, sl].astype(F32) * o_ref[:, sl].astype(F32), axis=-1, keepdims=True)
            d_ref[:, sl] = jnp.broadcast_to(d, (tm, C_V))

    row = pl.BlockSpec((tm, W), lambda i: (i, 0))
    return pl.pallas_call(body, name=name, grid=(S // tm,), in_specs=[row, row], out_specs=row,
                          out_shape=_sds((S, W), F32), compiler_params=_params("parallel"))(do, o)


def _mla_flash_dq(qn, qr, kv, kr, do, lse, delta, *, name):
    S = qn.shape[0]
    H, G, t = C_HEADS, MLA_GROUP, min(FLASH_T, S)
    qi, kj = _causal_pairs(S // t, True)

    def body(qi_ref, kj_ref, qn_ref, qr_ref, kn_ref, v_ref, kr_ref, do_ref, lse_ref, dl_ref, dqn_ref, dqr_ref, acc):
        p_id = pl.program_id(1)
        i, j = qi_ref[p_id], kj_ref[p_id]
        sls = [slice(hh * LANES, (hh + 1) * LANES) for hh in range(G)]

        @pl.when(j == 0)
        def _():
            acc[...] = jnp.zeros_like(acc)

        def step(diagonal):
            def products(hh):
                s = _mla_scores2(qn_ref[:, sls[hh]], qr_ref[hh], kn_ref[:, sls[hh]], kr_ref[...], diagonal)
                return s, lax.dot_general(do_ref[:, sls[hh]], v_ref[:, sls[hh]], C11, preferred_element_type=F32)

            nxt = products(0)
            for hh in range(G):
                s, dp = nxt
                if hh + 1 < G:
                    nxt = products(hh + 1)
                p = jnp.exp2(s - lse_ref[:, hh * LANES:hh * LANES + 1])
                ds = (p * (dp - dl_ref[:, hh * LANES:hh * LANES + 1])).astype(BF16)
                acc[hh] += jnp.dot(ds, _mla_cat(kn_ref[:, sls[hh]], kr_ref[...]), preferred_element_type=F32)

        @pl.when(j < i)
        def _():
            step(False)

        @pl.when(j == i)
        def _():
            step(True)
            for hh in range(G):
                dqn_ref[:, sls[hh]] = (acc[hh, :, :C_NOPE] * MLA_SCALE).astype(BF16)
                dqr_ref[hh] = acc[hh, :, C_NOPE:] * MLA_SCALE

    wide = lambda which, off: pl.BlockSpec((t, G * LANES), lambda h, p, qi, kj: ((qi if which == "q" else kj)[p], off + h))
    qrb = pl.BlockSpec((G, t, C_ROPE), lambda h, p, qi, kj: (h, qi[p], 0))
    return pl.pallas_call(
        body, name=name,
        grid_spec=pltpu.PrefetchScalarGridSpec(
            num_scalar_prefetch=2, grid=(H // G, qi.shape[0]),
            in_specs=[wide("q", 0), qrb, wide("k", 0), wide("k", H // G),
                      pl.BlockSpec((t, C_ROPE), lambda h, p, qi, kj: (kj[p], 0)), wide("q", 0), wide("q", 0), wide("q", 0)],
            out_specs=[wide("q", 0), qrb],
            scratch_shapes=[pltpu.VMEM((G, t, C_NOPE + C_ROPE), F32)]),
        out_shape=[_sds((S, H * C_NOPE), BF16), _sds((H, S, C_ROPE), F32)],
        compiler_params=_params("parallel", "arbitrary"),
    )(qi, kj, qn, qr, kv, kv, kr, do, lse, delta)


def _mla_flash_dkv(qn, qr, kv, kr, do, lse, delta, *, name):
    S = qn.shape[0]
    H, G, t = C_HEADS, MLA_GROUP, min(FLASH_T, S)
    n = S // t
    qi, kj = _causal_pairs(n, False)

    def body(qi_ref, kj_ref, qn_ref, qr_ref, kn_ref, v_ref, kr_ref, do_ref, lse_ref, dl_ref, dkn_ref, dv_ref, dkr_ref, ak, av):
        p_id = pl.program_id(1)
        i, j = qi_ref[p_id], kj_ref[p_id]
        sls = [slice(hh * LANES, (hh + 1) * LANES) for hh in range(G)]

        def step(diagonal):
            def products(hh):
                s = _mla_scores2(qn_ref[:, sls[hh]], qr_ref[hh], kn_ref[:, sls[hh]], kr_ref[...], diagonal)
                return s, lax.dot_general(do_ref[:, sls[hh]], v_ref[:, sls[hh]], C11, preferred_element_type=F32)

            nxt = products(0)
            for hh in range(G):
                s, dp = nxt
                if hh + 1 < G:
                    nxt = products(hh + 1)
                sl = sls[hh]
                p = jnp.exp2(s - lse_ref[:, hh * LANES:hh * LANES + 1])
                ds = (p * (dp - dl_ref[:, hh * LANES:hh * LANES + 1])).astype(BF16)
                av[:, sl] += lax.dot_general(p.astype(BF16), do_ref[:, sl], C00, preferred_element_type=F32)
                ak[hh] += lax.dot_general(ds, _mla_cat(qn_ref[:, sl], qr_ref[hh]), C00, preferred_element_type=F32)

        @pl.when(i == j)
        def _():
            ak[...] = jnp.zeros_like(ak)
            av[...] = jnp.zeros_like(av)
            step(True)

        @pl.when(i > j)
        def _():
            step(False)

        @pl.when(i == n - 1)
        def _():
            dv_ref[...] = av[...].astype(BF16)
            for hh in range(G):
                dkn_ref[:, sls[hh]] = (ak[hh, :, :C_NOPE] * MLA_SCALE).astype(BF16)
                dkr_ref[hh] = ak[hh, :, C_NOPE:] * MLA_SCALE

    wide = lambda which, off: pl.BlockSpec((t, G * LANES), lambda h, p, qi, kj: ((qi if which == "q" else kj)[p], off + h))
    krb = pl.BlockSpec((G, t, C_ROPE), lambda h, p, qi, kj: (h, kj[p], 0))
    return pl.pallas_call(
        body, name=name,
        grid_spec=pltpu.PrefetchScalarGridSpec(
            num_scalar_prefetch=2, grid=(H // G, qi.shape[0]),
            in_specs=[wide("q", 0), pl.BlockSpec((G, t, C_ROPE), lambda h, p, qi, kj: (h, qi[p], 0)), wide("k", 0),
                      wide("k", H // G), pl.BlockSpec((t, C_ROPE), lambda h, p, qi, kj: (kj[p], 0)),
                      wide("q", 0), wide("q", 0), wide("q", 0)],
            out_specs=[wide("k", 0), wide("k", 0), krb],
            scratch_shapes=[pltpu.VMEM((G, t, C_NOPE + C_ROPE), F32), pltpu.VMEM((t, G * LANES), F32)]),
        out_shape=[_sds((S, H * C_NOPE), BF16), _sds((S, H * C_V), BF16), _sds((H, S, C_ROPE), F32)],
        compiler_params=_params("parallel", "arbitrary"),
    )(qi, kj, qn, qr, kv, kv, kr, do, lse, delta)


def _place():
    return lax.axis_index("x"), lax.axis_index("y"), lax.axis_index("c")


def _other_chips(x, y):
    return [(1 - x, y), (x, 1 - y), (1 - x, 1 - y)]


def _all_gather_chips(p, *, name):
    R, C = p.shape

    def body(p_ref, o_ref, send_sems, recv_sems, local_sem):
        x, y, c = _place()
        me = 2 * x + y
        local = pltpu.make_async_copy(p_ref, o_ref.at[me], local_sem)
        local.start()
        copies = [pltpu.make_async_remote_copy(src_ref=p_ref, dst_ref=o_ref.at[me], send_sem=send_sems.at[k],
                                               recv_sem=recv_sems.at[k], device_id=(px, py, c), device_id_type=MESH)
                  for k, (px, py) in enumerate(_other_chips(x, y))]
        for cp in copies:
            cp.start()
        for cp in copies:
            cp.wait()
        local.wait()

    any_spec = pl.BlockSpec(memory_space=pl.ANY)
    return pl.pallas_call(
        body, name=name, in_specs=[any_spec], out_specs=any_spec, out_shape=_sds((4, R, C), p.dtype),
        scratch_shapes=[pltpu.SemaphoreType.DMA((3,)), pltpu.SemaphoreType.DMA((3,)), pltpu.SemaphoreType.DMA(())],
    )(p)


def _shard_of(ref, axis, pos, size):
    idx = [slice(None)] * len(ref.shape)
    idx[axis] = pl.ds(pos * size, size)
    return ref.at[tuple(idx)]


def _shard_pos(chip, swapped):
    return (chip % 2) * 2 + chip // 2 if swapped else chip


class _Comm:
    def __init__(self, inputs, out_shapes, sems, start, finish, deliver):
        self.inputs, self.out_shapes, self.sems = list(inputs), list(out_shapes), list(sems)
        self.start, self.finish, self.deliver = start, finish, deliver


def _call(body, *, name, grid, in_specs, out_specs, out_shape, scratch_shapes=(), params, comm=None):
    in_specs, out_specs, out_shape, scratch_shapes = list(in_specs), list(out_specs), list(out_shape), list(scratch_shapes)
    if comm is None:
        return pl.pallas_call(body, name=name, grid=grid, in_specs=in_specs, out_specs=out_specs, out_shape=out_shape,
                              scratch_shapes=scratch_shapes, compiler_params=params)
    n_in, n_out, n_scr = len(in_specs), len(out_specs), len(scratch_shapes)
    c_in, c_out = len(comm.inputs), len(comm.out_shapes)

    def hosted(*refs):
        a, rest = refs[:n_in], refs[n_in:]
        cin, rest = rest[:c_in], rest[c_in:]
        o, rest = rest[:n_out], rest[n_out:]
        cout, rest = rest[:c_out], rest[c_out:]
        scr, sems = rest[:n_scr], rest[n_scr:]
        first = functools.reduce(jnp.logical_and, [pl.program_id(d) == 0 for d in range(len(grid))])
        last = functools.reduce(jnp.logical_and, [pl.program_id(d) == grid[d] - 1 for d in range(len(grid))])

        @pl.when(first)
        def _():
            comm.start(cin, cout, sems)

        body(*a, *o, *scr)

        @pl.when(last)
        def _():
            comm.finish(cin, cout, sems)

    any_spec = pl.BlockSpec(memory_space=pl.ANY)
    call = pl.pallas_call(
        hosted, name=name, grid=grid, in_specs=in_specs + [any_spec] * c_in, out_specs=out_specs + [any_spec] * c_out,
        out_shape=out_shape + comm.out_shapes, scratch_shapes=scratch_shapes + comm.sems, compiler_params=params)

    def run(*args):
        outs = call(*args, *comm.inputs)
        comm.deliver(outs[n_out:])
        return outs[:n_out]

    return run


def _run_comm(comm, *, name):
    c_in, c_out = len(comm.inputs), len(comm.out_shapes)

    def body(*refs):
        cin, cout, sems = refs[:c_in], refs[c_in:c_in + c_out], refs[c_in + c_out:]
        comm.start(cin, cout, sems)
        comm.finish(cin, cout, sems)

    any_spec = pl.BlockSpec(memory_space=pl.ANY)
    outs = pl.pallas_call(body, name=name, in_specs=[any_spec] * c_in, out_specs=[any_spec] * c_out,
                          out_shape=comm.out_shapes, scratch_shapes=comm.sems)(*comm.inputs)
    comm.deliver(outs)


def _gather_comm(items, deliver):
    n = len(items)
    shard_shapes = [a.shape if j is None else a.shape[1:] for a, j, _, _ in items]
    axes = [ax for _, _, ax, _ in items]
    swapped = [sw for _, _, _, sw in items]
    sizes = [s[a] for s, a in zip(shard_shapes, axes)]
    halves = [s[-2] // 2 for s in shard_shapes]
    full = [tuple(4 * d if i == a else d for i, d in enumerate(s)) for s, a in zip(shard_shapes, axes)]

    def mine(ins, k):
        j = items[k][1]
        return ins[k] if j is None else ins[k].at[j]

    def half_of(ref, k, half, chip=None):
        nd = len(ref.shape)
        split = nd - 2
        idx = [slice(None)] * nd
        start = half * halves[k]
        if chip is not None:
            pos = _shard_pos(chip, swapped[k]) * sizes[k]
            if axes[k] == split:
                start = start + pos
            else:
                idx[axes[k]] = pl.ds(pos, sizes[k])
        idx[split] = pl.ds(start, halves[k])
        return ref.at[tuple(idx)]

    def local_copy(ins, outs, sems, k, me):
        return pltpu.make_async_copy(mine(ins, k), _shard_of(outs[k], axes[k], _shard_pos(me, swapped[k]), sizes[k]), sems[4].at[k])

    def ici_copy(ins, outs, sems, k, j, peer, c, landing_chip):
        return pltpu.make_async_remote_copy(
            src_ref=half_of(mine(ins, k), k, c), dst_ref=half_of(outs[k], k, c, chip=landing_chip), send_sem=sems[0].at[3 * k + j],
            recv_sem=sems[1].at[3 * k + j], device_id=(peer[0], peer[1], c), device_id_type=MESH)

    def pass_copy(outs, sems, k, j, half, chip, sibling):
        region = half_of(outs[k], k, half, chip=chip)
        return pltpu.make_async_remote_copy(src_ref=region, dst_ref=region, send_sem=sems[2].at[3 * k + j],
                                            recv_sem=sems[3].at[3 * k + j], device_id=sibling, device_id_type=MESH)

    def start(ins, outs, sems):
        x, y, c = _place()
        me = 2 * x + y
        for k in range(n):
            local_copy(ins, outs, sems, k, me).start()
            for j, peer in enumerate(_other_chips(x, y)):
                ici_copy(ins, outs, sems, k, j, peer, c, me).start()

    def finish(ins, outs, sems):
        x, y, c = _place()
        me = 2 * x + y
        chips = _other_chips(x, y)
        sibling = (x, y, 1 - c)
        for k in range(n):
            for j, peer in enumerate(chips):
                ici_copy(ins, outs, sems, k, j, peer, c, 2 * peer[0] + peer[1]).wait_recv()
                pass_copy(outs, sems, k, j, c, 2 * peer[0] + peer[1], sibling).start()
        for k in range(n):
            for j, peer in enumerate(chips):
                pass_copy(outs, sems, k, j, 1 - c, 2 * peer[0] + peer[1], sibling).wait_recv()
        for k in range(n):
            local_copy(ins, outs, sems, k, me).wait()
            for j, peer in enumerate(chips):
                ici_copy(ins, outs, sems, k, j, peer, c, me).wait_send()
                pass_copy(outs, sems, k, j, c, 2 * peer[0] + peer[1], sibling).wait_send()

    return _Comm([a for a, _, _, _ in items], [_sds(f, a.dtype) for f, (a, _, _, _) in zip(full, items)],
                 [pltpu.SemaphoreType.DMA((3 * n,))] * 4 + [pltpu.SemaphoreType.DMA((n,))], start, finish, deliver)


def _scatter_comm(items, deliver):
    n = len(items)
    axes = [ax for _, ax, _ in items]
    swapped = [sw for _, _, sw in items]
    sizes = [g.shape[a] // 4 for g, a, _ in items]
    shard = [tuple(d // 4 if i == a else d for i, d in enumerate(g.shape)) for g, a, _ in items]

    def copies(ins, outs, sems):
        x, y, c = _place()
        me = 2 * x + y
        out = []
        for k in range(n):
            own = _shard_of(ins[k], axes[k], _shard_pos(me, swapped[k]), sizes[k])
            out.append(pltpu.make_async_copy(own, outs[k].at[3], sems[2].at[k]))
            for j, (px, py) in enumerate(_other_chips(x, y)):
                src = _shard_of(ins[k], axes[k], _shard_pos(2 * px + py, swapped[k]), sizes[k])
                out.append(pltpu.make_async_remote_copy(src_ref=src, dst_ref=outs[k].at[j], send_sem=sems[0].at[3 * k + j],
                                                        recv_sem=sems[1].at[3 * k + j], device_id=(px, py, c), device_id_type=MESH))
        return out

    def start(ins, outs, sems):
        for cp in copies(ins, outs, sems):
            cp.start()

    def finish(ins, outs, sems):
        for cp in copies(ins, outs, sems):
            cp.wait()

    return _Comm([g for g, _, _ in items], [_sds((4,) + s, g.dtype) for s, (g, _, _) in zip(shard, items)],
                 [pltpu.SemaphoreType.DMA((3 * n,)), pltpu.SemaphoreType.DMA((3 * n,)), pltpu.SemaphoreType.DMA((n,))],
                 start, finish, deliver)


def _row_tile(rows, cols, budget=2 ** 20):
    best = None
    for t in range(8, rows + 1, 8):
        if rows % t == 0 and t * cols * 4 <= budget:
            best = t
    return best or rows


def _sum_partials(recv, into, layer, layers, *, name):
    _, R, C = recv.shape
    tr = _row_tile(R, C)
    nt = R // tr

    def body(own_ref, r0_ref, r1_ref, r2_ref, *rest):
        f = lambda ref: ref[...].astype(F32)
        rest[-1][...] = ((f(own_ref) + f(r0_ref)) + f(r1_ref)) + f(r2_ref)

    rspec = lambda k: pl.BlockSpec((None, tr, C), lambda i: (k, i, 0))
    extra = [] if into is None else [pl.BlockSpec(memory_space=pl.ANY)]
    return pl.pallas_call(
        body, name=name, grid=(nt,), in_specs=[rspec(3), rspec(0), rspec(1), rspec(2)] + extra,
        out_specs=pl.BlockSpec((tr, C), lambda i: (layer * nt + i, 0)), out_shape=_sds((layers * R, C), F32),
        input_output_aliases={} if into is None else {4: 0}, compiler_params=_params("parallel"),
    )(recv, recv, recv, recv, *([] if into is None else [into]))


def _swap_cores(parts, *, name):
    n = len(parts)

    def body(*refs):
        ins, outs = refs[:n], refs[n:2 * n]
        send_sems, recv_sems = refs[2 * n:]
        x, y, c = _place()
        copies = [pltpu.make_async_remote_copy(src_ref=ins[k], dst_ref=outs[k], send_sem=send_sems.at[k], recv_sem=recv_sems.at[k],
                                               device_id=(x, y, 1 - c), device_id_type=MESH) for k in range(n)]
        for cp in copies:
            cp.start()
        for cp in copies:
            cp.wait()

    any_spec = pl.BlockSpec(memory_space=pl.ANY)
    return pl.pallas_call(
        body, name=name, in_specs=[any_spec] * n, out_specs=[any_spec] * n, out_shape=[_sds(p.shape, p.dtype) for p in parts],
        scratch_shapes=[pltpu.SemaphoreType.DMA((n,)), pltpu.SemaphoreType.DMA((n,))],
    )(*parts)


def _all_reduce_small(v, *, name):
    r, C = v.shape

    def body(v_ref, o_ref, buf, send_sems, recv_sems):
        x, y, c = _place()
        me = 4 * x + 2 * y + c
        buf[me] = v_ref[...]
        peers = []
        for k in range(1, 8):
            kx, ky, kc = (k >> 2) & 1, (k >> 1) & 1, k & 1
            px = 1 - x if kx else x
            py = 1 - y if ky else y
            pc = 1 - c if kc else c
            peers.append((px, py, pc))
        copies = []
        for k, peer in enumerate(peers):
            cp = pltpu.make_async_remote_copy(src_ref=v_ref, dst_ref=buf.at[me], send_sem=send_sems.at[k],
                                              recv_sem=recv_sems.at[me], device_id=peer, device_id_type=MESH)
            cp.start()
            copies.append(cp)
        for k, (px, py, pc) in enumerate(peers):
            src = 4 * px + 2 * py + pc
            pltpu.make_async_remote_copy(src_ref=v_ref, dst_ref=buf.at[src], send_sem=send_sems.at[k],
                                         recv_sem=recv_sems.at[src], device_id=peers[k], device_id_type=MESH).wait_recv()
        for cp in copies:
            cp.wait_send()
        acc = buf[0]
        for d in range(1, 8):
            acc = acc + buf[d]
        o_ref[...] = acc

    vm = pl.BlockSpec(memory_space=pltpu.VMEM)
    return pl.pallas_call(
        body, name=name, in_specs=[vm], out_specs=vm, out_shape=_sds((r, C), F32),
        scratch_shapes=[pltpu.VMEM((8, r, C), F32), pltpu.SemaphoreType.DMA((7,)), pltpu.SemaphoreType.DMA((8,))],
    )(v)


def _adamw(w, m, v, ga, gb, *, name):
    R, C = w.shape
    tr = _row_tile(R, C)
    has_b = gb is not None
    c1 = 1.0 / (1.0 - ADAM_B1 ** ADAM_STEP)
    c2 = 1.0 / (1.0 - ADAM_B2 ** ADAM_STEP)

    def body(*refs):
        if has_b:
            w_ref, m_ref, v_ref, ga_ref, gb_ref, g_ref, d_ref, nm_ref, nv_ref = refs
            g = ga_ref[...] + gb_ref[...]
        else:
            w_ref, m_ref, v_ref, ga_ref, g_ref, d_ref, nm_ref, nv_ref = refs
            g = ga_ref[...]
        nm = ADAM_B1 * m_ref[...] + (1.0 - ADAM_B1) * g
        nv = ADAM_B2 * v_ref[...] + (1.0 - ADAM_B2) * (g * g)
        g_ref[...] = g
        nm_ref[...] = nm
        nv_ref[...] = nv
        d_ref[...] = -ADAM_LR * ((nm * c1) / (jnp.sqrt(nv * c2) + ADAM_EPS) + ADAM_WD * w_ref[...])

    blk = pl.BlockSpec((tr, C), lambda i: (i, 0))
    n_in = 5 if has_b else 4
    args = (w, m, v, ga) + ((gb,) if has_b else ())
    return pl.pallas_call(body, name=name, grid=(R // tr,), in_specs=[blk] * n_in, out_specs=[blk] * 4,
                          out_shape=[_sds((R, C), F32)] * 4, compiler_params=_params("parallel"))(*args)


def _seg_rows(n, cols):
    return -(-n // (16 * cols)) * 16


def _pack(arrays, dtype, cols=PACK_COLS, row_mult=512):
    parts, rows = [], 0
    for a in arrays:
        n = int(np.prod(a.shape))
        r = _seg_rows(n, cols)
        flat = a.reshape(-1).astype(dtype)
        if r * cols != n:
            flat = jnp.pad(flat, (0, r * cols - n))
        parts.append(flat.reshape(r, cols))
        rows += r
    pad = -rows % row_mult
    if pad:
        parts.append(jnp.zeros((pad, cols), dtype))
    return jnp.concatenate(parts, axis=0)


def _unpack(packed, shapes, cols=PACK_COLS):
    out, r0 = [], 0
    for shp in shapes:
        n = int(np.prod(shp))
        used = -(-n // cols)
        out.append(packed[r0:r0 + used].reshape(-1)[:n].reshape(shp))
        r0 += _seg_rows(n, cols)
    return out


def _rope_tables(seq):
    inv = 1.0 / (ROPE_THETA ** (jnp.arange(0, 64, 2, dtype=F32) / 64))
    ang = jnp.arange(seq, dtype=F32)[:, None] * inv[None, :]
    cos, sin = jnp.cos(ang), jnp.sin(ang)
    cos128 = jnp.concatenate([cos, cos, cos, cos], axis=1)
    sin128 = jnp.concatenate([-sin, sin, -sin, sin], axis=1)
    return cos128, sin128


def _ffn_perm(a):
    lead = a.shape[:-1]
    nj = D_FF // FFN_TC
    return jnp.swapaxes(a.reshape(lead + (2, nj, FFN_TC)), -3, -2).reshape(lead + (2 * D_FF,))


def _ffn_unperm(a):
    lead = a.shape[:-1]
    nj = D_FF // FFN_TC
    return jnp.swapaxes(a.reshape(lead + (nj, 2, FFN_TC)), -3, -2).reshape(lead + (2 * D_FF,))


def _mixer_a_fwd(xb, w, j, cos, sin, tag, gathers):
    qkv = _mm(xb, w["a_w_qkv"][j], name=f"mm_qkv_{tag}", comm=gathers.get(f"mm_qkv_{tag}"))
    qkv_r = _rope_cols(qkv, cos, sin, (A_HEADS + A_KV_HEADS) * A_HEAD_DIM, name=f"rope_qkv_{tag}",
                       comm=gathers.get(f"rope_qkv_{tag}"))
    o = _swa_fwd(qkv_r, w["a_sinks"][j], name=f"swa_fwd_{tag}", comm=gathers.get(f"swa_fwd_{tag}"))
    return (o, w["a_w_o"][j]), (xb, qkv_r, o)


def _mixer_a_bwd(dzb, res, w, j, cos, sin, tag, grads, make_comm):
    xb, qkv_r, o = res
    do = _mm(dzb, w["a_w_o"][j], tb=True, out_dtype=BF16, name=f"mm_dao_{tag}")
    grads["a_w_o"][j] = _mm(o, dzb, ta=True, out_dtype=BF16, tk=GRAD_TK, name=f"mm_gao_{tag}")
    dq, dcur, dprev, dsink = _swa_bwd(qkv_r, w["a_sinks"][j], do, cos, sin, name=f"swa_bwd_{tag}", comm=make_comm())
    grads["a_sinks"][j] = dsink[0, :A_HEADS]
    dqkv = _swa_dqkv(dq, dcur, dprev, cos, sin, name=f"swa_dqkv_{tag}")
    grads["a_w_qkv"][j] = _mm(xb, dqkv, ta=True, out_dtype=BF16, tk=GRAD_TK, name=f"mm_gqkv_{tag}")
    return dqkv, w["a_w_qkv"][j]


def _mixer_b_fwd(xb, w, j, tag, gathers):
    xw = _mm(xb, w["b_w_in"][j], name=f"mm_bin_{tag}")
    wri = jnp.concatenate([w["b_w_rgate"][j], w["b_w_igate"][j]], axis=-1)
    y, u, h = _lru_fwd(xw, w["b_conv_w"][j], w["b_conv_b"][j][None], wri, w["b_b_rgate"][j][None],
                       w["b_b_igate"][j][None], w["b_lambda"][j][None], name=f"lru_fwd_{tag}", comm=gathers.get(f"lru_fwd_{tag}"))
    return (y, w["b_w_o"][j]), (xb, xw, wri, u, h, y)


def _mixer_b_bwd(dzb, res, w, j, tag, grads):
    xb, xw, wri, u, h, y = res
    dy = _mm(dzb, w["b_w_o"][j], tb=True, out_dtype=BF16, name=f"mm_dbo_{tag}")
    grads["b_w_o"][j] = _mm(y, dzb, ta=True, out_dtype=BF16, tk=GRAD_TK, name=f"mm_gbo_{tag}")
    dxw, dcw, dcb, dwri, dbr, dbi, dlam = _lru_bwd(
        xw, u, h, dy, w["b_conv_w"][j], wri, w["b_b_rgate"][j][None], w["b_b_igate"][j][None], w["b_lambda"][j][None],
        name=f"lru_bwd_{tag}")
    grads["b_conv_w"][j], grads["b_conv_b"][j] = dcw, dcb[0]
    grads["b_w_rgate"][j], grads["b_w_igate"][j] = dwri[..., :LRU_BLOCK_W].astype(BF16), dwri[..., LRU_BLOCK_W:].astype(BF16)
    grads["b_b_rgate"][j], grads["b_b_igate"][j], grads["b_lambda"][j] = dbr[0], dbi[0], dlam[0]
    grads["b_w_in"][j] = _mm(xb, dxw, ta=True, out_dtype=BF16, tk=GRAD_TK, name=f"mm_gbin_{tag}")
    return dxw, w["b_w_in"][j]


def _mla_weights(w, j):
    H = C_HEADS
    uq = w["c_w_uq"][j].reshape(C_Q_RANK, H, C_NOPE + C_ROPE)
    ukv = w["c_w_ukv"][j].reshape(C_KV_RANK, H, C_NOPE + C_V)
    uq_n = uq[:, :, :C_NOPE].reshape(C_Q_RANK, H * C_NOPE)
    uq_r = uq[:, :, C_NOPE:].reshape(C_Q_RANK, H * C_ROPE)
    ukv_p = jnp.concatenate([ukv[:, :, :C_NOPE].reshape(C_KV_RANK, H * C_NOPE),
                             ukv[:, :, C_NOPE:].reshape(C_KV_RANK, H * C_V)], axis=1)
    return uq_n, uq_r, ukv_p


def _mixer_c_fwd(xb, w, j, cos, sin, tag):
    S = xb.shape[0]
    H = C_HEADS
    uq_n, uq_r, ukv_p = _mla_weights(w, j)
    c = _mm(xb, w["c_w_down"][j], name=f"mm_cdown_{tag}")
    cq, ckv, kr = _mla_pre(c, w["c_q_norm"][j][None], w["c_kv_norm"][j][None], cos, sin, name=f"mla_pre_{tag}")
    qn = _mm(cq, uq_n, out_dtype=BF16, name=f"mm_uqn_{tag}")
    qr_flat = _rope_heads(_mm(cq, uq_r, name=f"mm_uqr_{tag}"), cos, sin, transpose=False, name=f"rope_qr_{tag}")
    qr = jnp.transpose(qr_flat.reshape(S, H, C_ROPE), (1, 0, 2))
    kv = _mm(ckv, ukv_p, out_dtype=BF16, name=f"mm_ukv_{tag}")
    o, lse = _mla_flash_fwd(qn, qr, kv, kr, name=f"mla_fwd_{tag}")
    return (o, w["c_w_o"][j]), (xb, c, cq, ckv, kr, qn, qr, kv, o, lse, uq_n, uq_r, ukv_p)


def _mixer_c_bwd(dzb, res, w, j, cos, sin, tag, grads):
    xb, c, cq, ckv, kr, qn, qr, kv, o, lse, uq_n, uq_r, ukv_p = res
    S = xb.shape[0]
    H = C_HEADS
    do = _mm(dzb, w["c_w_o"][j], tb=True, out_dtype=BF16, name=f"mm_dco_{tag}")
    grads["c_w_o"][j] = _mm(o, dzb, ta=True, out_dtype=BF16, tk=GRAD_TK, name=f"mm_gco_{tag}")
    delta = _mla_delta(do, o, name=f"mla_delta_{tag}")
    dqn, dqr = _mla_flash_dq(qn, qr, kv, kr, do, lse, delta, name=f"mla_dq_{tag}")
    dkn, dv, dkr_h = _mla_flash_dkv(qn, qr, kv, kr, do, lse, delta, name=f"mla_dkv_{tag}")
    dkv = jnp.concatenate([dkn, dv], axis=1)
    dqr_flat = _rope_heads(jnp.transpose(dqr, (1, 0, 2)).reshape(S, H * C_ROPE), cos, sin, transpose=True, name=f"rope_dqr_{tag}")
    g_uq_n = _mm(cq, dqn, ta=True, out_dtype=BF16, tk=GRAD_TK, name=f"mm_guqn_{tag}")
    g_uq_r = _mm(cq, dqr_flat, ta=True, out_dtype=BF16, tk=GRAD_TK, name=f"mm_guqr_{tag}")
    g_ukv = _mm(ckv, dkv, ta=True, out_dtype=BF16, tk=GRAD_TK, name=f"mm_gukv_{tag}")
    grads["c_w_uq"][j] = jnp.concatenate([g_uq_n.reshape(C_Q_RANK, H, C_NOPE), g_uq_r.reshape(C_Q_RANK, H, C_ROPE)],
                                         axis=2).reshape(C_Q_RANK, H * (C_NOPE + C_ROPE))
    grads["c_w_ukv"][j] = jnp.concatenate([g_ukv[:, :H * C_NOPE].reshape(C_KV_RANK, H, C_NOPE),
                                           g_ukv[:, H * C_NOPE:].reshape(C_KV_RANK, H, C_V)],
                                          axis=2).reshape(C_KV_RANK, H * (C_NOPE + C_V))
    dcq_a = _mm(dqn, uq_n, tb=True, name=f"mm_dcqa_{tag}")
    dcq_b = _mm(dqr_flat, uq_r, tb=True, name=f"mm_dcqb_{tag}")
    dckv = _mm(dkv, ukv_p, tb=True, name=f"mm_dckv_{tag}")
    dc, dqg, dkvg = _mla_post_bwd(c, dcq_a, dcq_b, dckv, dkr_h, w["c_q_norm"][j][None], w["c_kv_norm"][j][None], cos, sin,
                                  name=f"mla_post_{tag}")
    grads["c_q_norm"][j], grads["c_kv_norm"][j] = dqg[0], dkvg[0]
    grads["c_w_down"][j] = _mm(xb, dc, ta=True, out_dtype=BF16, tk=GRAD_TK, name=f"mm_gcdown_{tag}")
    return dc, w["c_w_down"][j]


def _layer_big(i, mixer=True, rest=True):
    kind, j = i % N_MIXERS, i // N_MIXERS
    own = [[("a_w_qkv", j), ("a_w_o", j)], [("b_w_in", j), ("b_w_rgate", j), ("b_w_igate", j), ("b_w_o", j)],
           [("c_w_down", j), ("c_w_uq", j), ("c_w_ukv", j), ("c_w_o", j)]][kind]
    return (own if mixer else []) + ([("x_w_q", i), ("x_w_o", i), ("f_w_up", i), ("f_w_down", i)] if rest else [])


def _local_step(x, mem, target, w, n_layers, gathers, scatter):
    S = x.shape[0]
    cos, sin = _rope_tables(S)
    grads = {n: [None] * n_layers[n] for n in WEIGHTS if n != "mem_w_kv"}

    xs, xb = x, x.astype(BF16)
    saved = []
    for i in range(DEPTH):
        kind, j = i % N_MIXERS, i // N_MIXERS
        tag = f"l{i}"
        if kind == 0:
            (act, w_out), res = _mixer_a_fwd(xb, w, j, cos, sin, tag, gathers)
        elif kind == 1:
            (act, w_out), res = _mixer_b_fwd(xb, w, j, tag, gathers)
        else:
            (act, w_out), res = _mixer_c_fwd(xb, w, j, cos, sin, tag)
        x1, x1b, xh1, rs1 = _mm_ln_fwd(act, w_out, xs, w["ln_g"][i, 0][None], w["ln_b"][i, 0][None], name=f"ln1_{tag}",
                                       comm=gathers.get(f"ln1_{tag}"))
        if i == 0:
            mkv = _mm(mem, w["mem_w_kv"], out_dtype=BF16, tm=MEM_LEN, name="mm_memkv")
        q = _mm(x1b, w["x_w_q"][i], out_dtype=BF16, name=f"mm_xq_{tag}")
        o = _xattn_fwd(q, mkv, name=f"xattn_fwd_{tag}")
        x2, x2b, xh2, rs2 = _mm_ln_fwd(o, w["x_w_o"][i], x1, w["ln_g"][i, 1][None], w["ln_b"][i, 1][None], name=f"ln2_{tag}")
        w_up = w["f_w_up"][i]
        cwp, cbp = _ffn_perm(w["f_conv_w"][i]), _ffn_perm(w["f_conv_b"][i][None])
        hh = _mm(x2b, w_up, out_dtype=BF16, tm=2 * MM_T, tn=FFN_TC, name=f"mm_up_{tag}", comm=gathers.get(f"mm_up_{tag}"))
        a = _ffn_act_fwd(hh, cwp, cbp, name=f"ffn_act_{tag}", comm=gathers.get(f"ffn_act_{tag}"))
        x3, x3b, xh3, rs3 = _mm_ln_fwd(a, w["f_w_down"][i], x2, w["ln_g"][i, 2][None], w["ln_b"][i, 2][None],
                                       name=f"ln3_{tag}", comm=gathers.get(f"ln3_{tag}"))
        saved.append((res, (xh1, rs1, x1b), (q, o, xh2, rs2, x2b), (w_up, cwp, cbp, hh, a, xh3, rs3)))
        xs, xb = x3, x3b

    dloss, loss = _loss_fwd(xs, target, name="loss")

    dmkv = None
    ln_dg = [[None] * 3 for _ in range(DEPTH)]
    ln_db = [[None] * 3 for _ in range(DEPTH)]
    for i in reversed(range(DEPTH)):
        kind, j = i % N_MIXERS, i // N_MIXERS
        tag = f"l{i}"
        res, (xh1, rs1, x1b), (q, o, xh2, rs2, x2b), (w_up, cwp, cbp, hh, a, xh3, rs3) = saved[i]
        if i == DEPTH - 1:
            dz3, dz3b, ln_dg[i][2], ln_db[i][2] = _ln_bwd(None, dloss, xh3, rs3, w["ln_g"][i, 2][None], name=f"ln3_bwd_{tag}")
        else:
            dz3, dz3b, ln_dg[i][2], ln_db[i][2] = _mm_ln_bwd(dz1, d_in, w_in, xh3, rs3, w["ln_g"][i, 2][None], name=f"ln3_bwd_{tag}")
        da = _mm(dz3b, w["f_w_down"][i], tb=True, out_dtype=BF16, name=f"mm_ddown_{tag}")
        grads["f_w_down"][i] = _mm(a, dz3b, ta=True, out_dtype=BF16, tm=FFN_TC, tk=GRAD_TK, name=f"mm_gdown_{tag}")
        later = scatter(_layer_big(i + 1), grads) if i + 1 < DEPTH else None
        dh, dcw, dcb = _ffn_act_bwd(hh, da, cwp, cbp, name=f"ffn_act_bwd_{tag}", comm=later)
        grads["f_conv_w"][i], grads["f_conv_b"][i] = _ffn_unperm(dcw), _ffn_unperm(dcb)[0]
        grads["f_w_up"][i] = _mm(x2b, dh, ta=True, out_dtype=BF16, tn=FFN_TC, tk=GRAD_TK, name=f"mm_gup_{tag}")

        dz2, dz2b, ln_dg[i][1], ln_db[i][1] = _mm_ln_bwd(dz3, dh, w_up, xh2, rs2, w["ln_g"][i, 1][None], name=f"ln2_bwd_{tag}")
        do = _mm(dz2b, w["x_w_o"][i], tb=True, out_dtype=BF16, name=f"mm_dxo_{tag}")
        grads["x_w_o"][i] = _mm(o, dz2b, ta=True, out_dtype=BF16, tk=GRAD_TK, name=f"mm_gxo_{tag}")
        dq, dmkv_i = _xattn_bwd(q, mkv, do, name=f"xattn_bwd_{tag}")
        dmkv = dmkv_i if dmkv is None else dmkv + dmkv_i
        if i == 0:
            g_mem = _mm(mem, dmkv, ta=True, out_dtype=BF16, tm=512, name="mm_gmemkv")
        grads["x_w_q"][i] = _mm(x1b, dq, ta=True, out_dtype=BF16, tk=GRAD_TK, name=f"mm_gxq_{tag}")

        dz1, dz1b, ln_dg[i][0], ln_db[i][0] = _mm_ln_bwd(dz2, dq, w["x_w_q"][i], xh1, rs1, w["ln_g"][i, 0][None], name=f"ln1_bwd_{tag}")
        if kind == 0:
            if i == 0:
                grads["mem_w_kv"] = g_mem
                done = lambda: scatter(_layer_big(0, mixer=False) + [("a_w_o", 0), ("mem_w_kv", None)], grads)
            else:
                done = lambda: None
            d_in, w_in = _mixer_a_bwd(dz1b, res, w, j, cos, sin, tag, grads, done)
        elif kind == 1:
            d_in, w_in = _mixer_b_bwd(dz1b, res, w, j, tag, grads)
        else:
            d_in, w_in = _mixer_c_bwd(dz1b, res, w, j, cos, sin, tag, grads)

    grad_x = _axpy(dz1, _mm(d_in, w_in, tb=True, name="mm_dx_l0"), name="grad_x")
    big = [n for n, _ in SHARDED[:N_BIG]]
    out = {n: (g if n in big else jnp.stack(g, axis=0)) for n, g in grads.items() if n not in ("ln_g", "ln_b")}
    out["ln_g"] = jnp.stack([jnp.concatenate(r, axis=0) for r in ln_dg], axis=0)
    out["ln_b"] = jnp.stack([jnp.concatenate(r, axis=0) for r in ln_db], axis=0)
    return loss, grad_x, out


def kernel(x, mem, a_w_qkv, a_sinks, a_w_o, b_w_in, b_conv_w, b_conv_b, b_w_rgate, b_b_rgate, b_w_igate, b_b_igate, b_lambda, b_w_o, c_w_down, c_q_norm, c_kv_norm, c_w_uq, c_w_ukv, c_w_o, mem_w_kv, x_w_q, x_w_o, f_w_up, f_conv_w, f_conv_b, f_w_down, ln_g, ln_b, loss_target, m_a_w_qkv, m_a_sinks, m_a_w_o, m_b_w_in, m_b_conv_w, m_b_conv_b, m_b_w_rgate, m_b_b_rgate, m_b_w_igate, m_b_b_igate, m_b_lambda, m_b_w_o, m_c_w_down, m_c_q_norm, m_c_kv_norm, m_c_w_uq, m_c_w_ukv, m_c_w_o, m_mem_w_kv, m_x_w_q, m_x_w_o, m_f_w_up, m_f_conv_w, m_f_conv_b, m_f_w_down, m_ln_g, m_ln_b, v_a_w_qkv, v_a_sinks, v_a_w_o, v_b_w_in, v_b_conv_w, v_b_conv_b, v_b_w_rgate, v_b_b_rgate, v_b_w_igate, v_b_b_igate, v_b_lambda, v_b_w_o, v_c_w_down, v_c_q_norm, v_c_kv_norm, v_c_w_uq, v_c_w_ukv, v_c_w_o, v_mem_w_kv, v_x_w_q, v_x_w_o, v_f_w_up, v_f_conv_w, v_f_conv_b, v_f_w_down, v_ln_g, v_ln_b):
    loc = locals()
    shard = {n: loc[n] for n in WEIGHTS}
    mom = {n: loc["m_" + n] for n in WEIGHTS}
    var = {n: loc["v_" + n] for n in WEIGHTS}
    names = [n for n, _ in SHARDED]
    axis = dict(SHARDED)
    big, small = names[:N_BIG], names[N_BIG:]

    chip = 2 * lax.axis_index("x") + lax.axis_index("y")
    n_layers = {n: shard[n].shape[0] for n in WEIGHTS if n != "mem_w_kv"}
    layer_axis = lambda n, j: axis[n] - (0 if j is None else 1)

    shard_b = {n: shard[n].astype(BF16) for n in big}
    w = {n: [None] * n_layers[n] for n in big if n != "mem_w_kv"}

    small_pack = _pack([shard[n] for n in small], F32)
    shard_b["small"] = small_pack

    def gather(pairs):
        def deliver(outs):
            for (n, j), o in zip(pairs, outs):
                if j is None:
                    w[n] = o
                else:
                    w[n][j] = o
        return _gather_comm([(shard_b[n], j, 0 if n == "small" else layer_axis(n, j), n == "f_w_up") for n, j in pairs], deliver)

    _run_comm(gather([("a_w_qkv", 0), ("small", None)]), name="gather_first")
    gathers = {
        "mm_qkv_l0": gather([("a_w_o", 0)]),
        "rope_qkv_l0": gather([("x_w_q", 0), ("mem_w_kv", None)]),
        "swa_fwd_l0": gather([("x_w_o", 0), ("f_w_up", 0)]),
        "ln1_l0": gather([("f_w_down", 0)]),
        "mm_up_l0": gather(_layer_big(1, rest=False) + [("x_w_q", 1), ("x_w_o", 1)]),
        "ffn_act_l0": gather([("f_w_up", 1)]),
        "ln3_l0": gather([("f_w_down", 1)]),
        "lru_fwd_l1": gather(_layer_big(2, rest=False) + [("x_w_q", 2), ("x_w_o", 2)]),
        "mm_up_l1": gather([("f_w_up", 2)]),
        "ffn_act_l1": gather([("f_w_down", 2)] + _layer_big(3, rest=False)),
        "mm_up_l2": gather([("f_w_up", 3)]),
        "ffn_act_l2": gather([("f_w_down", 3), ("x_w_q", 3), ("x_w_o", 3)]),
    }
    got = w.pop("small").reshape((4,) + small_pack.shape)
    per_chip = [_unpack(got[s], [shard[n].shape for n in small]) for s in range(4)]
    for k, n in enumerate(small):
        w[n] = jnp.concatenate([per_chip[s][k] for s in range(4)], axis=axis[n])
    for n in REPLICATED:
        w[n] = shard[n]

    recv = {}

    def scatter(pairs, grads):
        def deliver(outs):
            recv.update(dict(zip(pairs, outs)))
        return _scatter_comm([(grads[n] if j is None else grads[n][j], layer_axis(n, j), n == "f_w_up") for n, j in pairs], deliver)

    loss, grad_x, g = _local_step(x[0], mem[0], loss_target[0], w, n_layers, gathers, scatter)
    _run_comm(scatter([("a_w_qkv", 0)], g), name="scatter_last")

    view = {n: (int(np.prod(shard[n].shape[:-1])), shard[n].shape[-1]) for n in big}
    parts = []
    for n in big:
        layers = n_layers.get(n, 1)
        part = None
        for j in range(layers):
            r = recv[(n, j if n in n_layers else None)]
            part = _sum_partials(r.reshape(4, view[n][0] // layers, view[n][1]), part, j, layers, name=f"sum_{n}_{j}")
        parts.append(part)
    sibs = _swap_cores(parts, name="swap_cores")
    grad_o, delta_o, m_o, v_o = {}, {}, {}, {}
    for n, part, sib in zip(big, parts, sibs):
        res = _adamw(shard[n].reshape(view[n]), mom[n].reshape(view[n]), var[n].reshape(view[n]), part, sib, name=f"adamw_{n}")
        for d, r in zip((grad_o, delta_o, m_o, v_o), res):
            d[n] = r.reshape(shard[n].shape)

    rest = small + REPLICATED
    vec = _pack([g[n] for n in rest] + [loss], F32, cols=LANES, row_mult=8)
    tot = _unpack(_all_reduce_small(vec, name="allreduce_small"), [g[n].shape for n in rest] + [(1, 1)], cols=LANES)
    loss_tot = tot[-1].reshape(())
    mine = {n: t for n, t in zip(rest, tot)}
    for n in small:
        size = shard[n].shape[axis[n]]
        mine[n] = lax.dynamic_slice_in_dim(mine[n], chip * size, size, axis=axis[n])
    rpack = lambda d: _pack([d[n] for n in rest], F32, cols=LANES, row_mult=8)
    res = _adamw(rpack(shard), rpack(mom), rpack(var), rpack(mine), None, name="adamw_small")
    for d, r in zip((grad_o, delta_o, m_o, v_o), res):
        d.update(dict(zip(rest, _unpack(r, [shard[n].shape for n in rest], cols=LANES))))

    return (loss_tot, grad_x[None], *[grad_o[n] for n in WEIGHTS], *[delta_o[n] for n in WEIGHTS],
            *[m_o[n] for n in WEIGHTS], *[v_o[n] for n in WEIGHTS])
```

```python
import functools
import math

import numpy as np
import jax
import jax.numpy as jnp
from jax import lax
from jax.experimental import pallas as pl
from jax.experimental.pallas import tpu as pltpu

F32 = jnp.float32
BF16 = jnp.bfloat16
MESH = pl.DeviceIdType.MESH

D_MODEL = 1024
DEPTH = 4
N_MIXERS = 3
MEM_LEN = 256
BLOCK = 128
ROPE_THETA = 10000.0
NEG = -1e30
LN_EPS = 1e-5
RMS_EPS = 1e-6
A_HEADS, A_KV_HEADS, A_HEAD_DIM = 16, 4, 64
LRU_BLOCKS, LRU_BLOCK_W, LRU_CONV, LRU_C = 4, 256, 4, 8.0
C_HEADS, C_NOPE, C_ROPE, C_V, C_Q_RANK, C_KV_RANK = 8, 128, 64, 128, 384, 256
X_HEADS, X_HEAD_DIM = 4, 256
D_FF, FFN_CONV = 2816, 3
ALPHA = (2.0 * DEPTH) ** 0.25
ADAM_LR, ADAM_B1, ADAM_B2, ADAM_EPS, ADAM_WD, ADAM_STEP = 0.001, 0.9, 0.999, 1e-08, 0.01, 10

VMEM_LIMIT = 56 * 2 ** 20
LANES = 128
PACK_COLS = 1024
ROW_T = 512
ACT_T = 256
LRU_T = 256
FLASH_T = 512
FFN_TC = 1408
MM_T = 1024
GRAD_TK = 2048

C11 = (((1,), (1,)), ((), ()))
C00 = (((0,), (0,)), ((), ()))

SHARDED = [
    ("a_w_qkv", 2), ("a_w_o", 1), ("b_w_in", 2), ("b_w_rgate", 2), ("b_w_igate", 2), ("b_w_o", 1), ("c_w_down", 1),
    ("c_w_uq", 2), ("c_w_ukv", 2), ("c_w_o", 1), ("mem_w_kv", 1), ("x_w_q", 1), ("x_w_o", 1), ("f_w_up", 2),
    ("f_w_down", 1),
    ("b_conv_w", 2), ("c_q_norm", 1), ("c_kv_norm", 1), ("f_conv_w", 2), ("ln_g", 2), ("ln_b", 2),
]
N_BIG = 15
REPLICATED = ["a_sinks", "b_conv_b", "b_b_rgate", "b_b_igate", "b_lambda", "f_conv_b"]
WEIGHTS = ["a_w_qkv", "a_sinks", "a_w_o", "b_w_in", "b_conv_w", "b_conv_b", "b_w_rgate", "b_b_rgate", "b_w_igate",
           "b_b_igate", "b_lambda", "b_w_o", "c_w_down", "c_q_norm", "c_kv_norm", "c_w_uq", "c_w_ukv", "c_w_o",
           "mem_w_kv", "x_w_q", "x_w_o", "f_w_up", "f_conv_w", "f_conv_b", "f_w_down", "ln_g", "ln_b"]


def _params(*sem):
    return pltpu.CompilerParams(dimension_semantics=sem, vmem_limit_bytes=VMEM_LIMIT)


def _sds(shape, dtype):
    return jax.ShapeDtypeStruct(tuple(shape), dtype)


def _mm(a, b, *, name, ta=False, tb=False, out_dtype=F32, tm=None, tn=None, tk=None, comm=None):
    (K, M) = a.shape if ta else a.shape[::-1]
    (N, K2) = b.shape if tb else b.shape[::-1]
    assert K == K2, (a.shape, b.shape, ta, tb)
    tm = min(tm or MM_T, M)
    tn = min(tn or N, N)
    tk = min(tk or K, K)
    assert M % tm == 0 and N % tn == 0 and K % tk == 0, (M, N, K, tm, tn, tk)
    nk = K // tk
    use_acc = nk > 1 and out_dtype != F32
    dims = (((0 if ta else 1,), (1 if tb else 0,)), ((), ()))

    def body(a_ref, b_ref, o_ref, *scratch):
        p = lax.dot_general(a_ref[...].astype(BF16), b_ref[...].astype(BF16), dims, preferred_element_type=F32)
        if nk == 1:
            o_ref[...] = p.astype(out_dtype)
        else:
            acc = scratch[0] if use_acc else o_ref
            k = pl.program_id(2)

            @pl.when(k == 0)
            def _():
                acc[...] = p

            @pl.when(k > 0)
            def _():
                acc[...] += p

            if use_acc:
                @pl.when(k == nk - 1)
                def _():
                    o_ref[...] = acc[...].astype(out_dtype)

    a_spec = pl.BlockSpec((tk, tm), lambda i, j, k: (k, i)) if ta else pl.BlockSpec((tm, tk), lambda i, j, k: (i, k))
    b_spec = pl.BlockSpec((tn, tk), lambda i, j, k: (j, k)) if tb else pl.BlockSpec((tk, tn), lambda i, j, k: (k, j))
    return _call(
        body, name=name, grid=(M // tm, N // tn, nk), in_specs=[a_spec, b_spec],
        out_specs=[pl.BlockSpec((tm, tn), lambda i, j, k: (i, j))], out_shape=[_sds((M, N), out_dtype)],
        scratch_shapes=[pltpu.VMEM((tm, tn), F32)] if use_acc else [],
        params=_params("parallel", "parallel", "arbitrary"), comm=comm,
    )(a, b)[0]


def _shift_down(cur, prev8, d):
    rolled = pltpu.roll(cur, d, 0)
    rid = lax.broadcasted_iota(jnp.int32, prev8.shape, 0)
    head = jnp.where(rid < d, pltpu.roll(prev8, d, 0), rolled[0:8])
    return jnp.concatenate([head, rolled[8:]], axis=0)


def _shift_up(cur, next8, d):
    n = cur.shape[0]
    rolled = pltpu.roll(cur, n - d, 0)
    rid = lax.broadcasted_iota(jnp.int32, next8.shape, 0)
    tail = jnp.where(rid >= 8 - d, pltpu.roll(next8, 8 - d, 0), rolled[n - 8:n])
    return jnp.concatenate([rolled[0:n - 8], tail], axis=0)


def _swap_halves(x):
    w = x.shape[-1]
    if w == 64:
        return jnp.concatenate([x[:, 32:64], x[:, 0:32]], axis=1)
    lane = lax.broadcasted_iota(jnp.int32, x.shape, 1)
    return jnp.where((lane % 64) < 32, pltpu.roll(x, w - 32, 1), pltpu.roll(x, 32, 1))


def _tile_lanes(t, w):
    return t if w == t.shape[-1] else jnp.concatenate([t] * (w // t.shape[-1]), axis=1)


def _rope(x, cos, sin):
    w = x.shape[-1]
    if w == 64:
        cos, sin = cos[:, :64], sin[:, :64]
    else:
        cos, sin = _tile_lanes(cos, w), _tile_lanes(sin, w)
    return x * cos + _swap_halves(x) * sin


def _rope_t(x, cos, sin):
    w = x.shape[-1]
    if w == 64:
        cos, sin = cos[:, :64], sin[:, :64]
    else:
        cos, sin = _tile_lanes(cos, w), _tile_lanes(sin, w)
    return x * cos - _swap_halves(x) * sin


def _sigmoid(x):
    return 1.0 / (1.0 + jnp.exp(-x))


def _gelu_and_grad(x):
    c0, c1 = math.sqrt(2.0 / math.pi), 0.044715
    t = jnp.tanh(c0 * (x + c1 * x * x * x))
    g = 0.5 * x * (1.0 + t)
    dg = 0.5 * (1.0 + t) + 0.5 * x * (1.0 - t * t) * c0 * (1.0 + 3.0 * c1 * x * x)
    return g, dg


def _neg_expm1(x):
    series = -x * (1.0 + x * (0.5 + x * (1.0 / 6.0 + x * (1.0 / 24.0 + x * (1.0 / 120.0)))))
    return jnp.where(x > -0.1, series, 1.0 - jnp.exp(x))


def _softplus_neg(lam):
    z = -lam
    e = jnp.exp(-jnp.abs(z))
    log1p = jnp.where(e < 0.01, e * (1.0 - e * (0.5 - e * (1.0 / 3.0))), jnp.log(1.0 + e))
    sp = jnp.maximum(z, 0.0) + log1p
    dsp = -_sigmoid(z)
    return sp, dsp


def _ln_fwd(x, y, g, b, *, name):
    S, D = x.shape
    tm = min(ROW_T, S)

    def body(x_ref, y_ref, g_ref, b_ref, o_ref, ob_ref, xh_ref, rs_ref):
        z = ALPHA * x_ref[...] + y_ref[...]
        mu = jnp.mean(z, axis=-1, keepdims=True)
        zc = z - mu
        var = jnp.mean(zc * zc, axis=-1, keepdims=True)
        r = lax.rsqrt(var + LN_EPS)
        xh = zc * r
        o = xh * g_ref[...] + b_ref[...]
        o_ref[...] = o
        ob_ref[...] = o.astype(BF16)
        xh_ref[...] = xh
        rs_ref[...] = r

    row = pl.BlockSpec((tm, D), lambda i: (i, 0))
    vec = pl.BlockSpec((1, D), lambda i: (0, 0))
    return pl.pallas_call(
        body, name=name, grid=(S // tm,), in_specs=[row, row, vec, vec],
        out_specs=[row, row, row, pl.BlockSpec((tm, 1), lambda i: (i, 0))],
        out_shape=[_sds((S, D), F32), _sds((S, D), BF16), _sds((S, D), F32), _sds((S, 1), F32)],
        compiler_params=_params("parallel"),
    )(x, y, g, b)


def _ln_bwd(d1, d2, xh, rs, g, *, name):
    S, D = xh.shape
    tm = min(ROW_T, S)
    has_d1 = d1 is not None

    def body(*refs):
        if has_d1:
            d1_ref, d2_ref, xh_ref, rs_ref, g_ref, dz_ref, dzb_ref, dg_ref, db_ref = refs
            dout = ALPHA * d1_ref[...] + d2_ref[...]
        else:
            d2_ref, xh_ref, rs_ref, g_ref, dz_ref, dzb_ref, dg_ref, db_ref = refs
            dout = d2_ref[...]
        xh_v = xh_ref[...]
        dxh = dout * g_ref[...]
        m1 = jnp.mean(dxh, axis=-1, keepdims=True)
        m2 = jnp.mean(dxh * xh_v, axis=-1, keepdims=True)
        dz = rs_ref[...] * (dxh - m1 - xh_v * m2)
        dz_ref[...] = dz
        dzb_ref[...] = dz.astype(BF16)

        @pl.when(pl.program_id(0) == 0)
        def _():
            dg_ref[...] = jnp.zeros_like(dg_ref)
            db_ref[...] = jnp.zeros_like(db_ref)

        dg_ref[...] += jnp.sum(dout * xh_v, axis=0, keepdims=True)
        db_ref[...] += jnp.sum(dout, axis=0, keepdims=True)

    row = pl.BlockSpec((tm, D), lambda i: (i, 0))
    vec = pl.BlockSpec((1, D), lambda i: (0, 0))
    ins = ([row] if has_d1 else []) + [row, row, pl.BlockSpec((tm, 1), lambda i: (i, 0)), vec]
    args = ([d1] if has_d1 else []) + [d2, xh, rs, g]
    return pl.pallas_call(
        body, name=name, grid=(S // tm,), in_specs=ins, out_specs=[row, row, vec, vec],
        out_shape=[_sds((S, D), F32), _sds((S, D), BF16), _sds((1, D), F32), _sds((1, D), F32)],
        compiler_params=_params("arbitrary"),
    )(*args)


def _mm_ln_fwd(a, b, x, g, beta, *, name, comm=None):
    S, K = a.shape
    D = b.shape[1]
    tm = min(ROW_T, S)

    def body(a_ref, b_ref, x_ref, g_ref, beta_ref, o_ref, ob_ref, xh_ref, rs_ref):
        y = jnp.dot(a_ref[...].astype(BF16), b_ref[...].astype(BF16), preferred_element_type=F32)
        z = ALPHA * x_ref[...] + y
        mu = jnp.mean(z, axis=-1, keepdims=True)
        zc = z - mu
        var = jnp.mean(zc * zc, axis=-1, keepdims=True)
        r = lax.rsqrt(var + LN_EPS)
        xh = zc * r
        o = xh * g_ref[...] + beta_ref[...]
        o_ref[...] = o
        ob_ref[...] = o.astype(BF16)
        xh_ref[...] = xh
        rs_ref[...] = r

    row = pl.BlockSpec((tm, D), lambda i: (i, 0))
    vec = pl.BlockSpec((1, D), lambda i: (0, 0))
    return _call(
        body, name=name, grid=(S // tm,),
        in_specs=[pl.BlockSpec((tm, K), lambda i: (i, 0)), pl.BlockSpec((K, D), lambda i: (0, 0)), row, vec, vec],
        out_specs=[row, row, row, pl.BlockSpec((tm, 1), lambda i: (i, 0))],
        out_shape=[_sds((S, D), F32), _sds((S, D), BF16), _sds((S, D), F32), _sds((S, 1), F32)],
        params=_params("arbitrary"), comm=comm,
    )(a, b, x, g, beta)


def _mm_ln_bwd(d1, da, wt, xh, rs, g, *, name):
    S, D = xh.shape
    K = da.shape[1]
    tm = min(ROW_T // 2, S)

    def body(d1_ref, da_ref, wt_ref, xh_ref, rs_ref, g_ref, dz_ref, dzb_ref, dg_ref, db_ref):
        d2 = lax.dot_general(da_ref[...].astype(BF16), wt_ref[...].astype(BF16), C11, preferred_element_type=F32)
        dout = ALPHA * d1_ref[...] + d2
        xh_v = xh_ref[...]
        dxh = dout * g_ref[...]
        m1 = jnp.mean(dxh, axis=-1, keepdims=True)
        m2 = jnp.mean(dxh * xh_v, axis=-1, keepdims=True)
        dz = rs_ref[...] * (dxh - m1 - xh_v * m2)
        dz_ref[...] = dz
        dzb_ref[...] = dz.astype(BF16)

        @pl.when(pl.program_id(0) == 0)
        def _():
            dg_ref[...] = jnp.zeros_like(dg_ref)
            db_ref[...] = jnp.zeros_like(db_ref)

        dg_ref[...] += jnp.sum(dout * xh_v, axis=0, keepdims=True)
        db_ref[...] += jnp.sum(dout, axis=0, keepdims=True)

    row = pl.BlockSpec((tm, D), lambda i: (i, 0))
    vec = pl.BlockSpec((1, D), lambda i: (0, 0))
    return pl.pallas_call(
        body, name=name, grid=(S // tm,),
        in_specs=[row, pl.BlockSpec((tm, K), lambda i: (i, 0)), pl.BlockSpec((D, K), lambda i: (0, 0)), row,
                  pl.BlockSpec((tm, 1), lambda i: (i, 0)), vec],
        out_specs=[row, row, vec, vec],
        out_shape=[_sds((S, D), F32), _sds((S, D), BF16), _sds((1, D), F32), _sds((1, D), F32)],
        compiler_params=_params("arbitrary"),
    )(d1, da, wt, xh, rs, g)


def _loss_fwd(y, target, *, name):
    S, D = y.shape
    tm = min(ROW_T, S)

    def body(y_ref, t_ref, d_ref, l_ref):
        e = y_ref[...] - t_ref[...]
        d_ref[...] = e * (1.0 / D)

        @pl.when(pl.program_id(0) == 0)
        def _():
            l_ref[...] = jnp.zeros_like(l_ref)

        part = jnp.sum(e * e, axis=0, keepdims=True)
        l_ref[...] += (0.5 / D) * jnp.sum(part, axis=1, keepdims=True)

    row = pl.BlockSpec((tm, D), lambda i: (i, 0))
    return pl.pallas_call(
        body, name=name, grid=(S // tm,), in_specs=[row, row],
        out_specs=[row, pl.BlockSpec((1, 1), lambda i: (0, 0))], out_shape=[_sds((S, D), F32), _sds((1, 1), F32)],
        compiler_params=_params("arbitrary"),
    )(y, target)


def _axpy(d1, d2, *, name):
    S, D = d1.shape
    tm = min(ROW_T, S)

    def body(a_ref, b_ref, o_ref):
        o_ref[...] = ALPHA * a_ref[...] + b_ref[...]

    row = pl.BlockSpec((tm, D), lambda i: (i, 0))
    return pl.pallas_call(body, name=name, grid=(S // tm,), in_specs=[row, row], out_specs=row,
                          out_shape=_sds((S, D), F32), compiler_params=_params("parallel"))(d1, d2)


def _ffn_act_fwd(h, cw, cb, *, name, comm=None):
    S, W = h.shape
    tc = FFN_TC
    nj = W // (2 * tc)
    tm = min(ACT_T, S)

    def body(h_ref, w_ref, b_ref, a_ref, carry):
        @pl.when(pl.program_id(1) == 0)
        def _():
            carry[...] = jnp.zeros_like(carry)

        cur = h_ref[...].astype(F32)
        prev8 = carry[...]
        hc = cur * w_ref[2:3, :] + _shift_down(cur, prev8, 1) * w_ref[1:2, :] + _shift_down(cur, prev8, 2) * w_ref[0:1, :]
        hc = hc + b_ref[...]
        carry[...] = cur[tm - 8:tm]
        hg, hu = hc[:, :tc], hc[:, tc:]
        a_ref[...] = (hg * _sigmoid(hg) * hu).astype(BF16)

    return _call(
        body, name=name, grid=(nj, S // tm),
        in_specs=[pl.BlockSpec((tm, 2 * tc), lambda j, i: (i, j)), pl.BlockSpec((3, 2 * tc), lambda j, i: (0, j)),
                  pl.BlockSpec((1, 2 * tc), lambda j, i: (0, j))],
        out_specs=[pl.BlockSpec((tm, tc), lambda j, i: (i, j))], out_shape=[_sds((S, W // 2), BF16)],
        scratch_shapes=[pltpu.VMEM((8, 2 * tc), F32)],
        params=_params("parallel", "arbitrary"), comm=comm,
    )(h, cw, cb)[0]


def _ffn_act_bwd(h, da, cw, cb, *, name, comm=None):
    S, W = h.shape
    tc = FFN_TC
    nj = W // (2 * tc)
    tm = min(ACT_T, S)
    ni = S // tm

    def body(h_ref, hp_ref, da_ref, w_ref, b_ref, dh_ref, dw_ref, db_ref, carry):
        i = pl.program_id(1)
        r = ni - 1 - i

        @pl.when(i == 0)
        def _():
            carry[...] = jnp.zeros_like(carry)
            dw_ref[...] = jnp.zeros_like(dw_ref)
            db_ref[...] = jnp.zeros_like(db_ref)

        cur = h_ref[...].astype(F32)
        prev8 = jnp.where(r > 0, hp_ref[8:16, :].astype(F32), 0.0)
        sh = [cur, _shift_down(cur, prev8, 1), _shift_down(cur, prev8, 2)]
        hc = sh[0] * w_ref[2:3, :] + sh[1] * w_ref[1:2, :] + sh[2] * w_ref[0:1, :] + b_ref[...]
        hg, hu = hc[:, :tc], hc[:, tc:]
        d = da_ref[...].astype(F32)
        sg = _sigmoid(hg)
        dg = d * hu * (sg * (1.0 + hg * (1.0 - sg)))
        du = d * (hg * sg)
        dhc = jnp.concatenate([dg, du], axis=1)
        db_ref[...] += jnp.sum(dhc, axis=0, keepdims=True)
        for k in range(3):
            dw_ref[k:k + 1, :] += jnp.sum(dhc * sh[2 - k], axis=0, keepdims=True)
        next8 = carry[...]
        dh = dhc * w_ref[2:3, :] + _shift_up(dhc, next8, 1) * w_ref[1:2, :] + _shift_up(dhc, next8, 2) * w_ref[0:1, :]
        carry[...] = dhc[0:8]
        dh_ref[...] = dh.astype(BF16)

    rev = lambda j, i: (ni - 1 - i, j)
    return _call(
        body, name=name, grid=(nj, ni),
        in_specs=[pl.BlockSpec((tm, 2 * tc), rev),
                  pl.BlockSpec((16, 2 * tc), lambda j, i: (jnp.maximum((ni - 1 - i) * (tm // 16) - 1, 0), j)),
                  pl.BlockSpec((tm, tc), rev), pl.BlockSpec((3, 2 * tc), lambda j, i: (0, j)),
                  pl.BlockSpec((1, 2 * tc), lambda j, i: (0, j))],
        out_specs=[pl.BlockSpec((tm, 2 * tc), rev), pl.BlockSpec((3, 2 * tc), lambda j, i: (0, j)),
                   pl.BlockSpec((1, 2 * tc), lambda j, i: (0, j))],
        out_shape=[_sds((S, W), BF16), _sds((3, W), F32), _sds((1, W), F32)],
        scratch_shapes=[pltpu.VMEM((8, 2 * tc), F32)],
        params=_params("parallel", "arbitrary"), comm=comm,
    )(h, h, da, cw, cb)


def _xattn_softmax(qk):
    s = qk * (X_HEAD_DIM ** -0.5)
    p = jnp.exp(s - jnp.max(s, axis=-1, keepdims=True))
    return p / jnp.sum(p, axis=-1, keepdims=True)


def _xattn_fwd(q, mkv, *, name):
    S, D = q.shape
    tm = min(ROW_T, S)

    def body(q_ref, k_ref, v_ref, o_ref):
        sls = [slice(h * X_HEAD_DIM, (h + 1) * X_HEAD_DIM) for h in range(X_HEADS)]
        scores = lambda h: lax.dot_general(q_ref[:, sls[h]], k_ref[:, sls[h]], C11, preferred_element_type=F32)
        nxt = scores(0)
        for h in range(X_HEADS):
            qk = nxt
            if h + 1 < X_HEADS:
                nxt = scores(h + 1)
            p = _xattn_softmax(qk)
            o_ref[:, sls[h]] = jnp.dot(p.astype(BF16), v_ref[:, sls[h]], preferred_element_type=F32).astype(BF16)

    return pl.pallas_call(
        body, name=name, grid=(S // tm,),
        in_specs=[pl.BlockSpec((tm, D), lambda i: (i, 0)), pl.BlockSpec((MEM_LEN, D), lambda i: (0, 0)),
                  pl.BlockSpec((MEM_LEN, D), lambda i: (0, 1))],
        out_specs=pl.BlockSpec((tm, D), lambda i: (i, 0)), out_shape=_sds((S, D), BF16),
        compiler_params=_params("parallel"),
    )(q, mkv, mkv)


def _xattn_bwd(q, mkv, do, *, name):
    S, D = q.shape
    tm = min(ROW_T, S)
    scale = X_HEAD_DIM ** -0.5

    def body(q_ref, k_ref, v_ref, do_ref, dq_ref, dkv_ref):
        @pl.when(pl.program_id(0) == 0)
        def _():
            dkv_ref[...] = jnp.zeros_like(dkv_ref)

        def products(h):
            sl = slice(h * X_HEAD_DIM, (h + 1) * X_HEAD_DIM)
            return (lax.dot_general(q_ref[:, sl], k_ref[:, sl], C11, preferred_element_type=F32),
                    lax.dot_general(do_ref[:, sl], v_ref[:, sl], C11, preferred_element_type=F32))

        nxt = products(0)
        for h in range(X_HEADS):
            qk, dp = nxt
            if h + 1 < X_HEADS:
                nxt = products(h + 1)
            sl = slice(h * X_HEAD_DIM, (h + 1) * X_HEAD_DIM)
            sv = slice(D + h * X_HEAD_DIM, D + (h + 1) * X_HEAD_DIM)
            qh, kh, doh = q_ref[:, sl], k_ref[:, sl], do_ref[:, sl]
            p = _xattn_softmax(qk)
            ds = (p * (dp - jnp.sum(p * dp, axis=-1, keepdims=True)) * scale).astype(BF16)
            dq_ref[:, sl] = jnp.dot(ds, kh, preferred_element_type=F32).astype(BF16)
            dkv_ref[:, sl] += lax.dot_general(ds, qh, C00, preferred_element_type=F32)
            dkv_ref[:, sv] += lax.dot_general(p.astype(BF16), doh, C00, preferred_element_type=F32)

    row = pl.BlockSpec((tm, D), lambda i: (i, 0))
    return pl.pallas_call(
        body, name=name, grid=(S // tm,),
        in_specs=[row, pl.BlockSpec((MEM_LEN, D), lambda i: (0, 0)), pl.BlockSpec((MEM_LEN, D), lambda i: (0, 1)), row],
        out_specs=[row, pl.BlockSpec((MEM_LEN, 2 * D), lambda i: (0, 0))],
        out_shape=[_sds((S, D), BF16), _sds((MEM_LEN, 2 * D), F32)],
        compiler_params=_params("arbitrary"),
    )(q, mkv, mkv, do)


def _rope_cols(x, cos, sin, n_rope, *, name, comm=None):
    S, W = x.shape
    tm = min(ROW_T, S)

    def body(x_ref, c_ref, s_ref, o_ref):
        o_ref[:, :n_rope] = _rope(x_ref[:, :n_rope], c_ref[...], s_ref[...]).astype(BF16)
        if n_rope < W:
            o_ref[:, n_rope:] = x_ref[:, n_rope:].astype(BF16)

    row = pl.BlockSpec((tm, W), lambda i: (i, 0))
    tab = pl.BlockSpec((tm, LANES), lambda i: (i, 0))
    return _call(body, name=name, grid=(S // tm,), in_specs=[row, tab, tab], out_specs=[row],
                 out_shape=[_sds((S, W), BF16)], params=_params("arbitrary"), comm=comm)(x, cos, sin)[0]


def _swa_band(n, stacked):
    qi = jnp.bitwise_and(lax.broadcasted_iota(jnp.int32, (stacked * BLOCK, 2 * BLOCK), 0), BLOCK - 1)
    kj = lax.broadcasted_iota(jnp.int32, (stacked * BLOCK, 2 * BLOCK), 1)
    first = jnp.where(n > 0, 0, BLOCK)
    return ((kj < BLOCK) & (kj > qi + first)) | ((kj >= BLOCK) & (kj - BLOCK <= qi))


def _swa_sink_rows(sink_ref, heads):
    row = lax.broadcasted_iota(jnp.int32, (len(heads) * BLOCK, 1), 0)
    col = jnp.full(row.shape, sink_ref[heads[-1]], F32)
    for gi in range(len(heads) - 2, -1, -1):
        col = jnp.where(row < (gi + 1) * BLOCK, sink_ref[heads[gi]], col)
    return col


def _swa_softmax(qk, band, sink):
    s = jnp.where(band, qk * (A_HEAD_DIM ** -0.5), NEG)
    m = jnp.maximum(jnp.max(s, axis=-1, keepdims=True), sink)
    p = jnp.exp(s - m)
    e_sink = jnp.exp(sink - m)
    den = jnp.sum(p, axis=-1, keepdims=True) + e_sink
    return p / den, e_sink / den


def _swa_specs():
    nq, nkv = A_HEADS * A_HEAD_DIM, A_KV_HEADS * A_HEAD_DIM
    kb, vb = nq // nkv, nq // nkv + 1
    prev = lambda n: jnp.maximum(n - 1, 0)
    return [pl.BlockSpec((BLOCK, nq), lambda n: (n, 0)),
            pl.BlockSpec((BLOCK, nkv), lambda n: (n, kb)), pl.BlockSpec((BLOCK, nkv), lambda n: (prev(n), kb)),
            pl.BlockSpec((BLOCK, nkv), lambda n: (n, vb)), pl.BlockSpec((BLOCK, nkv), lambda n: (prev(n), vb)),
            pl.BlockSpec(memory_space=pltpu.SMEM)]


def _swa_fwd(qkv, sinks, *, name, comm=None):
    S = qkv.shape[0]
    hd, grp = A_HEAD_DIM, A_HEADS // A_KV_HEADS

    def body(q_ref, kc_ref, kp_ref, vc_ref, vp_ref, sink_ref, o_ref):
        band = _swa_band(pl.program_id(0), grp)
        qa, kc, kp, vc, vp = q_ref[...], kc_ref[...], kp_ref[...], vc_ref[...], vp_ref[...]
        def products(hk):
            ks = slice(hk * hd, (hk + 1) * hd)
            k = jnp.concatenate([kp[:, ks], kc[:, ks]], axis=0)
            v = jnp.concatenate([vp[:, ks], vc[:, ks]], axis=0)
            q = jnp.concatenate([qa[:, h * hd:(h + 1) * hd] for h in range(hk * grp, (hk + 1) * grp)], axis=0)
            return v, lax.dot_general(q, k, C11, preferred_element_type=F32)

        nxt = products(0)
        for hk in range(A_KV_HEADS):
            v, s = nxt
            if hk + 1 < A_KV_HEADS:
                nxt = products(hk + 1)
            heads = [hk * grp + gi for gi in range(grp)]
            p, _ = _swa_softmax(s, band, _swa_sink_rows(sink_ref, heads))
            o = jnp.dot(p.astype(BF16), v, preferred_element_type=F32).astype(BF16)
            for gi, h in enumerate(heads):
                o_ref[:, h * hd:(h + 1) * hd] = o[gi * BLOCK:(gi + 1) * BLOCK]

    return _call(
        body, name=name, grid=(S // BLOCK,), in_specs=_swa_specs(),
        out_specs=[pl.BlockSpec((BLOCK, A_HEADS * hd), lambda n: (n, 0))], out_shape=[_sds((S, A_HEADS * hd), BF16)],
        params=_params("arbitrary"), comm=comm,
    )(qkv, qkv, qkv, qkv, qkv, sinks)[0]


def _swa_bwd(qkv, sinks, do, cos, sin, *, name, comm=None):
    S = qkv.shape[0]
    hd, grp = A_HEAD_DIM, A_HEADS // A_KV_HEADS
    nq, nkv = A_HEADS * hd, A_KV_HEADS * hd
    scale = hd ** -0.5

    def body(q_ref, kc_ref, kp_ref, vc_ref, vp_ref, sink_ref, do_ref, c_ref, s_ref, dq_ref, dc_ref, dp_ref, ds_ref, dq_s):
        @pl.when(pl.program_id(0) == 0)
        def _():
            ds_ref[...] = jnp.zeros_like(ds_ref)

        band = _swa_band(pl.program_id(0), grp)
        lane = lax.broadcasted_iota(jnp.int32, (1, LANES), 1)
        qa, kc, kp, vc, vp, doa = q_ref[...], kc_ref[...], kp_ref[...], vc_ref[...], vp_ref[...], do_ref[...]
        dsink = jnp.zeros((1, LANES), F32)
        def products(hk):
            ks = slice(hk * hd, (hk + 1) * hd)
            k = jnp.concatenate([kp[:, ks], kc[:, ks]], axis=0)
            v = jnp.concatenate([vp[:, ks], vc[:, ks]], axis=0)
            q = jnp.concatenate([qa[:, h * hd:(h + 1) * hd] for h in range(hk * grp, (hk + 1) * grp)], axis=0)
            dog = jnp.concatenate([doa[:, h * hd:(h + 1) * hd] for h in range(hk * grp, (hk + 1) * grp)], axis=0)
            return (k, q, dog, lax.dot_general(q, k, C11, preferred_element_type=F32),
                    lax.dot_general(dog, v, C11, preferred_element_type=F32))

        nxt = products(0)
        for hk in range(A_KV_HEADS):
            k, q, dog, s, dpr = nxt
            if hk + 1 < A_KV_HEADS:
                nxt = products(hk + 1)
            ks = slice(hk * hd, (hk + 1) * hd)
            heads = [hk * grp + gi for gi in range(grp)]
            p, p_sink = _swa_softmax(s, band, _swa_sink_rows(sink_ref, heads))
            delta = jnp.sum(p * dpr, axis=-1, keepdims=True)
            dsc = (p * (dpr - delta) * scale).astype(BF16)
            dqg = jnp.dot(dsc, k, preferred_element_type=F32)
            dk = lax.dot_general(dsc, q, C00, preferred_element_type=F32)
            dv = lax.dot_general(p.astype(BF16), dog, C00, preferred_element_type=F32)
            sink_term = p_sink * delta
            for gi, h in enumerate(heads):
                rows = slice(gi * BLOCK, (gi + 1) * BLOCK)
                dq_s[:, h * hd:(h + 1) * hd] = dqg[rows]
                dsink = dsink + jnp.where(lane == h, -jnp.sum(sink_term[rows], axis=0, keepdims=True), 0.0)
            dp_ref[:, ks] = dk[:BLOCK]
            dc_ref[:, ks] = dk[BLOCK:]
            dp_ref[:, nkv + hk * hd:nkv + (hk + 1) * hd] = dv[:BLOCK]
            dc_ref[:, nkv + hk * hd:nkv + (hk + 1) * hd] = dv[BLOCK:]
        ds_ref[...] += dsink
        dq_ref[...] = _rope_t(dq_s[...], c_ref[...], s_ref[...]).astype(BF16)

    tab = pl.BlockSpec((BLOCK, LANES), lambda n: (n, 0))
    blk = lambda w: pl.BlockSpec((BLOCK, w), lambda n: (n, 0))
    return _call(
        body, name=name, grid=(S // BLOCK,), in_specs=_swa_specs() + [blk(nq), tab, tab],
        out_specs=[blk(nq), blk(2 * nkv), blk(2 * nkv), pl.BlockSpec((1, LANES), lambda n: (0, 0))],
        out_shape=[_sds((S, nq), BF16), _sds((S, 2 * nkv), F32), _sds((S, 2 * nkv), F32), _sds((1, LANES), F32)],
        scratch_shapes=[pltpu.VMEM((BLOCK, nq), F32)],
        params=_params("arbitrary"), comm=comm,
    )(qkv, qkv, qkv, qkv, qkv, sinks, do, cos, sin)


def _swa_dqkv(dq, dcur, dprev, cos, sin, *, name):
    S, nq = dq.shape
    nkv = dcur.shape[1] // 2
    nb = S // BLOCK

    def body(dq_ref, dc_ref, dp_ref, c_ref, s_ref, o_ref):
        o_ref[:, :nq] = dq_ref[...]
        d = dc_ref[...] + jnp.where(pl.program_id(0) < nb - 1, dp_ref[...], 0.0)
        o_ref[:, nq:nq + nkv] = _rope_t(d[:, :nkv], c_ref[...], s_ref[...]).astype(BF16)
        o_ref[:, nq + nkv:] = d[:, nkv:].astype(BF16)

    tab = pl.BlockSpec((BLOCK, LANES), lambda m: (m, 0))
    blk = lambda w: pl.BlockSpec((BLOCK, w), lambda m: (m, 0))
    return pl.pallas_call(
        body, name=name, grid=(nb,),
        in_specs=[blk(nq), blk(2 * nkv), pl.BlockSpec((BLOCK, 2 * nkv), lambda m: (jnp.minimum(m + 1, nb - 1), 0)), tab, tab],
        out_specs=blk(nq + 2 * nkv), out_shape=_sds((S, nq + 2 * nkv), BF16), compiler_params=_params("parallel"),
    )(dq, dcur, dprev, cos, sin)


def _lru_gates(u, wri_ref, br, bi, sp):
    ub = u.astype(BF16)
    rs, igs = [], []
    for hb in range(LRU_BLOCKS):
        sl = slice(hb * LRU_BLOCK_W, (hb + 1) * LRU_BLOCK_W)
        ri = jnp.dot(ub[:, sl], wri_ref[hb], preferred_element_type=F32)
        rs.append(ri[:, :LRU_BLOCK_W])
        igs.append(ri[:, LRU_BLOCK_W:])
    r = _sigmoid(jnp.concatenate(rs, axis=1) + br)
    ig = _sigmoid(jnp.concatenate(igs, axis=1) + bi)
    la = -LRU_C * r * sp
    a = jnp.exp(la)
    sq = jnp.sqrt(_neg_expm1(2.0 * la))
    return r, ig, a, sq


def _lru_fwd(xw, cw, cb, wri, br, bi, lam, *, name, comm=None):
    S = xw.shape[0]
    W = D_MODEL
    tm = min(LRU_T, S)

    def body(gate_ref, up_ref, cw_ref, cb_ref, wri_ref, br_ref, bi_ref, lam_ref, y_ref, u_ref, h_ref, cu, ch, a_s, b_s):
        @pl.when(pl.program_id(0) == 0)
        def _():
            cu[...] = jnp.zeros_like(cu)
            ch[...] = jnp.zeros_like(ch)

        up = up_ref[...]
        prev8 = cu[...]
        u = up * cw_ref[3:4, :] + cb_ref[...]
        for d in range(1, LRU_CONV):
            u = u + _shift_down(up, prev8, d) * cw_ref[3 - d:4 - d, :]
        cu[...] = up[tm - 8:tm]
        u_ref[...] = u
        sp, _ = _softplus_neg(lam_ref[...])
        _, ig, a, sq = _lru_gates(u, wri_ref, br_ref[...], bi_ref[...], sp)
        a_s[...] = a
        b_s[...] = sq * (ig * u)
        rid = lax.broadcasted_iota(jnp.int32, (8, W), 0)

        def tile(t, h):
            r0 = pl.multiple_of(t * 8, 8)
            at, bt = a_s[pl.ds(r0, 8), :], b_s[pl.ds(r0, 8), :]
            out = jnp.zeros((8, W), F32)
            for j in range(8):
                h = at[j:j + 1, :] * h + bt[j:j + 1, :]
                out = jnp.where(rid == j, h, out)
            h_ref[pl.ds(r0, 8), :] = out
            return h

        ch[0:1, :] = lax.fori_loop(0, tm // 8, tile, ch[0:1, :])
        g, _ = _gelu_and_grad(gate_ref[...])
        y_ref[...] = (h_ref[...] * g).astype(BF16)

    row = pl.BlockSpec((tm, W), lambda i: (i, 0))
    full = lambda shape: pl.BlockSpec(shape, lambda i: (0,) * len(shape))
    return _call(
        body, name=name, grid=(S // tm,),
        in_specs=[row, pl.BlockSpec((tm, W), lambda i: (i, 1)), full((LRU_CONV, W)), full((1, W)),
                  full((LRU_BLOCKS, LRU_BLOCK_W, 2 * LRU_BLOCK_W)), full((1, W)), full((1, W)), full((1, W))],
        out_specs=[row, row, row], out_shape=[_sds((S, W), BF16), _sds((S, W), F32), _sds((S, W), F32)],
        scratch_shapes=[pltpu.VMEM((8, W), F32), pltpu.VMEM((8, W), F32), pltpu.VMEM((tm, W), F32), pltpu.VMEM((tm, W), F32)],
        params=_params("arbitrary"), comm=comm,
    )(xw, xw, cw, cb, wri, br, bi, lam)


def _lru_bwd(xw, u, h, dy, cw, wri, br, bi, lam, *, name):
    S = xw.shape[0]
    W = D_MODEL
    tm = min(LRU_T, S)
    nb = S // tm

    def body(gate_ref, up_ref, upp_ref, u_ref, h_ref, hp_ref, dy_ref, cw_ref, wri_ref, br_ref, bi_ref, lam_ref,
             dxw_ref, dcw_ref, dcb_ref, dwri_ref, dbr_ref, dbi_ref, dlam_ref, cg, cdu, a_s, d_s, g_s):
        i = pl.program_id(0)
        r_blk = nb - 1 - i

        @pl.when(i == 0)
        def _():
            cg[...] = jnp.zeros_like(cg)
            cdu[...] = jnp.zeros_like(cdu)
            for ref in (dcw_ref, dcb_ref, dwri_ref, dbr_ref, dbi_ref, dlam_ref):
                ref[...] = jnp.zeros_like(ref)

        u = u_ref[...]
        hv = h_ref[...]
        sp, dsp = _softplus_neg(lam_ref[...])
        r, ig, a, sq = _lru_gates(u, wri_ref, br_ref[...], bi_ref[...], sp)
        dy = dy_ref[...].astype(F32)
        g, dgelu = _gelu_and_grad(gate_ref[...])
        dxw_ref[:, :W] = (dy * hv * dgelu).astype(BF16)
        a_s[...] = a
        d_s[...] = dy * g
        rid = lax.broadcasted_iota(jnp.int32, (8, W), 0)

        def tile(t, c):
            r0 = pl.multiple_of((tm // 8 - 1 - t) * 8, 8)
            at, dt = a_s[pl.ds(r0, 8), :], d_s[pl.ds(r0, 8), :]
            out = jnp.zeros((8, W), F32)
            for j in range(7, -1, -1):
                gt = dt[j:j + 1, :] + c
                c = at[j:j + 1, :] * gt
                out = jnp.where(rid == j, gt, out)
            g_s[pl.ds(r0, 8), :] = out
            return c

        cg[0:1, :] = lax.fori_loop(0, tm // 8, tile, cg[0:1, :])
        gt = g_s[...]
        hprev8 = jnp.where(r_blk > 0, hp_ref[...], 0.0)
        da = gt * _shift_down(hv, hprev8, 1)
        iu = ig * u
        d_iu = gt * sq
        dla = da * a - (gt * iu) * (a * a) / sq
        dlam_ref[...] += jnp.sum(dla * r, axis=0, keepdims=True) * (-LRU_C) * dsp
        dr_pre = dla * (-LRU_C) * sp * r * (1.0 - r)
        di_pre = d_iu * u * ig * (1.0 - ig)
        dbr_ref[...] += jnp.sum(dr_pre, axis=0, keepdims=True)
        dbi_ref[...] += jnp.sum(di_pre, axis=0, keepdims=True)
        ub = u.astype(BF16)
        dus = []
        for hb in range(LRU_BLOCKS):
            sl = slice(hb * LRU_BLOCK_W, (hb + 1) * LRU_BLOCK_W)
            dri = jnp.concatenate([dr_pre[:, sl], di_pre[:, sl]], axis=1).astype(BF16)
            dus.append(lax.dot_general(dri, wri_ref[hb], C11, preferred_element_type=F32))
            dwri_ref[hb] += lax.dot_general(ub[:, sl], dri, C00, preferred_element_type=F32)
        du = d_iu * ig + jnp.concatenate(dus, axis=1)
        dcb_ref[...] += jnp.sum(du, axis=0, keepdims=True)
        up = up_ref[...]
        upprev8 = jnp.where(r_blk > 0, upp_ref[...], 0.0)
        dcw_ref[3:4, :] += jnp.sum(du * up, axis=0, keepdims=True)
        for d in range(1, LRU_CONV):
            dcw_ref[3 - d:4 - d, :] += jnp.sum(du * _shift_down(up, upprev8, d), axis=0, keepdims=True)
        next8 = cdu[...]
        dup = du * cw_ref[3:4, :]
        for d in range(1, LRU_CONV):
            dup = dup + _shift_up(du, next8, d) * cw_ref[3 - d:4 - d, :]
        cdu[...] = du[0:8]
        dxw_ref[:, W:] = dup.astype(BF16)

    rev = lambda c: (lambda i: (nb - 1 - i, c))
    halo = lambda c: (lambda i: (jnp.maximum((nb - 1 - i) * (tm // 8) - 1, 0), c))
    full = lambda shape: pl.BlockSpec(shape, lambda i: (0,) * len(shape))
    vec = full((1, W))
    return pl.pallas_call(
        body, name=name, grid=(nb,),
        in_specs=[pl.BlockSpec((tm, W), rev(0)), pl.BlockSpec((tm, W), rev(1)), pl.BlockSpec((8, W), halo(1)),
                  pl.BlockSpec((tm, W), rev(0)), pl.BlockSpec((tm, W), rev(0)), pl.BlockSpec((8, W), halo(0)),
                  pl.BlockSpec((tm, W), rev(0)), full((LRU_CONV, W)), full((LRU_BLOCKS, LRU_BLOCK_W, 2 * LRU_BLOCK_W)),
                  vec, vec, vec],
        out_specs=[pl.BlockSpec((tm, 2 * W), rev(0)), full((LRU_CONV, W)), vec,
                   full((LRU_BLOCKS, LRU_BLOCK_W, 2 * LRU_BLOCK_W)), vec, vec, vec],
        out_shape=[_sds((S, 2 * W), BF16), _sds((LRU_CONV, W), F32), _sds((1, W), F32),
                   _sds((LRU_BLOCKS, LRU_BLOCK_W, 2 * LRU_BLOCK_W), F32), _sds((1, W), F32), _sds((1, W), F32),
                   _sds((1, W), F32)],
        scratch_shapes=[pltpu.VMEM((8, W), F32), pltpu.VMEM((8, W), F32), pltpu.VMEM((tm, W), F32),
                        pltpu.VMEM((tm, W), F32), pltpu.VMEM((tm, W), F32)],
        compiler_params=_params("arbitrary"),
    )(xw, xw, xw, u, h, h, dy, cw, wri, br, bi, lam)


def _rms(x, g):
    r = lax.rsqrt(jnp.mean(x * x, axis=-1, keepdims=True) + RMS_EPS)
    return x * r * g, r


def _mla_pre(c, qg, kvg, cos, sin, *, name):
    S = c.shape[0]
    tm = min(ROW_T, S)
    q0, k0 = C_Q_RANK, C_Q_RANK + C_KV_RANK

    def body(c_ref, qg_ref, kvg_ref, cs_ref, sn_ref, cq_ref, ckv_ref, kr_ref):
        cq_ref[...] = _rms(c_ref[:, :q0], qg_ref[...])[0].astype(BF16)
        ckv_ref[...] = _rms(c_ref[:, q0:k0], kvg_ref[...])[0].astype(BF16)
        kr_ref[...] = _rope(c_ref[:, k0:], cs_ref[...], sn_ref[...]).astype(BF16)

    blk = lambda w: pl.BlockSpec((tm, w), lambda i: (i, 0))
    vec = lambda w: pl.BlockSpec((1, w), lambda i: (0, 0))
    return pl.pallas_call(
        body, name=name, grid=(S // tm,),
        in_specs=[blk(c.shape[1]), vec(C_Q_RANK), vec(C_KV_RANK), blk(LANES), blk(LANES)],
        out_specs=[blk(C_Q_RANK), blk(C_KV_RANK), blk(C_ROPE)],
        out_shape=[_sds((S, C_Q_RANK), BF16), _sds((S, C_KV_RANK), BF16), _sds((S, C_ROPE), BF16)],
        compiler_params=_params("parallel"),
    )(c, qg, kvg, cos, sin)


def _mla_post_bwd(c, dcq_a, dcq_b, dckv, dkr_h, qg, kvg, cos, sin, *, name):
    S = c.shape[0]
    tm = min(ROW_T, S)
    q0, k0 = C_Q_RANK, C_Q_RANK + C_KV_RANK

    def rms_bwd(x, g, dy):
        r = lax.rsqrt(jnp.mean(x * x, axis=-1, keepdims=True) + RMS_EPS)
        uu = dy * g
        dx = r * uu - x * (r * r * r) * jnp.mean(uu * x, axis=-1, keepdims=True)
        return dx, jnp.sum(dy * x * r, axis=0, keepdims=True)

    def body(c_ref, da_ref, db_ref, dkv_ref, dkr_ref, qg_ref, kvg_ref, cs_ref, sn_ref, dc_ref, dqg_ref, dkvg_ref):
        @pl.when(pl.program_id(0) == 0)
        def _():
            dqg_ref[...] = jnp.zeros_like(dqg_ref)
            dkvg_ref[...] = jnp.zeros_like(dkvg_ref)

        dx, dg = rms_bwd(c_ref[:, :q0], qg_ref[...], da_ref[...] + db_ref[...])
        dc_ref[:, :q0] = dx.astype(BF16)
        dqg_ref[...] += dg
        dx, dg = rms_bwd(c_ref[:, q0:k0], kvg_ref[...], dkv_ref[...])
        dc_ref[:, q0:k0] = dx.astype(BF16)
        dkvg_ref[...] += dg
        dkr = dkr_ref[0]
        for hh in range(1, C_HEADS):
            dkr = dkr + dkr_ref[hh]
        dc_ref[:, k0:] = _rope_t(dkr, cs_ref[...], sn_ref[...]).astype(BF16)

    blk = lambda w: pl.BlockSpec((tm, w), lambda i: (i, 0))
    vec = lambda w: pl.BlockSpec((1, w), lambda i: (0, 0))
    return pl.pallas_call(
        body, name=name, grid=(S // tm,),
        in_specs=[blk(c.shape[1]), blk(C_Q_RANK), blk(C_Q_RANK), blk(C_KV_RANK),
                  pl.BlockSpec((C_HEADS, tm, C_ROPE), lambda i: (0, i, 0)), vec(C_Q_RANK), vec(C_KV_RANK), blk(LANES), blk(LANES)],
        out_specs=[blk(c.shape[1]), vec(C_Q_RANK), vec(C_KV_RANK)],
        out_shape=[_sds(c.shape, BF16), _sds((1, C_Q_RANK), F32), _sds((1, C_KV_RANK), F32)],
        compiler_params=_params("arbitrary"),
    )(c, dcq_a, dcq_b, dckv, dkr_h, qg, kvg, cos, sin)


def _rope_heads(x, cos, sin, *, transpose, name):
    S, W = x.shape
    tm = min(ROW_T, S)
    fn = _rope_t if transpose else _rope

    def body(x_ref, c_ref, s_ref, o_ref):
        o_ref[...] = fn(x_ref[...].astype(F32), c_ref[...], s_ref[...]).astype(BF16)

    row = pl.BlockSpec((tm, W), lambda i: (i, 0))
    tab = pl.BlockSpec((tm, LANES), lambda i: (i, 0))
    return pl.pallas_call(body, name=name, grid=(S // tm,), in_specs=[row, tab, tab], out_specs=row,
                          out_shape=_sds((S, W), BF16), compiler_params=_params("parallel"))(x, cos, sin)


MLA_GROUP = 4
MLA_SCALE = (C_NOPE + C_ROPE) ** -0.5
LOG2E = 1.4426950408889634


def _mla_cat(nope, rope):
    return jnp.concatenate([nope, rope], axis=1)


def _mla_scores2(qn, qr, kn, kr, diagonal):
    s = lax.dot_general(_mla_cat(qn, qr), _mla_cat(kn, kr), C11, preferred_element_type=F32)
    s = s * (MLA_SCALE * LOG2E)
    if diagonal:
        row = lax.broadcasted_iota(jnp.int32, s.shape, 0)
        col = lax.broadcasted_iota(jnp.int32, s.shape, 1)
        s = jnp.where(col <= row, s, NEG)
    return s


def _causal_pairs(n, query_major):
    if query_major:
        pairs = [(i, j) for i in range(n) for j in range(i + 1)]
    else:
        pairs = [(i, j) for j in range(n) for i in range(j, n)]
    return jnp.asarray([p[0] for p in pairs], jnp.int32), jnp.asarray([p[1] for p in pairs], jnp.int32)


def _mla_flash_fwd(qn, qr, kv, kr, *, name):
    S = qn.shape[0]
    H, G, t = C_HEADS, MLA_GROUP, min(FLASH_T, S)
    qi, kj = _causal_pairs(S // t, True)

    def body(qi_ref, kj_ref, qn_ref, qr_ref, kn_ref, v_ref, kr_ref, o_ref, lse_ref, *scr):
        m_s, l_s, acc = scr[:G], scr[G:2 * G], scr[2 * G:]
        p_id = pl.program_id(1)
        i, j = qi_ref[p_id], kj_ref[p_id]
        sls = [slice(hh * LANES, (hh + 1) * LANES) for hh in range(G)]

        @pl.when(j == 0)
        def _():
            for hh in range(G):
                m_s[hh][...] = jnp.full_like(m_s[hh], NEG)
                l_s[hh][...] = jnp.zeros_like(l_s[hh])
                acc[hh][...] = jnp.zeros_like(acc[hh])

        def step(diagonal):
            scores = lambda hh: _mla_scores2(qn_ref[:, sls[hh]], qr_ref[hh], kn_ref[:, sls[hh]], kr_ref[...], diagonal)
            s_next = scores(0)
            for hh in range(G):
                s = s_next
                if hh + 1 < G:
                    s_next = scores(hh + 1)
                m_prev = m_s[hh][...]
                m_new = jnp.maximum(m_prev, jnp.max(s, axis=-1, keepdims=True))
                corr = jnp.exp2(m_prev - m_new)
                pb = jnp.exp2(s - m_new[:, 0:1]).astype(BF16)
                l_s[hh][...] = corr * l_s[hh][...] + jnp.dot(pb, jnp.ones((pb.shape[1], LANES), BF16), preferred_element_type=F32)
                acc[hh][...] = corr * acc[hh][...] + jnp.dot(pb, v_ref[:, sls[hh]], preferred_element_type=F32)
                m_s[hh][...] = m_new

        @pl.when(j < i)
        def _():
            step(False)

        @pl.when(j == i)
        def _():
            step(True)
            for hh in range(G):
                o_ref[:, sls[hh]] = (acc[hh][...] / l_s[hh][...]).astype(BF16)
                lse_ref[:, sls[hh]] = m_s[hh][...] + jnp.log2(l_s[hh][...])

    wide = lambda which, off: pl.BlockSpec((t, G * LANES), lambda h, p, qi, kj: ((qi if which == "q" else kj)[p], off + h))
    return pl.pallas_call(
        body, name=name,
        grid_spec=pltpu.PrefetchScalarGridSpec(
            num_scalar_prefetch=2, grid=(H // G, qi.shape[0]),
            in_specs=[wide("q", 0), pl.BlockSpec((G, t, C_ROPE), lambda h, p, qi, kj: (h, qi[p], 0)),
                      wide("k", 0), wide("k", H // G), pl.BlockSpec((t, C_ROPE), lambda h, p, qi, kj: (kj[p], 0))],
            out_specs=[wide("q", 0), wide("q", 0)],
            scratch_shapes=[pltpu.VMEM((t, LANES), F32)] * (3 * G)),
        out_shape=[_sds((S, H * C_V), BF16), _sds((S, H * LANES), F32)],
        compiler_params=_params("parallel", "arbitrary"),
    )(qi, kj, qn, qr, kv, kv, kr)


def _mla_delta(do, o, *, name):
    S, W = do.shape
    tm = min(ROW_T, S)

    def body(do_ref, o_ref, d_ref):
        for h in range(C_HEADS):
            sl = slice(h * C_V, (h + 1) * C_V)
            d = jnp.sum(do_ref[:, sl].astype(F32) * o_ref[:, sl].astype(F32), axis=-1, keepdims=True)
            d_ref[:, sl] = jnp.broadcast_to(d, (tm, C_V))

    row = pl.BlockSpec((tm, W), lambda i: (i, 0))
    return pl.pallas_call(body, name=name, grid=(S // tm,), in_specs=[row, row], out_specs=row,
                          out_shape=_sds((S, W), F32), compiler_params=_params("parallel"))(do, o)


def _mla_flash_dq(qn, qr, kv, kr, do, lse, delta, *, name):
    S = qn.shape[0]
    H, G, t = C_HEADS, MLA_GROUP, min(FLASH_T, S)
    qi, kj = _causal_pairs(S // t, True)

    def body(qi_ref, kj_ref, qn_ref, qr_ref, kn_ref, v_ref, kr_ref, do_ref, lse_ref, dl_ref, dqn_ref, dqr_ref, acc):
        p_id = pl.program_id(1)
        i, j = qi_ref[p_id], kj_ref[p_id]
        sls = [slice(hh * LANES, (hh + 1) * LANES) for hh in range(G)]

        @pl.when(j == 0)
        def _():
            acc[...] = jnp.zeros_like(acc)

        def step(diagonal):
            def products(hh):
                s = _mla_scores2(qn_ref[:, sls[hh]], qr_ref[hh], kn_ref[:, sls[hh]], kr_ref[...], diagonal)
                return s, lax.dot_general(do_ref[:, sls[hh]], v_ref[:, sls[hh]], C11, preferred_element_type=F32)

            nxt = products(0)
            for hh in range(G):
                s, dp = nxt
                if hh + 1 < G:
                    nxt = products(hh + 1)
                p = jnp.exp2(s - lse_ref[:, hh * LANES:hh * LANES + 1])
                ds = (p * (dp - dl_ref[:, hh * LANES:hh * LANES + 1])).astype(BF16)
                acc[hh] += jnp.dot(ds, _mla_cat(kn_ref[:, sls[hh]], kr_ref[...]), preferred_element_type=F32)

        @pl.when(j < i)
        def _():
            step(False)

        @pl.when(j == i)
        def _():
            step(True)
            for hh in range(G):
                dqn_ref[:, sls[hh]] = (acc[hh, :, :C_NOPE] * MLA_SCALE).astype(BF16)
                dqr_ref[hh] = acc[hh, :, C_NOPE:] * MLA_SCALE

    wide = lambda which, off: pl.BlockSpec((t, G * LANES), lambda h, p, qi, kj: ((qi if which == "q" else kj)[p], off + h))
    qrb = pl.BlockSpec((G, t, C_ROPE), lambda h, p, qi, kj: (h, qi[p], 0))
    return pl.pallas_call(
        body, name=name,
        grid_spec=pltpu.PrefetchScalarGridSpec(
            num_scalar_prefetch=2, grid=(H // G, qi.shape[0]),
            in_specs=[wide("q", 0), qrb, wide("k", 0), wide("k", H // G),
                      pl.BlockSpec((t, C_ROPE), lambda h, p, qi, kj: (kj[p], 0)), wide("q", 0), wide("q", 0), wide("q", 0)],
            out_specs=[wide("q", 0), qrb],
            scratch_shapes=[pltpu.VMEM((G, t, C_NOPE + C_ROPE), F32)]),
        out_shape=[_sds((S, H * C_NOPE), BF16), _sds((H, S, C_ROPE), F32)],
        compiler_params=_params("parallel", "arbitrary"),
    )(qi, kj, qn, qr, kv, kv, kr, do, lse, delta)


def _mla_flash_dkv(qn, qr, kv, kr, do, lse, delta, *, name):
    S = qn.shape[0]
    H, G, t = C_HEADS, MLA_GROUP, min(FLASH_T, S)
    n = S // t
    qi, kj = _causal_pairs(n, False)

    def body(qi_ref, kj_ref, qn_ref, qr_ref, kn_ref, v_ref, kr_ref, do_ref, lse_ref, dl_ref, dkn_ref, dv_ref, dkr_ref, ak, av):
        p_id = pl.program_id(1)
        i, j = qi_ref[p_id], kj_ref[p_id]
        sls = [slice(hh * LANES, (hh + 1) * LANES) for hh in range(G)]

        def step(diagonal):
            def products(hh):
                qc = _mla_cat(qn_ref[:, sls[hh]], qr_ref[hh])
                st = lax.dot_general(_mla_cat(kn_ref[:, sls[hh]], kr_ref[...]), qc, C11, preferred_element_type=F32)
                st = st * (MLA_SCALE * LOG2E)
                if diagonal:
                    key = lax.broadcasted_iota(jnp.int32, st.shape, 0)
                    qry = lax.broadcasted_iota(jnp.int32, st.shape, 1)
                    st = jnp.where(key <= qry, st, NEG)
                return qc, st, lax.dot_general(v_ref[:, sls[hh]], do_ref[:, sls[hh]], C11, preferred_element_type=F32)

            nxt = products(0)
            for hh in range(G):
                qc, st, dpt = nxt
                if hh + 1 < G:
                    nxt = products(hh + 1)
                sl = sls[hh]
                lse_row = jnp.transpose(lse_ref[:, sl])[0:1, :]
                dl_row = jnp.transpose(dl_ref[:, sl])[0:1, :]
                pt = jnp.exp2(st - lse_row)
                dst = (pt * (dpt - dl_row)).astype(BF16)
                av[:, sl] += jnp.dot(pt.astype(BF16), do_ref[:, sl], preferred_element_type=F32)
                ak[hh] += jnp.dot(dst, qc, preferred_element_type=F32)

        @pl.when(i == j)
        def _():
            ak[...] = jnp.zeros_like(ak)
            av[...] = jnp.zeros_like(av)
            step(True)

        @pl.when(i > j)
        def _():
            step(False)

        @pl.when(i == n - 1)
        def _():
            dv_ref[...] = av[...].astype(BF16)
            for hh in range(G):
                dkn_ref[:, sls[hh]] = (ak[hh, :, :C_NOPE] * MLA_SCALE).astype(BF16)
                dkr_ref[hh] = ak[hh, :, C_NOPE:] * MLA_SCALE

    wide = lambda which, off: pl.BlockSpec((t, G * LANES), lambda h, p, qi, kj: ((qi if which == "q" else kj)[p], off + h))
    krb = pl.BlockSpec((G, t, C_ROPE), lambda h, p, qi, kj: (h, kj[p], 0))
    return pl.pallas_call(
        body, name=name,
        grid_spec=pltpu.PrefetchScalarGridSpec(
            num_scalar_prefetch=2, grid=(H // G, qi.shape[0]),
            in_specs=[wide("q", 0), pl.BlockSpec((G, t, C_ROPE), lambda h, p, qi, kj: (h, qi[p], 0)), wide("k", 0),
                      wide("k", H // G), pl.BlockSpec((t, C_ROPE), lambda h, p, qi, kj: (kj[p], 0)),
                      wide("q", 0), wide("q", 0), wide("q", 0)],
            out_specs=[wide("k", 0), wide("k", 0), krb],
            scratch_shapes=[pltpu.VMEM((G, t, C_NOPE + C_ROPE), F32), pltpu.VMEM((t, G * LANES), F32)]),
        out_shape=[_sds((S, H * C_NOPE), BF16), _sds((S, H * C_V), BF16), _sds((H, S, C_ROPE), F32)],
        compiler_params=_params("parallel", "arbitrary"),
    )(qi, kj, qn, qr, kv, kv, kr, do, lse, delta)


def _place():
    return lax.axis_index("x"), lax.axis_index("y"), lax.axis_index("c")


def _other_chips(x, y):
    return [(1 - x, y), (x, 1 - y), (1 - x, 1 - y)]


def _all_gather_chips(p, *, name):
    R, C = p.shape

    def body(p_ref, o_ref, send_sems, recv_sems, local_sem):
        x, y, c = _place()
        me = 2 * x + y
        local = pltpu.make_async_copy(p_ref, o_ref.at[me], local_sem)
        local.start()
        copies = [pltpu.make_async_remote_copy(src_ref=p_ref, dst_ref=o_ref.at[me], send_sem=send_sems.at[k],
                                               recv_sem=recv_sems.at[k], device_id=(px, py, c), device_id_type=MESH)
                  for k, (px, py) in enumerate(_other_chips(x, y))]
        for cp in copies:
            cp.start()
        for cp in copies:
            cp.wait()
        local.wait()

    any_spec = pl.BlockSpec(memory_space=pl.ANY)
    return pl.pallas_call(
        body, name=name, in_specs=[any_spec], out_specs=any_spec, out_shape=_sds((4, R, C), p.dtype),
        scratch_shapes=[pltpu.SemaphoreType.DMA((3,)), pltpu.SemaphoreType.DMA((3,)), pltpu.SemaphoreType.DMA(())],
    )(p)


def _shard_of(ref, axis, pos, size):
    idx = [slice(None)] * len(ref.shape)
    idx[axis] = pl.ds(pos * size, size)
    return ref.at[tuple(idx)]


def _shard_pos(chip, swapped):
    return (chip % 2) * 2 + chip // 2 if swapped else chip


class _Comm:
    def __init__(self, inputs, out_shapes, sems, start, finish, deliver):
        self.inputs, self.out_shapes, self.sems = list(inputs), list(out_shapes), list(sems)
        self.start, self.finish, self.deliver = start, finish, deliver


def _call(body, *, name, grid, in_specs, out_specs, out_shape, scratch_shapes=(), params, comm=None):
    in_specs, out_specs, out_shape, scratch_shapes = list(in_specs), list(out_specs), list(out_shape), list(scratch_shapes)
    if comm is None:
        return pl.pallas_call(body, name=name, grid=grid, in_specs=in_specs, out_specs=out_specs, out_shape=out_shape,
                              scratch_shapes=scratch_shapes, compiler_params=params)
    n_in, n_out, n_scr = len(in_specs), len(out_specs), len(scratch_shapes)
    c_in, c_out = len(comm.inputs), len(comm.out_shapes)

    def hosted(*refs):
        a, rest = refs[:n_in], refs[n_in:]
        cin, rest = rest[:c_in], rest[c_in:]
        o, rest = rest[:n_out], rest[n_out:]
        cout, rest = rest[:c_out], rest[c_out:]
        scr, sems = rest[:n_scr], rest[n_scr:]
        first = functools.reduce(jnp.logical_and, [pl.program_id(d) == 0 for d in range(len(grid))])
        last = functools.reduce(jnp.logical_and, [pl.program_id(d) == grid[d] - 1 for d in range(len(grid))])

        @pl.when(first)
        def _():
            comm.start(cin, cout, sems)

        body(*a, *o, *scr)

        @pl.when(last)
        def _():
            comm.finish(cin, cout, sems)

    any_spec = pl.BlockSpec(memory_space=pl.ANY)
    call = pl.pallas_call(
        hosted, name=name, grid=grid, in_specs=in_specs + [any_spec] * c_in, out_specs=out_specs + [any_spec] * c_out,
        out_shape=out_shape + comm.out_shapes, scratch_shapes=scratch_shapes + comm.sems, compiler_params=params)

    def run(*args):
        outs = call(*args, *comm.inputs)
        comm.deliver(outs[n_out:])
        return outs[:n_out]

    return run


def _run_comm(comm, *, name):
    c_in, c_out = len(comm.inputs), len(comm.out_shapes)

    def body(*refs):
        cin, cout, sems = refs[:c_in], refs[c_in:c_in + c_out], refs[c_in + c_out:]
        comm.start(cin, cout, sems)
        comm.finish(cin, cout, sems)

    any_spec = pl.BlockSpec(memory_space=pl.ANY)
    outs = pl.pallas_call(body, name=name, in_specs=[any_spec] * c_in, out_specs=[any_spec] * c_out,
                          out_shape=comm.out_shapes, scratch_shapes=comm.sems)(*comm.inputs)
    comm.deliver(outs)


def _gather_comm(items, deliver):
    n = len(items)
    shard_shapes = [a.shape if j is None else a.shape[1:] for a, j, _, _ in items]
    axes = [ax for _, _, ax, _ in items]
    swapped = [sw for _, _, _, sw in items]
    sizes = [s[a] for s, a in zip(shard_shapes, axes)]
    halves = [s[-2] // 2 for s in shard_shapes]
    full = [tuple(4 * d if i == a else d for i, d in enumerate(s)) for s, a in zip(shard_shapes, axes)]

    def mine(ins, k):
        j = items[k][1]
        return ins[k] if j is None else ins[k].at[j]

    def half_of(ref, k, half, chip=None):
        nd = len(ref.shape)
        split = nd - 2
        idx = [slice(None)] * nd
        start = half * halves[k]
        if chip is not None:
            pos = _shard_pos(chip, swapped[k]) * sizes[k]
            if axes[k] == split:
                start = start + pos
            else:
                idx[axes[k]] = pl.ds(pos, sizes[k])
        idx[split] = pl.ds(start, halves[k])
        return ref.at[tuple(idx)]

    def local_copy(ins, outs, sems, k, me):
        return pltpu.make_async_copy(mine(ins, k), _shard_of(outs[k], axes[k], _shard_pos(me, swapped[k]), sizes[k]), sems[4].at[k])

    def ici_copy(ins, outs, sems, k, j, peer, c, landing_chip):
        return pltpu.make_async_remote_copy(
            src_ref=half_of(mine(ins, k), k, c), dst_ref=half_of(outs[k], k, c, chip=landing_chip), send_sem=sems[0].at[3 * k + j],
            recv_sem=sems[1].at[3 * k + j], device_id=(peer[0], peer[1], c), device_id_type=MESH)

    def pass_copy(outs, sems, k, j, half, chip, sibling):
        region = half_of(outs[k], k, half, chip=chip)
        return pltpu.make_async_remote_copy(src_ref=region, dst_ref=region, send_sem=sems[2].at[3 * k + j],
                                            recv_sem=sems[3].at[3 * k + j], device_id=sibling, device_id_type=MESH)

    def start(ins, outs, sems):
        x, y, c = _place()
        me = 2 * x + y
        for k in range(n):
            local_copy(ins, outs, sems, k, me).start()
            for j, peer in enumerate(_other_chips(x, y)):
                ici_copy(ins, outs, sems, k, j, peer, c, me).start()

    def finish(ins, outs, sems):
        x, y, c = _place()
        me = 2 * x + y
        chips = _other_chips(x, y)
        sibling = (x, y, 1 - c)
        for k in range(n):
            for j, peer in enumerate(chips):
                ici_copy(ins, outs, sems, k, j, peer, c, 2 * peer[0] + peer[1]).wait_recv()
                pass_copy(outs, sems, k, j, c, 2 * peer[0] + peer[1], sibling).start()
        for k in range(n):
            for j, peer in enumerate(chips):
                pass_copy(outs, sems, k, j, 1 - c, 2 * peer[0] + peer[1], sibling).wait_recv()
        for k in range(n):
            local_copy(ins, outs, sems, k, me).wait()
            for j, peer in enumerate(chips):
                ici_copy(ins, outs, sems, k, j, peer, c, me).wait_send()
                pass_copy(outs, sems, k, j, c, 2 * peer[0] + peer[1], sibling).wait_send()

    return _Comm([a for a, _, _, _ in items], [_sds(f, a.dtype) for f, (a, _, _, _) in zip(full, items)],
                 [pltpu.SemaphoreType.DMA((3 * n,))] * 4 + [pltpu.SemaphoreType.DMA((n,))], start, finish, deliver)


def _scatter_comm(items, deliver):
    n = len(items)
    axes = [ax for _, ax, _ in items]
    swapped = [sw for _, _, sw in items]
    sizes = [g.shape[a] // 4 for g, a, _ in items]
    shard = [tuple(d // 4 if i == a else d for i, d in enumerate(g.shape)) for g, a, _ in items]

    def copies(ins, outs, sems):
        x, y, c = _place()
        me = 2 * x + y
        out = []
        for k in range(n):
            own = _shard_of(ins[k], axes[k], _shard_pos(me, swapped[k]), sizes[k])
            out.append(pltpu.make_async_copy(own, outs[k].at[3], sems[2].at[k]))
            for j, (px, py) in enumerate(_other_chips(x, y)):
                src = _shard_of(ins[k], axes[k], _shard_pos(2 * px + py, swapped[k]), sizes[k])
                out.append(pltpu.make_async_remote_copy(src_ref=src, dst_ref=outs[k].at[j], send_sem=sems[0].at[3 * k + j],
                                                        recv_sem=sems[1].at[3 * k + j], device_id=(px, py, c), device_id_type=MESH))
        return out

    def start(ins, outs, sems):
        for cp in copies(ins, outs, sems):
            cp.start()

    def finish(ins, outs, sems):
        for cp in copies(ins, outs, sems):
            cp.wait()

    return _Comm([g for g, _, _ in items], [_sds((4,) + s, g.dtype) for s, (g, _, _) in zip(shard, items)],
                 [pltpu.SemaphoreType.DMA((3 * n,)), pltpu.SemaphoreType.DMA((3 * n,)), pltpu.SemaphoreType.DMA((n,))],
                 start, finish, deliver)


def _row_tile(rows, cols, budget=2 ** 20):
    best = None
    for t in range(8, rows + 1, 8):
        if rows % t == 0 and t * cols * 4 <= budget:
            best = t
    return best or rows


def _sum_partials(recv, into, layer, layers, *, name):
    _, R, C = recv.shape
    tr = _row_tile(R, C)
    nt = R // tr

    def body(own_ref, r0_ref, r1_ref, r2_ref, *rest):
        f = lambda ref: ref[...].astype(F32)
        rest[-1][...] = ((f(own_ref) + f(r0_ref)) + f(r1_ref)) + f(r2_ref)

    rspec = lambda k: pl.BlockSpec((None, tr, C), lambda i: (k, i, 0))
    extra = [] if into is None else [pl.BlockSpec(memory_space=pl.ANY)]
    return pl.pallas_call(
        body, name=name, grid=(nt,), in_specs=[rspec(3), rspec(0), rspec(1), rspec(2)] + extra,
        out_specs=pl.BlockSpec((tr, C), lambda i: (layer * nt + i, 0)), out_shape=_sds((layers * R, C), F32),
        input_output_aliases={} if into is None else {4: 0}, compiler_params=_params("parallel"),
    )(recv, recv, recv, recv, *([] if into is None else [into]))


def _swap_cores(parts, *, name):
    n = len(parts)

    def body(*refs):
        ins, outs = refs[:n], refs[n:2 * n]
        send_sems, recv_sems = refs[2 * n:]
        x, y, c = _place()
        copies = [pltpu.make_async_remote_copy(src_ref=ins[k], dst_ref=outs[k], send_sem=send_sems.at[k], recv_sem=recv_sems.at[k],
                                               device_id=(x, y, 1 - c), device_id_type=MESH) for k in range(n)]
        for cp in copies:
            cp.start()
        for cp in copies:
            cp.wait()

    any_spec = pl.BlockSpec(memory_space=pl.ANY)
    return pl.pallas_call(
        body, name=name, in_specs=[any_spec] * n, out_specs=[any_spec] * n, out_shape=[_sds(p.shape, p.dtype) for p in parts],
        scratch_shapes=[pltpu.SemaphoreType.DMA((n,)), pltpu.SemaphoreType.DMA((n,))],
    )(*parts)


def _all_reduce_small(v, *, name):
    r, C = v.shape

    def body(v_ref, o_ref, buf, send_sems, recv_sems):
        x, y, c = _place()
        me = 4 * x + 2 * y + c
        buf[me] = v_ref[...]
        peers = []
        for k in range(1, 8):
            kx, ky, kc = (k >> 2) & 1, (k >> 1) & 1, k & 1
            px = 1 - x if kx else x
            py = 1 - y if ky else y
            pc = 1 - c if kc else c
            peers.append((px, py, pc))
        copies = []
        for k, peer in enumerate(peers):
            cp = pltpu.make_async_remote_copy(src_ref=v_ref, dst_ref=buf.at[me], send_sem=send_sems.at[k],
                                              recv_sem=recv_sems.at[me], device_id=peer, device_id_type=MESH)
            cp.start()
            copies.append(cp)
        for k, (px, py, pc) in enumerate(peers):
            src = 4 * px + 2 * py + pc
            pltpu.make_async_remote_copy(src_ref=v_ref, dst_ref=buf.at[src], send_sem=send_sems.at[k],
                                         recv_sem=recv_sems.at[src], device_id=peers[k], device_id_type=MESH).wait_recv()
        for cp in copies:
            cp.wait_send()
        acc = buf[0]
        for d in range(1, 8):
            acc = acc + buf[d]
        o_ref[...] = acc

    vm = pl.BlockSpec(memory_space=pltpu.VMEM)
    return pl.pallas_call(
        body, name=name, in_specs=[vm], out_specs=vm, out_shape=_sds((r, C), F32),
        scratch_shapes=[pltpu.VMEM((8, r, C), F32), pltpu.SemaphoreType.DMA((7,)), pltpu.SemaphoreType.DMA((8,))],
    )(v)


def _adamw(w, m, v, ga, gb, *, name):
    R, C = w.shape
    tr = _row_tile(R, C)
    has_b = gb is not None
    c1 = 1.0 / (1.0 - ADAM_B1 ** ADAM_STEP)
    c2 = 1.0 / (1.0 - ADAM_B2 ** ADAM_STEP)

    def body(*refs):
        if has_b:
            w_ref, m_ref, v_ref, ga_ref, gb_ref, g_ref, d_ref, nm_ref, nv_ref = refs
            g = ga_ref[...] + gb_ref[...]
        else:
            w_ref, m_ref, v_ref, ga_ref, g_ref, d_ref, nm_ref, nv_ref = refs
            g = ga_ref[...]
        nm = ADAM_B1 * m_ref[...] + (1.0 - ADAM_B1) * g
        nv = ADAM_B2 * v_ref[...] + (1.0 - ADAM_B2) * (g * g)
        g_ref[...] = g
        nm_ref[...] = nm
        nv_ref[...] = nv
        d_ref[...] = -ADAM_LR * ((nm * c1) / (jnp.sqrt(nv * c2) + ADAM_EPS) + ADAM_WD * w_ref[...])

    blk = pl.BlockSpec((tr, C), lambda i: (i, 0))
    n_in = 5 if has_b else 4
    args = (w, m, v, ga) + ((gb,) if has_b else ())
    return pl.pallas_call(body, name=name, grid=(R // tr,), in_specs=[blk] * n_in, out_specs=[blk] * 4,
                          out_shape=[_sds((R, C), F32)] * 4, compiler_params=_params("parallel"))(*args)


def _seg_rows(n, cols):
    return -(-n // (16 * cols)) * 16


def _pack(arrays, dtype, cols=PACK_COLS, row_mult=512):
    parts, rows = [], 0
    for a in arrays:
        n = int(np.prod(a.shape))
        r = _seg_rows(n, cols)
        flat = a.reshape(-1).astype(dtype)
        if r * cols != n:
            flat = jnp.pad(flat, (0, r * cols - n))
        parts.append(flat.reshape(r, cols))
        rows += r
    pad = -rows % row_mult
    if pad:
        parts.append(jnp.zeros((pad, cols), dtype))
    return jnp.concatenate(parts, axis=0)


def _unpack(packed, shapes, cols=PACK_COLS):
    out, r0 = [], 0
    for shp in shapes:
        n = int(np.prod(shp))
        used = -(-n // cols)
        out.append(packed[r0:r0 + used].reshape(-1)[:n].reshape(shp))
        r0 += _seg_rows(n, cols)
    return out


def _rope_tables(seq):
    inv = 1.0 / (ROPE_THETA ** (jnp.arange(0, 64, 2, dtype=F32) / 64))
    ang = jnp.arange(seq, dtype=F32)[:, None] * inv[None, :]
    cos, sin = jnp.cos(ang), jnp.sin(ang)
    cos128 = jnp.concatenate([cos, cos, cos, cos], axis=1)
    sin128 = jnp.concatenate([-sin, sin, -sin, sin], axis=1)
    return cos128, sin128


def _ffn_perm(a):
    lead = a.shape[:-1]
    nj = D_FF // FFN_TC
    return jnp.swapaxes(a.reshape(lead + (2, nj, FFN_TC)), -3, -2).reshape(lead + (2 * D_FF,))


def _ffn_unperm(a):
    lead = a.shape[:-1]
    nj = D_FF // FFN_TC
    return jnp.swapaxes(a.reshape(lead + (nj, 2, FFN_TC)), -3, -2).reshape(lead + (2 * D_FF,))


def _mixer_a_fwd(xb, w, j, cos, sin, tag, gathers):
    qkv = _mm(xb, w["a_w_qkv"][j], name=f"mm_qkv_{tag}", comm=gathers.get(f"mm_qkv_{tag}"))
    qkv_r = _rope_cols(qkv, cos, sin, (A_HEADS + A_KV_HEADS) * A_HEAD_DIM, name=f"rope_qkv_{tag}",
                       comm=gathers.get(f"rope_qkv_{tag}"))
    o = _swa_fwd(qkv_r, w["a_sinks"][j], name=f"swa_fwd_{tag}", comm=gathers.get(f"swa_fwd_{tag}"))
    return (o, w["a_w_o"][j]), (xb, qkv_r, o)


def _mixer_a_bwd(dzb, res, w, j, cos, sin, tag, grads, make_comm):
    xb, qkv_r, o = res
    do = _mm(dzb, w["a_w_o"][j], tb=True, out_dtype=BF16, name=f"mm_dao_{tag}")
    grads["a_w_o"][j] = _mm(o, dzb, ta=True, out_dtype=BF16, tk=GRAD_TK, name=f"mm_gao_{tag}")
    dq, dcur, dprev, dsink = _swa_bwd(qkv_r, w["a_sinks"][j], do, cos, sin, name=f"swa_bwd_{tag}", comm=make_comm())
    grads["a_sinks"][j] = dsink[0, :A_HEADS]
    dqkv = _swa_dqkv(dq, dcur, dprev, cos, sin, name=f"swa_dqkv_{tag}")
    grads["a_w_qkv"][j] = _mm(xb, dqkv, ta=True, out_dtype=BF16, tk=GRAD_TK, name=f"mm_gqkv_{tag}")
    return dqkv, w["a_w_qkv"][j]


def _mixer_b_fwd(xb, w, j, tag, gathers):
    xw = _mm(xb, w["b_w_in"][j], name=f"mm_bin_{tag}")
    wri = jnp.concatenate([w["b_w_rgate"][j], w["b_w_igate"][j]], axis=-1)
    y, u, h = _lru_fwd(xw, w["b_conv_w"][j], w["b_conv_b"][j][None], wri, w["b_b_rgate"][j][None],
                       w["b_b_igate"][j][None], w["b_lambda"][j][None], name=f"lru_fwd_{tag}", comm=gathers.get(f"lru_fwd_{tag}"))
    return (y, w["b_w_o"][j]), (xb, xw, wri, u, h, y)


def _mixer_b_bwd(dzb, res, w, j, tag, grads):
    xb, xw, wri, u, h, y = res
    dy = _mm(dzb, w["b_w_o"][j], tb=True, out_dtype=BF16, name=f"mm_dbo_{tag}")
    grads["b_w_o"][j] = _mm(y, dzb, ta=True, out_dtype=BF16, tk=GRAD_TK, name=f"mm_gbo_{tag}")
    dxw, dcw, dcb, dwri, dbr, dbi, dlam = _lru_bwd(
        xw, u, h, dy, w["b_conv_w"][j], wri, w["b_b_rgate"][j][None], w["b_b_igate"][j][None], w["b_lambda"][j][None],
        name=f"lru_bwd_{tag}")
    grads["b_conv_w"][j], grads["b_conv_b"][j] = dcw, dcb[0]
    grads["b_w_rgate"][j], grads["b_w_igate"][j] = dwri[..., :LRU_BLOCK_W].astype(BF16), dwri[..., LRU_BLOCK_W:].astype(BF16)
    grads["b_b_rgate"][j], grads["b_b_igate"][j], grads["b_lambda"][j] = dbr[0], dbi[0], dlam[0]
    grads["b_w_in"][j] = _mm(xb, dxw, ta=True, out_dtype=BF16, tk=GRAD_TK, name=f"mm_gbin_{tag}")
    return dxw, w["b_w_in"][j]


def _mla_weights(w, j):
    H = C_HEADS
    uq = w["c_w_uq"][j].reshape(C_Q_RANK, H, C_NOPE + C_ROPE)
    ukv = w["c_w_ukv"][j].reshape(C_KV_RANK, H, C_NOPE + C_V)
    uq_n = uq[:, :, :C_NOPE].reshape(C_Q_RANK, H * C_NOPE)
    uq_r = uq[:, :, C_NOPE:].reshape(C_Q_RANK, H * C_ROPE)
    ukv_p = jnp.concatenate([ukv[:, :, :C_NOPE].reshape(C_KV_RANK, H * C_NOPE),
                             ukv[:, :, C_NOPE:].reshape(C_KV_RANK, H * C_V)], axis=1)
    return uq_n, uq_r, ukv_p


def _mixer_c_fwd(xb, w, j, cos, sin, tag):
    S = xb.shape[0]
    H = C_HEADS
    uq_n, uq_r, ukv_p = _mla_weights(w, j)
    c = _mm(xb, w["c_w_down"][j], name=f"mm_cdown_{tag}")
    cq, ckv, kr = _mla_pre(c, w["c_q_norm"][j][None], w["c_kv_norm"][j][None], cos, sin, name=f"mla_pre_{tag}")
    qn = _mm(cq, uq_n, out_dtype=BF16, name=f"mm_uqn_{tag}")
    qr_flat = _rope_heads(_mm(cq, uq_r, name=f"mm_uqr_{tag}"), cos, sin, transpose=False, name=f"rope_qr_{tag}")
    qr = jnp.transpose(qr_flat.reshape(S, H, C_ROPE), (1, 0, 2))
    kv = _mm(ckv, ukv_p, out_dtype=BF16, name=f"mm_ukv_{tag}")
    o, lse = _mla_flash_fwd(qn, qr, kv, kr, name=f"mla_fwd_{tag}")
    return (o, w["c_w_o"][j]), (xb, c, cq, ckv, kr, qn, qr, kv, o, lse, uq_n, uq_r, ukv_p)


def _mixer_c_bwd(dzb, res, w, j, cos, sin, tag, grads):
    xb, c, cq, ckv, kr, qn, qr, kv, o, lse, uq_n, uq_r, ukv_p = res
    S = xb.shape[0]
    H = C_HEADS
    do = _mm(dzb, w["c_w_o"][j], tb=True, out_dtype=BF16, name=f"mm_dco_{tag}")
    grads["c_w_o"][j] = _mm(o, dzb, ta=True, out_dtype=BF16, tk=GRAD_TK, name=f"mm_gco_{tag}")
    delta = _mla_delta(do, o, name=f"mla_delta_{tag}")
    dqn, dqr = _mla_flash_dq(qn, qr, kv, kr, do, lse, delta, name=f"mla_dq_{tag}")
    dkn, dv, dkr_h = _mla_flash_dkv(qn, qr, kv, kr, do, lse, delta, name=f"mla_dkv_{tag}")
    dkv = jnp.concatenate([dkn, dv], axis=1)
    dqr_flat = _rope_heads(jnp.transpose(dqr, (1, 0, 2)).reshape(S, H * C_ROPE), cos, sin, transpose=True, name=f"rope_dqr_{tag}")
    g_uq_n = _mm(cq, dqn, ta=True, out_dtype=BF16, tk=GRAD_TK, name=f"mm_guqn_{tag}")
    g_uq_r = _mm(cq, dqr_flat, ta=True, out_dtype=BF16, tk=GRAD_TK, name=f"mm_guqr_{tag}")
    g_ukv = _mm(ckv, dkv, ta=True, out_dtype=BF16, tk=GRAD_TK, name=f"mm_gukv_{tag}")
    grads["c_w_uq"][j] = jnp.concatenate([g_uq_n.reshape(C_Q_RANK, H, C_NOPE), g_uq_r.reshape(C_Q_RANK, H, C_ROPE)],
                                         axis=2).reshape(C_Q_RANK, H * (C_NOPE + C_ROPE))
    grads["c_w_ukv"][j] = jnp.concatenate([g_ukv[:, :H * C_NOPE].reshape(C_KV_RANK, H, C_NOPE),
                                           g_ukv[:, H * C_NOPE:].reshape(C_KV_RANK, H, C_V)],
                                          axis=2).reshape(C_KV_RANK, H * (C_NOPE + C_V))
    dcq_a = _mm(dqn, uq_n, tb=True, name=f"mm_dcqa_{tag}")
    dcq_b = _mm(dqr_flat, uq_r, tb=True, name=f"mm_dcqb_{tag}")
    dckv = _mm(dkv, ukv_p, tb=True, name=f"mm_dckv_{tag}")
    dc, dqg, dkvg = _mla_post_bwd(c, dcq_a, dcq_b, dckv, dkr_h, w["c_q_norm"][j][None], w["c_kv_norm"][j][None], cos, sin,
                                  name=f"mla_post_{tag}")
    grads["c_q_norm"][j], grads["c_kv_norm"][j] = dqg[0], dkvg[0]
    grads["c_w_down"][j] = _mm(xb, dc, ta=True, out_dtype=BF16, tk=GRAD_TK, name=f"mm_gcdown_{tag}")
    return dc, w["c_w_down"][j]


def _layer_big(i, mixer=True, rest=True):
    kind, j = i % N_MIXERS, i // N_MIXERS
    own = [[("a_w_qkv", j), ("a_w_o", j)], [("b_w_in", j), ("b_w_rgate", j), ("b_w_igate", j), ("b_w_o", j)],
           [("c_w_down", j), ("c_w_uq", j), ("c_w_ukv", j), ("c_w_o", j)]][kind]
    return (own if mixer else []) + ([("x_w_q", i), ("x_w_o", i), ("f_w_up", i), ("f_w_down", i)] if rest else [])


def _local_step(x, mem, target, w, n_layers, gathers, scatter):
    S = x.shape[0]
    cos, sin = _rope_tables(S)
    grads = {n: [None] * n_layers[n] for n in WEIGHTS if n != "mem_w_kv"}

    xs, xb = x, x.astype(BF16)
    saved = []
    for i in range(DEPTH):
        kind, j = i % N_MIXERS, i // N_MIXERS
        tag = f"l{i}"
        if kind == 0:
            (act, w_out), res = _mixer_a_fwd(xb, w, j, cos, sin, tag, gathers)
        elif kind == 1:
            (act, w_out), res = _mixer_b_fwd(xb, w, j, tag, gathers)
        else:
            (act, w_out), res = _mixer_c_fwd(xb, w, j, cos, sin, tag)
        x1, x1b, xh1, rs1 = _mm_ln_fwd(act, w_out, xs, w["ln_g"][i, 0][None], w["ln_b"][i, 0][None], name=f"ln1_{tag}",
                                       comm=gathers.get(f"ln1_{tag}"))
        if i == 0:
            mkv = _mm(mem, w["mem_w_kv"], out_dtype=BF16, tm=MEM_LEN, name="mm_memkv")
        q = _mm(x1b, w["x_w_q"][i], out_dtype=BF16, name=f"mm_xq_{tag}")
        o = _xattn_fwd(q, mkv, name=f"xattn_fwd_{tag}")
        x2, x2b, xh2, rs2 = _mm_ln_fwd(o, w["x_w_o"][i], x1, w["ln_g"][i, 1][None], w["ln_b"][i, 1][None], name=f"ln2_{tag}")
        w_up = w["f_w_up"][i]
        cwp, cbp = _ffn_perm(w["f_conv_w"][i]), _ffn_perm(w["f_conv_b"][i][None])
        hh = _mm(x2b, w_up, out_dtype=BF16, tm=2 * MM_T, tn=FFN_TC, name=f"mm_up_{tag}", comm=gathers.get(f"mm_up_{tag}"))
        a = _ffn_act_fwd(hh, cwp, cbp, name=f"ffn_act_{tag}", comm=gathers.get(f"ffn_act_{tag}"))
        x3, x3b, xh3, rs3 = _mm_ln_fwd(a, w["f_w_down"][i], x2, w["ln_g"][i, 2][None], w["ln_b"][i, 2][None],
                                       name=f"ln3_{tag}", comm=gathers.get(f"ln3_{tag}"))
        saved.append((res, (xh1, rs1, x1b), (q, o, xh2, rs2, x2b), (w_up, cwp, cbp, hh, a, xh3, rs3)))
        xs, xb = x3, x3b

    dloss, loss = _loss_fwd(xs, target, name="loss")

    dmkv = None
    ln_dg = [[None] * 3 for _ in range(DEPTH)]
    ln_db = [[None] * 3 for _ in range(DEPTH)]
    for i in reversed(range(DEPTH)):
        kind, j = i % N_MIXERS, i // N_MIXERS
        tag = f"l{i}"
        res, (xh1, rs1, x1b), (q, o, xh2, rs2, x2b), (w_up, cwp, cbp, hh, a, xh3, rs3) = saved[i]
        if i == DEPTH - 1:
            dz3, dz3b, ln_dg[i][2], ln_db[i][2] = _ln_bwd(None, dloss, xh3, rs3, w["ln_g"][i, 2][None], name=f"ln3_bwd_{tag}")
        else:
            dz3, dz3b, ln_dg[i][2], ln_db[i][2] = _mm_ln_bwd(dz1, d_in, w_in, xh3, rs3, w["ln_g"][i, 2][None], name=f"ln3_bwd_{tag}")
        da = _mm(dz3b, w["f_w_down"][i], tb=True, out_dtype=BF16, name=f"mm_ddown_{tag}")
        grads["f_w_down"][i] = _mm(a, dz3b, ta=True, out_dtype=BF16, tm=FFN_TC, tk=GRAD_TK, name=f"mm_gdown_{tag}")
        later = scatter(_layer_big(i + 1), grads) if i + 1 < DEPTH else None
        dh, dcw, dcb = _ffn_act_bwd(hh, da, cwp, cbp, name=f"ffn_act_bwd_{tag}", comm=later)
        grads["f_conv_w"][i], grads["f_conv_b"][i] = _ffn_unperm(dcw), _ffn_unperm(dcb)[0]
        grads["f_w_up"][i] = _mm(x2b, dh, ta=True, out_dtype=BF16, tn=FFN_TC, tk=GRAD_TK, name=f"mm_gup_{tag}")

        dz2, dz2b, ln_dg[i][1], ln_db[i][1] = _mm_ln_bwd(dz3, dh, w_up, xh2, rs2, w["ln_g"][i, 1][None], name=f"ln2_bwd_{tag}")
        do = _mm(dz2b, w["x_w_o"][i], tb=True, out_dtype=BF16, name=f"mm_dxo_{tag}")
        grads["x_w_o"][i] = _mm(o, dz2b, ta=True, out_dtype=BF16, tk=GRAD_TK, name=f"mm_gxo_{tag}")
        dq, dmkv_i = _xattn_bwd(q, mkv, do, name=f"xattn_bwd_{tag}")
        dmkv = dmkv_i if dmkv is None else dmkv + dmkv_i
        if i == 0:
            g_mem = _mm(mem, dmkv, ta=True, out_dtype=BF16, tm=512, name="mm_gmemkv")
        grads["x_w_q"][i] = _mm(x1b, dq, ta=True, out_dtype=BF16, tk=GRAD_TK, name=f"mm_gxq_{tag}")

        dz1, dz1b, ln_dg[i][0], ln_db[i][0] = _mm_ln_bwd(dz2, dq, w["x_w_q"][i], xh1, rs1, w["ln_g"][i, 0][None], name=f"ln1_bwd_{tag}")
        if kind == 0:
            if i == 0:
                grads["mem_w_kv"] = g_mem
                done = lambda: scatter(_layer_big(0, mixer=False) + [("a_w_o", 0), ("mem_w_kv", None)], grads)
            else:
                done = lambda: None
            d_in, w_in = _mixer_a_bwd(dz1b, res, w, j, cos, sin, tag, grads, done)
        elif kind == 1:
            d_in, w_in = _mixer_b_bwd(dz1b, res, w, j, tag, grads)
        else:
            d_in, w_in = _mixer_c_bwd(dz1b, res, w, j, cos, sin, tag, grads)

    grad_x = _axpy(dz1, _mm(d_in, w_in, tb=True, name="mm_dx_l0"), name="grad_x")
    big = [n for n, _ in SHARDED[:N_BIG]]
    out = {n: (g if n in big else jnp.stack(g, axis=0)) for n, g in grads.items() if n not in ("ln_g", "ln_b")}
    out["ln_g"] = jnp.stack([jnp.concatenate(r, axis=0) for r in ln_dg], axis=0)
    out["ln_b"] = jnp.stack([jnp.concatenate(r, axis=0) for r in ln_db], axis=0)
    return loss, grad_x, out


def kernel(x, mem, a_w_qkv, a_sinks, a_w_o, b_w_in, b_conv_w, b_conv_b, b_w_rgate, b_b_rgate, b_w_igate, b_b_igate, b_lambda, b_w_o, c_w_down, c_q_norm, c_kv_norm, c_w_uq, c_w_ukv, c_w_o, mem_w_kv, x_w_q, x_w_o, f_w_up, f_conv_w, f_conv_b, f_w_down, ln_g, ln_b, loss_target, m_a_w_qkv, m_a_sinks, m_a_w_o, m_b_w_in, m_b_conv_w, m_b_conv_b, m_b_w_rgate, m_b_b_rgate, m_b_w_igate, m_b_b_igate, m_b_lambda, m_b_w_o, m_c_w_down, m_c_q_norm, m_c_kv_norm, m_c_w_uq, m_c_w_ukv, m_c_w_o, m_mem_w_kv, m_x_w_q, m_x_w_o, m_f_w_up, m_f_conv_w, m_f_conv_b, m_f_w_down, m_ln_g, m_ln_b, v_a_w_qkv, v_a_sinks, v_a_w_o, v_b_w_in, v_b_conv_w, v_b_conv_b, v_b_w_rgate, v_b_b_rgate, v_b_w_igate, v_b_b_igate, v_b_lambda, v_b_w_o, v_c_w_down, v_c_q_norm, v_c_kv_norm, v_c_w_uq, v_c_w_ukv, v_c_w_o, v_mem_w_kv, v_x_w_q, v_x_w_o, v_f_w_up, v_f_conv_w, v_f_conv_b, v_f_w_down, v_ln_g, v_ln_b):
    loc = locals()
    shard = {n: loc[n] for n in WEIGHTS}
    mom = {n: loc["m_" + n] for n in WEIGHTS}
    var = {n: loc["v_" + n] for n in WEIGHTS}
    names = [n for n, _ in SHARDED]
    axis = dict(SHARDED)
    big, small = names[:N_BIG], names[N_BIG:]

    chip = 2 * lax.axis_index("x") + lax.axis_index("y")
    n_layers = {n: shard[n].shape[0] for n in WEIGHTS if n != "mem_w_kv"}
    layer_axis = lambda n, j: axis[n] - (0 if j is None else 1)

    shard_b = {n: shard[n].astype(BF16) for n in big}
    w = {n: [None] * n_layers[n] for n in big if n != "mem_w_kv"}

    small_pack = _pack([shard[n] for n in small], F32)
    shard_b["small"] = small_pack

    def gather(pairs):
        def deliver(outs):
            for (n, j), o in zip(pairs, outs):
                if j is None:
                    w[n] = o
                else:
                    w[n][j] = o
        return _gather_comm([(shard_b[n], j, 0 if n == "small" else layer_axis(n, j), n == "f_w_up") for n, j in pairs], deliver)

    _run_comm(gather([("a_w_qkv", 0), ("small", None)]), name="gather_first")
    gathers = {
        "mm_qkv_l0": gather([("a_w_o", 0)]),
        "rope_qkv_l0": gather([("x_w_q", 0), ("mem_w_kv", None)]),
        "swa_fwd_l0": gather([("x_w_o", 0), ("f_w_up", 0)]),
        "ln1_l0": gather([("f_w_down", 0)]),
        "mm_up_l0": gather(_layer_big(1, rest=False) + [("x_w_q", 1), ("x_w_o", 1)]),
        "ffn_act_l0": gather([("f_w_up", 1)]),
        "ln3_l0": gather([("f_w_down", 1)]),
        "lru_fwd_l1": gather(_layer_big(2, rest=False) + [("x_w_q", 2), ("x_w_o", 2)]),
        "mm_up_l1": gather([("f_w_up", 2)]),
        "ffn_act_l1": gather([("f_w_down", 2)] + _layer_big(3, rest=False)),
        "mm_up_l2": gather([("f_w_up", 3)]),
        "ffn_act_l2": gather([("f_w_down", 3), ("x_w_q", 3), ("x_w_o", 3)]),
    }
    got = w.pop("small").reshape((4,) + small_pack.shape)
    per_chip = [_unpack(got[s], [shard[n].shape for n in small]) for s in range(4)]
    for k, n in enumerate(small):
        w[n] = jnp.concatenate([per_chip[s][k] for s in range(4)], axis=axis[n])
    for n in REPLICATED:
        w[n] = shard[n]

    recv = {}

    def scatter(pairs, grads):
        def deliver(outs):
            recv.update(dict(zip(pairs, outs)))
        return _scatter_comm([(grads[n] if j is None else grads[n][j], layer_axis(n, j), n == "f_w_up") for n, j in pairs], deliver)

    loss, grad_x, g = _local_step(x[0], mem[0], loss_target[0], w, n_layers, gathers, scatter)
    _run_comm(scatter([("a_w_qkv", 0)], g), name="scatter_last")

    view = {n: (int(np.prod(shard[n].shape[:-1])), shard[n].shape[-1]) for n in big}
    parts = []
    for n in big:
        layers = n_layers.get(n, 1)
        part = None
        for j in range(layers):
            r = recv[(n, j if n in n_layers else None)]
            part = _sum_partials(r.reshape(4, view[n][0] // layers, view[n][1]), part, j, layers, name=f"sum_{n}_{j}")
        parts.append(part)
    sibs = _swap_cores(parts, name="swap_cores")
    grad_o, delta_o, m_o, v_o = {}, {}, {}, {}
    for n, part, sib in zip(big, parts, sibs):
        res = _adamw(shard[n].reshape(view[n]), mom[n].reshape(view[n]), var[n].reshape(view[n]), part, sib, name=f"adamw_{n}")
        for d, r in zip((grad_o, delta_o, m_o, v_o), res):
            d[n] = r.reshape(shard[n].shape)

    rest = small + REPLICATED
    vec = _pack([g[n] for n in rest] + [loss], F32, cols=LANES, row_mult=8)
    tot = _unpack(_all_reduce_small(vec, name="allreduce_small"), [g[n].shape for n in rest] + [(1, 1)], cols=LANES)
    loss_tot = tot[-1].reshape(())
    mine = {n: t for n, t in zip(rest, tot)}
    for n in small:
        size = shard[n].shape[axis[n]]
        mine[n] = lax.dynamic_slice_in_dim(mine[n], chip * size, size, axis=axis[n])
    rpack = lambda d: _pack([d[n] for n in rest], F32, cols=LANES, row_mult=8)
    res = _adamw(rpack(shard), rpack(mom), rpack(var), rpack(mine), None, name="adamw_small")
    for d, r in zip((grad_o, delta_o, m_o, v_o), res):
        d.update(dict(zip(rest, _unpack(r, [shard[n].shape for n in rest], cols=LANES))))

    return (loss_tot, grad_x[None], *[grad_o[n] for n in WEIGHTS], *[delta_o[n] for n in WEIGHTS],
            *[m_o[n] for n in WEIGHTS], *[v_o[n] for n in WEIGHTS])
```

```python
import functools
import math

import numpy as np
import jax
import jax.numpy as jnp
from jax import lax
from jax.experimental import pallas as pl
from jax.experimental.pallas import tpu as pltpu

F32 = jnp.float32
BF16 = jnp.bfloat16
MESH = pl.DeviceIdType.MESH

D_MODEL = 1024
DEPTH = 4
N_MIXERS = 3
MEM_LEN = 256
BLOCK = 128
ROPE_THETA = 10000.0
NEG = -1e30
LN_EPS = 1e-5
RMS_EPS = 1e-6
A_HEADS, A_KV_HEADS, A_HEAD_DIM = 16, 4, 64
LRU_BLOCKS, LRU_BLOCK_W, LRU_CONV, LRU_C = 4, 256, 4, 8.0
C_HEADS, C_NOPE, C_ROPE, C_V, C_Q_RANK, C_KV_RANK = 8, 128, 64, 128, 384, 256
X_HEADS, X_HEAD_DIM = 4, 256
D_FF, FFN_CONV = 2816, 3
ALPHA = (2.0 * DEPTH) ** 0.25
ADAM_LR, ADAM_B1, ADAM_B2, ADAM_EPS, ADAM_WD, ADAM_STEP = 0.001, 0.9, 0.999, 1e-08, 0.01, 10

VMEM_LIMIT = 56 * 2 ** 20
LANES = 128
PACK_COLS = 1024
ROW_T = 512
ACT_T = 256
LRU_T = 256
FLASH_T = 512
FFN_TC = 1408
MM_T = 1024
GRAD_TK = 2048

C11 = (((1,), (1,)), ((), ()))
C00 = (((0,), (0,)), ((), ()))

SHARDED = [
    ("a_w_qkv", 2), ("a_w_o", 1), ("b_w_in", 2), ("b_w_rgate", 2), ("b_w_igate", 2), ("b_w_o", 1), ("c_w_down", 1),
    ("c_w_uq", 2), ("c_w_ukv", 2), ("c_w_o", 1), ("mem_w_kv", 1), ("x_w_q", 1), ("x_w_o", 1), ("f_w_up", 2),
    ("f_w_down", 1),
    ("b_conv_w", 2), ("c_q_norm", 1), ("c_kv_norm", 1), ("f_conv_w", 2), ("ln_g", 2), ("ln_b", 2),
]
N_BIG = 15
REPLICATED = ["a_sinks", "b_conv_b", "b_b_rgate", "b_b_igate", "b_lambda", "f_conv_b"]
WEIGHTS = ["a_w_qkv", "a_sinks", "a_w_o", "b_w_in", "b_conv_w", "b_conv_b", "b_w_rgate", "b_b_rgate", "b_w_igate",
           "b_b_igate", "b_lambda", "b_w_o", "c_w_down", "c_q_norm", "c_kv_norm", "c_w_uq", "c_w_ukv", "c_w_o",
           "mem_w_kv", "x_w_q", "x_w_o", "f_w_up", "f_conv_w", "f_conv_b", "f_w_down", "ln_g", "ln_b"]


def _params(*sem):
    return pltpu.CompilerParams(dimension_semantics=sem, vmem_limit_bytes=VMEM_LIMIT)


def _sds(shape, dtype):
    return jax.ShapeDtypeStruct(tuple(shape), dtype)


def _mm(a, b, *, name, ta=False, tb=False, out_dtype=F32, tm=None, tn=None, tk=None, comm=None):
    (K, M) = a.shape if ta else a.shape[::-1]
    (N, K2) = b.shape if tb else b.shape[::-1]
    assert K == K2, (a.shape, b.shape, ta, tb)
    tm = min(tm or MM_T, M)
    tn = min(tn or N, N)
    tk = min(tk or K, K)
    assert M % tm == 0 and N % tn == 0 and K % tk == 0, (M, N, K, tm, tn, tk)
    nk = K // tk
    use_acc = nk > 1 and out_dtype != F32
    dims = (((0 if ta else 1,), (1 if tb else 0,)), ((), ()))

    def body(a_ref, b_ref, o_ref, *scratch):
        p = lax.dot_general(a_ref[...].astype(BF16), b_ref[...].astype(BF16), dims, preferred_element_type=F32)
        if nk == 1:
            o_ref[...] = p.astype(out_dtype)
        else:
            acc = scratch[0] if use_acc else o_ref
            k = pl.program_id(2)

            @pl.when(k == 0)
            def _():
                acc[...] = p

            @pl.when(k > 0)
            def _():
                acc[...] += p

            if use_acc:
                @pl.when(k == nk - 1)
                def _():
                    o_ref[...] = acc[...].astype(out_dtype)

    a_spec = pl.BlockSpec((tk, tm), lambda i, j, k: (k, i)) if ta else pl.BlockSpec((tm, tk), lambda i, j, k: (i, k))
    b_spec = pl.BlockSpec((tn, tk), lambda i, j, k: (j, k)) if tb else pl.BlockSpec((tk, tn), lambda i, j, k: (k, j))
    return _call(
        body, name=name, grid=(M // tm, N // tn, nk), in_specs=[a_spec, b_spec],
        out_specs=[pl.BlockSpec((tm, tn), lambda i, j, k: (i, j))], out_shape=[_sds((M, N), out_dtype)],
        scratch_shapes=[pltpu.VMEM((tm, tn), F32)] if use_acc else [],
        params=_params("parallel", "parallel", "arbitrary"), comm=comm,
    )(a, b)[0]


def _shift_down(cur, prev8, d):
    rolled = pltpu.roll(cur, d, 0)
    rid = lax.broadcasted_iota(jnp.int32, prev8.shape, 0)
    head = jnp.where(rid < d, pltpu.roll(prev8, d, 0), rolled[0:8])
    return jnp.concatenate([head, rolled[8:]], axis=0)


def _shift_up(cur, next8, d):
    n = cur.shape[0]
    rolled = pltpu.roll(cur, n - d, 0)
    rid = lax.broadcasted_iota(jnp.int32, next8.shape, 0)
    tail = jnp.where(rid >= 8 - d, pltpu.roll(next8, 8 - d, 0), rolled[n - 8:n])
    return jnp.concatenate([rolled[0:n - 8], tail], axis=0)


def _swap_halves(x):
    w = x.shape[-1]
    if w == 64:
        return jnp.concatenate([x[:, 32:64], x[:, 0:32]], axis=1)
    lane = lax.broadcasted_iota(jnp.int32, x.shape, 1)
    return jnp.where((lane % 64) < 32, pltpu.roll(x, w - 32, 1), pltpu.roll(x, 32, 1))


def _tile_lanes(t, w):
    return t if w == t.shape[-1] else jnp.concatenate([t] * (w // t.shape[-1]), axis=1)


def _rope(x, cos, sin):
    w = x.shape[-1]
    if w == 64:
        cos, sin = cos[:, :64], sin[:, :64]
    else:
        cos, sin = _tile_lanes(cos, w), _tile_lanes(sin, w)
    return x * cos + _swap_halves(x) * sin


def _rope_t(x, cos, sin):
    w = x.shape[-1]
    if w == 64:
        cos, sin = cos[:, :64], sin[:, :64]
    else:
        cos, sin = _tile_lanes(cos, w), _tile_lanes(sin, w)
    return x * cos - _swap_halves(x) * sin


def _sigmoid(x):
    return 1.0 / (1.0 + jnp.exp(-x))


def _gelu_and_grad(x):
    c0, c1 = math.sqrt(2.0 / math.pi), 0.044715
    t = jnp.tanh(c0 * (x + c1 * x * x * x))
    g = 0.5 * x * (1.0 + t)
    dg = 0.5 * (1.0 + t) + 0.5 * x * (1.0 - t * t) * c0 * (1.0 + 3.0 * c1 * x * x)
    return g, dg


def _neg_expm1(x):
    series = -x * (1.0 + x * (0.5 + x * (1.0 / 6.0 + x * (1.0 / 24.0 + x * (1.0 / 120.0)))))
    return jnp.where(x > -0.1, series, 1.0 - jnp.exp(x))


def _softplus_neg(lam):
    z = -lam
    e = jnp.exp(-jnp.abs(z))
    log1p = jnp.where(e < 0.01, e * (1.0 - e * (0.5 - e * (1.0 / 3.0))), jnp.log(1.0 + e))
    sp = jnp.maximum(z, 0.0) + log1p
    dsp = -_sigmoid(z)
    return sp, dsp


def _ln_fwd(x, y, g, b, *, name):
    S, D = x.shape
    tm = min(ROW_T, S)

    def body(x_ref, y_ref, g_ref, b_ref, o_ref, ob_ref, xh_ref, rs_ref):
        z = ALPHA * x_ref[...] + y_ref[...]
        mu = jnp.mean(z, axis=-1, keepdims=True)
        zc = z - mu
        var = jnp.mean(zc * zc, axis=-1, keepdims=True)
        r = lax.rsqrt(var + LN_EPS)
        xh = zc * r
        o = xh * g_ref[...] + b_ref[...]
        o_ref[...] = o
        ob_ref[...] = o.astype(BF16)
        xh_ref[...] = xh
        rs_ref[...] = r

    row = pl.BlockSpec((tm, D), lambda i: (i, 0))
    vec = pl.BlockSpec((1, D), lambda i: (0, 0))
    return pl.pallas_call(
        body, name=name, grid=(S // tm,), in_specs=[row, row, vec, vec],
        out_specs=[row, row, row, pl.BlockSpec((tm, 1), lambda i: (i, 0))],
        out_shape=[_sds((S, D), F32), _sds((S, D), BF16), _sds((S, D), F32), _sds((S, 1), F32)],
        compiler_params=_params("parallel"),
    )(x, y, g, b)


def _ln_bwd(d1, d2, xh, rs, g, *, name):
    S, D = xh.shape
    tm = min(ROW_T, S)
    has_d1 = d1 is not None

    def body(*refs):
        if has_d1:
            d1_ref, d2_ref, xh_ref, rs_ref, g_ref, dz_ref, dzb_ref, dg_ref, db_ref = refs
            dout = ALPHA * d1_ref[...] + d2_ref[...]
        else:
            d2_ref, xh_ref, rs_ref, g_ref, dz_ref, dzb_ref, dg_ref, db_ref = refs
            dout = d2_ref[...]
        xh_v = xh_ref[...]
        dxh = dout * g_ref[...]
        m1 = jnp.mean(dxh, axis=-1, keepdims=True)
        m2 = jnp.mean(dxh * xh_v, axis=-1, keepdims=True)
        dz = rs_ref[...] * (dxh - m1 - xh_v * m2)
        dz_ref[...] = dz
        dzb_ref[...] = dz.astype(BF16)

        @pl.when(pl.program_id(0) == 0)
        def _():
            dg_ref[...] = jnp.zeros_like(dg_ref)
            db_ref[...] = jnp.zeros_like(db_ref)

        dg_ref[...] += jnp.sum(dout * xh_v, axis=0, keepdims=True)
        db_ref[...] += jnp.sum(dout, axis=0, keepdims=True)

    row = pl.BlockSpec((tm, D), lambda i: (i, 0))
    vec = pl.BlockSpec((1, D), lambda i: (0, 0))
    ins = ([row] if has_d1 else []) + [row, row, pl.BlockSpec((tm, 1), lambda i: (i, 0)), vec]
    args = ([d1] if has_d1 else []) + [d2, xh, rs, g]
    return pl.pallas_call(
        body, name=name, grid=(S // tm,), in_specs=ins, out_specs=[row, row, vec, vec],
        out_shape=[_sds((S, D), F32), _sds((S, D), BF16), _sds((1, D), F32), _sds((1, D), F32)],
        compiler_params=_params("arbitrary"),
    )(*args)


def _mm_ln_fwd(a, b, x, g, beta, *, name, comm=None):
    S, K = a.shape
    D = b.shape[1]
    tm = min(ROW_T, S)

    def body(a_ref, b_ref, x_ref, g_ref, beta_ref, o_ref, ob_ref, xh_ref, rs_ref):
        y = jnp.dot(a_ref[...].astype(BF16), b_ref[...].astype(BF16), preferred_element_type=F32)
        z = ALPHA * x_ref[...] + y
        mu = jnp.mean(z, axis=-1, keepdims=True)
        zc = z - mu
        var = jnp.mean(zc * zc, axis=-1, keepdims=True)
        r = lax.rsqrt(var + LN_EPS)
        xh = zc * r
        o = xh * g_ref[...] + beta_ref[...]
        o_ref[...] = o
        ob_ref[...] = o.astype(BF16)
        xh_ref[...] = xh
        rs_ref[...] = r

    row = pl.BlockSpec((tm, D), lambda i: (i, 0))
    vec = pl.BlockSpec((1, D), lambda i: (0, 0))
    return _call(
        body, name=name, grid=(S // tm,),
        in_specs=[pl.BlockSpec((tm, K), lambda i: (i, 0)), pl.BlockSpec((K, D), lambda i: (0, 0)), row, vec, vec],
        out_specs=[row, row, row, pl.BlockSpec((tm, 1), lambda i: (i, 0))],
        out_shape=[_sds((S, D), F32), _sds((S, D), BF16), _sds((S, D), F32), _sds((S, 1), F32)],
        params=_params("arbitrary"), comm=comm,
    )(a, b, x, g, beta)


def _mm_ln_bwd(d1, da, wt, xh, rs, g, *, name):
    S, D = xh.shape
    K = da.shape[1]
    tm = min(ROW_T // 2, S)

    def body(d1_ref, da_ref, wt_ref, xh_ref, rs_ref, g_ref, dz_ref, dzb_ref, dg_ref, db_ref):
        d2 = lax.dot_general(da_ref[...].astype(BF16), wt_ref[...].astype(BF16), C11, preferred_element_type=F32)
        dout = ALPHA * d1_ref[...] + d2
        xh_v = xh_ref[...]
        dxh = dout * g_ref[...]
        m1 = jnp.mean(dxh, axis=-1, keepdims=True)
        m2 = jnp.mean(dxh * xh_v, axis=-1, keepdims=True)
        dz = rs_ref[...] * (dxh - m1 - xh_v * m2)
        dz_ref[...] = dz
        dzb_ref[...] = dz.astype(BF16)

        @pl.when(pl.program_id(0) == 0)
        def _():
            dg_ref[...] = jnp.zeros_like(dg_ref)
            db_ref[...] = jnp.zeros_like(db_ref)

        dg_ref[...] += jnp.sum(dout * xh_v, axis=0, keepdims=True)
        db_ref[...] += jnp.sum(dout, axis=0, keepdims=True)

    row = pl.BlockSpec((tm, D), lambda i: (i, 0))
    vec = pl.BlockSpec((1, D), lambda i: (0, 0))
    return pl.pallas_call(
        body, name=name, grid=(S // tm,),
        in_specs=[row, pl.BlockSpec((tm, K), lambda i: (i, 0)), pl.BlockSpec((D, K), lambda i: (0, 0)), row,
                  pl.BlockSpec((tm, 1), lambda i: (i, 0)), vec],
        out_specs=[row, row, vec, vec],
        out_shape=[_sds((S, D), F32), _sds((S, D), BF16), _sds((1, D), F32), _sds((1, D), F32)],
        compiler_params=_params("arbitrary"),
    )(d1, da, wt, xh, rs, g)


def _loss_fwd(y, target, *, name):
    S, D = y.shape
    tm = min(ROW_T, S)

    def body(y_ref, t_ref, d_ref, l_ref):
        e = y_ref[...] - t_ref[...]
        d_ref[...] = e * (1.0 / D)

        @pl.when(pl.program_id(0) == 0)
        def _():
            l_ref[...] = jnp.zeros_like(l_ref)

        part = jnp.sum(e * e, axis=0, keepdims=True)
        l_ref[...] += (0.5 / D) * jnp.sum(part, axis=1, keepdims=True)

    row = pl.BlockSpec((tm, D), lambda i: (i, 0))
    return pl.pallas_call(
        body, name=name, grid=(S // tm,), in_specs=[row, row],
        out_specs=[row, pl.BlockSpec((1, 1), lambda i: (0, 0))], out_shape=[_sds((S, D), F32), _sds((1, 1), F32)],
        compiler_params=_params("arbitrary"),
    )(y, target)


def _axpy(d1, d2, *, name):
    S, D = d1.shape
    tm = min(ROW_T, S)

    def body(a_ref, b_ref, o_ref):
        o_ref[...] = ALPHA * a_ref[...] + b_ref[...]

    row = pl.BlockSpec((tm, D), lambda i: (i, 0))
    return pl.pallas_call(body, name=name, grid=(S // tm,), in_specs=[row, row], out_specs=row,
                          out_shape=_sds((S, D), F32), compiler_params=_params("parallel"))(d1, d2)


def _ffn_act_fwd(h, cw, cb, *, name, comm=None):
    S, W = h.shape
    tc = FFN_TC
    nj = W // (2 * tc)
    tm = min(ACT_T, S)

    def body(h_ref, w_ref, b_ref, a_ref, hc_ref, carry):
        @pl.when(pl.program_id(1) == 0)
        def _():
            carry[...] = jnp.zeros_like(carry)

        cur = h_ref[...].astype(F32)
        prev8 = carry[...]
        hc = cur * w_ref[2:3, :] + _shift_down(cur, prev8, 1) * w_ref[1:2, :] + _shift_down(cur, prev8, 2) * w_ref[0:1, :]
        hc = hc + b_ref[...]
        carry[...] = cur[tm - 8:tm]
        hg, hu = hc[:, :tc], hc[:, tc:]
        a_ref[...] = (hg * _sigmoid(hg) * hu).astype(BF16)
        hc_ref[...] = hc.astype(BF16)

    return _call(
        body, name=name, grid=(nj, S // tm),
        in_specs=[pl.BlockSpec((tm, 2 * tc), lambda j, i: (i, j)), pl.BlockSpec((3, 2 * tc), lambda j, i: (0, j)),
                  pl.BlockSpec((1, 2 * tc), lambda j, i: (0, j))],
        out_specs=[pl.BlockSpec((tm, tc), lambda j, i: (i, j)), pl.BlockSpec((tm, 2 * tc), lambda j, i: (i, j))],
        out_shape=[_sds((S, W // 2), BF16), _sds((S, W), BF16)],
        scratch_shapes=[pltpu.VMEM((8, 2 * tc), F32)],
        params=_params("parallel", "arbitrary"), comm=comm,
    )(h, cw, cb)


def _ffn_act_bwd(h, hc, da, cw, *, name, comm=None):
    S, W = h.shape
    tc = FFN_TC
    nj = W // (2 * tc)
    tm = min(ACT_T, S)
    ni = S // tm

    def body(h_ref, hc_ref, da_ref, w_ref, dh_ref, dw_ref, db_ref, carry):
        @pl.when(pl.program_id(1) == 0)
        def _():
            carry[...] = jnp.zeros_like(carry)
            dw_ref[...] = jnp.zeros_like(dw_ref)
            db_ref[...] = jnp.zeros_like(db_ref)

        cur = h_ref[...].astype(F32)
        hcv = hc_ref[...].astype(F32)
        hg, hu = hcv[:, :tc], hcv[:, tc:]
        d = da_ref[...].astype(F32)
        sg = _sigmoid(hg)
        dg = d * hu * (sg * (1.0 + hg * (1.0 - sg)))
        du = d * (hg * sg)
        dhc = jnp.concatenate([dg, du], axis=1)
        next8 = carry[...]
        up = [dhc, _shift_up(dhc, next8, 1), _shift_up(dhc, next8, 2)]
        db_ref[...] += jnp.sum(dhc, axis=0, keepdims=True)
        for k in range(3):
            dw_ref[k:k + 1, :] += jnp.sum(up[2 - k] * cur, axis=0, keepdims=True)
        dh = up[0] * w_ref[2:3, :] + up[1] * w_ref[1:2, :] + up[2] * w_ref[0:1, :]
        carry[...] = dhc[0:8]
        dh_ref[...] = dh.astype(BF16)

    rev = lambda j, i: (ni - 1 - i, j)
    return _call(
        body, name=name, grid=(nj, ni),
        in_specs=[pl.BlockSpec((tm, 2 * tc), rev), pl.BlockSpec((tm, 2 * tc), rev), pl.BlockSpec((tm, tc), rev),
                  pl.BlockSpec((3, 2 * tc), lambda j, i: (0, j))],
        out_specs=[pl.BlockSpec((tm, 2 * tc), rev), pl.BlockSpec((3, 2 * tc), lambda j, i: (0, j)),
                   pl.BlockSpec((1, 2 * tc), lambda j, i: (0, j))],
        out_shape=[_sds((S, W), BF16), _sds((3, W), F32), _sds((1, W), F32)],
        scratch_shapes=[pltpu.VMEM((8, 2 * tc), F32)],
        params=_params("parallel", "arbitrary"), comm=comm,
    )(h, hc, da, cw)


def _xattn_softmax(qk):
    s = qk * (X_HEAD_DIM ** -0.5)
    p = jnp.exp(s - jnp.max(s, axis=-1, keepdims=True))
    return p / jnp.sum(p, axis=-1, keepdims=True)


def _xattn_fwd(q, mkv, *, name):
    S, D = q.shape
    tm = min(ROW_T, S)

    def body(q_ref, k_ref, v_ref, o_ref):
        sls = [slice(h * X_HEAD_DIM, (h + 1) * X_HEAD_DIM) for h in range(X_HEADS)]
        scores = lambda h: lax.dot_general(q_ref[:, sls[h]], k_ref[:, sls[h]], C11, preferred_element_type=F32)
        nxt = scores(0)
        for h in range(X_HEADS):
            qk = nxt
            if h + 1 < X_HEADS:
                nxt = scores(h + 1)
            p = _xattn_softmax(qk)
            o_ref[:, sls[h]] = jnp.dot(p.astype(BF16), v_ref[:, sls[h]], preferred_element_type=F32).astype(BF16)

    return pl.pallas_call(
        body, name=name, grid=(S // tm,),
        in_specs=[pl.BlockSpec((tm, D), lambda i: (i, 0)), pl.BlockSpec((MEM_LEN, D), lambda i: (0, 0)),
                  pl.BlockSpec((MEM_LEN, D), lambda i: (0, 1))],
        out_specs=pl.BlockSpec((tm, D), lambda i: (i, 0)), out_shape=_sds((S, D), BF16),
        compiler_params=_params("parallel"),
    )(q, mkv, mkv)


def _xattn_bwd(q, mkv, do, *, name):
    S, D = q.shape
    tm = min(ROW_T, S)
    scale = X_HEAD_DIM ** -0.5

    def body(q_ref, k_ref, v_ref, do_ref, dq_ref, dkv_ref):
        @pl.when(pl.program_id(0) == 0)
        def _():
            dkv_ref[...] = jnp.zeros_like(dkv_ref)

        def products(h):
            sl = slice(h * X_HEAD_DIM, (h + 1) * X_HEAD_DIM)
            return (lax.dot_general(q_ref[:, sl], k_ref[:, sl], C11, preferred_element_type=F32),
                    lax.dot_general(do_ref[:, sl], v_ref[:, sl], C11, preferred_element_type=F32))

        nxt = products(0)
        for h in range(X_HEADS):
            qk, dp = nxt
            if h + 1 < X_HEADS:
                nxt = products(h + 1)
            sl = slice(h * X_HEAD_DIM, (h + 1) * X_HEAD_DIM)
            sv = slice(D + h * X_HEAD_DIM, D + (h + 1) * X_HEAD_DIM)
            qh, kh, doh = q_ref[:, sl], k_ref[:, sl], do_ref[:, sl]
            p = _xattn_softmax(qk)
            ds = (p * (dp - jnp.sum(p * dp, axis=-1, keepdims=True)) * scale).astype(BF16)
            dq_ref[:, sl] = jnp.dot(ds, kh, preferred_element_type=F32).astype(BF16)
            dkv_ref[:, sl] += lax.dot_general(ds, qh, C00, preferred_element_type=F32)
            dkv_ref[:, sv] += lax.dot_general(p.astype(BF16), doh, C00, preferred_element_type=F32)

    row = pl.BlockSpec((tm, D), lambda i: (i, 0))
    return pl.pallas_call(
        body, name=name, grid=(S // tm,),
        in_specs=[row, pl.BlockSpec((MEM_LEN, D), lambda i: (0, 0)), pl.BlockSpec((MEM_LEN, D), lambda i: (0, 1)), row],
        out_specs=[row, pl.BlockSpec((MEM_LEN, 2 * D), lambda i: (0, 0))],
        out_shape=[_sds((S, D), BF16), _sds((MEM_LEN, 2 * D), F32)],
        compiler_params=_params("arbitrary"),
    )(q, mkv, mkv, do)


def _rope_cols(x, cos, sin, n_rope, *, name, comm=None):
    S, W = x.shape
    tm = min(ROW_T, S)

    def body(x_ref, c_ref, s_ref, o_ref):
        o_ref[:, :n_rope] = _rope(x_ref[:, :n_rope], c_ref[...], s_ref[...]).astype(BF16)
        if n_rope < W:
            o_ref[:, n_rope:] = x_ref[:, n_rope:].astype(BF16)

    row = pl.BlockSpec((tm, W), lambda i: (i, 0))
    tab = pl.BlockSpec((tm, LANES), lambda i: (i, 0))
    return _call(body, name=name, grid=(S // tm,), in_specs=[row, tab, tab], out_specs=[row],
                 out_shape=[_sds((S, W), BF16)], params=_params("arbitrary"), comm=comm)(x, cos, sin)[0]


def _swa_band(n, stacked):
    qi = jnp.bitwise_and(lax.broadcasted_iota(jnp.int32, (stacked * BLOCK, 2 * BLOCK), 0), BLOCK - 1)
    kj = lax.broadcasted_iota(jnp.int32, (stacked * BLOCK, 2 * BLOCK), 1)
    first = jnp.where(n > 0, 0, BLOCK)
    return ((kj < BLOCK) & (kj > qi + first)) | ((kj >= BLOCK) & (kj - BLOCK <= qi))


def _swa_sink_rows(sink_ref, heads):
    row = lax.broadcasted_iota(jnp.int32, (len(heads) * BLOCK, 1), 0)
    col = jnp.full(row.shape, sink_ref[heads[-1]], F32)
    for gi in range(len(heads) - 2, -1, -1):
        col = jnp.where(row < (gi + 1) * BLOCK, sink_ref[heads[gi]], col)
    return col


def _swa_softmax(qk, band, sink):
    s = jnp.where(band, qk * (A_HEAD_DIM ** -0.5), NEG)
    m = jnp.maximum(jnp.max(s, axis=-1, keepdims=True), sink)
    p = jnp.exp(s - m)
    e_sink = jnp.exp(sink - m)
    den = jnp.sum(p, axis=-1, keepdims=True) + e_sink
    return p / den, e_sink / den


def _swa_specs():
    nq, nkv = A_HEADS * A_HEAD_DIM, A_KV_HEADS * A_HEAD_DIM
    kb, vb = nq // nkv, nq // nkv + 1
    prev = lambda n: jnp.maximum(n - 1, 0)
    return [pl.BlockSpec((BLOCK, nq), lambda n: (n, 0)),
            pl.BlockSpec((BLOCK, nkv), lambda n: (n, kb)), pl.BlockSpec((BLOCK, nkv), lambda n: (prev(n), kb)),
            pl.BlockSpec((BLOCK, nkv), lambda n: (n, vb)), pl.BlockSpec((BLOCK, nkv), lambda n: (prev(n), vb)),
            pl.BlockSpec(memory_space=pltpu.SMEM)]


def _swa_fwd(qkv, sinks, *, name, comm=None):
    S = qkv.shape[0]
    hd, grp = A_HEAD_DIM, A_HEADS // A_KV_HEADS

    def body(q_ref, kc_ref, kp_ref, vc_ref, vp_ref, sink_ref, o_ref):
        band = _swa_band(pl.program_id(0), grp)
        qa, kc, kp, vc, vp = q_ref[...], kc_ref[...], kp_ref[...], vc_ref[...], vp_ref[...]
        def products(hk):
            ks = slice(hk * hd, (hk + 1) * hd)
            k = jnp.concatenate([kp[:, ks], kc[:, ks]], axis=0)
            v = jnp.concatenate([vp[:, ks], vc[:, ks]], axis=0)
            q = jnp.concatenate([qa[:, h * hd:(h + 1) * hd] for h in range(hk * grp, (hk + 1) * grp)], axis=0)
            return v, lax.dot_general(q, k, C11, preferred_element_type=F32)

        nxt = products(0)
        for hk in range(A_KV_HEADS):
            v, s = nxt
            if hk + 1 < A_KV_HEADS:
                nxt = products(hk + 1)
            heads = [hk * grp + gi for gi in range(grp)]
            p, _ = _swa_softmax(s, band, _swa_sink_rows(sink_ref, heads))
            o = jnp.dot(p.astype(BF16), v, preferred_element_type=F32).astype(BF16)
            for gi, h in enumerate(heads):
                o_ref[:, h * hd:(h + 1) * hd] = o[gi * BLOCK:(gi + 1) * BLOCK]

    return _call(
        body, name=name, grid=(S // BLOCK,), in_specs=_swa_specs(),
        out_specs=[pl.BlockSpec((BLOCK, A_HEADS * hd), lambda n: (n, 0))], out_shape=[_sds((S, A_HEADS * hd), BF16)],
        params=_params("arbitrary"), comm=comm,
    )(qkv, qkv, qkv, qkv, qkv, sinks)[0]


def _swa_bwd(qkv, sinks, do, cos, sin, *, name, comm=None):
    S = qkv.shape[0]
    hd, grp = A_HEAD_DIM, A_HEADS // A_KV_HEADS
    nq, nkv = A_HEADS * hd, A_KV_HEADS * hd
    scale = hd ** -0.5

    def body(q_ref, kc_ref, kp_ref, vc_ref, vp_ref, sink_ref, do_ref, c_ref, s_ref, dq_ref, dc_ref, dp_ref, ds_ref, dq_s):
        @pl.when(pl.program_id(0) == 0)
        def _():
            ds_ref[...] = jnp.zeros_like(ds_ref)

        band = _swa_band(pl.program_id(0), grp)
        lane = lax.broadcasted_iota(jnp.int32, (1, LANES), 1)
        qa, kc, kp, vc, vp, doa = q_ref[...], kc_ref[...], kp_ref[...], vc_ref[...], vp_ref[...], do_ref[...]
        dsink = jnp.zeros((1, LANES), F32)
        def products(hk):
            ks = slice(hk * hd, (hk + 1) * hd)
            k = jnp.concatenate([kp[:, ks], kc[:, ks]], axis=0)
            v = jnp.concatenate([vp[:, ks], vc[:, ks]], axis=0)
            q = jnp.concatenate([qa[:, h * hd:(h + 1) * hd] for h in range(hk * grp, (hk + 1) * grp)], axis=0)
            dog = jnp.concatenate([doa[:, h * hd:(h + 1) * hd] for h in range(hk * grp, (hk + 1) * grp)], axis=0)
            return (k, q, dog, lax.dot_general(q, k, C11, preferred_element_type=F32),
                    lax.dot_general(dog, v, C11, preferred_element_type=F32))

        nxt = products(0)
        for hk in range(A_KV_HEADS):
            k, q, dog, s, dpr = nxt
            if hk + 1 < A_KV_HEADS:
                nxt = products(hk + 1)
            ks = slice(hk * hd, (hk + 1) * hd)
            heads = [hk * grp + gi for gi in range(grp)]
            p, p_sink = _swa_softmax(s, band, _swa_sink_rows(sink_ref, heads))
            delta = jnp.sum(p * dpr, axis=-1, keepdims=True)
            dsc = (p * (dpr - delta) * scale).astype(BF16)
            dqg = jnp.dot(dsc, k, preferred_element_type=F32)
            dk = lax.dot_general(dsc, q, C00, preferred_element_type=F32)
            dv = lax.dot_general(p.astype(BF16), dog, C00, preferred_element_type=F32)
            sink_term = p_sink * delta
            for gi, h in enumerate(heads):
                rows = slice(gi * BLOCK, (gi + 1) * BLOCK)
                dq_s[:, h * hd:(h + 1) * hd] = dqg[rows]
                dsink = dsink + jnp.where(lane == h, -jnp.sum(sink_term[rows], axis=0, keepdims=True), 0.0)
            dp_ref[:, ks] = dk[:BLOCK]
            dc_ref[:, ks] = dk[BLOCK:]
            dp_ref[:, nkv + hk * hd:nkv + (hk + 1) * hd] = dv[:BLOCK]
            dc_ref[:, nkv + hk * hd:nkv + (hk + 1) * hd] = dv[BLOCK:]
        ds_ref[...] += dsink
        dq_ref[...] = _rope_t(dq_s[...], c_ref[...], s_ref[...]).astype(BF16)

    tab = pl.BlockSpec((BLOCK, LANES), lambda n: (n, 0))
    blk = lambda w: pl.BlockSpec((BLOCK, w), lambda n: (n, 0))
    return _call(
        body, name=name, grid=(S // BLOCK,), in_specs=_swa_specs() + [blk(nq), tab, tab],
        out_specs=[blk(nq), blk(2 * nkv), blk(2 * nkv), pl.BlockSpec((1, LANES), lambda n: (0, 0))],
        out_shape=[_sds((S, nq), BF16), _sds((S, 2 * nkv), F32), _sds((S, 2 * nkv), F32), _sds((1, LANES), F32)],
        scratch_shapes=[pltpu.VMEM((BLOCK, nq), F32)],
        params=_params("arbitrary"), comm=comm,
    )(qkv, qkv, qkv, qkv, qkv, sinks, do, cos, sin)


def _swa_dqkv(dq, dcur, dprev, cos, sin, *, name):
    S, nq = dq.shape
    nkv = dcur.shape[1] // 2
    nb = S // BLOCK

    def body(dq_ref, dc_ref, dp_ref, c_ref, s_ref, o_ref):
        o_ref[:, :nq] = dq_ref[...]
        d = dc_ref[...] + jnp.where(pl.program_id(0) < nb - 1, dp_ref[...], 0.0)
        o_ref[:, nq:nq + nkv] = _rope_t(d[:, :nkv], c_ref[...], s_ref[...]).astype(BF16)
        o_ref[:, nq + nkv:] = d[:, nkv:].astype(BF16)

    tab = pl.BlockSpec((BLOCK, LANES), lambda m: (m, 0))
    blk = lambda w: pl.BlockSpec((BLOCK, w), lambda m: (m, 0))
    return pl.pallas_call(
        body, name=name, grid=(nb,),
        in_specs=[blk(nq), blk(2 * nkv), pl.BlockSpec((BLOCK, 2 * nkv), lambda m: (jnp.minimum(m + 1, nb - 1), 0)), tab, tab],
        out_specs=blk(nq + 2 * nkv), out_shape=_sds((S, nq + 2 * nkv), BF16), compiler_params=_params("parallel"),
    )(dq, dcur, dprev, cos, sin)


def _lru_gates(u, wri_ref, br, bi, sp):
    ub = u.astype(BF16)
    rs, igs = [], []
    for hb in range(LRU_BLOCKS):
        sl = slice(hb * LRU_BLOCK_W, (hb + 1) * LRU_BLOCK_W)
        ri = jnp.dot(ub[:, sl], wri_ref[hb], preferred_element_type=F32)
        rs.append(ri[:, :LRU_BLOCK_W])
        igs.append(ri[:, LRU_BLOCK_W:])
    r = _sigmoid(jnp.concatenate(rs, axis=1) + br)
    ig = _sigmoid(jnp.concatenate(igs, axis=1) + bi)
    la = -LRU_C * r * sp
    a = jnp.exp(la)
    sq = jnp.sqrt(_neg_expm1(2.0 * la))
    return r, ig, a, sq


def _lru_fwd(xw, cw, cb, wri, br, bi, lam, *, name, comm=None):
    S = xw.shape[0]
    W = D_MODEL
    tm = min(LRU_T, S)

    def body(gate_ref, up_ref, cw_ref, cb_ref, wri_ref, br_ref, bi_ref, lam_ref, y_ref, u_ref, h_ref, cu, ch, a_s, b_s):
        @pl.when(pl.program_id(0) == 0)
        def _():
            cu[...] = jnp.zeros_like(cu)
            ch[...] = jnp.zeros_like(ch)

        up = up_ref[...]
        prev8 = cu[...]
        u = up * cw_ref[3:4, :] + cb_ref[...]
        for d in range(1, LRU_CONV):
            u = u + _shift_down(up, prev8, d) * cw_ref[3 - d:4 - d, :]
        cu[...] = up[tm - 8:tm]
        u_ref[...] = u
        sp, _ = _softplus_neg(lam_ref[...])
        _, ig, a, sq = _lru_gates(u, wri_ref, br_ref[...], bi_ref[...], sp)
        a_s[...] = a
        b_s[...] = sq * (ig * u)
        rid = lax.broadcasted_iota(jnp.int32, (8, W), 0)

        def tile(t, h):
            r0 = pl.multiple_of(t * 8, 8)
            at, bt = a_s[pl.ds(r0, 8), :], b_s[pl.ds(r0, 8), :]
            out = jnp.zeros((8, W), F32)
            for j in range(8):
                h = at[j:j + 1, :] * h + bt[j:j + 1, :]
                out = jnp.where(rid == j, h, out)
            h_ref[pl.ds(r0, 8), :] = out
            return h

        ch[0:1, :] = lax.fori_loop(0, tm // 8, tile, ch[0:1, :])
        g, _ = _gelu_and_grad(gate_ref[...])
        y_ref[...] = (h_ref[...] * g).astype(BF16)

    row = pl.BlockSpec((tm, W), lambda i: (i, 0))
    full = lambda shape: pl.BlockSpec(shape, lambda i: (0,) * len(shape))
    return _call(
        body, name=name, grid=(S // tm,),
        in_specs=[row, pl.BlockSpec((tm, W), lambda i: (i, 1)), full((LRU_CONV, W)), full((1, W)),
                  full((LRU_BLOCKS, LRU_BLOCK_W, 2 * LRU_BLOCK_W)), full((1, W)), full((1, W)), full((1, W))],
        out_specs=[row, row, row], out_shape=[_sds((S, W), BF16), _sds((S, W), F32), _sds((S, W), F32)],
        scratch_shapes=[pltpu.VMEM((8, W), F32), pltpu.VMEM((8, W), F32), pltpu.VMEM((tm, W), F32), pltpu.VMEM((tm, W), F32)],
        params=_params("arbitrary"), comm=comm,
    )(xw, xw, cw, cb, wri, br, bi, lam)


def _lru_bwd(xw, u, h, dy, cw, wri, br, bi, lam, *, name):
    S = xw.shape[0]
    W = D_MODEL
    tm = min(LRU_T, S)
    nb = S // tm

    def body(gate_ref, up_ref, upp_ref, u_ref, h_ref, hp_ref, dy_ref, cw_ref, wri_ref, br_ref, bi_ref, lam_ref,
             dxw_ref, dcw_ref, dcb_ref, dwri_ref, dbr_ref, dbi_ref, dlam_ref, cg, cdu, a_s, d_s, g_s):
        i = pl.program_id(0)
        r_blk = nb - 1 - i

        @pl.when(i == 0)
        def _():
            cg[...] = jnp.zeros_like(cg)
            cdu[...] = jnp.zeros_like(cdu)
            for ref in (dcw_ref, dcb_ref, dwri_ref, dbr_ref, dbi_ref, dlam_ref):
                ref[...] = jnp.zeros_like(ref)

        u = u_ref[...]
        hv = h_ref[...]
        sp, dsp = _softplus_neg(lam_ref[...])
        r, ig, a, sq = _lru_gates(u, wri_ref, br_ref[...], bi_ref[...], sp)
        dy = dy_ref[...].astype(F32)
        g, dgelu = _gelu_and_grad(gate_ref[...])
        dxw_ref[:, :W] = (dy * hv * dgelu).astype(BF16)
        a_s[...] = a
        d_s[...] = dy * g
        rid = lax.broadcasted_iota(jnp.int32, (8, W), 0)

        def tile(t, c):
            r0 = pl.multiple_of((tm // 8 - 1 - t) * 8, 8)
            at, dt = a_s[pl.ds(r0, 8), :], d_s[pl.ds(r0, 8), :]
            out = jnp.zeros((8, W), F32)
            for j in range(7, -1, -1):
                gt = dt[j:j + 1, :] + c
                c = at[j:j + 1, :] * gt
                out = jnp.where(rid == j, gt, out)
            g_s[pl.ds(r0, 8), :] = out
            return c

        cg[0:1, :] = lax.fori_loop(0, tm // 8, tile, cg[0:1, :])
        gt = g_s[...]
        hprev8 = jnp.where(r_blk > 0, hp_ref[...], 0.0)
        da = gt * _shift_down(hv, hprev8, 1)
        iu = ig * u
        d_iu = gt * sq
        dla = da * a - (gt * iu) * (a * a) / sq
        dlam_ref[...] += jnp.sum(dla * r, axis=0, keepdims=True) * (-LRU_C) * dsp
        dr_pre = dla * (-LRU_C) * sp * r * (1.0 - r)
        di_pre = d_iu * u * ig * (1.0 - ig)
        dbr_ref[...] += jnp.sum(dr_pre, axis=0, keepdims=True)
        dbi_ref[...] += jnp.sum(di_pre, axis=0, keepdims=True)
        ub = u.astype(BF16)
        dus = []
        for hb in range(LRU_BLOCKS):
            sl = slice(hb * LRU_BLOCK_W, (hb + 1) * LRU_BLOCK_W)
            dri = jnp.concatenate([dr_pre[:, sl], di_pre[:, sl]], axis=1).astype(BF16)
            dus.append(lax.dot_general(dri, wri_ref[hb], C11, preferred_element_type=F32))
            dwri_ref[hb] += lax.dot_general(ub[:, sl], dri, C00, preferred_element_type=F32)
        du = d_iu * ig + jnp.concatenate(dus, axis=1)
        dcb_ref[...] += jnp.sum(du, axis=0, keepdims=True)
        up = up_ref[...]
        upprev8 = jnp.where(r_blk > 0, upp_ref[...], 0.0)
        dcw_ref[3:4, :] += jnp.sum(du * up, axis=0, keepdims=True)
        for d in range(1, LRU_CONV):
            dcw_ref[3 - d:4 - d, :] += jnp.sum(du * _shift_down(up, upprev8, d), axis=0, keepdims=True)
        next8 = cdu[...]
        dup = du * cw_ref[3:4, :]
        for d in range(1, LRU_CONV):
            dup = dup + _shift_up(du, next8, d) * cw_ref[3 - d:4 - d, :]
        cdu[...] = du[0:8]
        dxw_ref[:, W:] = dup.astype(BF16)

    rev = lambda c: (lambda i: (nb - 1 - i, c))
    halo = lambda c: (lambda i: (jnp.maximum((nb - 1 - i) * (tm // 8) - 1, 0), c))
    full = lambda shape: pl.BlockSpec(shape, lambda i: (0,) * len(shape))
    vec = full((1, W))
    return pl.pallas_call(
        body, name=name, grid=(nb,),
        in_specs=[pl.BlockSpec((tm, W), rev(0)), pl.BlockSpec((tm, W), rev(1)), pl.BlockSpec((8, W), halo(1)),
                  pl.BlockSpec((tm, W), rev(0)), pl.BlockSpec((tm, W), rev(0)), pl.BlockSpec((8, W), halo(0)),
                  pl.BlockSpec((tm, W), rev(0)), full((LRU_CONV, W)), full((LRU_BLOCKS, LRU_BLOCK_W, 2 * LRU_BLOCK_W)),
                  vec, vec, vec],
        out_specs=[pl.BlockSpec((tm, 2 * W), rev(0)), full((LRU_CONV, W)), vec,
                   full((LRU_BLOCKS, LRU_BLOCK_W, 2 * LRU_BLOCK_W)), vec, vec, vec],
        out_shape=[_sds((S, 2 * W), BF16), _sds((LRU_CONV, W), F32), _sds((1, W), F32),
                   _sds((LRU_BLOCKS, LRU_BLOCK_W, 2 * LRU_BLOCK_W), F32), _sds((1, W), F32), _sds((1, W), F32),
                   _sds((1, W), F32)],
        scratch_shapes=[pltpu.VMEM((8, W), F32), pltpu.VMEM((8, W), F32), pltpu.VMEM((tm, W), F32),
                        pltpu.VMEM((tm, W), F32), pltpu.VMEM((tm, W), F32)],
        compiler_params=_params("arbitrary"),
    )(xw, xw, xw, u, h, h, dy, cw, wri, br, bi, lam)


def _rms(x, g):
    r = lax.rsqrt(jnp.mean(x * x, axis=-1, keepdims=True) + RMS_EPS)
    return x * r * g, r


def _mla_pre(c, qg, kvg, cos, sin, *, name):
    S = c.shape[0]
    tm = min(ROW_T, S)
    q0, k0 = C_Q_RANK, C_Q_RANK + C_KV_RANK

    def body(c_ref, qg_ref, kvg_ref, cs_ref, sn_ref, cq_ref, ckv_ref, kr_ref):
        cq_ref[...] = _rms(c_ref[:, :q0], qg_ref[...])[0].astype(BF16)
        ckv_ref[...] = _rms(c_ref[:, q0:k0], kvg_ref[...])[0].astype(BF16)
        kr_ref[...] = _rope(c_ref[:, k0:], cs_ref[...], sn_ref[...]).astype(BF16)

    blk = lambda w: pl.BlockSpec((tm, w), lambda i: (i, 0))
    vec = lambda w: pl.BlockSpec((1, w), lambda i: (0, 0))
    return pl.pallas_call(
        body, name=name, grid=(S // tm,),
        in_specs=[blk(c.shape[1]), vec(C_Q_RANK), vec(C_KV_RANK), blk(LANES), blk(LANES)],
        out_specs=[blk(C_Q_RANK), blk(C_KV_RANK), blk(C_ROPE)],
        out_shape=[_sds((S, C_Q_RANK), BF16), _sds((S, C_KV_RANK), BF16), _sds((S, C_ROPE), BF16)],
        compiler_params=_params("parallel"),
    )(c, qg, kvg, cos, sin)


def _mla_post_bwd(c, dcq_a, dcq_b, dckv, dkr_h, qg, kvg, cos, sin, *, name):
    S = c.shape[0]
    tm = min(ROW_T, S)
    q0, k0 = C_Q_RANK, C_Q_RANK + C_KV_RANK

    def rms_bwd(x, g, dy):
        r = lax.rsqrt(jnp.mean(x * x, axis=-1, keepdims=True) + RMS_EPS)
        uu = dy * g
        dx = r * uu - x * (r * r * r) * jnp.mean(uu * x, axis=-1, keepdims=True)
        return dx, jnp.sum(dy * x * r, axis=0, keepdims=True)

    def body(c_ref, da_ref, db_ref, dkv_ref, dkr_ref, qg_ref, kvg_ref, cs_ref, sn_ref, dc_ref, dqg_ref, dkvg_ref):
        @pl.when(pl.program_id(0) == 0)
        def _():
            dqg_ref[...] = jnp.zeros_like(dqg_ref)
            dkvg_ref[...] = jnp.zeros_like(dkvg_ref)

        dx, dg = rms_bwd(c_ref[:, :q0], qg_ref[...], da_ref[...] + db_ref[...])
        dc_ref[:, :q0] = dx.astype(BF16)
        dqg_ref[...] += dg
        dx, dg = rms_bwd(c_ref[:, q0:k0], kvg_ref[...], dkv_ref[...])
        dc_ref[:, q0:k0] = dx.astype(BF16)
        dkvg_ref[...] += dg
        dkr = dkr_ref[0]
        for hh in range(1, C_HEADS):
            dkr = dkr + dkr_ref[hh]
        dc_ref[:, k0:] = _rope_t(dkr, cs_ref[...], sn_ref[...]).astype(BF16)

    blk = lambda w: pl.BlockSpec((tm, w), lambda i: (i, 0))
    vec = lambda w: pl.BlockSpec((1, w), lambda i: (0, 0))
    return pl.pallas_call(
        body, name=name, grid=(S // tm,),
        in_specs=[blk(c.shape[1]), blk(C_Q_RANK), blk(C_Q_RANK), blk(C_KV_RANK),
                  pl.BlockSpec((C_HEADS, tm, C_ROPE), lambda i: (0, i, 0)), vec(C_Q_RANK), vec(C_KV_RANK), blk(LANES), blk(LANES)],
        out_specs=[blk(c.shape[1]), vec(C_Q_RANK), vec(C_KV_RANK)],
        out_shape=[_sds(c.shape, BF16), _sds((1, C_Q_RANK), F32), _sds((1, C_KV_RANK), F32)],
        compiler_params=_params("arbitrary"),
    )(c, dcq_a, dcq_b, dckv, dkr_h, qg, kvg, cos, sin)


def _rope_heads(x, cos, sin, *, transpose, name):
    S, W = x.shape
    tm = min(ROW_T, S)
    fn = _rope_t if transpose else _rope

    def body(x_ref, c_ref, s_ref, o_ref):
        o_ref[...] = fn(x_ref[...].astype(F32), c_ref[...], s_ref[...]).astype(BF16)

    row = pl.BlockSpec((tm, W), lambda i: (i, 0))
    tab = pl.BlockSpec((tm, LANES), lambda i: (i, 0))
    return pl.pallas_call(body, name=name, grid=(S // tm,), in_specs=[row, tab, tab], out_specs=row,
                          out_shape=_sds((S, W), BF16), compiler_params=_params("parallel"))(x, cos, sin)


MLA_GROUP = 4
MLA_SCALE = (C_NOPE + C_ROPE) ** -0.5
LOG2E = 1.4426950408889634


def _mla_cat(nope, rope):
    return jnp.concatenate([nope, rope], axis=1)


def _mla_scores2(qn, qr, kn, kr, diagonal):
    s = lax.dot_general(_mla_cat(qn, qr), _mla_cat(kn, kr), C11, preferred_element_type=F32)
    s = s * (MLA_SCALE * LOG2E)
    if diagonal:
        row = lax.broadcasted_iota(jnp.int32, s.shape, 0)
        col = lax.broadcasted_iota(jnp.int32, s.shape, 1)
        s = jnp.where(col <= row, s, NEG)
    return s


def _causal_pairs(n, query_major):
    if query_major:
        pairs = [(i, j) for i in range(n) for j in range(i + 1)]
    else:
        pairs = [(i, j) for j in range(n) for i in range(j, n)]
    return jnp.asarray([p[0] for p in pairs], jnp.int32), jnp.asarray([p[1] for p in pairs], jnp.int32)


def _mla_flash_fwd(qn, qr, kv, kr, *, name):
    S = qn.shape[0]
    H, G, t = C_HEADS, MLA_GROUP, min(FLASH_T, S)
    qi, kj = _causal_pairs(S // t, True)

    def body(qi_ref, kj_ref, qn_ref, qr_ref, kn_ref, v_ref, kr_ref, o_ref, lse_ref, *scr):
        m_s, l_s, acc = scr[:G], scr[G:2 * G], scr[2 * G:]
        p_id = pl.program_id(1)
        i, j = qi_ref[p_id], kj_ref[p_id]
        sls = [slice(hh * LANES, (hh + 1) * LANES) for hh in range(G)]

        @pl.when(j == 0)
        def _():
            for hh in range(G):
                m_s[hh][...] = jnp.full_like(m_s[hh], NEG)
                l_s[hh][...] = jnp.zeros_like(l_s[hh])
                acc[hh][...] = jnp.zeros_like(acc[hh])

        def step(diagonal):
            scores = lambda hh: _mla_scores2(qn_ref[:, sls[hh]], qr_ref[hh], kn_ref[:, sls[hh]], kr_ref[...], diagonal)
            s_next = scores(0)
            for hh in range(G):
                s = s_next
                if hh + 1 < G:
                    s_next = scores(hh + 1)
                m_prev = m_s[hh][...]
                m_new = jnp.maximum(m_prev, jnp.max(s, axis=-1, keepdims=True))
                corr = jnp.exp2(m_prev - m_new)
                pb = jnp.exp2(s - m_new[:, 0:1]).astype(BF16)
                l_s[hh][...] = corr * l_s[hh][...] + jnp.dot(pb, jnp.ones((pb.shape[1], LANES), BF16), preferred_element_type=F32)
                acc[hh][...] = corr * acc[hh][...] + jnp.dot(pb, v_ref[:, sls[hh]], preferred_element_type=F32)
                m_s[hh][...] = m_new

        @pl.when(j < i)
        def _():
            step(False)

        @pl.when(j == i)
        def _():
            step(True)
            for hh in range(G):
                o_ref[:, sls[hh]] = (acc[hh][...] / l_s[hh][...]).astype(BF16)
                lse_ref[:, sls[hh]] = m_s[hh][...] + jnp.log2(l_s[hh][...])

    wide = lambda which, off: pl.BlockSpec((t, G * LANES), lambda h, p, qi, kj: ((qi if which == "q" else kj)[p], off + h))
    return pl.pallas_call(
        body, name=name,
        grid_spec=pltpu.PrefetchScalarGridSpec(
            num_scalar_prefetch=2, grid=(H // G, qi.shape[0]),
            in_specs=[wide("q", 0), pl.BlockSpec((G, t, C_ROPE), lambda h, p, qi, kj: (h, qi[p], 0)),
                      wide("k", 0), wide("k", H // G), pl.BlockSpec((t, C_ROPE), lambda h, p, qi, kj: (kj[p], 0))],
            out_specs=[wide("q", 0), wide("q", 0)],
            scratch_shapes=[pltpu.VMEM((t, LANES), F32)] * (3 * G)),
        out_shape=[_sds((S, H * C_V), BF16), _sds((S, H * LANES), F32)],
        compiler_params=_params("parallel", "arbitrary"),
    )(qi, kj, qn, qr, kv, kv, kr)


def _mla_delta(do, o, *, name):
    S, W = do.shape
    tm = min(ROW_T, S)

    def body(do_ref, o_ref, d_ref):
        for h in range(C_HEADS):
            sl = slice(h * C_V, (h + 1) * C_V)
            d = jnp.sum(do_ref[:, sl].astype(F32) * o_ref[:, sl].astype(F32), axis=-1, keepdims=True)
            d_ref[:, sl] = jnp.broadcast_to(d, (tm, C_V))

    row = pl.BlockSpec((tm, W), lambda i: (i, 0))
    return pl.pallas_call(body, name=name, grid=(S // tm,), in_specs=[row, row], out_specs=row,
                          out_shape=_sds((S, W), F32), compiler_params=_params("parallel"))(do, o)


def _mla_flash_dq(qn, qr, kv, kr, do, lse, delta, *, name):
    S = qn.shape[0]
    H, G, t = C_HEADS, MLA_GROUP, min(FLASH_T, S)
    qi, kj = _causal_pairs(S // t, True)

    def body(qi_ref, kj_ref, qn_ref, qr_ref, kn_ref, v_ref, kr_ref, do_ref, lse_ref, dl_ref, dqn_ref, dqr_ref, acc):
        p_id = pl.program_id(1)
        i, j = qi_ref[p_id], kj_ref[p_id]
        sls = [slice(hh * LANES, (hh + 1) * LANES) for hh in range(G)]

        @pl.when(j == 0)
        def _():
            acc[...] = jnp.zeros_like(acc)

        def step(diagonal):
            def products(hh):
                s = _mla_scores2(qn_ref[:, sls[hh]], qr_ref[hh], kn_ref[:, sls[hh]], kr_ref[...], diagonal)
                return s, lax.dot_general(do_ref[:, sls[hh]], v_ref[:, sls[hh]], C11, preferred_element_type=F32)

            nxt = products(0)
            for hh in range(G):
                s, dp = nxt
                if hh + 1 < G:
                    nxt = products(hh + 1)
                p = jnp.exp2(s - lse_ref[:, hh * LANES:hh * LANES + 1])
                ds = (p * (dp - dl_ref[:, hh * LANES:hh * LANES + 1])).astype(BF16)
                acc[hh] += jnp.dot(ds, _mla_cat(kn_ref[:, sls[hh]], kr_ref[...]), preferred_element_type=F32)

        @pl.when(j < i)
        def _():
            step(False)

        @pl.when(j == i)
        def _():
            step(True)
            for hh in range(G):
                dqn_ref[:, sls[hh]] = (acc[hh, :, :C_NOPE] * MLA_SCALE).astype(BF16)
                dqr_ref[hh] = acc[hh, :, C_NOPE:] * MLA_SCALE

    wide = lambda which, off: pl.BlockSpec((t, G * LANES), lambda h, p, qi, kj: ((qi if which == "q" else kj)[p], off + h))
    qrb = pl.BlockSpec((G, t, C_ROPE), lambda h, p, qi, kj: (h, qi[p], 0))
    return pl.pallas_call(
        body, name=name,
        grid_spec=pltpu.PrefetchScalarGridSpec(
            num_scalar_prefetch=2, grid=(H // G, qi.shape[0]),
            in_specs=[wide("q", 0), qrb, wide("k", 0), wide("k", H // G),
                      pl.BlockSpec((t, C_ROPE), lambda h, p, qi, kj: (kj[p], 0)), wide("q", 0), wide("q", 0), wide("q", 0)],
            out_specs=[wide("q", 0), qrb],
            scratch_shapes=[pltpu.VMEM((G, t, C_NOPE + C_ROPE), F32)]),
        out_shape=[_sds((S, H * C_NOPE), BF16), _sds((H, S, C_ROPE), F32)],
        compiler_params=_params("parallel", "arbitrary"),
    )(qi, kj, qn, qr, kv, kv, kr, do, lse, delta)


def _mla_flash_dkv(qn, qr, kv, kr, do, lse, delta, *, name):
    S = qn.shape[0]
    H, G, t = C_HEADS, MLA_GROUP, min(FLASH_T, S)
    n = S // t
    qi, kj = _causal_pairs(n, False)

    def body(qi_ref, kj_ref, qn_ref, qr_ref, kn_ref, v_ref, kr_ref, do_ref, lse_ref, dl_ref, dkn_ref, dv_ref, dkr_ref, ak, av):
        p_id = pl.program_id(1)
        i, j = qi_ref[p_id], kj_ref[p_id]
        sls = [slice(hh * LANES, (hh + 1) * LANES) for hh in range(G)]

        def step(diagonal):
            def products(hh):
                qc = _mla_cat(qn_ref[:, sls[hh]], qr_ref[hh])
                st = lax.dot_general(_mla_cat(kn_ref[:, sls[hh]], kr_ref[...]), qc, C11, preferred_element_type=F32)
                st = st * (MLA_SCALE * LOG2E)
                if diagonal:
                    key = lax.broadcasted_iota(jnp.int32, st.shape, 0)
                    qry = lax.broadcasted_iota(jnp.int32, st.shape, 1)
                    st = jnp.where(key <= qry, st, NEG)
                return qc, st, lax.dot_general(v_ref[:, sls[hh]], do_ref[:, sls[hh]], C11, preferred_element_type=F32)

            nxt = products(0)
            for hh in range(G):
                qc, st, dpt = nxt
                if hh + 1 < G:
                    nxt = products(hh + 1)
                sl = sls[hh]
                lse_row = jnp.transpose(lse_ref[:, sl])[0:1, :]
                dl_row = jnp.transpose(dl_ref[:, sl])[0:1, :]
                pt = jnp.exp2(st - lse_row)
                dst = (pt * (dpt - dl_row)).astype(BF16)
                av[:, sl] += jnp.dot(pt.astype(BF16), do_ref[:, sl], preferred_element_type=F32)
                ak[hh] += jnp.dot(dst, qc, preferred_element_type=F32)

        @pl.when(i == j)
        def _():
            ak[...] = jnp.zeros_like(ak)
            av[...] = jnp.zeros_like(av)
            step(True)

        @pl.when(i > j)
        def _():
            step(False)

        @pl.when(i == n - 1)
        def _():
            dv_ref[...] = av[...].astype(BF16)
            for hh in range(G):
                dkn_ref[:, sls[hh]] = (ak[hh, :, :C_NOPE] * MLA_SCALE).astype(BF16)
                dkr_ref[hh] = ak[hh, :, C_NOPE:] * MLA_SCALE

    wide = lambda which, off: pl.BlockSpec((t, G * LANES), lambda h, p, qi, kj: ((qi if which == "q" else kj)[p], off + h))
    krb = pl.BlockSpec((G, t, C_ROPE), lambda h, p, qi, kj: (h, kj[p], 0))
    return pl.pallas_call(
        body, name=name,
        grid_spec=pltpu.PrefetchScalarGridSpec(
            num_scalar_prefetch=2, grid=(H // G, qi.shape[0]),
            in_specs=[wide("q", 0), pl.BlockSpec((G, t, C_ROPE), lambda h, p, qi, kj: (h, qi[p], 0)), wide("k", 0),
                      wide("k", H // G), pl.BlockSpec((t, C_ROPE), lambda h, p, qi, kj: (kj[p], 0)),
                      wide("q", 0), wide("q", 0), wide("q", 0)],
            out_specs=[wide("k", 0), wide("k", 0), krb],
            scratch_shapes=[pltpu.VMEM((G, t, C_NOPE + C_ROPE), F32), pltpu.VMEM((t, G * LANES), F32)]),
        out_shape=[_sds((S, H * C_NOPE), BF16), _sds((S, H * C_V), BF16), _sds((H, S, C_ROPE), F32)],
        compiler_params=_params("parallel", "arbitrary"),
    )(qi, kj, qn, qr, kv, kv, kr, do, lse, delta)


def _place():
    return lax.axis_index("x"), lax.axis_index("y"), lax.axis_index("c")


def _other_chips(x, y):
    return [(1 - x, y), (x, 1 - y), (1 - x, 1 - y)]


def _all_gather_chips(p, *, name):
    R, C = p.shape

    def body(p_ref, o_ref, send_sems, recv_sems, local_sem):
        x, y, c = _place()
        me = 2 * x + y
        local = pltpu.make_async_copy(p_ref, o_ref.at[me], local_sem)
        local.start()
        copies = [pltpu.make_async_remote_copy(src_ref=p_ref, dst_ref=o_ref.at[me], send_sem=send_sems.at[k],
                                               recv_sem=recv_sems.at[k], device_id=(px, py, c), device_id_type=MESH)
                  for k, (px, py) in enumerate(_other_chips(x, y))]
        for cp in copies:
            cp.start()
        for cp in copies:
            cp.wait()
        local.wait()

    any_spec = pl.BlockSpec(memory_space=pl.ANY)
    return pl.pallas_call(
        body, name=name, in_specs=[any_spec], out_specs=any_spec, out_shape=_sds((4, R, C), p.dtype),
        scratch_shapes=[pltpu.SemaphoreType.DMA((3,)), pltpu.SemaphoreType.DMA((3,)), pltpu.SemaphoreType.DMA(())],
    )(p)


def _shard_of(ref, axis, pos, size):
    idx = [slice(None)] * len(ref.shape)
    idx[axis] = pl.ds(pos * size, size)
    return ref.at[tuple(idx)]


def _shard_pos(chip, swapped):
    return (chip % 2) * 2 + chip // 2 if swapped else chip


class _Comm:
    def __init__(self, inputs, out_shapes, sems, start, finish, deliver):
        self.inputs, self.out_shapes, self.sems = list(inputs), list(out_shapes), list(sems)
        self.start, self.finish, self.deliver = start, finish, deliver


def _call(body, *, name, grid, in_specs, out_specs, out_shape, scratch_shapes=(), params, comm=None):
    in_specs, out_specs, out_shape, scratch_shapes = list(in_specs), list(out_specs), list(out_shape), list(scratch_shapes)
    if comm is None:
        return pl.pallas_call(body, name=name, grid=grid, in_specs=in_specs, out_specs=out_specs, out_shape=out_shape,
                              scratch_shapes=scratch_shapes, compiler_params=params)
    n_in, n_out, n_scr = len(in_specs), len(out_specs), len(scratch_shapes)
    c_in, c_out = len(comm.inputs), len(comm.out_shapes)

    def hosted(*refs):
        a, rest = refs[:n_in], refs[n_in:]
        cin, rest = rest[:c_in], rest[c_in:]
        o, rest = rest[:n_out], rest[n_out:]
        cout, rest = rest[:c_out], rest[c_out:]
        scr, sems = rest[:n_scr], rest[n_scr:]
        first = functools.reduce(jnp.logical_and, [pl.program_id(d) == 0 for d in range(len(grid))])
        last = functools.reduce(jnp.logical_and, [pl.program_id(d) == grid[d] - 1 for d in range(len(grid))])

        @pl.when(first)
        def _():
            comm.start(cin, cout, sems)

        body(*a, *o, *scr)

        @pl.when(last)
        def _():
            comm.finish(cin, cout, sems)

    any_spec = pl.BlockSpec(memory_space=pl.ANY)
    call = pl.pallas_call(
        hosted, name=name, grid=grid, in_specs=in_specs + [any_spec] * c_in, out_specs=out_specs + [any_spec] * c_out,
        out_shape=out_shape + comm.out_shapes, scratch_shapes=scratch_shapes + comm.sems, compiler_params=params)

    def run(*args):
        outs = call(*args, *comm.inputs)
        comm.deliver(outs[n_out:])
        return outs[:n_out]

    return run


def _run_comm(comm, *, name):
    c_in, c_out = len(comm.inputs), len(comm.out_shapes)

    def body(*refs):
        cin, cout, sems = refs[:c_in], refs[c_in:c_in + c_out], refs[c_in + c_out:]
        comm.start(cin, cout, sems)
        comm.finish(cin, cout, sems)

    any_spec = pl.BlockSpec(memory_space=pl.ANY)
    outs = pl.pallas_call(body, name=name, in_specs=[any_spec] * c_in, out_specs=[any_spec] * c_out,
                          out_shape=comm.out_shapes, scratch_shapes=comm.sems)(*comm.inputs)
    comm.deliver(outs)


def _gather_comm(items, deliver):
    n = len(items)
    shard_shapes = [a.shape if j is None else a.shape[1:] for a, j, _, _ in items]
    axes = [ax for _, _, ax, _ in items]
    swapped = [sw for _, _, _, sw in items]
    sizes = [s[a] for s, a in zip(shard_shapes, axes)]
    halves = [s[-2] // 2 for s in shard_shapes]
    full = [tuple(4 * d if i == a else d for i, d in enumerate(s)) for s, a in zip(shard_shapes, axes)]

    def mine(ins, k):
        j = items[k][1]
        return ins[k] if j is None else ins[k].at[j]

    def half_of(ref, k, half, chip=None):
        nd = len(ref.shape)
        split = nd - 2
        idx = [slice(None)] * nd
        start = half * halves[k]
        if chip is not None:
            pos = _shard_pos(chip, swapped[k]) * sizes[k]
            if axes[k] == split:
                start = start + pos
            else:
                idx[axes[k]] = pl.ds(pos, sizes[k])
        idx[split] = pl.ds(start, halves[k])
        return ref.at[tuple(idx)]

    def local_copy(ins, outs, sems, k, me):
        return pltpu.make_async_copy(mine(ins, k), _shard_of(outs[k], axes[k], _shard_pos(me, swapped[k]), sizes[k]), sems[4].at[k])

    def ici_copy(ins, outs, sems, k, j, peer, c, landing_chip):
        return pltpu.make_async_remote_copy(
            src_ref=half_of(mine(ins, k), k, c), dst_ref=half_of(outs[k], k, c, chip=landing_chip), send_sem=sems[0].at[3 * k + j],
            recv_sem=sems[1].at[3 * k + j], device_id=(peer[0], peer[1], c), device_id_type=MESH)

    def pass_copy(outs, sems, k, j, half, chip, sibling):
        region = half_of(outs[k], k, half, chip=chip)
        return pltpu.make_async_remote_copy(src_ref=region, dst_ref=region, send_sem=sems[2].at[3 * k + j],
                                            recv_sem=sems[3].at[3 * k + j], device_id=sibling, device_id_type=MESH)

    def start(ins, outs, sems):
        x, y, c = _place()
        me = 2 * x + y
        for k in range(n):
            local_copy(ins, outs, sems, k, me).start()
            for j, peer in enumerate(_other_chips(x, y)):
                ici_copy(ins, outs, sems, k, j, peer, c, me).start()

    def finish(ins, outs, sems):
        x, y, c = _place()
        me = 2 * x + y
        chips = _other_chips(x, y)
        sibling = (x, y, 1 - c)
        for k in range(n):
            for j, peer in enumerate(chips):
                ici_copy(ins, outs, sems, k, j, peer, c, 2 * peer[0] + peer[1]).wait_recv()
                pass_copy(outs, sems, k, j, c, 2 * peer[0] + peer[1], sibling).start()
        for k in range(n):
            for j, peer in enumerate(chips):
                pass_copy(outs, sems, k, j, 1 - c, 2 * peer[0] + peer[1], sibling).wait_recv()
        for k in range(n):
            local_copy(ins, outs, sems, k, me).wait()
            for j, peer in enumerate(chips):
                ici_copy(ins, outs, sems, k, j, peer, c, me).wait_send()
                pass_copy(outs, sems, k, j, c, 2 * peer[0] + peer[1], sibling).wait_send()

    return _Comm([a for a, _, _, _ in items], [_sds(f, a.dtype) for f, (a, _, _, _) in zip(full, items)],
                 [pltpu.SemaphoreType.DMA((3 * n,))] * 4 + [pltpu.SemaphoreType.DMA((n,))], start, finish, deliver)


def _scatter_comm(items, deliver):
    n = len(items)
    axes = [ax for _, ax, _ in items]
    swapped = [sw for _, _, sw in items]
    sizes = [g.shape[a] // 4 for g, a, _ in items]
    shard = [tuple(d // 4 if i == a else d for i, d in enumerate(g.shape)) for g, a, _ in items]

    def copies(ins, outs, sems):
        x, y, c = _place()
        me = 2 * x + y
        out = []
        for k in range(n):
            own = _shard_of(ins[k], axes[k], _shard_pos(me, swapped[k]), sizes[k])
            out.append(pltpu.make_async_copy(own, outs[k].at[3], sems[2].at[k]))
            for j, (px, py) in enumerate(_other_chips(x, y)):
                src = _shard_of(ins[k], axes[k], _shard_pos(2 * px + py, swapped[k]), sizes[k])
                out.append(pltpu.make_async_remote_copy(src_ref=src, dst_ref=outs[k].at[j], send_sem=sems[0].at[3 * k + j],
                                                        recv_sem=sems[1].at[3 * k + j], device_id=(px, py, c), device_id_type=MESH))
        return out

    def start(ins, outs, sems):
        for cp in copies(ins, outs, sems):
            cp.start()

    def finish(ins, outs, sems):
        for cp in copies(ins, outs, sems):
            cp.wait()

    return _Comm([g for g, _, _ in items], [_sds((4,) + s, g.dtype) for s, (g, _, _) in zip(shard, items)],
                 [pltpu.SemaphoreType.DMA((3 * n,)), pltpu.SemaphoreType.DMA((3 * n,)), pltpu.SemaphoreType.DMA((n,))],
                 start, finish, deliver)


def _row_tile(rows, cols, budget=2 ** 20):
    best = None
    for t in range(8, rows + 1, 8):
        if rows % t == 0 and t * cols * 4 <= budget:
            best = t
    return best or rows


def _sum_partials(recv, into, layer, layers, *, name):
    _, R, C = recv.shape
    tr = _row_tile(R, C)
    nt = R // tr

    def body(own_ref, r0_ref, r1_ref, r2_ref, *rest):
        f = lambda ref: ref[...].astype(F32)
        rest[-1][...] = ((f(own_ref) + f(r0_ref)) + f(r1_ref)) + f(r2_ref)

    rspec = lambda k: pl.BlockSpec((None, tr, C), lambda i: (k, i, 0))
    extra = [] if into is None else [pl.BlockSpec(memory_space=pl.ANY)]
    return pl.pallas_call(
        body, name=name, grid=(nt,), in_specs=[rspec(3), rspec(0), rspec(1), rspec(2)] + extra,
        out_specs=pl.BlockSpec((tr, C), lambda i: (layer * nt + i, 0)), out_shape=_sds((layers * R, C), F32),
        input_output_aliases={} if into is None else {4: 0}, compiler_params=_params("parallel"),
    )(recv, recv, recv, recv, *([] if into is None else [into]))


def _swap_cores(parts, *, name):
    n = len(parts)

    def body(*refs):
        ins, outs = refs[:n], refs[n:2 * n]
        send_sems, recv_sems = refs[2 * n:]
        x, y, c = _place()
        copies = [pltpu.make_async_remote_copy(src_ref=ins[k], dst_ref=outs[k], send_sem=send_sems.at[k], recv_sem=recv_sems.at[k],
                                               device_id=(x, y, 1 - c), device_id_type=MESH) for k in range(n)]
        for cp in copies:
            cp.start()
        for cp in copies:
            cp.wait()

    any_spec = pl.BlockSpec(memory_space=pl.ANY)
    return pl.pallas_call(
        body, name=name, in_specs=[any_spec] * n, out_specs=[any_spec] * n, out_shape=[_sds(p.shape, p.dtype) for p in parts],
        scratch_shapes=[pltpu.SemaphoreType.DMA((n,)), pltpu.SemaphoreType.DMA((n,))],
    )(*parts)


def _all_reduce_small(v, *, name):
    r, C = v.shape

    def body(v_ref, o_ref, buf, send_sems, recv_sems):
        x, y, c = _place()
        me = 4 * x + 2 * y + c
        buf[me] = v_ref[...]
        peers = []
        for k in range(1, 8):
            kx, ky, kc = (k >> 2) & 1, (k >> 1) & 1, k & 1
            px = 1 - x if kx else x
            py = 1 - y if ky else y
            pc = 1 - c if kc else c
            peers.append((px, py, pc))
        copies = []
        for k, peer in enumerate(peers):
            cp = pltpu.make_async_remote_copy(src_ref=v_ref, dst_ref=buf.at[me], send_sem=send_sems.at[k],
                                              recv_sem=recv_sems.at[me], device_id=peer, device_id_type=MESH)
            cp.start()
            copies.append(cp)
        for k, (px, py, pc) in enumerate(peers):
            src = 4 * px + 2 * py + pc
            pltpu.make_async_remote_copy(src_ref=v_ref, dst_ref=buf.at[src], send_sem=send_sems.at[k],
                                         recv_sem=recv_sems.at[src], device_id=peers[k], device_id_type=MESH).wait_recv()
        for cp in copies:
            cp.wait_send()
        acc = buf[0]
        for d in range(1, 8):
            acc = acc + buf[d]
        o_ref[...] = acc

    vm = pl.BlockSpec(memory_space=pltpu.VMEM)
    return pl.pallas_call(
        body, name=name, in_specs=[vm], out_specs=vm, out_shape=_sds((r, C), F32),
        scratch_shapes=[pltpu.VMEM((8, r, C), F32), pltpu.SemaphoreType.DMA((7,)), pltpu.SemaphoreType.DMA((8,))],
    )(v)


def _adamw(w, m, v, ga, gb, *, name):
    R, C = w.shape
    tr = _row_tile(R, C)
    has_b = gb is not None
    c1 = 1.0 / (1.0 - ADAM_B1 ** ADAM_STEP)
    c2 = 1.0 / (1.0 - ADAM_B2 ** ADAM_STEP)

    def body(*refs):
        if has_b:
            w_ref, m_ref, v_ref, ga_ref, gb_ref, g_ref, d_ref, nm_ref, nv_ref = refs
            g = ga_ref[...] + gb_ref[...]
        else:
            w_ref, m_ref, v_ref, ga_ref, g_ref, d_ref, nm_ref, nv_ref = refs
            g = ga_ref[...]
        nm = ADAM_B1 * m_ref[...] + (1.0 - ADAM_B1) * g
        nv = ADAM_B2 * v_ref[...] + (1.0 - ADAM_B2) * (g * g)
        g_ref[...] = g
        nm_ref[...] = nm
        nv_ref[...] = nv
        d_ref[...] = -ADAM_LR * ((nm * c1) / (jnp.sqrt(nv * c2) + ADAM_EPS) + ADAM_WD * w_ref[...])

    blk = pl.BlockSpec((tr, C), lambda i: (i, 0))
    n_in = 5 if has_b else 4
    args = (w, m, v, ga) + ((gb,) if has_b else ())
    return pl.pallas_call(body, name=name, grid=(R // tr,), in_specs=[blk] * n_in, out_specs=[blk] * 4,
                          out_shape=[_sds((R, C), F32)] * 4, compiler_params=_params("parallel"))(*args)


def _seg_rows(n, cols):
    return -(-n // (16 * cols)) * 16


def _pack(arrays, dtype, cols=PACK_COLS, row_mult=512):
    parts, rows = [], 0
    for a in arrays:
        n = int(np.prod(a.shape))
        r = _seg_rows(n, cols)
        flat = a.reshape(-1).astype(dtype)
        if r * cols != n:
            flat = jnp.pad(flat, (0, r * cols - n))
        parts.append(flat.reshape(r, cols))
        rows += r
    pad = -rows % row_mult
    if pad:
        parts.append(jnp.zeros((pad, cols), dtype))
    return jnp.concatenate(parts, axis=0)


def _unpack(packed, shapes, cols=PACK_COLS):
    out, r0 = [], 0
    for shp in shapes:
        n = int(np.prod(shp))
        used = -(-n // cols)
        out.append(packed[r0:r0 + used].reshape(-1)[:n].reshape(shp))
        r0 += _seg_rows(n, cols)
    return out


def _rope_tables(seq):
    inv = 1.0 / (ROPE_THETA ** (jnp.arange(0, 64, 2, dtype=F32) / 64))
    ang = jnp.arange(seq, dtype=F32)[:, None] * inv[None, :]
    cos, sin = jnp.cos(ang), jnp.sin(ang)
    cos128 = jnp.concatenate([cos, cos, cos, cos], axis=1)
    sin128 = jnp.concatenate([-sin, sin, -sin, sin], axis=1)
    return cos128, sin128


def _ffn_perm(a):
    lead = a.shape[:-1]
    nj = D_FF // FFN_TC
    return jnp.swapaxes(a.reshape(lead + (2, nj, FFN_TC)), -3, -2).reshape(lead + (2 * D_FF,))


def _ffn_unperm(a):
    lead = a.shape[:-1]
    nj = D_FF // FFN_TC
    return jnp.swapaxes(a.reshape(lead + (nj, 2, FFN_TC)), -3, -2).reshape(lead + (2 * D_FF,))


def _mixer_a_fwd(xb, w, j, cos, sin, tag, gathers):
    qkv = _mm(xb, w["a_w_qkv"][j], name=f"mm_qkv_{tag}", comm=gathers.get(f"mm_qkv_{tag}"))
    qkv_r = _rope_cols(qkv, cos, sin, (A_HEADS + A_KV_HEADS) * A_HEAD_DIM, name=f"rope_qkv_{tag}",
                       comm=gathers.get(f"rope_qkv_{tag}"))
    o = _swa_fwd(qkv_r, w["a_sinks"][j], name=f"swa_fwd_{tag}", comm=gathers.get(f"swa_fwd_{tag}"))
    return (o, w["a_w_o"][j]), (xb, qkv_r, o)


def _mixer_a_bwd(dzb, res, w, j, cos, sin, tag, grads, make_comm):
    xb, qkv_r, o = res
    do = _mm(dzb, w["a_w_o"][j], tb=True, out_dtype=BF16, name=f"mm_dao_{tag}")
    grads["a_w_o"][j] = _mm(o, dzb, ta=True, out_dtype=BF16, tk=GRAD_TK, name=f"mm_gao_{tag}")
    dq, dcur, dprev, dsink = _swa_bwd(qkv_r, w["a_sinks"][j], do, cos, sin, name=f"swa_bwd_{tag}", comm=make_comm())
    grads["a_sinks"][j] = dsink[0, :A_HEADS]
    dqkv = _swa_dqkv(dq, dcur, dprev, cos, sin, name=f"swa_dqkv_{tag}")
    grads["a_w_qkv"][j] = _mm(xb, dqkv, ta=True, out_dtype=BF16, tk=GRAD_TK, name=f"mm_gqkv_{tag}")
    return dqkv, w["a_w_qkv"][j]


def _mixer_b_fwd(xb, w, j, tag, gathers):
    xw = _mm(xb, w["b_w_in"][j], name=f"mm_bin_{tag}")
    wri = jnp.concatenate([w["b_w_rgate"][j], w["b_w_igate"][j]], axis=-1)
    y, u, h = _lru_fwd(xw, w["b_conv_w"][j], w["b_conv_b"][j][None], wri, w["b_b_rgate"][j][None],
                       w["b_b_igate"][j][None], w["b_lambda"][j][None], name=f"lru_fwd_{tag}", comm=gathers.get(f"lru_fwd_{tag}"))
    return (y, w["b_w_o"][j]), (xb, xw, wri, u, h, y)


def _mixer_b_bwd(dzb, res, w, j, tag, grads):
    xb, xw, wri, u, h, y = res
    dy = _mm(dzb, w["b_w_o"][j], tb=True, out_dtype=BF16, name=f"mm_dbo_{tag}")
    grads["b_w_o"][j] = _mm(y, dzb, ta=True, out_dtype=BF16, tk=GRAD_TK, name=f"mm_gbo_{tag}")
    dxw, dcw, dcb, dwri, dbr, dbi, dlam = _lru_bwd(
        xw, u, h, dy, w["b_conv_w"][j], wri, w["b_b_rgate"][j][None], w["b_b_igate"][j][None], w["b_lambda"][j][None],
        name=f"lru_bwd_{tag}")
    grads["b_conv_w"][j], grads["b_conv_b"][j] = dcw, dcb[0]
    grads["b_w_rgate"][j], grads["b_w_igate"][j] = dwri[..., :LRU_BLOCK_W].astype(BF16), dwri[..., LRU_BLOCK_W:].astype(BF16)
    grads["b_b_rgate"][j], grads["b_b_igate"][j], grads["b_lambda"][j] = dbr[0], dbi[0], dlam[0]
    grads["b_w_in"][j] = _mm(xb, dxw, ta=True, out_dtype=BF16, tk=GRAD_TK, name=f"mm_gbin_{tag}")
    return dxw, w["b_w_in"][j]


def _mla_weights(w, j):
    H = C_HEADS
    uq = w["c_w_uq"][j].reshape(C_Q_RANK, H, C_NOPE + C_ROPE)
    ukv = w["c_w_ukv"][j].reshape(C_KV_RANK, H, C_NOPE + C_V)
    uq_n = uq[:, :, :C_NOPE].reshape(C_Q_RANK, H * C_NOPE)
    uq_r = uq[:, :, C_NOPE:].reshape(C_Q_RANK, H * C_ROPE)
    ukv_p = jnp.concatenate([ukv[:, :, :C_NOPE].reshape(C_KV_RANK, H * C_NOPE),
                             ukv[:, :, C_NOPE:].reshape(C_KV_RANK, H * C_V)], axis=1)
    return uq_n, uq_r, ukv_p


def _mixer_c_fwd(xb, w, j, cos, sin, tag):
    S = xb.shape[0]
    H = C_HEADS
    uq_n, uq_r, ukv_p = _mla_weights(w, j)
    c = _mm(xb, w["c_w_down"][j], name=f"mm_cdown_{tag}")
    cq, ckv, kr = _mla_pre(c, w["c_q_norm"][j][None], w["c_kv_norm"][j][None], cos, sin, name=f"mla_pre_{tag}")
    qn = _mm(cq, uq_n, out_dtype=BF16, name=f"mm_uqn_{tag}")
    qr_flat = _rope_heads(_mm(cq, uq_r, name=f"mm_uqr_{tag}"), cos, sin, transpose=False, name=f"rope_qr_{tag}")
    qr = jnp.transpose(qr_flat.reshape(S, H, C_ROPE), (1, 0, 2))
    kv = _mm(ckv, ukv_p, out_dtype=BF16, name=f"mm_ukv_{tag}")
    o, lse = _mla_flash_fwd(qn, qr, kv, kr, name=f"mla_fwd_{tag}")
    return (o, w["c_w_o"][j]), (xb, c, cq, ckv, kr, qn, qr, kv, o, lse, uq_n, uq_r, ukv_p)


def _mixer_c_bwd(dzb, res, w, j, cos, sin, tag, grads):
    xb, c, cq, ckv, kr, qn, qr, kv, o, lse, uq_n, uq_r, ukv_p = res
    S = xb.shape[0]
    H = C_HEADS
    do = _mm(dzb, w["c_w_o"][j], tb=True, out_dtype=BF16, name=f"mm_dco_{tag}")
    grads["c_w_o"][j] = _mm(o, dzb, ta=True, out_dtype=BF16, tk=GRAD_TK, name=f"mm_gco_{tag}")
    delta = _mla_delta(do, o, name=f"mla_delta_{tag}")
    dqn, dqr = _mla_flash_dq(qn, qr, kv, kr, do, lse, delta, name=f"mla_dq_{tag}")
    dkn, dv, dkr_h = _mla_flash_dkv(qn, qr, kv, kr, do, lse, delta, name=f"mla_dkv_{tag}")
    dkv = jnp.concatenate([dkn, dv], axis=1)
    dqr_flat = _rope_heads(jnp.transpose(dqr, (1, 0, 2)).reshape(S, H * C_ROPE), cos, sin, transpose=True, name=f"rope_dqr_{tag}")
    g_uq_n = _mm(cq, dqn, ta=True, out_dtype=BF16, tk=GRAD_TK, name=f"mm_guqn_{tag}")
    g_uq_r = _mm(cq, dqr_flat, ta=True, out_dtype=BF16, tk=GRAD_TK, name=f"mm_guqr_{tag}")
    g_ukv = _mm(ckv, dkv, ta=True, out_dtype=BF16, tk=GRAD_TK, name=f"mm_gukv_{tag}")
    grads["c_w_uq"][j] = jnp.concatenate([g_uq_n.reshape(C_Q_RANK, H, C_NOPE), g_uq_r.reshape(C_Q_RANK, H, C_ROPE)],
                                         axis=2).reshape(C_Q_RANK, H * (C_NOPE + C_ROPE))
    grads["c_w_ukv"][j] = jnp.concatenate([g_ukv[:, :H * C_NOPE].reshape(C_KV_RANK, H, C_NOPE),
                                           g_ukv[:, H * C_NOPE:].reshape(C_KV_RANK, H, C_V)],
                                          axis=2).reshape(C_KV_RANK, H * (C_NOPE + C_V))
    dcq_a = _mm(dqn, uq_n, tb=True, name=f"mm_dcqa_{tag}")
    dcq_b = _mm(dqr_flat, uq_r, tb=True, name=f"mm_dcqb_{tag}")
    dckv = _mm(dkv, ukv_p, tb=True, name=f"mm_dckv_{tag}")
    dc, dqg, dkvg = _mla_post_bwd(c, dcq_a, dcq_b, dckv, dkr_h, w["c_q_norm"][j][None], w["c_kv_norm"][j][None], cos, sin,
                                  name=f"mla_post_{tag}")
    grads["c_q_norm"][j], grads["c_kv_norm"][j] = dqg[0], dkvg[0]
    grads["c_w_down"][j] = _mm(xb, dc, ta=True, out_dtype=BF16, tk=GRAD_TK, name=f"mm_gcdown_{tag}")
    return dc, w["c_w_down"][j]


def _layer_big(i, mixer=True, rest=True):
    kind, j = i % N_MIXERS, i // N_MIXERS
    own = [[("a_w_qkv", j), ("a_w_o", j)], [("b_w_in", j), ("b_w_rgate", j), ("b_w_igate", j), ("b_w_o", j)],
           [("c_w_down", j), ("c_w_uq", j), ("c_w_ukv", j), ("c_w_o", j)]][kind]
    return (own if mixer else []) + ([("x_w_q", i), ("x_w_o", i), ("f_w_up", i), ("f_w_down", i)] if rest else [])


def _local_step(x, mem, target, w, n_layers, gathers, scatter):
    S = x.shape[0]
    cos, sin = _rope_tables(S)
    grads = {n: [None] * n_layers[n] for n in WEIGHTS if n != "mem_w_kv"}

    xs, xb = x, x.astype(BF16)
    saved = []
    for i in range(DEPTH):
        kind, j = i % N_MIXERS, i // N_MIXERS
        tag = f"l{i}"
        if kind == 0:
            (act, w_out), res = _mixer_a_fwd(xb, w, j, cos, sin, tag, gathers)
        elif kind == 1:
            (act, w_out), res = _mixer_b_fwd(xb, w, j, tag, gathers)
        else:
            (act, w_out), res = _mixer_c_fwd(xb, w, j, cos, sin, tag)
        x1, x1b, xh1, rs1 = _mm_ln_fwd(act, w_out, xs, w["ln_g"][i, 0][None], w["ln_b"][i, 0][None], name=f"ln1_{tag}",
                                       comm=gathers.get(f"ln1_{tag}"))
        if i == 0:
            mkv = _mm(mem, w["mem_w_kv"], out_dtype=BF16, tm=MEM_LEN, name="mm_memkv")
        q = _mm(x1b, w["x_w_q"][i], out_dtype=BF16, name=f"mm_xq_{tag}")
        o = _xattn_fwd(q, mkv, name=f"xattn_fwd_{tag}")
        x2, x2b, xh2, rs2 = _mm_ln_fwd(o, w["x_w_o"][i], x1, w["ln_g"][i, 1][None], w["ln_b"][i, 1][None], name=f"ln2_{tag}")
        w_up = w["f_w_up"][i]
        cwp, cbp = _ffn_perm(w["f_conv_w"][i]), _ffn_perm(w["f_conv_b"][i][None])
        hh = _mm(x2b, w_up, out_dtype=BF16, tm=2 * MM_T, tn=FFN_TC, name=f"mm_up_{tag}", comm=gathers.get(f"mm_up_{tag}"))
        a, hc = _ffn_act_fwd(hh, cwp, cbp, name=f"ffn_act_{tag}", comm=gathers.get(f"ffn_act_{tag}"))
        x3, x3b, xh3, rs3 = _mm_ln_fwd(a, w["f_w_down"][i], x2, w["ln_g"][i, 2][None], w["ln_b"][i, 2][None],
                                       name=f"ln3_{tag}", comm=gathers.get(f"ln3_{tag}"))
        saved.append((res, (xh1, rs1, x1b), (q, o, xh2, rs2, x2b), (w_up, cwp, hc, hh, a, xh3, rs3)))
        xs, xb = x3, x3b

    dloss, loss = _loss_fwd(xs, target, name="loss")

    dmkv = None
    ln_dg = [[None] * 3 for _ in range(DEPTH)]
    ln_db = [[None] * 3 for _ in range(DEPTH)]
    for i in reversed(range(DEPTH)):
        kind, j = i % N_MIXERS, i // N_MIXERS
        tag = f"l{i}"
        res, (xh1, rs1, x1b), (q, o, xh2, rs2, x2b), (w_up, cwp, hc, hh, a, xh3, rs3) = saved[i]
        if i == DEPTH - 1:
            dz3, dz3b, ln_dg[i][2], ln_db[i][2] = _ln_bwd(None, dloss, xh3, rs3, w["ln_g"][i, 2][None], name=f"ln3_bwd_{tag}")
        else:
            dz3, dz3b, ln_dg[i][2], ln_db[i][2] = _mm_ln_bwd(dz1, d_in, w_in, xh3, rs3, w["ln_g"][i, 2][None], name=f"ln3_bwd_{tag}")
        da = _mm(dz3b, w["f_w_down"][i], tb=True, out_dtype=BF16, name=f"mm_ddown_{tag}")
        grads["f_w_down"][i] = _mm(a, dz3b, ta=True, out_dtype=BF16, tm=FFN_TC, tk=GRAD_TK, name=f"mm_gdown_{tag}")
        later = scatter(_layer_big(i + 1), grads) if i + 1 < DEPTH else None
        dh, dcw, dcb = _ffn_act_bwd(hh, hc, da, cwp, name=f"ffn_act_bwd_{tag}", comm=later)
        grads["f_conv_w"][i], grads["f_conv_b"][i] = _ffn_unperm(dcw), _ffn_unperm(dcb)[0]
        grads["f_w_up"][i] = _mm(x2b, dh, ta=True, out_dtype=BF16, tn=FFN_TC, tk=GRAD_TK, name=f"mm_gup_{tag}")

        dz2, dz2b, ln_dg[i][1], ln_db[i][1] = _mm_ln_bwd(dz3, dh, w_up, xh2, rs2, w["ln_g"][i, 1][None], name=f"ln2_bwd_{tag}")
        do = _mm(dz2b, w["x_w_o"][i], tb=True, out_dtype=BF16, name=f"mm_dxo_{tag}")
        grads["x_w_o"][i] = _mm(o, dz2b, ta=True, out_dtype=BF16, tk=GRAD_TK, name=f"mm_gxo_{tag}")
        dq, dmkv_i = _xattn_bwd(q, mkv, do, name=f"xattn_bwd_{tag}")
        dmkv = dmkv_i if dmkv is None else dmkv + dmkv_i
        if i == 0:
            g_mem = _mm(mem, dmkv, ta=True, out_dtype=BF16, tm=512, name="mm_gmemkv")
        grads["x_w_q"][i] = _mm(x1b, dq, ta=True, out_dtype=BF16, tk=GRAD_TK, name=f"mm_gxq_{tag}")

        dz1, dz1b, ln_dg[i][0], ln_db[i][0] = _mm_ln_bwd(dz2, dq, w["x_w_q"][i], xh1, rs1, w["ln_g"][i, 0][None], name=f"ln1_bwd_{tag}")
        if kind == 0:
            if i == 0:
                grads["mem_w_kv"] = g_mem
                done = lambda: scatter(_layer_big(0, mixer=False) + [("a_w_o", 0), ("mem_w_kv", None)], grads)
            else:
                done = lambda: None
            d_in, w_in = _mixer_a_bwd(dz1b, res, w, j, cos, sin, tag, grads, done)
        elif kind == 1:
            d_in, w_in = _mixer_b_bwd(dz1b, res, w, j, tag, grads)
        else:
            d_in, w_in = _mixer_c_bwd(dz1b, res, w, j, cos, sin, tag, grads)

    grad_x = _axpy(dz1, _mm(d_in, w_in, tb=True, name="mm_dx_l0"), name="grad_x")
    big = [n for n, _ in SHARDED[:N_BIG]]
    out = {n: (g if n in big else jnp.stack(g, axis=0)) for n, g in grads.items() if n not in ("ln_g", "ln_b")}
    out["ln_g"] = jnp.stack([jnp.concatenate(r, axis=0) for r in ln_dg], axis=0)
    out["ln_b"] = jnp.stack([jnp.concatenate(r, axis=0) for r in ln_db], axis=0)
    return loss, grad_x, out


def kernel(x, mem, a_w_qkv, a_sinks, a_w_o, b_w_in, b_conv_w, b_conv_b, b_w_rgate, b_b_rgate, b_w_igate, b_b_igate, b_lambda, b_w_o, c_w_down, c_q_norm, c_kv_norm, c_w_uq, c_w_ukv, c_w_o, mem_w_kv, x_w_q, x_w_o, f_w_up, f_conv_w, f_conv_b, f_w_down, ln_g, ln_b, loss_target, m_a_w_qkv, m_a_sinks, m_a_w_o, m_b_w_in, m_b_conv_w, m_b_conv_b, m_b_w_rgate, m_b_b_rgate, m_b_w_igate, m_b_b_igate, m_b_lambda, m_b_w_o, m_c_w_down, m_c_q_norm, m_c_kv_norm, m_c_w_uq, m_c_w_ukv, m_c_w_o, m_mem_w_kv, m_x_w_q, m_x_w_o, m_f_w_up, m_f_conv_w, m_f_conv_b, m_f_w_down, m_ln_g, m_ln_b, v_a_w_qkv, v_a_sinks, v_a_w_o, v_b_w_in, v_b_conv_w, v_b_conv_b, v_b_w_rgate, v_b_b_rgate, v_b_w_igate, v_b_b_igate, v_b_lambda, v_b_w_o, v_c_w_down, v_c_q_norm, v_c_kv_norm, v_c_w_uq, v_c_w_ukv, v_c_w_o, v_mem_w_kv, v_x_w_q, v_x_w_o, v_f_w_up, v_f_conv_w, v_f_conv_b, v_f_w_down, v_ln_g, v_ln_b):
    loc = locals()
    shard = {n: loc[n] for n in WEIGHTS}
    mom = {n: loc["m_" + n] for n in WEIGHTS}
    var = {n: loc["v_" + n] for n in WEIGHTS}
    names = [n for n, _ in SHARDED]
    axis = dict(SHARDED)
    big, small = names[:N_BIG], names[N_BIG:]

    chip = 2 * lax.axis_index("x") + lax.axis_index("y")
    n_layers = {n: shard[n].shape[0] for n in WEIGHTS if n != "mem_w_kv"}
    layer_axis = lambda n, j: axis[n] - (0 if j is None else 1)

    shard_b = {n: shard[n].astype(BF16) for n in big}
    w = {n: [None] * n_layers[n] for n in big if n != "mem_w_kv"}

    small_pack = _pack([shard[n] for n in small], F32)
    shard_b["small"] = small_pack

    def gather(pairs):
        def deliver(outs):
            for (n, j), o in zip(pairs, outs):
                if j is None:
                    w[n] = o
                else:
                    w[n][j] = o
        return _gather_comm([(shard_b[n], j, 0 if n == "small" else layer_axis(n, j), n == "f_w_up") for n, j in pairs], deliver)

    _run_comm(gather([("a_w_qkv", 0), ("small", None)]), name="gather_first")
    gathers = {
        "mm_qkv_l0": gather([("a_w_o", 0)]),
        "rope_qkv_l0": gather([("x_w_q", 0), ("mem_w_kv", None)]),
        "swa_fwd_l0": gather([("x_w_o", 0), ("f_w_up", 0)]),
        "ln1_l0": gather([("f_w_down", 0)]),
        "mm_up_l0": gather(_layer_big(1, rest=False) + [("x_w_q", 1), ("x_w_o", 1)]),
        "ffn_act_l0": gather([("f_w_up", 1)]),
        "ln3_l0": gather([("f_w_down", 1)]),
        "lru_fwd_l1": gather(_layer_big(2, rest=False) + [("x_w_q", 2), ("x_w_o", 2)]),
        "mm_up_l1": gather([("f_w_up", 2)]),
        "ffn_act_l1": gather([("f_w_down", 2)] + _layer_big(3, rest=False)),
        "mm_up_l2": gather([("f_w_up", 3)]),
        "ffn_act_l2": gather([("f_w_down", 3), ("x_w_q", 3), ("x_w_o", 3)]),
    }
    got = w.pop("small").reshape((4,) + small_pack.shape)
    per_chip = [_unpack(got[s], [shard[n].shape for n in small]) for s in range(4)]
    for k, n in enumerate(small):
        w[n] = jnp.concatenate([per_chip[s][k] for s in range(4)], axis=axis[n])
    for n in REPLICATED:
        w[n] = shard[n]

    recv = {}

    def scatter(pairs, grads):
        def deliver(outs):
            recv.update(dict(zip(pairs, outs)))
        return _scatter_comm([(grads[n] if j is None else grads[n][j], layer_axis(n, j), n == "f_w_up") for n, j in pairs], deliver)

    loss, grad_x, g = _local_step(x[0], mem[0], loss_target[0], w, n_layers, gathers, scatter)
    _run_comm(scatter([("a_w_qkv", 0)], g), name="scatter_last")

    view = {n: (int(np.prod(shard[n].shape[:-1])), shard[n].shape[-1]) for n in big}
    parts = []
    for n in big:
        layers = n_layers.get(n, 1)
        part = None
        for j in range(layers):
            r = recv[(n, j if n in n_layers else None)]
            part = _sum_partials(r.reshape(4, view[n][0] // layers, view[n][1]), part, j, layers, name=f"sum_{n}_{j}")
        parts.append(part)
    sibs = _swap_cores(parts, name="swap_cores")
    grad_o, delta_o, m_o, v_o = {}, {}, {}, {}
    for n, part, sib in zip(big, parts, sibs):
        res = _adamw(shard[n].reshape(view[n]), mom[n].reshape(view[n]), var[n].reshape(view[n]), part, sib, name=f"adamw_{n}")
        for d, r in zip((grad_o, delta_o, m_o, v_o), res):
            d[n] = r.reshape(shard[n].shape)

    rest = small + REPLICATED
    vec = _pack([g[n] for n in rest] + [loss], F32, cols=LANES, row_mult=8)
    tot = _unpack(_all_reduce_small(vec, name="allreduce_small"), [g[n].shape for n in rest] + [(1, 1)], cols=LANES)
    loss_tot = tot[-1].reshape(())
    mine = {n: t for n, t in zip(rest, tot)}
    for n in small:
        size = shard[n].shape[axis[n]]
        mine[n] = lax.dynamic_slice_in_dim(mine[n], chip * size, size, axis=axis[n])
    rpack = lambda d: _pack([d[n] for n in rest], F32, cols=LANES, row_mult=8)
    res = _adamw(rpack(shard), rpack(mom), rpack(var), rpack(mine), None, name="adamw_small")
    for d, r in zip((grad_o, delta_o, m_o, v_o), res):
        d.update(dict(zip(rest, _unpack(r, [shard[n].shape for n in rest], cols=LANES))))

    return (loss_tot, grad_x[None], *[grad_o[n] for n in WEIGHTS], *[delta_o[n] for n in WEIGHTS],
            *[m_o[n] for n in WEIGHTS], *[v_o[n] for n in WEIGHTS])
```
